```python
import math
import jax
import jax.numpy as jnp
from jax import lax
import numpy as np

D_MODEL = 1024
BATCH = 8
SEQ = 2048
DEPTH = 4

D_MIX = 2 * D_MODEL
SSD_WIDTH = D_MIX // 2
ATTN_WIDTH = D_MIX // 4
CM_CHANNELS = D_MIX // 4
SSD_HEAD_DIM = 64
SSD_HEADS = SSD_WIDTH // SSD_HEAD_DIM
SSD_STATE = 128
SSD_GROUPS = 2
SSD_CONV = 4
SSD_CHUNK = 128
SSD_XBC = SSD_WIDTH + 2 * SSD_GROUPS * SSD_STATE
ATTN_HEAD_DIM = 64
ATTN_Q_HEADS = ATTN_WIDTH // ATTN_HEAD_DIM
ATTN_KV_HEADS = 2
WINDOW = 128
ATTN_BLOCK = WINDOW
ROPE_THETA = 10000.0
CM_CONV_WIDTH = 31
D_FF = 4 * D_MODEL
RMS_EPS = 1e-6
LN_EPS = 1e-5
IN_SIZES = (SSD_WIDTH, SSD_XBC, SSD_HEADS,
            ATTN_Q_HEADS * ATTN_HEAD_DIM, ATTN_KV_HEADS * ATTN_HEAD_DIM, ATTN_KV_HEADS * ATTN_HEAD_DIM,
            2 * CM_CHANNELS)
N_IN = sum(IN_SIZES)

kernel_name = "hybrid_ssd_swa_conformer_parallel_heads"


def _split(u, sizes):
    idx, acc = [], 0
    for s in sizes[:-1]:
        acc += s
        idx.append(acc)
    return jnp.split(u, idx, axis=-1)


def rms_norm(x, w, eps=RMS_EPS):
    xf = x.astype(jnp.float32)
    y = xf * lax.rsqrt(jnp.mean(xf * xf, axis=-1, keepdims=True) + eps)
    return (y * w.astype(jnp.float32)).astype(x.dtype)


def layer_norm(x, w, b, eps=LN_EPS):
    xf = x.astype(jnp.float32)
    mu = jnp.mean(xf, axis=-1, keepdims=True)
    var = jnp.mean(jnp.square(xf - mu), axis=-1, keepdims=True)
    y = (xf - mu) * lax.rsqrt(var + eps)
    return (y * w.astype(jnp.float32) + b.astype(jnp.float32)).astype(x.dtype)


def gated_rms_norm(y, z, w):
    g = y.astype(jnp.float32) * jax.nn.silu(z.astype(jnp.float32))
    shp = g.shape
    g = g.reshape(shp[:-1] + (SSD_GROUPS, shp[-1] // SSD_GROUPS))
    g = g * lax.rsqrt(jnp.mean(g * g, axis=-1, keepdims=True) + RMS_EPS)
    return (g.reshape(shp) * w.astype(jnp.float32)).astype(y.dtype)


def causal_depthwise_conv(u, w, b):
    k_w, ch = w.shape
    out = lax.conv_general_dilated(
        u, w[:, None, :].astype(u.dtype), window_strides=(1,), padding=[(k_w - 1, 0)],
        dimension_numbers=("NWC", "WIO", "NWC"), feature_group_count=ch)
    return out + b.astype(u.dtype)


def rope_tables(seq_len, dim):
    inv_freq = ROPE_THETA ** (-jnp.arange(0, dim, 2, dtype=jnp.float32) / dim)
    ang = jnp.arange(seq_len, dtype=jnp.float32)[:, None] * inv_freq[None, :]
    return jnp.cos(ang), jnp.sin(ang)


def apply_rope(x, cos, sin):
    xf = x.astype(jnp.float32)
    x1, x2 = jnp.split(xf, 2, axis=-1)
    c = cos[None, :, None, :]
    s = sin[None, :, None, :]
    return jnp.concatenate([x1 * c - x2 * s, x2 * c + x1 * s], axis=-1).astype(x.dtype)


def ssd_chunked(x, dt, a, bm, cm, d_skip):
    bsz, seq, nh, hp = x.shape
    q = SSD_CHUNK
    nc = seq // q
    r = nh // SSD_GROUPS
    xf = x.astype(jnp.float32).reshape(bsz, nc, q, SSD_GROUPS, r, hp)
    dtc = dt.reshape(bsz, nc, q, SSD_GROUPS, r)
    bf = bm.astype(jnp.float32).reshape(bsz, nc, q, SSD_GROUPS, SSD_STATE)
    cf = cm.astype(jnp.float32).reshape(bsz, nc, q, SSD_GROUPS, SSD_STATE)
    a_dt = jnp.moveaxis(dtc * a.reshape(SSD_GROUPS, r), 2, -1)
    a_cs = jnp.cumsum(a_dt, axis=-1)
    xdt = xf * dtc[..., None]
    causal = jnp.tril(jnp.ones((q, q), dtype=bool))
    seg = a_cs[..., :, None] - a_cs[..., None, :]
    decay_ls = jnp.exp(jnp.where(causal, seg, -jnp.inf))
    cb = jnp.einsum("bclgn,bcsgn->bcgls", cf, bf)
    y_diag = jnp.einsum("bcgrls,bcsgrp->bclgrp", cb[:, :, :, None] * decay_ls, xdt)
    decay_s = jnp.exp(a_cs[..., -1:] - a_cs)
    states = jnp.einsum("bcsgn,bcgrs,bcsgrp->bcgrpn", bf, decay_s, xdt)
    chunk_decay = jnp.exp(a_cs[..., -1])

    def step(h, inp):
        s_c, d_c = inp
        return h * d_c[..., None, None] + s_c, h

    h0 = jnp.zeros((bsz, SSD_GROUPS, r, hp, SSD_STATE), jnp.float32)
    _, prev = lax.scan(step, h0, (jnp.moveaxis(states, 1, 0), jnp.moveaxis(chunk_decay, 1, 0)))
    prev = jnp.moveaxis(prev, 0, 1)
    y_off = jnp.einsum("bclgn,bcgrpn,bcgrl->bclgrp", cf, prev, jnp.exp(a_cs))
    y = y_diag + y_off + xf * d_skip.astype(jnp.float32).reshape(SSD_GROUPS, r)[..., None]
    return y.reshape(bsz, seq, nh * hp).astype(x.dtype)


def sliding_window_gqa(q, k, v, sinks):
    bsz, seq, hq, hd = q.shape
    hkv = k.shape[2]
    r = hq // hkv
    blk = ATTN_BLOCK
    nb = seq // blk
    qb = q.astype(jnp.float32).reshape(bsz, nb, blk, hkv, r, hd)

    def with_prev(t):
        tb = t.reshape(bsz, nb, blk, hkv, hd)
        tp = jnp.pad(tb, ((0, 0), (1, 0), (0, 0), (0, 0), (0, 0)))[:, :-1]
        return jnp.concatenate([tp, tb], axis=2)

    kc = with_prev(k).astype(jnp.float32)
    vc = with_prev(v)
    s = jnp.einsum("bnqhrd,bnkhd->bnhrqk", qb, kc) * (1.0 / math.sqrt(hd))
    qi = jnp.arange(blk)[:, None]
    ki = jnp.arange(2 * blk)[None, :]
    diff = qi + blk - ki
    band = (diff >= 0) & (diff < WINDOW)
    kpos = jnp.arange(nb)[:, None, None] * blk + ki[None] - blk
    mask = band[None] & (kpos >= 0)
    s = jnp.where(mask[None, :, None, None], s, -jnp.inf)
    sink = jnp.broadcast_to(sinks.astype(jnp.float32).reshape(1, 1, hkv, r, 1, 1), s.shape[:-1] + (1,))
    p = jax.nn.softmax(jnp.concatenate([s, sink], axis=-1), axis=-1)[..., :-1]
    o = jnp.einsum("bnhrqk,bnkhd->bnqhrd", p.astype(v.dtype), vc)
    return o.reshape(bsz, seq, hq * hd)


def conformer_conv(u, dw_w, dw_b, ln_w, ln_b):
    a, g = jnp.split(u, 2, axis=-1)
    h = a * jax.nn.sigmoid(g)
    h = causal_depthwise_conv(h, dw_w, dw_b)
    return jax.nn.silu(layer_norm(h, ln_w, ln_b))


def hybrid_layer(x, cos, sin, norm_mix_w, w_in, ssd_conv_w, ssd_conv_b, ssd_dt_bias, ssd_a_log,
                 ssd_d, ssd_norm_w, q_norm_w, k_norm_w, attn_sinks, cm_dw_w, cm_dw_b,
                 cm_ln_w, cm_ln_b, w_out, norm_mlp_w, w_mlp_up, w_mlp_down):
    bsz, seq, _ = x.shape
    h = rms_norm(x, norm_mix_w)
    u = h @ w_in
    z, xbc, dt_raw, q, k, v, glu = _split(u, IN_SIZES)
    xbc = jax.nn.silu(causal_depthwise_conv(xbc, ssd_conv_w, ssd_conv_b))
    xs, bm, cm = _split(xbc, (SSD_WIDTH, SSD_GROUPS * SSD_STATE, SSD_GROUPS * SSD_STATE))
    dt = jax.nn.softplus(dt_raw.astype(jnp.float32) + ssd_dt_bias.astype(jnp.float32))
    a = -jnp.exp(ssd_a_log.astype(jnp.float32))
    y_ssd = ssd_chunked(xs.reshape(bsz, seq, SSD_HEADS, SSD_HEAD_DIM), dt, a,
                        bm.reshape(bsz, seq, SSD_GROUPS, SSD_STATE),
                        cm.reshape(bsz, seq, SSD_GROUPS, SSD_STATE), ssd_d)
    y_ssd = gated_rms_norm(y_ssd, z, ssd_norm_w)
    q = rms_norm(q.reshape(bsz, seq, ATTN_Q_HEADS, ATTN_HEAD_DIM), q_norm_w)
    k = rms_norm(k.reshape(bsz, seq, ATTN_KV_HEADS, ATTN_HEAD_DIM), k_norm_w)
    q = apply_rope(q, cos, sin)
    k = apply_rope(k, cos, sin)
    y_attn = sliding_window_gqa(q, k, v.reshape(bsz, seq, ATTN_KV_HEADS, ATTN_HEAD_DIM), attn_sinks)
    y_conv = conformer_conv(glu, cm_dw_w, cm_dw_b, cm_ln_w, cm_ln_b)
    x = x + jnp.concatenate([y_ssd, y_attn, y_conv], axis=-1) @ w_out
    hm = rms_norm(x, norm_mlp_w)
    x = x + jnp.square(jax.nn.relu(hm @ w_mlp_up)) @ w_mlp_down
    return x


def _fwd_setup_inputs(seed: int = 0) -> dict:
    key = jax.random.key(seed)
    ks = jax.random.split(key, 24)
    f32 = jnp.float32
    nrm = lambda k, shp, scale: jax.random.normal(k, shp, f32) * scale
    dt_init = jnp.exp(jax.random.uniform(ks[5], (DEPTH, SSD_HEADS), f32)
                      * (math.log(0.1) - math.log(0.001)) + math.log(0.001))
    return {
        "x": nrm(ks[0], (BATCH, SEQ, D_MODEL), 1.0),
        "norm_mix_w": 1.0 + nrm(ks[1], (DEPTH, D_MODEL), 0.02),
        "w_in": nrm(ks[2], (DEPTH, D_MODEL, N_IN), D_MODEL ** -0.5),
        "ssd_conv_w": nrm(ks[3], (DEPTH, SSD_CONV, SSD_XBC), SSD_CONV ** -0.5),
        "ssd_conv_b": nrm(ks[4], (DEPTH, SSD_XBC), 0.02),
        "ssd_dt_bias": dt_init + jnp.log(-jnp.expm1(-dt_init)),
        "ssd_a_log": jnp.log(jax.random.uniform(ks[6], (DEPTH, SSD_HEADS), f32, 1.0, 16.0)),
        "ssd_d": 1.0 + nrm(ks[7], (DEPTH, SSD_HEADS), 0.02),
        "ssd_norm_w": 1.0 + nrm(ks[8], (DEPTH, SSD_WIDTH), 0.02),
        "q_norm_w": 1.0 + nrm(ks[9], (DEPTH, ATTN_HEAD_DIM), 0.02),
        "k_norm_w": 1.0 + nrm(ks[10], (DEPTH, ATTN_HEAD_DIM), 0.02),
        "attn_sinks": nrm(ks[11], (DEPTH, ATTN_Q_HEADS), 0.5),
        "cm_dw_w": nrm(ks[12], (DEPTH, CM_CONV_WIDTH, CM_CHANNELS), CM_CONV_WIDTH ** -0.5),
        "cm_dw_b": nrm(ks[13], (DEPTH, CM_CHANNELS), 0.02),
        "cm_ln_w": 1.0 + nrm(ks[14], (DEPTH, CM_CHANNELS), 0.02),
        "cm_ln_b": nrm(ks[15], (DEPTH, CM_CHANNELS), 0.02),
        "w_out": nrm(ks[16], (DEPTH, D_MIX, D_MODEL), D_MIX ** -0.5),
        "norm_mlp_w": 1.0 + nrm(ks[17], (DEPTH, D_MODEL), 0.02),
        "w_mlp_up": nrm(ks[18], (DEPTH, D_MODEL, D_FF), D_MODEL ** -0.5),
        "w_mlp_down": nrm(ks[19], (DEPTH, D_FF, D_MODEL), D_FF ** -0.5),
    }


def _fwd_reference(x, norm_mix_w, w_in, ssd_conv_w, ssd_conv_b, ssd_dt_bias, ssd_a_log, ssd_d,
              ssd_norm_w, q_norm_w, k_norm_w, attn_sinks, cm_dw_w, cm_dw_b, cm_ln_w, cm_ln_b,
              w_out, norm_mlp_w, w_mlp_up, w_mlp_down):
    cos, sin = rope_tables(x.shape[1], ATTN_HEAD_DIM)
    for i in range(DEPTH):
        x = hybrid_layer(x, cos, sin, norm_mix_w[i], w_in[i], ssd_conv_w[i], ssd_conv_b[i],
                         ssd_dt_bias[i], ssd_a_log[i], ssd_d[i], ssd_norm_w[i], q_norm_w[i],
                         k_norm_w[i], attn_sinks[i], cm_dw_w[i], cm_dw_b[i], cm_ln_w[i],
                         cm_ln_b[i], w_out[i], norm_mlp_w[i], w_mlp_up[i], w_mlp_down[i])
    return x


import jax as _jax
import jax.numpy as _jnp

TWIN_FORMAT = 'train_step'
FWD_PARAMS = ['x', 'norm_mix_w', 'w_in', 'ssd_conv_w', 'ssd_conv_b', 'ssd_dt_bias', 'ssd_a_log', 'ssd_d', 'ssd_norm_w', 'q_norm_w', 'k_norm_w', 'attn_sinks', 'cm_dw_w', 'cm_dw_b', 'cm_ln_w', 'cm_ln_b', 'w_out', 'norm_mlp_w', 'w_mlp_up', 'w_mlp_down']
TWIN_WEIGHTS = ['norm_mix_w', 'w_in', 'ssd_conv_w', 'ssd_conv_b', 'ssd_dt_bias', 'ssd_a_log', 'ssd_d', 'ssd_norm_w', 'q_norm_w', 'k_norm_w', 'attn_sinks', 'cm_dw_w', 'cm_dw_b', 'cm_ln_w', 'cm_ln_b', 'w_out', 'norm_mlp_w', 'w_mlp_up', 'w_mlp_down']
TWIN_DIFF_INPUT = 'x'
TWIN_INPUTS = ['x', 'norm_mix_w', 'w_in', 'ssd_conv_w', 'ssd_conv_b', 'ssd_dt_bias', 'ssd_a_log', 'ssd_d', 'ssd_norm_w', 'q_norm_w', 'k_norm_w', 'attn_sinks', 'cm_dw_w', 'cm_dw_b', 'cm_ln_w', 'cm_ln_b', 'w_out', 'norm_mlp_w', 'w_mlp_up', 'w_mlp_down', 'loss_target', 'm_norm_mix_w', 'm_w_in', 'm_ssd_conv_w', 'm_ssd_conv_b', 'm_ssd_dt_bias', 'm_ssd_a_log', 'm_ssd_d', 'm_ssd_norm_w', 'm_q_norm_w', 'm_k_norm_w', 'm_attn_sinks', 'm_cm_dw_w', 'm_cm_dw_b', 'm_cm_ln_w', 'm_cm_ln_b', 'm_w_out', 'm_norm_mlp_w', 'm_w_mlp_up', 'm_w_mlp_down', 'v_norm_mix_w', 'v_w_in', 'v_ssd_conv_w', 'v_ssd_conv_b', 'v_ssd_dt_bias', 'v_ssd_a_log', 'v_ssd_d', 'v_ssd_norm_w', 'v_q_norm_w', 'v_k_norm_w', 'v_attn_sinks', 'v_cm_dw_w', 'v_cm_dw_b', 'v_cm_ln_w', 'v_cm_ln_b', 'v_w_out', 'v_norm_mlp_w', 'v_w_mlp_up', 'v_w_mlp_down']
TWIN_OUTPUTS = ['loss', 'grad_x', 'grad_norm_mix_w', 'grad_w_in', 'grad_ssd_conv_w', 'grad_ssd_conv_b', 'grad_ssd_dt_bias', 'grad_ssd_a_log', 'grad_ssd_d', 'grad_ssd_norm_w', 'grad_q_norm_w', 'grad_k_norm_w', 'grad_attn_sinks', 'grad_cm_dw_w', 'grad_cm_dw_b', 'grad_cm_ln_w', 'grad_cm_ln_b', 'grad_w_out', 'grad_norm_mlp_w', 'grad_w_mlp_up', 'grad_w_mlp_down', 'delta_norm_mix_w', 'delta_w_in', 'delta_ssd_conv_w', 'delta_ssd_conv_b', 'delta_ssd_dt_bias', 'delta_ssd_a_log', 'delta_ssd_d', 'delta_ssd_norm_w', 'delta_q_norm_w', 'delta_k_norm_w', 'delta_attn_sinks', 'delta_cm_dw_w', 'delta_cm_dw_b', 'delta_cm_ln_w', 'delta_cm_ln_b', 'delta_w_out', 'delta_norm_mlp_w', 'delta_w_mlp_up', 'delta_w_mlp_down', 'new_m_norm_mix_w', 'new_m_w_in', 'new_m_ssd_conv_w', 'new_m_ssd_conv_b', 'new_m_ssd_dt_bias', 'new_m_ssd_a_log', 'new_m_ssd_d', 'new_m_ssd_norm_w', 'new_m_q_norm_w', 'new_m_k_norm_w', 'new_m_attn_sinks', 'new_m_cm_dw_w', 'new_m_cm_dw_b', 'new_m_cm_ln_w', 'new_m_cm_ln_b', 'new_m_w_out', 'new_m_norm_mlp_w', 'new_m_w_mlp_up', 'new_m_w_mlp_down', 'new_v_norm_mix_w', 'new_v_w_in', 'new_v_ssd_conv_w', 'new_v_ssd_conv_b', 'new_v_ssd_dt_bias', 'new_v_ssd_a_log', 'new_v_ssd_d', 'new_v_ssd_norm_w', 'new_v_q_norm_w', 'new_v_k_norm_w', 'new_v_attn_sinks', 'new_v_cm_dw_w', 'new_v_cm_dw_b', 'new_v_cm_ln_w', 'new_v_cm_ln_b', 'new_v_w_out', 'new_v_norm_mlp_w', 'new_v_w_mlp_up', 'new_v_w_mlp_down']
TWIN_LEAF_KINDS = {'loss': 'loss', 'grad_x': 'grad_x', 'grad_norm_mix_w': 'grad_w', 'grad_w_in': 'grad_w', 'grad_ssd_conv_w': 'grad_w', 'grad_ssd_conv_b': 'grad_w', 'grad_ssd_dt_bias': 'grad_w', 'grad_ssd_a_log': 'grad_w', 'grad_ssd_d': 'grad_w', 'grad_ssd_norm_w': 'grad_w', 'grad_q_norm_w': 'grad_w', 'grad_k_norm_w': 'grad_w', 'grad_attn_sinks': 'grad_w', 'grad_cm_dw_w': 'grad_w', 'grad_cm_dw_b': 'grad_w', 'grad_cm_ln_w': 'grad_w', 'grad_cm_ln_b': 'grad_w', 'grad_w_out': 'grad_w', 'grad_norm_mlp_w': 'grad_w', 'grad_w_mlp_up': 'grad_w', 'grad_w_mlp_down': 'grad_w', 'delta_norm_mix_w': 'delta_w', 'delta_w_in': 'delta_w', 'delta_ssd_conv_w': 'delta_w', 'delta_ssd_conv_b': 'delta_w', 'delta_ssd_dt_bias': 'delta_w', 'delta_ssd_a_log': 'delta_w', 'delta_ssd_d': 'delta_w', 'delta_ssd_norm_w': 'delta_w', 'delta_q_norm_w': 'delta_w', 'delta_k_norm_w': 'delta_w', 'delta_attn_sinks': 'delta_w', 'delta_cm_dw_w': 'delta_w', 'delta_cm_dw_b': 'delta_w', 'delta_cm_ln_w': 'delta_w', 'delta_cm_ln_b': 'delta_w', 'delta_w_out': 'delta_w', 'delta_norm_mlp_w': 'delta_w', 'delta_w_mlp_up': 'delta_w', 'delta_w_mlp_down': 'delta_w', 'new_m_norm_mix_w': 'new_m', 'new_m_w_in': 'new_m', 'new_m_ssd_conv_w': 'new_m', 'new_m_ssd_conv_b': 'new_m', 'new_m_ssd_dt_bias': 'new_m', 'new_m_ssd_a_log': 'new_m', 'new_m_ssd_d': 'new_m', 'new_m_ssd_norm_w': 'new_m', 'new_m_q_norm_w': 'new_m', 'new_m_k_norm_w': 'new_m', 'new_m_attn_sinks': 'new_m', 'new_m_cm_dw_w': 'new_m', 'new_m_cm_dw_b': 'new_m', 'new_m_cm_ln_w': 'new_m', 'new_m_cm_ln_b': 'new_m', 'new_m_w_out': 'new_m', 'new_m_norm_mlp_w': 'new_m', 'new_m_w_mlp_up': 'new_m', 'new_m_w_mlp_down': 'new_m', 'new_v_norm_mix_w': 'new_v', 'new_v_w_in': 'new_v', 'new_v_ssd_conv_w': 'new_v', 'new_v_ssd_conv_b': 'new_v', 'new_v_ssd_dt_bias': 'new_v', 'new_v_ssd_a_log': 'new_v', 'new_v_ssd_d': 'new_v', 'new_v_ssd_norm_w': 'new_v', 'new_v_q_norm_w': 'new_v', 'new_v_k_norm_w': 'new_v', 'new_v_attn_sinks': 'new_v', 'new_v_cm_dw_w': 'new_v', 'new_v_cm_dw_b': 'new_v', 'new_v_cm_ln_w': 'new_v', 'new_v_cm_ln_b': 'new_v', 'new_v_w_out': 'new_v', 'new_v_norm_mlp_w': 'new_v', 'new_v_w_mlp_up': 'new_v', 'new_v_w_mlp_down': 'new_v'}


def _forward(args):
    return _fwd_reference(*[args[k] for k in FWD_PARAMS])


def _output_shape():
    out = _jax.eval_shape(lambda: _forward(_fwd_setup_inputs(0)))
    return out.shape, out.dtype

N_MICROBATCH = 1
ADAM_LR = 0.001
ADAM_B1 = 0.9
ADAM_B2 = 0.999
ADAM_EPS = 1e-08
ADAM_WD = 0.01
ADAM_STEP = 10
PER_EXAMPLE_BATCH_AXIS = {'x': 0, 'loss_target': 0}
SHARED_INPUTS = []
_WEIGHT_DTYPES = {'norm_mix_w': _jnp.float32, 'w_in': _jnp.float32, 'ssd_conv_w': _jnp.float32, 'ssd_conv_b': _jnp.float32, 'ssd_dt_bias': _jnp.float32, 'ssd_a_log': _jnp.float32, 'ssd_d': _jnp.float32, 'ssd_norm_w': _jnp.float32, 'q_norm_w': _jnp.float32, 'k_norm_w': _jnp.float32, 'attn_sinks': _jnp.float32, 'cm_dw_w': _jnp.float32, 'cm_dw_b': _jnp.float32, 'cm_ln_w': _jnp.float32, 'cm_ln_b': _jnp.float32, 'w_out': _jnp.float32, 'norm_mlp_w': _jnp.float32, 'w_mlp_up': _jnp.float32, 'w_mlp_down': _jnp.float32}
MOMENT_SCALE = {'norm_mix_w': 6.016333e+00, 'w_in': 2.749541e+00, 'ssd_conv_w': 3.394669e+00, 'ssd_conv_b': 8.525499e+00, 'ssd_dt_bias': 2.508757e+00, 'ssd_a_log': 1.480448e+01, 'ssd_d': 1.484533e+01, 'ssd_norm_w': 1.663920e+01, 'q_norm_w': 7.201151e-01, 'k_norm_w': 6.988906e-01, 'attn_sinks': 4.827973e-01, 'cm_dw_w': 2.883170e+00, 'cm_dw_b': 1.875441e+01, 'cm_ln_w': 8.783448e+00, 'cm_ln_b': 1.086147e+01, 'w_out': 7.353469e+00, 'norm_mlp_w': 4.952601e+01, 'w_mlp_up': 4.023578e+00, 'w_mlp_down': 1.447804e+01}


def _to_microbatches(a, axis):
    t = _jnp.moveaxis(a, axis, 0)
    t = t.reshape((N_MICROBATCH, t.shape[0] // N_MICROBATCH) + t.shape[1:])
    return _jnp.moveaxis(t, 1, axis + 1)


def setup_inputs(seed: int = 0) -> dict:
    inp = _fwd_setup_inputs(seed)
    key = _jax.random.fold_in(_jax.random.key(seed), 7919)
    shape, _ = _output_shape()
    out = dict(inp)
    out["loss_target"] = _jax.random.normal(_jax.random.fold_in(key, 0), shape, _jnp.float32)
    for i, name in enumerate(TWIN_WEIGHTS):
        w = inp[name].astype(_jnp.float32)
        if MOMENT_SCALE is None:
            s = _jnp.sqrt(_jnp.mean(_jnp.square(w)) + 1e-30)
        else:
            s = MOMENT_SCALE[name]
        km, kv = _jax.random.split(_jax.random.fold_in(key, i + 1))
        out[name] = w
        out["m_" + name] = s * _jax.random.normal(km, w.shape, _jnp.float32)
        out["v_" + name] = (s * s) * _jax.random.uniform(kv, w.shape, _jnp.float32, 0.5, 1.5)
    if N_MICROBATCH > 1:
        for name, axis in PER_EXAMPLE_BATCH_AXIS.items():
            out[name] = _to_microbatches(out[name], axis)
    return {'x': out['x'], 'norm_mix_w': out['norm_mix_w'], 'w_in': out['w_in'], 'ssd_conv_w': out['ssd_conv_w'], 'ssd_conv_b': out['ssd_conv_b'], 'ssd_dt_bias': out['ssd_dt_bias'], 'ssd_a_log': out['ssd_a_log'], 'ssd_d': out['ssd_d'], 'ssd_norm_w': out['ssd_norm_w'], 'q_norm_w': out['q_norm_w'], 'k_norm_w': out['k_norm_w'], 'attn_sinks': out['attn_sinks'], 'cm_dw_w': out['cm_dw_w'], 'cm_dw_b': out['cm_dw_b'], 'cm_ln_w': out['cm_ln_w'], 'cm_ln_b': out['cm_ln_b'], 'w_out': out['w_out'], 'norm_mlp_w': out['norm_mlp_w'], 'w_mlp_up': out['w_mlp_up'], 'w_mlp_down': out['w_mlp_down'], 'loss_target': out['loss_target'], 'm_norm_mix_w': out['m_norm_mix_w'], 'm_w_in': out['m_w_in'], 'm_ssd_conv_w': out['m_ssd_conv_w'], 'm_ssd_conv_b': out['m_ssd_conv_b'], 'm_ssd_dt_bias': out['m_ssd_dt_bias'], 'm_ssd_a_log': out['m_ssd_a_log'], 'm_ssd_d': out['m_ssd_d'], 'm_ssd_norm_w': out['m_ssd_norm_w'], 'm_q_norm_w': out['m_q_norm_w'], 'm_k_norm_w': out['m_k_norm_w'], 'm_attn_sinks': out['m_attn_sinks'], 'm_cm_dw_w': out['m_cm_dw_w'], 'm_cm_dw_b': out['m_cm_dw_b'], 'm_cm_ln_w': out['m_cm_ln_w'], 'm_cm_ln_b': out['m_cm_ln_b'], 'm_w_out': out['m_w_out'], 'm_norm_mlp_w': out['m_norm_mlp_w'], 'm_w_mlp_up': out['m_w_mlp_up'], 'm_w_mlp_down': out['m_w_mlp_down'], 'v_norm_mix_w': out['v_norm_mix_w'], 'v_w_in': out['v_w_in'], 'v_ssd_conv_w': out['v_ssd_conv_w'], 'v_ssd_conv_b': out['v_ssd_conv_b'], 'v_ssd_dt_bias': out['v_ssd_dt_bias'], 'v_ssd_a_log': out['v_ssd_a_log'], 'v_ssd_d': out['v_ssd_d'], 'v_ssd_norm_w': out['v_ssd_norm_w'], 'v_q_norm_w': out['v_q_norm_w'], 'v_k_norm_w': out['v_k_norm_w'], 'v_attn_sinks': out['v_attn_sinks'], 'v_cm_dw_w': out['v_cm_dw_w'], 'v_cm_dw_b': out['v_cm_dw_b'], 'v_cm_ln_w': out['v_cm_ln_w'], 'v_cm_ln_b': out['v_cm_ln_b'], 'v_w_out': out['v_w_out'], 'v_norm_mlp_w': out['v_norm_mlp_w'], 'v_w_mlp_up': out['v_w_mlp_up'], 'v_w_mlp_down': out['v_w_mlp_down']}


def _loss(weights, diff, rest, loss_target):
    with _jax.named_scope("forward"):
        args = {**rest, TWIN_DIFF_INPUT: diff, **{k: w.astype(_WEIGHT_DTYPES[k]) for k, w in weights.items()}}
        y = _forward(args)
    with _jax.named_scope("loss_head"):
        err = _jnp.square(y.astype(_jnp.float32) - loss_target)
        return 0.5 * _jnp.sum(_jnp.mean(err, axis=-1)) if err.ndim else 0.5 * err


def _adamw(w, g, m, v):
    m = ADAM_B1 * m + (1.0 - ADAM_B1) * g
    v = ADAM_B2 * v + (1.0 - ADAM_B2) * _jnp.square(g)
    m_hat = m / (1.0 - ADAM_B1 ** ADAM_STEP)
    v_hat = v / (1.0 - ADAM_B2 ** ADAM_STEP)
    delta = -ADAM_LR * (m_hat / (_jnp.sqrt(v_hat) + ADAM_EPS) + ADAM_WD * w)
    return delta, m, v


def reference(x, norm_mix_w, w_in, ssd_conv_w, ssd_conv_b, ssd_dt_bias, ssd_a_log, ssd_d, ssd_norm_w, q_norm_w, k_norm_w, attn_sinks, cm_dw_w, cm_dw_b, cm_ln_w, cm_ln_b, w_out, norm_mlp_w, w_mlp_up, w_mlp_down, loss_target, m_norm_mix_w, m_w_in, m_ssd_conv_w, m_ssd_conv_b, m_ssd_dt_bias, m_ssd_a_log, m_ssd_d, m_ssd_norm_w, m_q_norm_w, m_k_norm_w, m_attn_sinks, m_cm_dw_w, m_cm_dw_b, m_cm_ln_w, m_cm_ln_b, m_w_out, m_norm_mlp_w, m_w_mlp_up, m_w_mlp_down, v_norm_mix_w, v_w_in, v_ssd_conv_w, v_ssd_conv_b, v_ssd_dt_bias, v_ssd_a_log, v_ssd_d, v_ssd_norm_w, v_q_norm_w, v_k_norm_w, v_attn_sinks, v_cm_dw_w, v_cm_dw_b, v_cm_ln_w, v_cm_ln_b, v_w_out, v_norm_mlp_w, v_w_mlp_up, v_w_mlp_down):
    given = dict(x=x, norm_mix_w=norm_mix_w, w_in=w_in, ssd_conv_w=ssd_conv_w, ssd_conv_b=ssd_conv_b, ssd_dt_bias=ssd_dt_bias, ssd_a_log=ssd_a_log, ssd_d=ssd_d, ssd_norm_w=ssd_norm_w, q_norm_w=q_norm_w, k_norm_w=k_norm_w, attn_sinks=attn_sinks, cm_dw_w=cm_dw_w, cm_dw_b=cm_dw_b, cm_ln_w=cm_ln_w, cm_ln_b=cm_ln_b, w_out=w_out, norm_mlp_w=norm_mlp_w, w_mlp_up=w_mlp_up, w_mlp_down=w_mlp_down, loss_target=loss_target, m_norm_mix_w=m_norm_mix_w, m_w_in=m_w_in, m_ssd_conv_w=m_ssd_conv_w, m_ssd_conv_b=m_ssd_conv_b, m_ssd_dt_bias=m_ssd_dt_bias, m_ssd_a_log=m_ssd_a_log, m_ssd_d=m_ssd_d, m_ssd_norm_w=m_ssd_norm_w, m_q_norm_w=m_q_norm_w, m_k_norm_w=m_k_norm_w, m_attn_sinks=m_attn_sinks, m_cm_dw_w=m_cm_dw_w, m_cm_dw_b=m_cm_dw_b, m_cm_ln_w=m_cm_ln_w, m_cm_ln_b=m_cm_ln_b, m_w_out=m_w_out, m_norm_mlp_w=m_norm_mlp_w, m_w_mlp_up=m_w_mlp_up, m_w_mlp_down=m_w_mlp_down, v_norm_mix_w=v_norm_mix_w, v_w_in=v_w_in, v_ssd_conv_w=v_ssd_conv_w, v_ssd_conv_b=v_ssd_conv_b, v_ssd_dt_bias=v_ssd_dt_bias, v_ssd_a_log=v_ssd_a_log, v_ssd_d=v_ssd_d, v_ssd_norm_w=v_ssd_norm_w, v_q_norm_w=v_q_norm_w, v_k_norm_w=v_k_norm_w, v_attn_sinks=v_attn_sinks, v_cm_dw_w=v_cm_dw_w, v_cm_dw_b=v_cm_dw_b, v_cm_ln_w=v_cm_ln_w, v_cm_ln_b=v_cm_ln_b, v_w_out=v_w_out, v_norm_mlp_w=v_norm_mlp_w, v_w_mlp_up=v_w_mlp_up, v_w_mlp_down=v_w_mlp_down)
    weights = {n: given[n] for n in TWIN_WEIGHTS}
    shared = {n: given[n] for n in SHARED_INPUTS}
    per_example = {n: given[n] for n in ['x']}
    grad_fn = _jax.value_and_grad(_loss, argnums=(0, 1))

    def one_microbatch(ex, loss_target):
        ex = dict(ex)
        diff = ex.pop(TWIN_DIFF_INPUT)
        return grad_fn(weights, diff, {**shared, **ex}, loss_target)

    if N_MICROBATCH == 1:
        loss, (grad_w, grad_x) = one_microbatch(per_example, given["loss_target"])
    else:
        def body(carry, xs):
            loss_sum, grad_sum = carry
            l_k, (gw_k, gx_k) = one_microbatch(xs[0], xs[1])
            with _jax.named_scope("update"):
                return (loss_sum + l_k, _jax.tree.map(_jnp.add, grad_sum, gw_k)), gx_k

        init = (_jnp.zeros((), _jnp.float32), _jax.tree.map(_jnp.zeros_like, weights))
        (loss, grad_w), grad_x = _jax.lax.scan(body, init, (per_example, given["loss_target"]))
    with _jax.named_scope("update"):
        delta_w, new_m, new_v = {}, {}, {}
        for n in TWIN_WEIGHTS:
            delta_w[n], new_m[n], new_v[n] = _adamw(weights[n], grad_w[n], given["m_" + n], given["v_" + n])
    return (loss, grad_x, *[grad_w[n] for n in TWIN_WEIGHTS], *[delta_w[n] for n in TWIN_WEIGHTS],
            *[new_m[n] for n in TWIN_WEIGHTS], *[new_v[n] for n in TWIN_WEIGHTS])
```

```python
import functools
import math

import jax
import jax.numpy as jnp
from jax import lax
from jax.experimental import pallas as pl
from jax.experimental.pallas import tpu as pltpu

F32 = jnp.float32
BF16 = jnp.bfloat16
MESH = pl.DeviceIdType.MESH
ANY = pl.BlockSpec(memory_space=pl.ANY)
VMEM_SPEC = pl.BlockSpec(memory_space=pltpu.VMEM)

D = 1024
L = 2048
DEPTH = 4
SSD_W = 1024
XBC = 1536
NH = 16
HP = 64
NS = 128
Q = 128
NC = L // Q
ATT_W = 512
NQH = 8
NKV = 2
HD = 64
HH = HD // 2
CMC = 512
CMK = 31
DFF = 4096
N_IN = 4368
N_PAD = 4480
RMS_EPS = 1e-6
LN_EPS = 1e-5
NEG = -1e30
LR, B1, B2, EPS_A, WD, STEP = 0.001, 0.9, 0.999, 1e-8, 0.01, 10
VMEM_LIMIT = 56 * 1024 * 1024
N_DEV = 8


def _cp(sem=None):
    kw = dict(vmem_limit_bytes=VMEM_LIMIT)
    if sem is not None:
        kw["dimension_semantics"] = sem
    return pltpu.CompilerParams(**kw)


def _dg(a, b, ca, cb):
    return lax.dot_general(a, b, (((ca,), (cb,)), ((), ())), preferred_element_type=F32)


def _split3(x):
    hi = x.astype(BF16)
    r = x - hi.astype(F32)
    mid = r.astype(BF16)
    lo = (r - mid.astype(F32)).astype(BF16)
    return hi, mid, lo


def _xdot_l(x, m, ca=1, cb=0):
    hi, mid, lo = _split3(x)
    return _dg(hi, m, ca, cb) + _dg(mid, m, ca, cb) + _dg(lo, m, ca, cb)


def _xdot_r(m, x, ca=1, cb=0):
    hi, mid, lo = _split3(x)
    return _dg(m, hi, ca, cb) + _dg(m, mid, ca, cb) + _dg(m, lo, ca, cb)


def _rowdot(v, m):
    return _xdot_l(jnp.broadcast_to(v, (8, v.shape[1])), m)[0:1]


def _sigmoid(x):
    return 1.0 / (1.0 + jnp.exp(-x))


def _softplus(x):
    e = jnp.exp(-jnp.abs(x))
    u = 1.0 + e
    l1p = jnp.where(u == 1.0, e, jnp.log(u) * (e / jnp.where(u == 1.0, 1.0, u - 1.0)))
    return jnp.maximum(x, 0.0) + l1p


def _mm(a, b, mode, name, add=None, out_dtype=F32, tm=512, tn=512):
    if mode == "nn":
        m, k = a.shape
        n = b.shape[1]
        a_spec = pl.BlockSpec((tm, k), lambda i, j: (i, 0))
        b_spec = pl.BlockSpec((k, tn), lambda i, j: (0, j))
        ca, cb = 1, 0
    elif mode == "nt":
        m, k = a.shape
        n = b.shape[0]
        a_spec = pl.BlockSpec((tm, k), lambda i, j: (i, 0))
        b_spec = pl.BlockSpec((tn, k), lambda i, j: (j, 0))
        ca, cb = 1, 1
    else:
        k, m = a.shape
        n = b.shape[1]
        a_spec = pl.BlockSpec((k, tm), lambda i, j: (0, i))
        b_spec = pl.BlockSpec((k, tn), lambda i, j: (0, j))
        ca, cb = 0, 0
    assert m % tm == 0 and n % tn == 0, (m, n, tm, tn)
    o_spec = pl.BlockSpec((tm, tn), lambda i, j: (i, j))

    if add is None:
        def body(a_ref, b_ref, o_ref):
            o_ref[...] = _dg(a_ref[...], b_ref[...], ca, cb).astype(o_ref.dtype)
        ins, specs = (a, b), [a_spec, b_spec]
    else:
        def body(a_ref, b_ref, c_ref, o_ref):
            o_ref[...] = (_dg(a_ref[...], b_ref[...], ca, cb) + c_ref[...]).astype(o_ref.dtype)
        ins, specs = (a, b, add), [a_spec, b_spec, o_spec]

    return pl.pallas_call(
        body, name=name, grid=(m // tm, n // tn), in_specs=specs, out_specs=o_spec,
        out_shape=jax.ShapeDtypeStruct((m, n), out_dtype),
        compiler_params=_cp(("parallel", "parallel")))(*ins)


TR = 256


def _rms_fwd(x, w, name):
    def body(x_ref, w_ref, o_ref):
        xv = x_ref[...]
        r = lax.rsqrt(jnp.mean(xv * xv, axis=-1, keepdims=True) + RMS_EPS)
        o_ref[...] = (xv * r * w_ref[...]).astype(BF16)

    return pl.pallas_call(
        body, name=name, grid=(L // TR,),
        in_specs=[pl.BlockSpec((TR, D), lambda i: (i, 0)), pl.BlockSpec((1, D), lambda i: (0, 0))],
        out_specs=pl.BlockSpec((TR, D), lambda i: (i, 0)),
        out_shape=jax.ShapeDtypeStruct((L, D), BF16), compiler_params=_cp(("parallel",)))(x, w)


def _rms_bwd(x, w, dh, dres, name):
    def body(x_ref, w_ref, dh_ref, dr_ref, dx_ref, dxb_ref, dw_ref):
        xv = x_ref[...]
        r = lax.rsqrt(jnp.mean(xv * xv, axis=-1, keepdims=True) + RMS_EPS)
        n = xv * r
        dhv = dh_ref[...]
        g = dhv * w_ref[...]
        dx = dr_ref[...] + r * (g - n * jnp.mean(g * n, axis=-1, keepdims=True))
        dx_ref[...] = dx
        dxb_ref[...] = dx.astype(BF16)

        @pl.when(pl.program_id(0) == 0)
        def _():
            dw_ref[...] = jnp.zeros_like(dw_ref)
        dw_ref[...] += jnp.sum(dhv * n, axis=0, keepdims=True)

    row = pl.BlockSpec((TR, D), lambda i: (i, 0))
    vec = pl.BlockSpec((1, D), lambda i: (0, 0))
    return pl.pallas_call(
        body, name=name, grid=(L // TR,), in_specs=[row, vec, row, row], out_specs=[row, row, vec],
        out_shape=[jax.ShapeDtypeStruct((L, D), F32), jax.ShapeDtypeStruct((L, D), BF16),
                   jax.ShapeDtypeStruct((1, D), F32)],
        compiler_params=_cp(("arbitrary",)))(x, w, dh, dres)


def _relu2_fwd(a, name):
    def body(a_ref, o_ref):
        r = jnp.maximum(a_ref[...], 0.0)
        o_ref[...] = (r * r).astype(BF16)

    row = pl.BlockSpec((TR, DFF), lambda i: (i, 0))
    return pl.pallas_call(body, name=name, grid=(L // TR,), in_specs=[row], out_specs=row,
                          out_shape=jax.ShapeDtypeStruct((L, DFF), BF16),
                          compiler_params=_cp(("parallel",)))(a)


def _relu2_bwd(dr, a, name):
    def body(dr_ref, a_ref, o_ref):
        o_ref[...] = (dr_ref[...] * 2.0 * jnp.maximum(a_ref[...], 0.0)).astype(BF16)

    row = pl.BlockSpec((TR, DFF), lambda i: (i, 0))
    return pl.pallas_call(body, name=name, grid=(L // TR,), in_specs=[row, row], out_specs=row,
                          out_shape=jax.ShapeDtypeStruct((L, DFF), BF16),
                          compiler_params=_cp(("parallel",)))(dr, a)


def _loss_bwd(y, t, name):
    def body(y_ref, t_ref, l_ref, d_ref, db_ref):
        e = y_ref[...] - t_ref[...]
        d = e * (1.0 / D)
        d_ref[...] = d
        db_ref[...] = d.astype(BF16)

        @pl.when(pl.program_id(0) == 0)
        def _():
            l_ref[...] = jnp.zeros_like(l_ref)
        s = jnp.sum(jnp.sum(e * e, axis=-1, keepdims=True), axis=0, keepdims=True)
        l_ref[...] += jnp.broadcast_to(s, l_ref.shape)

    row = pl.BlockSpec((TR, D), lambda i: (i, 0))
    tile = pl.BlockSpec((8, 128), lambda i: (0, 0))
    return pl.pallas_call(
        body, name=name, grid=(L // TR,), in_specs=[row, row], out_specs=[tile, row, row],
        out_shape=[jax.ShapeDtypeStruct((8, 128), F32), jax.ShapeDtypeStruct((L, D), F32),
                   jax.ShapeDtypeStruct((L, D), BF16)],
        compiler_params=_cp(("arbitrary",)))(y, t)


def _adamw(w, g, m, v, name):
    rows, cols = w.shape
    tr = rows
    for cand in (512, 256, 128, 64, 32, 16, 8):
        if rows % cand == 0 and cand * cols * 4 <= 2 * 1024 * 1024:
            tr = cand
            break
    c1 = 1.0 / (1.0 - B1 ** STEP)
    c2 = 1.0 / (1.0 - B2 ** STEP)

    def body(w_ref, g_ref, m_ref, v_ref, d_ref, mo_ref, vo_ref):
        gv = g_ref[...]
        mn = B1 * m_ref[...] + (1.0 - B1) * gv
        vn = B2 * v_ref[...] + (1.0 - B2) * (gv * gv)
        mo_ref[...] = mn
        vo_ref[...] = vn
        d_ref[...] = -LR * ((mn * c1) / (jnp.sqrt(vn * c2) + EPS_A) + WD * w_ref[...])

    blk = pl.BlockSpec((tr, cols), lambda i: (i, 0))
    shp = jax.ShapeDtypeStruct((rows, cols), F32)
    return pl.pallas_call(body, name=name, grid=(rows // tr,), in_specs=[blk] * 4, out_specs=[blk] * 3,
                          out_shape=[shp] * 3, compiler_params=_cp(("parallel",)))(w, g, m, v)


def _add_n(xs, name, out_dtype=F32):
    rows, cols = xs[0].shape
    tr = rows
    for cand in (512, 256, 128, 64, 32, 16):
        if rows % cand == 0 and cand * cols * 4 <= 2 * 1024 * 1024:
            tr = cand
            break
    n = len(xs)

    def body(*refs):
        acc = refs[0][...].astype(F32)
        for r in refs[1:n]:
            acc = acc + r[...].astype(F32)
        refs[n][...] = acc.astype(out_dtype)

    blk = pl.BlockSpec((tr, cols), lambda i: (i, 0))
    return pl.pallas_call(body, name=name, grid=(rows // tr,), in_specs=[blk] * n, out_specs=blk,
                          out_shape=jax.ShapeDtypeStruct((rows, cols), out_dtype),
                          compiler_params=_cp(("parallel",)))(*xs)


CT = 256
CPAD = 32


def _conf_fwd(a, g, w, b, lw, lb, name):
    def body(a_ref, g_ref, w_ref, b_ref, lw_ref, lb_ref, o_ref, hp_ref, win_ref):
        hp_ref[0:CPAD, :] = jnp.zeros((CPAD, CMC), F32)
        hp_ref[CPAD:, :] = a_ref[...] * _sigmoid(g_ref[...])

        def tile(i, carry):
            base = pl.multiple_of(i * CT, CT)
            win_ref[...] = hp_ref[pl.ds(base, CT + CPAD), :]
            c = jnp.broadcast_to(b_ref[...], (CT, CMC))
            for k in range(CMK):
                c = c + w_ref[k:k + 1, :] * win_ref[pl.ds(2 + k, CT), :]
            mu = jnp.mean(c, axis=-1, keepdims=True)
            cc = c - mu
            var = jnp.mean(cc * cc, axis=-1, keepdims=True)
            l = cc * lax.rsqrt(var + LN_EPS) * lw_ref[...] + lb_ref[...]
            o_ref[pl.ds(base, CT), :] = (l * _sigmoid(l)).astype(BF16)
            return carry

        lax.fori_loop(0, L // CT, tile, 0)

    return pl.pallas_call(
        body, name=name, in_specs=[VMEM_SPEC] * 6, out_specs=VMEM_SPEC,
        out_shape=jax.ShapeDtypeStruct((L, CMC), BF16),
        scratch_shapes=[pltpu.VMEM((L + CPAD, CMC), F32), pltpu.VMEM((CT + CPAD, CMC), F32)],
        compiler_params=_cp())(a, g, w, b, lw, lb)


def _conf_bwd(a, g, w, b, lw, lb, dy, name):
    def body(a_ref, g_ref, w_ref, b_ref, lw_ref, lb_ref, dy_ref,
             da_ref, dg_ref, dw_ref, db_ref, dlw_ref, dlb_ref, hp_ref, dcp_ref, win_ref):
        hp_ref[0:CPAD, :] = jnp.zeros((CPAD, CMC), F32)
        hp_ref[CPAD:, :] = a_ref[...] * _sigmoid(g_ref[...])
        dcp_ref[L:, :] = jnp.zeros((CPAD, CMC), F32)
        dw_ref[...] = jnp.zeros_like(dw_ref)
        db_ref[...] = jnp.zeros_like(db_ref)
        dlw_ref[...] = jnp.zeros_like(dlw_ref)
        dlb_ref[...] = jnp.zeros_like(dlb_ref)

        def tile1(i, carry):
            base = pl.multiple_of(i * CT, CT)
            win_ref[...] = hp_ref[pl.ds(base, CT + CPAD), :]
            c = jnp.broadcast_to(b_ref[...], (CT, CMC))
            for k in range(CMK):
                c = c + w_ref[k:k + 1, :] * win_ref[pl.ds(2 + k, CT), :]
            mu = jnp.mean(c, axis=-1, keepdims=True)
            cc = c - mu
            var = jnp.mean(cc * cc, axis=-1, keepdims=True)
            rstd = lax.rsqrt(var + LN_EPS)
            n = cc * rstd
            l = n * lw_ref[...] + lb_ref[...]
            sl = _sigmoid(l)
            dl = dy_ref[pl.ds(base, CT), :] * (sl * (1.0 + l * (1.0 - sl)))
            dlw_ref[...] += jnp.sum(dl * n, axis=0, keepdims=True)
            dlb_ref[...] += jnp.sum(dl, axis=0, keepdims=True)
            dn = dl * lw_ref[...]
            dc = rstd * (dn - jnp.mean(dn, axis=-1, keepdims=True)
                         - n * jnp.mean(dn * n, axis=-1, keepdims=True))
            db_ref[...] += jnp.sum(dc, axis=0, keepdims=True)
            for k in range(CMK):
                dw_ref[k:k + 1, :] += jnp.sum(dc * win_ref[pl.ds(2 + k, CT), :], axis=0, keepdims=True)
            dcp_ref[pl.ds(base, CT), :] = dc
            return carry

        lax.fori_loop(0, L // CT, tile1, 0)

        def tile2(i, carry):
            base = pl.multiple_of(i * CT, CT)
            win_ref[...] = dcp_ref[pl.ds(base, CT + CPAD), :]
            dh = jnp.zeros((CT, CMC), F32)
            for k in range(CMK):
                dh = dh + w_ref[k:k + 1, :] * win_ref[pl.ds(CMK - 1 - k, CT), :]
            av = a_ref[pl.ds(base, CT), :]
            sg = _sigmoid(g_ref[pl.ds(base, CT), :])
            da_ref[pl.ds(base, CT), :] = (dh * sg).astype(BF16)
            dg_ref[pl.ds(base, CT), :] = (dh * av * sg * (1.0 - sg)).astype(BF16)
            return carry

        lax.fori_loop(0, L // CT, tile2, 0)

    vec = jax.ShapeDtypeStruct((1, CMC), F32)
    return pl.pallas_call(
        body, name=name, in_specs=[VMEM_SPEC] * 7, out_specs=[VMEM_SPEC] * 6,
        out_shape=[jax.ShapeDtypeStruct((L, CMC), BF16), jax.ShapeDtypeStruct((L, CMC), BF16),
                   jax.ShapeDtypeStruct((32, CMC), F32), vec, vec, vec],
        scratch_shapes=[pltpu.VMEM((L + CPAD, CMC), F32), pltpu.VMEM((L + CPAD, CMC), F32),
                        pltpu.VMEM((CT + CPAD, CMC), F32)],
        compiler_params=_cp())(a, g, w, b, lw, lb, dy)


def _attn_prep(a, b, wa, wb, c, s):
    r = lax.rsqrt((jnp.sum(a * a, axis=-1, keepdims=True) + jnp.sum(b * b, axis=-1, keepdims=True))
                  * (1.0 / HD) + RMS_EPS)
    ha = a * r
    hb = b * r
    na = ha * wa
    nb = hb * wb
    return r, ha, hb, na * c - nb * s, nb * c + na * s


def _attn_scores(qa_n, qb_n, ka_c, kb_c, n, lo, sink):
    nk = ka_c.shape[0]
    s = (_dg(qa_n, ka_c, 1, 1) + _dg(qb_n, kb_c, 1, 1)) * (1.0 / math.sqrt(HD))
    qi = lax.broadcasted_iota(jnp.int32, (Q, nk), 0) + n * Q
    ki = lax.broadcasted_iota(jnp.int32, (Q, nk), 1) + lo
    diff = qi - ki
    s = jnp.where((diff >= 0) & (diff < Q), s, NEG)
    m = jnp.maximum(jnp.max(s, axis=-1, keepdims=True), sink)
    p = jnp.exp(s - m)
    ps = jnp.exp(sink - m)
    den = jnp.sum(p, axis=-1, keepdims=True) + ps
    return p / den, ps / den


def _attn_specs():
    qh = pl.BlockSpec((1, L, HH), lambda j: (j, 0, 0))
    kh = pl.BlockSpec((1, L, HH), lambda j: (j // 4, 0, 0))
    vv = pl.BlockSpec((1, L, HD), lambda j: (j // 4, 0, 0))
    tab = pl.BlockSpec((L, HH), lambda j: (0, 0))
    wv = pl.BlockSpec((1, HH), lambda j: (0, 0))
    sk = pl.BlockSpec((1, 1, 128), lambda j: (j, 0, 0))
    return qh, kh, vv, tab, wv, sk


def _attn_fwd(qa, qb, ka, kb, v, cos, sin, qwa, qwb, kwa, kwb, sinks, name):
    def body(qa_ref, qb_ref, ka_ref, kb_ref, v_ref, c_ref, s_ref, qwa_ref, qwb_ref, kwa_ref, kwb_ref,
             sk_ref, o_ref):
        c = c_ref[...]
        s = s_ref[...]
        _, _, _, qra, qrb = _attn_prep(qa_ref[0], qb_ref[0], qwa_ref[...], qwb_ref[...], c, s)
        _, _, _, kra, krb = _attn_prep(ka_ref[0], kb_ref[0], kwa_ref[...], kwb_ref[...], c, s)
        qra, qrb, kra, krb = (t.astype(BF16) for t in (qra, qrb, kra, krb))
        vb = v_ref[0].astype(BF16)
        sink = sk_ref[0][:, 0:1]
        for n in range(NC):
            lo = max(n - 1, 0) * Q
            hi = (n + 1) * Q
            p, _ = _attn_scores(qra[n * Q:hi], qrb[n * Q:hi], kra[lo:hi], krb[lo:hi], n, lo, sink)
            o_ref[0, n * Q:hi, :] = _dg(p.astype(BF16), vb[lo:hi], 1, 0)

    qh, kh, vv, tab, wv, sk = _attn_specs()
    return pl.pallas_call(
        body, name=name, grid=(NQH,),
        in_specs=[qh, qh, kh, kh, vv, tab, tab, wv, wv, wv, wv, sk],
        out_specs=pl.BlockSpec((1, L, HD), lambda j: (j, 0, 0)),
        out_shape=jax.ShapeDtypeStruct((NQH, L, HD), F32),
        compiler_params=_cp(("parallel",)))(qa, qb, ka, kb, v, cos, sin, qwa, qwb, kwa, kwb, sinks)


def _attn_bwd(qa, qb, ka, kb, v, cos, sin, qwa, qwb, kwa, kwb, sinks, do, name):
    def body(qa_ref, qb_ref, ka_ref, kb_ref, v_ref, c_ref, s_ref, qwa_ref, qwb_ref, kwa_ref, kwb_ref,
             sk_ref, do_ref,
             dqa_ref, dqb_ref, dka_ref, dkb_ref, dv_ref, dqwa_ref, dqwb_ref, dkwa_ref, dkwb_ref, dsk_ref,
             dqra_s, dqrb_s, dkra_s, dkrb_s, dv_s):
        j = pl.program_id(0)
        c = c_ref[...]
        s = s_ref[...]
        qwa, qwb, kwa, kwb = qwa_ref[...], qwb_ref[...], kwa_ref[...], kwb_ref[...]
        qr, qha, qhb, qra, qrb = _attn_prep(qa_ref[0], qb_ref[0], qwa, qwb, c, s)
        kr, kha, khb, kra, krb = _attn_prep(ka_ref[0], kb_ref[0], kwa, kwb, c, s)
        qra, qrb, kra, krb = (t.astype(BF16) for t in (qra, qrb, kra, krb))
        vb = v_ref[0].astype(BF16)
        sink = sk_ref[0][:, 0:1]
        dkra_s[...] = jnp.zeros_like(dkra_s)
        dkrb_s[...] = jnp.zeros_like(dkrb_s)
        dv_s[...] = jnp.zeros_like(dv_s)
        dsink = jnp.zeros((1, 1), F32)
        scale = 1.0 / math.sqrt(HD)
        for n in range(NC):
            lo = max(n - 1, 0) * Q
            hi = (n + 1) * Q
            p, ps = _attn_scores(qra[n * Q:hi], qrb[n * Q:hi], kra[lo:hi], krb[lo:hi], n, lo, sink)
            don = do_ref[0, n * Q:hi, :]
            dob = don.astype(BF16)
            pb = p.astype(BF16)
            dv_s[lo:hi, :] += _dg(pb, dob, 0, 0)
            dp = _dg(dob, vb[lo:hi], 1, 1)
            delta = jnp.sum(p * dp, axis=-1, keepdims=True)
            dsink = dsink - jnp.sum(ps * delta, axis=0, keepdims=True)
            ds = (p * (dp - delta) * scale).astype(BF16)
            dqra_s[n * Q:hi, :] = _dg(ds, kra[lo:hi], 1, 0)
            dqrb_s[n * Q:hi, :] = _dg(ds, krb[lo:hi], 1, 0)
            dkra_s[lo:hi, :] += _dg(ds, qra[n * Q:hi], 0, 0)
            dkrb_s[lo:hi, :] += _dg(ds, qrb[n * Q:hi], 0, 0)

        def unprep(dra, drb, r, ha, hb, wa, wb):
            dna = dra * c + drb * s
            dnb = drb * c - dra * s
            dwa = jnp.sum(dna * ha, axis=0, keepdims=True)
            dwb = jnp.sum(dnb * hb, axis=0, keepdims=True)
            ga = dna * wa
            gb = dnb * wb
            mu = (jnp.sum(ga * ha, axis=-1, keepdims=True) + jnp.sum(gb * hb, axis=-1, keepdims=True)) * (1.0 / HD)
            return r * (ga - ha * mu), r * (gb - hb * mu), dwa, dwb

        dqa, dqb, dqwa, dqwb = unprep(dqra_s[...], dqrb_s[...], qr, qha, qhb, qwa, qwb)
        dka, dkb, dkwa, dkwb = unprep(dkra_s[...], dkrb_s[...], kr, kha, khb, kwa, kwb)
        dqa_ref[0] = dqa
        dqb_ref[0] = dqb
        dsk_ref[0] = jnp.broadcast_to(dsink, (1, 128))

        @pl.when(j == 0)
        def _():
            dqwa_ref[...] = jnp.zeros_like(dqwa_ref)
            dqwb_ref[...] = jnp.zeros_like(dqwb_ref)
            dkwa_ref[...] = jnp.zeros_like(dkwa_ref)
            dkwb_ref[...] = jnp.zeros_like(dkwb_ref)
        dqwa_ref[...] += dqwa
        dqwb_ref[...] += dqwb
        dkwa_ref[...] += dkwa
        dkwb_ref[...] += dkwb

        @pl.when(j % 4 == 0)
        def _():
            dka_ref[0] = dka
            dkb_ref[0] = dkb
            dv_ref[0] = dv_s[...]

        @pl.when(j % 4 != 0)
        def _():
            dka_ref[0] += dka
            dkb_ref[0] += dkb
            dv_ref[0] += dv_s[...]

    qh, kh, vv, tab, wv, sk = _attn_specs()
    qo = pl.BlockSpec((1, L, HD), lambda j: (j, 0, 0))
    hvec = jax.ShapeDtypeStruct((1, HH), F32)
    return pl.pallas_call(
        body, name=name, grid=(NQH,),
        in_specs=[qh, qh, kh, kh, vv, tab, tab, wv, wv, wv, wv, sk, qo],
        out_specs=[qh, qh, kh, kh, vv, wv, wv, wv, wv, sk],
        out_shape=[jax.ShapeDtypeStruct((NQH, L, HH), F32), jax.ShapeDtypeStruct((NQH, L, HH), F32),
                   jax.ShapeDtypeStruct((NKV, L, HH), F32), jax.ShapeDtypeStruct((NKV, L, HH), F32),
                   jax.ShapeDtypeStruct((NKV, L, HD), F32), hvec, hvec, hvec, hvec,
                   jax.ShapeDtypeStruct((NQH, 1, 128), F32)],
        scratch_shapes=[pltpu.VMEM((L, HH), F32), pltpu.VMEM((L, HH), F32), pltpu.VMEM((L, HH), F32),
                        pltpu.VMEM((L, HH), F32), pltpu.VMEM((L, HD), F32)],
        compiler_params=_cp(("arbitrary",)))(qa, qb, ka, kb, v, cos, sin, qwa, qwb, kwa, kwb, sinks, do)


def _ssd_consts():
    hh = jnp.arange(128)[:, None]
    e = (hh == (jnp.arange(SSD_W)[None, :] // HP)).astype(BF16)
    e2 = (hh == (jnp.arange(NH * 128)[None, :] // 128)).astype(BF16)
    et = e.T
    tril = (jnp.arange(Q)[:, None] >= jnp.arange(Q)[None, :]).astype(BF16)
    triu = tril.T
    eye = jnp.eye(128, dtype=BF16)
    return e, e2, et, tril, triu, eye


def _ssd_common(x_ref, ext_scr, cw_ref, cb_ref, dt_ref, dtb_ref, alog_ref, e_ref, e2_ref, tril_ref, triu_ref,
                arow_scr, acol_scr, eax_scr):
    conv = jnp.broadcast_to(cb_ref[...], (Q, XBC))
    for k in range(4):
        conv = conv + cw_ref[k:k + 1, :] * ext_scr[pl.ds(5 + k, Q), :]
    sg = _sigmoid(conv)
    xbc = conv * sg
    dtpre = dt_ref[...] + dtb_ref[...]
    dt = _softplus(dtpre)
    a = -jnp.exp(alog_ref[...])
    adt = dt * a
    acol = _xdot_r(tril_ref[...], adt)
    acol_scr[...] = acol
    arow_scr[...] = _xdot_l(adt, triu_ref[...], 0, 0)
    alast = acol_scr[Q - 1:Q, :]
    ea = jnp.exp(acol)
    decs = jnp.exp(alast - acol)
    e = e_ref[...]
    dt_x = _xdot_l(dt, e)
    eax_scr[...] = _xdot_l(ea, e)
    decs_x = _xdot_l(decs, e)
    acx2 = _xdot_l(acol, e2_ref[...])
    return conv, sg, xbc, dtpre, dt, a, adt, acol, alast, ea, decs, dt_x, decs_x, acx2


def _ssd_fwd(z, xr, dtr, cw, cb, dtb, alog, dxp, nw, consts, name):
    e, e2, et, tril, triu, eye = consts

    def body(z_ref, x_ref, dt_ref, cw_ref, cb_ref, dtb_ref, alog_ref, dx_ref, nw_ref, e_ref, e2_ref,
             tril_ref, triu_ref, ya_ref, ypre_ref, st_ref, s_scr, ext_scr, arow_scr, acol_scr, eax_scr):
        c = pl.program_id(0)

        @pl.when(c == 0)
        def _():
            s_scr[...] = jnp.zeros_like(s_scr)
            ext_scr[0:8, :] = jnp.zeros((8, XBC), F32)
        ext_scr[8:8 + Q, :] = x_ref[...]
        (conv, sg, xbc, dtpre, dt, a, adt, acol, alast, ea, decs, dt_x, decs_x, acx2) = _ssd_common(
            x_ref, ext_scr, cw_ref, cb_ref, dt_ref, dtb_ref, alog_ref, e_ref, e2_ref, tril_ref, triu_ref,
            arow_scr, acol_scr, eax_scr)
        ext_scr[0:8, :] = ext_scr[Q:Q + 8, :]
        xs = xbc[:, :SSD_W]
        xdt = xs * dt_x
        lane = lax.broadcasted_iota(jnp.int32, (Q, 128), 1)
        causal = lax.broadcasted_iota(jnp.int32, (Q, Q), 0) >= lax.broadcasted_iota(jnp.int32, (Q, Q), 1)
        for g in range(2):
            bg = xbc[:, SSD_W + g * NS:SSD_W + (g + 1) * NS].astype(BF16)
            cg = xbc[:, SSD_W + 2 * NS + g * NS:SSD_W + 2 * NS + (g + 1) * NS].astype(BF16)
            cbm = _dg(cg, bg, 1, 1)
            sgv = s_scr[g]
            st_ref[0, g] = sgv
            gc = slice(g * 512, (g + 1) * 512)
            yoff = _dg(cg, sgv.astype(BF16), 1, 0) * eax_scr[:, gc]
            for pr in range(4):
                h0 = g * 8 + 2 * pr
                h1 = h0 + 1
                c0 = g * 512 + pr * 128
                xp = xdt[:, c0:c0 + 128].astype(BF16)
                w0 = (cbm * jnp.exp(jnp.where(causal, acx2[:, h0 * 128:(h0 + 1) * 128] - arow_scr[h0:h0 + 1, :],
                                              NEG))).astype(BF16)
                w1 = (cbm * jnp.exp(jnp.where(causal, acx2[:, h1 * 128:(h1 + 1) * 128] - arow_scr[h1:h1 + 1, :],
                                              NEG))).astype(BF16)
                yd = jnp.where(lane < HP, _dg(w0, xp, 1, 0), _dg(w1, xp, 1, 0))
                ypre_ref[:, c0:c0 + 128] = (yd + yoff[:, pr * 128:(pr + 1) * 128]
                                            + xs[:, c0:c0 + 128] * dx_ref[:, c0:c0 + 128])
            contrib = _dg(bg, (xdt[:, gc] * decs_x[:, gc]).astype(BF16), 0, 0)
            s_scr[g] = sgv * eax_scr[Q - 1:Q, gc] + contrib
        zz = z_ref[...]
        gg = ypre_ref[...] * (zz * _sigmoid(zz))
        for g in range(2):
            gc = slice(g * 512, (g + 1) * 512)
            ggg = gg[:, gc]
            rstd = lax.rsqrt(jnp.mean(ggg * ggg, axis=-1, keepdims=True) + RMS_EPS)
            ya_ref[:, gc] = (ggg * rstd * nw_ref[:, gc]).astype(BF16)

    def row(w):
        return pl.BlockSpec((Q, w), lambda c: (c, 0))

    def full(shape):
        return pl.BlockSpec(shape, lambda c: (0,) * len(shape))

    return pl.pallas_call(
        body, name=name, grid=(NC,),
        in_specs=[row(SSD_W), row(XBC), row(128), full((4, XBC)), full((1, XBC)), full((1, 128)), full((1, 128)),
                  full((1, SSD_W)), full((1, SSD_W)), full((128, SSD_W)), full((128, NH * 128)), full((Q, Q)),
                  full((Q, Q))],
        out_specs=[row(SSD_W), row(SSD_W), pl.BlockSpec((1, 2, NS, 512), lambda c: (c, 0, 0, 0))],
        out_shape=[jax.ShapeDtypeStruct((L, SSD_W), BF16), jax.ShapeDtypeStruct((L, SSD_W), F32),
                   jax.ShapeDtypeStruct((NC, 2, NS, 512), F32)],
        scratch_shapes=[pltpu.VMEM((2, NS, 512), F32), pltpu.VMEM((Q + 8, XBC), F32), pltpu.VMEM((128, Q), F32),
                        pltpu.VMEM((Q, 128), F32), pltpu.VMEM((Q, SSD_W), F32)],
        compiler_params=_cp(("arbitrary",)))(z, xr, dtr, cw, cb, dtb, alog, dxp, nw, e, e2, tril, triu)


def _ssd_bwd(z, xr, dtr, ypre, st, dya, cw, cb, dtb, alog, dxp, nw, consts, name):
    e, e2, et, tril, triu, eye = consts

    def body(z_ref, x_ref, xp_ref, dt_ref, ypre_ref, st_ref, dya_ref, cw_ref, cb_ref, dtb_ref, alog_ref, dx_ref,
             nw_ref, e_ref, e2_ref, et_ref, tril_ref, triu_ref, eye_ref,
             dz_ref, dxr_ref, ddtr_ref, dcw_ref, dcb_ref, ddtb_ref, dalog_ref, dd_ref, dnw_ref,
             g_scr, ext_scr, ext2_scr, arow_scr, acol_scr, eax_scr, darow_scr, dxdt_scr, t1_scr, t2_scr, dgg_scr):
        i = pl.program_id(0)

        @pl.when(i == 0)
        def _():
            g_scr[...] = jnp.zeros_like(g_scr)
            ext2_scr[Q:Q + 8, :] = jnp.zeros((8, XBC), F32)
            for r in (dcw_ref, dcb_ref, ddtb_ref, dalog_ref, dd_ref, dnw_ref):
                r[...] = jnp.zeros_like(r)
        not_first = jnp.where(i < NC - 1, 1.0, 0.0)
        ext_scr[0:8, :] = xp_ref[Q - 8:Q, :] * not_first
        ext_scr[8:8 + Q, :] = x_ref[...]
        (conv, sg, xbc, dtpre, dt, a, adt, acol, alast, ea, decs, dt_x, decs_x, acx2) = _ssd_common(
            x_ref, ext_scr, cw_ref, cb_ref, dt_ref, dtb_ref, alog_ref, e_ref, e2_ref, tril_ref, triu_ref,
            arow_scr, acol_scr, eax_scr)
        et_m = et_ref[...]
        xs = xbc[:, :SSD_W]
        xdt = xs * dt_x
        y = ypre_ref[...]
        zz = z_ref[...]
        sz = _sigmoid(zz)
        silu_z = zz * sz
        gg = y * silu_z
        dya = dya_ref[...]
        for g in range(2):
            gc = slice(g * 512, (g + 1) * 512)
            ggg = gg[:, gc]
            rstd = lax.rsqrt(jnp.mean(ggg * ggg, axis=-1, keepdims=True) + RMS_EPS)
            n = ggg * rstd
            dyag = dya[:, gc]
            dnw_ref[:, gc] += jnp.sum(dyag * n, axis=0, keepdims=True)
            dn = dyag * nw_ref[:, gc]
            dgg_scr[:, gc] = rstd * (dn - n * jnp.mean(dn * n, axis=-1, keepdims=True))
        dgg = dgg_scr[...]
        dy = dgg * silu_z
        dz_ref[...] = (dgg * y * (sz * (1.0 + zz * (1.0 - sz)))).astype(BF16)
        dd_ref[...] += _rowdot(jnp.sum(dy * xs, axis=0, keepdims=True), et_m)
        dxs = dy * dx_ref[...]
        dys = dy * eax_scr[...]
        lane = lax.broadcasted_iota(jnp.int32, (Q, 128), 1)
        causal = lax.broadcasted_iota(jnp.int32, (Q, Q), 0) >= lax.broadcasted_iota(jnp.int32, (Q, Q), 1)
        darow_scr[...] = jnp.zeros_like(darow_scr)
        dacol = jnp.zeros((Q, 128), F32)
        dcdx = []
        dbs = []
        dcs = []
        for g in range(2):
            gc = slice(g * 512, (g + 1) * 512)
            bg = xbc[:, SSD_W + g * NS:SSD_W + (g + 1) * NS].astype(BF16)
            cg = xbc[:, SSD_W + 2 * NS + g * NS:SSD_W + 2 * NS + (g + 1) * NS].astype(BF16)
            cbm = _dg(cg, bg, 1, 1)
            sgv = st_ref[0, g]
            sgb = sgv.astype(BF16)
            gv = g_scr[g]
            gvb = gv.astype(BF16)
            yoff = _dg(cg, sgb, 1, 0) * eax_scr[:, gc]
            dysg = dys[:, gc].astype(BF16)
            dcg = _dg(dysg, sgb, 1, 1)
            ds_off = _dg(cg, dysg, 0, 0)
            t1_scr[:, gc] = dy[:, gc] * yoff
            xdec = xdt[:, gc] * decs_x[:, gc]
            dxd = _dg(bg, gvb, 1, 0)
            dbg = _dg(xdec.astype(BF16), gvb, 1, 1)
            dxdt_g = dxd * decs_x[:, gc]
            t2_scr[:, gc] = dxd * xdt[:, gc]
            cdx = eax_scr[Q - 1:Q, gc]
            dcdx.append(jnp.sum(gv * sgv, axis=0, keepdims=True))
            g_scr[g] = gv * cdx + ds_off
            dcb_acc = jnp.zeros((Q, Q), F32)
            for pr in range(4):
                c0 = g * 512 + pr * 128
                xp = xdt[:, c0:c0 + 128].astype(BF16)
                dyp = dy[:, c0:c0 + 128]
                dypb = dyp.astype(BF16)
                halves = []
                for hh, keep in ((g * 8 + 2 * pr, lane < HP), (g * 8 + 2 * pr + 1, lane >= HP)):
                    lam = jnp.exp(jnp.where(causal, acx2[:, hh * 128:(hh + 1) * 128] - arow_scr[hh:hh + 1, :], NEG))
                    w = cbm * lam
                    dw = _dg(jnp.where(keep, dyp, 0.0).astype(BF16), xp, 1, 1)
                    dcb_acc = dcb_acc + dw * lam
                    t = dw * w
                    dacol = dacol + jnp.sum(t, axis=-1, keepdims=True) * (lane == hh).astype(F32)
                    darow_scr[hh:hh + 1, :] -= jnp.sum(t, axis=0, keepdims=True)
                    halves.append(_dg(w.astype(BF16), dypb, 0, 0))
                dxdt_scr[:, c0:c0 + 128] = (jnp.where(lane < HP, halves[0], halves[1])
                                            + dxdt_g[:, pr * 128:(pr + 1) * 128])
            dcbb = dcb_acc.astype(BF16)
            dcs.append(dcg + _dg(dcbb, bg, 1, 0))
            dbs.append(dbg + _dg(dcbb, cg, 0, 0))
        dacol = dacol + _xdot_l(t1_scr[...], et_m)
        ddecs = _xdot_l(t2_scr[...], et_m) * decs
        dacol = dacol - ddecs
        dalast = jnp.sum(ddecs, axis=0, keepdims=True)
        dcd = _rowdot(jnp.concatenate(dcdx, axis=1), et_m)
        dalast = dalast + dcd * jnp.exp(alast)
        dacol = dacol + _xdot_l(darow_scr[...], eye_ref[...], 0, 0)
        rowi = lax.broadcasted_iota(jnp.int32, (Q, 128), 0)
        dacol = dacol + jnp.where(rowi == Q - 1, dalast, 0.0)
        dadt = _xdot_r(triu_ref[...], dacol)
        dxdt = dxdt_scr[...]
        ddt = dadt * a + _xdot_l(dxdt * xs, et_m)
        dalog_ref[...] += jnp.sum(dadt * dt, axis=0, keepdims=True) * a
        dxs = dxs + dxdt * dt_x
        ddtr = ddt * _sigmoid(dtpre)
        ddtb_ref[...] += jnp.sum(ddtr, axis=0, keepdims=True)
        ddtr_ref[...] = ddtr.astype(BF16)
        dsilu = sg * (1.0 + conv * (1.0 - sg))
        ext2_scr[0:Q, 0:SSD_W] = dxs * dsilu[:, :SSD_W]
        for g in range(2):
            o1 = SSD_W + g * NS
            o2 = SSD_W + 2 * NS + g * NS
            ext2_scr[0:Q, o1:o1 + NS] = dbs[g] * dsilu[:, o1:o1 + NS]
            ext2_scr[0:Q, o2:o2 + NS] = dcs[g] * dsilu[:, o2:o2 + NS]
        dconv = ext2_scr[0:Q, :]
        dcb_ref[...] += jnp.sum(dconv, axis=0, keepdims=True)
        dxr = jnp.zeros((Q, XBC), F32)
        for k in range(4):
            dcw_ref[k:k + 1, :] += jnp.sum(dconv * ext_scr[pl.ds(5 + k, Q), :], axis=0, keepdims=True)
            dxr = dxr + cw_ref[k:k + 1, :] * ext2_scr[pl.ds(3 - k, Q), :]
        dxr_ref[...] = dxr.astype(BF16)
        ext2_scr[Q:Q + 8, :] = ext2_scr[0:8, :]

    def row(w):
        return pl.BlockSpec((Q, w), lambda i: (NC - 1 - i, 0))

    def full(shape):
        return pl.BlockSpec(shape, lambda i: (0,) * len(shape))

    prev = pl.BlockSpec((Q, XBC), lambda i: (jnp.maximum(NC - 2 - i, 0), 0))
    return pl.pallas_call(
        body, name=name, grid=(NC,),
        in_specs=[row(SSD_W), row(XBC), prev, row(128), row(SSD_W),
                  pl.BlockSpec((1, 2, NS, 512), lambda i: (NC - 1 - i, 0, 0, 0)), row(SSD_W),
                  full((4, XBC)), full((1, XBC)), full((1, 128)), full((1, 128)), full((1, SSD_W)),
                  full((1, SSD_W)), full((128, SSD_W)), full((128, NH * 128)), full((SSD_W, 128)), full((Q, Q)),
                  full((Q, Q)), full((128, 128))],
        out_specs=[row(SSD_W), row(XBC), row(128), full((8, XBC)), full((1, XBC)), full((1, 128)), full((1, 128)),
                   full((1, 128)), full((1, SSD_W))],
        out_shape=[jax.ShapeDtypeStruct((L, SSD_W), BF16), jax.ShapeDtypeStruct((L, XBC), BF16),
                   jax.ShapeDtypeStruct((L, 128), BF16), jax.ShapeDtypeStruct((8, XBC), F32),
                   jax.ShapeDtypeStruct((1, XBC), F32), jax.ShapeDtypeStruct((1, 128), F32),
                   jax.ShapeDtypeStruct((1, 128), F32), jax.ShapeDtypeStruct((1, 128), F32),
                   jax.ShapeDtypeStruct((1, SSD_W), F32)],
        scratch_shapes=[pltpu.VMEM((2, NS, 512), F32), pltpu.VMEM((Q + 8, XBC), F32), pltpu.VMEM((Q + 8, XBC), F32),
                        pltpu.VMEM((128, Q), F32), pltpu.VMEM((Q, 128), F32), pltpu.VMEM((Q, SSD_W), F32),
                        pltpu.VMEM((128, Q), F32), pltpu.VMEM((Q, SSD_W), F32), pltpu.VMEM((Q, SSD_W), F32),
                        pltpu.VMEM((Q, SSD_W), F32), pltpu.VMEM((Q, SSD_W), F32)],
        compiler_params=_cp(("arbitrary",)))(z, xr, xr, dtr, ypre, st, dya, cw, cb, dtb, alog, dxp, nw,
                                             e, e2, et, tril, triu, eye)


def _my_pos():
    return lax.axis_index("x"), lax.axis_index("y"), lax.axis_index("c")


def _sib_exchange(buf, name):
    def body(src, dst, send_sem, recv_sem):
        x, y, c = _my_pos()
        cp = pltpu.make_async_remote_copy(src_ref=src, dst_ref=dst, send_sem=send_sem, recv_sem=recv_sem,
                                          device_id=(x, y, 1 - c), device_id_type=MESH)
        cp.start()
        cp.wait()

    return pl.pallas_call(
        body, name=name, in_specs=[ANY], out_specs=ANY, out_shape=jax.ShapeDtypeStruct(buf.shape, buf.dtype),
        scratch_shapes=[pltpu.SemaphoreType.DMA, pltpu.SemaphoreType.DMA])(buf)


CHIP_REL = ((1, 0), (0, 1), (1, 1))
CHIP_XOR = (2, 1, 3)


def _chip_exchange(src, name, bcast):
    shape = src.shape if bcast else src.shape[1:]

    def body(src_ref, dst_ref, send_sems, recv_sems):
        x, y, c = _my_pos()
        cps = []
        for k, (dx, dy) in enumerate(CHIP_REL):
            tx = 1 - x if dx else x
            ty = 1 - y if dy else y
            cps.append(pltpu.make_async_remote_copy(
                src_ref=src_ref if bcast else src_ref.at[k], dst_ref=dst_ref.at[k],
                send_sem=send_sems.at[k], recv_sem=recv_sems.at[k], device_id=(tx, ty, c), device_id_type=MESH))
        for cp in cps:
            cp.start()
        for cp in cps:
            cp.wait()

    return pl.pallas_call(
        body, name=name, in_specs=[ANY], out_specs=ANY, out_shape=jax.ShapeDtypeStruct((3,) + shape, src.dtype),
        scratch_shapes=[pltpu.SemaphoreType.DMA((3,)), pltpu.SemaphoreType.DMA((3,))])(src)


def _allreduce_small(buf, name):
    rows = buf.shape[0]

    def body(src_ref, out_ref, gat_ref, send_sems, recv_sems):
        x, y, c = _my_pos()
        me = 4 * x + 2 * y + c
        gat_ref[me] = src_ref[...]
        cps = []
        for r in range(1, N_DEV):
            tx = 1 - x if (r >> 2) & 1 else x
            ty = 1 - y if (r >> 1) & 1 else y
            tc = 1 - c if r & 1 else c
            cps.append(pltpu.make_async_remote_copy(
                src_ref=src_ref, dst_ref=gat_ref.at[me], send_sem=send_sems.at[r - 1], recv_sem=recv_sems.at[r - 1],
                device_id=(tx, ty, tc), device_id_type=MESH))
        for cp in cps:
            cp.start()
        for cp in cps:
            cp.wait()
        acc = gat_ref[0]
        for k in range(1, N_DEV):
            acc = acc + gat_ref[k]
        out_ref[...] = acc

    return pl.pallas_call(
        body, name=name, in_specs=[VMEM_SPEC], out_specs=VMEM_SPEC, out_shape=jax.ShapeDtypeStruct((rows, 128), F32),
        scratch_shapes=[pltpu.VMEM((N_DEV, rows, 128), F32), pltpu.SemaphoreType.DMA((N_DEV - 1,)),
                        pltpu.SemaphoreType.DMA((N_DEV - 1,))],
        compiler_params=_cp())(buf)


def _gather_weight(w_shard, axis, tag):
    x, y, c = _my_pos()
    s = 2 * x + y
    wb = w_shard.astype(BF16)
    mine = lax.dynamic_slice_in_dim(wb, 2 * c, 2, axis=0)
    recv = _chip_exchange(mine, "ag_chip_" + tag, bcast=True)
    sib = _sib_exchange(recv, "ag_sib_" + tag)
    lo = jnp.where(c == 0, recv, sib)
    hi = jnp.where(c == 0, sib, recv)
    others = jnp.concatenate([lo, hi], axis=1)
    full = jnp.zeros((4,) + wb.shape, BF16)
    full = lax.dynamic_update_slice_in_dim(full, wb[None], s, axis=0)
    for k in range(3):
        full = lax.dynamic_update_slice_in_dim(full, others[k][None], s ^ CHIP_XOR[k], axis=0)
    if axis == 1:
        return jnp.transpose(full, (1, 0, 2, 3)).reshape(DEPTH, 4 * wb.shape[1], wb.shape[2])
    return jnp.transpose(full, (1, 2, 0, 3)).reshape(DEPTH, wb.shape[1], 4 * wb.shape[2])


def _scatter_grad(gfull, axis, tag):
    x, y, c = _my_pos()
    s = 2 * x + y
    mine = lax.dynamic_slice_in_dim(gfull, 2 * c, 2, axis=0)
    other = lax.dynamic_slice_in_dim(gfull, 2 * (1 - c), 2, axis=0).astype(BF16)
    got = _sib_exchange(other, "rs_pair_" + tag)
    shp = mine.shape
    pair = _add_n([mine.reshape(-1, shp[-1]), got.reshape(-1, shp[-1])], "rs_add2_" + tag).reshape(shp)
    size = shp[axis] // 4
    own = lax.dynamic_slice_in_dim(pair, s * size, size, axis=axis)
    send = jnp.stack([lax.dynamic_slice_in_dim(pair, (s ^ CHIP_XOR[k]) * size, size, axis=axis).astype(BF16)
                      for k in range(3)])
    recv = _chip_exchange(send, "rs_chip_" + tag, bcast=False)
    oshp = own.shape
    red = _add_n([own.reshape(-1, oshp[-1])] + [recv[k].reshape(-1, oshp[-1]) for k in range(3)],
                 "rs_add4_" + tag).reshape(oshp)
    sib = _sib_exchange(red, "rs_sib_" + tag)
    lo = jnp.where(c == 0, red, sib)
    hi = jnp.where(c == 0, sib, red)
    return jnp.concatenate([lo, hi], axis=0)


SMALL = (("norm_mix_w", (D,)), ("ssd_conv_w", (4, XBC)), ("ssd_conv_b", (XBC,)), ("ssd_dt_bias", (NH,)),
         ("ssd_a_log", (NH,)), ("ssd_d", (NH,)), ("ssd_norm_w", (SSD_W,)), ("q_norm_w", (HD,)),
         ("k_norm_w", (HD,)), ("attn_sinks", (NQH,)), ("cm_dw_w", (CMK, CMC)), ("cm_dw_b", (CMC,)),
         ("cm_ln_w", (CMC,)), ("cm_ln_b", (CMC,)), ("norm_mlp_w", (D,)))
SHARDED_SMALL = ("ssd_conv_w", "cm_dw_w")


def _seg_len(shape):
    n = 1
    for d in shape:
        n *= d
    return -(-n // 128) * 128


def _pack_small(vals, names):
    parts = []
    for name, shape in SMALL:
        if name not in names:
            continue
        v = vals[name].reshape(DEPTH, -1)
        pad = _seg_len(shape) - v.shape[1]
        parts.append(jnp.pad(v, ((0, 0), (0, pad))))
    flat = jnp.concatenate(parts, axis=1)
    return flat.reshape(-1, 128)


def _unpack_small(buf, names):
    flat = buf.reshape(DEPTH, -1)
    out = {}
    off = 0
    for name, shape in SMALL:
        if name not in names:
            continue
        n = 1
        for d in shape:
            n *= d
        out[name] = flat[:, off:off + n].reshape((DEPTH,) + shape)
        off += _seg_len(shape)
    return out


def _rope_tables():
    inv = 10000.0 ** (-jnp.arange(0, HD, 2, dtype=F32) / HD)
    ang = jnp.arange(L, dtype=F32)[:, None] * inv[None, :]
    return jnp.cos(ang), jnp.sin(ang)


def _pad128(v):
    return jnp.pad(v, (0, 128 - v.shape[0]))[None, :]


def _heads(t, nh, w):
    return jnp.transpose(t.reshape(L, nh, w), (1, 0, 2))


def _unheads(t):
    nh, _, w = t.shape
    return jnp.transpose(t, (1, 0, 2)).reshape(L, nh * w)


def kernel(x, norm_mix_w, w_in, ssd_conv_w, ssd_conv_b, ssd_dt_bias, ssd_a_log, ssd_d, ssd_norm_w, q_norm_w, k_norm_w, attn_sinks, cm_dw_w, cm_dw_b, cm_ln_w, cm_ln_b, w_out, norm_mlp_w, w_mlp_up, w_mlp_down, loss_target, m_norm_mix_w, m_w_in, m_ssd_conv_w, m_ssd_conv_b, m_ssd_dt_bias, m_ssd_a_log, m_ssd_d, m_ssd_norm_w, m_q_norm_w, m_k_norm_w, m_attn_sinks, m_cm_dw_w, m_cm_dw_b, m_cm_ln_w, m_cm_ln_b, m_w_out, m_norm_mlp_w, m_w_mlp_up, m_w_mlp_down, v_norm_mix_w, v_w_in, v_ssd_conv_w, v_ssd_conv_b, v_ssd_dt_bias, v_ssd_a_log, v_ssd_d, v_ssd_norm_w, v_q_norm_w, v_k_norm_w, v_attn_sinks, v_cm_dw_w, v_cm_dw_b, v_cm_ln_w, v_cm_ln_b, v_w_out, v_norm_mlp_w, v_w_mlp_up, v_w_mlp_down):
    px, py, pc = _my_pos()
    shard = 2 * px + py
    consts = _ssd_consts()
    cos, sin = _rope_tables()

    win_full = _gather_weight(w_in, 2, "win")
    wout_full = _gather_weight(w_out, 1, "wout")
    wup_full = _gather_weight(w_mlp_up, 2, "wup")
    wdn_full = _gather_weight(w_mlp_down, 1, "wdn")
    zc = jnp.zeros((DEPTH, 4, XBC), F32)
    zc = lax.dynamic_update_slice_in_dim(zc, ssd_conv_w, shard * (XBC // 4), axis=2)
    zd = jnp.zeros((DEPTH, CMK, CMC), F32)
    zd = lax.dynamic_update_slice_in_dim(zd, cm_dw_w, shard * (CMC // 4), axis=2)
    half = jnp.where(pc == 0, 1.0, 0.0).astype(F32)
    gw = _allreduce_small(_pack_small({"ssd_conv_w": zc * half, "cm_dw_w": zd * half}, SHARDED_SMALL), "ag_small")
    gw = _unpack_small(gw, SHARDED_SMALL)
    conv_w_full, dw_w_full = gw["ssd_conv_w"], gw["cm_dw_w"]

    o_z, o_x, o_dt, o_q, o_k, o_v, o_g = 0, 1024, 2560, 2576, 3088, 3216, 3344

    xcur = x[0]
    saved = []
    for l in range(DEPTH):
        wi = win_full[l]
        w_z = wi[:, o_z:o_x]
        w_x = wi[:, o_x:o_dt]
        w_dt = jnp.pad(wi[:, o_dt:o_q], ((0, 0), (0, 128 - NH)))
        w_qkv = wi[:, o_q:o_g]
        w_glu = wi[:, o_g:]
        h = _rms_fwd(xcur, norm_mix_w[l][None], "rms_mix_fwd")
        z = _mm(h, w_z, "nn", "in_z")
        xr = _mm(h, w_x, "nn", "in_xbc")
        dtr = _mm(h, w_dt, "nn", "in_dt", tn=128)
        qkv = _mm(h, w_qkv, "nn", "in_qkv", tn=384)
        glu = _mm(h, w_glu, "nn", "in_glu")
        alog = _pad128(ssd_a_log[l])
        dtb = _pad128(ssd_dt_bias[l])
        dxp = jnp.repeat(ssd_d[l], HP)[None, :]
        ya, ypre, st = _ssd_fwd(z, xr, dtr, conv_w_full[l], ssd_conv_b[l][None], dtb, alog, dxp,
                                ssd_norm_w[l][None], consts, "ssd_fwd")
        q3 = _heads(qkv[:, :ATT_W], NQH, HD)
        k3 = _heads(qkv[:, ATT_W:ATT_W + NKV * HD], NKV, HD)
        v3 = _heads(qkv[:, ATT_W + NKV * HD:], NKV, HD)
        qa, qb, ka, kb = q3[..., :HH], q3[..., HH:], k3[..., :HH], k3[..., HH:]
        qw, kw = q_norm_w[l], k_norm_w[l]
        sinks = jnp.broadcast_to(attn_sinks[l][:, None, None], (NQH, 1, 128))
        attn_args = (qa, qb, ka, kb, v3, cos, sin, qw[None, :HH], qw[None, HH:], kw[None, :HH], kw[None, HH:], sinks)
        yb = _unheads(_attn_fwd(*attn_args, "attn_fwd")).astype(BF16)
        ga, gg = glu[:, :CMC], glu[:, CMC:]
        conf_args = (ga, gg, dw_w_full[l], cm_dw_b[l][None], cm_ln_w[l][None], cm_ln_b[l][None])
        yc = _conf_fwd(*conf_args, "conf_fwd")
        ycat = jnp.concatenate([ya, yb, yc], axis=1)
        x1 = _mm(ycat, wout_full[l], "nn", "out_proj", add=xcur)
        hm = _rms_fwd(x1, norm_mlp_w[l][None], "rms_mlp_fwd")
        a_up = _mm(hm, wup_full[l], "nn", "mlp_up")
        r_up = _relu2_fwd(a_up, "relu2_fwd")
        x2 = _mm(r_up, wdn_full[l], "nn", "mlp_down", add=x1)
        saved.append(dict(x=xcur, h=h, z=z, xr=xr, dtr=dtr, ypre=ypre, st=st, attn_args=attn_args,
                          conf_args=conf_args, ycat=ycat, x1=x1, hm=hm, a_up=a_up, r_up=r_up,
                          ssd_p=(conv_w_full[l], ssd_conv_b[l][None], dtb, alog, dxp, ssd_norm_w[l][None]),
                          w_pad=jnp.concatenate([wi[:, :o_q], jnp.zeros((D, 128 - NH), BF16), wi[:, o_q:]], axis=1)))
        xcur = x2

    lsum, dx, dxb = _loss_bwd(xcur, loss_target[0], "loss")

    gbig = {"w_in": [], "w_out": [], "w_mlp_up": [], "w_mlp_down": []}
    gsm = {name: [] for name, _ in SMALL}
    for l in reversed(range(DEPTH)):
        sv = saved[l]
        dr = _mm(dxb, wdn_full[l], "nt", "mlp_down_dx")
        gbig["w_mlp_down"].append(_mm(sv["r_up"], dxb, "tn", "mlp_down_dw"))
        da = _relu2_bwd(dr, sv["a_up"], "relu2_bwd")
        gbig["w_mlp_up"].append(_mm(sv["hm"], da, "tn", "mlp_up_dw"))
        dhm = _mm(da, wup_full[l], "nt", "mlp_up_dx")
        dx1, dx1b, dnw = _rms_bwd(sv["x1"], norm_mlp_w[l][None], dhm, dx, "rms_mlp_bwd")
        gsm["norm_mlp_w"].append(dnw[0])
        dy = _mm(dx1b, wout_full[l], "nt", "out_proj_dx")
        gbig["w_out"].append(_mm(sv["ycat"], dx1b, "tn", "out_proj_dw"))
        da_c, dg_c, dww, dwb, dlw, dlb = _conf_bwd(*sv["conf_args"], dy[:, SSD_W + ATT_W:], "conf_bwd")
        gsm["cm_dw_w"].append(dww[:CMK])
        gsm["cm_dw_b"].append(dwb[0])
        gsm["cm_ln_w"].append(dlw[0])
        gsm["cm_ln_b"].append(dlb[0])
        do3 = _heads(dy[:, SSD_W:SSD_W + ATT_W], NQH, HD)
        (dqa, dqb, dka, dkb, dv3, dqwa, dqwb, dkwa, dkwb, dsk) = _attn_bwd(*sv["attn_args"], do3, "attn_bwd")
        dq = _unheads(jnp.concatenate([dqa, dqb], axis=-1))
        dk = _unheads(jnp.concatenate([dka, dkb], axis=-1))
        dv = _unheads(dv3)
        gsm["q_norm_w"].append(jnp.concatenate([dqwa[0], dqwb[0]]))
        gsm["k_norm_w"].append(jnp.concatenate([dkwa[0], dkwb[0]]))
        gsm["attn_sinks"].append(dsk[:, 0, 0])
        (dz, dxr, ddtr, dcw, dcb, ddtb, dalog, ddd, dnsw) = _ssd_bwd(
            sv["z"], sv["xr"], sv["dtr"], sv["ypre"], sv["st"], dy[:, :SSD_W], *sv["ssd_p"], consts, "ssd_bwd")
        gsm["ssd_conv_w"].append(dcw[:4])
        gsm["ssd_conv_b"].append(dcb[0])
        gsm["ssd_dt_bias"].append(ddtb[0, :NH])
        gsm["ssd_a_log"].append(dalog[0, :NH])
        gsm["ssd_d"].append(ddd[0, :NH])
        gsm["ssd_norm_w"].append(dnsw[0])
        du = jnp.concatenate([dz, dxr, ddtr, dq.astype(BF16), dk.astype(BF16), dv.astype(BF16), da_c, dg_c], axis=1)
        dwin = _mm(sv["h"], du, "tn", "in_dw", tn=640)
        gbig["w_in"].append(jnp.concatenate([dwin[:, :o_q], dwin[:, o_q + 128 - NH:]], axis=1))
        dh = _mm(du, sv["w_pad"], "nt", "in_dx")
        dx, dxb, dnm = _rms_bwd(sv["x"], norm_mix_w[l][None], dh, dx1, "rms_mix_bwd")
        gsm["norm_mix_w"].append(dnm[0])

    gsm = {k: jnp.stack(v[::-1]) for k, v in gsm.items()}
    packed = _pack_small(gsm, [n for n, _ in SMALL])
    packed = jnp.concatenate([packed, lsum], axis=0)
    red = _allreduce_small(packed, "ar_small")
    loss = 0.5 * red[-8, 0] / D
    gsm = _unpack_small(red[:-8], [n for n, _ in SMALL])
    gsm["ssd_conv_w"] = lax.dynamic_slice_in_dim(gsm["ssd_conv_w"], shard * (XBC // 4), XBC // 4, axis=2)
    gsm["cm_dw_w"] = lax.dynamic_slice_in_dim(gsm["cm_dw_w"], shard * (CMC // 4), CMC // 4, axis=2)
    grads = dict(gsm)
    grads["w_in"] = _scatter_grad(jnp.stack(gbig["w_in"][::-1]), 2, "win")
    grads["w_out"] = _scatter_grad(jnp.stack(gbig["w_out"][::-1]), 1, "wout")
    grads["w_mlp_up"] = _scatter_grad(jnp.stack(gbig["w_mlp_up"][::-1]), 2, "wup")
    grads["w_mlp_down"] = _scatter_grad(jnp.stack(gbig["w_mlp_down"][::-1]), 1, "wdn")

    loc = locals()
    names = ["norm_mix_w", "w_in", "ssd_conv_w", "ssd_conv_b", "ssd_dt_bias", "ssd_a_log", "ssd_d", "ssd_norm_w",
             "q_norm_w", "k_norm_w", "attn_sinks", "cm_dw_w", "cm_dw_b", "cm_ln_w", "cm_ln_b", "w_out", "norm_mlp_w",
             "w_mlp_up", "w_mlp_down"]
    weights = {n: loc[n] for n in names}
    moms = {n: loc["m_" + n] for n in names}
    vars_ = {n: loc["v_" + n] for n in names}
    delta, new_m, new_v = {}, {}, {}
    packed_names = [n for n, _ in SMALL if n not in SHARDED_SMALL]
    pw = _pack_small(weights, packed_names)
    pg = _pack_small(grads, packed_names)
    pm = _pack_small(moms, packed_names)
    pv = _pack_small(vars_, packed_names)
    pd, pmn, pvn = _adamw(pw, pg, pm, pv, "adamw_small")
    for dst, buf in ((delta, pd), (new_m, pmn), (new_v, pvn)):
        dst.update(_unpack_small(buf, packed_names))
    for n in ("w_in", "w_out", "w_mlp_up", "w_mlp_down", "ssd_conv_w", "cm_dw_w"):
        shp = weights[n].shape
        flat = lambda t: t.reshape(-1, shp[-1])
        d_, m_, v_ = _adamw(flat(weights[n]), flat(grads[n]), flat(moms[n]), flat(vars_[n]), "adamw_" + n)
        delta[n], new_m[n], new_v[n] = d_.reshape(shp), m_.reshape(shp), v_.reshape(shp)

    return (loss, dx[None], *[grads[n] for n in names], *[delta[n] for n in names],
            *[new_m[n] for n in names], *[new_v[n] for n in names])
```

```python
import functools
import math

import jax
import jax.numpy as jnp
from jax import lax
from jax.experimental import pallas as pl
from jax.experimental.pallas import tpu as pltpu

F32 = jnp.float32
BF16 = jnp.bfloat16
MESH = pl.DeviceIdType.MESH
ANY = pl.BlockSpec(memory_space=pl.ANY)
VMEM_SPEC = pl.BlockSpec(memory_space=pltpu.VMEM)

D = 1024
L = 2048
DEPTH = 4
SSD_W = 1024
XBC = 1536
NH = 16
HP = 64
NS = 128
Q = 128
NC = L // Q
ATT_W = 512
NQH = 8
NKV = 2
HD = 64
HH = HD // 2
CMC = 512
CMK = 31
DFF = 4096
N_IN = 4368
N_PAD = 4480
RMS_EPS = 1e-6
LN_EPS = 1e-5
NEG = -1e30
LR, B1, B2, EPS_A, WD, STEP = 0.001, 0.9, 0.999, 1e-8, 0.01, 10
VMEM_LIMIT = 56 * 1024 * 1024
N_DEV = 8


def _cp(sem=None):
    kw = dict(vmem_limit_bytes=VMEM_LIMIT)
    if sem is not None:
        kw["dimension_semantics"] = sem
    return pltpu.CompilerParams(**kw)


def _dg(a, b, ca, cb):
    return lax.dot_general(a, b, (((ca,), (cb,)), ((), ())), preferred_element_type=F32)


def _split3(x):
    hi = x.astype(BF16)
    r = x - hi.astype(F32)
    mid = r.astype(BF16)
    lo = (r - mid.astype(F32)).astype(BF16)
    return hi, mid, lo


def _xdot_l(x, m, ca=1, cb=0):
    hi, mid, lo = _split3(x)
    return _dg(hi, m, ca, cb) + _dg(mid, m, ca, cb) + _dg(lo, m, ca, cb)


def _xdot_r(m, x, ca=1, cb=0):
    hi, mid, lo = _split3(x)
    return _dg(m, hi, ca, cb) + _dg(m, mid, ca, cb) + _dg(m, lo, ca, cb)


def _rowdot(v, m):
    return _xdot_l(jnp.broadcast_to(v, (8, v.shape[1])), m)[0:1]


def _sigmoid(x):
    return 1.0 / (1.0 + jnp.exp(-x))


def _softplus(x):
    e = jnp.exp(-jnp.abs(x))
    u = 1.0 + e
    l1p = jnp.where(u == 1.0, e, jnp.log(u) * (e / jnp.where(u == 1.0, 1.0, u - 1.0)))
    return jnp.maximum(x, 0.0) + l1p


def _mm(a, b, mode, name, add=None, out_dtype=F32, tm=512, tn=512):
    if mode == "nn":
        m, k = a.shape
        n = b.shape[1]
        a_spec = pl.BlockSpec((tm, k), lambda i, j: (i, 0))
        b_spec = pl.BlockSpec((k, tn), lambda i, j: (0, j))
        ca, cb = 1, 0
    elif mode == "nt":
        m, k = a.shape
        n = b.shape[0]
        a_spec = pl.BlockSpec((tm, k), lambda i, j: (i, 0))
        b_spec = pl.BlockSpec((tn, k), lambda i, j: (j, 0))
        ca, cb = 1, 1
    else:
        k, m = a.shape
        n = b.shape[1]
        a_spec = pl.BlockSpec((k, tm), lambda i, j: (0, i))
        b_spec = pl.BlockSpec((k, tn), lambda i, j: (0, j))
        ca, cb = 0, 0
    assert m % tm == 0 and n % tn == 0, (m, n, tm, tn)
    o_spec = pl.BlockSpec((tm, tn), lambda i, j: (i, j))

    if add is None:
        def body(a_ref, b_ref, o_ref):
            o_ref[...] = _dg(a_ref[...], b_ref[...], ca, cb).astype(o_ref.dtype)
        ins, specs = (a, b), [a_spec, b_spec]
    else:
        def body(a_ref, b_ref, c_ref, o_ref):
            o_ref[...] = (_dg(a_ref[...], b_ref[...], ca, cb) + c_ref[...]).astype(o_ref.dtype)
        ins, specs = (a, b, add), [a_spec, b_spec, o_spec]

    return pl.pallas_call(
        body, name=name, grid=(m // tm, n // tn), in_specs=specs, out_specs=o_spec,
        out_shape=jax.ShapeDtypeStruct((m, n), out_dtype),
        compiler_params=_cp(("parallel", "parallel")))(*ins)


def _mm_wl(a, b, layer, mode, name, add=None, tm=512, tn=512):
    m, k = a.shape
    a_spec = pl.BlockSpec((tm, k), lambda i, j: (i, 0))
    if mode == "nn":
        n = b.shape[2]
        b_spec = pl.BlockSpec((1, k, tn), lambda i, j: (layer, 0, j))
        cb = 0
    else:
        n = b.shape[1]
        b_spec = pl.BlockSpec((1, tn, k), lambda i, j: (layer, j, 0))
        cb = 1
    o_spec = pl.BlockSpec((tm, tn), lambda i, j: (i, j))
    if add is None:
        def body(a_ref, b_ref, o_ref):
            o_ref[...] = _dg(a_ref[...], b_ref[0], 1, cb)
        ins, specs = (a, b), [a_spec, b_spec]
    else:
        def body(a_ref, b_ref, c_ref, o_ref):
            o_ref[...] = _dg(a_ref[...], b_ref[0], 1, cb) + c_ref[...]
        ins, specs = (a, b, add), [a_spec, b_spec, o_spec]
    return pl.pallas_call(
        body, name=name, grid=(m // tm, n // tn), in_specs=specs, out_specs=o_spec,
        out_shape=jax.ShapeDtypeStruct((m, n), F32), compiler_params=_cp(("parallel", "parallel")))(*ins)


def _mm_up(a, b, layer, mode, name, tm=512, tn=512):
    m = a.shape[0]
    cs = DFF // 4
    o_spec = pl.BlockSpec((tm, tn), lambda i, j: (i, j))
    if mode == "nn":
        n = DFF
        per = cs // tn
        a_spec = pl.BlockSpec((tm, D), lambda i, j: (i, 0))
        b_spec = pl.BlockSpec((1, 1, D, tn), lambda i, j: (layer, j // per, 0, j % per))

        def body(a_ref, b_ref, o_ref):
            o_ref[...] = _dg(a_ref[...], b_ref[0, 0], 1, 0)
    else:
        n = D
        a_spec = pl.BlockSpec((tm, DFF), lambda i, j: (i, 0))
        b_spec = pl.BlockSpec((1, 4, tn, cs), lambda i, j: (layer, 0, j, 0))

        def body(a_ref, b_ref, o_ref):
            acc = _dg(a_ref[:, 0:cs], b_ref[0, 0], 1, 1)
            for s in range(1, 4):
                acc = acc + _dg(a_ref[:, s * cs:(s + 1) * cs], b_ref[0, s], 1, 1)
            o_ref[...] = acc
    return pl.pallas_call(
        body, name=name, grid=(m // tm, n // tn), in_specs=[a_spec, b_spec], out_specs=o_spec,
        out_shape=jax.ShapeDtypeStruct((m, n), F32), compiler_params=_cp(("parallel", "parallel")))(a, b)


def _mm_dw(a, b, name, col_shards=False, tm=512, tn=512):
    k, m = a.shape
    n = b.shape[1]
    a_spec = pl.BlockSpec((k, tm), lambda i, j: (0, i))
    b_spec = pl.BlockSpec((k, tn), lambda i, j: (0, j))
    if col_shards:
        per = (n // 4) // tn
        o_spec = pl.BlockSpec((1, tm, tn), lambda i, j: (j // per, i, j % per))
        shape = (4, m, n // 4)
    else:
        o_spec = pl.BlockSpec((tm, tn), lambda i, j: (i, j))
        shape = (m, n)

    def body(a_ref, b_ref, o_ref, ob_ref):
        acc = _dg(a_ref[...], b_ref[...], 0, 0).reshape(o_ref.shape)
        o_ref[...] = acc
        ob_ref[...] = acc.astype(BF16)

    return pl.pallas_call(
        body, name=name, grid=(m // tm, n // tn), in_specs=[a_spec, b_spec], out_specs=[o_spec, o_spec],
        out_shape=[jax.ShapeDtypeStruct(shape, F32), jax.ShapeDtypeStruct(shape, BF16)],
        compiler_params=_cp(("parallel", "parallel")))(a, b)


TR = 256


def _rms_fwd(x, w, name):
    def body(x_ref, w_ref, o_ref):
        xv = x_ref[...]
        r = lax.rsqrt(jnp.mean(xv * xv, axis=-1, keepdims=True) + RMS_EPS)
        o_ref[...] = (xv * r * w_ref[...]).astype(BF16)

    return pl.pallas_call(
        body, name=name, grid=(L // TR,),
        in_specs=[pl.BlockSpec((TR, D), lambda i: (i, 0)), pl.BlockSpec((1, D), lambda i: (0, 0))],
        out_specs=pl.BlockSpec((TR, D), lambda i: (i, 0)),
        out_shape=jax.ShapeDtypeStruct((L, D), BF16), compiler_params=_cp(("parallel",)))(x, w)


def _rms_bwd(x, w, dh, dres, name):
    def body(x_ref, w_ref, dh_ref, dr_ref, dx_ref, dxb_ref, dw_ref):
        xv = x_ref[...]
        r = lax.rsqrt(jnp.mean(xv * xv, axis=-1, keepdims=True) + RMS_EPS)
        n = xv * r
        dhv = dh_ref[...]
        g = dhv * w_ref[...]
        dx = dr_ref[...] + r * (g - n * jnp.mean(g * n, axis=-1, keepdims=True))
        dx_ref[...] = dx
        dxb_ref[...] = dx.astype(BF16)

        @pl.when(pl.program_id(0) == 0)
        def _():
            dw_ref[...] = jnp.zeros_like(dw_ref)
        dw_ref[...] += jnp.sum(dhv * n, axis=0, keepdims=True)

    row = pl.BlockSpec((TR, D), lambda i: (i, 0))
    vec = pl.BlockSpec((1, D), lambda i: (0, 0))
    return pl.pallas_call(
        body, name=name, grid=(L // TR,), in_specs=[row, vec, row, row], out_specs=[row, row, vec],
        out_shape=[jax.ShapeDtypeStruct((L, D), F32), jax.ShapeDtypeStruct((L, D), BF16),
                   jax.ShapeDtypeStruct((1, D), F32)],
        compiler_params=_cp(("arbitrary",)))(x, w, dh, dres)


def _relu2_fwd(a, name):
    def body(a_ref, o_ref):
        r = jnp.maximum(a_ref[...], 0.0)
        o_ref[...] = (r * r).astype(BF16)

    row = pl.BlockSpec((TR, DFF), lambda i: (i, 0))
    return pl.pallas_call(body, name=name, grid=(L // TR,), in_specs=[row], out_specs=row,
                          out_shape=jax.ShapeDtypeStruct((L, DFF), BF16),
                          compiler_params=_cp(("parallel",)))(a)


def _relu2_bwd(dr, a, name):
    def body(dr_ref, a_ref, o_ref):
        o_ref[...] = (dr_ref[...] * 2.0 * jnp.maximum(a_ref[...], 0.0)).astype(BF16)

    row = pl.BlockSpec((TR, DFF), lambda i: (i, 0))
    return pl.pallas_call(body, name=name, grid=(L // TR,), in_specs=[row, row], out_specs=row,
                          out_shape=jax.ShapeDtypeStruct((L, DFF), BF16),
                          compiler_params=_cp(("parallel",)))(dr, a)


def _loss_bwd(y, t, name):
    def body(y_ref, t_ref, l_ref, d_ref, db_ref):
        e = y_ref[...] - t_ref[...]
        d = e * (1.0 / D)
        d_ref[...] = d
        db_ref[...] = d.astype(BF16)

        @pl.when(pl.program_id(0) == 0)
        def _():
            l_ref[...] = jnp.zeros_like(l_ref)
        s = jnp.sum(jnp.sum(e * e, axis=-1, keepdims=True), axis=0, keepdims=True)
        l_ref[...] += jnp.broadcast_to(s, l_ref.shape)

    row = pl.BlockSpec((TR, D), lambda i: (i, 0))
    tile = pl.BlockSpec((8, 128), lambda i: (0, 0))
    return pl.pallas_call(
        body, name=name, grid=(L // TR,), in_specs=[row, row], out_specs=[tile, row, row],
        out_shape=[jax.ShapeDtypeStruct((8, 128), F32), jax.ShapeDtypeStruct((L, D), F32),
                   jax.ShapeDtypeStruct((L, D), BF16)],
        compiler_params=_cp(("arbitrary",)))(y, t)


def _adamw(w, g, m, v, name):
    rows, cols = w.shape
    tr = rows
    for cand in (512, 256, 128, 64, 32, 16, 8):
        if rows % cand == 0 and cand * cols * 4 <= 2 * 1024 * 1024:
            tr = cand
            break
    c1 = 1.0 / (1.0 - B1 ** STEP)
    c2 = 1.0 / (1.0 - B2 ** STEP)

    def body(w_ref, g_ref, m_ref, v_ref, d_ref, mo_ref, vo_ref):
        gv = g_ref[...]
        mn = B1 * m_ref[...] + (1.0 - B1) * gv
        vn = B2 * v_ref[...] + (1.0 - B2) * (gv * gv)
        mo_ref[...] = mn
        vo_ref[...] = vn
        d_ref[...] = -LR * ((mn * c1) / (jnp.sqrt(vn * c2) + EPS_A) + WD * w_ref[...])

    blk = pl.BlockSpec((tr, cols), lambda i: (i, 0))
    shp = jax.ShapeDtypeStruct((rows, cols), F32)
    return pl.pallas_call(body, name=name, grid=(rows // tr,), in_specs=[blk] * 4, out_specs=[blk] * 3,
                          out_shape=[shp] * 3, compiler_params=_cp(("parallel",)))(w, g, m, v)


CT = 256
CPAD = 32


U_X, U_Z, U_A, U_G, U_Q, U_K, U_V, U_DT = 0, 1536, 2560, 3072, 3584, 4096, 4224, 4352


def _cfull(shape):
    return pl.BlockSpec(shape, lambda i: (0,) * len(shape))


def _conf_in_specs():
    return [pl.BlockSpec((L, CMC), lambda i: (0, U_A // CMC)), pl.BlockSpec((L, CMC), lambda i: (0, U_G // CMC)),
            _cfull((CMK, CMC)), _cfull((1, CMC)), _cfull((1, CMC)), _cfull((1, CMC))]


def _conf_fwd(u, w, b, lw, lb, name):
    def body(a_ref, g_ref, w_ref, b_ref, lw_ref, lb_ref, o_ref, hp_ref, win_ref):
        hp_ref[0:CPAD, :] = jnp.zeros((CPAD, CMC), F32)
        hp_ref[CPAD:, :] = a_ref[...] * _sigmoid(g_ref[...])

        def tile(i, carry):
            base = pl.multiple_of(i * CT, CT)
            win_ref[...] = hp_ref[pl.ds(base, CT + CPAD), :]
            c = jnp.broadcast_to(b_ref[...], (CT, CMC))
            for k in range(CMK):
                c = c + w_ref[k:k + 1, :] * win_ref[pl.ds(2 + k, CT), :]
            mu = jnp.mean(c, axis=-1, keepdims=True)
            cc = c - mu
            var = jnp.mean(cc * cc, axis=-1, keepdims=True)
            l = cc * lax.rsqrt(var + LN_EPS) * lw_ref[...] + lb_ref[...]
            o_ref[pl.ds(base, CT), :] = (l * _sigmoid(l)).astype(BF16)
            return carry

        lax.fori_loop(0, L // CT, tile, 0)

    return pl.pallas_call(
        body, name=name, grid=(1,), in_specs=_conf_in_specs(), out_specs=_cfull((L, CMC)),
        out_shape=jax.ShapeDtypeStruct((L, CMC), BF16),
        scratch_shapes=[pltpu.VMEM((L + CPAD, CMC), F32), pltpu.VMEM((CT + CPAD, CMC), F32)],
        compiler_params=_cp(("arbitrary",)))(u, u, w, b, lw, lb)


def _conf_bwd(u, w, b, lw, lb, dy, name):
    def body(a_ref, g_ref, w_ref, b_ref, lw_ref, lb_ref, dy_ref,
             da_ref, dg_ref, dw_ref, db_ref, dlw_ref, dlb_ref, hp_ref, dcp_ref, win_ref):
        hp_ref[0:CPAD, :] = jnp.zeros((CPAD, CMC), F32)
        hp_ref[CPAD:, :] = a_ref[...] * _sigmoid(g_ref[...])
        dcp_ref[L:, :] = jnp.zeros((CPAD, CMC), F32)
        dw_ref[...] = jnp.zeros_like(dw_ref)
        db_ref[...] = jnp.zeros_like(db_ref)
        dlw_ref[...] = jnp.zeros_like(dlw_ref)
        dlb_ref[...] = jnp.zeros_like(dlb_ref)

        def tile1(i, carry):
            base = pl.multiple_of(i * CT, CT)
            win_ref[...] = hp_ref[pl.ds(base, CT + CPAD), :]
            c = jnp.broadcast_to(b_ref[...], (CT, CMC))
            for k in range(CMK):
                c = c + w_ref[k:k + 1, :] * win_ref[pl.ds(2 + k, CT), :]
            mu = jnp.mean(c, axis=-1, keepdims=True)
            cc = c - mu
            var = jnp.mean(cc * cc, axis=-1, keepdims=True)
            rstd = lax.rsqrt(var + LN_EPS)
            n = cc * rstd
            l = n * lw_ref[...] + lb_ref[...]
            sl = _sigmoid(l)
            dl = dy_ref[pl.ds(base, CT), :] * (sl * (1.0 + l * (1.0 - sl)))
            dlw_ref[...] += jnp.sum(dl * n, axis=0, keepdims=True)
            dlb_ref[...] += jnp.sum(dl, axis=0, keepdims=True)
            dn = dl * lw_ref[...]
            dc = rstd * (dn - jnp.mean(dn, axis=-1, keepdims=True)
                         - n * jnp.mean(dn * n, axis=-1, keepdims=True))
            db_ref[...] += jnp.sum(dc, axis=0, keepdims=True)
            for k in range(CMK):
                dw_ref[k:k + 1, :] += jnp.sum(dc * win_ref[pl.ds(2 + k, CT), :], axis=0, keepdims=True)
            dcp_ref[pl.ds(base, CT), :] = dc
            return carry

        lax.fori_loop(0, L // CT, tile1, 0)

        def tile2(i, carry):
            base = pl.multiple_of(i * CT, CT)
            win_ref[...] = dcp_ref[pl.ds(base, CT + CPAD), :]
            dh = jnp.zeros((CT, CMC), F32)
            for k in range(CMK):
                dh = dh + w_ref[k:k + 1, :] * win_ref[pl.ds(CMK - 1 - k, CT), :]
            av = a_ref[pl.ds(base, CT), :]
            sg = _sigmoid(g_ref[pl.ds(base, CT), :])
            da_ref[pl.ds(base, CT), :] = (dh * sg).astype(BF16)
            dg_ref[pl.ds(base, CT), :] = (dh * av * sg * (1.0 - sg)).astype(BF16)
            return carry

        lax.fori_loop(0, L // CT, tile2, 0)

    vec = jax.ShapeDtypeStruct((1, CMC), F32)
    dy_spec = pl.BlockSpec((L, CMC), lambda i: (0, (SSD_W + ATT_W) // CMC))
    return pl.pallas_call(
        body, name=name, grid=(1,), in_specs=_conf_in_specs() + [dy_spec],
        out_specs=[_cfull((L, CMC)), _cfull((L, CMC)), _cfull((32, CMC)), _cfull((1, CMC)), _cfull((1, CMC)),
                   _cfull((1, CMC))],
        out_shape=[jax.ShapeDtypeStruct((L, CMC), BF16), jax.ShapeDtypeStruct((L, CMC), BF16),
                   jax.ShapeDtypeStruct((32, CMC), F32), vec, vec, vec],
        scratch_shapes=[pltpu.VMEM((L + CPAD, CMC), F32), pltpu.VMEM((L + CPAD, CMC), F32),
                        pltpu.VMEM((CT + CPAD, CMC), F32)],
        compiler_params=_cp(("arbitrary",)))(u, u, w, b, lw, lb, dy)


def _attn_prep(a, b, wa, wb, c, s):
    r = lax.rsqrt((jnp.sum(a * a, axis=-1, keepdims=True) + jnp.sum(b * b, axis=-1, keepdims=True))
                  * (1.0 / HD) + RMS_EPS)
    ha = a * r
    hb = b * r
    na = ha * wa
    nb = hb * wb
    return r, ha, hb, na * c - nb * s, nb * c + na * s


def _attn_scores(qa_n, qb_n, ka_c, kb_c, n, lo, sink):
    nk = ka_c.shape[0]
    s = (_dg(qa_n, ka_c, 1, 1) + _dg(qb_n, kb_c, 1, 1)) * (1.0 / math.sqrt(HD))
    qi = lax.broadcasted_iota(jnp.int32, (Q, nk), 0) + n * Q
    ki = lax.broadcasted_iota(jnp.int32, (Q, nk), 1) + lo
    diff = qi - ki
    s = jnp.where((diff >= 0) & (diff < Q), s, NEG)
    m = jnp.maximum(jnp.max(s, axis=-1, keepdims=True), sink)
    p = jnp.exp(s - m)
    ps = jnp.exp(sink - m)
    den = jnp.sum(p, axis=-1, keepdims=True) + ps
    return p / den, ps / den


def _attn_specs():
    qh = pl.BlockSpec((1, L, HH), lambda j: (j, 0, 0))
    kh = pl.BlockSpec((1, L, HH), lambda j: (j // 4, 0, 0))
    vv = pl.BlockSpec((1, L, HD), lambda j: (j // 4, 0, 0))
    tab = pl.BlockSpec((L, HH), lambda j: (0, 0))
    wv = pl.BlockSpec((1, HH), lambda j: (0, 0))
    sk = pl.BlockSpec((1, 1, 128), lambda j: (j, 0, 0))
    return qh, kh, vv, tab, wv, sk


def _attn_fwd(qa, qb, ka, kb, v, cos, sin, qwa, qwb, kwa, kwb, sinks, name):
    def body(qa_ref, qb_ref, ka_ref, kb_ref, v_ref, c_ref, s_ref, qwa_ref, qwb_ref, kwa_ref, kwb_ref,
             sk_ref, o_ref):
        c = c_ref[...]
        s = s_ref[...]
        _, _, _, qra, qrb = _attn_prep(qa_ref[0], qb_ref[0], qwa_ref[...], qwb_ref[...], c, s)
        _, _, _, kra, krb = _attn_prep(ka_ref[0], kb_ref[0], kwa_ref[...], kwb_ref[...], c, s)
        qra, qrb, kra, krb = (t.astype(BF16) for t in (qra, qrb, kra, krb))
        vb = v_ref[0].astype(BF16)
        sink = sk_ref[0][:, 0:1]
        for n in range(NC):
            lo = max(n - 1, 0) * Q
            hi = (n + 1) * Q
            p, _ = _attn_scores(qra[n * Q:hi], qrb[n * Q:hi], kra[lo:hi], krb[lo:hi], n, lo, sink)
            o_ref[0, n * Q:hi, :] = _dg(p.astype(BF16), vb[lo:hi], 1, 0)

    qh, kh, vv, tab, wv, sk = _attn_specs()
    return pl.pallas_call(
        body, name=name, grid=(NQH,),
        in_specs=[qh, qh, kh, kh, vv, tab, tab, wv, wv, wv, wv, sk],
        out_specs=pl.BlockSpec((1, L, HD), lambda j: (j, 0, 0)),
        out_shape=jax.ShapeDtypeStruct((NQH, L, HD), F32),
        compiler_params=_cp(("parallel",)))(qa, qb, ka, kb, v, cos, sin, qwa, qwb, kwa, kwb, sinks)


def _attn_bwd(qa, qb, ka, kb, v, cos, sin, qwa, qwb, kwa, kwb, sinks, do, name):
    def body(qa_ref, qb_ref, ka_ref, kb_ref, v_ref, c_ref, s_ref, qwa_ref, qwb_ref, kwa_ref, kwb_ref,
             sk_ref, do_ref,
             dqa_ref, dqb_ref, dka_ref, dkb_ref, dv_ref, dqwa_ref, dqwb_ref, dkwa_ref, dkwb_ref, dsk_ref,
             dqra_s, dqrb_s, dkra_s, dkrb_s, dv_s):
        j = pl.program_id(0)
        c = c_ref[...]
        s = s_ref[...]
        qwa, qwb, kwa, kwb = qwa_ref[...], qwb_ref[...], kwa_ref[...], kwb_ref[...]
        qr, qha, qhb, qra, qrb = _attn_prep(qa_ref[0], qb_ref[0], qwa, qwb, c, s)
        kr, kha, khb, kra, krb = _attn_prep(ka_ref[0], kb_ref[0], kwa, kwb, c, s)
        qra, qrb, kra, krb = (t.astype(BF16) for t in (qra, qrb, kra, krb))
        vb = v_ref[0].astype(BF16)
        sink = sk_ref[0][:, 0:1]
        dkra_s[...] = jnp.zeros_like(dkra_s)
        dkrb_s[...] = jnp.zeros_like(dkrb_s)
        dv_s[...] = jnp.zeros_like(dv_s)
        dsink = jnp.zeros((1, 1), F32)
        scale = 1.0 / math.sqrt(HD)
        for n in range(NC):
            lo = max(n - 1, 0) * Q
            hi = (n + 1) * Q
            p, ps = _attn_scores(qra[n * Q:hi], qrb[n * Q:hi], kra[lo:hi], krb[lo:hi], n, lo, sink)
            don = do_ref[0, n * Q:hi, :]
            dob = don.astype(BF16)
            pb = p.astype(BF16)
            dv_s[lo:hi, :] += _dg(pb, dob, 0, 0)
            dp = _dg(dob, vb[lo:hi], 1, 1)
            delta = jnp.sum(p * dp, axis=-1, keepdims=True)
            dsink = dsink - jnp.sum(ps * delta, axis=0, keepdims=True)
            ds = (p * (dp - delta) * scale).astype(BF16)
            dqra_s[n * Q:hi, :] = _dg(ds, kra[lo:hi], 1, 0)
            dqrb_s[n * Q:hi, :] = _dg(ds, krb[lo:hi], 1, 0)
            dkra_s[lo:hi, :] += _dg(ds, qra[n * Q:hi], 0, 0)
            dkrb_s[lo:hi, :] += _dg(ds, qrb[n * Q:hi], 0, 0)

        def unprep(dra, drb, r, ha, hb, wa, wb):
            dna = dra * c + drb * s
            dnb = drb * c - dra * s
            dwa = jnp.sum(dna * ha, axis=0, keepdims=True)
            dwb = jnp.sum(dnb * hb, axis=0, keepdims=True)
            ga = dna * wa
            gb = dnb * wb
            mu = (jnp.sum(ga * ha, axis=-1, keepdims=True) + jnp.sum(gb * hb, axis=-1, keepdims=True)) * (1.0 / HD)
            return r * (ga - ha * mu), r * (gb - hb * mu), dwa, dwb

        dqa, dqb, dqwa, dqwb = unprep(dqra_s[...], dqrb_s[...], qr, qha, qhb, qwa, qwb)
        dka, dkb, dkwa, dkwb = unprep(dkra_s[...], dkrb_s[...], kr, kha, khb, kwa, kwb)
        dqa_ref[0] = dqa
        dqb_ref[0] = dqb
        dsk_ref[0] = jnp.broadcast_to(dsink, (1, 128))

        @pl.when(j == 0)
        def _():
            dqwa_ref[...] = jnp.zeros_like(dqwa_ref)
            dqwb_ref[...] = jnp.zeros_like(dqwb_ref)
            dkwa_ref[...] = jnp.zeros_like(dkwa_ref)
            dkwb_ref[...] = jnp.zeros_like(dkwb_ref)
        dqwa_ref[...] += dqwa
        dqwb_ref[...] += dqwb
        dkwa_ref[...] += dkwa
        dkwb_ref[...] += dkwb

        @pl.when(j % 4 == 0)
        def _():
            dka_ref[0] = dka
            dkb_ref[0] = dkb
            dv_ref[0] = dv_s[...]

        @pl.when(j % 4 != 0)
        def _():
            dka_ref[0] += dka
            dkb_ref[0] += dkb
            dv_ref[0] += dv_s[...]

    qh, kh, vv, tab, wv, sk = _attn_specs()
    qo = pl.BlockSpec((1, L, HD), lambda j: (j, 0, 0))
    hvec = jax.ShapeDtypeStruct((1, HH), F32)
    return pl.pallas_call(
        body, name=name, grid=(NQH,),
        in_specs=[qh, qh, kh, kh, vv, tab, tab, wv, wv, wv, wv, sk, qo],
        out_specs=[qh, qh, kh, kh, vv, wv, wv, wv, wv, sk],
        out_shape=[jax.ShapeDtypeStruct((NQH, L, HH), F32), jax.ShapeDtypeStruct((NQH, L, HH), F32),
                   jax.ShapeDtypeStruct((NKV, L, HH), F32), jax.ShapeDtypeStruct((NKV, L, HH), F32),
                   jax.ShapeDtypeStruct((NKV, L, HD), F32), hvec, hvec, hvec, hvec,
                   jax.ShapeDtypeStruct((NQH, 1, 128), F32)],
        scratch_shapes=[pltpu.VMEM((L, HH), F32), pltpu.VMEM((L, HH), F32), pltpu.VMEM((L, HH), F32),
                        pltpu.VMEM((L, HH), F32), pltpu.VMEM((L, HD), F32)],
        compiler_params=_cp(("arbitrary",)))(qa, qb, ka, kb, v, cos, sin, qwa, qwb, kwa, kwb, sinks, do)


def _ssd_consts():
    hh = jnp.arange(128)[:, None]
    e = (hh == (jnp.arange(SSD_W)[None, :] // HP)).astype(BF16)
    e2 = (hh == (jnp.arange(NH * 128)[None, :] // 128)).astype(BF16)
    et = e.T
    tril = (jnp.arange(Q)[:, None] >= jnp.arange(Q)[None, :]).astype(BF16)
    triu = tril.T
    eye = jnp.eye(128, dtype=BF16)
    return e, e2, et, tril, triu, eye


def _ssd_common(x_ref, ext_scr, cw_ref, cb_ref, dt_ref, dtb_ref, alog_ref, e_ref, e2_ref, tril_ref, triu_ref,
                arow_scr, acol_scr, eax_scr):
    conv = jnp.broadcast_to(cb_ref[...], (Q, XBC))
    for k in range(4):
        conv = conv + cw_ref[k:k + 1, :] * ext_scr[pl.ds(5 + k, Q), :]
    sg = _sigmoid(conv)
    xbc = conv * sg
    dtpre = dt_ref[...] + dtb_ref[...]
    dt = _softplus(dtpre)
    a = -jnp.exp(alog_ref[...])
    adt = dt * a
    acol = _xdot_r(tril_ref[...], adt)
    acol_scr[...] = acol
    arow_scr[...] = _xdot_l(adt, triu_ref[...], 0, 0)
    alast = acol_scr[Q - 1:Q, :]
    ea = jnp.exp(acol)
    decs = jnp.exp(alast - acol)
    e = e_ref[...]
    dt_x = _xdot_l(dt, e)
    eax_scr[...] = _xdot_l(ea, e)
    decs_x = _xdot_l(decs, e)
    acx2 = _xdot_l(acol, e2_ref[...])
    return conv, sg, xbc, dtpre, dt, a, adt, acol, alast, ea, decs, dt_x, decs_x, acx2


def _ssd_fwd(u, cw, cb, dtb, alog, dxp, nw, consts, name):
    e, e2, et, tril, triu, eye = consts

    def body(z0_ref, z1_ref, x_ref, dt_ref, cw_ref, cb_ref, dtb_ref, alog_ref, dx_ref, nw_ref, e_ref, e2_ref,
             tril_ref, triu_ref, ya_ref, ypre_ref, st_ref, s_scr, ext_scr, arow_scr, acol_scr, eax_scr):
        c = pl.program_id(0)

        @pl.when(c == 0)
        def _():
            s_scr[...] = jnp.zeros_like(s_scr)
            ext_scr[0:8, :] = jnp.zeros((8, XBC), F32)
        ext_scr[8:8 + Q, :] = x_ref[...]
        (conv, sg, xbc, dtpre, dt, a, adt, acol, alast, ea, decs, dt_x, decs_x, acx2) = _ssd_common(
            x_ref, ext_scr, cw_ref, cb_ref, dt_ref, dtb_ref, alog_ref, e_ref, e2_ref, tril_ref, triu_ref,
            arow_scr, acol_scr, eax_scr)
        ext_scr[0:8, :] = ext_scr[Q:Q + 8, :]
        xs = xbc[:, :SSD_W]
        xdt = xs * dt_x
        lane = lax.broadcasted_iota(jnp.int32, (Q, 128), 1)
        causal = lax.broadcasted_iota(jnp.int32, (Q, Q), 0) >= lax.broadcasted_iota(jnp.int32, (Q, Q), 1)
        for g in range(2):
            bg = xbc[:, SSD_W + g * NS:SSD_W + (g + 1) * NS].astype(BF16)
            cg = xbc[:, SSD_W + 2 * NS + g * NS:SSD_W + 2 * NS + (g + 1) * NS].astype(BF16)
            cbm = _dg(cg, bg, 1, 1)
            sgv = s_scr[g]
            st_ref[0, g] = sgv
            gc = slice(g * 512, (g + 1) * 512)
            yoff = _dg(cg, sgv.astype(BF16), 1, 0) * eax_scr[:, gc]
            for pr in range(4):
                h0 = g * 8 + 2 * pr
                h1 = h0 + 1
                c0 = g * 512 + pr * 128
                xp = xdt[:, c0:c0 + 128].astype(BF16)
                w0 = (cbm * jnp.exp(jnp.where(causal, acx2[:, h0 * 128:(h0 + 1) * 128] - arow_scr[h0:h0 + 1, :],
                                              NEG))).astype(BF16)
                w1 = (cbm * jnp.exp(jnp.where(causal, acx2[:, h1 * 128:(h1 + 1) * 128] - arow_scr[h1:h1 + 1, :],
                                              NEG))).astype(BF16)
                yd = jnp.where(lane < HP, _dg(w0, xp, 1, 0), _dg(w1, xp, 1, 0))
                ypre_ref[:, c0:c0 + 128] = (yd + yoff[:, pr * 128:(pr + 1) * 128]
                                            + xs[:, c0:c0 + 128] * dx_ref[:, c0:c0 + 128])
            contrib = _dg(bg, (xdt[:, gc] * decs_x[:, gc]).astype(BF16), 0, 0)
            s_scr[g] = sgv * eax_scr[Q - 1:Q, gc] + contrib
        for g, zr in enumerate((z0_ref, z1_ref)):
            gc = slice(g * 512, (g + 1) * 512)
            zz = zr[...]
            ggg = ypre_ref[:, gc] * (zz * _sigmoid(zz))
            rstd = lax.rsqrt(jnp.mean(ggg * ggg, axis=-1, keepdims=True) + RMS_EPS)
            ya_ref[:, gc] = (ggg * rstd * nw_ref[:, gc]).astype(BF16)

    def row(w, blk=0):
        return pl.BlockSpec((Q, w), lambda c: (c, blk))

    def full(shape):
        return pl.BlockSpec(shape, lambda c: (0,) * len(shape))

    return pl.pallas_call(
        body, name=name, grid=(NC,),
        in_specs=[row(512, U_Z // 512), row(512, U_Z // 512 + 1), row(XBC, U_X // XBC), row(128, U_DT // 128),
                  full((4, XBC)), full((1, XBC)), full((1, 128)), full((1, 128)),
                  full((1, SSD_W)), full((1, SSD_W)), full((128, SSD_W)), full((128, NH * 128)), full((Q, Q)),
                  full((Q, Q))],
        out_specs=[row(SSD_W), row(SSD_W), pl.BlockSpec((1, 2, NS, 512), lambda c: (c, 0, 0, 0))],
        out_shape=[jax.ShapeDtypeStruct((L, SSD_W), BF16), jax.ShapeDtypeStruct((L, SSD_W), F32),
                   jax.ShapeDtypeStruct((NC, 2, NS, 512), F32)],
        scratch_shapes=[pltpu.VMEM((2, NS, 512), F32), pltpu.VMEM((Q + 8, XBC), F32), pltpu.VMEM((128, Q), F32),
                        pltpu.VMEM((Q, 128), F32), pltpu.VMEM((Q, SSD_W), F32)],
        compiler_params=_cp(("arbitrary",)))(u, u, u, u, cw, cb, dtb, alog, dxp, nw, e, e2, tril, triu)


def _ssd_bwd(u, ypre, st, dy, cw, cb, dtb, alog, dxp, nw, consts, name):
    e, e2, et, tril, triu, eye = consts

    def body(z0_ref, z1_ref, x_ref, xp_ref, dt_ref, ypre_ref, st_ref, dya_ref, cw_ref, cb_ref, dtb_ref, alog_ref, dx_ref,
             nw_ref, e_ref, e2_ref, et_ref, tril_ref, triu_ref, eye_ref,
             dz_ref, dxr_ref, ddtr_ref, dcw_ref, dcb_ref, ddtb_ref, dalog_ref, dd_ref, dnw_ref,
             g_scr, ext_scr, ext2_scr, arow_scr, acol_scr, eax_scr, darow_scr, dxdt_scr, t1_scr, t2_scr, dgg_scr):
        i = pl.program_id(0)

        @pl.when(i == 0)
        def _():
            g_scr[...] = jnp.zeros_like(g_scr)
            ext2_scr[Q:Q + 8, :] = jnp.zeros((8, XBC), F32)
            for r in (dcw_ref, dcb_ref, ddtb_ref, dalog_ref, dd_ref, dnw_ref):
                r[...] = jnp.zeros_like(r)
        not_first = jnp.where(i < NC - 1, 1.0, 0.0)
        ext_scr[0:8, :] = xp_ref[Q - 8:Q, :] * not_first
        ext_scr[8:8 + Q, :] = x_ref[...]
        (conv, sg, xbc, dtpre, dt, a, adt, acol, alast, ea, decs, dt_x, decs_x, acx2) = _ssd_common(
            x_ref, ext_scr, cw_ref, cb_ref, dt_ref, dtb_ref, alog_ref, e_ref, e2_ref, tril_ref, triu_ref,
            arow_scr, acol_scr, eax_scr)
        et_m = et_ref[...]
        xs = xbc[:, :SSD_W]
        xdt = xs * dt_x
        y = ypre_ref[...]
        zz = jnp.concatenate([z0_ref[...], z1_ref[...]], axis=1)
        sz = _sigmoid(zz)
        silu_z = zz * sz
        gg = y * silu_z
        dya = dya_ref[...]
        for g in range(2):
            gc = slice(g * 512, (g + 1) * 512)
            ggg = gg[:, gc]
            rstd = lax.rsqrt(jnp.mean(ggg * ggg, axis=-1, keepdims=True) + RMS_EPS)
            n = ggg * rstd
            dyag = dya[:, gc]
            dnw_ref[:, gc] += jnp.sum(dyag * n, axis=0, keepdims=True)
            dn = dyag * nw_ref[:, gc]
            dgg_scr[:, gc] = rstd * (dn - n * jnp.mean(dn * n, axis=-1, keepdims=True))
        dgg = dgg_scr[...]
        dy = dgg * silu_z
        dz_ref[...] = (dgg * y * (sz * (1.0 + zz * (1.0 - sz)))).astype(BF16)
        dd_ref[...] += _rowdot(jnp.sum(dy * xs, axis=0, keepdims=True), et_m)
        dxs = dy * dx_ref[...]
        dys = dy * eax_scr[...]
        lane = lax.broadcasted_iota(jnp.int32, (Q, 128), 1)
        causal = lax.broadcasted_iota(jnp.int32, (Q, Q), 0) >= lax.broadcasted_iota(jnp.int32, (Q, Q), 1)
        darow_scr[...] = jnp.zeros_like(darow_scr)
        dacol = jnp.zeros((Q, 128), F32)
        dcdx = []
        dbs = []
        dcs = []
        for g in range(2):
            gc = slice(g * 512, (g + 1) * 512)
            bg = xbc[:, SSD_W + g * NS:SSD_W + (g + 1) * NS].astype(BF16)
            cg = xbc[:, SSD_W + 2 * NS + g * NS:SSD_W + 2 * NS + (g + 1) * NS].astype(BF16)
            cbm = _dg(cg, bg, 1, 1)
            sgv = st_ref[0, g]
            sgb = sgv.astype(BF16)
            gv = g_scr[g]
            gvb = gv.astype(BF16)
            yoff = _dg(cg, sgb, 1, 0) * eax_scr[:, gc]
            dysg = dys[:, gc].astype(BF16)
            dcg = _dg(dysg, sgb, 1, 1)
            ds_off = _dg(cg, dysg, 0, 0)
            t1_scr[:, gc] = dy[:, gc] * yoff
            xdec = xdt[:, gc] * decs_x[:, gc]
            dxd = _dg(bg, gvb, 1, 0)
            dbg = _dg(xdec.astype(BF16), gvb, 1, 1)
            dxdt_g = dxd * decs_x[:, gc]
            t2_scr[:, gc] = dxd * xdt[:, gc]
            cdx = eax_scr[Q - 1:Q, gc]
            dcdx.append(jnp.sum(gv * sgv, axis=0, keepdims=True))
            g_scr[g] = gv * cdx + ds_off
            dcb_acc = jnp.zeros((Q, Q), F32)
            for pr in range(4):
                c0 = g * 512 + pr * 128
                xp = xdt[:, c0:c0 + 128].astype(BF16)
                dyp = dy[:, c0:c0 + 128]
                dypb = dyp.astype(BF16)
                halves = []
                for hh, keep in ((g * 8 + 2 * pr, lane < HP), (g * 8 + 2 * pr + 1, lane >= HP)):
                    lam = jnp.exp(jnp.where(causal, acx2[:, hh * 128:(hh + 1) * 128] - arow_scr[hh:hh + 1, :], NEG))
                    w = cbm * lam
                    dw = _dg(jnp.where(keep, dyp, 0.0).astype(BF16), xp, 1, 1)
                    dcb_acc = dcb_acc + dw * lam
                    t = dw * w
                    dacol = dacol + jnp.sum(t, axis=-1, keepdims=True) * (lane == hh).astype(F32)
                    darow_scr[hh:hh + 1, :] -= jnp.sum(t, axis=0, keepdims=True)
                    halves.append(_dg(w.astype(BF16), dypb, 0, 0))
                dxdt_scr[:, c0:c0 + 128] = (jnp.where(lane < HP, halves[0], halves[1])
                                            + dxdt_g[:, pr * 128:(pr + 1) * 128])
            dcbb = dcb_acc.astype(BF16)
            dcs.append(dcg + _dg(dcbb, bg, 1, 0))
            dbs.append(dbg + _dg(dcbb, cg, 0, 0))
        dacol = dacol + _xdot_l(t1_scr[...], et_m)
        ddecs = _xdot_l(t2_scr[...], et_m) * decs
        dacol = dacol - ddecs
        dalast = jnp.sum(ddecs, axis=0, keepdims=True)
        dcd = _rowdot(jnp.concatenate(dcdx, axis=1), et_m)
        dalast = dalast + dcd * jnp.exp(alast)
        dacol = dacol + _xdot_l(darow_scr[...], eye_ref[...], 0, 0)
        rowi = lax.broadcasted_iota(jnp.int32, (Q, 128), 0)
        dacol = dacol + jnp.where(rowi == Q - 1, dalast, 0.0)
        dadt = _xdot_r(triu_ref[...], dacol)
        dxdt = dxdt_scr[...]
        ddt = dadt * a + _xdot_l(dxdt * xs, et_m)
        dalog_ref[...] += jnp.sum(dadt * dt, axis=0, keepdims=True) * a
        dxs = dxs + dxdt * dt_x
        ddtr = ddt * _sigmoid(dtpre)
        ddtb_ref[...] += jnp.sum(ddtr, axis=0, keepdims=True)
        ddtr_ref[...] = ddtr.astype(BF16)
        dsilu = sg * (1.0 + conv * (1.0 - sg))
        ext2_scr[0:Q, 0:SSD_W] = dxs * dsilu[:, :SSD_W]
        for g in range(2):
            o1 = SSD_W + g * NS
            o2 = SSD_W + 2 * NS + g * NS
            ext2_scr[0:Q, o1:o1 + NS] = dbs[g] * dsilu[:, o1:o1 + NS]
            ext2_scr[0:Q, o2:o2 + NS] = dcs[g] * dsilu[:, o2:o2 + NS]
        dconv = ext2_scr[0:Q, :]
        dcb_ref[...] += jnp.sum(dconv, axis=0, keepdims=True)
        dxr = jnp.zeros((Q, XBC), F32)
        for k in range(4):
            dcw_ref[k:k + 1, :] += jnp.sum(dconv * ext_scr[pl.ds(5 + k, Q), :], axis=0, keepdims=True)
            dxr = dxr + cw_ref[k:k + 1, :] * ext2_scr[pl.ds(3 - k, Q), :]
        dxr_ref[...] = dxr.astype(BF16)
        ext2_scr[Q:Q + 8, :] = ext2_scr[0:8, :]

    def row(w, blk=0):
        return pl.BlockSpec((Q, w), lambda i: (NC - 1 - i, blk))

    def full(shape):
        return pl.BlockSpec(shape, lambda i: (0,) * len(shape))

    prev = pl.BlockSpec((Q, XBC), lambda i: (jnp.maximum(NC - 2 - i, 0), U_X // XBC))
    return pl.pallas_call(
        body, name=name, grid=(NC,),
        in_specs=[row(512, U_Z // 512), row(512, U_Z // 512 + 1), row(XBC, U_X // XBC), prev, row(128, U_DT // 128),
                  row(SSD_W),
                  pl.BlockSpec((1, 2, NS, 512), lambda i: (NC - 1 - i, 0, 0, 0)), row(SSD_W),
                  full((4, XBC)), full((1, XBC)), full((1, 128)), full((1, 128)), full((1, SSD_W)),
                  full((1, SSD_W)), full((128, SSD_W)), full((128, NH * 128)), full((SSD_W, 128)), full((Q, Q)),
                  full((Q, Q)), full((128, 128))],
        out_specs=[row(SSD_W), row(XBC), row(128), full((8, XBC)), full((1, XBC)), full((1, 128)), full((1, 128)),
                   full((1, 128)), full((1, SSD_W))],
        out_shape=[jax.ShapeDtypeStruct((L, SSD_W), BF16), jax.ShapeDtypeStruct((L, XBC), BF16),
                   jax.ShapeDtypeStruct((L, 128), BF16), jax.ShapeDtypeStruct((8, XBC), F32),
                   jax.ShapeDtypeStruct((1, XBC), F32), jax.ShapeDtypeStruct((1, 128), F32),
                   jax.ShapeDtypeStruct((1, 128), F32), jax.ShapeDtypeStruct((1, 128), F32),
                   jax.ShapeDtypeStruct((1, SSD_W), F32)],
        scratch_shapes=[pltpu.VMEM((2, NS, 512), F32), pltpu.VMEM((Q + 8, XBC), F32), pltpu.VMEM((Q + 8, XBC), F32),
                        pltpu.VMEM((128, Q), F32), pltpu.VMEM((Q, 128), F32), pltpu.VMEM((Q, SSD_W), F32),
                        pltpu.VMEM((128, Q), F32), pltpu.VMEM((Q, SSD_W), F32), pltpu.VMEM((Q, SSD_W), F32),
                        pltpu.VMEM((Q, SSD_W), F32), pltpu.VMEM((Q, SSD_W), F32)],
        compiler_params=_cp(("arbitrary",)))(u, u, u, u, u, ypre, st, dy, cw, cb, dtb, alog, dxp, nw,
                                             e, e2, et, tril, triu, eye)


def _my_pos():
    return lax.axis_index("x"), lax.axis_index("y"), lax.axis_index("c")


CHIP_REL = ((1, 0), (0, 1), (1, 1))
CHIP_XOR = (2, 1, 3)
NW = 4


def _chips(x, y):
    return [(1 - x if dx else x, 1 - y if dy else y) for dx, dy in CHIP_REL]


def _gather_layer(shards, name):
    def body(*refs):
        ins, outs = refs[:NW], refs[NW:2 * NW]
        send_sems, recv_sems, loc_sems = refs[2 * NW:]
        x, y, c = _my_pos()
        s = 2 * x + y
        sib = (x, y, 1 - c)
        chips = _chips(x, y)
        locs, sends = [], []
        for w in range(NW):
            hr = ins[w].shape[0] // 2
            lc = pltpu.make_async_copy(ins[w], outs[w].at[s], loc_sems.at[w])
            lc.start()
            locs.append(lc)
            mine = pl.ds(c * hr, hr)
            for k in range(3):
                cp = pltpu.make_async_remote_copy(
                    src_ref=ins[w].at[mine], dst_ref=outs[w].at[s, mine], send_sem=send_sems.at[w, k],
                    recv_sem=recv_sems.at[w, k], device_id=(*chips[k], c), device_id_type=MESH)
                cp.start()
                sends.append(cp)
        for w in range(NW):
            hr = ins[w].shape[0] // 2
            mine = pl.ds(c * hr, hr)
            for k in range(3):
                blk = outs[w].at[s ^ CHIP_XOR[k], mine]
                pltpu.make_async_remote_copy(
                    src_ref=blk, dst_ref=blk, send_sem=send_sems.at[w, k], recv_sem=recv_sems.at[w, k],
                    device_id=(*chips[k], c), device_id_type=MESH).wait_recv()
                fw = pltpu.make_async_remote_copy(
                    src_ref=blk, dst_ref=blk, send_sem=send_sems.at[w, 3 + k], recv_sem=recv_sems.at[w, 3 + k],
                    device_id=sib, device_id_type=MESH)
                fw.start()
                sends.append(fw)
        for w in range(NW):
            hr = ins[w].shape[0] // 2
            other = pl.ds((1 - c) * hr, hr)
            for k in range(3):
                blk = outs[w].at[s ^ CHIP_XOR[k], other]
                pltpu.make_async_remote_copy(
                    src_ref=blk, dst_ref=blk, send_sem=send_sems.at[w, 3 + k], recv_sem=recv_sems.at[w, 3 + k],
                    device_id=sib, device_id_type=MESH).wait_recv()
        for cp in sends:
            cp.wait_send()
        for lc in locs:
            lc.wait()

    return pl.pallas_call(
        body, name=name, in_specs=[ANY] * NW, out_specs=[ANY] * NW,
        out_shape=[jax.ShapeDtypeStruct((4,) + t.shape, t.dtype) for t in shards],
        scratch_shapes=[pltpu.SemaphoreType.DMA((NW, 6)), pltpu.SemaphoreType.DMA((NW, 6)),
                        pltpu.SemaphoreType.DMA((NW,))])(*shards)


def _rs_pair(dwb, name):
    def body(*refs):
        ins, outs = refs[:NW], refs[NW:2 * NW]
        send_sems, recv_sems = refs[2 * NW:]
        x, y, c = _my_pos()
        cps = []
        for w in range(NW):
            hr = ins[w].shape[1] // 2
            cp = pltpu.make_async_remote_copy(
                src_ref=ins[w].at[:, pl.ds((1 - c) * hr, hr)], dst_ref=outs[w], send_sem=send_sems.at[w],
                recv_sem=recv_sems.at[w], device_id=(x, y, 1 - c), device_id_type=MESH)
            cp.start()
            cps.append(cp)
        for cp in cps:
            cp.wait()

    return pl.pallas_call(
        body, name=name, in_specs=[ANY] * NW, out_specs=[ANY] * NW,
        out_shape=[jax.ShapeDtypeStruct((4, t.shape[1] // 2, t.shape[2]), t.dtype) for t in dwb],
        scratch_shapes=[pltpu.SemaphoreType.DMA((NW,)), pltpu.SemaphoreType.DMA((NW,))])(*dwb)


def _rs_chip(pb, name):
    def body(*refs):
        ins, outs = refs[:NW], refs[NW:2 * NW]
        send_sems, recv_sems = refs[2 * NW:]
        x, y, c = _my_pos()
        s = 2 * x + y
        chips = _chips(x, y)
        cps = []
        for w in range(NW):
            for k in range(3):
                cp = pltpu.make_async_remote_copy(
                    src_ref=ins[w].at[s ^ CHIP_XOR[k]], dst_ref=outs[w].at[k], send_sem=send_sems.at[w, k],
                    recv_sem=recv_sems.at[w, k], device_id=(*chips[k], c), device_id_type=MESH)
                cp.start()
                cps.append(cp)
        for cp in cps:
            cp.wait()

    return pl.pallas_call(
        body, name=name, in_specs=[ANY] * NW, out_specs=[ANY] * NW,
        out_shape=[jax.ShapeDtypeStruct((3,) + t.shape[1:], t.dtype) for t in pb],
        scratch_shapes=[pltpu.SemaphoreType.DMA((NW, 3)), pltpu.SemaphoreType.DMA((NW, 3))])(*pb)


def _rs_sib(q, name):
    def body(*refs):
        ins, outs = refs[:NW], refs[NW:2 * NW]
        send_sems, recv_sems, loc_sems = refs[2 * NW:]
        x, y, c = _my_pos()
        cps, locs = [], []
        for w in range(NW):
            hr = ins[w].shape[0]
            mine = pl.ds(c * hr, hr)
            lc = pltpu.make_async_copy(ins[w], outs[w].at[mine], loc_sems.at[w])
            lc.start()
            locs.append(lc)
            cp = pltpu.make_async_remote_copy(
                src_ref=ins[w], dst_ref=outs[w].at[mine], send_sem=send_sems.at[w], recv_sem=recv_sems.at[w],
                device_id=(x, y, 1 - c), device_id_type=MESH)
            cp.start()
            cps.append(cp)
        for w in range(NW):
            hr = ins[w].shape[0]
            other = outs[w].at[pl.ds((1 - c) * hr, hr)]
            pltpu.make_async_remote_copy(
                src_ref=ins[w], dst_ref=other, send_sem=send_sems.at[w], recv_sem=recv_sems.at[w],
                device_id=(x, y, 1 - c), device_id_type=MESH).wait_recv()
        for cp in cps:
            cp.wait_send()
        for lc in locs:
            lc.wait()

    return pl.pallas_call(
        body, name=name, in_specs=[ANY] * NW, out_specs=[ANY] * NW,
        out_shape=[jax.ShapeDtypeStruct((2 * t.shape[0], t.shape[1]), t.dtype) for t in q],
        scratch_shapes=[pltpu.SemaphoreType.DMA((NW,)), pltpu.SemaphoreType.DMA((NW,)),
                        pltpu.SemaphoreType.DMA((NW,))])(*q)


AT = 256


def _rs_add2(dw, got, cidx, name):
    _, r, cc = dw.shape
    hr = r // 2
    nb = hr // AT

    def body(c_ref, a_ref, b_ref, o_ref, ob_ref):
        acc = a_ref[...] + b_ref[...].astype(F32)
        o_ref[...] = acc
        ob_ref[...] = acc.astype(BF16)

    blk = pl.BlockSpec((1, AT, cc), lambda sh, i, c_ref: (sh, i, 0))
    return pl.pallas_call(
        body, name=name,
        grid_spec=pltpu.PrefetchScalarGridSpec(
            num_scalar_prefetch=1, grid=(4, nb),
            in_specs=[pl.BlockSpec((1, AT, cc), lambda sh, i, c_ref: (sh, c_ref[0] * nb + i, 0)), blk],
            out_specs=[blk, blk]),
        out_shape=[jax.ShapeDtypeStruct((4, hr, cc), F32), jax.ShapeDtypeStruct((4, hr, cc), BF16)],
        compiler_params=_cp(("parallel", "parallel")))(cidx, dw, got)


def _rs_add4(p, got, sidx, name):
    _, hr, cc = p.shape

    def body(s_ref, p_ref, g0_ref, g1_ref, g2_ref, o_ref):
        acc = p_ref[0] + g0_ref[0].astype(F32)
        acc = acc + g1_ref[0].astype(F32)
        o_ref[...] = acc + g2_ref[0].astype(F32)

    def gk(k):
        return pl.BlockSpec((1, AT, cc), lambda i, s_ref: (k, i, 0))

    return pl.pallas_call(
        body, name=name,
        grid_spec=pltpu.PrefetchScalarGridSpec(
            num_scalar_prefetch=1, grid=(hr // AT,),
            in_specs=[pl.BlockSpec((1, AT, cc), lambda i, s_ref: (s_ref[0], i, 0)), gk(0), gk(1), gk(2)],
            out_specs=pl.BlockSpec((AT, cc), lambda i, s_ref: (i, 0))),
        out_shape=jax.ShapeDtypeStruct((hr, cc), F32),
        compiler_params=_cp(("parallel",)))(sidx, p, got, got, got)


def _reduce_scatter_layer(dws, dwbs, tag):
    x, y, c = _my_pos()
    cidx = jnp.reshape(c, (1,)).astype(jnp.int32)
    sidx = jnp.reshape(2 * x + y, (1,)).astype(jnp.int32)
    got = _rs_pair(dwbs, "rs_pair")
    pairs = [_rs_add2(dws[w], got[w], cidx, "rs_add2_%d" % w) for w in range(NW)]
    recv = _rs_chip([p[1] for p in pairs], "rs_chip")
    q = [_rs_add4(pairs[w][0], recv[w], sidx, "rs_add4_%d" % w) for w in range(NW)]
    return _rs_sib(q, "rs_sib")


def _allreduce_small(buf, name):
    rows = buf.shape[0]

    def body(src_ref, out_ref, gat_ref, send_sems, recv_sems):
        x, y, c = _my_pos()
        me = 4 * x + 2 * y + c
        gat_ref[me] = src_ref[...]
        cps = []
        for r in range(1, N_DEV):
            tx = 1 - x if (r >> 2) & 1 else x
            ty = 1 - y if (r >> 1) & 1 else y
            tc = 1 - c if r & 1 else c
            cps.append(pltpu.make_async_remote_copy(
                src_ref=src_ref, dst_ref=gat_ref.at[me], send_sem=send_sems.at[r - 1], recv_sem=recv_sems.at[r - 1],
                device_id=(tx, ty, tc), device_id_type=MESH))
        for cp in cps:
            cp.start()
        for cp in cps:
            cp.wait()
        acc = gat_ref[0]
        for k in range(1, N_DEV):
            acc = acc + gat_ref[k]
        out_ref[...] = acc

    return pl.pallas_call(
        body, name=name, in_specs=[VMEM_SPEC], out_specs=VMEM_SPEC, out_shape=jax.ShapeDtypeStruct((rows, 128), F32),
        scratch_shapes=[pltpu.VMEM((N_DEV, rows, 128), F32), pltpu.SemaphoreType.DMA((N_DEV - 1,)),
                        pltpu.SemaphoreType.DMA((N_DEV - 1,))],
        compiler_params=_cp())(buf)


SMALL = (("norm_mix_w", (D,)), ("ssd_conv_w", (4, XBC)), ("ssd_conv_b", (XBC,)), ("ssd_dt_bias", (NH,)),
         ("ssd_a_log", (NH,)), ("ssd_d", (NH,)), ("ssd_norm_w", (SSD_W,)), ("q_norm_w", (HD,)),
         ("k_norm_w", (HD,)), ("attn_sinks", (NQH,)), ("cm_dw_w", (CMK, CMC)), ("cm_dw_b", (CMC,)),
         ("cm_ln_w", (CMC,)), ("cm_ln_b", (CMC,)), ("norm_mlp_w", (D,)))
SHARDED_SMALL = ("ssd_conv_w", "cm_dw_w")


def _seg_len(shape):
    n = 1
    for d in shape:
        n *= d
    return -(-n // 128) * 128


def _pack_small(vals, names):
    parts = []
    for name, shape in SMALL:
        if name not in names:
            continue
        v = vals[name].reshape(DEPTH, -1)
        pad = _seg_len(shape) - v.shape[1]
        parts.append(jnp.pad(v, ((0, 0), (0, pad))))
    flat = jnp.concatenate(parts, axis=1)
    return flat.reshape(-1, 128)


def _unpack_small(buf, names):
    flat = buf.reshape(DEPTH, -1)
    out = {}
    off = 0
    for name, shape in SMALL:
        if name not in names:
            continue
        n = 1
        for d in shape:
            n *= d
        out[name] = flat[:, off:off + n].reshape((DEPTH,) + shape)
        off += _seg_len(shape)
    return out


def _rope_tables():
    inv = 10000.0 ** (-jnp.arange(0, HD, 2, dtype=F32) / HD)
    ang = jnp.arange(L, dtype=F32)[:, None] * inv[None, :]
    return jnp.cos(ang), jnp.sin(ang)


def _pad128(v):
    return jnp.pad(v, (0, 128 - v.shape[0]))[None, :]


def _heads(t, nh, w):
    return jnp.transpose(t.reshape(L, nh, w), (1, 0, 2))


def _unheads(t):
    nh, _, w = t.shape
    return jnp.transpose(t, (1, 0, 2)).reshape(L, nh * w)


def kernel(x, norm_mix_w, w_in, ssd_conv_w, ssd_conv_b, ssd_dt_bias, ssd_a_log, ssd_d, ssd_norm_w, q_norm_w, k_norm_w, attn_sinks, cm_dw_w, cm_dw_b, cm_ln_w, cm_ln_b, w_out, norm_mlp_w, w_mlp_up, w_mlp_down, loss_target, m_norm_mix_w, m_w_in, m_ssd_conv_w, m_ssd_conv_b, m_ssd_dt_bias, m_ssd_a_log, m_ssd_d, m_ssd_norm_w, m_q_norm_w, m_k_norm_w, m_attn_sinks, m_cm_dw_w, m_cm_dw_b, m_cm_ln_w, m_cm_ln_b, m_w_out, m_norm_mlp_w, m_w_mlp_up, m_w_mlp_down, v_norm_mix_w, v_w_in, v_ssd_conv_w, v_ssd_conv_b, v_ssd_dt_bias, v_ssd_a_log, v_ssd_d, v_ssd_norm_w, v_q_norm_w, v_k_norm_w, v_attn_sinks, v_cm_dw_w, v_cm_dw_b, v_cm_ln_w, v_cm_ln_b, v_w_out, v_norm_mlp_w, v_w_mlp_up, v_w_mlp_down):
    px, py, pc = _my_pos()
    shard = 2 * px + py
    consts = _ssd_consts()
    cos, sin = _rope_tables()

    wb = [w.astype(BF16) for w in (w_in, w_out, w_mlp_up, w_mlp_down)]
    zc = jnp.zeros((DEPTH, 4, XBC), F32)
    zc = lax.dynamic_update_slice_in_dim(zc, ssd_conv_w, shard * (XBC // 4), axis=2)
    zd = jnp.zeros((DEPTH, CMK, CMC), F32)
    zd = lax.dynamic_update_slice_in_dim(zd, cm_dw_w, shard * (CMC // 4), axis=2)
    half = jnp.where(pc == 0, 1.0, 0.0).astype(F32)
    gw = _allreduce_small(_pack_small({"ssd_conv_w": zc * half, "cm_dw_w": zd * half}, SHARDED_SMALL), "ag_small")
    gw = _unpack_small(gw, SHARDED_SMALL)
    conv_w_full, dw_w_full = gw["ssd_conv_w"], gw["cm_dw_w"]

    o_x, o_dt, o_q, o_g = 1024, 2560, 2576, 3344

    xcur = x[0]
    saved = []
    for l in range(DEPTH):
        g_in, g_out, g_up, g_dn = _gather_layer([t[l] for t in wb], "ag_layer")
        wo = jnp.concatenate([g_in[0], g_in[1], g_in[2], g_in[3]], axis=1)
        w_perm = jnp.concatenate([wo[:, o_x:o_dt], wo[:, :o_x], wo[:, o_g:], wo[:, o_q:o_g], wo[:, o_dt:o_q],
                                  jnp.zeros((D, 128 - NH), BF16)], axis=1)
        wout_l = g_out.reshape(1, 2 * D, D)
        wdn_l = g_dn.reshape(1, DFF, D)
        wup_l = g_up[None]
        h = _rms_fwd(xcur, norm_mix_w[l][None], "rms_mix_fwd")
        u = _mm(h, w_perm, "nn", "in_proj", tn=640)
        alog = _pad128(ssd_a_log[l])
        dtb = _pad128(ssd_dt_bias[l])
        dxp = jnp.repeat(ssd_d[l], HP)[None, :]
        ssd_p = (conv_w_full[l], ssd_conv_b[l][None], dtb, alog, dxp, ssd_norm_w[l][None])
        ya, ypre, st = _ssd_fwd(u, *ssd_p, consts, "ssd_fwd")
        q3 = _heads(u[:, U_Q:U_K], NQH, HD)
        k3 = _heads(u[:, U_K:U_V], NKV, HD)
        v3 = _heads(u[:, U_V:U_DT], NKV, HD)
        qa, qb, ka, kb = q3[..., :HH], q3[..., HH:], k3[..., :HH], k3[..., HH:]
        qw, kw = q_norm_w[l], k_norm_w[l]
        sinks = jnp.broadcast_to(attn_sinks[l][:, None, None], (NQH, 1, 128))
        attn_args = (qa, qb, ka, kb, v3, cos, sin, qw[None, :HH], qw[None, HH:], kw[None, :HH], kw[None, HH:], sinks)
        yb = _unheads(_attn_fwd(*attn_args, "attn_fwd")).astype(BF16)
        conf_p = (dw_w_full[l], cm_dw_b[l][None], cm_ln_w[l][None], cm_ln_b[l][None])
        yc = _conf_fwd(u, *conf_p, "conf_fwd")
        ycat = jnp.concatenate([ya, yb, yc], axis=1)
        x1 = _mm_wl(ycat, wout_l, 0, "nn", "out_proj", add=xcur)
        hm = _rms_fwd(x1, norm_mlp_w[l][None], "rms_mlp_fwd")
        a_up = _mm_up(hm, wup_l, 0, "nn", "mlp_up")
        r_up = _relu2_fwd(a_up, "relu2_fwd")
        x2 = _mm_wl(r_up, wdn_l, 0, "nn", "mlp_down", add=x1)
        saved.append(dict(x=xcur, h=h, u=u, ypre=ypre, st=st, attn_args=attn_args, conf_p=conf_p, ycat=ycat, x1=x1,
                          hm=hm, a_up=a_up, r_up=r_up, ssd_p=ssd_p, w_perm=w_perm, wout_l=wout_l, wdn_l=wdn_l,
                          wup_l=wup_l))
        xcur = x2

    lsum, dx, dxb = _loss_bwd(xcur, loss_target[0], "loss")

    gbig = [None] * DEPTH
    gsm = {name: [] for name, _ in SMALL}
    for l in reversed(range(DEPTH)):
        sv = saved[l]
        dr = _mm_wl(dxb, sv["wdn_l"], 0, "nt", "mlp_down_dx")
        dwdn, dwdn_b = _mm_dw(sv["r_up"], dxb, "mlp_down_dw")
        da = _relu2_bwd(dr, sv["a_up"], "relu2_bwd")
        dwup, dwup_b = _mm_dw(sv["hm"], da, "mlp_up_dw", col_shards=True)
        dhm = _mm_up(da, sv["wup_l"], 0, "nt", "mlp_up_dx")
        dx1, dx1b, dnw = _rms_bwd(sv["x1"], norm_mlp_w[l][None], dhm, dx, "rms_mlp_bwd")
        gsm["norm_mlp_w"].append(dnw[0])
        dy = _mm_wl(dx1b, sv["wout_l"], 0, "nt", "out_proj_dx")
        dwout, dwout_b = _mm_dw(sv["ycat"], dx1b, "out_proj_dw")
        da_c, dg_c, dww, dwb, dlw, dlb = _conf_bwd(sv["u"], *sv["conf_p"], dy, "conf_bwd")
        gsm["cm_dw_w"].append(dww[:CMK])
        gsm["cm_dw_b"].append(dwb[0])
        gsm["cm_ln_w"].append(dlw[0])
        gsm["cm_ln_b"].append(dlb[0])
        do3 = _heads(dy[:, SSD_W:SSD_W + ATT_W], NQH, HD)
        (dqa, dqb, dka, dkb, dv3, dqwa, dqwb, dkwa, dkwb, dsk) = _attn_bwd(*sv["attn_args"], do3, "attn_bwd")
        dq = _unheads(jnp.concatenate([dqa, dqb], axis=-1))
        dk = _unheads(jnp.concatenate([dka, dkb], axis=-1))
        dv = _unheads(dv3)
        gsm["q_norm_w"].append(jnp.concatenate([dqwa[0], dqwb[0]]))
        gsm["k_norm_w"].append(jnp.concatenate([dkwa[0], dkwb[0]]))
        gsm["attn_sinks"].append(dsk[:, 0, 0])
        (dz, dxr, ddtr, dcw, dcb, ddtb, dalog, ddd, dnsw) = _ssd_bwd(
            sv["u"], sv["ypre"], sv["st"], dy, *sv["ssd_p"], consts, "ssd_bwd")
        gsm["ssd_conv_w"].append(dcw[:4])
        gsm["ssd_conv_b"].append(dcb[0])
        gsm["ssd_dt_bias"].append(ddtb[0, :NH])
        gsm["ssd_a_log"].append(dalog[0, :NH])
        gsm["ssd_d"].append(ddd[0, :NH])
        gsm["ssd_norm_w"].append(dnsw[0])
        du = jnp.concatenate([dxr, dz, da_c, dg_c, dq.astype(BF16), dk.astype(BF16), dv.astype(BF16), ddtr], axis=1)
        dwp, dwp_b = _mm_dw(sv["h"], du, "in_dw", tn=640)

        def unperm(t):
            t = jnp.concatenate([t[:, U_Z:U_A], t[:, U_X:U_Z], t[:, U_DT:U_DT + NH], t[:, U_Q:U_DT], t[:, U_A:U_Q]],
                                axis=1)
            return jnp.transpose(t.reshape(D, 4, N_IN // 4), (1, 0, 2))

        dh = _mm(du, sv["w_perm"], "nt", "in_dx")
        dx, dxb, dnm = _rms_bwd(sv["x"], norm_mix_w[l][None], dh, dx1, "rms_mix_bwd")
        gsm["norm_mix_w"].append(dnm[0])
        gbig[l] = _reduce_scatter_layer(
            [unperm(dwp), dwout.reshape(4, D // 2, D), dwup, dwdn.reshape(4, D, D)],
            [unperm(dwp_b), dwout_b.reshape(4, D // 2, D), dwup_b, dwdn_b.reshape(4, D, D)], "l%d" % l)

    gsm = {k: jnp.stack(v[::-1]) for k, v in gsm.items()}
    packed = _pack_small(gsm, [n for n, _ in SMALL])
    packed = jnp.concatenate([packed, lsum], axis=0)
    red = _allreduce_small(packed, "ar_small")
    loss = 0.5 * red[-8, 0] / D
    gsm = _unpack_small(red[:-8], [n for n, _ in SMALL])
    gsm["ssd_conv_w"] = lax.dynamic_slice_in_dim(gsm["ssd_conv_w"], shard * (XBC // 4), XBC // 4, axis=2)
    gsm["cm_dw_w"] = lax.dynamic_slice_in_dim(gsm["cm_dw_w"], shard * (CMC // 4), CMC // 4, axis=2)
    grads = dict(gsm)
    for w, n in enumerate(("w_in", "w_out", "w_mlp_up", "w_mlp_down")):
        grads[n] = jnp.stack([gbig[l][w] for l in range(DEPTH)])

    loc = locals()
    names = ["norm_mix_w", "w_in", "ssd_conv_w", "ssd_conv_b", "ssd_dt_bias", "ssd_a_log", "ssd_d", "ssd_norm_w",
             "q_norm_w", "k_norm_w", "attn_sinks", "cm_dw_w", "cm_dw_b", "cm_ln_w", "cm_ln_b", "w_out", "norm_mlp_w",
             "w_mlp_up", "w_mlp_down"]
    weights = {n: loc[n] for n in names}
    moms = {n: loc["m_" + n] for n in names}
    vars_ = {n: loc["v_" + n] for n in names}
    delta, new_m, new_v = {}, {}, {}
    packed_names = [n for n, _ in SMALL if n not in SHARDED_SMALL]
    pw = _pack_small(weights, packed_names)
    pg = _pack_small(grads, packed_names)
    pm = _pack_small(moms, packed_names)
    pv = _pack_small(vars_, packed_names)
    pd, pmn, pvn = _adamw(pw, pg, pm, pv, "adamw_small")
    for dst, buf in ((delta, pd), (new_m, pmn), (new_v, pvn)):
        dst.update(_unpack_small(buf, packed_names))
    for n in ("w_in", "w_out", "w_mlp_up", "w_mlp_down", "ssd_conv_w", "cm_dw_w"):
        shp = weights[n].shape
        flat = lambda t: t.reshape(-1, shp[-1])
        d_, m_, v_ = _adamw(flat(weights[n]), flat(grads[n]), flat(moms[n]), flat(vars_[n]), "adamw_" + n)
        delta[n], new_m[n], new_v[n] = d_.reshape(shp), m_.reshape(shp), v_.reshape(shp)

    return (loss, dx[None], *[grads[n] for n in names], *[delta[n] for n in names],
            *[new_m[n] for n in names], *[new_v[n] for n in names])
```

```python
import functools
import math

import jax
import jax.numpy as jnp
from jax import lax
from jax.experimental import pallas as pl
from jax.experimental.pallas import tpu as pltpu

F32 = jnp.float32
BF16 = jnp.bfloat16
MESH = pl.DeviceIdType.MESH
ANY = pl.BlockSpec(memory_space=pl.ANY)
VMEM_SPEC = pl.BlockSpec(memory_space=pltpu.VMEM)

D = 1024
L = 2048
DEPTH = 4
SSD_W = 1024
XBC = 1536
NH = 16
HP = 64
NS = 128
Q = 128
NC = L // Q
ATT_W = 512
NQH = 8
NKV = 2
HD = 64
HH = HD // 2
CMC = 512
CMK = 31
DFF = 4096
N_IN = 4368
N_PAD = 4480
RMS_EPS = 1e-6
LN_EPS = 1e-5
NEG = -1e30
LR, B1, B2, EPS_A, WD, STEP = 0.001, 0.9, 0.999, 1e-8, 0.01, 10
VMEM_LIMIT = 56 * 1024 * 1024
N_DEV = 8


def _cp(sem=None):
    kw = dict(vmem_limit_bytes=VMEM_LIMIT)
    if sem is not None:
        kw["dimension_semantics"] = sem
    return pltpu.CompilerParams(**kw)


def _dg(a, b, ca, cb):
    return lax.dot_general(a, b, (((ca,), (cb,)), ((), ())), preferred_element_type=F32)


def _split3(x):
    hi = x.astype(BF16)
    r = x - hi.astype(F32)
    mid = r.astype(BF16)
    lo = (r - mid.astype(F32)).astype(BF16)
    return hi, mid, lo


def _xdot_l(x, m, ca=1, cb=0):
    hi, mid, lo = _split3(x)
    return _dg(hi, m, ca, cb) + _dg(mid, m, ca, cb) + _dg(lo, m, ca, cb)


def _xdot_r(m, x, ca=1, cb=0):
    hi, mid, lo = _split3(x)
    return _dg(m, hi, ca, cb) + _dg(m, mid, ca, cb) + _dg(m, lo, ca, cb)


def _rowdot(v, m):
    return _xdot_l(jnp.broadcast_to(v, (8, v.shape[1])), m)[0:1]


def _sigmoid(x):
    return 1.0 / (1.0 + jnp.exp(-x))


def _softplus(x):
    e = jnp.exp(-jnp.abs(x))
    u = 1.0 + e
    l1p = jnp.where(u == 1.0, e, jnp.log(u) * (e / jnp.where(u == 1.0, 1.0, u - 1.0)))
    return jnp.maximum(x, 0.0) + l1p


def _mm(a, b, mode, name, add=None, out_dtype=F32, tm=512, tn=512):
    if mode == "nn":
        m, k = a.shape
        n = b.shape[1]
        a_spec = pl.BlockSpec((tm, k), lambda i, j: (i, 0))
        b_spec = pl.BlockSpec((k, tn), lambda i, j: (0, j))
        ca, cb = 1, 0
    elif mode == "nt":
        m, k = a.shape
        n = b.shape[0]
        a_spec = pl.BlockSpec((tm, k), lambda i, j: (i, 0))
        b_spec = pl.BlockSpec((tn, k), lambda i, j: (j, 0))
        ca, cb = 1, 1
    else:
        k, m = a.shape
        n = b.shape[1]
        a_spec = pl.BlockSpec((k, tm), lambda i, j: (0, i))
        b_spec = pl.BlockSpec((k, tn), lambda i, j: (0, j))
        ca, cb = 0, 0
    assert m % tm == 0 and n % tn == 0, (m, n, tm, tn)
    o_spec = pl.BlockSpec((tm, tn), lambda i, j: (i, j))

    if add is None:
        def body(a_ref, b_ref, o_ref):
            o_ref[...] = _dg(a_ref[...], b_ref[...], ca, cb).astype(o_ref.dtype)
        ins, specs = (a, b), [a_spec, b_spec]
    else:
        def body(a_ref, b_ref, c_ref, o_ref):
            o_ref[...] = (_dg(a_ref[...], b_ref[...], ca, cb) + c_ref[...]).astype(o_ref.dtype)
        ins, specs = (a, b, add), [a_spec, b_spec, o_spec]

    return pl.pallas_call(
        body, name=name, grid=(m // tm, n // tn), in_specs=specs, out_specs=o_spec,
        out_shape=jax.ShapeDtypeStruct((m, n), out_dtype),
        compiler_params=_cp(("parallel", "parallel")))(*ins)


def _mm_wl(a, b, layer, mode, name, add=None, relu2_of=None, tm=512, tn=512):
    m, k = a.shape
    a_spec = pl.BlockSpec((tm, k), lambda i, j: (i, 0))
    if mode == "nn":
        n = b.shape[2]
        b_spec = pl.BlockSpec((1, k, tn), lambda i, j: (layer, 0, j))
        cb = 0
    else:
        n = b.shape[1]
        b_spec = pl.BlockSpec((1, tn, k), lambda i, j: (layer, j, 0))
        cb = 1
    o_spec = pl.BlockSpec((tm, tn), lambda i, j: (i, j))
    out_dtype = F32
    if relu2_of is not None:
        def body(a_ref, b_ref, c_ref, o_ref):
            o_ref[...] = (_dg(a_ref[...], b_ref[0], 1, cb) * 2.0 * jnp.maximum(c_ref[...], 0.0)).astype(BF16)
        ins, specs, out_dtype = (a, b, relu2_of), [a_spec, b_spec, o_spec], BF16
    elif add is None:
        def body(a_ref, b_ref, o_ref):
            o_ref[...] = _dg(a_ref[...], b_ref[0], 1, cb)
        ins, specs = (a, b), [a_spec, b_spec]
    else:
        def body(a_ref, b_ref, c_ref, o_ref):
            o_ref[...] = _dg(a_ref[...], b_ref[0], 1, cb) + c_ref[...]
        ins, specs = (a, b, add), [a_spec, b_spec, o_spec]
    return pl.pallas_call(
        body, name=name, grid=(m // tm, n // tn), in_specs=specs, out_specs=o_spec,
        out_shape=jax.ShapeDtypeStruct((m, n), out_dtype), compiler_params=_cp(("parallel", "parallel")))(*ins)


def _mm_up(a, b, layer, mode, name, tm=512, tn=512):
    m = a.shape[0]
    cs = DFF // 4
    o_spec = pl.BlockSpec((tm, tn), lambda i, j: (i, j))
    if mode == "nn":
        per = cs // tn
        a_spec = pl.BlockSpec((tm, D), lambda i, j: (i, 0))
        b_spec = pl.BlockSpec((1, 1, D, tn), lambda i, j: (layer, j // per, 0, j % per))

        def body(a_ref, b_ref, o_ref, r_ref):
            acc = _dg(a_ref[...], b_ref[0, 0], 1, 0)
            o_ref[...] = acc
            r = jnp.maximum(acc, 0.0)
            r_ref[...] = (r * r).astype(BF16)

        return pl.pallas_call(
            body, name=name, grid=(m // tm, DFF // tn), in_specs=[a_spec, b_spec], out_specs=[o_spec, o_spec],
            out_shape=[jax.ShapeDtypeStruct((m, DFF), F32), jax.ShapeDtypeStruct((m, DFF), BF16)],
            compiler_params=_cp(("parallel", "parallel")))(a, b)
    else:
        n = D
        a_spec = pl.BlockSpec((tm, DFF), lambda i, j: (i, 0))
        b_spec = pl.BlockSpec((1, 4, tn, cs), lambda i, j: (layer, 0, j, 0))

        def body(a_ref, b_ref, o_ref):
            acc = _dg(a_ref[:, 0:cs], b_ref[0, 0], 1, 1)
            for s in range(1, 4):
                acc = acc + _dg(a_ref[:, s * cs:(s + 1) * cs], b_ref[0, s], 1, 1)
            o_ref[...] = acc
    return pl.pallas_call(
        body, name=name, grid=(m // tm, n // tn), in_specs=[a_spec, b_spec], out_specs=o_spec,
        out_shape=jax.ShapeDtypeStruct((m, n), F32), compiler_params=_cp(("parallel", "parallel")))(a, b)


def _mm_dw(a, b, name, col_shards=False, tm=512, tn=512):
    k, m = a.shape
    n = b.shape[1]
    a_spec = pl.BlockSpec((k, tm), lambda i, j: (0, i))
    b_spec = pl.BlockSpec((k, tn), lambda i, j: (0, j))
    if col_shards:
        per = (n // 4) // tn
        o_spec = pl.BlockSpec((1, tm, tn), lambda i, j: (j // per, i, j % per))
        shape = (4, m, n // 4)
    else:
        o_spec = pl.BlockSpec((tm, tn), lambda i, j: (i, j))
        shape = (m, n)

    def body(a_ref, b_ref, o_ref, ob_ref):
        acc = _dg(a_ref[...], b_ref[...], 0, 0).reshape(o_ref.shape)
        o_ref[...] = acc
        ob_ref[...] = acc.astype(BF16)

    return pl.pallas_call(
        body, name=name, grid=(m // tm, n // tn), in_specs=[a_spec, b_spec], out_specs=[o_spec, o_spec],
        out_shape=[jax.ShapeDtypeStruct(shape, F32), jax.ShapeDtypeStruct(shape, BF16)],
        compiler_params=_cp(("parallel", "parallel")))(a, b)


TR = 256


def _rms_fwd(x, w, name):
    def body(x_ref, w_ref, o_ref):
        xv = x_ref[...]
        r = lax.rsqrt(jnp.mean(xv * xv, axis=-1, keepdims=True) + RMS_EPS)
        o_ref[...] = (xv * r * w_ref[...]).astype(BF16)

    return pl.pallas_call(
        body, name=name, grid=(L // TR,),
        in_specs=[pl.BlockSpec((TR, D), lambda i: (i, 0)), pl.BlockSpec((1, D), lambda i: (0, 0))],
        out_specs=pl.BlockSpec((TR, D), lambda i: (i, 0)),
        out_shape=jax.ShapeDtypeStruct((L, D), BF16), compiler_params=_cp(("parallel",)))(x, w)


def _rms_bwd(x, w, dh, dres, name):
    def body(x_ref, w_ref, dh_ref, dr_ref, dx_ref, dxb_ref, dw_ref):
        xv = x_ref[...]
        r = lax.rsqrt(jnp.mean(xv * xv, axis=-1, keepdims=True) + RMS_EPS)
        n = xv * r
        dhv = dh_ref[...]
        g = dhv * w_ref[...]
        dx = dr_ref[...] + r * (g - n * jnp.mean(g * n, axis=-1, keepdims=True))
        dx_ref[...] = dx
        dxb_ref[...] = dx.astype(BF16)

        @pl.when(pl.program_id(0) == 0)
        def _():
            dw_ref[...] = jnp.zeros_like(dw_ref)
        dw_ref[...] += jnp.sum(dhv * n, axis=0, keepdims=True)

    row = pl.BlockSpec((TR, D), lambda i: (i, 0))
    vec = pl.BlockSpec((1, D), lambda i: (0, 0))
    return pl.pallas_call(
        body, name=name, grid=(L // TR,), in_specs=[row, vec, row, row], out_specs=[row, row, vec],
        out_shape=[jax.ShapeDtypeStruct((L, D), F32), jax.ShapeDtypeStruct((L, D), BF16),
                   jax.ShapeDtypeStruct((1, D), F32)],
        compiler_params=_cp(("arbitrary",)))(x, w, dh, dres)


def _loss_bwd(y, t, name):
    def body(y_ref, t_ref, l_ref, d_ref, db_ref):
        e = y_ref[...] - t_ref[...]
        d = e * (1.0 / D)
        d_ref[...] = d
        db_ref[...] = d.astype(BF16)

        @pl.when(pl.program_id(0) == 0)
        def _():
            l_ref[...] = jnp.zeros_like(l_ref)
        s = jnp.sum(jnp.sum(e * e, axis=-1, keepdims=True), axis=0, keepdims=True)
        l_ref[...] += jnp.broadcast_to(s, l_ref.shape)

    row = pl.BlockSpec((TR, D), lambda i: (i, 0))
    tile = pl.BlockSpec((8, 128), lambda i: (0, 0))
    return pl.pallas_call(
        body, name=name, grid=(L // TR,), in_specs=[row, row], out_specs=[tile, row, row],
        out_shape=[jax.ShapeDtypeStruct((8, 128), F32), jax.ShapeDtypeStruct((L, D), F32),
                   jax.ShapeDtypeStruct((L, D), BF16)],
        compiler_params=_cp(("arbitrary",)))(y, t)


def _adamw(w, g, m, v, name):
    rows, cols = w.shape
    tr = rows
    for cand in (512, 256, 128, 64, 32, 16, 8):
        if rows % cand == 0 and cand * cols * 4 <= 2 * 1024 * 1024:
            tr = cand
            break
    c1 = 1.0 / (1.0 - B1 ** STEP)
    c2 = 1.0 / (1.0 - B2 ** STEP)

    def body(w_ref, g_ref, m_ref, v_ref, d_ref, mo_ref, vo_ref):
        gv = g_ref[...]
        mn = B1 * m_ref[...] + (1.0 - B1) * gv
        vn = B2 * v_ref[...] + (1.0 - B2) * (gv * gv)
        mo_ref[...] = mn
        vo_ref[...] = vn
        d_ref[...] = -LR * ((mn * c1) / (jnp.sqrt(vn * c2) + EPS_A) + WD * w_ref[...])

    blk = pl.BlockSpec((tr, cols), lambda i: (i, 0))
    shp = jax.ShapeDtypeStruct((rows, cols), F32)
    return pl.pallas_call(body, name=name, grid=(rows // tr,), in_specs=[blk] * 4, out_specs=[blk] * 3,
                          out_shape=[shp] * 3, compiler_params=_cp(("parallel",)))(w, g, m, v)


CT = 256
CPAD = 32


U_X, U_Z, U_A, U_G, U_Q, U_K, U_V, U_DT = 0, 1536, 2560, 3072, 3584, 4096, 4224, 4352


def _cfull(shape):
    return pl.BlockSpec(shape, lambda i: (0,) * len(shape))


def _conf_in_specs():
    return [pl.BlockSpec((L, CMC), lambda i: (0, U_A // CMC)), pl.BlockSpec((L, CMC), lambda i: (0, U_G // CMC)),
            _cfull((CMK, CMC)), _cfull((1, CMC)), _cfull((1, CMC)), _cfull((1, CMC))]


def _conf_fwd(u, w, b, lw, lb, name):
    def body(a_ref, g_ref, w_ref, b_ref, lw_ref, lb_ref, o_ref, hp_ref, win_ref):
        hp_ref[0:CPAD, :] = jnp.zeros((CPAD, CMC), F32)
        hp_ref[CPAD:, :] = a_ref[...] * _sigmoid(g_ref[...])

        def tile(i, carry):
            base = pl.multiple_of(i * CT, CT)
            win_ref[...] = hp_ref[pl.ds(base, CT + CPAD), :]
            c = jnp.broadcast_to(b_ref[...], (CT, CMC))
            for k in range(CMK):
                c = c + w_ref[k:k + 1, :] * win_ref[pl.ds(2 + k, CT), :]
            mu = jnp.mean(c, axis=-1, keepdims=True)
            cc = c - mu
            var = jnp.mean(cc * cc, axis=-1, keepdims=True)
            l = cc * lax.rsqrt(var + LN_EPS) * lw_ref[...] + lb_ref[...]
            o_ref[pl.ds(base, CT), :] = (l * _sigmoid(l)).astype(BF16)
            return carry

        lax.fori_loop(0, L // CT, tile, 0)

    return pl.pallas_call(
        body, name=name, grid=(1,), in_specs=_conf_in_specs(), out_specs=_cfull((L, CMC)),
        out_shape=jax.ShapeDtypeStruct((L, CMC), BF16),
        scratch_shapes=[pltpu.VMEM((L + CPAD, CMC), F32), pltpu.VMEM((CT + CPAD, CMC), F32)],
        compiler_params=_cp(("arbitrary",)))(u, u, w, b, lw, lb)


def _conf_bwd(u, w, b, lw, lb, dy, name):
    def body(a_ref, g_ref, w_ref, b_ref, lw_ref, lb_ref, dy_ref,
             da_ref, dg_ref, dw_ref, db_ref, dlw_ref, dlb_ref, hp_ref, dcp_ref, win_ref):
        hp_ref[0:CPAD, :] = jnp.zeros((CPAD, CMC), F32)
        hp_ref[CPAD:, :] = a_ref[...] * _sigmoid(g_ref[...])
        dcp_ref[L:, :] = jnp.zeros((CPAD, CMC), F32)
        dw_ref[...] = jnp.zeros_like(dw_ref)
        db_ref[...] = jnp.zeros_like(db_ref)
        dlw_ref[...] = jnp.zeros_like(dlw_ref)
        dlb_ref[...] = jnp.zeros_like(dlb_ref)

        def tile1(i, carry):
            base = pl.multiple_of(i * CT, CT)
            win_ref[...] = hp_ref[pl.ds(base, CT + CPAD), :]
            c = jnp.broadcast_to(b_ref[...], (CT, CMC))
            for k in range(CMK):
                c = c + w_ref[k:k + 1, :] * win_ref[pl.ds(2 + k, CT), :]
            mu = jnp.mean(c, axis=-1, keepdims=True)
            cc = c - mu
            var = jnp.mean(cc * cc, axis=-1, keepdims=True)
            rstd = lax.rsqrt(var + LN_EPS)
            n = cc * rstd
            l = n * lw_ref[...] + lb_ref[...]
            sl = _sigmoid(l)
            dl = dy_ref[pl.ds(base, CT), :] * (sl * (1.0 + l * (1.0 - sl)))
            dlw_ref[...] += jnp.sum(dl * n, axis=0, keepdims=True)
            dlb_ref[...] += jnp.sum(dl, axis=0, keepdims=True)
            dn = dl * lw_ref[...]
            dc = rstd * (dn - jnp.mean(dn, axis=-1, keepdims=True)
                         - n * jnp.mean(dn * n, axis=-1, keepdims=True))
            db_ref[...] += jnp.sum(dc, axis=0, keepdims=True)
            for k in range(CMK):
                dw_ref[k:k + 1, :] += jnp.sum(dc * win_ref[pl.ds(2 + k, CT), :], axis=0, keepdims=True)
            dcp_ref[pl.ds(base, CT), :] = dc
            return carry

        lax.fori_loop(0, L // CT, tile1, 0)

        def tile2(i, carry):
            base = pl.multiple_of(i * CT, CT)
            win_ref[...] = dcp_ref[pl.ds(base, CT + CPAD), :]
            dh = jnp.zeros((CT, CMC), F32)
            for k in range(CMK):
                dh = dh + w_ref[k:k + 1, :] * win_ref[pl.ds(CMK - 1 - k, CT), :]
            av = a_ref[pl.ds(base, CT), :]
            sg = _sigmoid(g_ref[pl.ds(base, CT), :])
            da_ref[pl.ds(base, CT), :] = (dh * sg).astype(BF16)
            dg_ref[pl.ds(base, CT), :] = (dh * av * sg * (1.0 - sg)).astype(BF16)
            return carry

        lax.fori_loop(0, L // CT, tile2, 0)

    vec = jax.ShapeDtypeStruct((1, CMC), F32)
    dy_spec = pl.BlockSpec((L, CMC), lambda i: (0, (SSD_W + ATT_W) // CMC))
    return pl.pallas_call(
        body, name=name, grid=(1,), in_specs=_conf_in_specs() + [dy_spec],
        out_specs=[_cfull((L, CMC)), _cfull((L, CMC)), _cfull((32, CMC)), _cfull((1, CMC)), _cfull((1, CMC)),
                   _cfull((1, CMC))],
        out_shape=[jax.ShapeDtypeStruct((L, CMC), BF16), jax.ShapeDtypeStruct((L, CMC), BF16),
                   jax.ShapeDtypeStruct((32, CMC), F32), vec, vec, vec],
        scratch_shapes=[pltpu.VMEM((L + CPAD, CMC), F32), pltpu.VMEM((L + CPAD, CMC), F32),
                        pltpu.VMEM((CT + CPAD, CMC), F32)],
        compiler_params=_cp(("arbitrary",)))(u, u, w, b, lw, lb, dy)


def _attn_prep(a, b, wa, wb, c, s):
    r = lax.rsqrt((jnp.sum(a * a, axis=-1, keepdims=True) + jnp.sum(b * b, axis=-1, keepdims=True))
                  * (1.0 / HD) + RMS_EPS)
    ha = a * r
    hb = b * r
    na = ha * wa
    nb = hb * wb
    return r, ha, hb, na * c - nb * s, nb * c + na * s


def _attn_scores(qa_n, qb_n, ka_c, kb_c, n, lo, sink):
    nk = ka_c.shape[0]
    s = (_dg(qa_n, ka_c, 1, 1) + _dg(qb_n, kb_c, 1, 1)) * (1.0 / math.sqrt(HD))
    qi = lax.broadcasted_iota(jnp.int32, (Q, nk), 0) + n * Q
    ki = lax.broadcasted_iota(jnp.int32, (Q, nk), 1) + lo
    diff = qi - ki
    s = jnp.where((diff >= 0) & (diff < Q), s, NEG)
    m = jnp.maximum(jnp.max(s, axis=-1, keepdims=True), sink)
    p = jnp.exp(s - m)
    ps = jnp.exp(sink - m)
    den = jnp.sum(p, axis=-1, keepdims=True) + ps
    return p / den, ps / den


def _attn_specs():
    qh = pl.BlockSpec((1, L, HH), lambda j: (j, 0, 0))
    kh = pl.BlockSpec((1, L, HH), lambda j: (j // 4, 0, 0))
    vv = pl.BlockSpec((1, L, HD), lambda j: (j // 4, 0, 0))
    tab = pl.BlockSpec((L, HH), lambda j: (0, 0))
    wv = pl.BlockSpec((1, HH), lambda j: (0, 0))
    sk = pl.BlockSpec((1, 1, 128), lambda j: (j, 0, 0))
    return qh, kh, vv, tab, wv, sk


def _attn_fwd(qa, qb, ka, kb, v, cos, sin, qwa, qwb, kwa, kwb, sinks, name):
    def body(qa_ref, qb_ref, ka_ref, kb_ref, v_ref, c_ref, s_ref, qwa_ref, qwb_ref, kwa_ref, kwb_ref,
             sk_ref, o_ref):
        c = c_ref[...]
        s = s_ref[...]
        _, _, _, qra, qrb = _attn_prep(qa_ref[0], qb_ref[0], qwa_ref[...], qwb_ref[...], c, s)
        _, _, _, kra, krb = _attn_prep(ka_ref[0], kb_ref[0], kwa_ref[...], kwb_ref[...], c, s)
        qra, qrb, kra, krb = (t.astype(BF16) for t in (qra, qrb, kra, krb))
        vb = v_ref[0].astype(BF16)
        sink = sk_ref[0][:, 0:1]
        for n in range(NC):
            lo = max(n - 1, 0) * Q
            hi = (n + 1) * Q
            p, _ = _attn_scores(qra[n * Q:hi], qrb[n * Q:hi], kra[lo:hi], krb[lo:hi], n, lo, sink)
            o_ref[0, n * Q:hi, :] = _dg(p.astype(BF16), vb[lo:hi], 1, 0)

    qh, kh, vv, tab, wv, sk = _attn_specs()
    return pl.pallas_call(
        body, name=name, grid=(NQH,),
        in_specs=[qh, qh, kh, kh, vv, tab, tab, wv, wv, wv, wv, sk],
        out_specs=pl.BlockSpec((1, L, HD), lambda j: (j, 0, 0)),
        out_shape=jax.ShapeDtypeStruct((NQH, L, HD), F32),
        compiler_params=_cp(("parallel",)))(qa, qb, ka, kb, v, cos, sin, qwa, qwb, kwa, kwb, sinks)


def _attn_bwd(qa, qb, ka, kb, v, cos, sin, qwa, qwb, kwa, kwb, sinks, do, name):
    def body(qa_ref, qb_ref, ka_ref, kb_ref, v_ref, c_ref, s_ref, qwa_ref, qwb_ref, kwa_ref, kwb_ref,
             sk_ref, do_ref,
             dqa_ref, dqb_ref, dka_ref, dkb_ref, dv_ref, dqwa_ref, dqwb_ref, dkwa_ref, dkwb_ref, dsk_ref,
             dqra_s, dqrb_s, dkra_s, dkrb_s, dv_s):
        j = pl.program_id(0)
        c = c_ref[...]
        s = s_ref[...]
        qwa, qwb, kwa, kwb = qwa_ref[...], qwb_ref[...], kwa_ref[...], kwb_ref[...]
        qr, qha, qhb, qra, qrb = _attn_prep(qa_ref[0], qb_ref[0], qwa, qwb, c, s)
        kr, kha, khb, kra, krb = _attn_prep(ka_ref[0], kb_ref[0], kwa, kwb, c, s)
        qra, qrb, kra, krb = (t.astype(BF16) for t in (qra, qrb, kra, krb))
        vb = v_ref[0].astype(BF16)
        sink = sk_ref[0][:, 0:1]
        dkra_s[...] = jnp.zeros_like(dkra_s)
        dkrb_s[...] = jnp.zeros_like(dkrb_s)
        dv_s[...] = jnp.zeros_like(dv_s)
        dsink = jnp.zeros((1, 1), F32)
        scale = 1.0 / math.sqrt(HD)
        for n in range(NC):
            lo = max(n - 1, 0) * Q
            hi = (n + 1) * Q
            p, ps = _attn_scores(qra[n * Q:hi], qrb[n * Q:hi], kra[lo:hi], krb[lo:hi], n, lo, sink)
            don = do_ref[0, n * Q:hi, :]
            dob = don.astype(BF16)
            pb = p.astype(BF16)
            dv_s[lo:hi, :] += _dg(pb, dob, 0, 0)
            dp = _dg(dob, vb[lo:hi], 1, 1)
            delta = jnp.sum(p * dp, axis=-1, keepdims=True)
            dsink = dsink - jnp.sum(ps * delta, axis=0, keepdims=True)
            ds = (p * (dp - delta) * scale).astype(BF16)
            dqra_s[n * Q:hi, :] = _dg(ds, kra[lo:hi], 1, 0)
            dqrb_s[n * Q:hi, :] = _dg(ds, krb[lo:hi], 1, 0)
            dkra_s[lo:hi, :] += _dg(ds, qra[n * Q:hi], 0, 0)
            dkrb_s[lo:hi, :] += _dg(ds, qrb[n * Q:hi], 0, 0)

        def unprep(dra, drb, r, ha, hb, wa, wb):
            dna = dra * c + drb * s
            dnb = drb * c - dra * s
            dwa = jnp.sum(dna * ha, axis=0, keepdims=True)
            dwb = jnp.sum(dnb * hb, axis=0, keepdims=True)
            ga = dna * wa
            gb = dnb * wb
            mu = (jnp.sum(ga * ha, axis=-1, keepdims=True) + jnp.sum(gb * hb, axis=-1, keepdims=True)) * (1.0 / HD)
            return r * (ga - ha * mu), r * (gb - hb * mu), dwa, dwb

        dqa, dqb, dqwa, dqwb = unprep(dqra_s[...], dqrb_s[...], qr, qha, qhb, qwa, qwb)
        dka, dkb, dkwa, dkwb = unprep(dkra_s[...], dkrb_s[...], kr, kha, khb, kwa, kwb)
        dqa_ref[0] = dqa
        dqb_ref[0] = dqb
        dsk_ref[0] = jnp.broadcast_to(dsink, (1, 128))

        @pl.when(j == 0)
        def _():
            dqwa_ref[...] = jnp.zeros_like(dqwa_ref)
            dqwb_ref[...] = jnp.zeros_like(dqwb_ref)
            dkwa_ref[...] = jnp.zeros_like(dkwa_ref)
            dkwb_ref[...] = jnp.zeros_like(dkwb_ref)
        dqwa_ref[...] += dqwa
        dqwb_ref[...] += dqwb
        dkwa_ref[...] += dkwa
        dkwb_ref[...] += dkwb

        @pl.when(j % 4 == 0)
        def _():
            dka_ref[0] = dka
            dkb_ref[0] = dkb
            dv_ref[0] = dv_s[...]

        @pl.when(j % 4 != 0)
        def _():
            dka_ref[0] += dka
            dkb_ref[0] += dkb
            dv_ref[0] += dv_s[...]

    qh, kh, vv, tab, wv, sk = _attn_specs()
    qo = pl.BlockSpec((1, L, HD), lambda j: (j, 0, 0))
    hvec = jax.ShapeDtypeStruct((1, HH), F32)
    return pl.pallas_call(
        body, name=name, grid=(NQH,),
        in_specs=[qh, qh, kh, kh, vv, tab, tab, wv, wv, wv, wv, sk, qo],
        out_specs=[qh, qh, kh, kh, vv, wv, wv, wv, wv, sk],
        out_shape=[jax.ShapeDtypeStruct((NQH, L, HH), F32), jax.ShapeDtypeStruct((NQH, L, HH), F32),
                   jax.ShapeDtypeStruct((NKV, L, HH), F32), jax.ShapeDtypeStruct((NKV, L, HH), F32),
                   jax.ShapeDtypeStruct((NKV, L, HD), F32), hvec, hvec, hvec, hvec,
                   jax.ShapeDtypeStruct((NQH, 1, 128), F32)],
        scratch_shapes=[pltpu.VMEM((L, HH), F32), pltpu.VMEM((L, HH), F32), pltpu.VMEM((L, HH), F32),
                        pltpu.VMEM((L, HH), F32), pltpu.VMEM((L, HD), F32)],
        compiler_params=_cp(("arbitrary",)))(qa, qb, ka, kb, v, cos, sin, qwa, qwb, kwa, kwb, sinks, do)


def _ssd_consts():
    hh = jnp.arange(128)[:, None]
    e = (hh == (jnp.arange(SSD_W)[None, :] // HP)).astype(BF16)
    e2 = (hh == (jnp.arange(NH * 128)[None, :] // 128)).astype(BF16)
    et = e.T
    tril = (jnp.arange(Q)[:, None] >= jnp.arange(Q)[None, :]).astype(BF16)
    triu = tril.T
    eye = jnp.eye(128, dtype=BF16)
    return e, e2, et, tril, triu, eye


def _ssd_common(x_ref, ext_scr, cw_ref, cb_ref, dt_ref, dtb_ref, alog_ref, e_ref, e2_ref, tril_ref, triu_ref,
                arow_scr, acol_scr, eax_scr):
    conv = jnp.broadcast_to(cb_ref[...], (Q, XBC))
    for k in range(4):
        conv = conv + cw_ref[k:k + 1, :] * ext_scr[pl.ds(5 + k, Q), :]
    sg = _sigmoid(conv)
    xbc = conv * sg
    dtpre = dt_ref[...] + dtb_ref[...]
    dt = _softplus(dtpre)
    a = -jnp.exp(alog_ref[...])
    adt = dt * a
    acol = _xdot_r(tril_ref[...], adt)
    acol_scr[...] = acol
    arow_scr[...] = _xdot_l(adt, triu_ref[...], 0, 0)
    alast = acol_scr[Q - 1:Q, :]
    ea = jnp.exp(acol)
    decs = jnp.exp(alast - acol)
    e = e_ref[...]
    dt_x = _xdot_l(dt, e)
    eax_scr[...] = _xdot_l(ea, e)
    decs_x = _xdot_l(decs, e)
    acx2 = _xdot_l(acol, e2_ref[...])
    return conv, sg, xbc, dtpre, dt, a, adt, acol, alast, ea, decs, dt_x, decs_x, acx2


def _ssd_fwd(u, cw, cb, dtb, alog, dxp, nw, consts, name):
    e, e2, et, tril, triu, eye = consts

    def body(z0_ref, z1_ref, x_ref, dt_ref, cw_ref, cb_ref, dtb_ref, alog_ref, dx_ref, nw_ref, e_ref, e2_ref,
             tril_ref, triu_ref, ya_ref, ypre_ref, st_ref, s_scr, ext_scr, arow_scr, acol_scr, eax_scr):
        c = pl.program_id(0)

        @pl.when(c == 0)
        def _():
            s_scr[...] = jnp.zeros_like(s_scr)
            ext_scr[0:8, :] = jnp.zeros((8, XBC), F32)
        ext_scr[8:8 + Q, :] = x_ref[...]
        (conv, sg, xbc, dtpre, dt, a, adt, acol, alast, ea, decs, dt_x, decs_x, acx2) = _ssd_common(
            x_ref, ext_scr, cw_ref, cb_ref, dt_ref, dtb_ref, alog_ref, e_ref, e2_ref, tril_ref, triu_ref,
            arow_scr, acol_scr, eax_scr)
        ext_scr[0:8, :] = ext_scr[Q:Q + 8, :]
        xs = xbc[:, :SSD_W]
        xdt = xs * dt_x
        lane = lax.broadcasted_iota(jnp.int32, (Q, 128), 1)
        causal = lax.broadcasted_iota(jnp.int32, (Q, Q), 0) >= lax.broadcasted_iota(jnp.int32, (Q, Q), 1)
        for g in range(2):
            bg = xbc[:, SSD_W + g * NS:SSD_W + (g + 1) * NS].astype(BF16)
            cg = xbc[:, SSD_W + 2 * NS + g * NS:SSD_W + 2 * NS + (g + 1) * NS].astype(BF16)
            cbm = _dg(cg, bg, 1, 1)
            sgv = s_scr[g]
            st_ref[0, g] = sgv
            gc = slice(g * 512, (g + 1) * 512)
            yoff = _dg(cg, sgv.astype(BF16), 1, 0) * eax_scr[:, gc]
            for pr in range(4):
                h0 = g * 8 + 2 * pr
                h1 = h0 + 1
                c0 = g * 512 + pr * 128
                xp = xdt[:, c0:c0 + 128].astype(BF16)
                w0 = (cbm * jnp.exp(jnp.where(causal, acx2[:, h0 * 128:(h0 + 1) * 128] - arow_scr[h0:h0 + 1, :],
                                              NEG))).astype(BF16)
                w1 = (cbm * jnp.exp(jnp.where(causal, acx2[:, h1 * 128:(h1 + 1) * 128] - arow_scr[h1:h1 + 1, :],
                                              NEG))).astype(BF16)
                yd = jnp.where(lane < HP, _dg(w0, xp, 1, 0), _dg(w1, xp, 1, 0))
                ypre_ref[:, c0:c0 + 128] = (yd + yoff[:, pr * 128:(pr + 1) * 128]
                                            + xs[:, c0:c0 + 128] * dx_ref[:, c0:c0 + 128])
            contrib = _dg(bg, (xdt[:, gc] * decs_x[:, gc]).astype(BF16), 0, 0)
            s_scr[g] = sgv * eax_scr[Q - 1:Q, gc] + contrib
        for g, zr in enumerate((z0_ref, z1_ref)):
            gc = slice(g * 512, (g + 1) * 512)
            zz = zr[...]
            ggg = ypre_ref[:, gc] * (zz * _sigmoid(zz))
            rstd = lax.rsqrt(jnp.mean(ggg * ggg, axis=-1, keepdims=True) + RMS_EPS)
            ya_ref[:, gc] = (ggg * rstd * nw_ref[:, gc]).astype(BF16)

    def row(w, blk=0):
        return pl.BlockSpec((Q, w), lambda c: (c, blk))

    def full(shape):
        return pl.BlockSpec(shape, lambda c: (0,) * len(shape))

    return pl.pallas_call(
        body, name=name, grid=(NC,),
        in_specs=[row(512, U_Z // 512), row(512, U_Z // 512 + 1), row(XBC, U_X // XBC), row(128, U_DT // 128),
                  full((4, XBC)), full((1, XBC)), full((1, 128)), full((1, 128)),
                  full((1, SSD_W)), full((1, SSD_W)), full((128, SSD_W)), full((128, NH * 128)), full((Q, Q)),
                  full((Q, Q))],
        out_specs=[row(SSD_W), row(SSD_W), pl.BlockSpec((1, 2, NS, 512), lambda c: (c, 0, 0, 0))],
        out_shape=[jax.ShapeDtypeStruct((L, SSD_W), BF16), jax.ShapeDtypeStruct((L, SSD_W), F32),
                   jax.ShapeDtypeStruct((NC, 2, NS, 512), F32)],
        scratch_shapes=[pltpu.VMEM((2, NS, 512), F32), pltpu.VMEM((Q + 8, XBC), F32), pltpu.VMEM((128, Q), F32),
                        pltpu.VMEM((Q, 128), F32), pltpu.VMEM((Q, SSD_W), F32)],
        compiler_params=_cp(("arbitrary",)))(u, u, u, u, cw, cb, dtb, alog, dxp, nw, e, e2, tril, triu)


def _ssd_bwd(u, ypre, st, dy, cw, cb, dtb, alog, dxp, nw, consts, name):
    e, e2, et, tril, triu, eye = consts

    def body(z0_ref, z1_ref, x_ref, xp_ref, dt_ref, ypre_ref, st_ref, dya_ref, cw_ref, cb_ref, dtb_ref, alog_ref, dx_ref,
             nw_ref, e_ref, e2_ref, et_ref, tril_ref, triu_ref, eye_ref,
             dz_ref, dxr_ref, ddtr_ref, dcw_ref, dcb_ref, ddtb_ref, dalog_ref, dd_ref, dnw_ref,
             g_scr, ext_scr, ext2_scr, arow_scr, acol_scr, eax_scr, darow_scr, dxdt_scr, t1_scr, t2_scr, dgg_scr):
        i = pl.program_id(0)

        @pl.when(i == 0)
        def _():
            g_scr[...] = jnp.zeros_like(g_scr)
            ext2_scr[Q:Q + 8, :] = jnp.zeros((8, XBC), F32)
            for r in (dcw_ref, dcb_ref, ddtb_ref, dalog_ref, dd_ref, dnw_ref):
                r[...] = jnp.zeros_like(r)
        not_first = jnp.where(i < NC - 1, 1.0, 0.0)
        ext_scr[0:8, :] = xp_ref[Q - 8:Q, :] * not_first
        ext_scr[8:8 + Q, :] = x_ref[...]
        (conv, sg, xbc, dtpre, dt, a, adt, acol, alast, ea, decs, dt_x, decs_x, acx2) = _ssd_common(
            x_ref, ext_scr, cw_ref, cb_ref, dt_ref, dtb_ref, alog_ref, e_ref, e2_ref, tril_ref, triu_ref,
            arow_scr, acol_scr, eax_scr)
        et_m = et_ref[...]
        xs = xbc[:, :SSD_W]
        xdt = xs * dt_x
        y = ypre_ref[...]
        zz = jnp.concatenate([z0_ref[...], z1_ref[...]], axis=1)
        sz = _sigmoid(zz)
        silu_z = zz * sz
        gg = y * silu_z
        dya = dya_ref[...]
        for g in range(2):
            gc = slice(g * 512, (g + 1) * 512)
            ggg = gg[:, gc]
            rstd = lax.rsqrt(jnp.mean(ggg * ggg, axis=-1, keepdims=True) + RMS_EPS)
            n = ggg * rstd
            dyag = dya[:, gc]
            dnw_ref[:, gc] += jnp.sum(dyag * n, axis=0, keepdims=True)
            dn = dyag * nw_ref[:, gc]
            dgg_scr[:, gc] = rstd * (dn - n * jnp.mean(dn * n, axis=-1, keepdims=True))
        dgg = dgg_scr[...]
        dy = dgg * silu_z
        dz_ref[...] = (dgg * y * (sz * (1.0 + zz * (1.0 - sz)))).astype(BF16)
        dd_ref[...] += _rowdot(jnp.sum(dy * xs, axis=0, keepdims=True), et_m)
        dxs = dy * dx_ref[...]
        dys = dy * eax_scr[...]
        lane = lax.broadcasted_iota(jnp.int32, (Q, 128), 1)
        causal = lax.broadcasted_iota(jnp.int32, (Q, Q), 0) >= lax.broadcasted_iota(jnp.int32, (Q, Q), 1)
        darow_scr[...] = jnp.zeros_like(darow_scr)
        dacol = jnp.zeros((Q, 128), F32)
        dcdx = []
        dbs = []
        dcs = []
        for g in range(2):
            gc = slice(g * 512, (g + 1) * 512)
            bg = xbc[:, SSD_W + g * NS:SSD_W + (g + 1) * NS].astype(BF16)
            cg = xbc[:, SSD_W + 2 * NS + g * NS:SSD_W + 2 * NS + (g + 1) * NS].astype(BF16)
            cbm = _dg(cg, bg, 1, 1)
            sgv = st_ref[0, g]
            sgb = sgv.astype(BF16)
            gv = g_scr[g]
            gvb = gv.astype(BF16)
            yoff = _dg(cg, sgb, 1, 0) * eax_scr[:, gc]
            dysg = dys[:, gc].astype(BF16)
            dcg = _dg(dysg, sgb, 1, 1)
            ds_off = _dg(cg, dysg, 0, 0)
            t1_scr[:, gc] = dy[:, gc] * yoff
            xdec = xdt[:, gc] * decs_x[:, gc]
            dxd = _dg(bg, gvb, 1, 0)
            dbg = _dg(xdec.astype(BF16), gvb, 1, 1)
            dxdt_g = dxd * decs_x[:, gc]
            t2_scr[:, gc] = dxd * xdt[:, gc]
            cdx = eax_scr[Q - 1:Q, gc]
            dcdx.append(jnp.sum(gv * sgv, axis=0, keepdims=True))
            g_scr[g] = gv * cdx + ds_off
            dcb_acc = jnp.zeros((Q, Q), F32)
            for pr in range(4):
                c0 = g * 512 + pr * 128
                xp = xdt[:, c0:c0 + 128].astype(BF16)
                dyp = dy[:, c0:c0 + 128]
                dypb = dyp.astype(BF16)
                halves = []
                for hh, keep in ((g * 8 + 2 * pr, lane < HP), (g * 8 + 2 * pr + 1, lane >= HP)):
                    lam = jnp.exp(jnp.where(causal, acx2[:, hh * 128:(hh + 1) * 128] - arow_scr[hh:hh + 1, :], NEG))
                    w = cbm * lam
                    dw = _dg(jnp.where(keep, dyp, 0.0).astype(BF16), xp, 1, 1)
                    dcb_acc = dcb_acc + dw * lam
                    t = dw * w
                    dacol = dacol + jnp.sum(t, axis=-1, keepdims=True) * (lane == hh).astype(F32)
                    darow_scr[hh:hh + 1, :] -= jnp.sum(t, axis=0, keepdims=True)
                    halves.append(_dg(w.astype(BF16), dypb, 0, 0))
                dxdt_scr[:, c0:c0 + 128] = (jnp.where(lane < HP, halves[0], halves[1])
                                            + dxdt_g[:, pr * 128:(pr + 1) * 128])
            dcbb = dcb_acc.astype(BF16)
            dcs.append(dcg + _dg(dcbb, bg, 1, 0))
            dbs.append(dbg + _dg(dcbb, cg, 0, 0))
        dacol = dacol + _xdot_l(t1_scr[...], et_m)
        ddecs = _xdot_l(t2_scr[...], et_m) * decs
        dacol = dacol - ddecs
        dalast = jnp.sum(ddecs, axis=0, keepdims=True)
        dcd = _rowdot(jnp.concatenate(dcdx, axis=1), et_m)
        dalast = dalast + dcd * jnp.exp(alast)
        dacol = dacol + _xdot_l(darow_scr[...], eye_ref[...], 0, 0)
        rowi = lax.broadcasted_iota(jnp.int32, (Q, 128), 0)
        dacol = dacol + jnp.where(rowi == Q - 1, dalast, 0.0)
        dadt = _xdot_r(triu_ref[...], dacol)
        dxdt = dxdt_scr[...]
        ddt = dadt * a + _xdot_l(dxdt * xs, et_m)
        dalog_ref[...] += jnp.sum(dadt * dt, axis=0, keepdims=True) * a
        dxs = dxs + dxdt * dt_x
        ddtr = ddt * _sigmoid(dtpre)
        ddtb_ref[...] += jnp.sum(ddtr, axis=0, keepdims=True)
        ddtr_ref[...] = ddtr.astype(BF16)
        dsilu = sg * (1.0 + conv * (1.0 - sg))
        ext2_scr[0:Q, 0:SSD_W] = dxs * dsilu[:, :SSD_W]
        for g in range(2):
            o1 = SSD_W + g * NS
            o2 = SSD_W + 2 * NS + g * NS
            ext2_scr[0:Q, o1:o1 + NS] = dbs[g] * dsilu[:, o1:o1 + NS]
            ext2_scr[0:Q, o2:o2 + NS] = dcs[g] * dsilu[:, o2:o2 + NS]
        dconv = ext2_scr[0:Q, :]
        dcb_ref[...] += jnp.sum(dconv, axis=0, keepdims=True)
        dxr = jnp.zeros((Q, XBC), F32)
        for k in range(4):
            dcw_ref[k:k + 1, :] += jnp.sum(dconv * ext_scr[pl.ds(5 + k, Q), :], axis=0, keepdims=True)
            dxr = dxr + cw_ref[k:k + 1, :] * ext2_scr[pl.ds(3 - k, Q), :]
        dxr_ref[...] = dxr.astype(BF16)
        ext2_scr[Q:Q + 8, :] = ext2_scr[0:8, :]

    def row(w, blk=0):
        return pl.BlockSpec((Q, w), lambda i: (NC - 1 - i, blk))

    def full(shape):
        return pl.BlockSpec(shape, lambda i: (0,) * len(shape))

    prev = pl.BlockSpec((Q, XBC), lambda i: (jnp.maximum(NC - 2 - i, 0), U_X // XBC))
    return pl.pallas_call(
        body, name=name, grid=(NC,),
        in_specs=[row(512, U_Z // 512), row(512, U_Z // 512 + 1), row(XBC, U_X // XBC), prev, row(128, U_DT // 128),
                  row(SSD_W),
                  pl.BlockSpec((1, 2, NS, 512), lambda i: (NC - 1 - i, 0, 0, 0)), row(SSD_W),
                  full((4, XBC)), full((1, XBC)), full((1, 128)), full((1, 128)), full((1, SSD_W)),
                  full((1, SSD_W)), full((128, SSD_W)), full((128, NH * 128)), full((SSD_W, 128)), full((Q, Q)),
                  full((Q, Q)), full((128, 128))],
        out_specs=[row(SSD_W), row(XBC), row(128), full((8, XBC)), full((1, XBC)), full((1, 128)), full((1, 128)),
                   full((1, 128)), full((1, SSD_W))],
        out_shape=[jax.ShapeDtypeStruct((L, SSD_W), BF16), jax.ShapeDtypeStruct((L, XBC), BF16),
                   jax.ShapeDtypeStruct((L, 128), BF16), jax.ShapeDtypeStruct((8, XBC), F32),
                   jax.ShapeDtypeStruct((1, XBC), F32), jax.ShapeDtypeStruct((1, 128), F32),
                   jax.ShapeDtypeStruct((1, 128), F32), jax.ShapeDtypeStruct((1, 128), F32),
                   jax.ShapeDtypeStruct((1, SSD_W), F32)],
        scratch_shapes=[pltpu.VMEM((2, NS, 512), F32), pltpu.VMEM((Q + 8, XBC), F32), pltpu.VMEM((Q + 8, XBC), F32),
                        pltpu.VMEM((128, Q), F32), pltpu.VMEM((Q, 128), F32), pltpu.VMEM((Q, SSD_W), F32),
                        pltpu.VMEM((128, Q), F32), pltpu.VMEM((Q, SSD_W), F32), pltpu.VMEM((Q, SSD_W), F32),
                        pltpu.VMEM((Q, SSD_W), F32), pltpu.VMEM((Q, SSD_W), F32)],
        compiler_params=_cp(("arbitrary",)))(u, u, u, u, u, ypre, st, dy, cw, cb, dtb, alog, dxp, nw,
                                             e, e2, et, tril, triu, eye)


def _my_pos():
    return lax.axis_index("x"), lax.axis_index("y"), lax.axis_index("c")


CHIP_REL = ((1, 0), (0, 1), (1, 1))
CHIP_XOR = (2, 1, 3)
NW = 4
AT = 256


def _chips(x, y):
    return [(1 - x if dx else x, 1 - y if dy else y) for dx, dy in CHIP_REL]


def _gather_layer(shards, name):
    def body(*refs):
        ins, outs = refs[:NW], refs[NW:2 * NW]
        send_sems, recv_sems = refs[2 * NW:]
        x, y, c = _my_pos()
        s = 2 * x + y
        sib = (x, y, 1 - c)
        chips = _chips(x, y)
        sends = []
        for w in range(NW):
            hr = ins[w].shape[0] // 2
            mine = pl.ds(c * hr, hr)
            for k in range(3):
                cp = pltpu.make_async_remote_copy(
                    src_ref=ins[w].at[mine], dst_ref=outs[w].at[s, mine], send_sem=send_sems.at[w, k],
                    recv_sem=recv_sems.at[w, k], device_id=(*chips[k], c), device_id_type=MESH)
                cp.start()
                sends.append(cp)
        for w in range(NW):
            hr = ins[w].shape[0] // 2
            mine = pl.ds(c * hr, hr)
            for k in range(3):
                blk = outs[w].at[s ^ CHIP_XOR[k], mine]
                pltpu.make_async_remote_copy(
                    src_ref=blk, dst_ref=blk, send_sem=send_sems.at[w, k], recv_sem=recv_sems.at[w, k],
                    device_id=(*chips[k], c), device_id_type=MESH).wait_recv()
                fw = pltpu.make_async_remote_copy(
                    src_ref=blk, dst_ref=blk, send_sem=send_sems.at[w, 3 + k], recv_sem=recv_sems.at[w, 3 + k],
                    device_id=sib, device_id_type=MESH)
                fw.start()
                sends.append(fw)
        for w in range(NW):
            hr = ins[w].shape[0] // 2
            other = pl.ds((1 - c) * hr, hr)
            for k in range(3):
                blk = outs[w].at[s ^ CHIP_XOR[k], other]
                pltpu.make_async_remote_copy(
                    src_ref=blk, dst_ref=blk, send_sem=send_sems.at[w, 3 + k], recv_sem=recv_sems.at[w, 3 + k],
                    device_id=sib, device_id_type=MESH).wait_recv()
        for cp in sends:
            cp.wait_send()

    return pl.pallas_call(
        body, name=name, in_specs=[ANY] * NW, out_specs=[ANY] * NW,
        out_shape=[jax.ShapeDtypeStruct((4,) + t.shape, t.dtype) for t in shards],
        scratch_shapes=[pltpu.SemaphoreType.DMA((NW, 6)), pltpu.SemaphoreType.DMA((NW, 6))])(*shards)


def _place_own(shard, gathered, sidx, name):
    r, cc = shard.shape

    def body(s_ref, a_ref, g_ref, o_ref):
        o_ref[0] = a_ref[...]

    return pl.pallas_call(
        body, name=name,
        grid_spec=pltpu.PrefetchScalarGridSpec(
            num_scalar_prefetch=1, grid=(r // AT,),
            in_specs=[pl.BlockSpec((AT, cc), lambda i, s_ref: (i, 0)), ANY],
            out_specs=pl.BlockSpec((1, AT, cc), lambda i, s_ref: (s_ref[0], i, 0))),
        out_shape=jax.ShapeDtypeStruct(gathered.shape, gathered.dtype),
        input_output_aliases={2: 0}, compiler_params=_cp(("parallel",)))(sidx, shard, gathered)


def _rs_pair(dwb, name):
    def body(*refs):
        ins, outs = refs[:NW], refs[NW:2 * NW]
        send_sems, recv_sems = refs[2 * NW:]
        x, y, c = _my_pos()
        cps = []
        for w in range(NW):
            hr = ins[w].shape[1] // 2
            cp = pltpu.make_async_remote_copy(
                src_ref=ins[w].at[:, pl.ds((1 - c) * hr, hr)], dst_ref=outs[w], send_sem=send_sems.at[w],
                recv_sem=recv_sems.at[w], device_id=(x, y, 1 - c), device_id_type=MESH)
            cp.start()
            cps.append(cp)
        for cp in cps:
            cp.wait()

    return pl.pallas_call(
        body, name=name, in_specs=[ANY] * NW, out_specs=[ANY] * NW,
        out_shape=[jax.ShapeDtypeStruct((4, t.shape[1] // 2, t.shape[2]), t.dtype) for t in dwb],
        scratch_shapes=[pltpu.SemaphoreType.DMA((NW,)), pltpu.SemaphoreType.DMA((NW,))])(*dwb)


def _rs_chip(pb, name):
    def body(*refs):
        ins, outs = refs[:NW], refs[NW:2 * NW]
        send_sems, recv_sems = refs[2 * NW:]
        x, y, c = _my_pos()
        s = 2 * x + y
        chips = _chips(x, y)
        cps = []
        for w in range(NW):
            for k in range(3):
                cp = pltpu.make_async_remote_copy(
                    src_ref=ins[w].at[s ^ CHIP_XOR[k]], dst_ref=outs[w].at[k], send_sem=send_sems.at[w, k],
                    recv_sem=recv_sems.at[w, k], device_id=(*chips[k], c), device_id_type=MESH)
                cp.start()
                cps.append(cp)
        for cp in cps:
            cp.wait()

    return pl.pallas_call(
        body, name=name, in_specs=[ANY] * NW, out_specs=[ANY] * NW,
        out_shape=[jax.ShapeDtypeStruct((3,) + t.shape[1:], t.dtype) for t in pb],
        scratch_shapes=[pltpu.SemaphoreType.DMA((NW, 3)), pltpu.SemaphoreType.DMA((NW, 3))])(*pb)


def _rs_sib(q, name):
    def body(*refs):
        ins, outs = refs[:NW], refs[NW:2 * NW]
        send_sems, recv_sems = refs[2 * NW:]
        x, y, c = _my_pos()
        cps = []
        for w in range(NW):
            hr = outs[w].shape[0] // 2
            mine = pl.ds(c * hr, hr)
            cp = pltpu.make_async_remote_copy(
                src_ref=outs[w].at[mine], dst_ref=outs[w].at[mine], send_sem=send_sems.at[w],
                recv_sem=recv_sems.at[w], device_id=(x, y, 1 - c), device_id_type=MESH)
            cp.start()
            cps.append(cp)
        for w in range(NW):
            hr = outs[w].shape[0] // 2
            other = outs[w].at[pl.ds((1 - c) * hr, hr)]
            pltpu.make_async_remote_copy(
                src_ref=other, dst_ref=other, send_sem=send_sems.at[w], recv_sem=recv_sems.at[w],
                device_id=(x, y, 1 - c), device_id_type=MESH).wait_recv()
        for cp in cps:
            cp.wait_send()

    return pl.pallas_call(
        body, name=name, in_specs=[ANY] * NW, out_specs=[ANY] * NW,
        out_shape=[jax.ShapeDtypeStruct(t.shape, t.dtype) for t in q],
        input_output_aliases={w: w for w in range(NW)},
        scratch_shapes=[pltpu.SemaphoreType.DMA((NW,)), pltpu.SemaphoreType.DMA((NW,))])(*q)


def _rs_add2(dw, got, cidx, name):
    _, r, cc = dw.shape
    hr = r // 2
    nb = hr // AT

    def body(c_ref, a_ref, b_ref, o_ref, ob_ref):
        acc = a_ref[...] + b_ref[...].astype(F32)
        o_ref[...] = acc
        ob_ref[...] = acc.astype(BF16)

    blk = pl.BlockSpec((1, AT, cc), lambda sh, i, c_ref: (sh, i, 0))
    return pl.pallas_call(
        body, name=name,
        grid_spec=pltpu.PrefetchScalarGridSpec(
            num_scalar_prefetch=1, grid=(4, nb),
            in_specs=[pl.BlockSpec((1, AT, cc), lambda sh, i, c_ref: (sh, c_ref[0] * nb + i, 0)), blk],
            out_specs=[blk, blk]),
        out_shape=[jax.ShapeDtypeStruct((4, hr, cc), F32), jax.ShapeDtypeStruct((4, hr, cc), BF16)],
        compiler_params=_cp(("parallel", "parallel")))(cidx, dw, got)


def _rs_add4(p, got, scidx, name):
    _, hr, cc = p.shape
    nb = hr // AT

    def body(s_ref, p_ref, g0_ref, g1_ref, g2_ref, o_ref):
        acc = p_ref[0] + g0_ref[0].astype(F32)
        acc = acc + g1_ref[0].astype(F32)
        o_ref[...] = acc + g2_ref[0].astype(F32)

    def gk(k):
        return pl.BlockSpec((1, AT, cc), lambda i, s_ref: (k, i, 0))

    return pl.pallas_call(
        body, name=name,
        grid_spec=pltpu.PrefetchScalarGridSpec(
            num_scalar_prefetch=1, grid=(nb,),
            in_specs=[pl.BlockSpec((1, AT, cc), lambda i, s_ref: (s_ref[0], i, 0)), gk(0), gk(1), gk(2)],
            out_specs=pl.BlockSpec((AT, cc), lambda i, s_ref: (s_ref[1] * nb + i, 0))),
        out_shape=jax.ShapeDtypeStruct((2 * hr, cc), F32),
        compiler_params=_cp(("parallel",)))(scidx, p, got, got, got)


def _reduce_scatter_layer(dws, dwbs, tag):
    x, y, c = _my_pos()
    cidx = jnp.reshape(c, (1,)).astype(jnp.int32)
    scidx = jnp.stack([2 * x + y, c]).astype(jnp.int32)
    got = _rs_pair(dwbs, "rs_pair")
    pairs = [_rs_add2(dws[w], got[w], cidx, "rs_add2_%d" % w) for w in range(NW)]
    recv = _rs_chip([p[1] for p in pairs], "rs_chip")
    q = [_rs_add4(pairs[w][0], recv[w], scidx, "rs_add4_%d" % w) for w in range(NW)]
    return _rs_sib(q, "rs_sib")


def _allreduce_small(buf, name):
    rows = buf.shape[0]

    def body(src_ref, out_ref, gat_ref, send_sems, recv_sems):
        x, y, c = _my_pos()
        me = 4 * x + 2 * y + c
        gat_ref[me] = src_ref[...]
        cps = []
        for r in range(1, N_DEV):
            tx = 1 - x if (r >> 2) & 1 else x
            ty = 1 - y if (r >> 1) & 1 else y
            tc = 1 - c if r & 1 else c
            cps.append(pltpu.make_async_remote_copy(
                src_ref=src_ref, dst_ref=gat_ref.at[me], send_sem=send_sems.at[r - 1], recv_sem=recv_sems.at[r - 1],
                device_id=(tx, ty, tc), device_id_type=MESH))
        for cp in cps:
            cp.start()
        for cp in cps:
            cp.wait()
        acc = gat_ref[0]
        for k in range(1, N_DEV):
            acc = acc + gat_ref[k]
        out_ref[...] = acc

    return pl.pallas_call(
        body, name=name, in_specs=[VMEM_SPEC], out_specs=VMEM_SPEC, out_shape=jax.ShapeDtypeStruct((rows, 128), F32),
        scratch_shapes=[pltpu.VMEM((N_DEV, rows, 128), F32), pltpu.SemaphoreType.DMA((N_DEV - 1,)),
                        pltpu.SemaphoreType.DMA((N_DEV - 1,))],
        compiler_params=_cp())(buf)


SMALL = (("norm_mix_w", (D,)), ("ssd_conv_w", (4, XBC)), ("ssd_conv_b", (XBC,)), ("ssd_dt_bias", (NH,)),
         ("ssd_a_log", (NH,)), ("ssd_d", (NH,)), ("ssd_norm_w", (SSD_W,)), ("q_norm_w", (HD,)),
         ("k_norm_w", (HD,)), ("attn_sinks", (NQH,)), ("cm_dw_w", (CMK, CMC)), ("cm_dw_b", (CMC,)),
         ("cm_ln_w", (CMC,)), ("cm_ln_b", (CMC,)), ("norm_mlp_w", (D,)))
SHARDED_SMALL = ("ssd_conv_w", "cm_dw_w")


def _seg_len(shape):
    n = 1
    for d in shape:
        n *= d
    return -(-n // 128) * 128


def _pack_small(vals, names):
    parts = []
    for name, shape in SMALL:
        if name not in names:
            continue
        v = vals[name].reshape(DEPTH, -1)
        pad = _seg_len(shape) - v.shape[1]
        parts.append(jnp.pad(v, ((0, 0), (0, pad))))
    flat = jnp.concatenate(parts, axis=1)
    return flat.reshape(-1, 128)


def _unpack_small(buf, names):
    flat = buf.reshape(DEPTH, -1)
    out = {}
    off = 0
    for name, shape in SMALL:
        if name not in names:
            continue
        n = 1
        for d in shape:
            n *= d
        out[name] = flat[:, off:off + n].reshape((DEPTH,) + shape)
        off += _seg_len(shape)
    return out


def _rope_tables():
    inv = 10000.0 ** (-jnp.arange(0, HD, 2, dtype=F32) / HD)
    ang = jnp.arange(L, dtype=F32)[:, None] * inv[None, :]
    return jnp.cos(ang), jnp.sin(ang)


def _pad128(v):
    return jnp.pad(v, (0, 128 - v.shape[0]))[None, :]


def _heads(t, nh, w):
    return jnp.transpose(t.reshape(L, nh, w), (1, 0, 2))


def _unheads(t):
    nh, _, w = t.shape
    return jnp.transpose(t, (1, 0, 2)).reshape(L, nh * w)


def kernel(x, norm_mix_w, w_in, ssd_conv_w, ssd_conv_b, ssd_dt_bias, ssd_a_log, ssd_d, ssd_norm_w, q_norm_w, k_norm_w, attn_sinks, cm_dw_w, cm_dw_b, cm_ln_w, cm_ln_b, w_out, norm_mlp_w, w_mlp_up, w_mlp_down, loss_target, m_norm_mix_w, m_w_in, m_ssd_conv_w, m_ssd_conv_b, m_ssd_dt_bias, m_ssd_a_log, m_ssd_d, m_ssd_norm_w, m_q_norm_w, m_k_norm_w, m_attn_sinks, m_cm_dw_w, m_cm_dw_b, m_cm_ln_w, m_cm_ln_b, m_w_out, m_norm_mlp_w, m_w_mlp_up, m_w_mlp_down, v_norm_mix_w, v_w_in, v_ssd_conv_w, v_ssd_conv_b, v_ssd_dt_bias, v_ssd_a_log, v_ssd_d, v_ssd_norm_w, v_q_norm_w, v_k_norm_w, v_attn_sinks, v_cm_dw_w, v_cm_dw_b, v_cm_ln_w, v_cm_ln_b, v_w_out, v_norm_mlp_w, v_w_mlp_up, v_w_mlp_down):
    px, py, pc = _my_pos()
    shard = 2 * px + py
    sidx = jnp.reshape(shard, (1,)).astype(jnp.int32)
    consts = _ssd_consts()
    cos, sin = _rope_tables()

    wb = [w.astype(BF16) for w in (w_in, w_out, w_mlp_up, w_mlp_down)]
    zc = jnp.zeros((DEPTH, 4, XBC), F32)
    zc = lax.dynamic_update_slice_in_dim(zc, ssd_conv_w, shard * (XBC // 4), axis=2)
    zd = jnp.zeros((DEPTH, CMK, CMC), F32)
    zd = lax.dynamic_update_slice_in_dim(zd, cm_dw_w, shard * (CMC // 4), axis=2)
    half = jnp.where(pc == 0, 1.0, 0.0).astype(F32)
    gw = _allreduce_small(_pack_small({"ssd_conv_w": zc * half, "cm_dw_w": zd * half}, SHARDED_SMALL), "ag_small")
    gw = _unpack_small(gw, SHARDED_SMALL)
    conv_w_full, dw_w_full = gw["ssd_conv_w"], gw["cm_dw_w"]

    o_x, o_dt, o_q, o_g = 1024, 2560, 2576, 3344

    xcur = x[0]
    saved = []
    for l in range(DEPTH):
        own = [t[l] for t in wb]
        g_in, g_out, g_up, g_dn = [_place_own(own[w], g, sidx, "ag_place_%d" % w)
                                   for w, g in enumerate(_gather_layer(own, "ag_layer"))]
        wo = jnp.concatenate([g_in[0], g_in[1], g_in[2], g_in[3]], axis=1)
        w_perm = jnp.concatenate([wo[:, o_x:o_dt], wo[:, :o_x], wo[:, o_g:], wo[:, o_q:o_g], wo[:, o_dt:o_q],
                                  jnp.zeros((D, 128 - NH), BF16)], axis=1)
        wout_l = g_out.reshape(1, 2 * D, D)
        wdn_l = g_dn.reshape(1, DFF, D)
        wup_l = g_up[None]
        h = _rms_fwd(xcur, norm_mix_w[l][None], "rms_mix_fwd")
        u = _mm(h, w_perm, "nn", "in_proj", tn=640)
        alog = _pad128(ssd_a_log[l])
        dtb = _pad128(ssd_dt_bias[l])
        dxp = jnp.repeat(ssd_d[l], HP)[None, :]
        ssd_p = (conv_w_full[l], ssd_conv_b[l][None], dtb, alog, dxp, ssd_norm_w[l][None])
        ya, ypre, st = _ssd_fwd(u, *ssd_p, consts, "ssd_fwd")
        q3 = _heads(u[:, U_Q:U_K], NQH, HD)
        k3 = _heads(u[:, U_K:U_V], NKV, HD)
        v3 = _heads(u[:, U_V:U_DT], NKV, HD)
        qa, qb, ka, kb = q3[..., :HH], q3[..., HH:], k3[..., :HH], k3[..., HH:]
        qw, kw = q_norm_w[l], k_norm_w[l]
        sinks = jnp.broadcast_to(attn_sinks[l][:, None, None], (NQH, 1, 128))
        attn_args = (qa, qb, ka, kb, v3, cos, sin, qw[None, :HH], qw[None, HH:], kw[None, :HH], kw[None, HH:], sinks)
        yb = _unheads(_attn_fwd(*attn_args, "attn_fwd")).astype(BF16)
        conf_p = (dw_w_full[l], cm_dw_b[l][None], cm_ln_w[l][None], cm_ln_b[l][None])
        yc = _conf_fwd(u, *conf_p, "conf_fwd")
        ycat = jnp.concatenate([ya, yb, yc], axis=1)
        x1 = _mm_wl(ycat, wout_l, 0, "nn", "out_proj", add=xcur)
        hm = _rms_fwd(x1, norm_mlp_w[l][None], "rms_mlp_fwd")
        a_up, r_up = _mm_up(hm, wup_l, 0, "nn", "mlp_up")
        x2 = _mm_wl(r_up, wdn_l, 0, "nn", "mlp_down", add=x1)
        saved.append(dict(x=xcur, h=h, u=u, ypre=ypre, st=st, attn_args=attn_args, conf_p=conf_p, ycat=ycat, x1=x1,
                          hm=hm, a_up=a_up, r_up=r_up, ssd_p=ssd_p, w_perm=w_perm, wout_l=wout_l, wdn_l=wdn_l,
                          wup_l=wup_l))
        xcur = x2

    lsum, dx, dxb = _loss_bwd(xcur, loss_target[0], "loss")

    gbig = [None] * DEPTH
    gsm = {name: [] for name, _ in SMALL}
    for l in reversed(range(DEPTH)):
        sv = saved[l]
        da = _mm_wl(dxb, sv["wdn_l"], 0, "nt", "mlp_down_dx", relu2_of=sv["a_up"])
        dwdn, dwdn_b = _mm_dw(sv["r_up"], dxb, "mlp_down_dw")
        dwup, dwup_b = _mm_dw(sv["hm"], da, "mlp_up_dw", col_shards=True)
        dhm = _mm_up(da, sv["wup_l"], 0, "nt", "mlp_up_dx")
        dx1, dx1b, dnw = _rms_bwd(sv["x1"], norm_mlp_w[l][None], dhm, dx, "rms_mlp_bwd")
        gsm["norm_mlp_w"].append(dnw[0])
        dy = _mm_wl(dx1b, sv["wout_l"], 0, "nt", "out_proj_dx")
        dwout, dwout_b = _mm_dw(sv["ycat"], dx1b, "out_proj_dw")
        da_c, dg_c, dww, dwb, dlw, dlb = _conf_bwd(sv["u"], *sv["conf_p"], dy, "conf_bwd")
        gsm["cm_dw_w"].append(dww[:CMK])
        gsm["cm_dw_b"].append(dwb[0])
        gsm["cm_ln_w"].append(dlw[0])
        gsm["cm_ln_b"].append(dlb[0])
        do3 = _heads(dy[:, SSD_W:SSD_W + ATT_W], NQH, HD)
        (dqa, dqb, dka, dkb, dv3, dqwa, dqwb, dkwa, dkwb, dsk) = _attn_bwd(*sv["attn_args"], do3, "attn_bwd")
        dq = _unheads(jnp.concatenate([dqa, dqb], axis=-1))
        dk = _unheads(jnp.concatenate([dka, dkb], axis=-1))
        dv = _unheads(dv3)
        gsm["q_norm_w"].append(jnp.concatenate([dqwa[0], dqwb[0]]))
        gsm["k_norm_w"].append(jnp.concatenate([dkwa[0], dkwb[0]]))
        gsm["attn_sinks"].append(dsk[:, 0, 0])
        (dz, dxr, ddtr, dcw, dcb, ddtb, dalog, ddd, dnsw) = _ssd_bwd(
            sv["u"], sv["ypre"], sv["st"], dy, *sv["ssd_p"], consts, "ssd_bwd")
        gsm["ssd_conv_w"].append(dcw[:4])
        gsm["ssd_conv_b"].append(dcb[0])
        gsm["ssd_dt_bias"].append(ddtb[0, :NH])
        gsm["ssd_a_log"].append(dalog[0, :NH])
        gsm["ssd_d"].append(ddd[0, :NH])
        gsm["ssd_norm_w"].append(dnsw[0])
        du = jnp.concatenate([dxr, dz, da_c, dg_c, dq.astype(BF16), dk.astype(BF16), dv.astype(BF16), ddtr], axis=1)
        dwp, dwp_b = _mm_dw(sv["h"], du, "in_dw", tn=640)

        def unperm(t):
            t = jnp.concatenate([t[:, U_Z:U_A], t[:, U_X:U_Z], t[:, U_DT:U_DT + NH], t[:, U_Q:U_DT], t[:, U_A:U_Q]],
                                axis=1)
            return jnp.transpose(t.reshape(D, 4, N_IN // 4), (1, 0, 2))

        dh = _mm(du, sv["w_perm"], "nt", "in_dx")
        dx, dxb, dnm = _rms_bwd(sv["x"], norm_mix_w[l][None], dh, dx1, "rms_mix_bwd")
        gsm["norm_mix_w"].append(dnm[0])
        gbig[l] = _reduce_scatter_layer(
            [unperm(dwp), dwout.reshape(4, D // 2, D), dwup, dwdn.reshape(4, D, D)],
            [unperm(dwp_b), dwout_b.reshape(4, D // 2, D), dwup_b, dwdn_b.reshape(4, D, D)], "l%d" % l)

    gsm = {k: jnp.stack(v[::-1]) for k, v in gsm.items()}
    packed = _pack_small(gsm, [n for n, _ in SMALL])
    packed = jnp.concatenate([packed, lsum], axis=0)
    red = _allreduce_small(packed, "ar_small")
    loss = 0.5 * red[-8, 0] / D
    gsm = _unpack_small(red[:-8], [n for n, _ in SMALL])
    gsm["ssd_conv_w"] = lax.dynamic_slice_in_dim(gsm["ssd_conv_w"], shard * (XBC // 4), XBC // 4, axis=2)
    gsm["cm_dw_w"] = lax.dynamic_slice_in_dim(gsm["cm_dw_w"], shard * (CMC // 4), CMC // 4, axis=2)
    grads = dict(gsm)
    for w, n in enumerate(("w_in", "w_out", "w_mlp_up", "w_mlp_down")):
        grads[n] = jnp.stack([gbig[l][w] for l in range(DEPTH)])

    loc = locals()
    names = ["norm_mix_w", "w_in", "ssd_conv_w", "ssd_conv_b", "ssd_dt_bias", "ssd_a_log", "ssd_d", "ssd_norm_w",
             "q_norm_w", "k_norm_w", "attn_sinks", "cm_dw_w", "cm_dw_b", "cm_ln_w", "cm_ln_b", "w_out", "norm_mlp_w",
             "w_mlp_up", "w_mlp_down"]
    weights = {n: loc[n] for n in names}
    moms = {n: loc["m_" + n] for n in names}
    vars_ = {n: loc["v_" + n] for n in names}
    delta, new_m, new_v = {}, {}, {}
    packed_names = [n for n, _ in SMALL if n not in SHARDED_SMALL]
    pw = _pack_small(weights, packed_names)
    pg = _pack_small(grads, packed_names)
    pm = _pack_small(moms, packed_names)
    pv = _pack_small(vars_, packed_names)
    pd, pmn, pvn = _adamw(pw, pg, pm, pv, "adamw_small")
    for dst, buf in ((delta, pd), (new_m, pmn), (new_v, pvn)):
        dst.update(_unpack_small(buf, packed_names))
    for n in ("w_in", "w_out", "w_mlp_up", "w_mlp_down", "ssd_conv_w", "cm_dw_w"):
        shp = weights[n].shape
        flat = lambda t: t.reshape(-1, shp[-1])
        d_, m_, v_ = _adamw(flat(weights[n]), flat(grads[n]), flat(moms[n]), flat(vars_[n]), "adamw_" + n)
        delta[n], new_m[n], new_v[n] = d_.reshape(shp), m_.reshape(shp), v_.reshape(shp)

    return (loss, dx[None], *[grads[n] for n in names], *[delta[n] for n in names],
            *[new_m[n] for n in names], *[new_v[n] for n in names])
```

```python
import functools
import math

import jax
import jax.numpy as jnp
from jax import lax
from jax.experimental import pallas as pl
from jax.experimental.pallas import tpu as pltpu

F32 = jnp.float32
BF16 = jnp.bfloat16
MESH = pl.DeviceIdType.MESH
ANY = pl.BlockSpec(memory_space=pl.ANY)
VMEM_SPEC = pl.BlockSpec(memory_space=pltpu.VMEM)

D = 1024
L = 2048
DEPTH = 4
SSD_W = 1024
XBC = 1536
NH = 16
HP = 64
NS = 128
Q = 128
NC = L // Q
ATT_W = 512
NQH = 8
NKV = 2
HD = 64
HH = HD // 2
CMC = 512
CMK = 31
DFF = 4096
N_IN = 4368
N_PAD = 4480
RMS_EPS = 1e-6
LN_EPS = 1e-5
NEG = -1e30
LR, B1, B2, EPS_A, WD, STEP = 0.001, 0.9, 0.999, 1e-8, 0.01, 10
VMEM_LIMIT = 56 * 1024 * 1024
N_DEV = 8


def _cp(sem=None):
    kw = dict(vmem_limit_bytes=VMEM_LIMIT)
    if sem is not None:
        kw["dimension_semantics"] = sem
    return pltpu.CompilerParams(**kw)


def _dg(a, b, ca, cb):
    return lax.dot_general(a, b, (((ca,), (cb,)), ((), ())), preferred_element_type=F32)


def _split3(x):
    hi = x.astype(BF16)
    r = x - hi.astype(F32)
    mid = r.astype(BF16)
    lo = (r - mid.astype(F32)).astype(BF16)
    return hi, mid, lo


def _xdot_l(x, m, ca=1, cb=0):
    hi, mid, lo = _split3(x)
    return _dg(hi, m, ca, cb) + _dg(mid, m, ca, cb) + _dg(lo, m, ca, cb)


def _xdot_r(m, x, ca=1, cb=0):
    hi, mid, lo = _split3(x)
    return _dg(m, hi, ca, cb) + _dg(m, mid, ca, cb) + _dg(m, lo, ca, cb)


def _rowdot(v, m):
    return _xdot_l(jnp.broadcast_to(v, (8, v.shape[1])), m)[0:1]


def _sigmoid(x):
    return 1.0 / (1.0 + jnp.exp(-x))


def _softplus(x):
    e = jnp.exp(-jnp.abs(x))
    u = 1.0 + e
    l1p = jnp.where(u == 1.0, e, jnp.log(u) * (e / jnp.where(u == 1.0, 1.0, u - 1.0)))
    return jnp.maximum(x, 0.0) + l1p


def _mm(a, b, mode, name, add=None, out_dtype=F32, tm=512, tn=512):
    if mode == "nn":
        m, k = a.shape
        n = b.shape[1]
        a_spec = pl.BlockSpec((tm, k), lambda i, j: (i, 0))
        b_spec = pl.BlockSpec((k, tn), lambda i, j: (0, j))
        ca, cb = 1, 0
    elif mode == "nt":
        m, k = a.shape
        n = b.shape[0]
        a_spec = pl.BlockSpec((tm, k), lambda i, j: (i, 0))
        b_spec = pl.BlockSpec((tn, k), lambda i, j: (j, 0))
        ca, cb = 1, 1
    else:
        k, m = a.shape
        n = b.shape[1]
        a_spec = pl.BlockSpec((k, tm), lambda i, j: (0, i))
        b_spec = pl.BlockSpec((k, tn), lambda i, j: (0, j))
        ca, cb = 0, 0
    assert m % tm == 0 and n % tn == 0, (m, n, tm, tn)
    o_spec = pl.BlockSpec((tm, tn), lambda i, j: (i, j))

    if add is None:
        def body(a_ref, b_ref, o_ref):
            o_ref[...] = _dg(a_ref[...], b_ref[...], ca, cb).astype(o_ref.dtype)
        ins, specs = (a, b), [a_spec, b_spec]
    else:
        def body(a_ref, b_ref, c_ref, o_ref):
            o_ref[...] = (_dg(a_ref[...], b_ref[...], ca, cb) + c_ref[...]).astype(o_ref.dtype)
        ins, specs = (a, b, add), [a_spec, b_spec, o_spec]

    return pl.pallas_call(
        body, name=name, grid=(m // tm, n // tn), in_specs=specs, out_specs=o_spec,
        out_shape=jax.ShapeDtypeStruct((m, n), out_dtype),
        compiler_params=_cp(("parallel", "parallel")))(*ins)


def _mm_wl(a, b, layer, mode, name, add=None, relu2_of=None, tm=512, tn=512):
    m, k = a.shape
    a_spec = pl.BlockSpec((tm, k), lambda i, j: (i, 0))
    if mode == "nn":
        n = b.shape[2]
        b_spec = pl.BlockSpec((1, k, tn), lambda i, j: (layer, 0, j))
        cb = 0
    else:
        n = b.shape[1]
        b_spec = pl.BlockSpec((1, tn, k), lambda i, j: (layer, j, 0))
        cb = 1
    o_spec = pl.BlockSpec((tm, tn), lambda i, j: (i, j))
    out_dtype = F32
    if relu2_of is not None:
        def body(a_ref, b_ref, c_ref, o_ref):
            o_ref[...] = (_dg(a_ref[...], b_ref[0], 1, cb) * 2.0 * jnp.maximum(c_ref[...], 0.0)).astype(BF16)
        ins, specs, out_dtype = (a, b, relu2_of), [a_spec, b_spec, o_spec], BF16
    elif add is None:
        def body(a_ref, b_ref, o_ref):
            o_ref[...] = _dg(a_ref[...], b_ref[0], 1, cb)
        ins, specs = (a, b), [a_spec, b_spec]
    else:
        def body(a_ref, b_ref, c_ref, o_ref):
            o_ref[...] = _dg(a_ref[...], b_ref[0], 1, cb) + c_ref[...]
        ins, specs = (a, b, add), [a_spec, b_spec, o_spec]
    return pl.pallas_call(
        body, name=name, grid=(m // tm, n // tn), in_specs=specs, out_specs=o_spec,
        out_shape=jax.ShapeDtypeStruct((m, n), out_dtype), compiler_params=_cp(("parallel", "parallel")))(*ins)


def _mm_up(a, b, layer, mode, name, tm=512, tn=512):
    m = a.shape[0]
    cs = DFF // 4
    o_spec = pl.BlockSpec((tm, tn), lambda i, j: (i, j))
    if mode == "nn":
        per = cs // tn
        a_spec = pl.BlockSpec((tm, D), lambda i, j: (i, 0))
        b_spec = pl.BlockSpec((1, 1, D, tn), lambda i, j: (layer, j // per, 0, j % per))

        def body(a_ref, b_ref, o_ref, r_ref):
            acc = _dg(a_ref[...], b_ref[0, 0], 1, 0)
            o_ref[...] = acc
            r = jnp.maximum(acc, 0.0)
            r_ref[...] = (r * r).astype(BF16)

        return pl.pallas_call(
            body, name=name, grid=(m // tm, DFF // tn), in_specs=[a_spec, b_spec], out_specs=[o_spec, o_spec],
            out_shape=[jax.ShapeDtypeStruct((m, DFF), F32), jax.ShapeDtypeStruct((m, DFF), BF16)],
            compiler_params=_cp(("parallel", "parallel")))(a, b)
    else:
        n = D
        a_spec = pl.BlockSpec((tm, DFF), lambda i, j: (i, 0))
        b_spec = pl.BlockSpec((1, 4, tn, cs), lambda i, j: (layer, 0, j, 0))

        def body(a_ref, b_ref, o_ref):
            acc = _dg(a_ref[:, 0:cs], b_ref[0, 0], 1, 1)
            for s in range(1, 4):
                acc = acc + _dg(a_ref[:, s * cs:(s + 1) * cs], b_ref[0, s], 1, 1)
            o_ref[...] = acc
    return pl.pallas_call(
        body, name=name, grid=(m // tm, n // tn), in_specs=[a_spec, b_spec], out_specs=o_spec,
        out_shape=jax.ShapeDtypeStruct((m, n), F32), compiler_params=_cp(("parallel", "parallel")))(a, b)


def _mm_dw(a, b, name, col_shards=False, tm=512, tn=512):
    k, m = a.shape
    n = b.shape[1]
    a_spec = pl.BlockSpec((k, tm), lambda i, j: (0, i))
    b_spec = pl.BlockSpec((k, tn), lambda i, j: (0, j))
    if col_shards:
        per = (n // 4) // tn
        o_spec = pl.BlockSpec((1, tm, tn), lambda i, j: (j // per, i, j % per))
        shape = (4, m, n // 4)
    else:
        o_spec = pl.BlockSpec((tm, tn), lambda i, j: (i, j))
        shape = (m, n)

    def body(a_ref, b_ref, o_ref, ob_ref):
        acc = _dg(a_ref[...], b_ref[...], 0, 0).reshape(o_ref.shape)
        o_ref[...] = acc
        ob_ref[...] = acc.astype(BF16)

    return pl.pallas_call(
        body, name=name, grid=(m // tm, n // tn), in_specs=[a_spec, b_spec], out_specs=[o_spec, o_spec],
        out_shape=[jax.ShapeDtypeStruct(shape, F32), jax.ShapeDtypeStruct(shape, BF16)],
        compiler_params=_cp(("parallel", "parallel")))(a, b)


TR = 256


def _rms_fwd(x, w, name):
    def body(x_ref, w_ref, o_ref):
        xv = x_ref[...]
        r = lax.rsqrt(jnp.mean(xv * xv, axis=-1, keepdims=True) + RMS_EPS)
        o_ref[...] = (xv * r * w_ref[...]).astype(BF16)

    return pl.pallas_call(
        body, name=name, grid=(L // TR,),
        in_specs=[pl.BlockSpec((TR, D), lambda i: (i, 0)), pl.BlockSpec((1, D), lambda i: (0, 0))],
        out_specs=pl.BlockSpec((TR, D), lambda i: (i, 0)),
        out_shape=jax.ShapeDtypeStruct((L, D), BF16), compiler_params=_cp(("parallel",)))(x, w)


def _rms_bwd(x, w, dh, dres, name):
    def body(x_ref, w_ref, dh_ref, dr_ref, dx_ref, dxb_ref, dw_ref):
        xv = x_ref[...]
        r = lax.rsqrt(jnp.mean(xv * xv, axis=-1, keepdims=True) + RMS_EPS)
        n = xv * r
        dhv = dh_ref[...]
        g = dhv * w_ref[...]
        dx = dr_ref[...] + r * (g - n * jnp.mean(g * n, axis=-1, keepdims=True))
        dx_ref[...] = dx
        dxb_ref[...] = dx.astype(BF16)

        @pl.when(pl.program_id(0) == 0)
        def _():
            dw_ref[...] = jnp.zeros_like(dw_ref)
        dw_ref[...] += jnp.sum(dhv * n, axis=0, keepdims=True)

    row = pl.BlockSpec((TR, D), lambda i: (i, 0))
    vec = pl.BlockSpec((1, D), lambda i: (0, 0))
    return pl.pallas_call(
        body, name=name, grid=(L // TR,), in_specs=[row, vec, row, row], out_specs=[row, row, vec],
        out_shape=[jax.ShapeDtypeStruct((L, D), F32), jax.ShapeDtypeStruct((L, D), BF16),
                   jax.ShapeDtypeStruct((1, D), F32)],
        compiler_params=_cp(("arbitrary",)))(x, w, dh, dres)


def _loss_bwd(y, t, name):
    def body(y_ref, t_ref, l_ref, d_ref, db_ref):
        e = y_ref[...] - t_ref[...]
        d = e * (1.0 / D)
        d_ref[...] = d
        db_ref[...] = d.astype(BF16)

        @pl.when(pl.program_id(0) == 0)
        def _():
            l_ref[...] = jnp.zeros_like(l_ref)
        s = jnp.sum(jnp.sum(e * e, axis=-1, keepdims=True), axis=0, keepdims=True)
        l_ref[...] += jnp.broadcast_to(s, l_ref.shape)

    row = pl.BlockSpec((TR, D), lambda i: (i, 0))
    tile = pl.BlockSpec((8, 128), lambda i: (0, 0))
    return pl.pallas_call(
        body, name=name, grid=(L // TR,), in_specs=[row, row], out_specs=[tile, row, row],
        out_shape=[jax.ShapeDtypeStruct((8, 128), F32), jax.ShapeDtypeStruct((L, D), F32),
                   jax.ShapeDtypeStruct((L, D), BF16)],
        compiler_params=_cp(("arbitrary",)))(y, t)


def _adamw(w, g, m, v, name):
    rows, cols = w.shape
    tr = rows
    for cand in (512, 256, 128, 64, 32, 16, 8):
        if rows % cand == 0 and cand * cols * 4 <= 2 * 1024 * 1024:
            tr = cand
            break
    c1 = 1.0 / (1.0 - B1 ** STEP)
    c2 = 1.0 / (1.0 - B2 ** STEP)

    def body(w_ref, g_ref, m_ref, v_ref, d_ref, mo_ref, vo_ref):
        gv = g_ref[...]
        mn = B1 * m_ref[...] + (1.0 - B1) * gv
        vn = B2 * v_ref[...] + (1.0 - B2) * (gv * gv)
        mo_ref[...] = mn
        vo_ref[...] = vn
        d_ref[...] = -LR * ((mn * c1) / (jnp.sqrt(vn * c2) + EPS_A) + WD * w_ref[...])

    blk = pl.BlockSpec((tr, cols), lambda i: (i, 0))
    shp = jax.ShapeDtypeStruct((rows, cols), F32)
    return pl.pallas_call(body, name=name, grid=(rows // tr,), in_specs=[blk] * 4, out_specs=[blk] * 3,
                          out_shape=[shp] * 3, compiler_params=_cp(("parallel",)))(w, g, m, v)


CT = 256
CPAD = 32


U_X, U_Z, U_A, U_G, U_Q, U_K, U_V, U_DT = 0, 1536, 2560, 3072, 3584, 4096, 4224, 4352


def _cfull(shape):
    return pl.BlockSpec(shape, lambda i: (0,) * len(shape))


def _conf_in_specs():
    return [pl.BlockSpec((L, CMC), lambda i: (0, U_A // CMC)), pl.BlockSpec((L, CMC), lambda i: (0, U_G // CMC)),
            _cfull((CMK, CMC)), _cfull((1, CMC)), _cfull((1, CMC)), _cfull((1, CMC))]


def _conf_fwd(u, w, b, lw, lb, name):
    def body(a_ref, g_ref, w_ref, b_ref, lw_ref, lb_ref, o_ref, hp_ref, win_ref):
        hp_ref[0:CPAD, :] = jnp.zeros((CPAD, CMC), F32)
        hp_ref[CPAD:, :] = a_ref[...] * _sigmoid(g_ref[...])

        def tile(i, carry):
            base = pl.multiple_of(i * CT, CT)
            win_ref[...] = hp_ref[pl.ds(base, CT + CPAD), :]
            c = jnp.broadcast_to(b_ref[...], (CT, CMC))
            for k in range(CMK):
                c = c + w_ref[k:k + 1, :] * win_ref[pl.ds(2 + k, CT), :]
            mu = jnp.mean(c, axis=-1, keepdims=True)
            cc = c - mu
            var = jnp.mean(cc * cc, axis=-1, keepdims=True)
            l = cc * lax.rsqrt(var + LN_EPS) * lw_ref[...] + lb_ref[...]
            o_ref[pl.ds(base, CT), :] = (l * _sigmoid(l)).astype(BF16)
            return carry

        lax.fori_loop(0, L // CT, tile, 0)

    return pl.pallas_call(
        body, name=name, grid=(1,), in_specs=_conf_in_specs(), out_specs=_cfull((L, CMC)),
        out_shape=jax.ShapeDtypeStruct((L, CMC), BF16),
        scratch_shapes=[pltpu.VMEM((L + CPAD, CMC), F32), pltpu.VMEM((CT + CPAD, CMC), F32)],
        compiler_params=_cp(("arbitrary",)))(u, u, w, b, lw, lb)


def _conf_bwd(u, w, b, lw, lb, dy, name):
    def body(a_ref, g_ref, w_ref, b_ref, lw_ref, lb_ref, dy_ref,
             da_ref, dg_ref, dw_ref, db_ref, dlw_ref, dlb_ref, hp_ref, dcp_ref, win_ref):
        hp_ref[0:CPAD, :] = jnp.zeros((CPAD, CMC), F32)
        hp_ref[CPAD:, :] = a_ref[...] * _sigmoid(g_ref[...])
        dcp_ref[L:, :] = jnp.zeros((CPAD, CMC), F32)
        dw_ref[...] = jnp.zeros_like(dw_ref)
        db_ref[...] = jnp.zeros_like(db_ref)
        dlw_ref[...] = jnp.zeros_like(dlw_ref)
        dlb_ref[...] = jnp.zeros_like(dlb_ref)

        def tile1(i, carry):
            base = pl.multiple_of(i * CT, CT)
            win_ref[...] = hp_ref[pl.ds(base, CT + CPAD), :]
            c = jnp.broadcast_to(b_ref[...], (CT, CMC))
            for k in range(CMK):
                c = c + w_ref[k:k + 1, :] * win_ref[pl.ds(2 + k, CT), :]
            mu = jnp.mean(c, axis=-1, keepdims=True)
            cc = c - mu
            var = jnp.mean(cc * cc, axis=-1, keepdims=True)
            rstd = lax.rsqrt(var + LN_EPS)
            n = cc * rstd
            l = n * lw_ref[...] + lb_ref[...]
            sl = _sigmoid(l)
            dl = dy_ref[pl.ds(base, CT), :] * (sl * (1.0 + l * (1.0 - sl)))
            dlw_ref[...] += jnp.sum(dl * n, axis=0, keepdims=True)
            dlb_ref[...] += jnp.sum(dl, axis=0, keepdims=True)
            dn = dl * lw_ref[...]
            dc = rstd * (dn - jnp.mean(dn, axis=-1, keepdims=True)
                         - n * jnp.mean(dn * n, axis=-1, keepdims=True))
            db_ref[...] += jnp.sum(dc, axis=0, keepdims=True)
            for k in range(CMK):
                dw_ref[k:k + 1, :] += jnp.sum(dc * win_ref[pl.ds(2 + k, CT), :], axis=0, keepdims=True)
            dcp_ref[pl.ds(base, CT), :] = dc
            return carry

        lax.fori_loop(0, L // CT, tile1, 0)

        def tile2(i, carry):
            base = pl.multiple_of(i * CT, CT)
            win_ref[...] = dcp_ref[pl.ds(base, CT + CPAD), :]
            dh = jnp.zeros((CT, CMC), F32)
            for k in range(CMK):
                dh = dh + w_ref[k:k + 1, :] * win_ref[pl.ds(CMK - 1 - k, CT), :]
            av = a_ref[pl.ds(base, CT), :]
            sg = _sigmoid(g_ref[pl.ds(base, CT), :])
            da_ref[pl.ds(base, CT), :] = (dh * sg).astype(BF16)
            dg_ref[pl.ds(base, CT), :] = (dh * av * sg * (1.0 - sg)).astype(BF16)
            return carry

        lax.fori_loop(0, L // CT, tile2, 0)

    vec = jax.ShapeDtypeStruct((1, CMC), F32)
    dy_spec = pl.BlockSpec((L, CMC), lambda i: (0, (SSD_W + ATT_W) // CMC))
    return pl.pallas_call(
        body, name=name, grid=(1,), in_specs=_conf_in_specs() + [dy_spec],
        out_specs=[_cfull((L, CMC)), _cfull((L, CMC)), _cfull((32, CMC)), _cfull((1, CMC)), _cfull((1, CMC)),
                   _cfull((1, CMC))],
        out_shape=[jax.ShapeDtypeStruct((L, CMC), BF16), jax.ShapeDtypeStruct((L, CMC), BF16),
                   jax.ShapeDtypeStruct((32, CMC), F32), vec, vec, vec],
        scratch_shapes=[pltpu.VMEM((L + CPAD, CMC), F32), pltpu.VMEM((L + CPAD, CMC), F32),
                        pltpu.VMEM((CT + CPAD, CMC), F32)],
        compiler_params=_cp(("arbitrary",)))(u, u, w, b, lw, lb, dy)


def _attn_prep(a, b, wa, wb, c, s):
    r = lax.rsqrt((jnp.sum(a * a, axis=-1, keepdims=True) + jnp.sum(b * b, axis=-1, keepdims=True))
                  * (1.0 / HD) + RMS_EPS)
    ha = a * r
    hb = b * r
    na = ha * wa
    nb = hb * wb
    return r, ha, hb, na * c - nb * s, nb * c + na * s


def _attn_scores(qa_n, qb_n, ka_c, kb_c, n, lo, sink):
    nk = ka_c.shape[0]
    s = (_dg(qa_n, ka_c, 1, 1) + _dg(qb_n, kb_c, 1, 1)) * (1.0 / math.sqrt(HD))
    qi = lax.broadcasted_iota(jnp.int32, (Q, nk), 0) + n * Q
    ki = lax.broadcasted_iota(jnp.int32, (Q, nk), 1) + lo
    diff = qi - ki
    s = jnp.where((diff >= 0) & (diff < Q), s, NEG)
    m = jnp.maximum(jnp.max(s, axis=-1, keepdims=True), sink)
    p = jnp.exp(s - m)
    ps = jnp.exp(sink - m)
    den = jnp.sum(p, axis=-1, keepdims=True) + ps
    return p / den, ps / den


def _attn_specs():
    qh = pl.BlockSpec((1, L, HH), lambda j: (j, 0, 0))
    kh = pl.BlockSpec((1, L, HH), lambda j: (j // 4, 0, 0))
    vv = pl.BlockSpec((1, L, HD), lambda j: (j // 4, 0, 0))
    tab = pl.BlockSpec((L, HH), lambda j: (0, 0))
    wv = pl.BlockSpec((1, HH), lambda j: (0, 0))
    sk = pl.BlockSpec((1, 1, 128), lambda j: (j, 0, 0))
    return qh, kh, vv, tab, wv, sk


def _attn_fwd(qa, qb, ka, kb, v, cos, sin, qwa, qwb, kwa, kwb, sinks, name):
    def body(qa_ref, qb_ref, ka_ref, kb_ref, v_ref, c_ref, s_ref, qwa_ref, qwb_ref, kwa_ref, kwb_ref,
             sk_ref, o_ref):
        c = c_ref[...]
        s = s_ref[...]
        _, _, _, qra, qrb = _attn_prep(qa_ref[0], qb_ref[0], qwa_ref[...], qwb_ref[...], c, s)
        _, _, _, kra, krb = _attn_prep(ka_ref[0], kb_ref[0], kwa_ref[...], kwb_ref[...], c, s)
        qra, qrb, kra, krb = (t.astype(BF16) for t in (qra, qrb, kra, krb))
        vb = v_ref[0].astype(BF16)
        sink = sk_ref[0][:, 0:1]
        for n in range(NC):
            lo = max(n - 1, 0) * Q
            hi = (n + 1) * Q
            p, _ = _attn_scores(qra[n * Q:hi], qrb[n * Q:hi], kra[lo:hi], krb[lo:hi], n, lo, sink)
            o_ref[0, n * Q:hi, :] = _dg(p.astype(BF16), vb[lo:hi], 1, 0)

    qh, kh, vv, tab, wv, sk = _attn_specs()
    return pl.pallas_call(
        body, name=name, grid=(NQH,),
        in_specs=[qh, qh, kh, kh, vv, tab, tab, wv, wv, wv, wv, sk],
        out_specs=pl.BlockSpec((1, L, HD), lambda j: (j, 0, 0)),
        out_shape=jax.ShapeDtypeStruct((NQH, L, HD), F32),
        compiler_params=_cp(("parallel",)))(qa, qb, ka, kb, v, cos, sin, qwa, qwb, kwa, kwb, sinks)


def _attn_bwd(qa, qb, ka, kb, v, cos, sin, qwa, qwb, kwa, kwb, sinks, do, name):
    def body(qa_ref, qb_ref, ka_ref, kb_ref, v_ref, c_ref, s_ref, qwa_ref, qwb_ref, kwa_ref, kwb_ref,
             sk_ref, do_ref,
             dqa_ref, dqb_ref, dka_ref, dkb_ref, dv_ref, dqwa_ref, dqwb_ref, dkwa_ref, dkwb_ref, dsk_ref,
             dqra_s, dqrb_s, dkra_s, dkrb_s, dv_s):
        j = pl.program_id(0)
        c = c_ref[...]
        s = s_ref[...]
        qwa, qwb, kwa, kwb = qwa_ref[...], qwb_ref[...], kwa_ref[...], kwb_ref[...]
        qr, qha, qhb, qra, qrb = _attn_prep(qa_ref[0], qb_ref[0], qwa, qwb, c, s)
        kr, kha, khb, kra, krb = _attn_prep(ka_ref[0], kb_ref[0], kwa, kwb, c, s)
        qra, qrb, kra, krb = (t.astype(BF16) for t in (qra, qrb, kra, krb))
        vb = v_ref[0].astype(BF16)
        sink = sk_ref[0][:, 0:1]
        dkra_s[...] = jnp.zeros_like(dkra_s)
        dkrb_s[...] = jnp.zeros_like(dkrb_s)
        dv_s[...] = jnp.zeros_like(dv_s)
        dsink = jnp.zeros((1, 1), F32)
        scale = 1.0 / math.sqrt(HD)
        for n in range(NC):
            lo = max(n - 1, 0) * Q
            hi = (n + 1) * Q
            p, ps = _attn_scores(qra[n * Q:hi], qrb[n * Q:hi], kra[lo:hi], krb[lo:hi], n, lo, sink)
            don = do_ref[0, n * Q:hi, :]
            dob = don.astype(BF16)
            pb = p.astype(BF16)
            dv_s[lo:hi, :] += _dg(pb, dob, 0, 0)
            dp = _dg(dob, vb[lo:hi], 1, 1)
            delta = jnp.sum(p * dp, axis=-1, keepdims=True)
            dsink = dsink - jnp.sum(ps * delta, axis=0, keepdims=True)
            ds = (p * (dp - delta) * scale).astype(BF16)
            dqra_s[n * Q:hi, :] = _dg(ds, kra[lo:hi], 1, 0)
            dqrb_s[n * Q:hi, :] = _dg(ds, krb[lo:hi], 1, 0)
            dkra_s[lo:hi, :] += _dg(ds, qra[n * Q:hi], 0, 0)
            dkrb_s[lo:hi, :] += _dg(ds, qrb[n * Q:hi], 0, 0)

        def unprep(dra, drb, r, ha, hb, wa, wb):
            dna = dra * c + drb * s
            dnb = drb * c - dra * s
            dwa = jnp.sum(dna * ha, axis=0, keepdims=True)
            dwb = jnp.sum(dnb * hb, axis=0, keepdims=True)
            ga = dna * wa
            gb = dnb * wb
            mu = (jnp.sum(ga * ha, axis=-1, keepdims=True) + jnp.sum(gb * hb, axis=-1, keepdims=True)) * (1.0 / HD)
            return r * (ga - ha * mu), r * (gb - hb * mu), dwa, dwb

        dqa, dqb, dqwa, dqwb = unprep(dqra_s[...], dqrb_s[...], qr, qha, qhb, qwa, qwb)
        dka, dkb, dkwa, dkwb = unprep(dkra_s[...], dkrb_s[...], kr, kha, khb, kwa, kwb)
        dqa_ref[0] = dqa
        dqb_ref[0] = dqb
        dsk_ref[0] = jnp.broadcast_to(dsink, (1, 128))

        @pl.when(j == 0)
        def _():
            dqwa_ref[...] = jnp.zeros_like(dqwa_ref)
            dqwb_ref[...] = jnp.zeros_like(dqwb_ref)
            dkwa_ref[...] = jnp.zeros_like(dkwa_ref)
            dkwb_ref[...] = jnp.zeros_like(dkwb_ref)
        dqwa_ref[...] += dqwa
        dqwb_ref[...] += dqwb
        dkwa_ref[...] += dkwa
        dkwb_ref[...] += dkwb

        @pl.when(j % 4 == 0)
        def _():
            dka_ref[0] = dka
            dkb_ref[0] = dkb
            dv_ref[0] = dv_s[...]

        @pl.when(j % 4 != 0)
        def _():
            dka_ref[0] += dka
            dkb_ref[0] += dkb
            dv_ref[0] += dv_s[...]

    qh, kh, vv, tab, wv, sk = _attn_specs()
    qo = pl.BlockSpec((1, L, HD), lambda j: (j, 0, 0))
    hvec = jax.ShapeDtypeStruct((1, HH), F32)
    return pl.pallas_call(
        body, name=name, grid=(NQH,),
        in_specs=[qh, qh, kh, kh, vv, tab, tab, wv, wv, wv, wv, sk, qo],
        out_specs=[qh, qh, kh, kh, vv, wv, wv, wv, wv, sk],
        out_shape=[jax.ShapeDtypeStruct((NQH, L, HH), F32), jax.ShapeDtypeStruct((NQH, L, HH), F32),
                   jax.ShapeDtypeStruct((NKV, L, HH), F32), jax.ShapeDtypeStruct((NKV, L, HH), F32),
                   jax.ShapeDtypeStruct((NKV, L, HD), F32), hvec, hvec, hvec, hvec,
                   jax.ShapeDtypeStruct((NQH, 1, 128), F32)],
        scratch_shapes=[pltpu.VMEM((L, HH), F32), pltpu.VMEM((L, HH), F32), pltpu.VMEM((L, HH), F32),
                        pltpu.VMEM((L, HH), F32), pltpu.VMEM((L, HD), F32)],
        compiler_params=_cp(("arbitrary",)))(qa, qb, ka, kb, v, cos, sin, qwa, qwb, kwa, kwb, sinks, do)


def _ssd_consts():
    hh = jnp.arange(128)[:, None]
    e = (hh == (jnp.arange(SSD_W)[None, :] // HP)).astype(BF16)
    e2 = (hh == (jnp.arange(NH * 128)[None, :] // 128)).astype(BF16)
    et = e.T
    tril = (jnp.arange(Q)[:, None] >= jnp.arange(Q)[None, :]).astype(BF16)
    triu = tril.T
    eye = jnp.eye(128, dtype=BF16)
    return e, e2, et, tril, triu, eye


def _ssd_common(x_ref, ext_scr, cw_ref, cb_ref, dt_ref, dtb_ref, alog_ref, e_ref, e2_ref, tril_ref, triu_ref,
                arow_scr, acol_scr, eax_scr):
    conv = jnp.broadcast_to(cb_ref[...], (Q, XBC))
    for k in range(4):
        conv = conv + cw_ref[k:k + 1, :] * ext_scr[pl.ds(5 + k, Q), :]
    sg = _sigmoid(conv)
    xbc = conv * sg
    dtpre = dt_ref[...] + dtb_ref[...]
    dt = _softplus(dtpre)
    a = -jnp.exp(alog_ref[...])
    adt = dt * a
    acol = _xdot_r(tril_ref[...], adt)
    acol_scr[...] = acol
    arow_scr[...] = _xdot_l(adt, triu_ref[...], 0, 0)
    alast = acol_scr[Q - 1:Q, :]
    ea = jnp.exp(acol)
    decs = jnp.exp(alast - acol)
    e = e_ref[...]
    dt_x = _xdot_l(dt, e)
    eax_scr[...] = _xdot_l(ea, e)
    decs_x = _xdot_l(decs, e)
    acx2 = _xdot_l(acol, e2_ref[...])
    return conv, sg, xbc, dtpre, dt, a, adt, acol, alast, ea, decs, dt_x, decs_x, acx2


def _ssd_fwd(u, cw, cb, dtb, alog, dxp, nw, consts, name):
    e, e2, et, tril, triu, eye = consts

    def body(z0_ref, z1_ref, x_ref, dt_ref, cw_ref, cb_ref, dtb_ref, alog_ref, dx_ref, nw_ref, e_ref, e2_ref,
             tril_ref, triu_ref, ya_ref, ypre_ref, st_ref, s_scr, ext_scr, arow_scr, acol_scr, eax_scr):
        c = pl.program_id(0)

        @pl.when(c == 0)
        def _():
            s_scr[...] = jnp.zeros_like(s_scr)
            ext_scr[0:8, :] = jnp.zeros((8, XBC), F32)
        ext_scr[8:8 + Q, :] = x_ref[...]
        (conv, sg, xbc, dtpre, dt, a, adt, acol, alast, ea, decs, dt_x, decs_x, acx2) = _ssd_common(
            x_ref, ext_scr, cw_ref, cb_ref, dt_ref, dtb_ref, alog_ref, e_ref, e2_ref, tril_ref, triu_ref,
            arow_scr, acol_scr, eax_scr)
        ext_scr[0:8, :] = ext_scr[Q:Q + 8, :]
        xs = xbc[:, :SSD_W]
        xdt = xs * dt_x
        lane = lax.broadcasted_iota(jnp.int32, (Q, 128), 1)
        causal = lax.broadcasted_iota(jnp.int32, (Q, Q), 0) >= lax.broadcasted_iota(jnp.int32, (Q, Q), 1)
        for g in range(2):
            bg = xbc[:, SSD_W + g * NS:SSD_W + (g + 1) * NS].astype(BF16)
            cg = xbc[:, SSD_W + 2 * NS + g * NS:SSD_W + 2 * NS + (g + 1) * NS].astype(BF16)
            cbm = _dg(cg, bg, 1, 1)
            sgv = s_scr[g]
            st_ref[0, g] = sgv
            gc = slice(g * 512, (g + 1) * 512)
            yoff = _dg(cg, sgv.astype(BF16), 1, 0) * eax_scr[:, gc]
            for pr in range(4):
                h0 = g * 8 + 2 * pr
                h1 = h0 + 1
                c0 = g * 512 + pr * 128
                xp = xdt[:, c0:c0 + 128].astype(BF16)
                w0 = (cbm * jnp.exp(jnp.where(causal, acx2[:, h0 * 128:(h0 + 1) * 128] - arow_scr[h0:h0 + 1, :],
                                              NEG))).astype(BF16)
                w1 = (cbm * jnp.exp(jnp.where(causal, acx2[:, h1 * 128:(h1 + 1) * 128] - arow_scr[h1:h1 + 1, :],
                                              NEG))).astype(BF16)
                yd = jnp.where(lane < HP, _dg(w0, xp, 1, 0), _dg(w1, xp, 1, 0))
                ypre_ref[:, c0:c0 + 128] = (yd + yoff[:, pr * 128:(pr + 1) * 128]
                                            + xs[:, c0:c0 + 128] * dx_ref[:, c0:c0 + 128])
            contrib = _dg(bg, (xdt[:, gc] * decs_x[:, gc]).astype(BF16), 0, 0)
            s_scr[g] = sgv * eax_scr[Q - 1:Q, gc] + contrib
        for g, zr in enumerate((z0_ref, z1_ref)):
            gc = slice(g * 512, (g + 1) * 512)
            zz = zr[...]
            ggg = ypre_ref[:, gc] * (zz * _sigmoid(zz))
            rstd = lax.rsqrt(jnp.mean(ggg * ggg, axis=-1, keepdims=True) + RMS_EPS)
            ya_ref[:, gc] = (ggg * rstd * nw_ref[:, gc]).astype(BF16)

    def row(w, blk=0):
        return pl.BlockSpec((Q, w), lambda c: (c, blk))

    def full(shape):
        return pl.BlockSpec(shape, lambda c: (0,) * len(shape))

    return pl.pallas_call(
        body, name=name, grid=(NC,),
        in_specs=[row(512, U_Z // 512), row(512, U_Z // 512 + 1), row(XBC, U_X // XBC), row(128, U_DT // 128),
                  full((4, XBC)), full((1, XBC)), full((1, 128)), full((1, 128)),
                  full((1, SSD_W)), full((1, SSD_W)), full((128, SSD_W)), full((128, NH * 128)), full((Q, Q)),
                  full((Q, Q))],
        out_specs=[row(SSD_W), row(SSD_W), pl.BlockSpec((1, 2, NS, 512), lambda c: (c, 0, 0, 0))],
        out_shape=[jax.ShapeDtypeStruct((L, SSD_W), BF16), jax.ShapeDtypeStruct((L, SSD_W), F32),
                   jax.ShapeDtypeStruct((NC, 2, NS, 512), F32)],
        scratch_shapes=[pltpu.VMEM((2, NS, 512), F32), pltpu.VMEM((Q + 8, XBC), F32), pltpu.VMEM((128, Q), F32),
                        pltpu.VMEM((Q, 128), F32), pltpu.VMEM((Q, SSD_W), F32)],
        compiler_params=_cp(("arbitrary",)))(u, u, u, u, cw, cb, dtb, alog, dxp, nw, e, e2, tril, triu)


def _ssd_bwd(u, ypre, st, dy, cw, cb, dtb, alog, dxp, nw, consts, name):
    e, e2, et, tril, triu, eye = consts

    def body(z0_ref, z1_ref, x_ref, xp_ref, dt_ref, ypre_ref, st_ref, dya_ref, cw_ref, cb_ref, dtb_ref, alog_ref, dx_ref,
             nw_ref, e_ref, e2_ref, et_ref, tril_ref, triu_ref, eye_ref,
             dz_ref, dxr_ref, ddtr_ref, dcw_ref, dcb_ref, ddtb_ref, dalog_ref, dd_ref, dnw_ref,
             g_scr, ext_scr, ext2_scr, arow_scr, acol_scr, eax_scr, darow_scr, dxdt_scr, t1_scr, t2_scr, dgg_scr):
        i = pl.program_id(0)

        @pl.when(i == 0)
        def _():
            g_scr[...] = jnp.zeros_like(g_scr)
            ext2_scr[Q:Q + 8, :] = jnp.zeros((8, XBC), F32)
            for r in (dcw_ref, dcb_ref, ddtb_ref, dalog_ref, dd_ref, dnw_ref):
                r[...] = jnp.zeros_like(r)
        not_first = jnp.where(i < NC - 1, 1.0, 0.0)
        ext_scr[0:8, :] = xp_ref[Q - 8:Q, :] * not_first
        ext_scr[8:8 + Q, :] = x_ref[...]
        (conv, sg, xbc, dtpre, dt, a, adt, acol, alast, ea, decs, dt_x, decs_x, acx2) = _ssd_common(
            x_ref, ext_scr, cw_ref, cb_ref, dt_ref, dtb_ref, alog_ref, e_ref, e2_ref, tril_ref, triu_ref,
            arow_scr, acol_scr, eax_scr)
        et_m = et_ref[...]
        xs = xbc[:, :SSD_W]
        xdt = xs * dt_x
        y = ypre_ref[...]
        zz = jnp.concatenate([z0_ref[...], z1_ref[...]], axis=1)
        sz = _sigmoid(zz)
        silu_z = zz * sz
        gg = y * silu_z
        dya = dya_ref[...]
        for g in range(2):
            gc = slice(g * 512, (g + 1) * 512)
            ggg = gg[:, gc]
            rstd = lax.rsqrt(jnp.mean(ggg * ggg, axis=-1, keepdims=True) + RMS_EPS)
            n = ggg * rstd
            dyag = dya[:, gc]
            dnw_ref[:, gc] += jnp.sum(dyag * n, axis=0, keepdims=True)
            dn = dyag * nw_ref[:, gc]
            dgg_scr[:, gc] = rstd * (dn - n * jnp.mean(dn * n, axis=-1, keepdims=True))
        dgg = dgg_scr[...]
        dy = dgg * silu_z
        dz_ref[...] = (dgg * y * (sz * (1.0 + zz * (1.0 - sz)))).astype(BF16)
        dd_ref[...] += _rowdot(jnp.sum(dy * xs, axis=0, keepdims=True), et_m)
        dxs = dy * dx_ref[...]
        dys = dy * eax_scr[...]
        lane = lax.broadcasted_iota(jnp.int32, (Q, 128), 1)
        causal = lax.broadcasted_iota(jnp.int32, (Q, Q), 0) >= lax.broadcasted_iota(jnp.int32, (Q, Q), 1)
        darow_scr[...] = jnp.zeros_like(darow_scr)
        dacol = jnp.zeros((Q, 128), F32)
        dcdx = []
        dbs = []
        dcs = []
        for g in range(2):
            gc = slice(g * 512, (g + 1) * 512)
            bg = xbc[:, SSD_W + g * NS:SSD_W + (g + 1) * NS].astype(BF16)
            cg = xbc[:, SSD_W + 2 * NS + g * NS:SSD_W + 2 * NS + (g + 1) * NS].astype(BF16)
            cbm = _dg(cg, bg, 1, 1)
            sgv = st_ref[0, g]
            sgb = sgv.astype(BF16)
            gv = g_scr[g]
            gvb = gv.astype(BF16)
            yoff = _dg(cg, sgb, 1, 0) * eax_scr[:, gc]
            dysg = dys[:, gc].astype(BF16)
            dcg = _dg(dysg, sgb, 1, 1)
            ds_off = _dg(cg, dysg, 0, 0)
            t1_scr[:, gc] = dy[:, gc] * yoff
            xdec = xdt[:, gc] * decs_x[:, gc]
            dxd = _dg(bg, gvb, 1, 0)
            dbg = _dg(xdec.astype(BF16), gvb, 1, 1)
            dxdt_g = dxd * decs_x[:, gc]
            t2_scr[:, gc] = dxd * xdt[:, gc]
            cdx = eax_scr[Q - 1:Q, gc]
            dcdx.append(jnp.sum(gv * sgv, axis=0, keepdims=True))
            g_scr[g] = gv * cdx + ds_off
            dcb_acc = jnp.zeros((Q, Q), F32)
            for pr in range(4):
                c0 = g * 512 + pr * 128
                xp = xdt[:, c0:c0 + 128].astype(BF16)
                dyp = dy[:, c0:c0 + 128]
                dypb = dyp.astype(BF16)
                halves = []
                for hh, keep in ((g * 8 + 2 * pr, lane < HP), (g * 8 + 2 * pr + 1, lane >= HP)):
                    lam = jnp.exp(jnp.where(causal, acx2[:, hh * 128:(hh + 1) * 128] - arow_scr[hh:hh + 1, :], NEG))
                    w = cbm * lam
                    dw = _dg(jnp.where(keep, dyp, 0.0).astype(BF16), xp, 1, 1)
                    dcb_acc = dcb_acc + dw * lam
                    t = dw * w
                    dacol = dacol + jnp.sum(t, axis=-1, keepdims=True) * (lane == hh).astype(F32)
                    darow_scr[hh:hh + 1, :] -= jnp.sum(t, axis=0, keepdims=True)
                    halves.append(_dg(w.astype(BF16), dypb, 0, 0))
                dxdt_scr[:, c0:c0 + 128] = (jnp.where(lane < HP, halves[0], halves[1])
                                            + dxdt_g[:, pr * 128:(pr + 1) * 128])
            dcbb = dcb_acc.astype(BF16)
            dcs.append(dcg + _dg(dcbb, bg, 1, 0))
            dbs.append(dbg + _dg(dcbb, cg, 0, 0))
        dacol = dacol + _xdot_l(t1_scr[...], et_m)
        ddecs = _xdot_l(t2_scr[...], et_m) * decs
        dacol = dacol - ddecs
        dalast = jnp.sum(ddecs, axis=0, keepdims=True)
        dcd = _rowdot(jnp.concatenate(dcdx, axis=1), et_m)
        dalast = dalast + dcd * jnp.exp(alast)
        dacol = dacol + _xdot_l(darow_scr[...], eye_ref[...], 0, 0)
        rowi = lax.broadcasted_iota(jnp.int32, (Q, 128), 0)
        dacol = dacol + jnp.where(rowi == Q - 1, dalast, 0.0)
        dadt = _xdot_r(triu_ref[...], dacol)
        dxdt = dxdt_scr[...]
        ddt = dadt * a + _xdot_l(dxdt * xs, et_m)
        dalog_ref[...] += jnp.sum(dadt * dt, axis=0, keepdims=True) * a
        dxs = dxs + dxdt * dt_x
        ddtr = ddt * _sigmoid(dtpre)
        ddtb_ref[...] += jnp.sum(ddtr, axis=0, keepdims=True)
        ddtr_ref[...] = ddtr.astype(BF16)
        dsilu = sg * (1.0 + conv * (1.0 - sg))
        ext2_scr[0:Q, 0:SSD_W] = dxs * dsilu[:, :SSD_W]
        for g in range(2):
            o1 = SSD_W + g * NS
            o2 = SSD_W + 2 * NS + g * NS
            ext2_scr[0:Q, o1:o1 + NS] = dbs[g] * dsilu[:, o1:o1 + NS]
            ext2_scr[0:Q, o2:o2 + NS] = dcs[g] * dsilu[:, o2:o2 + NS]
        dconv = ext2_scr[0:Q, :]
        dcb_ref[...] += jnp.sum(dconv, axis=0, keepdims=True)
        dxr = jnp.zeros((Q, XBC), F32)
        for k in range(4):
            dcw_ref[k:k + 1, :] += jnp.sum(dconv * ext_scr[pl.ds(5 + k, Q), :], axis=0, keepdims=True)
            dxr = dxr + cw_ref[k:k + 1, :] * ext2_scr[pl.ds(3 - k, Q), :]
        dxr_ref[...] = dxr.astype(BF16)
        ext2_scr[Q:Q + 8, :] = ext2_scr[0:8, :]

    def row(w, blk=0):
        return pl.BlockSpec((Q, w), lambda i: (NC - 1 - i, blk))

    def full(shape):
        return pl.BlockSpec(shape, lambda i: (0,) * len(shape))

    prev = pl.BlockSpec((Q, XBC), lambda i: (jnp.maximum(NC - 2 - i, 0), U_X // XBC))
    return pl.pallas_call(
        body, name=name, grid=(NC,),
        in_specs=[row(512, U_Z // 512), row(512, U_Z // 512 + 1), row(XBC, U_X // XBC), prev, row(128, U_DT // 128),
                  row(SSD_W),
                  pl.BlockSpec((1, 2, NS, 512), lambda i: (NC - 1 - i, 0, 0, 0)), row(SSD_W),
                  full((4, XBC)), full((1, XBC)), full((1, 128)), full((1, 128)), full((1, SSD_W)),
                  full((1, SSD_W)), full((128, SSD_W)), full((128, NH * 128)), full((SSD_W, 128)), full((Q, Q)),
                  full((Q, Q)), full((128, 128))],
        out_specs=[row(SSD_W), row(XBC), row(128), full((8, XBC)), full((1, XBC)), full((1, 128)), full((1, 128)),
                   full((1, 128)), full((1, SSD_W))],
        out_shape=[jax.ShapeDtypeStruct((L, SSD_W), BF16), jax.ShapeDtypeStruct((L, XBC), BF16),
                   jax.ShapeDtypeStruct((L, 128), BF16), jax.ShapeDtypeStruct((8, XBC), F32),
                   jax.ShapeDtypeStruct((1, XBC), F32), jax.ShapeDtypeStruct((1, 128), F32),
                   jax.ShapeDtypeStruct((1, 128), F32), jax.ShapeDtypeStruct((1, 128), F32),
                   jax.ShapeDtypeStruct((1, SSD_W), F32)],
        scratch_shapes=[pltpu.VMEM((2, NS, 512), F32), pltpu.VMEM((Q + 8, XBC), F32), pltpu.VMEM((Q + 8, XBC), F32),
                        pltpu.VMEM((128, Q), F32), pltpu.VMEM((Q, 128), F32), pltpu.VMEM((Q, SSD_W), F32),
                        pltpu.VMEM((128, Q), F32), pltpu.VMEM((Q, SSD_W), F32), pltpu.VMEM((Q, SSD_W), F32),
                        pltpu.VMEM((Q, SSD_W), F32), pltpu.VMEM((Q, SSD_W), F32)],
        compiler_params=_cp(("arbitrary",)))(u, u, u, u, u, ypre, st, dy, cw, cb, dtb, alog, dxp, nw,
                                             e, e2, et, tril, triu, eye)


def _my_pos():
    return lax.axis_index("x"), lax.axis_index("y"), lax.axis_index("c")


CHIP_REL = ((1, 0), (0, 1), (1, 1))
CHIP_XOR = (2, 1, 3)
NW = 4
AT = 256


def _chips(x, y):
    return [(1 - x if dx else x, 1 - y if dy else y) for dx, dy in CHIP_REL]


HBM_SPEC = pl.BlockSpec(memory_space=pltpu.HBM)
SEM_SPEC = pl.BlockSpec(memory_space=pltpu.SEMAPHORE)
EFFECT = pltpu.SideEffectType.DATAFLOW_SIDE_EFFECTING


def _hbm(t):
    return pltpu.with_memory_space_constraint(t, pltpu.HBM)


def _split_start(srcs, lands, after, copies, name):
    n = len(srcs)

    def body(*refs):
        src_refs, land_refs = refs[:n], refs[n:2 * n]
        send_sems, recv_sems = refs[2 * n + 1], refs[2 * n + 2]
        token = refs[-1]
        for w, k, src, dst, dev in copies(src_refs, land_refs):
            pltpu.make_async_remote_copy(src_ref=src, dst_ref=dst, send_sem=send_sems.at[3 * w + k],
                                         recv_sem=recv_sems.at[3 * w + k], device_id=dev, device_id_type=MESH).start()
        token[...] = jnp.zeros_like(token)

    outs = pl.pallas_call(
        body, name=name,
        out_shape=(pltpu.SemaphoreType.DMA((3 * n,)), pltpu.SemaphoreType.DMA((3 * n,)),
                   *[pltpu.HBM(t.shape, t.dtype) for t in srcs], *[pltpu.HBM(t.shape, t.dtype) for t in lands],
                   jax.ShapeDtypeStruct((8, 128), F32)),
        in_specs=[HBM_SPEC] * (2 * n) + [ANY],
        out_specs=(SEM_SPEC, SEM_SPEC, *([HBM_SPEC] * (2 * n)), VMEM_SPEC),
        input_output_aliases={i: 2 + i for i in range(2 * n)},
        compiler_params=pltpu.CompilerParams(has_side_effects=EFFECT))(
            *[_hbm(t) for t in srcs], *[_hbm(t) for t in lands], after)
    return outs[0], outs[1], list(outs[2:2 + n]), list(outs[2 + n:2 + 2 * n]), outs[-1]


def _split_wait(send_sems, recv_sems, srcs, lands, after, copies, name):
    n = len(srcs)

    def body(*refs):
        src_refs, land_refs = refs[:n], refs[n:2 * n]
        ssem, rsem = refs[2 * n], refs[2 * n + 1]
        for w, k, src, dst, dev in copies(src_refs, land_refs):
            cp = pltpu.make_async_remote_copy(src_ref=src, dst_ref=dst, send_sem=ssem.at[3 * w + k],
                                              recv_sem=rsem.at[3 * w + k], device_id=dev, device_id_type=MESH)
            cp.wait_send()
            cp.wait_recv()

    outs = pl.pallas_call(
        body, name=name,
        out_shape=tuple([pltpu.HBM(t.shape, t.dtype) for t in srcs] + [pltpu.HBM(t.shape, t.dtype) for t in lands]),
        in_specs=[HBM_SPEC] * (2 * n) + [SEM_SPEC, SEM_SPEC, ANY],
        out_specs=tuple([HBM_SPEC] * (2 * n)),
        input_output_aliases={i: i for i in range(2 * n)},
        compiler_params=pltpu.CompilerParams(has_side_effects=EFFECT))(*srcs, *lands, send_sems, recv_sems, after)
    return list(outs[:n]), list(outs[n:])


def _ag_copies(arrival):
    def copies(src_refs, land_refs):
        x, y, c = _my_pos()
        s = 2 * x + y
        chips = _chips(x, y)
        for w in range(NW):
            hr = src_refs[w].shape[0] // 2
            mine = pl.ds(c * hr, hr)
            for k in range(3):
                slot = s ^ CHIP_XOR[k] if arrival else s
                yield w, k, src_refs[w].at[mine], land_refs[w].at[slot, mine], (*chips[k], c)
    return copies


def _ag_forward(lands, name):
    def body(*refs):
        outs = refs[NW:2 * NW]
        send_sems, recv_sems = refs[2 * NW:]
        x, y, c = _my_pos()
        s = 2 * x + y
        sib = (x, y, 1 - c)
        sends = []
        for w in range(NW):
            hr = outs[w].shape[1] // 2
            for k in range(3):
                blk = outs[w].at[s ^ CHIP_XOR[k], pl.ds(c * hr, hr)]
                fw = pltpu.make_async_remote_copy(
                    src_ref=blk, dst_ref=blk, send_sem=send_sems.at[w, k], recv_sem=recv_sems.at[w, k],
                    device_id=sib, device_id_type=MESH)
                fw.start()
                sends.append(fw)
        for w in range(NW):
            hr = outs[w].shape[1] // 2
            for k in range(3):
                blk = outs[w].at[s ^ CHIP_XOR[k], pl.ds((1 - c) * hr, hr)]
                pltpu.make_async_remote_copy(
                    src_ref=blk, dst_ref=blk, send_sem=send_sems.at[w, k], recv_sem=recv_sems.at[w, k],
                    device_id=sib, device_id_type=MESH).wait_recv()
        for cp in sends:
            cp.wait_send()

    return pl.pallas_call(
        body, name=name, in_specs=[ANY] * NW, out_specs=[ANY] * NW,
        out_shape=[jax.ShapeDtypeStruct(t.shape, t.dtype) for t in lands],
        input_output_aliases={w: w for w in range(NW)},
        scratch_shapes=[pltpu.SemaphoreType.DMA((NW, 3)), pltpu.SemaphoreType.DMA((NW, 3))])(*lands)


def _rs_copies(src_refs, land_refs):
    x, y, c = _my_pos()
    s = 2 * x + y
    chips = _chips(x, y)
    for w in range(NW):
        for k in range(3):
            yield w, k, src_refs[w].at[s ^ CHIP_XOR[k]], land_refs[w].at[k], (*chips[k], c)


def _place_own(shard, gathered, sidx, name):
    r, cc = shard.shape

    def body(s_ref, a_ref, g_ref, o_ref):
        o_ref[0] = a_ref[...]

    return pl.pallas_call(
        body, name=name,
        grid_spec=pltpu.PrefetchScalarGridSpec(
            num_scalar_prefetch=1, grid=(r // AT,),
            in_specs=[pl.BlockSpec((AT, cc), lambda i, s_ref: (i, 0)), ANY],
            out_specs=pl.BlockSpec((1, AT, cc), lambda i, s_ref: (s_ref[0], i, 0))),
        out_shape=jax.ShapeDtypeStruct(gathered.shape, gathered.dtype),
        input_output_aliases={2: 0}, compiler_params=_cp(("parallel",)))(sidx, shard, gathered)


def _rs_pair(dwb, name):
    def body(*refs):
        ins, outs = refs[:NW], refs[NW:2 * NW]
        send_sems, recv_sems = refs[2 * NW:]
        x, y, c = _my_pos()
        cps = []
        for w in range(NW):
            hr = ins[w].shape[1] // 2
            cp = pltpu.make_async_remote_copy(
                src_ref=ins[w].at[:, pl.ds((1 - c) * hr, hr)], dst_ref=outs[w], send_sem=send_sems.at[w],
                recv_sem=recv_sems.at[w], device_id=(x, y, 1 - c), device_id_type=MESH)
            cp.start()
            cps.append(cp)
        for cp in cps:
            cp.wait()

    return pl.pallas_call(
        body, name=name, in_specs=[ANY] * NW, out_specs=[ANY] * NW,
        out_shape=[jax.ShapeDtypeStruct((4, t.shape[1] // 2, t.shape[2]), t.dtype) for t in dwb],
        scratch_shapes=[pltpu.SemaphoreType.DMA((NW,)), pltpu.SemaphoreType.DMA((NW,))])(*dwb)


def _rs_sib(q, name):
    def body(*refs):
        ins, outs = refs[:NW], refs[NW:2 * NW]
        send_sems, recv_sems = refs[2 * NW:]
        x, y, c = _my_pos()
        cps = []
        for w in range(NW):
            hr = outs[w].shape[0] // 2
            mine = pl.ds(c * hr, hr)
            cp = pltpu.make_async_remote_copy(
                src_ref=outs[w].at[mine], dst_ref=outs[w].at[mine], send_sem=send_sems.at[w],
                recv_sem=recv_sems.at[w], device_id=(x, y, 1 - c), device_id_type=MESH)
            cp.start()
            cps.append(cp)
        for w in range(NW):
            hr = outs[w].shape[0] // 2
            other = outs[w].at[pl.ds((1 - c) * hr, hr)]
            pltpu.make_async_remote_copy(
                src_ref=other, dst_ref=other, send_sem=send_sems.at[w], recv_sem=recv_sems.at[w],
                device_id=(x, y, 1 - c), device_id_type=MESH).wait_recv()
        for cp in cps:
            cp.wait_send()

    return pl.pallas_call(
        body, name=name, in_specs=[ANY] * NW, out_specs=[ANY] * NW,
        out_shape=[jax.ShapeDtypeStruct(t.shape, t.dtype) for t in q],
        input_output_aliases={w: w for w in range(NW)},
        scratch_shapes=[pltpu.SemaphoreType.DMA((NW,)), pltpu.SemaphoreType.DMA((NW,))])(*q)


def _rs_add2(dw, got, cidx, name):
    _, r, cc = dw.shape
    hr = r // 2
    nb = hr // AT

    def body(c_ref, a_ref, b_ref, o_ref, ob_ref):
        acc = a_ref[...] + b_ref[...].astype(F32)
        o_ref[...] = acc
        ob_ref[...] = acc.astype(BF16)

    blk = pl.BlockSpec((1, AT, cc), lambda sh, i, c_ref: (sh, i, 0))
    return pl.pallas_call(
        body, name=name,
        grid_spec=pltpu.PrefetchScalarGridSpec(
            num_scalar_prefetch=1, grid=(4, nb),
            in_specs=[pl.BlockSpec((1, AT, cc), lambda sh, i, c_ref: (sh, c_ref[0] * nb + i, 0)), blk],
            out_specs=[blk, blk]),
        out_shape=[jax.ShapeDtypeStruct((4, hr, cc), F32), jax.ShapeDtypeStruct((4, hr, cc), BF16)],
        compiler_params=_cp(("parallel", "parallel")))(cidx, dw, got)


def _rs_add4(p, got, scidx, name):
    _, hr, cc = p.shape
    nb = hr // AT

    def body(s_ref, p_ref, g0_ref, g1_ref, g2_ref, o_ref):
        acc = p_ref[0] + g0_ref[0].astype(F32)
        acc = acc + g1_ref[0].astype(F32)
        o_ref[...] = acc + g2_ref[0].astype(F32)

    def gk(k):
        return pl.BlockSpec((1, AT, cc), lambda i, s_ref: (k, i, 0))

    return pl.pallas_call(
        body, name=name,
        grid_spec=pltpu.PrefetchScalarGridSpec(
            num_scalar_prefetch=1, grid=(nb,),
            in_specs=[pl.BlockSpec((1, AT, cc), lambda i, s_ref: (s_ref[0], i, 0)), gk(0), gk(1), gk(2)],
            out_specs=pl.BlockSpec((AT, cc), lambda i, s_ref: (s_ref[1] * nb + i, 0))),
        out_shape=jax.ShapeDtypeStruct((2 * hr, cc), F32),
        compiler_params=_cp(("parallel",)))(scidx, p, got, got, got)


def _rs_begin(dws, dwbs, after):
    _, _, c = _my_pos()
    cidx = jnp.reshape(c, (1,)).astype(jnp.int32)
    got = _rs_pair(dwbs, "rs_pair")
    pairs = [_rs_add2(dws[w], got[w], cidx, "rs_add2_%d" % w) for w in range(NW)]
    pb = [p[1] for p in pairs]
    lands = [lax.empty((3,) + t.shape[1:], BF16) for t in pb]
    ssem, rsem, pb, lands, token = _split_start(pb, lands, after, _rs_copies, "rs_chip_start")
    return ([p[0] for p in pairs], ssem, rsem, pb, lands), token


def _rs_end(state, after):
    x, y, c = _my_pos()
    scidx = jnp.stack([2 * x + y, c]).astype(jnp.int32)
    p, ssem, rsem, pb, lands = state
    _, recv = _split_wait(ssem, rsem, pb, lands, after, _rs_copies, "rs_chip_wait")
    q = [_rs_add4(p[w], recv[w], scidx, "rs_add4_%d" % w) for w in range(NW)]
    return _rs_sib(q, "rs_sib")


def _allreduce_small(buf, name):
    rows = buf.shape[0]

    def body(src_ref, out_ref, gat_ref, send_sems, recv_sems):
        x, y, c = _my_pos()
        me = 4 * x + 2 * y + c
        gat_ref[me] = src_ref[...]
        cps = []
        for r in range(1, N_DEV):
            tx = 1 - x if (r >> 2) & 1 else x
            ty = 1 - y if (r >> 1) & 1 else y
            tc = 1 - c if r & 1 else c
            cps.append(pltpu.make_async_remote_copy(
                src_ref=src_ref, dst_ref=gat_ref.at[me], send_sem=send_sems.at[r - 1], recv_sem=recv_sems.at[r - 1],
                device_id=(tx, ty, tc), device_id_type=MESH))
        for cp in cps:
            cp.start()
        for cp in cps:
            cp.wait()
        acc = gat_ref[0]
        for k in range(1, N_DEV):
            acc = acc + gat_ref[k]
        out_ref[...] = acc

    return pl.pallas_call(
        body, name=name, in_specs=[VMEM_SPEC], out_specs=VMEM_SPEC, out_shape=jax.ShapeDtypeStruct((rows, 128), F32),
        scratch_shapes=[pltpu.VMEM((N_DEV, rows, 128), F32), pltpu.SemaphoreType.DMA((N_DEV - 1,)),
                        pltpu.SemaphoreType.DMA((N_DEV - 1,))],
        compiler_params=_cp())(buf)


SMALL = (("norm_mix_w", (D,)), ("ssd_conv_w", (4, XBC)), ("ssd_conv_b", (XBC,)), ("ssd_dt_bias", (NH,)),
         ("ssd_a_log", (NH,)), ("ssd_d", (NH,)), ("ssd_norm_w", (SSD_W,)), ("q_norm_w", (HD,)),
         ("k_norm_w", (HD,)), ("attn_sinks", (NQH,)), ("cm_dw_w", (CMK, CMC)), ("cm_dw_b", (CMC,)),
         ("cm_ln_w", (CMC,)), ("cm_ln_b", (CMC,)), ("norm_mlp_w", (D,)))
SHARDED_SMALL = ("ssd_conv_w", "cm_dw_w")


def _seg_len(shape):
    n = 1
    for d in shape:
        n *= d
    return -(-n // 128) * 128


def _pack_small(vals, names):
    parts = []
    for name, shape in SMALL:
        if name not in names:
            continue
        v = vals[name].reshape(DEPTH, -1)
        pad = _seg_len(shape) - v.shape[1]
        parts.append(jnp.pad(v, ((0, 0), (0, pad))))
    flat = jnp.concatenate(parts, axis=1)
    return flat.reshape(-1, 128)


def _unpack_small(buf, names):
    flat = buf.reshape(DEPTH, -1)
    out = {}
    off = 0
    for name, shape in SMALL:
        if name not in names:
            continue
        n = 1
        for d in shape:
            n *= d
        out[name] = flat[:, off:off + n].reshape((DEPTH,) + shape)
        off += _seg_len(shape)
    return out


def _rope_tables():
    inv = 10000.0 ** (-jnp.arange(0, HD, 2, dtype=F32) / HD)
    ang = jnp.arange(L, dtype=F32)[:, None] * inv[None, :]
    return jnp.cos(ang), jnp.sin(ang)


def _pad128(v):
    return jnp.pad(v, (0, 128 - v.shape[0]))[None, :]


def _heads(t, nh, w):
    return jnp.transpose(t.reshape(L, nh, w), (1, 0, 2))


def _unheads(t):
    nh, _, w = t.shape
    return jnp.transpose(t, (1, 0, 2)).reshape(L, nh * w)


def kernel(x, norm_mix_w, w_in, ssd_conv_w, ssd_conv_b, ssd_dt_bias, ssd_a_log, ssd_d, ssd_norm_w, q_norm_w, k_norm_w, attn_sinks, cm_dw_w, cm_dw_b, cm_ln_w, cm_ln_b, w_out, norm_mlp_w, w_mlp_up, w_mlp_down, loss_target, m_norm_mix_w, m_w_in, m_ssd_conv_w, m_ssd_conv_b, m_ssd_dt_bias, m_ssd_a_log, m_ssd_d, m_ssd_norm_w, m_q_norm_w, m_k_norm_w, m_attn_sinks, m_cm_dw_w, m_cm_dw_b, m_cm_ln_w, m_cm_ln_b, m_w_out, m_norm_mlp_w, m_w_mlp_up, m_w_mlp_down, v_norm_mix_w, v_w_in, v_ssd_conv_w, v_ssd_conv_b, v_ssd_dt_bias, v_ssd_a_log, v_ssd_d, v_ssd_norm_w, v_q_norm_w, v_k_norm_w, v_attn_sinks, v_cm_dw_w, v_cm_dw_b, v_cm_ln_w, v_cm_ln_b, v_w_out, v_norm_mlp_w, v_w_mlp_up, v_w_mlp_down):
    px, py, pc = _my_pos()
    shard = 2 * px + py
    sidx = jnp.reshape(shard, (1,)).astype(jnp.int32)
    consts = _ssd_consts()
    cos, sin = _rope_tables()

    wb = [w.astype(BF16) for w in (w_in, w_out, w_mlp_up, w_mlp_down)]
    zc = jnp.zeros((DEPTH, 4, XBC), F32)
    zc = lax.dynamic_update_slice_in_dim(zc, ssd_conv_w, shard * (XBC // 4), axis=2)
    zd = jnp.zeros((DEPTH, CMK, CMC), F32)
    zd = lax.dynamic_update_slice_in_dim(zd, cm_dw_w, shard * (CMC // 4), axis=2)
    half = jnp.where(pc == 0, 1.0, 0.0).astype(F32)
    gw = _allreduce_small(_pack_small({"ssd_conv_w": zc * half, "cm_dw_w": zd * half}, SHARDED_SMALL), "ag_small")
    gw = _unpack_small(gw, SHARDED_SMALL)
    conv_w_full, dw_w_full = gw["ssd_conv_w"], gw["cm_dw_w"]

    o_x, o_dt, o_q, o_g = 1024, 2560, 2576, 3344

    def gather_start(l, after):
        own = [t[l] for t in wb]
        lands = [lax.empty((4,) + t.shape, BF16) for t in own]
        return _split_start(own, lands, after, _ag_copies(False), "ag_start")

    xcur = x[0]
    saved = []
    zero_tile = jnp.zeros((8, 128), F32)
    in_flight = gather_start(0, zero_tile)
    for l in range(DEPTH):
        ssem, rsem, own, lands, _ = in_flight
        own, lands = _split_wait(ssem, rsem, own, lands, xcur, _ag_copies(True), "ag_wait")
        lands = _ag_forward(lands, "ag_forward")
        g_in, g_out, g_up, g_dn = [_place_own(own[w], g, sidx, "ag_place_%d" % w) for w, g in enumerate(lands)]
        nmw = norm_mix_w[l][None]
        if l + 1 < DEPTH:
            in_flight = gather_start(l + 1, g_dn)
            nmw = nmw + in_flight[4][0:1, 0:1]
        wo = jnp.concatenate([g_in[0], g_in[1], g_in[2], g_in[3]], axis=1)
        w_perm = jnp.concatenate([wo[:, o_x:o_dt], wo[:, :o_x], wo[:, o_g:], wo[:, o_q:o_g], wo[:, o_dt:o_q],
                                  jnp.zeros((D, 128 - NH), BF16)], axis=1)
        wout_l = g_out.reshape(1, 2 * D, D)
        wdn_l = g_dn.reshape(1, DFF, D)
        wup_l = g_up[None]
        h = _rms_fwd(xcur, nmw, "rms_mix_fwd")
        u = _mm(h, w_perm, "nn", "in_proj", tn=640)
        alog = _pad128(ssd_a_log[l])
        dtb = _pad128(ssd_dt_bias[l])
        dxp = jnp.repeat(ssd_d[l], HP)[None, :]
        ssd_p = (conv_w_full[l], ssd_conv_b[l][None], dtb, alog, dxp, ssd_norm_w[l][None])
        ya, ypre, st = _ssd_fwd(u, *ssd_p, consts, "ssd_fwd")
        q3 = _heads(u[:, U_Q:U_K], NQH, HD)
        k3 = _heads(u[:, U_K:U_V], NKV, HD)
        v3 = _heads(u[:, U_V:U_DT], NKV, HD)
        qa, qb, ka, kb = q3[..., :HH], q3[..., HH:], k3[..., :HH], k3[..., HH:]
        qw, kw = q_norm_w[l], k_norm_w[l]
        sinks = jnp.broadcast_to(attn_sinks[l][:, None, None], (NQH, 1, 128))
        attn_args = (qa, qb, ka, kb, v3, cos, sin, qw[None, :HH], qw[None, HH:], kw[None, :HH], kw[None, HH:], sinks)
        yb = _unheads(_attn_fwd(*attn_args, "attn_fwd")).astype(BF16)
        conf_p = (dw_w_full[l], cm_dw_b[l][None], cm_ln_w[l][None], cm_ln_b[l][None])
        yc = _conf_fwd(u, *conf_p, "conf_fwd")
        ycat = jnp.concatenate([ya, yb, yc], axis=1)
        x1 = _mm_wl(ycat, wout_l, 0, "nn", "out_proj", add=xcur)
        hm = _rms_fwd(x1, norm_mlp_w[l][None], "rms_mlp_fwd")
        a_up, r_up = _mm_up(hm, wup_l, 0, "nn", "mlp_up")
        x2 = _mm_wl(r_up, wdn_l, 0, "nn", "mlp_down", add=x1)
        saved.append(dict(x=xcur, h=h, u=u, ypre=ypre, st=st, attn_args=attn_args, conf_p=conf_p, ycat=ycat, x1=x1,
                          hm=hm, a_up=a_up, r_up=r_up, ssd_p=ssd_p, w_perm=w_perm, wout_l=wout_l, wdn_l=wdn_l,
                          wup_l=wup_l))
        xcur = x2

    lsum, dx, dxb = _loss_bwd(xcur, loss_target[0], "loss")

    gbig = [None] * DEPTH
    pending = None
    gsm = {name: [] for name, _ in SMALL}
    for l in reversed(range(DEPTH)):
        sv = saved[l]
        da = _mm_wl(dxb, sv["wdn_l"], 0, "nt", "mlp_down_dx", relu2_of=sv["a_up"])
        dwdn, dwdn_b = _mm_dw(sv["r_up"], dxb, "mlp_down_dw")
        dwup, dwup_b = _mm_dw(sv["hm"], da, "mlp_up_dw", col_shards=True)
        dhm = _mm_up(da, sv["wup_l"], 0, "nt", "mlp_up_dx")
        dx1, dx1b, dnw = _rms_bwd(sv["x1"], norm_mlp_w[l][None], dhm, dx, "rms_mlp_bwd")
        gsm["norm_mlp_w"].append(dnw[0])
        dy = _mm_wl(dx1b, sv["wout_l"], 0, "nt", "out_proj_dx")
        dwout, dwout_b = _mm_dw(sv["ycat"], dx1b, "out_proj_dw")
        da_c, dg_c, dww, dwb, dlw, dlb = _conf_bwd(sv["u"], *sv["conf_p"], dy, "conf_bwd")
        gsm["cm_dw_w"].append(dww[:CMK])
        gsm["cm_dw_b"].append(dwb[0])
        gsm["cm_ln_w"].append(dlw[0])
        gsm["cm_ln_b"].append(dlb[0])
        do3 = _heads(dy[:, SSD_W:SSD_W + ATT_W], NQH, HD)
        (dqa, dqb, dka, dkb, dv3, dqwa, dqwb, dkwa, dkwb, dsk) = _attn_bwd(*sv["attn_args"], do3, "attn_bwd")
        dq = _unheads(jnp.concatenate([dqa, dqb], axis=-1))
        dk = _unheads(jnp.concatenate([dka, dkb], axis=-1))
        dv = _unheads(dv3)
        gsm["q_norm_w"].append(jnp.concatenate([dqwa[0], dqwb[0]]))
        gsm["k_norm_w"].append(jnp.concatenate([dkwa[0], dkwb[0]]))
        gsm["attn_sinks"].append(dsk[:, 0, 0])
        (dz, dxr, ddtr, dcw, dcb, ddtb, dalog, ddd, dnsw) = _ssd_bwd(
            sv["u"], sv["ypre"], sv["st"], dy, *sv["ssd_p"], consts, "ssd_bwd")
        gsm["ssd_conv_w"].append(dcw[:4])
        gsm["ssd_conv_b"].append(dcb[0])
        gsm["ssd_dt_bias"].append(ddtb[0, :NH])
        gsm["ssd_a_log"].append(dalog[0, :NH])
        gsm["ssd_d"].append(ddd[0, :NH])
        gsm["ssd_norm_w"].append(dnsw[0])
        du = jnp.concatenate([dxr, dz, da_c, dg_c, dq.astype(BF16), dk.astype(BF16), dv.astype(BF16), ddtr], axis=1)
        dwp, dwp_b = _mm_dw(sv["h"], du, "in_dw", tn=640)

        def unperm(t):
            t = jnp.concatenate([t[:, U_Z:U_A], t[:, U_X:U_Z], t[:, U_DT:U_DT + NH], t[:, U_Q:U_DT], t[:, U_A:U_Q]],
                                axis=1)
            return jnp.transpose(t.reshape(D, 4, N_IN // 4), (1, 0, 2))

        state, token = _rs_begin(
            [unperm(dwp), dwout.reshape(4, D // 2, D), dwup, dwdn.reshape(4, D, D)],
            [unperm(dwp_b), dwout_b.reshape(4, D // 2, D), dwup_b, dwdn_b.reshape(4, D, D)], zero_tile)
        dh = _mm(du, sv["w_perm"], "nt", "in_dx")
        dx, dxb, dnm = _rms_bwd(sv["x"], norm_mix_w[l][None] + token[0:1, 0:1], dh, dx1, "rms_mix_bwd")
        gsm["norm_mix_w"].append(dnm[0])
        if pending is not None:
            gbig[l + 1] = _rs_end(pending, dx)
        pending = state
    gbig[0] = _rs_end(pending, dx)

    gsm = {k: jnp.stack(v[::-1]) for k, v in gsm.items()}
    packed = _pack_small(gsm, [n for n, _ in SMALL])
    packed = jnp.concatenate([packed, lsum], axis=0)
    red = _allreduce_small(packed, "ar_small")
    loss = 0.5 * red[-8, 0] / D
    gsm = _unpack_small(red[:-8], [n for n, _ in SMALL])
    gsm["ssd_conv_w"] = lax.dynamic_slice_in_dim(gsm["ssd_conv_w"], shard * (XBC // 4), XBC // 4, axis=2)
    gsm["cm_dw_w"] = lax.dynamic_slice_in_dim(gsm["cm_dw_w"], shard * (CMC // 4), CMC // 4, axis=2)
    grads = dict(gsm)
    for w, n in enumerate(("w_in", "w_out", "w_mlp_up", "w_mlp_down")):
        grads[n] = jnp.stack([gbig[l][w] for l in range(DEPTH)])

    loc = locals()
    names = ["norm_mix_w", "w_in", "ssd_conv_w", "ssd_conv_b", "ssd_dt_bias", "ssd_a_log", "ssd_d", "ssd_norm_w",
             "q_norm_w", "k_norm_w", "attn_sinks", "cm_dw_w", "cm_dw_b", "cm_ln_w", "cm_ln_b", "w_out", "norm_mlp_w",
             "w_mlp_up", "w_mlp_down"]
    weights = {n: loc[n] for n in names}
    moms = {n: loc["m_" + n] for n in names}
    vars_ = {n: loc["v_" + n] for n in names}
    delta, new_m, new_v = {}, {}, {}
    packed_names = [n for n, _ in SMALL if n not in SHARDED_SMALL]
    pw = _pack_small(weights, packed_names)
    pg = _pack_small(grads, packed_names)
    pm = _pack_small(moms, packed_names)
    pv = _pack_small(vars_, packed_names)
    pd, pmn, pvn = _adamw(pw, pg, pm, pv, "adamw_small")
    for dst, buf in ((delta, pd), (new_m, pmn), (new_v, pvn)):
        dst.update(_unpack_small(buf, packed_names))
    for n in ("w_in", "w_out", "w_mlp_up", "w_mlp_down", "ssd_conv_w", "cm_dw_w"):
        shp = weights[n].shape
        flat = lambda t: t.reshape(-1, shp[-1])
        d_, m_, v_ = _adamw(flat(weights[n]), flat(grads[n]), flat(moms[n]), flat(vars_[n]), "adamw_" + n)
        delta[n], new_m[n], new_v[n] = d_.reshape(shp), m_.reshape(shp), v_.reshape(shp)

    return (loss, dx[None], *[grads[n] for n in names], *[delta[n] for n in names],
            *[new_m[n] for n in names], *[new_v[n] for n in names])
```

```python
import functools
import math

import jax
import jax.numpy as jnp
from jax import lax
from jax.experimental import pallas as pl
from jax.experimental.pallas import tpu as pltpu

F32 = jnp.float32
BF16 = jnp.bfloat16
MESH = pl.DeviceIdType.MESH
ANY = pl.BlockSpec(memory_space=pl.ANY)
VMEM_SPEC = pl.BlockSpec(memory_space=pltpu.VMEM)

D = 1024
L = 2048
DEPTH = 4
SSD_W = 1024
XBC = 1536
NH = 16
HP = 64
NS = 128
Q = 128
NC = L // Q
ATT_W = 512
NQH = 8
NKV = 2
HD = 64
HH = HD // 2
CMC = 512
CMK = 31
DFF = 4096
N_IN = 4368
N_PAD = 4480
RMS_EPS = 1e-6
LN_EPS = 1e-5
NEG = -1e30
LR, B1, B2, EPS_A, WD, STEP = 0.001, 0.9, 0.999, 1e-8, 0.01, 10
VMEM_LIMIT = 56 * 1024 * 1024
N_DEV = 8


def _cp(sem=None):
    kw = dict(vmem_limit_bytes=VMEM_LIMIT)
    if sem is not None:
        kw["dimension_semantics"] = sem
    return pltpu.CompilerParams(**kw)


def _dg(a, b, ca, cb):
    return lax.dot_general(a, b, (((ca,), (cb,)), ((), ())), preferred_element_type=F32)


def _split3(x):
    hi = x.astype(BF16)
    r = x - hi.astype(F32)
    mid = r.astype(BF16)
    lo = (r - mid.astype(F32)).astype(BF16)
    return hi, mid, lo


def _xdot_l(x, m, ca=1, cb=0):
    hi, mid, lo = _split3(x)
    return _dg(hi, m, ca, cb) + _dg(mid, m, ca, cb) + _dg(lo, m, ca, cb)


def _xdot_r(m, x, ca=1, cb=0):
    hi, mid, lo = _split3(x)
    return _dg(m, hi, ca, cb) + _dg(m, mid, ca, cb) + _dg(m, lo, ca, cb)


def _rowdot(v, m):
    return _xdot_l(jnp.broadcast_to(v, (8, v.shape[1])), m)[0:1]


def _sigmoid(x):
    return 1.0 / (1.0 + jnp.exp(-x))


def _softplus(x):
    e = jnp.exp(-jnp.abs(x))
    u = 1.0 + e
    l1p = jnp.where(u == 1.0, e, jnp.log(u) * (e / jnp.where(u == 1.0, 1.0, u - 1.0)))
    return jnp.maximum(x, 0.0) + l1p


def _tm(k):
    return L if k <= D else L // 2


def _mm(a, b, mode, name, add=None, relu2_of=None, tn=512):
    m, k = a.shape
    tm = min(m, _tm(k))
    a_spec = pl.BlockSpec((tm, k), lambda i, j: (i, 0))
    if mode == "nn":
        n = b.shape[1]
        b_spec = pl.BlockSpec((k, tn), lambda i, j: (0, j))
        cb = 0
    else:
        n = b.shape[0]
        b_spec = pl.BlockSpec((tn, k), lambda i, j: (j, 0))
        cb = 1
    assert m % tm == 0 and n % tn == 0, (m, n, tm, tn)
    o_spec = pl.BlockSpec((tm, tn), lambda i, j: (i, j))
    out_dtype = F32
    if relu2_of is not None:
        def body(a_ref, b_ref, c_ref, o_ref):
            o_ref[...] = (_dg(a_ref[...], b_ref[...], 1, cb) * (2.0 * jnp.sqrt(c_ref[...].astype(F32)))).astype(BF16)
        ins, specs, out_dtype = (a, b, relu2_of), [a_spec, b_spec, o_spec], BF16
    elif add is None:
        def body(a_ref, b_ref, o_ref):
            o_ref[...] = _dg(a_ref[...], b_ref[...], 1, cb)
        ins, specs = (a, b), [a_spec, b_spec]
    else:
        def body(a_ref, b_ref, c_ref, o_ref):
            o_ref[...] = _dg(a_ref[...], b_ref[...], 1, cb) + c_ref[...]
        ins, specs = (a, b, add), [a_spec, b_spec, o_spec]
    return pl.pallas_call(
        body, name=name, grid=(m // tm, n // tn), in_specs=specs, out_specs=o_spec,
        out_shape=jax.ShapeDtypeStruct((m, n), out_dtype), compiler_params=_cp(("parallel", "parallel")))(*ins)


def _mm_up(a, b, name, tn=512):
    m = a.shape[0]
    cs = DFF // 4
    per = cs // tn
    tm = min(m, _tm(D))

    def body(a_ref, b_ref, r_ref):
        r = jnp.maximum(_dg(a_ref[...], b_ref[0], 1, 0), 0.0)
        r_ref[...] = (r * r).astype(BF16)

    return pl.pallas_call(
        body, name=name, grid=(m // tm, DFF // tn),
        in_specs=[pl.BlockSpec((tm, D), lambda i, j: (i, 0)),
                  pl.BlockSpec((1, D, tn), lambda i, j: (j // per, 0, j % per))],
        out_specs=pl.BlockSpec((tm, tn), lambda i, j: (i, j)),
        out_shape=jax.ShapeDtypeStruct((m, DFF), BF16), compiler_params=_cp(("parallel", "parallel")))(a, b)


def _mm_cs_nt(a, b, name, tn=512):
    m = a.shape[0]
    _, n, cs = b.shape
    tm = min(m, _tm(4 * cs))

    def body(a_ref, b_ref, o_ref):
        acc = _dg(a_ref[:, 0:cs], b_ref[0], 1, 1)
        for s in range(1, 4):
            acc = acc + _dg(a_ref[:, s * cs:(s + 1) * cs], b_ref[s], 1, 1)
        o_ref[...] = acc

    return pl.pallas_call(
        body, name=name, grid=(m // tm, n // tn),
        in_specs=[pl.BlockSpec((tm, 4 * cs), lambda i, j: (i, 0)), pl.BlockSpec((4, tn, cs), lambda i, j: (0, j, 0))],
        out_specs=pl.BlockSpec((tm, tn), lambda i, j: (i, j)),
        out_shape=jax.ShapeDtypeStruct((m, n), F32), compiler_params=_cp(("parallel", "parallel")))(a, b)


def _mm_dw(a, b, name, col_shards=False, tn=512):
    k, m = a.shape
    n = b.shape[1]
    tm = min(m, D)
    a_spec = pl.BlockSpec((k, tm), lambda i, j: (0, i))
    b_spec = pl.BlockSpec((k, tn), lambda i, j: (0, j))
    if col_shards:
        per = (n // 4) // tn
        o_spec = pl.BlockSpec((1, tm, tn), lambda i, j: (j // per, i, j % per))
        shape = (4, m, n // 4)
    else:
        o_spec = pl.BlockSpec((tm, tn), lambda i, j: (i, j))
        shape = (m, n)

    def body(a_ref, b_ref, o_ref, ob_ref):
        acc = _dg(a_ref[...], b_ref[...], 0, 0).reshape(o_ref.shape)
        o_ref[...] = acc
        ob_ref[...] = acc.astype(BF16)

    return pl.pallas_call(
        body, name=name, grid=(m // tm, n // tn), in_specs=[a_spec, b_spec], out_specs=[o_spec, o_spec],
        out_shape=[jax.ShapeDtypeStruct(shape, F32), jax.ShapeDtypeStruct(shape, BF16)],
        compiler_params=_cp(("parallel", "parallel")))(a, b)


TR = 256


def _rms_fwd(x, w, name):
    def body(x_ref, w_ref, o_ref):
        xv = x_ref[...]
        r = lax.rsqrt(jnp.mean(xv * xv, axis=-1, keepdims=True) + RMS_EPS)
        o_ref[...] = (xv * r * w_ref[...]).astype(BF16)

    return pl.pallas_call(
        body, name=name, grid=(L // TR,),
        in_specs=[pl.BlockSpec((TR, D), lambda i: (i, 0)), pl.BlockSpec((1, D), lambda i: (0, 0))],
        out_specs=pl.BlockSpec((TR, D), lambda i: (i, 0)),
        out_shape=jax.ShapeDtypeStruct((L, D), BF16), compiler_params=_cp(("parallel",)))(x, w)


def _rms_bwd(x, w, dh, dres, name):
    def body(x_ref, w_ref, dh_ref, dr_ref, dx_ref, dxb_ref, dw_ref):
        xv = x_ref[...]
        r = lax.rsqrt(jnp.mean(xv * xv, axis=-1, keepdims=True) + RMS_EPS)
        n = xv * r
        dhv = dh_ref[...]
        g = dhv * w_ref[...]
        dx = dr_ref[...] + r * (g - n * jnp.mean(g * n, axis=-1, keepdims=True))
        dx_ref[...] = dx
        dxb_ref[...] = dx.astype(BF16)

        @pl.when(pl.program_id(0) == 0)
        def _():
            dw_ref[...] = jnp.zeros_like(dw_ref)
        dw_ref[...] += jnp.sum(dhv * n, axis=0, keepdims=True)

    row = pl.BlockSpec((TR, D), lambda i: (i, 0))
    vec = pl.BlockSpec((1, D), lambda i: (0, 0))
    return pl.pallas_call(
        body, name=name, grid=(L // TR,), in_specs=[row, vec, row, row], out_specs=[row, row, vec],
        out_shape=[jax.ShapeDtypeStruct((L, D), F32), jax.ShapeDtypeStruct((L, D), BF16),
                   jax.ShapeDtypeStruct((1, D), F32)],
        compiler_params=_cp(("arbitrary",)))(x, w, dh, dres)


def _loss_bwd(y, t, name):
    def body(y_ref, t_ref, l_ref, d_ref, db_ref):
        e = y_ref[...] - t_ref[...]
        d = e * (1.0 / D)
        d_ref[...] = d
        db_ref[...] = d.astype(BF16)

        @pl.when(pl.program_id(0) == 0)
        def _():
            l_ref[...] = jnp.zeros_like(l_ref)
        s = jnp.sum(jnp.sum(e * e, axis=-1, keepdims=True), axis=0, keepdims=True)
        l_ref[...] += jnp.broadcast_to(s, l_ref.shape)

    row = pl.BlockSpec((TR, D), lambda i: (i, 0))
    tile = pl.BlockSpec((8, 128), lambda i: (0, 0))
    return pl.pallas_call(
        body, name=name, grid=(L // TR,), in_specs=[row, row], out_specs=[tile, row, row],
        out_shape=[jax.ShapeDtypeStruct((8, 128), F32), jax.ShapeDtypeStruct((L, D), F32),
                   jax.ShapeDtypeStruct((L, D), BF16)],
        compiler_params=_cp(("arbitrary",)))(y, t)


def _adamw(w, g, m, v, name):
    rows, cols = w.shape
    tr = rows
    for cand in (512, 256, 128, 64, 32, 16, 8):
        if rows % cand == 0 and cand * cols * 4 <= 2 * 1024 * 1024:
            tr = cand
            break
    c1 = 1.0 / (1.0 - B1 ** STEP)
    c2 = 1.0 / (1.0 - B2 ** STEP)

    def body(w_ref, g_ref, m_ref, v_ref, d_ref, mo_ref, vo_ref):
        gv = g_ref[...]
        mn = B1 * m_ref[...] + (1.0 - B1) * gv
        vn = B2 * v_ref[...] + (1.0 - B2) * (gv * gv)
        mo_ref[...] = mn
        vo_ref[...] = vn
        d_ref[...] = -LR * ((mn * c1) / (jnp.sqrt(vn * c2) + EPS_A) + WD * w_ref[...])

    blk = pl.BlockSpec((tr, cols), lambda i: (i, 0))
    shp = jax.ShapeDtypeStruct((rows, cols), F32)
    return pl.pallas_call(body, name=name, grid=(rows // tr,), in_specs=[blk] * 4, out_specs=[blk] * 3,
                          out_shape=[shp] * 3, compiler_params=_cp(("parallel",)))(w, g, m, v)


CT = 256
CPAD = 32


U_X, U_Z, U_A, U_G, U_Q, U_K, U_V, U_DT = 0, 1536, 2560, 3072, 3584, 4096, 4224, 4352


def _cfull(shape):
    return pl.BlockSpec(shape, lambda i: (0,) * len(shape))


def _conf_in_specs():
    return [pl.BlockSpec((L, CMC), lambda i: (0, U_A // CMC)), pl.BlockSpec((L, CMC), lambda i: (0, U_G // CMC)),
            _cfull((CMK, CMC)), _cfull((1, CMC)), _cfull((1, CMC)), _cfull((1, CMC))]


CEXT = 8
CWIN = CT + CPAD
CROWS = L + CPAD + CEXT


def _fill_shifted(src_ref, base, win_ref, sh_ref):
    win_ref[...] = src_ref[pl.ds(base, CWIN + CEXT), :]
    for p in range(8):
        sh_ref[p] = win_ref[pl.ds(p, CWIN), :]


def _tap(sh_ref, o):
    return sh_ref[o % 8, 8 * (o // 8):8 * (o // 8) + CT, :]


def _conf_scratch(n_padded):
    return ([pltpu.VMEM((CROWS, CMC), F32)] * n_padded
            + [pltpu.VMEM((CWIN + CEXT, CMC), F32), pltpu.VMEM((8, CWIN, CMC), F32)])


def _conf_fwd(u, w, b, lw, lb, name):
    def body(a_ref, g_ref, w_ref, b_ref, lw_ref, lb_ref, o_ref, hp_ref, win_ref, sh_ref):
        hp_ref[0:CPAD, :] = jnp.zeros((CPAD, CMC), F32)
        hp_ref[CPAD:CPAD + L, :] = a_ref[...] * _sigmoid(g_ref[...])
        hp_ref[CPAD + L:, :] = jnp.zeros((CEXT, CMC), F32)

        def tile(i, carry):
            base = pl.multiple_of(i * CT, CT)
            _fill_shifted(hp_ref, base, win_ref, sh_ref)
            c = jnp.broadcast_to(b_ref[...], (CT, CMC))
            for k in range(CMK):
                c = c + w_ref[k:k + 1, :] * _tap(sh_ref, 2 + k)
            mu = jnp.mean(c, axis=-1, keepdims=True)
            cc = c - mu
            var = jnp.mean(cc * cc, axis=-1, keepdims=True)
            l = cc * lax.rsqrt(var + LN_EPS) * lw_ref[...] + lb_ref[...]
            o_ref[pl.ds(base, CT), :] = (l * _sigmoid(l)).astype(BF16)
            return carry

        lax.fori_loop(0, L // CT, tile, 0)

    return pl.pallas_call(
        body, name=name, grid=(1,), in_specs=_conf_in_specs(), out_specs=_cfull((L, CMC)),
        out_shape=jax.ShapeDtypeStruct((L, CMC), BF16), scratch_shapes=_conf_scratch(1),
        compiler_params=_cp(("arbitrary",)))(u, u, w, b, lw, lb)


def _conf_bwd(u, w, b, lw, lb, dy, name):
    def body(a_ref, g_ref, w_ref, b_ref, lw_ref, lb_ref, dy_ref,
             da_ref, dg_ref, dw_ref, db_ref, dlw_ref, dlb_ref, hp_ref, dcp_ref, win_ref, sh_ref):
        hp_ref[0:CPAD, :] = jnp.zeros((CPAD, CMC), F32)
        hp_ref[CPAD:CPAD + L, :] = a_ref[...] * _sigmoid(g_ref[...])
        hp_ref[CPAD + L:, :] = jnp.zeros((CEXT, CMC), F32)
        dcp_ref[L:, :] = jnp.zeros((CPAD + CEXT, CMC), F32)
        dw_ref[...] = jnp.zeros_like(dw_ref)
        db_ref[...] = jnp.zeros_like(db_ref)
        dlw_ref[...] = jnp.zeros_like(dlw_ref)
        dlb_ref[...] = jnp.zeros_like(dlb_ref)

        def tile1(i, carry):
            base = pl.multiple_of(i * CT, CT)
            _fill_shifted(hp_ref, base, win_ref, sh_ref)
            c = jnp.broadcast_to(b_ref[...], (CT, CMC))
            for k in range(CMK):
                c = c + w_ref[k:k + 1, :] * _tap(sh_ref, 2 + k)
            mu = jnp.mean(c, axis=-1, keepdims=True)
            cc = c - mu
            var = jnp.mean(cc * cc, axis=-1, keepdims=True)
            rstd = lax.rsqrt(var + LN_EPS)
            n = cc * rstd
            l = n * lw_ref[...] + lb_ref[...]
            sl = _sigmoid(l)
            dl = dy_ref[pl.ds(base, CT), :] * (sl * (1.0 + l * (1.0 - sl)))
            dlw_ref[...] += jnp.sum(dl * n, axis=0, keepdims=True)
            dlb_ref[...] += jnp.sum(dl, axis=0, keepdims=True)
            dn = dl * lw_ref[...]
            dc = rstd * (dn - jnp.mean(dn, axis=-1, keepdims=True)
                         - n * jnp.mean(dn * n, axis=-1, keepdims=True))
            db_ref[...] += jnp.sum(dc, axis=0, keepdims=True)
            for k in range(CMK):
                dw_ref[k:k + 1, :] += jnp.sum(dc * _tap(sh_ref, 2 + k), axis=0, keepdims=True)
            dcp_ref[pl.ds(base, CT), :] = dc
            return carry

        lax.fori_loop(0, L // CT, tile1, 0)

        def tile2(i, carry):
            base = pl.multiple_of(i * CT, CT)
            _fill_shifted(dcp_ref, base, win_ref, sh_ref)
            dh = jnp.zeros((CT, CMC), F32)
            for k in range(CMK):
                dh = dh + w_ref[k:k + 1, :] * _tap(sh_ref, CMK - 1 - k)
            av = a_ref[pl.ds(base, CT), :]
            sg = _sigmoid(g_ref[pl.ds(base, CT), :])
            da_ref[pl.ds(base, CT), :] = (dh * sg).astype(BF16)
            dg_ref[pl.ds(base, CT), :] = (dh * av * sg * (1.0 - sg)).astype(BF16)
            return carry

        lax.fori_loop(0, L // CT, tile2, 0)

    vec = jax.ShapeDtypeStruct((1, CMC), F32)
    dy_spec = pl.BlockSpec((L, CMC), lambda i: (0, (SSD_W + ATT_W) // CMC))
    return pl.pallas_call(
        body, name=name, grid=(1,), in_specs=_conf_in_specs() + [dy_spec],
        out_specs=[_cfull((L, CMC)), _cfull((L, CMC)), _cfull((32, CMC)), _cfull((1, CMC)), _cfull((1, CMC)),
                   _cfull((1, CMC))],
        out_shape=[jax.ShapeDtypeStruct((L, CMC), BF16), jax.ShapeDtypeStruct((L, CMC), BF16),
                   jax.ShapeDtypeStruct((32, CMC), F32), vec, vec, vec],
        scratch_shapes=_conf_scratch(2),
        compiler_params=_cp(("arbitrary",)))(u, u, w, b, lw, lb, dy)


def _attn_prep(a, b, wa, wb, c, s):
    r = lax.rsqrt((jnp.sum(a * a, axis=-1, keepdims=True) + jnp.sum(b * b, axis=-1, keepdims=True))
                  * (1.0 / HD) + RMS_EPS)
    ha = a * r
    hb = b * r
    na = ha * wa
    nb = hb * wb
    return r, ha, hb, na * c - nb * s, nb * c + na * s


def _attn_scores(qa_n, qb_n, ka_c, kb_c, n, lo, sink):
    nk = ka_c.shape[0]
    s = (_dg(qa_n, ka_c, 1, 1) + _dg(qb_n, kb_c, 1, 1)) * (1.0 / math.sqrt(HD))
    qi = lax.broadcasted_iota(jnp.int32, (Q, nk), 0) + n * Q
    ki = lax.broadcasted_iota(jnp.int32, (Q, nk), 1) + lo
    diff = qi - ki
    s = jnp.where((diff >= 0) & (diff < Q), s, NEG)
    m = jnp.maximum(jnp.max(s, axis=-1, keepdims=True), sink)
    p = jnp.exp(s - m)
    ps = jnp.exp(sink - m)
    den = jnp.sum(p, axis=-1, keepdims=True) + ps
    return p / den, ps / den


def _attn_specs():
    qh = pl.BlockSpec((1, L, HH), lambda j: (j, 0, 0))
    kh = pl.BlockSpec((1, L, HH), lambda j: (j // 4, 0, 0))
    vv = pl.BlockSpec((1, L, HD), lambda j: (j // 4, 0, 0))
    tab = pl.BlockSpec((L, HH), lambda j: (0, 0))
    wv = pl.BlockSpec((1, HH), lambda j: (0, 0))
    sk = pl.BlockSpec((1, 1, 128), lambda j: (j, 0, 0))
    return qh, kh, vv, tab, wv, sk


def _attn_fwd(qa, qb, ka, kb, v, cos, sin, qwa, qwb, kwa, kwb, sinks, name):
    def body(qa_ref, qb_ref, ka_ref, kb_ref, v_ref, c_ref, s_ref, qwa_ref, qwb_ref, kwa_ref, kwb_ref,
             sk_ref, o_ref):
        c = c_ref[...]
        s = s_ref[...]
        _, _, _, qra, qrb = _attn_prep(qa_ref[0], qb_ref[0], qwa_ref[...], qwb_ref[...], c, s)
        _, _, _, kra, krb = _attn_prep(ka_ref[0], kb_ref[0], kwa_ref[...], kwb_ref[...], c, s)
        qra, qrb, kra, krb = (t.astype(BF16) for t in (qra, qrb, kra, krb))
        vb = v_ref[0].astype(BF16)
        sink = sk_ref[0][:, 0:1]
        for n in range(NC):
            lo = max(n - 1, 0) * Q
            hi = (n + 1) * Q
            p, _ = _attn_scores(qra[n * Q:hi], qrb[n * Q:hi], kra[lo:hi], krb[lo:hi], n, lo, sink)
            o_ref[0, n * Q:hi, :] = _dg(p.astype(BF16), vb[lo:hi], 1, 0)

    qh, kh, vv, tab, wv, sk = _attn_specs()
    return pl.pallas_call(
        body, name=name, grid=(NQH,),
        in_specs=[qh, qh, kh, kh, vv, tab, tab, wv, wv, wv, wv, sk],
        out_specs=pl.BlockSpec((1, L, HD), lambda j: (j, 0, 0)),
        out_shape=jax.ShapeDtypeStruct((NQH, L, HD), F32),
        compiler_params=_cp(("parallel",)))(qa, qb, ka, kb, v, cos, sin, qwa, qwb, kwa, kwb, sinks)


def _attn_bwd(qa, qb, ka, kb, v, cos, sin, qwa, qwb, kwa, kwb, sinks, do, name):
    def body(qa_ref, qb_ref, ka_ref, kb_ref, v_ref, c_ref, s_ref, qwa_ref, qwb_ref, kwa_ref, kwb_ref,
             sk_ref, do_ref,
             dqa_ref, dqb_ref, dka_ref, dkb_ref, dv_ref, dqwa_ref, dqwb_ref, dkwa_ref, dkwb_ref, dsk_ref,
             dqra_s, dqrb_s, dkra_s, dkrb_s, dv_s):
        j = pl.program_id(0)
        c = c_ref[...]
        s = s_ref[...]
        qwa, qwb, kwa, kwb = qwa_ref[...], qwb_ref[...], kwa_ref[...], kwb_ref[...]
        qr, qha, qhb, qra, qrb = _attn_prep(qa_ref[0], qb_ref[0], qwa, qwb, c, s)
        kr, kha, khb, kra, krb = _attn_prep(ka_ref[0], kb_ref[0], kwa, kwb, c, s)
        qra, qrb, kra, krb = (t.astype(BF16) for t in (qra, qrb, kra, krb))
        vb = v_ref[0].astype(BF16)
        sink = sk_ref[0][:, 0:1]
        dkra_s[...] = jnp.zeros_like(dkra_s)
        dkrb_s[...] = jnp.zeros_like(dkrb_s)
        dv_s[...] = jnp.zeros_like(dv_s)
        dsink = jnp.zeros((1, 1), F32)
        scale = 1.0 / math.sqrt(HD)
        for n in range(NC):
            lo = max(n - 1, 0) * Q
            hi = (n + 1) * Q
            p, ps = _attn_scores(qra[n * Q:hi], qrb[n * Q:hi], kra[lo:hi], krb[lo:hi], n, lo, sink)
            don = do_ref[0, n * Q:hi, :]
            dob = don.astype(BF16)
            pb = p.astype(BF16)
            dv_s[lo:hi, :] += _dg(pb, dob, 0, 0)
            dp = _dg(dob, vb[lo:hi], 1, 1)
            delta = jnp.sum(p * dp, axis=-1, keepdims=True)
            dsink = dsink - jnp.sum(ps * delta, axis=0, keepdims=True)
            ds = (p * (dp - delta) * scale).astype(BF16)
            dqra_s[n * Q:hi, :] = _dg(ds, kra[lo:hi], 1, 0)
            dqrb_s[n * Q:hi, :] = _dg(ds, krb[lo:hi], 1, 0)
            dkra_s[lo:hi, :] += _dg(ds, qra[n * Q:hi], 0, 0)
            dkrb_s[lo:hi, :] += _dg(ds, qrb[n * Q:hi], 0, 0)

        def unprep(dra, drb, r, ha, hb, wa, wb):
            dna = dra * c + drb * s
            dnb = drb * c - dra * s
            dwa = jnp.sum(dna * ha, axis=0, keepdims=True)
            dwb = jnp.sum(dnb * hb, axis=0, keepdims=True)
            ga = dna * wa
            gb = dnb * wb
            mu = (jnp.sum(ga * ha, axis=-1, keepdims=True) + jnp.sum(gb * hb, axis=-1, keepdims=True)) * (1.0 / HD)
            return r * (ga - ha * mu), r * (gb - hb * mu), dwa, dwb

        dqa, dqb, dqwa, dqwb = unprep(dqra_s[...], dqrb_s[...], qr, qha, qhb, qwa, qwb)
        dka, dkb, dkwa, dkwb = unprep(dkra_s[...], dkrb_s[...], kr, kha, khb, kwa, kwb)
        dqa_ref[0] = dqa
        dqb_ref[0] = dqb
        dsk_ref[0] = jnp.broadcast_to(dsink, (1, 128))

        @pl.when(j == 0)
        def _():
            dqwa_ref[...] = jnp.zeros_like(dqwa_ref)
            dqwb_ref[...] = jnp.zeros_like(dqwb_ref)
            dkwa_ref[...] = jnp.zeros_like(dkwa_ref)
            dkwb_ref[...] = jnp.zeros_like(dkwb_ref)
        dqwa_ref[...] += dqwa
        dqwb_ref[...] += dqwb
        dkwa_ref[...] += dkwa
        dkwb_ref[...] += dkwb

        @pl.when(j % 4 == 0)
        def _():
            dka_ref[0] = dka
            dkb_ref[0] = dkb
            dv_ref[0] = dv_s[...]

        @pl.when(j % 4 != 0)
        def _():
            dka_ref[0] += dka
            dkb_ref[0] += dkb
            dv_ref[0] += dv_s[...]

    qh, kh, vv, tab, wv, sk = _attn_specs()
    qo = pl.BlockSpec((1, L, HD), lambda j: (j, 0, 0))
    hvec = jax.ShapeDtypeStruct((1, HH), F32)
    return pl.pallas_call(
        body, name=name, grid=(NQH,),
        in_specs=[qh, qh, kh, kh, vv, tab, tab, wv, wv, wv, wv, sk, qo],
        out_specs=[qh, qh, kh, kh, vv, wv, wv, wv, wv, sk],
        out_shape=[jax.ShapeDtypeStruct((NQH, L, HH), F32), jax.ShapeDtypeStruct((NQH, L, HH), F32),
                   jax.ShapeDtypeStruct((NKV, L, HH), F32), jax.ShapeDtypeStruct((NKV, L, HH), F32),
                   jax.ShapeDtypeStruct((NKV, L, HD), F32), hvec, hvec, hvec, hvec,
                   jax.ShapeDtypeStruct((NQH, 1, 128), F32)],
        scratch_shapes=[pltpu.VMEM((L, HH), F32), pltpu.VMEM((L, HH), F32), pltpu.VMEM((L, HH), F32),
                        pltpu.VMEM((L, HH), F32), pltpu.VMEM((L, HD), F32)],
        compiler_params=_cp(("arbitrary",)))(qa, qb, ka, kb, v, cos, sin, qwa, qwb, kwa, kwb, sinks, do)


def _ssd_consts():
    hh = jnp.arange(128)[:, None]
    e = (hh == (jnp.arange(SSD_W)[None, :] // HP)).astype(BF16)
    e2 = (hh == (jnp.arange(NH * 128)[None, :] // 128)).astype(BF16)
    et = e.T
    tril = (jnp.arange(Q)[:, None] >= jnp.arange(Q)[None, :]).astype(BF16)
    triu = tril.T
    eye = jnp.eye(128, dtype=BF16)
    return e, e2, et, tril, triu, eye


def _ssd_common(x_ref, ext_scr, cw_ref, cb_ref, dt_ref, dtb_ref, alog_ref, e_ref, e2_ref, tril_ref, triu_ref,
                arow_scr, acol_scr, eax_scr):
    conv = jnp.broadcast_to(cb_ref[...], (Q, XBC))
    for k in range(4):
        conv = conv + cw_ref[k:k + 1, :] * ext_scr[pl.ds(5 + k, Q), :]
    sg = _sigmoid(conv)
    xbc = conv * sg
    dtpre = dt_ref[...] + dtb_ref[...]
    dt = _softplus(dtpre)
    a = -jnp.exp(alog_ref[...])
    adt = dt * a
    acol = _xdot_r(tril_ref[...], adt)
    acol_scr[...] = acol
    arow_scr[...] = _xdot_l(adt, triu_ref[...], 0, 0)
    alast = acol_scr[Q - 1:Q, :]
    ea = jnp.exp(acol)
    decs = jnp.exp(alast - acol)
    e = e_ref[...]
    dt_x = _xdot_l(dt, e)
    eax_scr[...] = _xdot_l(ea, e)
    decs_x = _xdot_l(decs, e)
    acx2 = _xdot_l(acol, e2_ref[...])
    return conv, sg, xbc, dtpre, dt, a, adt, acol, alast, ea, decs, dt_x, decs_x, acx2


def _ssd_fwd(u, cw, cb, dtb, alog, dxp, nw, consts, name):
    e, e2, et, tril, triu, eye = consts

    def body(z0_ref, z1_ref, x_ref, dt_ref, cw_ref, cb_ref, dtb_ref, alog_ref, dx_ref, nw_ref, e_ref, e2_ref,
             tril_ref, triu_ref, ya_ref, ypre_ref, st_ref, s_scr, ext_scr, arow_scr, acol_scr, eax_scr):
        c = pl.program_id(0)

        @pl.when(c == 0)
        def _():
            s_scr[...] = jnp.zeros_like(s_scr)
            ext_scr[0:8, :] = jnp.zeros((8, XBC), F32)
        ext_scr[8:8 + Q, :] = x_ref[...]
        (conv, sg, xbc, dtpre, dt, a, adt, acol, alast, ea, decs, dt_x, decs_x, acx2) = _ssd_common(
            x_ref, ext_scr, cw_ref, cb_ref, dt_ref, dtb_ref, alog_ref, e_ref, e2_ref, tril_ref, triu_ref,
            arow_scr, acol_scr, eax_scr)
        ext_scr[0:8, :] = ext_scr[Q:Q + 8, :]
        xs = xbc[:, :SSD_W]
        xdt = xs * dt_x
        lane = lax.broadcasted_iota(jnp.int32, (Q, 128), 1)
        causal = lax.broadcasted_iota(jnp.int32, (Q, Q), 0) >= lax.broadcasted_iota(jnp.int32, (Q, Q), 1)
        for g in range(2):
            bg = xbc[:, SSD_W + g * NS:SSD_W + (g + 1) * NS].astype(BF16)
            cg = xbc[:, SSD_W + 2 * NS + g * NS:SSD_W + 2 * NS + (g + 1) * NS].astype(BF16)
            cbm = _dg(cg, bg, 1, 1)
            sgv = s_scr[g]
            st_ref[0, g] = sgv
            gc = slice(g * 512, (g + 1) * 512)
            yoff = _dg(cg, sgv.astype(BF16), 1, 0) * eax_scr[:, gc]
            for pr in range(4):
                h0 = g * 8 + 2 * pr
                h1 = h0 + 1
                c0 = g * 512 + pr * 128
                xp = xdt[:, c0:c0 + 128].astype(BF16)
                w0 = (cbm * jnp.exp(jnp.where(causal, acx2[:, h0 * 128:(h0 + 1) * 128] - arow_scr[h0:h0 + 1, :],
                                              NEG))).astype(BF16)
                w1 = (cbm * jnp.exp(jnp.where(causal, acx2[:, h1 * 128:(h1 + 1) * 128] - arow_scr[h1:h1 + 1, :],
                                              NEG))).astype(BF16)
                yd = jnp.where(lane < HP, _dg(w0, xp, 1, 0), _dg(w1, xp, 1, 0))
                ypre_ref[:, c0:c0 + 128] = (yd + yoff[:, pr * 128:(pr + 1) * 128]
                                            + xs[:, c0:c0 + 128] * dx_ref[:, c0:c0 + 128])
            contrib = _dg(bg, (xdt[:, gc] * decs_x[:, gc]).astype(BF16), 0, 0)
            s_scr[g] = sgv * eax_scr[Q - 1:Q, gc] + contrib
        for g, zr in enumerate((z0_ref, z1_ref)):
            gc = slice(g * 512, (g + 1) * 512)
            zz = zr[...]
            ggg = ypre_ref[:, gc] * (zz * _sigmoid(zz))
            rstd = lax.rsqrt(jnp.mean(ggg * ggg, axis=-1, keepdims=True) + RMS_EPS)
            ya_ref[:, gc] = (ggg * rstd * nw_ref[:, gc]).astype(BF16)

    def row(w, blk=0):
        return pl.BlockSpec((Q, w), lambda c: (c, blk))

    def full(shape):
        return pl.BlockSpec(shape, lambda c: (0,) * len(shape))

    return pl.pallas_call(
        body, name=name, grid=(NC,),
        in_specs=[row(512, U_Z // 512), row(512, U_Z // 512 + 1), row(XBC, U_X // XBC), row(128, U_DT // 128),
                  full((4, XBC)), full((1, XBC)), full((1, 128)), full((1, 128)),
                  full((1, SSD_W)), full((1, SSD_W)), full((128, SSD_W)), full((128, NH * 128)), full((Q, Q)),
                  full((Q, Q))],
        out_specs=[row(SSD_W), row(SSD_W), pl.BlockSpec((1, 2, NS, 512), lambda c: (c, 0, 0, 0))],
        out_shape=[jax.ShapeDtypeStruct((L, SSD_W), BF16), jax.ShapeDtypeStruct((L, SSD_W), F32),
                   jax.ShapeDtypeStruct((NC, 2, NS, 512), F32)],
        scratch_shapes=[pltpu.VMEM((2, NS, 512), F32), pltpu.VMEM((Q + 8, XBC), F32), pltpu.VMEM((128, Q), F32),
                        pltpu.VMEM((Q, 128), F32), pltpu.VMEM((Q, SSD_W), F32)],
        compiler_params=_cp(("arbitrary",)))(u, u, u, u, cw, cb, dtb, alog, dxp, nw, e, e2, tril, triu)


def _ssd_bwd(u, ypre, st, dy, cw, cb, dtb, alog, dxp, nw, consts, name):
    e, e2, et, tril, triu, eye = consts

    def body(z0_ref, z1_ref, x_ref, xp_ref, dt_ref, ypre_ref, st_ref, dya_ref, cw_ref, cb_ref, dtb_ref, alog_ref, dx_ref,
             nw_ref, e_ref, e2_ref, et_ref, tril_ref, triu_ref, eye_ref,
             dz_ref, dxr_ref, ddtr_ref, dcw_ref, dcb_ref, ddtb_ref, dalog_ref, dd_ref, dnw_ref,
             g_scr, ext_scr, ext2_scr, arow_scr, acol_scr, eax_scr, darow_scr, dxdt_scr, t1_scr, t2_scr, dgg_scr):
        i = pl.program_id(0)

        @pl.when(i == 0)
        def _():
            g_scr[...] = jnp.zeros_like(g_scr)
            ext2_scr[Q:Q + 8, :] = jnp.zeros((8, XBC), F32)
            for r in (dcw_ref, dcb_ref, ddtb_ref, dalog_ref, dd_ref, dnw_ref):
                r[...] = jnp.zeros_like(r)
        not_first = jnp.where(i < NC - 1, 1.0, 0.0)
        ext_scr[0:8, :] = xp_ref[Q - 8:Q, :] * not_first
        ext_scr[8:8 + Q, :] = x_ref[...]
        (conv, sg, xbc, dtpre, dt, a, adt, acol, alast, ea, decs, dt_x, decs_x, acx2) = _ssd_common(
            x_ref, ext_scr, cw_ref, cb_ref, dt_ref, dtb_ref, alog_ref, e_ref, e2_ref, tril_ref, triu_ref,
            arow_scr, acol_scr, eax_scr)
        et_m = et_ref[...]
        xs = xbc[:, :SSD_W]
        xdt = xs * dt_x
        y = ypre_ref[...]
        zz = jnp.concatenate([z0_ref[...], z1_ref[...]], axis=1)
        sz = _sigmoid(zz)
        silu_z = zz * sz
        gg = y * silu_z
        dya = dya_ref[...]
        for g in range(2):
            gc = slice(g * 512, (g + 1) * 512)
            ggg = gg[:, gc]
            rstd = lax.rsqrt(jnp.mean(ggg * ggg, axis=-1, keepdims=True) + RMS_EPS)
            n = ggg * rstd
            dyag = dya[:, gc]
            dnw_ref[:, gc] += jnp.sum(dyag * n, axis=0, keepdims=True)
            dn = dyag * nw_ref[:, gc]
            dgg_scr[:, gc] = rstd * (dn - n * jnp.mean(dn * n, axis=-1, keepdims=True))
        dgg = dgg_scr[...]
        dy = dgg * silu_z
        dz_ref[...] = (dgg * y * (sz * (1.0 + zz * (1.0 - sz)))).astype(BF16)
        dd_ref[...] += _rowdot(jnp.sum(dy * xs, axis=0, keepdims=True), et_m)
        dxs = dy * dx_ref[...]
        dys = dy * eax_scr[...]
        lane = lax.broadcasted_iota(jnp.int32, (Q, 128), 1)
        causal = lax.broadcasted_iota(jnp.int32, (Q, Q), 0) >= lax.broadcasted_iota(jnp.int32, (Q, Q), 1)
        darow_scr[...] = jnp.zeros_like(darow_scr)
        dacol = jnp.zeros((Q, 128), F32)
        dcdx = []
        dbs = []
        dcs = []
        for g in range(2):
            gc = slice(g * 512, (g + 1) * 512)
            bg = xbc[:, SSD_W + g * NS:SSD_W + (g + 1) * NS].astype(BF16)
            cg = xbc[:, SSD_W + 2 * NS + g * NS:SSD_W + 2 * NS + (g + 1) * NS].astype(BF16)
            cbm = _dg(cg, bg, 1, 1)
            sgv = st_ref[0, g]
            sgb = sgv.astype(BF16)
            gv = g_scr[g]
            gvb = gv.astype(BF16)
            yoff = _dg(cg, sgb, 1, 0) * eax_scr[:, gc]
            dysg = dys[:, gc].astype(BF16)
            dcg = _dg(dysg, sgb, 1, 1)
            ds_off = _dg(cg, dysg, 0, 0)
            t1_scr[:, gc] = dy[:, gc] * yoff
            xdec = xdt[:, gc] * decs_x[:, gc]
            dxd = _dg(bg, gvb, 1, 0)
            dbg = _dg(xdec.astype(BF16), gvb, 1, 1)
            dxdt_g = dxd * decs_x[:, gc]
            t2_scr[:, gc] = dxd * xdt[:, gc]
            cdx = eax_scr[Q - 1:Q, gc]
            dcdx.append(jnp.sum(gv * sgv, axis=0, keepdims=True))
            g_scr[g] = gv * cdx + ds_off
            dcb_acc = jnp.zeros((Q, Q), F32)
            for pr in range(4):
                c0 = g * 512 + pr * 128
                xp = xdt[:, c0:c0 + 128].astype(BF16)
                dyp = dy[:, c0:c0 + 128]
                dypb = dyp.astype(BF16)
                halves = []
                for hh, keep in ((g * 8 + 2 * pr, lane < HP), (g * 8 + 2 * pr + 1, lane >= HP)):
                    lam = jnp.exp(jnp.where(causal, acx2[:, hh * 128:(hh + 1) * 128] - arow_scr[hh:hh + 1, :], NEG))
                    w = cbm * lam
                    dw = _dg(jnp.where(keep, dyp, 0.0).astype(BF16), xp, 1, 1)
                    dcb_acc = dcb_acc + dw * lam
                    t = dw * w
                    dacol = dacol + jnp.sum(t, axis=-1, keepdims=True) * (lane == hh).astype(F32)
                    darow_scr[hh:hh + 1, :] -= jnp.sum(t, axis=0, keepdims=True)
                    halves.append(_dg(w.astype(BF16), dypb, 0, 0))
                dxdt_scr[:, c0:c0 + 128] = (jnp.where(lane < HP, halves[0], halves[1])
                                            + dxdt_g[:, pr * 128:(pr + 1) * 128])
            dcbb = dcb_acc.astype(BF16)
            dcs.append(dcg + _dg(dcbb, bg, 1, 0))
            dbs.append(dbg + _dg(dcbb, cg, 0, 0))
        dacol = dacol + _xdot_l(t1_scr[...], et_m)
        ddecs = _xdot_l(t2_scr[...], et_m) * decs
        dacol = dacol - ddecs
        dalast = jnp.sum(ddecs, axis=0, keepdims=True)
        dcd = _rowdot(jnp.concatenate(dcdx, axis=1), et_m)
        dalast = dalast + dcd * jnp.exp(alast)
        dacol = dacol + _xdot_l(darow_scr[...], eye_ref[...], 0, 0)
        rowi = lax.broadcasted_iota(jnp.int32, (Q, 128), 0)
        dacol = dacol + jnp.where(rowi == Q - 1, dalast, 0.0)
        dadt = _xdot_r(triu_ref[...], dacol)
        dxdt = dxdt_scr[...]
        ddt = dadt * a + _xdot_l(dxdt * xs, et_m)
        dalog_ref[...] += jnp.sum(dadt * dt, axis=0, keepdims=True) * a
        dxs = dxs + dxdt * dt_x
        ddtr = ddt * _sigmoid(dtpre)
        ddtb_ref[...] += jnp.sum(ddtr, axis=0, keepdims=True)
        ddtr_ref[...] = ddtr.astype(BF16)
        dsilu = sg * (1.0 + conv * (1.0 - sg))
        ext2_scr[0:Q, 0:SSD_W] = dxs * dsilu[:, :SSD_W]
        for g in range(2):
            o1 = SSD_W + g * NS
            o2 = SSD_W + 2 * NS + g * NS
            ext2_scr[0:Q, o1:o1 + NS] = dbs[g] * dsilu[:, o1:o1 + NS]
            ext2_scr[0:Q, o2:o2 + NS] = dcs[g] * dsilu[:, o2:o2 + NS]
        dconv = ext2_scr[0:Q, :]
        dcb_ref[...] += jnp.sum(dconv, axis=0, keepdims=True)
        dxr = jnp.zeros((Q, XBC), F32)
        for k in range(4):
            dcw_ref[k:k + 1, :] += jnp.sum(dconv * ext_scr[pl.ds(5 + k, Q), :], axis=0, keepdims=True)
            dxr = dxr + cw_ref[k:k + 1, :] * ext2_scr[pl.ds(3 - k, Q), :]
        dxr_ref[...] = dxr.astype(BF16)
        ext2_scr[Q:Q + 8, :] = ext2_scr[0:8, :]

    def row(w, blk=0):
        return pl.BlockSpec((Q, w), lambda i: (NC - 1 - i, blk))

    def full(shape):
        return pl.BlockSpec(shape, lambda i: (0,) * len(shape))

    prev = pl.BlockSpec((Q, XBC), lambda i: (jnp.maximum(NC - 2 - i, 0), U_X // XBC))
    return pl.pallas_call(
        body, name=name, grid=(NC,),
        in_specs=[row(512, U_Z // 512), row(512, U_Z // 512 + 1), row(XBC, U_X // XBC), prev, row(128, U_DT // 128),
                  row(SSD_W),
                  pl.BlockSpec((1, 2, NS, 512), lambda i: (NC - 1 - i, 0, 0, 0)), row(SSD_W),
                  full((4, XBC)), full((1, XBC)), full((1, 128)), full((1, 128)), full((1, SSD_W)),
                  full((1, SSD_W)), full((128, SSD_W)), full((128, NH * 128)), full((SSD_W, 128)), full((Q, Q)),
                  full((Q, Q)), full((128, 128))],
        out_specs=[row(SSD_W), row(XBC), row(128), full((8, XBC)), full((1, XBC)), full((1, 128)), full((1, 128)),
                   full((1, 128)), full((1, SSD_W))],
        out_shape=[jax.ShapeDtypeStruct((L, SSD_W), BF16), jax.ShapeDtypeStruct((L, XBC), BF16),
                   jax.ShapeDtypeStruct((L, 128), BF16), jax.ShapeDtypeStruct((8, XBC), F32),
                   jax.ShapeDtypeStruct((1, XBC), F32), jax.ShapeDtypeStruct((1, 128), F32),
                   jax.ShapeDtypeStruct((1, 128), F32), jax.ShapeDtypeStruct((1, 128), F32),
                   jax.ShapeDtypeStruct((1, SSD_W), F32)],
        scratch_shapes=[pltpu.VMEM((2, NS, 512), F32), pltpu.VMEM((Q + 8, XBC), F32), pltpu.VMEM((Q + 8, XBC), F32),
                        pltpu.VMEM((128, Q), F32), pltpu.VMEM((Q, 128), F32), pltpu.VMEM((Q, SSD_W), F32),
                        pltpu.VMEM((128, Q), F32), pltpu.VMEM((Q, SSD_W), F32), pltpu.VMEM((Q, SSD_W), F32),
                        pltpu.VMEM((Q, SSD_W), F32), pltpu.VMEM((Q, SSD_W), F32)],
        compiler_params=_cp(("arbitrary",)))(u, u, u, u, u, ypre, st, dy, cw, cb, dtb, alog, dxp, nw,
                                             e, e2, et, tril, triu, eye)


def _my_pos():
    return lax.axis_index("x"), lax.axis_index("y"), lax.axis_index("c")


CHIP_REL = ((1, 0), (0, 1), (1, 1))
CHIP_XOR = (2, 1, 3)
NW = 4
AT = 256


def _chips(x, y):
    return [(1 - x if dx else x, 1 - y if dy else y) for dx, dy in CHIP_REL]


HBM_SPEC = pl.BlockSpec(memory_space=pltpu.HBM)
SEM_SPEC = pl.BlockSpec(memory_space=pltpu.SEMAPHORE)
EFFECT = pltpu.SideEffectType.DATAFLOW_SIDE_EFFECTING


def _hbm(t):
    return pltpu.with_memory_space_constraint(t, pltpu.HBM)


def _split_start(srcs, lands, after, copies, name):
    n = len(srcs)

    def body(*refs):
        src_refs, land_refs = refs[:n], refs[n:2 * n]
        send_sems, recv_sems = refs[2 * n + 1], refs[2 * n + 2]
        token = refs[-1]
        for w, k, src, dst, dev in copies(src_refs, land_refs):
            pltpu.make_async_remote_copy(src_ref=src, dst_ref=dst, send_sem=send_sems.at[3 * w + k],
                                         recv_sem=recv_sems.at[3 * w + k], device_id=dev, device_id_type=MESH).start()
        token[...] = jnp.zeros_like(token)

    outs = pl.pallas_call(
        body, name=name,
        out_shape=(pltpu.SemaphoreType.DMA((3 * n,)), pltpu.SemaphoreType.DMA((3 * n,)),
                   *[pltpu.HBM(t.shape, t.dtype) for t in srcs], *[pltpu.HBM(t.shape, t.dtype) for t in lands],
                   jax.ShapeDtypeStruct((8, 128), F32)),
        in_specs=[HBM_SPEC] * (2 * n) + [ANY],
        out_specs=(SEM_SPEC, SEM_SPEC, *([HBM_SPEC] * (2 * n)), VMEM_SPEC),
        input_output_aliases={i: 2 + i for i in range(2 * n)},
        compiler_params=pltpu.CompilerParams(has_side_effects=EFFECT))(
            *[_hbm(t) for t in srcs], *[_hbm(t) for t in lands], after)
    return outs[0], outs[1], list(outs[2:2 + n]), list(outs[2 + n:2 + 2 * n]), outs[-1]


def _split_wait(send_sems, recv_sems, srcs, lands, after, copies, name):
    n = len(srcs)

    def body(*refs):
        src_refs, land_refs = refs[:n], refs[n:2 * n]
        ssem, rsem = refs[2 * n], refs[2 * n + 1]
        for w, k, src, dst, dev in copies(src_refs, land_refs):
            cp = pltpu.make_async_remote_copy(src_ref=src, dst_ref=dst, send_sem=ssem.at[3 * w + k],
                                              recv_sem=rsem.at[3 * w + k], device_id=dev, device_id_type=MESH)
            cp.wait_send()
            cp.wait_recv()

    outs = pl.pallas_call(
        body, name=name,
        out_shape=tuple([pltpu.HBM(t.shape, t.dtype) for t in srcs] + [pltpu.HBM(t.shape, t.dtype) for t in lands]),
        in_specs=[HBM_SPEC] * (2 * n) + [SEM_SPEC, SEM_SPEC, ANY],
        out_specs=tuple([HBM_SPEC] * (2 * n)),
        input_output_aliases={i: i for i in range(2 * n)},
        compiler_params=pltpu.CompilerParams(has_side_effects=EFFECT))(*srcs, *lands, send_sems, recv_sems, after)
    return list(outs[:n]), list(outs[n:])


def _ag_copies(arrival):
    def copies(src_refs, land_refs):
        x, y, c = _my_pos()
        s = 2 * x + y
        chips = _chips(x, y)
        for w in range(NW):
            hr = src_refs[w].shape[0] // 2
            mine = pl.ds(c * hr, hr)
            for k in range(3):
                slot = s ^ CHIP_XOR[k] if arrival else s
                yield w, k, src_refs[w].at[mine], land_refs[w].at[slot, mine], (*chips[k], c)
    return copies


def _ag_forward(lands, name):
    def body(*refs):
        outs = refs[NW:2 * NW]
        send_sems, recv_sems = refs[2 * NW:]
        x, y, c = _my_pos()
        s = 2 * x + y
        sib = (x, y, 1 - c)
        sends = []
        for w in range(NW):
            hr = outs[w].shape[1] // 2
            for k in range(3):
                blk = outs[w].at[s ^ CHIP_XOR[k], pl.ds(c * hr, hr)]
                fw = pltpu.make_async_remote_copy(
                    src_ref=blk, dst_ref=blk, send_sem=send_sems.at[w, k], recv_sem=recv_sems.at[w, k],
                    device_id=sib, device_id_type=MESH)
                fw.start()
                sends.append(fw)
        for w in range(NW):
            hr = outs[w].shape[1] // 2
            for k in range(3):
                blk = outs[w].at[s ^ CHIP_XOR[k], pl.ds((1 - c) * hr, hr)]
                pltpu.make_async_remote_copy(
                    src_ref=blk, dst_ref=blk, send_sem=send_sems.at[w, k], recv_sem=recv_sems.at[w, k],
                    device_id=sib, device_id_type=MESH).wait_recv()
        for cp in sends:
            cp.wait_send()

    return pl.pallas_call(
        body, name=name, in_specs=[ANY] * NW, out_specs=[ANY] * NW,
        out_shape=[jax.ShapeDtypeStruct(t.shape, t.dtype) for t in lands],
        input_output_aliases={w: w for w in range(NW)},
        scratch_shapes=[pltpu.SemaphoreType.DMA((NW, 3)), pltpu.SemaphoreType.DMA((NW, 3))])(*lands)


def _rs_copies(src_refs, land_refs):
    x, y, c = _my_pos()
    s = 2 * x + y
    chips = _chips(x, y)
    for w in range(NW):
        for k in range(3):
            yield w, k, src_refs[w].at[s ^ CHIP_XOR[k]], land_refs[w].at[k], (*chips[k], c)


def _place_own(shard, gathered, sidx, name):
    r, cc = shard.shape

    def body(s_ref, a_ref, g_ref, o_ref):
        o_ref[0] = a_ref[...]

    return pl.pallas_call(
        body, name=name,
        grid_spec=pltpu.PrefetchScalarGridSpec(
            num_scalar_prefetch=1, grid=(r // AT,),
            in_specs=[pl.BlockSpec((AT, cc), lambda i, s_ref: (i, 0)), ANY],
            out_specs=pl.BlockSpec((1, AT, cc), lambda i, s_ref: (s_ref[0], i, 0))),
        out_shape=jax.ShapeDtypeStruct(gathered.shape, gathered.dtype),
        input_output_aliases={2: 0}, compiler_params=_cp(("parallel",)))(sidx, shard, gathered)


def _rs_pair(dwb, name):
    def body(*refs):
        ins, outs = refs[:NW], refs[NW:2 * NW]
        send_sems, recv_sems = refs[2 * NW:]
        x, y, c = _my_pos()
        cps = []
        for w in range(NW):
            hr = ins[w].shape[1] // 2
            cp = pltpu.make_async_remote_copy(
                src_ref=ins[w].at[:, pl.ds((1 - c) * hr, hr)], dst_ref=outs[w], send_sem=send_sems.at[w],
                recv_sem=recv_sems.at[w], device_id=(x, y, 1 - c), device_id_type=MESH)
            cp.start()
            cps.append(cp)
        for cp in cps:
            cp.wait()

    return pl.pallas_call(
        body, name=name, in_specs=[ANY] * NW, out_specs=[ANY] * NW,
        out_shape=[jax.ShapeDtypeStruct((4, t.shape[1] // 2, t.shape[2]), t.dtype) for t in dwb],
        scratch_shapes=[pltpu.SemaphoreType.DMA((NW,)), pltpu.SemaphoreType.DMA((NW,))])(*dwb)


def _rs_sib(q, name):
    def body(*refs):
        ins, outs = refs[:NW], refs[NW:2 * NW]
        send_sems, recv_sems = refs[2 * NW:]
        x, y, c = _my_pos()
        cps = []
        for w in range(NW):
            hr = outs[w].shape[0] // 2
            mine = pl.ds(c * hr, hr)
            cp = pltpu.make_async_remote_copy(
                src_ref=outs[w].at[mine], dst_ref=outs[w].at[mine], send_sem=send_sems.at[w],
                recv_sem=recv_sems.at[w], device_id=(x, y, 1 - c), device_id_type=MESH)
            cp.start()
            cps.append(cp)
        for w in range(NW):
            hr = outs[w].shape[0] // 2
            other = outs[w].at[pl.ds((1 - c) * hr, hr)]
            pltpu.make_async_remote_copy(
                src_ref=other, dst_ref=other, send_sem=send_sems.at[w], recv_sem=recv_sems.at[w],
                device_id=(x, y, 1 - c), device_id_type=MESH).wait_recv()
        for cp in cps:
            cp.wait_send()

    return pl.pallas_call(
        body, name=name, in_specs=[ANY] * NW, out_specs=[ANY] * NW,
        out_shape=[jax.ShapeDtypeStruct(t.shape, t.dtype) for t in q],
        input_output_aliases={w: w for w in range(NW)},
        scratch_shapes=[pltpu.SemaphoreType.DMA((NW,)), pltpu.SemaphoreType.DMA((NW,))])(*q)


def _rs_add2(dw, got, cidx, name):
    _, r, cc = dw.shape
    hr = r // 2
    nb = hr // AT

    def body(c_ref, a_ref, b_ref, o_ref, ob_ref):
        acc = a_ref[...] + b_ref[...].astype(F32)
        o_ref[...] = acc
        ob_ref[...] = acc.astype(BF16)

    blk = pl.BlockSpec((1, AT, cc), lambda sh, i, c_ref: (sh, i, 0))
    return pl.pallas_call(
        body, name=name,
        grid_spec=pltpu.PrefetchScalarGridSpec(
            num_scalar_prefetch=1, grid=(4, nb),
            in_specs=[pl.BlockSpec((1, AT, cc), lambda sh, i, c_ref: (sh, c_ref[0] * nb + i, 0)), blk],
            out_specs=[blk, blk]),
        out_shape=[jax.ShapeDtypeStruct((4, hr, cc), F32), jax.ShapeDtypeStruct((4, hr, cc), BF16)],
        compiler_params=_cp(("parallel", "parallel")))(cidx, dw, got)


def _rs_add4(p, got, scidx, name):
    _, hr, cc = p.shape
    nb = hr // AT

    def body(s_ref, p_ref, g0_ref, g1_ref, g2_ref, o_ref):
        acc = p_ref[0] + g0_ref[0].astype(F32)
        acc = acc + g1_ref[0].astype(F32)
        o_ref[...] = acc + g2_ref[0].astype(F32)

    def gk(k):
        return pl.BlockSpec((1, AT, cc), lambda i, s_ref: (k, i, 0))

    return pl.pallas_call(
        body, name=name,
        grid_spec=pltpu.PrefetchScalarGridSpec(
            num_scalar_prefetch=1, grid=(nb,),
            in_specs=[pl.BlockSpec((1, AT, cc), lambda i, s_ref: (s_ref[0], i, 0)), gk(0), gk(1), gk(2)],
            out_specs=pl.BlockSpec((AT, cc), lambda i, s_ref: (s_ref[1] * nb + i, 0))),
        out_shape=jax.ShapeDtypeStruct((2 * hr, cc), F32),
        compiler_params=_cp(("parallel",)))(scidx, p, got, got, got)


def _rs_begin(dws, dwbs, after):
    _, _, c = _my_pos()
    cidx = jnp.reshape(c, (1,)).astype(jnp.int32)
    got = _rs_pair(dwbs, "rs_pair")
    pairs = [_rs_add2(dws[w], got[w], cidx, "rs_add2_%d" % w) for w in range(NW)]
    pb = [p[1] for p in pairs]
    lands = [lax.empty((3,) + t.shape[1:], BF16) for t in pb]
    ssem, rsem, pb, lands, token = _split_start(pb, lands, after, _rs_copies, "rs_chip_start")
    return ([p[0] for p in pairs], ssem, rsem, pb, lands), token


def _rs_end(state, after):
    x, y, c = _my_pos()
    scidx = jnp.stack([2 * x + y, c]).astype(jnp.int32)
    p, ssem, rsem, pb, lands = state
    _, recv = _split_wait(ssem, rsem, pb, lands, after, _rs_copies, "rs_chip_wait")
    q = [_rs_add4(p[w], recv[w], scidx, "rs_add4_%d" % w) for w in range(NW)]
    return _rs_sib(q, "rs_sib")


def _allreduce_small(buf, name):
    rows = buf.shape[0]

    def body(src_ref, out_ref, gat_ref, send_sems, recv_sems):
        x, y, c = _my_pos()
        me = 4 * x + 2 * y + c
        gat_ref[me] = src_ref[...]
        cps = []
        for r in range(1, N_DEV):
            tx = 1 - x if (r >> 2) & 1 else x
            ty = 1 - y if (r >> 1) & 1 else y
            tc = 1 - c if r & 1 else c
            cps.append(pltpu.make_async_remote_copy(
                src_ref=src_ref, dst_ref=gat_ref.at[me], send_sem=send_sems.at[r - 1], recv_sem=recv_sems.at[r - 1],
                device_id=(tx, ty, tc), device_id_type=MESH))
        for cp in cps:
            cp.start()
        for cp in cps:
            cp.wait()
        acc = gat_ref[0]
        for k in range(1, N_DEV):
            acc = acc + gat_ref[k]
        out_ref[...] = acc

    return pl.pallas_call(
        body, name=name, in_specs=[VMEM_SPEC], out_specs=VMEM_SPEC, out_shape=jax.ShapeDtypeStruct((rows, 128), F32),
        scratch_shapes=[pltpu.VMEM((N_DEV, rows, 128), F32), pltpu.SemaphoreType.DMA((N_DEV - 1,)),
                        pltpu.SemaphoreType.DMA((N_DEV - 1,))],
        compiler_params=_cp())(buf)


SMALL = (("norm_mix_w", (D,)), ("ssd_conv_w", (4, XBC)), ("ssd_conv_b", (XBC,)), ("ssd_dt_bias", (NH,)),
         ("ssd_a_log", (NH,)), ("ssd_d", (NH,)), ("ssd_norm_w", (SSD_W,)), ("q_norm_w", (HD,)),
         ("k_norm_w", (HD,)), ("attn_sinks", (NQH,)), ("cm_dw_w", (CMK, CMC)), ("cm_dw_b", (CMC,)),
         ("cm_ln_w", (CMC,)), ("cm_ln_b", (CMC,)), ("norm_mlp_w", (D,)))
SHARDED_SMALL = ("ssd_conv_w", "cm_dw_w")


def _seg_len(shape):
    n = 1
    for d in shape:
        n *= d
    return -(-n // 128) * 128


def _pack_small(vals, names):
    parts = []
    for name, shape in SMALL:
        if name not in names:
            continue
        v = vals[name].reshape(DEPTH, -1)
        pad = _seg_len(shape) - v.shape[1]
        parts.append(jnp.pad(v, ((0, 0), (0, pad))))
    flat = jnp.concatenate(parts, axis=1)
    return flat.reshape(-1, 128)


def _unpack_small(buf, names):
    flat = buf.reshape(DEPTH, -1)
    out = {}
    off = 0
    for name, shape in SMALL:
        if name not in names:
            continue
        n = 1
        for d in shape:
            n *= d
        out[name] = flat[:, off:off + n].reshape((DEPTH,) + shape)
        off += _seg_len(shape)
    return out


SW = N_IN // 4
SWP = 1152
ORIG = (("z", 0, 1024), ("x", 1024, 2560), ("dt", 2560, 2576), ("q", 2576, 3088), ("k", 3088, 3216),
        ("v", 3216, 3344), ("a", 3344, 3856), ("g", 3856, 4368))


def _orig_cols(g_in, lo, hi):
    out = []
    for s in range(4):
        a, b = max(lo, s * SW), min(hi, (s + 1) * SW)
        if a < b:
            out.append(g_in[s][:, a - s * SW:b - s * SW])
    return out


def _shard_major(parts):
    cols = []
    for s in range(4):
        for name, g0, g1 in ORIG:
            a, b = max(g0, s * SW), min(g1, (s + 1) * SW)
            if a < b:
                cols.append(parts[name][:, a - g0:b - g0])
        cols.append(jnp.zeros((L, SWP - SW), BF16))
    return jnp.concatenate(cols, axis=1)


def _rope_tables():
    inv = 10000.0 ** (-jnp.arange(0, HD, 2, dtype=F32) / HD)
    ang = jnp.arange(L, dtype=F32)[:, None] * inv[None, :]
    return jnp.cos(ang), jnp.sin(ang)


def _pad128(v):
    return jnp.pad(v, (0, 128 - v.shape[0]))[None, :]


def _heads(t, nh, w):
    return jnp.transpose(t.reshape(L, nh, w), (1, 0, 2))


def _unheads(t):
    nh, _, w = t.shape
    return jnp.transpose(t, (1, 0, 2)).reshape(L, nh * w)


def kernel(x, norm_mix_w, w_in, ssd_conv_w, ssd_conv_b, ssd_dt_bias, ssd_a_log, ssd_d, ssd_norm_w, q_norm_w, k_norm_w, attn_sinks, cm_dw_w, cm_dw_b, cm_ln_w, cm_ln_b, w_out, norm_mlp_w, w_mlp_up, w_mlp_down, loss_target, m_norm_mix_w, m_w_in, m_ssd_conv_w, m_ssd_conv_b, m_ssd_dt_bias, m_ssd_a_log, m_ssd_d, m_ssd_norm_w, m_q_norm_w, m_k_norm_w, m_attn_sinks, m_cm_dw_w, m_cm_dw_b, m_cm_ln_w, m_cm_ln_b, m_w_out, m_norm_mlp_w, m_w_mlp_up, m_w_mlp_down, v_norm_mix_w, v_w_in, v_ssd_conv_w, v_ssd_conv_b, v_ssd_dt_bias, v_ssd_a_log, v_ssd_d, v_ssd_norm_w, v_q_norm_w, v_k_norm_w, v_attn_sinks, v_cm_dw_w, v_cm_dw_b, v_cm_ln_w, v_cm_ln_b, v_w_out, v_norm_mlp_w, v_w_mlp_up, v_w_mlp_down):
    px, py, pc = _my_pos()
    shard = 2 * px + py
    sidx = jnp.reshape(shard, (1,)).astype(jnp.int32)
    consts = _ssd_consts()
    cos, sin = _rope_tables()

    wb = [w.astype(BF16) for w in (jnp.pad(w_in, ((0, 0), (0, 0), (0, SWP - SW))), w_out, w_mlp_up, w_mlp_down)]
    zc = jnp.zeros((DEPTH, 4, XBC), F32)
    zc = lax.dynamic_update_slice_in_dim(zc, ssd_conv_w, shard * (XBC // 4), axis=2)
    zd = jnp.zeros((DEPTH, CMK, CMC), F32)
    zd = lax.dynamic_update_slice_in_dim(zd, cm_dw_w, shard * (CMC // 4), axis=2)
    half = jnp.where(pc == 0, 1.0, 0.0).astype(F32)
    gw = _allreduce_small(_pack_small({"ssd_conv_w": zc * half, "cm_dw_w": zd * half}, SHARDED_SMALL), "ag_small")
    gw = _unpack_small(gw, SHARDED_SMALL)
    conv_w_full, dw_w_full = gw["ssd_conv_w"], gw["cm_dw_w"]

    def gather_start(l, after):
        own = [t[l] for t in wb]
        lands = [lax.empty((4,) + t.shape, BF16) for t in own]
        return _split_start(own, lands, after, _ag_copies(False), "ag_start")

    xcur = x[0]
    saved = []
    zero_tile = jnp.zeros((8, 128), F32)
    in_flight = gather_start(0, zero_tile)
    for l in range(DEPTH):
        ssem, rsem, own, lands, _ = in_flight
        own, lands = _split_wait(ssem, rsem, own, lands, xcur, _ag_copies(True), "ag_wait")
        lands = _ag_forward(lands, "ag_forward")
        g_in, g_out, g_up, g_dn = [_place_own(own[w], g, sidx, "ag_place_%d" % w) for w, g in enumerate(lands)]
        nmw = norm_mix_w[l][None]
        if l + 1 < DEPTH:
            in_flight = gather_start(l + 1, g_dn)
            nmw = nmw + in_flight[4][0:1, 0:1]
        grp = dict((n, (a, b)) for n, a, b in ORIG)
        w_perm = jnp.concatenate(
            _orig_cols(g_in, *grp["x"]) + _orig_cols(g_in, *grp["z"]) + _orig_cols(g_in, grp["a"][0], grp["g"][1])
            + _orig_cols(g_in, grp["q"][0], grp["v"][1]) + _orig_cols(g_in, *grp["dt"])
            + [jnp.zeros((D, 128 - NH), BF16)], axis=1)
        g_out = g_out.reshape(2 * D, D)
        g_dn = g_dn.reshape(DFF, D)
        h = _rms_fwd(xcur, nmw, "rms_mix_fwd")
        u = _mm(h, w_perm, "nn", "in_proj", tn=640)
        alog = _pad128(ssd_a_log[l])
        dtb = _pad128(ssd_dt_bias[l])
        dxp = jnp.repeat(ssd_d[l], HP)[None, :]
        ssd_p = (conv_w_full[l], ssd_conv_b[l][None], dtb, alog, dxp, ssd_norm_w[l][None])
        ya, ypre, st = _ssd_fwd(u, *ssd_p, consts, "ssd_fwd")
        q3 = _heads(u[:, U_Q:U_K], NQH, HD)
        k3 = _heads(u[:, U_K:U_V], NKV, HD)
        v3 = _heads(u[:, U_V:U_DT], NKV, HD)
        qa, qb, ka, kb = q3[..., :HH], q3[..., HH:], k3[..., :HH], k3[..., HH:]
        qw, kw = q_norm_w[l], k_norm_w[l]
        sinks = jnp.broadcast_to(attn_sinks[l][:, None, None], (NQH, 1, 128))
        attn_args = (qa, qb, ka, kb, v3, cos, sin, qw[None, :HH], qw[None, HH:], kw[None, :HH], kw[None, HH:], sinks)
        yb = _unheads(_attn_fwd(*attn_args, "attn_fwd")).astype(BF16)
        conf_p = (dw_w_full[l], cm_dw_b[l][None], cm_ln_w[l][None], cm_ln_b[l][None])
        yc = _conf_fwd(u, *conf_p, "conf_fwd")
        ycat = jnp.concatenate([ya, yb, yc], axis=1)
        x1 = _mm(ycat, g_out, "nn", "out_proj", add=xcur)
        hm = _rms_fwd(x1, norm_mlp_w[l][None], "rms_mlp_fwd")
        r_up = _mm_up(hm, g_up, "mlp_up")
        x2 = _mm(r_up, g_dn, "nn", "mlp_down", add=x1)
        saved.append(dict(x=xcur, h=h, u=u, ypre=ypre, st=st, attn_args=attn_args, conf_p=conf_p, ycat=ycat, x1=x1,
                          hm=hm, r_up=r_up, ssd_p=ssd_p, g_in=g_in, g_out=g_out, g_up=g_up, g_dn=g_dn))
        xcur = x2

    lsum, dx, dxb = _loss_bwd(xcur, loss_target[0], "loss")

    gbig = [None] * DEPTH
    pending = None
    gsm = {name: [] for name, _ in SMALL}
    for l in reversed(range(DEPTH)):
        sv = saved[l]
        da = _mm(dxb, sv["g_dn"], "nt", "mlp_down_dx", relu2_of=sv["r_up"])
        dwdn, dwdn_b = _mm_dw(sv["r_up"], dxb, "mlp_down_dw")
        dwup, dwup_b = _mm_dw(sv["hm"], da, "mlp_up_dw", col_shards=True)
        dhm = _mm_cs_nt(da, sv["g_up"], "mlp_up_dx")
        dx1, dx1b, dnw = _rms_bwd(sv["x1"], norm_mlp_w[l][None], dhm, dx, "rms_mlp_bwd")
        gsm["norm_mlp_w"].append(dnw[0])
        dy = _mm(dx1b, sv["g_out"], "nt", "out_proj_dx")
        dwout, dwout_b = _mm_dw(sv["ycat"], dx1b, "out_proj_dw")
        da_c, dg_c, dww, dwb, dlw, dlb = _conf_bwd(sv["u"], *sv["conf_p"], dy, "conf_bwd")
        gsm["cm_dw_w"].append(dww[:CMK])
        gsm["cm_dw_b"].append(dwb[0])
        gsm["cm_ln_w"].append(dlw[0])
        gsm["cm_ln_b"].append(dlb[0])
        do3 = _heads(dy[:, SSD_W:SSD_W + ATT_W], NQH, HD)
        (dqa, dqb, dka, dkb, dv3, dqwa, dqwb, dkwa, dkwb, dsk) = _attn_bwd(*sv["attn_args"], do3, "attn_bwd")
        dq = _unheads(jnp.concatenate([dqa, dqb], axis=-1))
        dk = _unheads(jnp.concatenate([dka, dkb], axis=-1))
        dv = _unheads(dv3)
        gsm["q_norm_w"].append(jnp.concatenate([dqwa[0], dqwb[0]]))
        gsm["k_norm_w"].append(jnp.concatenate([dkwa[0], dkwb[0]]))
        gsm["attn_sinks"].append(dsk[:, 0, 0])
        (dz, dxr, ddtr, dcw, dcb, ddtb, dalog, ddd, dnsw) = _ssd_bwd(
            sv["u"], sv["ypre"], sv["st"], dy, *sv["ssd_p"], consts, "ssd_bwd")
        gsm["ssd_conv_w"].append(dcw[:4])
        gsm["ssd_conv_b"].append(dcb[0])
        gsm["ssd_dt_bias"].append(ddtb[0, :NH])
        gsm["ssd_a_log"].append(dalog[0, :NH])
        gsm["ssd_d"].append(ddd[0, :NH])
        gsm["ssd_norm_w"].append(dnsw[0])
        du = _shard_major(dict(z=dz, x=dxr, dt=ddtr[:, :NH], q=dq.astype(BF16), k=dk.astype(BF16),
                               v=dv.astype(BF16), a=da_c, g=dg_c))
        dwin, dwin_b = _mm_dw(sv["h"], du, "in_dw", col_shards=True, tn=SWP // 3)
        state, token = _rs_begin(
            [dwin, dwout.reshape(4, D // 2, D), dwup, dwdn.reshape(4, D, D)],
            [dwin_b, dwout_b.reshape(4, D // 2, D), dwup_b, dwdn_b.reshape(4, D, D)], zero_tile)
        dh = _mm_cs_nt(du, sv["g_in"], "in_dx")
        dx, dxb, dnm = _rms_bwd(sv["x"], norm_mix_w[l][None] + token[0:1, 0:1], dh, dx1, "rms_mix_bwd")
        gsm["norm_mix_w"].append(dnm[0])
        if pending is not None:
            gbig[l + 1] = _rs_end(pending, dx)
        pending = state
    gbig[0] = _rs_end(pending, dx)

    gsm = {k: jnp.stack(v[::-1]) for k, v in gsm.items()}
    packed = _pack_small(gsm, [n for n, _ in SMALL])
    packed = jnp.concatenate([packed, lsum], axis=0)
    red = _allreduce_small(packed, "ar_small")
    loss = 0.5 * red[-8, 0] / D
    gsm = _unpack_small(red[:-8], [n for n, _ in SMALL])
    gsm["ssd_conv_w"] = lax.dynamic_slice_in_dim(gsm["ssd_conv_w"], shard * (XBC // 4), XBC // 4, axis=2)
    gsm["cm_dw_w"] = lax.dynamic_slice_in_dim(gsm["cm_dw_w"], shard * (CMC // 4), CMC // 4, axis=2)
    grads = dict(gsm)
    for w, n in enumerate(("w_in", "w_out", "w_mlp_up", "w_mlp_down")):
        grads[n] = jnp.stack([gbig[l][w] for l in range(DEPTH)])
    grads["w_in"] = grads["w_in"][:, :, :SW]

    loc = locals()
    names = ["norm_mix_w", "w_in", "ssd_conv_w", "ssd_conv_b", "ssd_dt_bias", "ssd_a_log", "ssd_d", "ssd_norm_w",
             "q_norm_w", "k_norm_w", "attn_sinks", "cm_dw_w", "cm_dw_b", "cm_ln_w", "cm_ln_b", "w_out", "norm_mlp_w",
             "w_mlp_up", "w_mlp_down"]
    weights = {n: loc[n] for n in names}
    moms = {n: loc["m_" + n] for n in names}
    vars_ = {n: loc["v_" + n] for n in names}
    delta, new_m, new_v = {}, {}, {}
    packed_names = [n for n, _ in SMALL if n not in SHARDED_SMALL]
    pw = _pack_small(weights, packed_names)
    pg = _pack_small(grads, packed_names)
    pm = _pack_small(moms, packed_names)
    pv = _pack_small(vars_, packed_names)
    pd, pmn, pvn = _adamw(pw, pg, pm, pv, "adamw_small")
    for dst, buf in ((delta, pd), (new_m, pmn), (new_v, pvn)):
        dst.update(_unpack_small(buf, packed_names))
    for n in ("w_in", "w_out", "w_mlp_up", "w_mlp_down", "ssd_conv_w", "cm_dw_w"):
        shp = weights[n].shape
        flat = lambda t: t.reshape(-1, shp[-1])
        d_, m_, v_ = _adamw(flat(weights[n]), flat(grads[n]), flat(moms[n]), flat(vars_[n]), "adamw_" + n)
        delta[n], new_m[n], new_v[n] = d_.reshape(shp), m_.reshape(shp), v_.reshape(shp)

    return (loss, dx[None], *[grads[n] for n in names], *[delta[n] for n in names],
            *[new_m[n] for n in names], *[new_v[n] for n in names])
```

```python
import functools
import math

import jax
import jax.numpy as jnp
from jax import lax
from jax.experimental import pallas as pl
from jax.experimental.pallas import tpu as pltpu

F32 = jnp.float32
BF16 = jnp.bfloat16
MESH = pl.DeviceIdType.MESH
ANY = pl.BlockSpec(memory_space=pl.ANY)
VMEM_SPEC = pl.BlockSpec(memory_space=pltpu.VMEM)

D = 1024
L = 2048
DEPTH = 4
SSD_W = 1024
XBC = 1536
NH = 16
HP = 64
NS = 128
Q = 128
NC = L // Q
ATT_W = 512
NQH = 8
NKV = 2
HD = 64
HH = HD // 2
CMC = 512
CMK = 31
DFF = 4096
N_IN = 4368
N_PAD = 4480
RMS_EPS = 1e-6
LN_EPS = 1e-5
NEG = -1e30
LR, B1, B2, EPS_A, WD, STEP = 0.001, 0.9, 0.999, 1e-8, 0.01, 10
VMEM_LIMIT = 56 * 1024 * 1024
N_DEV = 8


def _cp(sem=None):
    kw = dict(vmem_limit_bytes=VMEM_LIMIT)
    if sem is not None:
        kw["dimension_semantics"] = sem
    return pltpu.CompilerParams(**kw)


def _dg(a, b, ca, cb):
    return lax.dot_general(a, b, (((ca,), (cb,)), ((), ())), preferred_element_type=F32)


def _split3(x):
    hi = x.astype(BF16)
    r = x - hi.astype(F32)
    mid = r.astype(BF16)
    lo = (r - mid.astype(F32)).astype(BF16)
    return hi, mid, lo


def _xdot_l(x, m, ca=1, cb=0):
    hi, mid, lo = _split3(x)
    return _dg(hi, m, ca, cb) + _dg(mid, m, ca, cb) + _dg(lo, m, ca, cb)


def _xdot_r(m, x, ca=1, cb=0):
    hi, mid, lo = _split3(x)
    return _dg(m, hi, ca, cb) + _dg(m, mid, ca, cb) + _dg(m, lo, ca, cb)


def _rowdot(v, m):
    return _xdot_l(jnp.broadcast_to(v, (8, v.shape[1])), m)[0:1]


def _sigmoid(x):
    return 1.0 / (1.0 + jnp.exp(-x))


def _softplus(x):
    e = jnp.exp(-jnp.abs(x))
    u = 1.0 + e
    l1p = jnp.where(u == 1.0, e, jnp.log(u) * (e / jnp.where(u == 1.0, 1.0, u - 1.0)))
    return jnp.maximum(x, 0.0) + l1p


def _tm(k):
    return L if k <= D else L // 2


def _mm(a, b, mode, name, add=None, relu2_of=None, tn=512):
    m, k = a.shape
    tm = min(m, _tm(k))
    a_spec = pl.BlockSpec((tm, k), lambda i, j: (i, 0))
    if mode == "nn":
        n = b.shape[1]
        b_spec = pl.BlockSpec((k, tn), lambda i, j: (0, j))
        cb = 0
    else:
        n = b.shape[0]
        b_spec = pl.BlockSpec((tn, k), lambda i, j: (j, 0))
        cb = 1
    assert m % tm == 0 and n % tn == 0, (m, n, tm, tn)
    o_spec = pl.BlockSpec((tm, tn), lambda i, j: (i, j))
    out_dtype = F32
    if relu2_of is not None:
        def body(a_ref, b_ref, c_ref, o_ref):
            o_ref[...] = (_dg(a_ref[...], b_ref[...], 1, cb) * (2.0 * jnp.sqrt(c_ref[...].astype(F32)))).astype(BF16)
        ins, specs, out_dtype = (a, b, relu2_of), [a_spec, b_spec, o_spec], BF16
    elif add is None:
        def body(a_ref, b_ref, o_ref):
            o_ref[...] = _dg(a_ref[...], b_ref[...], 1, cb)
        ins, specs = (a, b), [a_spec, b_spec]
    else:
        def body(a_ref, b_ref, c_ref, o_ref):
            o_ref[...] = _dg(a_ref[...], b_ref[...], 1, cb) + c_ref[...]
        ins, specs = (a, b, add), [a_spec, b_spec, o_spec]
    return pl.pallas_call(
        body, name=name, grid=(m // tm, n // tn), in_specs=specs, out_specs=o_spec,
        out_shape=jax.ShapeDtypeStruct((m, n), out_dtype), compiler_params=_cp(("parallel", "parallel")))(*ins)


def _mm_up(a, b, name, tn=512):
    m = a.shape[0]
    cs = DFF // 4
    per = cs // tn
    tm = min(m, _tm(D))

    def body(a_ref, b_ref, r_ref):
        r = jnp.maximum(_dg(a_ref[...], b_ref[0], 1, 0), 0.0)
        r_ref[...] = (r * r).astype(BF16)

    return pl.pallas_call(
        body, name=name, grid=(m // tm, DFF // tn),
        in_specs=[pl.BlockSpec((tm, D), lambda i, j: (i, 0)),
                  pl.BlockSpec((1, D, tn), lambda i, j: (j // per, 0, j % per))],
        out_specs=pl.BlockSpec((tm, tn), lambda i, j: (i, j)),
        out_shape=jax.ShapeDtypeStruct((m, DFF), BF16), compiler_params=_cp(("parallel", "parallel")))(a, b)


def _mm_cs_nt(a, b, name, tn=512):
    m = a.shape[0]
    _, n, cs = b.shape
    tm = min(m, _tm(4 * cs))

    def body(a_ref, b_ref, o_ref):
        acc = _dg(a_ref[:, 0:cs], b_ref[0], 1, 1)
        for s in range(1, 4):
            acc = acc + _dg(a_ref[:, s * cs:(s + 1) * cs], b_ref[s], 1, 1)
        o_ref[...] = acc

    return pl.pallas_call(
        body, name=name, grid=(m // tm, n // tn),
        in_specs=[pl.BlockSpec((tm, 4 * cs), lambda i, j: (i, 0)), pl.BlockSpec((4, tn, cs), lambda i, j: (0, j, 0))],
        out_specs=pl.BlockSpec((tm, tn), lambda i, j: (i, j)),
        out_shape=jax.ShapeDtypeStruct((m, n), F32), compiler_params=_cp(("parallel", "parallel")))(a, b)


def _mm_dw(a, b, name, col_shards=False, tn=512):
    k, m = a.shape
    n = b.shape[1]
    tm = min(m, D)
    a_spec = pl.BlockSpec((k, tm), lambda i, j: (0, i))
    b_spec = pl.BlockSpec((k, tn), lambda i, j: (0, j))
    if col_shards:
        per = (n // 4) // tn
        o_spec = pl.BlockSpec((1, tm, tn), lambda i, j: (j // per, i, j % per))
        shape = (4, m, n // 4)
    else:
        o_spec = pl.BlockSpec((tm, tn), lambda i, j: (i, j))
        shape = (m, n)

    def body(a_ref, b_ref, o_ref, ob_ref):
        acc = _dg(a_ref[...], b_ref[...], 0, 0).reshape(o_ref.shape)
        o_ref[...] = acc
        ob_ref[...] = acc.astype(BF16)

    return pl.pallas_call(
        body, name=name, grid=(m // tm, n // tn), in_specs=[a_spec, b_spec], out_specs=[o_spec, o_spec],
        out_shape=[jax.ShapeDtypeStruct(shape, F32), jax.ShapeDtypeStruct(shape, BF16)],
        compiler_params=_cp(("parallel", "parallel")))(a, b)


TR = 256


def _rms_fwd(x, w, name):
    def body(x_ref, w_ref, o_ref):
        xv = x_ref[...]
        r = lax.rsqrt(jnp.mean(xv * xv, axis=-1, keepdims=True) + RMS_EPS)
        o_ref[...] = (xv * r * w_ref[...]).astype(BF16)

    return pl.pallas_call(
        body, name=name, grid=(L // TR,),
        in_specs=[pl.BlockSpec((TR, D), lambda i: (i, 0)), pl.BlockSpec((1, D), lambda i: (0, 0))],
        out_specs=pl.BlockSpec((TR, D), lambda i: (i, 0)),
        out_shape=jax.ShapeDtypeStruct((L, D), BF16), compiler_params=_cp(("parallel",)))(x, w)


def _rms_bwd(x, w, dh, dres, name):
    def body(x_ref, w_ref, dh_ref, dr_ref, dx_ref, dxb_ref, dw_ref):
        xv = x_ref[...]
        r = lax.rsqrt(jnp.mean(xv * xv, axis=-1, keepdims=True) + RMS_EPS)
        n = xv * r
        dhv = dh_ref[...]
        g = dhv * w_ref[...]
        dx = dr_ref[...] + r * (g - n * jnp.mean(g * n, axis=-1, keepdims=True))
        dx_ref[...] = dx
        dxb_ref[...] = dx.astype(BF16)

        @pl.when(pl.program_id(0) == 0)
        def _():
            dw_ref[...] = jnp.zeros_like(dw_ref)
        dw_ref[...] += jnp.sum(dhv * n, axis=0, keepdims=True)

    row = pl.BlockSpec((TR, D), lambda i: (i, 0))
    vec = pl.BlockSpec((1, D), lambda i: (0, 0))
    return pl.pallas_call(
        body, name=name, grid=(L // TR,), in_specs=[row, vec, row, row], out_specs=[row, row, vec],
        out_shape=[jax.ShapeDtypeStruct((L, D), F32), jax.ShapeDtypeStruct((L, D), BF16),
                   jax.ShapeDtypeStruct((1, D), F32)],
        compiler_params=_cp(("arbitrary",)))(x, w, dh, dres)


def _loss_bwd(y, t, name):
    def body(y_ref, t_ref, l_ref, d_ref, db_ref):
        e = y_ref[...] - t_ref[...]
        d = e * (1.0 / D)
        d_ref[...] = d
        db_ref[...] = d.astype(BF16)

        @pl.when(pl.program_id(0) == 0)
        def _():
            l_ref[...] = jnp.zeros_like(l_ref)
        s = jnp.sum(jnp.sum(e * e, axis=-1, keepdims=True), axis=0, keepdims=True)
        l_ref[...] += jnp.broadcast_to(s, l_ref.shape)

    row = pl.BlockSpec((TR, D), lambda i: (i, 0))
    tile = pl.BlockSpec((8, 128), lambda i: (0, 0))
    return pl.pallas_call(
        body, name=name, grid=(L // TR,), in_specs=[row, row], out_specs=[tile, row, row],
        out_shape=[jax.ShapeDtypeStruct((8, 128), F32), jax.ShapeDtypeStruct((L, D), F32),
                   jax.ShapeDtypeStruct((L, D), BF16)],
        compiler_params=_cp(("arbitrary",)))(y, t)


def _adamw(w, g, m, v, name):
    rows, cols = w.shape
    tr = rows
    for cand in (512, 256, 128, 64, 32, 16, 8):
        if rows % cand == 0 and cand * cols * 4 <= 2 * 1024 * 1024:
            tr = cand
            break
    c1 = 1.0 / (1.0 - B1 ** STEP)
    c2 = 1.0 / (1.0 - B2 ** STEP)

    def body(w_ref, g_ref, m_ref, v_ref, d_ref, mo_ref, vo_ref):
        gv = g_ref[...]
        mn = B1 * m_ref[...] + (1.0 - B1) * gv
        vn = B2 * v_ref[...] + (1.0 - B2) * (gv * gv)
        mo_ref[...] = mn
        vo_ref[...] = vn
        d_ref[...] = -LR * ((mn * c1) / (jnp.sqrt(vn * c2) + EPS_A) + WD * w_ref[...])

    blk = pl.BlockSpec((tr, cols), lambda i: (i, 0))
    shp = jax.ShapeDtypeStruct((rows, cols), F32)
    return pl.pallas_call(body, name=name, grid=(rows // tr,), in_specs=[blk] * 4, out_specs=[blk] * 3,
                          out_shape=[shp] * 3, compiler_params=_cp(("parallel",)))(w, g, m, v)


CT = 256
CPAD = 32


U_X, U_Z, U_A, U_G, U_Q, U_K, U_V, U_DT = 0, 1536, 2560, 3072, 3584, 4096, 4224, 4352


def _cfull(shape):
    return pl.BlockSpec(shape, lambda i: (0,) * len(shape))


def _conf_in_specs():
    return [pl.BlockSpec((L, CMC), lambda i: (0, U_A // CMC)), pl.BlockSpec((L, CMC), lambda i: (0, U_G // CMC)),
            _cfull((CMK, CMC)), _cfull((1, CMC)), _cfull((1, CMC)), _cfull((1, CMC))]


CEXT = 8
CWIN = CT + CPAD
CROWS = L + CPAD + CEXT


def _fill_shifted(src_ref, base, win_ref, sh_ref):
    win_ref[...] = src_ref[pl.ds(base, CWIN + CEXT), :]
    for p in range(8):
        sh_ref[p] = win_ref[pl.ds(p, CWIN), :]


def _tap(sh_ref, o):
    return sh_ref[o % 8, 8 * (o // 8):8 * (o // 8) + CT, :]


def _conf_scratch(n_padded):
    return ([pltpu.VMEM((CROWS, CMC), F32)] * n_padded
            + [pltpu.VMEM((CWIN + CEXT, CMC), F32), pltpu.VMEM((8, CWIN, CMC), F32)])


def _conf_fwd(u, w, b, lw, lb, name):
    def body(a_ref, g_ref, w_ref, b_ref, lw_ref, lb_ref, o_ref, hp_ref, win_ref, sh_ref):
        hp_ref[0:CPAD, :] = jnp.zeros((CPAD, CMC), F32)
        hp_ref[CPAD:CPAD + L, :] = a_ref[...] * _sigmoid(g_ref[...])
        hp_ref[CPAD + L:, :] = jnp.zeros((CEXT, CMC), F32)

        def tile(i, carry):
            base = pl.multiple_of(i * CT, CT)
            _fill_shifted(hp_ref, base, win_ref, sh_ref)
            c = jnp.broadcast_to(b_ref[...], (CT, CMC))
            for k in range(CMK):
                c = c + w_ref[k:k + 1, :] * _tap(sh_ref, 2 + k)
            mu = jnp.mean(c, axis=-1, keepdims=True)
            cc = c - mu
            var = jnp.mean(cc * cc, axis=-1, keepdims=True)
            l = cc * lax.rsqrt(var + LN_EPS) * lw_ref[...] + lb_ref[...]
            o_ref[pl.ds(base, CT), :] = (l * _sigmoid(l)).astype(BF16)
            return carry

        lax.fori_loop(0, L // CT, tile, 0)

    return pl.pallas_call(
        body, name=name, grid=(1,), in_specs=_conf_in_specs(), out_specs=_cfull((L, CMC)),
        out_shape=jax.ShapeDtypeStruct((L, CMC), BF16), scratch_shapes=_conf_scratch(1),
        compiler_params=_cp(("arbitrary",)))(u, u, w, b, lw, lb)


def _conf_bwd(u, w, b, lw, lb, dy, name):
    def body(a_ref, g_ref, w_ref, b_ref, lw_ref, lb_ref, dy_ref,
             da_ref, dg_ref, dw_ref, db_ref, dlw_ref, dlb_ref, hp_ref, dcp_ref, win_ref, sh_ref):
        hp_ref[0:CPAD, :] = jnp.zeros((CPAD, CMC), F32)
        hp_ref[CPAD:CPAD + L, :] = a_ref[...] * _sigmoid(g_ref[...])
        hp_ref[CPAD + L:, :] = jnp.zeros((CEXT, CMC), F32)
        dcp_ref[L:, :] = jnp.zeros((CPAD + CEXT, CMC), F32)
        dw_ref[...] = jnp.zeros_like(dw_ref)
        db_ref[...] = jnp.zeros_like(db_ref)
        dlw_ref[...] = jnp.zeros_like(dlw_ref)
        dlb_ref[...] = jnp.zeros_like(dlb_ref)

        def tile1(i, carry):
            base = pl.multiple_of(i * CT, CT)
            _fill_shifted(hp_ref, base, win_ref, sh_ref)
            c = jnp.broadcast_to(b_ref[...], (CT, CMC))
            for k in range(CMK):
                c = c + w_ref[k:k + 1, :] * _tap(sh_ref, 2 + k)
            mu = jnp.mean(c, axis=-1, keepdims=True)
            cc = c - mu
            var = jnp.mean(cc * cc, axis=-1, keepdims=True)
            rstd = lax.rsqrt(var + LN_EPS)
            n = cc * rstd
            l = n * lw_ref[...] + lb_ref[...]
            sl = _sigmoid(l)
            dl = dy_ref[pl.ds(base, CT), :] * (sl * (1.0 + l * (1.0 - sl)))
            dlw_ref[...] += jnp.sum(dl * n, axis=0, keepdims=True)
            dlb_ref[...] += jnp.sum(dl, axis=0, keepdims=True)
            dn = dl * lw_ref[...]
            dc = rstd * (dn - jnp.mean(dn, axis=-1, keepdims=True)
                         - n * jnp.mean(dn * n, axis=-1, keepdims=True))
            db_ref[...] += jnp.sum(dc, axis=0, keepdims=True)
            for k in range(CMK):
                dw_ref[k:k + 1, :] += jnp.sum(dc * _tap(sh_ref, 2 + k), axis=0, keepdims=True)
            dcp_ref[pl.ds(base, CT), :] = dc
            return carry

        lax.fori_loop(0, L // CT, tile1, 0)

        def tile2(i, carry):
            base = pl.multiple_of(i * CT, CT)
            _fill_shifted(dcp_ref, base, win_ref, sh_ref)
            dh = jnp.zeros((CT, CMC), F32)
            for k in range(CMK):
                dh = dh + w_ref[k:k + 1, :] * _tap(sh_ref, CMK - 1 - k)
            av = a_ref[pl.ds(base, CT), :]
            sg = _sigmoid(g_ref[pl.ds(base, CT), :])
            da_ref[pl.ds(base, CT), :] = (dh * sg).astype(BF16)
            dg_ref[pl.ds(base, CT), :] = (dh * av * sg * (1.0 - sg)).astype(BF16)
            return carry

        lax.fori_loop(0, L // CT, tile2, 0)

    vec = jax.ShapeDtypeStruct((1, CMC), F32)
    dy_spec = pl.BlockSpec((L, CMC), lambda i: (0, (SSD_W + ATT_W) // CMC))
    return pl.pallas_call(
        body, name=name, grid=(1,), in_specs=_conf_in_specs() + [dy_spec],
        out_specs=[_cfull((L, CMC)), _cfull((L, CMC)), _cfull((32, CMC)), _cfull((1, CMC)), _cfull((1, CMC)),
                   _cfull((1, CMC))],
        out_shape=[jax.ShapeDtypeStruct((L, CMC), BF16), jax.ShapeDtypeStruct((L, CMC), BF16),
                   jax.ShapeDtypeStruct((32, CMC), F32), vec, vec, vec],
        scratch_shapes=_conf_scratch(2),
        compiler_params=_cp(("arbitrary",)))(u, u, w, b, lw, lb, dy)


NPAIR = NQH // 2


def _partner(x, lo32):
    return jnp.where(lo32, pltpu.roll(x, 96, 1), pltpu.roll(x, 32, 1))


def _swa_prep(x, w2, w2p, c4, s4, bd, lo32):
    r = lax.rsqrt(_xdot_l(x * x, bd) * (1.0 / HD) + RMS_EPS)
    xh = x * r
    return r, xh, xh * w2 * c4 + _partner(xh, lo32) * w2p * s4


def _swa_unprep(dr, r, xh, w2, w2p, c4, s4, bd, lo32):
    dn = dr * c4
    dnp = dr * s4
    gx = dn * w2 + _partner(dnp * w2p, lo32)
    dw = jnp.sum((dn + _partner(dnp, lo32)) * xh, axis=0, keepdims=True)
    mu = _xdot_l(gx * xh, bd) * (1.0 / HD)
    return r * (gx - xh * mu), dw


def _swa_softmax(s, sink):
    row = lax.broadcasted_iota(jnp.int32, (L, 2 * Q), 0)
    col = lax.broadcasted_iota(jnp.int32, (L, 2 * Q), 1)
    rm = row & (Q - 1)
    valid = (col > rm) & (col <= rm + Q) & ((row >= Q) | (col >= Q))
    s = jnp.where(valid, s * (1.0 / math.sqrt(HD)), NEG)
    m = jnp.maximum(jnp.max(s, axis=-1, keepdims=True), sink)
    p = jnp.exp(s - m)
    ps = jnp.exp(sink - m)
    inv = 1.0 / (jnp.sum(p, axis=-1, keepdims=True) + ps)
    return p * inv, ps * inv


def _swa_in_specs():
    tab = pl.BlockSpec((L, 128), lambda p: (0, 0))
    wv = pl.BlockSpec((1, 128), lambda p: (0, 0))
    sk = pl.BlockSpec((1, 1, 128), lambda p: (p, 0, 0))
    return [pl.BlockSpec((L, 128), lambda p: (0, U_Q // 128 + p)), pl.BlockSpec((L, 128), lambda p: (0, U_K // 128)),
            pl.BlockSpec((L, 128), lambda p: (0, U_V // 128)), tab, tab, wv, wv, wv, wv, sk, sk,
            pl.BlockSpec((128, 128), lambda p: (0, 0))]


def _swa_setup(q_ref, k_ref, v_ref, c_ref, s_ref, qw_ref, qwp_ref, kw_ref, kwp_ref, bd_ref, kpad, vpad):
    g = pl.program_id(0) // 2
    lane = lax.broadcasted_iota(jnp.int32, (L, 128), 1)
    lo32 = (lane & 32) == 0
    own = (lane >> 6) == g
    c4, s4, bd = c_ref[...], s_ref[...], bd_ref[...]
    qn = _swa_prep(q_ref[...], qw_ref[...], qwp_ref[...], c4, s4, bd, lo32)
    kn = _swa_prep(k_ref[...], kw_ref[...], kwp_ref[...], c4, s4, bd, lo32)
    vv = v_ref[...]
    kpad[0:Q, :] = jnp.zeros((Q, 128), BF16)
    vpad[0:Q, :] = jnp.zeros((Q, 128), BF16)
    kpad[Q:, :] = jnp.where(own, kn[2], pltpu.roll(kn[2], HD, 1)).astype(BF16)
    vpad[Q:, :] = jnp.where(own, vv, pltpu.roll(vv, HD, 1)).astype(BF16)
    return qn, kn, lo32, own, c4, s4, bd


def _swa_fwd(u, cos4, sin4, qw2, qw2p, kw2, kw2p, sink_e, sink_o, bd, name):
    def body(q_ref, k_ref, v_ref, c_ref, s_ref, qw_ref, qwp_ref, kw_ref, kwp_ref, ske_ref, sko_ref, bd_ref,
             o_ref, kpad, vpad, s_scr, p_scr):
        qn, _, _, _, _, _, _ = _swa_setup(q_ref, k_ref, v_ref, c_ref, s_ref, qw_ref, qwp_ref, kw_ref, kwp_ref,
                                          bd_ref, kpad, vpad)
        qr = qn[2]
        first = lax.broadcasted_iota(jnp.int32, (Q, 128), 1) < HD
        for n in range(NC):
            rows = slice(n * Q, (n + 1) * Q)
            kc = kpad[n * Q:(n + 2) * Q, :]
            s_scr[0, rows, :] = _dg(jnp.where(first, qr[rows], 0.0).astype(BF16), kc, 1, 1)
            s_scr[1, rows, :] = _dg(jnp.where(first, 0.0, qr[rows]).astype(BF16), kc, 1, 1)
        for h, sk_ref in ((0, ske_ref), (1, sko_ref)):
            p, _ = _swa_softmax(s_scr[h], sk_ref[0][:, 0:1])
            p_scr[h] = p.astype(BF16)
        for n in range(NC):
            rows = slice(n * Q, (n + 1) * Q)
            vc = vpad[n * Q:(n + 2) * Q, :]
            o_ref[rows, :] = jnp.where(first, _dg(p_scr[0, rows, :], vc, 1, 0),
                                       _dg(p_scr[1, rows, :], vc, 1, 0)).astype(BF16)

    return pl.pallas_call(
        body, name=name, grid=(NPAIR,), in_specs=_swa_in_specs(),
        out_specs=pl.BlockSpec((L, 128), lambda p: (0, p)),
        out_shape=jax.ShapeDtypeStruct((L, ATT_W), BF16),
        scratch_shapes=[pltpu.VMEM((L + Q, 128), BF16), pltpu.VMEM((L + Q, 128), BF16),
                        pltpu.VMEM((2, L, 2 * Q), F32), pltpu.VMEM((2, L, 2 * Q), BF16)],
        compiler_params=_cp(("arbitrary",)))(u, u, u, cos4, sin4, qw2, qw2p, kw2, kw2p, sink_e, sink_o, bd)


def _swa_bwd(u, dy, cos4, sin4, qw2, qw2p, kw2, kw2p, sink_e, sink_o, bd, name):
    def body(q_ref, k_ref, v_ref, c_ref, s_ref, qw_ref, qwp_ref, kw_ref, kwp_ref, ske_ref, sko_ref, bd_ref, do_ref,
             dq_ref, dk_ref, dv_ref, dqw_ref, dkw_ref, dse_ref, dso_ref,
             kpad, vpad, s_scr, dp_scr, ds_scr, pb_scr, dkr_acc, dv_acc, dqr_scr):
        pidx = pl.program_id(0)

        @pl.when(pidx == 0)
        def _():
            dkr_acc[...] = jnp.zeros_like(dkr_acc)
            dv_acc[...] = jnp.zeros_like(dv_acc)
            dqw_ref[...] = jnp.zeros_like(dqw_ref)

        qn, kn, lo32, own, c4, s4, bd = _swa_setup(q_ref, k_ref, v_ref, c_ref, s_ref, qw_ref, qwp_ref, kw_ref,
                                                   kwp_ref, bd_ref, kpad, vpad)
        qr = qn[2]
        lane_q = lax.broadcasted_iota(jnp.int32, (Q, 128), 1)
        first = lane_q < HD
        own_q = (lane_q >> 6) == pidx // 2

        def halves(t):
            return jnp.where(first, t, 0.0).astype(BF16), jnp.where(first, 0.0, t).astype(BF16)

        for n in range(NC):
            rows = slice(n * Q, (n + 1) * Q)
            kc = kpad[n * Q:(n + 2) * Q, :]
            vc = vpad[n * Q:(n + 2) * Q, :]
            qm = halves(qr[rows])
            dom = halves(do_ref[rows, :])
            for h in range(2):
                s_scr[h, rows, :] = _dg(qm[h], kc, 1, 1)
                dp_scr[h, rows, :] = _dg(dom[h], vc, 1, 1)
        for h, sk_ref, dsk_ref in ((0, ske_ref, dse_ref), (1, sko_ref, dso_ref)):
            p, ps = _swa_softmax(s_scr[h], sk_ref[0][:, 0:1])
            dp = dp_scr[h]
            delta = jnp.sum(p * dp, axis=-1, keepdims=True)
            dsk_ref[0] = jnp.broadcast_to(-jnp.sum(ps * delta, axis=0, keepdims=True), (1, 128))
            ds_scr[h] = (p * (dp - delta) * (1.0 / math.sqrt(HD))).astype(BF16)
            pb_scr[h] = p.astype(BF16)
        for n in range(NC):
            rows = slice(n * Q, (n + 1) * Q)
            kc = kpad[n * Q:(n + 2) * Q, :]
            dqr_scr[rows, :] = jnp.where(first, _dg(ds_scr[0, rows, :], kc, 1, 0), _dg(ds_scr[1, rows, :], kc, 1, 0))
        for m in range(NC):
            acc_k = jnp.zeros((Q, 128), F32)
            acc_v = jnp.zeros((Q, 128), F32)
            for n, cols in ((m, slice(Q, 2 * Q)), (m + 1, slice(0, Q))):
                if n >= NC:
                    continue
                rows = slice(n * Q, (n + 1) * Q)
                qm = halves(qr[rows])
                dom = halves(do_ref[rows, :])
                for h in range(2):
                    acc_k = acc_k + _dg(ds_scr[h, rows, cols], qm[h], 0, 0)
                    acc_v = acc_v + _dg(pb_scr[h, rows, cols], dom[h], 0, 0)
            rows = slice(m * Q, (m + 1) * Q)
            dkr_acc[rows, :] += jnp.where(own_q, acc_k + pltpu.roll(acc_k, HD, 1), 0.0)
            dv_acc[rows, :] += jnp.where(own_q, acc_v + pltpu.roll(acc_v, HD, 1), 0.0)
        dq, dqw = _swa_unprep(dqr_scr[...], qn[0], qn[1], qw_ref[...], qwp_ref[...], c4, s4, bd, lo32)
        dq_ref[...] = dq.astype(BF16)
        dqw_ref[...] += dqw

        @pl.when(pidx == NPAIR - 1)
        def _():
            dk, dkw = _swa_unprep(dkr_acc[...], kn[0], kn[1], kw_ref[...], kwp_ref[...], c4, s4, bd, lo32)
            dk_ref[...] = dk.astype(BF16)
            dkw_ref[...] = dkw
            dv_ref[...] = dv_acc[...].astype(BF16)

    full = pl.BlockSpec((L, 128), lambda p: (0, 0))
    wv = pl.BlockSpec((1, 128), lambda p: (0, 0))
    sk = pl.BlockSpec((1, 1, 128), lambda p: (p, 0, 0))
    vec = jax.ShapeDtypeStruct((1, 128), F32)
    skv = jax.ShapeDtypeStruct((NPAIR, 1, 128), F32)
    return pl.pallas_call(
        body, name=name, grid=(NPAIR,),
        in_specs=_swa_in_specs() + [pl.BlockSpec((L, 128), lambda p: (0, SSD_W // 128 + p))],
        out_specs=[pl.BlockSpec((L, 128), lambda p: (0, p)), full, full, wv, wv, sk, sk],
        out_shape=[jax.ShapeDtypeStruct((L, ATT_W), BF16), jax.ShapeDtypeStruct((L, 128), BF16),
                   jax.ShapeDtypeStruct((L, 128), BF16), vec, vec, skv, skv],
        scratch_shapes=[pltpu.VMEM((L + Q, 128), BF16), pltpu.VMEM((L + Q, 128), BF16),
                        pltpu.VMEM((2, L, 2 * Q), F32), pltpu.VMEM((2, L, 2 * Q), F32),
                        pltpu.VMEM((2, L, 2 * Q), BF16), pltpu.VMEM((2, L, 2 * Q), BF16),
                        pltpu.VMEM((L, 128), F32), pltpu.VMEM((L, 128), F32), pltpu.VMEM((L, 128), F32)],
        compiler_params=_cp(("arbitrary",)))(u, u, u, cos4, sin4, qw2, qw2p, kw2, kw2p, sink_e, sink_o, bd, dy)


def _ssd_consts():
    hh = jnp.arange(128)[:, None]
    e = (hh == (jnp.arange(SSD_W)[None, :] // HP)).astype(BF16)
    e2 = (hh == (jnp.arange(NH * 128)[None, :] // 128)).astype(BF16)
    et = e.T
    tril = (jnp.arange(Q)[:, None] >= jnp.arange(Q)[None, :]).astype(BF16)
    triu = tril.T
    eye = jnp.eye(128, dtype=BF16)
    return e, e2, et, tril, triu, eye


def _ssd_common(x_ref, ext_scr, cw_ref, cb_ref, dt_ref, dtb_ref, alog_ref, e_ref, e2_ref, tril_ref, triu_ref,
                arow_scr, acol_scr, eax_scr):
    conv = jnp.broadcast_to(cb_ref[...], (Q, XBC))
    for k in range(4):
        conv = conv + cw_ref[k:k + 1, :] * ext_scr[pl.ds(5 + k, Q), :]
    sg = _sigmoid(conv)
    xbc = conv * sg
    dtpre = dt_ref[...] + dtb_ref[...]
    dt = _softplus(dtpre)
    a = -jnp.exp(alog_ref[...])
    adt = dt * a
    acol = _xdot_r(tril_ref[...], adt)
    acol_scr[...] = acol
    arow_scr[...] = _xdot_l(adt, triu_ref[...], 0, 0)
    alast = acol_scr[Q - 1:Q, :]
    ea = jnp.exp(acol)
    decs = jnp.exp(alast - acol)
    e = e_ref[...]
    dt_x = _xdot_l(dt, e)
    eax_scr[...] = _xdot_l(ea, e)
    decs_x = _xdot_l(decs, e)
    acx2 = _xdot_l(acol, e2_ref[...])
    return conv, sg, xbc, dtpre, dt, a, adt, acol, alast, ea, decs, dt_x, decs_x, acx2


def _ssd_fwd(u, cw, cb, dtb, alog, dxp, nw, consts, name):
    e, e2, et, tril, triu, eye = consts

    def body(z0_ref, z1_ref, x_ref, dt_ref, cw_ref, cb_ref, dtb_ref, alog_ref, dx_ref, nw_ref, e_ref, e2_ref,
             tril_ref, triu_ref, ya_ref, ypre_ref, st_ref, s_scr, ext_scr, arow_scr, acol_scr, eax_scr):
        c = pl.program_id(0)

        @pl.when(c == 0)
        def _():
            s_scr[...] = jnp.zeros_like(s_scr)
            ext_scr[0:8, :] = jnp.zeros((8, XBC), F32)
        ext_scr[8:8 + Q, :] = x_ref[...]
        (conv, sg, xbc, dtpre, dt, a, adt, acol, alast, ea, decs, dt_x, decs_x, acx2) = _ssd_common(
            x_ref, ext_scr, cw_ref, cb_ref, dt_ref, dtb_ref, alog_ref, e_ref, e2_ref, tril_ref, triu_ref,
            arow_scr, acol_scr, eax_scr)
        ext_scr[0:8, :] = ext_scr[Q:Q + 8, :]
        xs = xbc[:, :SSD_W]
        xdt = xs * dt_x
        lane = lax.broadcasted_iota(jnp.int32, (Q, 128), 1)
        causal = lax.broadcasted_iota(jnp.int32, (Q, Q), 0) >= lax.broadcasted_iota(jnp.int32, (Q, Q), 1)
        for g in range(2):
            bg = xbc[:, SSD_W + g * NS:SSD_W + (g + 1) * NS].astype(BF16)
            cg = xbc[:, SSD_W + 2 * NS + g * NS:SSD_W + 2 * NS + (g + 1) * NS].astype(BF16)
            cbm = _dg(cg, bg, 1, 1)
            sgv = s_scr[g]
            st_ref[0, g] = sgv
            gc = slice(g * 512, (g + 1) * 512)
            yoff = _dg(cg, sgv.astype(BF16), 1, 0) * eax_scr[:, gc]
            for pr in range(4):
                h0 = g * 8 + 2 * pr
                h1 = h0 + 1
                c0 = g * 512 + pr * 128
                xp = xdt[:, c0:c0 + 128].astype(BF16)
                w0 = (cbm * jnp.exp(jnp.where(causal, acx2[:, h0 * 128:(h0 + 1) * 128] - arow_scr[h0:h0 + 1, :],
                                              NEG))).astype(BF16)
                w1 = (cbm * jnp.exp(jnp.where(causal, acx2[:, h1 * 128:(h1 + 1) * 128] - arow_scr[h1:h1 + 1, :],
                                              NEG))).astype(BF16)
                yd = jnp.where(lane < HP, _dg(w0, xp, 1, 0), _dg(w1, xp, 1, 0))
                ypre_ref[:, c0:c0 + 128] = (yd + yoff[:, pr * 128:(pr + 1) * 128]
                                            + xs[:, c0:c0 + 128] * dx_ref[:, c0:c0 + 128])
            contrib = _dg(bg, (xdt[:, gc] * decs_x[:, gc]).astype(BF16), 0, 0)
            s_scr[g] = sgv * eax_scr[Q - 1:Q, gc] + contrib
        for g, zr in enumerate((z0_ref, z1_ref)):
            gc = slice(g * 512, (g + 1) * 512)
            zz = zr[...]
            ggg = ypre_ref[:, gc] * (zz * _sigmoid(zz))
            rstd = lax.rsqrt(jnp.mean(ggg * ggg, axis=-1, keepdims=True) + RMS_EPS)
            ya_ref[:, gc] = (ggg * rstd * nw_ref[:, gc]).astype(BF16)

    def row(w, blk=0):
        return pl.BlockSpec((Q, w), lambda c: (c, blk))

    def full(shape):
        return pl.BlockSpec(shape, lambda c: (0,) * len(shape))

    return pl.pallas_call(
        body, name=name, grid=(NC,),
        in_specs=[row(512, U_Z // 512), row(512, U_Z // 512 + 1), row(XBC, U_X // XBC), row(128, U_DT // 128),
                  full((4, XBC)), full((1, XBC)), full((1, 128)), full((1, 128)),
                  full((1, SSD_W)), full((1, SSD_W)), full((128, SSD_W)), full((128, NH * 128)), full((Q, Q)),
                  full((Q, Q))],
        out_specs=[row(SSD_W), row(SSD_W), pl.BlockSpec((1, 2, NS, 512), lambda c: (c, 0, 0, 0))],
        out_shape=[jax.ShapeDtypeStruct((L, SSD_W), BF16), jax.ShapeDtypeStruct((L, SSD_W), F32),
                   jax.ShapeDtypeStruct((NC, 2, NS, 512), F32)],
        scratch_shapes=[pltpu.VMEM((2, NS, 512), F32), pltpu.VMEM((Q + 8, XBC), F32), pltpu.VMEM((128, Q), F32),
                        pltpu.VMEM((Q, 128), F32), pltpu.VMEM((Q, SSD_W), F32)],
        compiler_params=_cp(("arbitrary",)))(u, u, u, u, cw, cb, dtb, alog, dxp, nw, e, e2, tril, triu)


def _ssd_bwd(u, ypre, st, dy, cw, cb, dtb, alog, dxp, nw, consts, name):
    e, e2, et, tril, triu, eye = consts

    def body(z0_ref, z1_ref, x_ref, xp_ref, dt_ref, ypre_ref, st_ref, dya_ref, cw_ref, cb_ref, dtb_ref, alog_ref, dx_ref,
             nw_ref, e_ref, e2_ref, et_ref, tril_ref, triu_ref, eye_ref,
             dz_ref, dxr_ref, ddtr_ref, dcw_ref, dcb_ref, ddtb_ref, dalog_ref, dd_ref, dnw_ref,
             g_scr, ext_scr, ext2_scr, arow_scr, acol_scr, eax_scr, darow_scr, dxdt_scr, t1_scr, t2_scr, dgg_scr):
        i = pl.program_id(0)

        @pl.when(i == 0)
        def _():
            g_scr[...] = jnp.zeros_like(g_scr)
            ext2_scr[Q:Q + 8, :] = jnp.zeros((8, XBC), F32)
            for r in (dcw_ref, dcb_ref, ddtb_ref, dalog_ref, dd_ref, dnw_ref):
                r[...] = jnp.zeros_like(r)
        not_first = jnp.where(i < NC - 1, 1.0, 0.0)
        ext_scr[0:8, :] = xp_ref[Q - 8:Q, :] * not_first
        ext_scr[8:8 + Q, :] = x_ref[...]
        (conv, sg, xbc, dtpre, dt, a, adt, acol, alast, ea, decs, dt_x, decs_x, acx2) = _ssd_common(
            x_ref, ext_scr, cw_ref, cb_ref, dt_ref, dtb_ref, alog_ref, e_ref, e2_ref, tril_ref, triu_ref,
            arow_scr, acol_scr, eax_scr)
        et_m = et_ref[...]
        xs = xbc[:, :SSD_W]
        xdt = xs * dt_x
        y = ypre_ref[...]
        zz = jnp.concatenate([z0_ref[...], z1_ref[...]], axis=1)
        sz = _sigmoid(zz)
        silu_z = zz * sz
        gg = y * silu_z
        dya = dya_ref[...]
        for g in range(2):
            gc = slice(g * 512, (g + 1) * 512)
            ggg = gg[:, gc]
            rstd = lax.rsqrt(jnp.mean(ggg * ggg, axis=-1, keepdims=True) + RMS_EPS)
            n = ggg * rstd
            dyag = dya[:, gc]
            dnw_ref[:, gc] += jnp.sum(dyag * n, axis=0, keepdims=True)
            dn = dyag * nw_ref[:, gc]
            dgg_scr[:, gc] = rstd * (dn - n * jnp.mean(dn * n, axis=-1, keepdims=True))
        dgg = dgg_scr[...]
        dy = dgg * silu_z
        dz_ref[...] = (dgg * y * (sz * (1.0 + zz * (1.0 - sz)))).astype(BF16)
        dd_ref[...] += _rowdot(jnp.sum(dy * xs, axis=0, keepdims=True), et_m)
        dxs = dy * dx_ref[...]
        dys = dy * eax_scr[...]
        lane = lax.broadcasted_iota(jnp.int32, (Q, 128), 1)
        causal = lax.broadcasted_iota(jnp.int32, (Q, Q), 0) >= lax.broadcasted_iota(jnp.int32, (Q, Q), 1)
        darow_scr[...] = jnp.zeros_like(darow_scr)
        dacol = jnp.zeros((Q, 128), F32)
        dcdx = []
        dbs = []
        dcs = []
        for g in range(2):
            gc = slice(g * 512, (g + 1) * 512)
            bg = xbc[:, SSD_W + g * NS:SSD_W + (g + 1) * NS].astype(BF16)
            cg = xbc[:, SSD_W + 2 * NS + g * NS:SSD_W + 2 * NS + (g + 1) * NS].astype(BF16)
            cbm = _dg(cg, bg, 1, 1)
            sgv = st_ref[0, g]
            sgb = sgv.astype(BF16)
            gv = g_scr[g]
            gvb = gv.astype(BF16)
            yoff = _dg(cg, sgb, 1, 0) * eax_scr[:, gc]
            dysg = dys[:, gc].astype(BF16)
            dcg = _dg(dysg, sgb, 1, 1)
            ds_off = _dg(cg, dysg, 0, 0)
            t1_scr[:, gc] = dy[:, gc] * yoff
            xdec = xdt[:, gc] * decs_x[:, gc]
            dxd = _dg(bg, gvb, 1, 0)
            dbg = _dg(xdec.astype(BF16), gvb, 1, 1)
            dxdt_g = dxd * decs_x[:, gc]
            t2_scr[:, gc] = dxd * xdt[:, gc]
            cdx = eax_scr[Q - 1:Q, gc]
            dcdx.append(jnp.sum(gv * sgv, axis=0, keepdims=True))
            g_scr[g] = gv * cdx + ds_off
            dcb_acc = jnp.zeros((Q, Q), F32)
            for pr in range(4):
                c0 = g * 512 + pr * 128
                xp = xdt[:, c0:c0 + 128].astype(BF16)
                dyp = dy[:, c0:c0 + 128]
                dypb = dyp.astype(BF16)
                halves = []
                for hh, keep in ((g * 8 + 2 * pr, lane < HP), (g * 8 + 2 * pr + 1, lane >= HP)):
                    lam = jnp.exp(jnp.where(causal, acx2[:, hh * 128:(hh + 1) * 128] - arow_scr[hh:hh + 1, :], NEG))
                    w = cbm * lam
                    dw = _dg(jnp.where(keep, dyp, 0.0).astype(BF16), xp, 1, 1)
                    dcb_acc = dcb_acc + dw * lam
                    t = dw * w
                    dacol = dacol + jnp.sum(t, axis=-1, keepdims=True) * (lane == hh).astype(F32)
                    darow_scr[hh:hh + 1, :] -= jnp.sum(t, axis=0, keepdims=True)
                    halves.append(_dg(w.astype(BF16), dypb, 0, 0))
                dxdt_scr[:, c0:c0 + 128] = (jnp.where(lane < HP, halves[0], halves[1])
                                            + dxdt_g[:, pr * 128:(pr + 1) * 128])
            dcbb = dcb_acc.astype(BF16)
            dcs.append(dcg + _dg(dcbb, bg, 1, 0))
            dbs.append(dbg + _dg(dcbb, cg, 0, 0))
        dacol = dacol + _xdot_l(t1_scr[...], et_m)
        ddecs = _xdot_l(t2_scr[...], et_m) * decs
        dacol = dacol - ddecs
        dalast = jnp.sum(ddecs, axis=0, keepdims=True)
        dcd = _rowdot(jnp.concatenate(dcdx, axis=1), et_m)
        dalast = dalast + dcd * jnp.exp(alast)
        dacol = dacol + _xdot_l(darow_scr[...], eye_ref[...], 0, 0)
        rowi = lax.broadcasted_iota(jnp.int32, (Q, 128), 0)
        dacol = dacol + jnp.where(rowi == Q - 1, dalast, 0.0)
        dadt = _xdot_r(triu_ref[...], dacol)
        dxdt = dxdt_scr[...]
        ddt = dadt * a + _xdot_l(dxdt * xs, et_m)
        dalog_ref[...] += jnp.sum(dadt * dt, axis=0, keepdims=True) * a
        dxs = dxs + dxdt * dt_x
        ddtr = ddt * _sigmoid(dtpre)
        ddtb_ref[...] += jnp.sum(ddtr, axis=0, keepdims=True)
        ddtr_ref[...] = ddtr.astype(BF16)
        dsilu = sg * (1.0 + conv * (1.0 - sg))
        ext2_scr[0:Q, 0:SSD_W] = dxs * dsilu[:, :SSD_W]
        for g in range(2):
            o1 = SSD_W + g * NS
            o2 = SSD_W + 2 * NS + g * NS
            ext2_scr[0:Q, o1:o1 + NS] = dbs[g] * dsilu[:, o1:o1 + NS]
            ext2_scr[0:Q, o2:o2 + NS] = dcs[g] * dsilu[:, o2:o2 + NS]
        dconv = ext2_scr[0:Q, :]
        dcb_ref[...] += jnp.sum(dconv, axis=0, keepdims=True)
        dxr = jnp.zeros((Q, XBC), F32)
        for k in range(4):
            dcw_ref[k:k + 1, :] += jnp.sum(dconv * ext_scr[pl.ds(5 + k, Q), :], axis=0, keepdims=True)
            dxr = dxr + cw_ref[k:k + 1, :] * ext2_scr[pl.ds(3 - k, Q), :]
        dxr_ref[...] = dxr.astype(BF16)
        ext2_scr[Q:Q + 8, :] = ext2_scr[0:8, :]

    def row(w, blk=0):
        return pl.BlockSpec((Q, w), lambda i: (NC - 1 - i, blk))

    def full(shape):
        return pl.BlockSpec(shape, lambda i: (0,) * len(shape))

    prev = pl.BlockSpec((Q, XBC), lambda i: (jnp.maximum(NC - 2 - i, 0), U_X // XBC))
    return pl.pallas_call(
        body, name=name, grid=(NC,),
        in_specs=[row(512, U_Z // 512), row(512, U_Z // 512 + 1), row(XBC, U_X // XBC), prev, row(128, U_DT // 128),
                  row(SSD_W),
                  pl.BlockSpec((1, 2, NS, 512), lambda i: (NC - 1 - i, 0, 0, 0)), row(SSD_W),
                  full((4, XBC)), full((1, XBC)), full((1, 128)), full((1, 128)), full((1, SSD_W)),
                  full((1, SSD_W)), full((128, SSD_W)), full((128, NH * 128)), full((SSD_W, 128)), full((Q, Q)),
                  full((Q, Q)), full((128, 128))],
        out_specs=[row(SSD_W), row(XBC), row(128), full((8, XBC)), full((1, XBC)), full((1, 128)), full((1, 128)),
                   full((1, 128)), full((1, SSD_W))],
        out_shape=[jax.ShapeDtypeStruct((L, SSD_W), BF16), jax.ShapeDtypeStruct((L, XBC), BF16),
                   jax.ShapeDtypeStruct((L, 128), BF16), jax.ShapeDtypeStruct((8, XBC), F32),
                   jax.ShapeDtypeStruct((1, XBC), F32), jax.ShapeDtypeStruct((1, 128), F32),
                   jax.ShapeDtypeStruct((1, 128), F32), jax.ShapeDtypeStruct((1, 128), F32),
                   jax.ShapeDtypeStruct((1, SSD_W), F32)],
        scratch_shapes=[pltpu.VMEM((2, NS, 512), F32), pltpu.VMEM((Q + 8, XBC), F32), pltpu.VMEM((Q + 8, XBC), F32),
                        pltpu.VMEM((128, Q), F32), pltpu.VMEM((Q, 128), F32), pltpu.VMEM((Q, SSD_W), F32),
                        pltpu.VMEM((128, Q), F32), pltpu.VMEM((Q, SSD_W), F32), pltpu.VMEM((Q, SSD_W), F32),
                        pltpu.VMEM((Q, SSD_W), F32), pltpu.VMEM((Q, SSD_W), F32)],
        compiler_params=_cp(("arbitrary",)))(u, u, u, u, u, ypre, st, dy, cw, cb, dtb, alog, dxp, nw,
                                             e, e2, et, tril, triu, eye)


def _my_pos():
    return lax.axis_index("x"), lax.axis_index("y"), lax.axis_index("c")


CHIP_REL = ((1, 0), (0, 1), (1, 1))
CHIP_XOR = (2, 1, 3)
NW = 4
AT = 256


def _chips(x, y):
    return [(1 - x if dx else x, 1 - y if dy else y) for dx, dy in CHIP_REL]


HBM_SPEC = pl.BlockSpec(memory_space=pltpu.HBM)
SEM_SPEC = pl.BlockSpec(memory_space=pltpu.SEMAPHORE)
EFFECT = pltpu.SideEffectType.DATAFLOW_SIDE_EFFECTING


def _hbm(t):
    return pltpu.with_memory_space_constraint(t, pltpu.HBM)


def _split_start(srcs, lands, after, copies, name):
    n = len(srcs)

    def body(*refs):
        src_refs, land_refs = refs[:n], refs[n:2 * n]
        send_sems, recv_sems = refs[2 * n + 1], refs[2 * n + 2]
        token = refs[-1]
        for w, k, src, dst, dev in copies(src_refs, land_refs):
            pltpu.make_async_remote_copy(src_ref=src, dst_ref=dst, send_sem=send_sems.at[3 * w + k],
                                         recv_sem=recv_sems.at[3 * w + k], device_id=dev, device_id_type=MESH).start()
        token[...] = jnp.zeros_like(token)

    outs = pl.pallas_call(
        body, name=name,
        out_shape=(pltpu.SemaphoreType.DMA((3 * n,)), pltpu.SemaphoreType.DMA((3 * n,)),
                   *[pltpu.HBM(t.shape, t.dtype) for t in srcs], *[pltpu.HBM(t.shape, t.dtype) for t in lands],
                   jax.ShapeDtypeStruct((8, 128), F32)),
        in_specs=[HBM_SPEC] * (2 * n) + [ANY],
        out_specs=(SEM_SPEC, SEM_SPEC, *([HBM_SPEC] * (2 * n)), VMEM_SPEC),
        input_output_aliases={i: 2 + i for i in range(2 * n)},
        compiler_params=pltpu.CompilerParams(has_side_effects=EFFECT))(
            *[_hbm(t) for t in srcs], *[_hbm(t) for t in lands], after)
    return outs[0], outs[1], list(outs[2:2 + n]), list(outs[2 + n:2 + 2 * n]), outs[-1]


def _split_wait(send_sems, recv_sems, srcs, lands, after, copies, name):
    n = len(srcs)

    def body(*refs):
        src_refs, land_refs = refs[:n], refs[n:2 * n]
        ssem, rsem = refs[2 * n], refs[2 * n + 1]
        for w, k, src, dst, dev in copies(src_refs, land_refs):
            cp = pltpu.make_async_remote_copy(src_ref=src, dst_ref=dst, send_sem=ssem.at[3 * w + k],
                                              recv_sem=rsem.at[3 * w + k], device_id=dev, device_id_type=MESH)
            cp.wait_send()
            cp.wait_recv()

    outs = pl.pallas_call(
        body, name=name,
        out_shape=tuple([pltpu.HBM(t.shape, t.dtype) for t in srcs] + [pltpu.HBM(t.shape, t.dtype) for t in lands]),
        in_specs=[HBM_SPEC] * (2 * n) + [SEM_SPEC, SEM_SPEC, ANY],
        out_specs=tuple([HBM_SPEC] * (2 * n)),
        input_output_aliases={i: i for i in range(2 * n)},
        compiler_params=pltpu.CompilerParams(has_side_effects=EFFECT))(*srcs, *lands, send_sems, recv_sems, after)
    return list(outs[:n]), list(outs[n:])


def _ag_copies(arrival):
    def copies(src_refs, land_refs):
        x, y, c = _my_pos()
        s = 2 * x + y
        chips = _chips(x, y)
        for w in range(NW):
            hr = src_refs[w].shape[0] // 2
            mine = pl.ds(c * hr, hr)
            for k in range(3):
                slot = s ^ CHIP_XOR[k] if arrival else s
                yield w, k, src_refs[w].at[mine], land_refs[w].at[slot, mine], (*chips[k], c)
    return copies


def _ag_forward(lands, name):
    def body(*refs):
        outs = refs[NW:2 * NW]
        send_sems, recv_sems = refs[2 * NW:]
        x, y, c = _my_pos()
        s = 2 * x + y
        sib = (x, y, 1 - c)
        sends = []
        for w in range(NW):
            hr = outs[w].shape[1] // 2
            for k in range(3):
                blk = outs[w].at[s ^ CHIP_XOR[k], pl.ds(c * hr, hr)]
                fw = pltpu.make_async_remote_copy(
                    src_ref=blk, dst_ref=blk, send_sem=send_sems.at[w, k], recv_sem=recv_sems.at[w, k],
                    device_id=sib, device_id_type=MESH)
                fw.start()
                sends.append(fw)
        for w in range(NW):
            hr = outs[w].shape[1] // 2
            for k in range(3):
                blk = outs[w].at[s ^ CHIP_XOR[k], pl.ds((1 - c) * hr, hr)]
                pltpu.make_async_remote_copy(
                    src_ref=blk, dst_ref=blk, send_sem=send_sems.at[w, k], recv_sem=recv_sems.at[w, k],
                    device_id=sib, device_id_type=MESH).wait_recv()
        for cp in sends:
            cp.wait_send()

    return pl.pallas_call(
        body, name=name, in_specs=[ANY] * NW, out_specs=[ANY] * NW,
        out_shape=[jax.ShapeDtypeStruct(t.shape, t.dtype) for t in lands],
        input_output_aliases={w: w for w in range(NW)},
        scratch_shapes=[pltpu.SemaphoreType.DMA((NW, 3)), pltpu.SemaphoreType.DMA((NW, 3))])(*lands)


def _rs_copies(src_refs, land_refs):
    x, y, c = _my_pos()
    s = 2 * x + y
    chips = _chips(x, y)
    for w in range(NW):
        for k in range(3):
            yield w, k, src_refs[w].at[s ^ CHIP_XOR[k]], land_refs[w].at[k], (*chips[k], c)


def _place_own(shard, gathered, sidx, name):
    r, cc = shard.shape

    def body(s_ref, a_ref, g_ref, o_ref):
        o_ref[0] = a_ref[...]

    return pl.pallas_call(
        body, name=name,
        grid_spec=pltpu.PrefetchScalarGridSpec(
            num_scalar_prefetch=1, grid=(r // AT,),
            in_specs=[pl.BlockSpec((AT, cc), lambda i, s_ref: (i, 0)), ANY],
            out_specs=pl.BlockSpec((1, AT, cc), lambda i, s_ref: (s_ref[0], i, 0))),
        out_shape=jax.ShapeDtypeStruct(gathered.shape, gathered.dtype),
        input_output_aliases={2: 0}, compiler_params=_cp(("parallel",)))(sidx, shard, gathered)


def _rs_pair(dwb, name):
    def body(*refs):
        ins, outs = refs[:NW], refs[NW:2 * NW]
        send_sems, recv_sems = refs[2 * NW:]
        x, y, c = _my_pos()
        cps = []
        for w in range(NW):
            hr = ins[w].shape[1] // 2
            cp = pltpu.make_async_remote_copy(
                src_ref=ins[w].at[:, pl.ds((1 - c) * hr, hr)], dst_ref=outs[w], send_sem=send_sems.at[w],
                recv_sem=recv_sems.at[w], device_id=(x, y, 1 - c), device_id_type=MESH)
            cp.start()
            cps.append(cp)
        for cp in cps:
            cp.wait()

    return pl.pallas_call(
        body, name=name, in_specs=[ANY] * NW, out_specs=[ANY] * NW,
        out_shape=[jax.ShapeDtypeStruct((4, t.shape[1] // 2, t.shape[2]), t.dtype) for t in dwb],
        scratch_shapes=[pltpu.SemaphoreType.DMA((NW,)), pltpu.SemaphoreType.DMA((NW,))])(*dwb)


def _rs_sib(q, name):
    def body(*refs):
        ins, outs = refs[:NW], refs[NW:2 * NW]
        send_sems, recv_sems = refs[2 * NW:]
        x, y, c = _my_pos()
        cps = []
        for w in range(NW):
            hr = outs[w].shape[0] // 2
            mine = pl.ds(c * hr, hr)
            cp = pltpu.make_async_remote_copy(
                src_ref=outs[w].at[mine], dst_ref=outs[w].at[mine], send_sem=send_sems.at[w],
                recv_sem=recv_sems.at[w], device_id=(x, y, 1 - c), device_id_type=MESH)
            cp.start()
            cps.append(cp)
        for w in range(NW):
            hr = outs[w].shape[0] // 2
            other = outs[w].at[pl.ds((1 - c) * hr, hr)]
            pltpu.make_async_remote_copy(
                src_ref=other, dst_ref=other, send_sem=send_sems.at[w], recv_sem=recv_sems.at[w],
                device_id=(x, y, 1 - c), device_id_type=MESH).wait_recv()
        for cp in cps:
            cp.wait_send()

    return pl.pallas_call(
        body, name=name, in_specs=[ANY] * NW, out_specs=[ANY] * NW,
        out_shape=[jax.ShapeDtypeStruct(t.shape, t.dtype) for t in q],
        input_output_aliases={w: w for w in range(NW)},
        scratch_shapes=[pltpu.SemaphoreType.DMA((NW,)), pltpu.SemaphoreType.DMA((NW,))])(*q)


def _rs_add2(dw, got, cidx, name):
    _, r, cc = dw.shape
    hr = r // 2
    nb = hr // AT

    def body(c_ref, a_ref, b_ref, o_ref, ob_ref):
        acc = a_ref[...] + b_ref[...].astype(F32)
        o_ref[...] = acc
        ob_ref[...] = acc.astype(BF16)

    blk = pl.BlockSpec((1, AT, cc), lambda sh, i, c_ref: (sh, i, 0))
    return pl.pallas_call(
        body, name=name,
        grid_spec=pltpu.PrefetchScalarGridSpec(
            num_scalar_prefetch=1, grid=(4, nb),
            in_specs=[pl.BlockSpec((1, AT, cc), lambda sh, i, c_ref: (sh, c_ref[0] * nb + i, 0)), blk],
            out_specs=[blk, blk]),
        out_shape=[jax.ShapeDtypeStruct((4, hr, cc), F32), jax.ShapeDtypeStruct((4, hr, cc), BF16)],
        compiler_params=_cp(("parallel", "parallel")))(cidx, dw, got)


def _rs_add4(p, got, scidx, name):
    _, hr, cc = p.shape
    nb = hr // AT

    def body(s_ref, p_ref, g0_ref, g1_ref, g2_ref, o_ref):
        acc = p_ref[0] + g0_ref[0].astype(F32)
        acc = acc + g1_ref[0].astype(F32)
        o_ref[...] = acc + g2_ref[0].astype(F32)

    def gk(k):
        return pl.BlockSpec((1, AT, cc), lambda i, s_ref: (k, i, 0))

    return pl.pallas_call(
        body, name=name,
        grid_spec=pltpu.PrefetchScalarGridSpec(
            num_scalar_prefetch=1, grid=(nb,),
            in_specs=[pl.BlockSpec((1, AT, cc), lambda i, s_ref: (s_ref[0], i, 0)), gk(0), gk(1), gk(2)],
            out_specs=pl.BlockSpec((AT, cc), lambda i, s_ref: (s_ref[1] * nb + i, 0))),
        out_shape=jax.ShapeDtypeStruct((2 * hr, cc), F32),
        compiler_params=_cp(("parallel",)))(scidx, p, got, got, got)


def _rs_begin(dws, dwbs, after):
    _, _, c = _my_pos()
    cidx = jnp.reshape(c, (1,)).astype(jnp.int32)
    got = _rs_pair(dwbs, "rs_pair")
    pairs = [_rs_add2(dws[w], got[w], cidx, "rs_add2_%d" % w) for w in range(NW)]
    pb = [p[1] for p in pairs]
    lands = [lax.empty((3,) + t.shape[1:], BF16) for t in pb]
    ssem, rsem, pb, lands, token = _split_start(pb, lands, after, _rs_copies, "rs_chip_start")
    return ([p[0] for p in pairs], ssem, rsem, pb, lands), token


def _rs_end(state, after):
    x, y, c = _my_pos()
    scidx = jnp.stack([2 * x + y, c]).astype(jnp.int32)
    p, ssem, rsem, pb, lands = state
    _, recv = _split_wait(ssem, rsem, pb, lands, after, _rs_copies, "rs_chip_wait")
    q = [_rs_add4(p[w], recv[w], scidx, "rs_add4_%d" % w) for w in range(NW)]
    return _rs_sib(q, "rs_sib")


def _allreduce_small(buf, name):
    rows = buf.shape[0]

    def body(src_ref, out_ref, gat_ref, send_sems, recv_sems):
        x, y, c = _my_pos()
        me = 4 * x + 2 * y + c
        gat_ref[me] = src_ref[...]
        cps = []
        for r in range(1, N_DEV):
            tx = 1 - x if (r >> 2) & 1 else x
            ty = 1 - y if (r >> 1) & 1 else y
            tc = 1 - c if r & 1 else c
            cps.append(pltpu.make_async_remote_copy(
                src_ref=src_ref, dst_ref=gat_ref.at[me], send_sem=send_sems.at[r - 1], recv_sem=recv_sems.at[r - 1],
                device_id=(tx, ty, tc), device_id_type=MESH))
        for cp in cps:
            cp.start()
        for cp in cps:
            cp.wait()
        acc = gat_ref[0]
        for k in range(1, N_DEV):
            acc = acc + gat_ref[k]
        out_ref[...] = acc

    return pl.pallas_call(
        body, name=name, in_specs=[VMEM_SPEC], out_specs=VMEM_SPEC, out_shape=jax.ShapeDtypeStruct((rows, 128), F32),
        scratch_shapes=[pltpu.VMEM((N_DEV, rows, 128), F32), pltpu.SemaphoreType.DMA((N_DEV - 1,)),
                        pltpu.SemaphoreType.DMA((N_DEV - 1,))],
        compiler_params=_cp())(buf)


SMALL = (("norm_mix_w", (D,)), ("ssd_conv_w", (4, XBC)), ("ssd_conv_b", (XBC,)), ("ssd_dt_bias", (NH,)),
         ("ssd_a_log", (NH,)), ("ssd_d", (NH,)), ("ssd_norm_w", (SSD_W,)), ("q_norm_w", (HD,)),
         ("k_norm_w", (HD,)), ("attn_sinks", (NQH,)), ("cm_dw_w", (CMK, CMC)), ("cm_dw_b", (CMC,)),
         ("cm_ln_w", (CMC,)), ("cm_ln_b", (CMC,)), ("norm_mlp_w", (D,)))
SHARDED_SMALL = ("ssd_conv_w", "cm_dw_w")


def _seg_len(shape):
    n = 1
    for d in shape:
        n *= d
    return -(-n // 128) * 128


def _pack_small(vals, names):
    parts = []
    for name, shape in SMALL:
        if name not in names:
            continue
        v = vals[name].reshape(DEPTH, -1)
        pad = _seg_len(shape) - v.shape[1]
        parts.append(jnp.pad(v, ((0, 0), (0, pad))))
    flat = jnp.concatenate(parts, axis=1)
    return flat.reshape(-1, 128)


def _unpack_small(buf, names):
    flat = buf.reshape(DEPTH, -1)
    out = {}
    off = 0
    for name, shape in SMALL:
        if name not in names:
            continue
        n = 1
        for d in shape:
            n *= d
        out[name] = flat[:, off:off + n].reshape((DEPTH,) + shape)
        off += _seg_len(shape)
    return out


SW = N_IN // 4
SWP = 1152
ORIG = (("z", 0, 1024), ("x", 1024, 2560), ("dt", 2560, 2576), ("q", 2576, 3088), ("k", 3088, 3216),
        ("v", 3216, 3344), ("a", 3344, 3856), ("g", 3856, 4368))


def _orig_cols(g_in, lo, hi):
    out = []
    for s in range(4):
        a, b = max(lo, s * SW), min(hi, (s + 1) * SW)
        if a < b:
            out.append(g_in[s][:, a - s * SW:b - s * SW])
    return out


def _shard_major(parts):
    cols = []
    for s in range(4):
        for name, g0, g1 in ORIG:
            a, b = max(g0, s * SW), min(g1, (s + 1) * SW)
            if a < b:
                cols.append(parts[name][:, a - g0:b - g0])
        cols.append(jnp.zeros((L, SWP - SW), BF16))
    return jnp.concatenate(cols, axis=1)


def _rope_tables():
    inv = 10000.0 ** (-jnp.arange(0, HD, 2, dtype=F32) / HD)
    ang = jnp.arange(L, dtype=F32)[:, None] * inv[None, :]
    return jnp.cos(ang), jnp.sin(ang)


def _swa_tables():
    cos, sin = _rope_tables()
    return jnp.tile(cos, (1, 4)), jnp.tile(jnp.concatenate([-sin, sin], axis=1), (1, 2))


def _swa_weights(w):
    return jnp.tile(w, 2)[None], jnp.tile(jnp.concatenate([w[HH:], w[:HH]]), 2)[None]


def _swa_sinks(s):
    s2 = s.reshape(NPAIR, 2)
    return (jnp.broadcast_to(s2[:, 0][:, None, None], (NPAIR, 1, 128)),
            jnp.broadcast_to(s2[:, 1][:, None, None], (NPAIR, 1, 128)))


def _swa_blockdiag():
    i = jnp.arange(128)
    return (i[:, None] // HD == i[None, :] // HD).astype(BF16)


def _pad128(v):
    return jnp.pad(v, (0, 128 - v.shape[0]))[None, :]


def kernel(x, norm_mix_w, w_in, ssd_conv_w, ssd_conv_b, ssd_dt_bias, ssd_a_log, ssd_d, ssd_norm_w, q_norm_w, k_norm_w, attn_sinks, cm_dw_w, cm_dw_b, cm_ln_w, cm_ln_b, w_out, norm_mlp_w, w_mlp_up, w_mlp_down, loss_target, m_norm_mix_w, m_w_in, m_ssd_conv_w, m_ssd_conv_b, m_ssd_dt_bias, m_ssd_a_log, m_ssd_d, m_ssd_norm_w, m_q_norm_w, m_k_norm_w, m_attn_sinks, m_cm_dw_w, m_cm_dw_b, m_cm_ln_w, m_cm_ln_b, m_w_out, m_norm_mlp_w, m_w_mlp_up, m_w_mlp_down, v_norm_mix_w, v_w_in, v_ssd_conv_w, v_ssd_conv_b, v_ssd_dt_bias, v_ssd_a_log, v_ssd_d, v_ssd_norm_w, v_q_norm_w, v_k_norm_w, v_attn_sinks, v_cm_dw_w, v_cm_dw_b, v_cm_ln_w, v_cm_ln_b, v_w_out, v_norm_mlp_w, v_w_mlp_up, v_w_mlp_down):
    px, py, pc = _my_pos()
    shard = 2 * px + py
    sidx = jnp.reshape(shard, (1,)).astype(jnp.int32)
    consts = _ssd_consts()
    cos4, sin4 = _swa_tables()
    bd = _swa_blockdiag()

    wb = [w.astype(BF16) for w in (jnp.pad(w_in, ((0, 0), (0, 0), (0, SWP - SW))), w_out, w_mlp_up, w_mlp_down)]
    zc = jnp.zeros((DEPTH, 4, XBC), F32)
    zc = lax.dynamic_update_slice_in_dim(zc, ssd_conv_w, shard * (XBC // 4), axis=2)
    zd = jnp.zeros((DEPTH, CMK, CMC), F32)
    zd = lax.dynamic_update_slice_in_dim(zd, cm_dw_w, shard * (CMC // 4), axis=2)
    half = jnp.where(pc == 0, 1.0, 0.0).astype(F32)
    gw = _allreduce_small(_pack_small({"ssd_conv_w": zc * half, "cm_dw_w": zd * half}, SHARDED_SMALL), "ag_small")
    gw = _unpack_small(gw, SHARDED_SMALL)
    conv_w_full, dw_w_full = gw["ssd_conv_w"], gw["cm_dw_w"]

    def gather_start(l, after):
        own = [t[l] for t in wb]
        lands = [lax.empty((4,) + t.shape, BF16) for t in own]
        return _split_start(own, lands, after, _ag_copies(False), "ag_start")

    xcur = x[0]
    saved = []
    zero_tile = jnp.zeros((8, 128), F32)
    in_flight = gather_start(0, zero_tile)
    for l in range(DEPTH):
        ssem, rsem, own, lands, _ = in_flight
        own, lands = _split_wait(ssem, rsem, own, lands, xcur, _ag_copies(True), "ag_wait")
        lands = _ag_forward(lands, "ag_forward")
        g_in, g_out, g_up, g_dn = [_place_own(own[w], g, sidx, "ag_place_%d" % w) for w, g in enumerate(lands)]
        nmw = norm_mix_w[l][None]
        if l + 1 < DEPTH:
            in_flight = gather_start(l + 1, g_dn)
            nmw = nmw + in_flight[4][0:1, 0:1]
        grp = dict((n, (a, b)) for n, a, b in ORIG)
        w_perm = jnp.concatenate(
            _orig_cols(g_in, *grp["x"]) + _orig_cols(g_in, *grp["z"]) + _orig_cols(g_in, grp["a"][0], grp["g"][1])
            + _orig_cols(g_in, grp["q"][0], grp["v"][1]) + _orig_cols(g_in, *grp["dt"])
            + [jnp.zeros((D, 128 - NH), BF16)], axis=1)
        g_out = g_out.reshape(2 * D, D)
        g_dn = g_dn.reshape(DFF, D)
        h = _rms_fwd(xcur, nmw, "rms_mix_fwd")
        u = _mm(h, w_perm, "nn", "in_proj", tn=640)
        alog = _pad128(ssd_a_log[l])
        dtb = _pad128(ssd_dt_bias[l])
        dxp = jnp.repeat(ssd_d[l], HP)[None, :]
        ssd_p = (conv_w_full[l], ssd_conv_b[l][None], dtb, alog, dxp, ssd_norm_w[l][None])
        ya, ypre, st = _ssd_fwd(u, *ssd_p, consts, "ssd_fwd")
        swa_p = (cos4, sin4, *_swa_weights(q_norm_w[l]), *_swa_weights(k_norm_w[l]), *_swa_sinks(attn_sinks[l]), bd)
        yb = _swa_fwd(u, *swa_p, "attn_fwd")
        conf_p = (dw_w_full[l], cm_dw_b[l][None], cm_ln_w[l][None], cm_ln_b[l][None])
        yc = _conf_fwd(u, *conf_p, "conf_fwd")
        ycat = jnp.concatenate([ya, yb, yc], axis=1)
        x1 = _mm(ycat, g_out, "nn", "out_proj", add=xcur)
        hm = _rms_fwd(x1, norm_mlp_w[l][None], "rms_mlp_fwd")
        r_up = _mm_up(hm, g_up, "mlp_up")
        x2 = _mm(r_up, g_dn, "nn", "mlp_down", add=x1)
        saved.append(dict(x=xcur, h=h, u=u, ypre=ypre, st=st, swa_p=swa_p, conf_p=conf_p, ycat=ycat, x1=x1,
                          hm=hm, r_up=r_up, ssd_p=ssd_p, g_in=g_in, g_out=g_out, g_up=g_up, g_dn=g_dn))
        xcur = x2

    lsum, dx, dxb = _loss_bwd(xcur, loss_target[0], "loss")

    gbig = [None] * DEPTH
    pending = None
    gsm = {name: [] for name, _ in SMALL}
    for l in reversed(range(DEPTH)):
        sv = saved[l]
        da = _mm(dxb, sv["g_dn"], "nt", "mlp_down_dx", relu2_of=sv["r_up"])
        dwdn, dwdn_b = _mm_dw(sv["r_up"], dxb, "mlp_down_dw")
        dwup, dwup_b = _mm_dw(sv["hm"], da, "mlp_up_dw", col_shards=True)
        dhm = _mm_cs_nt(da, sv["g_up"], "mlp_up_dx")
        dx1, dx1b, dnw = _rms_bwd(sv["x1"], norm_mlp_w[l][None], dhm, dx, "rms_mlp_bwd")
        gsm["norm_mlp_w"].append(dnw[0])
        dy = _mm(dx1b, sv["g_out"], "nt", "out_proj_dx")
        dwout, dwout_b = _mm_dw(sv["ycat"], dx1b, "out_proj_dw")
        da_c, dg_c, dww, dwb, dlw, dlb = _conf_bwd(sv["u"], *sv["conf_p"], dy, "conf_bwd")
        gsm["cm_dw_w"].append(dww[:CMK])
        gsm["cm_dw_b"].append(dwb[0])
        gsm["cm_ln_w"].append(dlw[0])
        gsm["cm_ln_b"].append(dlb[0])
        dq, dk, dv, dqw, dkw, dse, dso = _swa_bwd(sv["u"], dy, *sv["swa_p"], "attn_bwd")
        gsm["q_norm_w"].append(dqw[0, :HD] + dqw[0, HD:])
        gsm["k_norm_w"].append(dkw[0, :HD] + dkw[0, HD:])
        gsm["attn_sinks"].append(jnp.stack([dse[:, 0, 0], dso[:, 0, 0]], axis=1).reshape(NQH))
        (dz, dxr, ddtr, dcw, dcb, ddtb, dalog, ddd, dnsw) = _ssd_bwd(
            sv["u"], sv["ypre"], sv["st"], dy, *sv["ssd_p"], consts, "ssd_bwd")
        gsm["ssd_conv_w"].append(dcw[:4])
        gsm["ssd_conv_b"].append(dcb[0])
        gsm["ssd_dt_bias"].append(ddtb[0, :NH])
        gsm["ssd_a_log"].append(dalog[0, :NH])
        gsm["ssd_d"].append(ddd[0, :NH])
        gsm["ssd_norm_w"].append(dnsw[0])
        du = _shard_major(dict(z=dz, x=dxr, dt=ddtr[:, :NH], q=dq, k=dk, v=dv, a=da_c, g=dg_c))
        dwin, dwin_b = _mm_dw(sv["h"], du, "in_dw", col_shards=True, tn=SWP // 3)
        state, token = _rs_begin(
            [dwin, dwout.reshape(4, D // 2, D), dwup, dwdn.reshape(4, D, D)],
            [dwin_b, dwout_b.reshape(4, D // 2, D), dwup_b, dwdn_b.reshape(4, D, D)], zero_tile)
        dh = _mm_cs_nt(du, sv["g_in"], "in_dx")
        dx, dxb, dnm = _rms_bwd(sv["x"], norm_mix_w[l][None] + token[0:1, 0:1], dh, dx1, "rms_mix_bwd")
        gsm["norm_mix_w"].append(dnm[0])
        if pending is not None:
            gbig[l + 1] = _rs_end(pending, dx)
        pending = state
    gbig[0] = _rs_end(pending, dx)

    gsm = {k: jnp.stack(v[::-1]) for k, v in gsm.items()}
    packed = _pack_small(gsm, [n for n, _ in SMALL])
    packed = jnp.concatenate([packed, lsum], axis=0)
    red = _allreduce_small(packed, "ar_small")
    loss = 0.5 * red[-8, 0] / D
    gsm = _unpack_small(red[:-8], [n for n, _ in SMALL])
    gsm["ssd_conv_w"] = lax.dynamic_slice_in_dim(gsm["ssd_conv_w"], shard * (XBC // 4), XBC // 4, axis=2)
    gsm["cm_dw_w"] = lax.dynamic_slice_in_dim(gsm["cm_dw_w"], shard * (CMC // 4), CMC // 4, axis=2)
    grads = dict(gsm)
    for w, n in enumerate(("w_in", "w_out", "w_mlp_up", "w_mlp_down")):
        grads[n] = jnp.stack([gbig[l][w] for l in range(DEPTH)])
    grads["w_in"] = grads["w_in"][:, :, :SW]

    loc = locals()
    names = ["norm_mix_w", "w_in", "ssd_conv_w", "ssd_conv_b", "ssd_dt_bias", "ssd_a_log", "ssd_d", "ssd_norm_w",
             "q_norm_w", "k_norm_w", "attn_sinks", "cm_dw_w", "cm_dw_b", "cm_ln_w", "cm_ln_b", "w_out", "norm_mlp_w",
             "w_mlp_up", "w_mlp_down"]
    weights = {n: loc[n] for n in names}
    moms = {n: loc["m_" + n] for n in names}
    vars_ = {n: loc["v_" + n] for n in names}
    delta, new_m, new_v = {}, {}, {}
    packed_names = [n for n, _ in SMALL if n not in SHARDED_SMALL]
    pw = _pack_small(weights, packed_names)
    pg = _pack_small(grads, packed_names)
    pm = _pack_small(moms, packed_names)
    pv = _pack_small(vars_, packed_names)
    pd, pmn, pvn = _adamw(pw, pg, pm, pv, "adamw_small")
    for dst, buf in ((delta, pd), (new_m, pmn), (new_v, pvn)):
        dst.update(_unpack_small(buf, packed_names))
    for n in ("w_in", "w_out", "w_mlp_up", "w_mlp_down", "ssd_conv_w", "cm_dw_w"):
        shp = weights[n].shape
        flat = lambda t: t.reshape(-1, shp[-1])
        d_, m_, v_ = _adamw(flat(weights[n]), flat(grads[n]), flat(moms[n]), flat(vars_[n]), "adamw_" + n)
        delta[n], new_m[n], new_v[n] = d_.reshape(shp), m_.reshape(shp), v_.reshape(shp)

    return (loss, dx[None], *[grads[n] for n in names], *[delta[n] for n in names],
            *[new_m[n] for n in names], *[new_v[n] for n in names])
```

```python
import functools
import math

import jax
import jax.numpy as jnp
from jax import lax
from jax.experimental import pallas as pl
from jax.experimental.pallas import tpu as pltpu

F32 = jnp.float32
BF16 = jnp.bfloat16
MESH = pl.DeviceIdType.MESH
ANY = pl.BlockSpec(memory_space=pl.ANY)
VMEM_SPEC = pl.BlockSpec(memory_space=pltpu.VMEM)

D = 1024
L = 2048
DEPTH = 4
SSD_W = 1024
XBC = 1536
NH = 16
HP = 64
NS = 128
Q = 128
NC = L // Q
ATT_W = 512
NQH = 8
NKV = 2
HD = 64
HH = HD // 2
CMC = 512
CMK = 31
DFF = 4096
N_IN = 4368
N_PAD = 4480
RMS_EPS = 1e-6
LN_EPS = 1e-5
NEG = -1e30
LR, B1, B2, EPS_A, WD, STEP = 0.001, 0.9, 0.999, 1e-8, 0.01, 10
VMEM_LIMIT = 56 * 1024 * 1024
N_DEV = 8


def _cp(sem=None):
    kw = dict(vmem_limit_bytes=VMEM_LIMIT)
    if sem is not None:
        kw["dimension_semantics"] = sem
    return pltpu.CompilerParams(**kw)


def _dg(a, b, ca, cb):
    return lax.dot_general(a, b, (((ca,), (cb,)), ((), ())), preferred_element_type=F32)


def _split3(x):
    hi = x.astype(BF16)
    r = x - hi.astype(F32)
    mid = r.astype(BF16)
    lo = (r - mid.astype(F32)).astype(BF16)
    return hi, mid, lo


def _xdot_l(x, m, ca=1, cb=0):
    hi, mid, lo = _split3(x)
    return _dg(hi, m, ca, cb) + _dg(mid, m, ca, cb) + _dg(lo, m, ca, cb)


def _xdot_r(m, x, ca=1, cb=0):
    hi, mid, lo = _split3(x)
    return _dg(m, hi, ca, cb) + _dg(m, mid, ca, cb) + _dg(m, lo, ca, cb)


def _rowdot(v, m):
    return _xdot_l(jnp.broadcast_to(v, (8, v.shape[1])), m)[0:1]


def _sigmoid(x):
    return 1.0 / (1.0 + jnp.exp(-x))


def _softplus(x):
    e = jnp.exp(-jnp.abs(x))
    u = 1.0 + e
    l1p = jnp.where(u == 1.0, e, jnp.log(u) * (e / jnp.where(u == 1.0, 1.0, u - 1.0)))
    return jnp.maximum(x, 0.0) + l1p


def _tm(k):
    return L if k <= D else L // 2


def _mm(a, b, mode, name, add=None, relu2_of=None, tn=512):
    m, k = a.shape
    tm = min(m, _tm(k))
    a_spec = pl.BlockSpec((tm, k), lambda i, j: (i, 0))
    if mode == "nn":
        n = b.shape[1]
        b_spec = pl.BlockSpec((k, tn), lambda i, j: (0, j))
        cb = 0
    else:
        n = b.shape[0]
        b_spec = pl.BlockSpec((tn, k), lambda i, j: (j, 0))
        cb = 1
    assert m % tm == 0 and n % tn == 0, (m, n, tm, tn)
    o_spec = pl.BlockSpec((tm, tn), lambda i, j: (i, j))
    out_dtype = F32
    if relu2_of is not None:
        def body(a_ref, b_ref, c_ref, o_ref):
            o_ref[...] = (_dg(a_ref[...], b_ref[...], 1, cb) * (2.0 * jnp.sqrt(c_ref[...].astype(F32)))).astype(BF16)
        ins, specs, out_dtype = (a, b, relu2_of), [a_spec, b_spec, o_spec], BF16
    elif add is None:
        def body(a_ref, b_ref, o_ref):
            o_ref[...] = _dg(a_ref[...], b_ref[...], 1, cb)
        ins, specs = (a, b), [a_spec, b_spec]
    else:
        def body(a_ref, b_ref, c_ref, o_ref):
            o_ref[...] = _dg(a_ref[...], b_ref[...], 1, cb) + c_ref[...]
        ins, specs = (a, b, add), [a_spec, b_spec, o_spec]
    return pl.pallas_call(
        body, name=name, grid=(m // tm, n // tn), in_specs=specs, out_specs=o_spec,
        out_shape=jax.ShapeDtypeStruct((m, n), out_dtype), compiler_params=_cp(("parallel", "parallel")))(*ins)


def _mm_up(a, b, name, tn=512):
    m = a.shape[0]
    cs = DFF // 4
    per = cs // tn
    tm = min(m, _tm(D))

    def body(a_ref, b_ref, r_ref):
        r = jnp.maximum(_dg(a_ref[...], b_ref[0], 1, 0), 0.0)
        r_ref[...] = (r * r).astype(BF16)

    return pl.pallas_call(
        body, name=name, grid=(m // tm, DFF // tn),
        in_specs=[pl.BlockSpec((tm, D), lambda i, j: (i, 0)),
                  pl.BlockSpec((1, D, tn), lambda i, j: (j // per, 0, j % per))],
        out_specs=pl.BlockSpec((tm, tn), lambda i, j: (i, j)),
        out_shape=jax.ShapeDtypeStruct((m, DFF), BF16), compiler_params=_cp(("parallel", "parallel")))(a, b)


def _mm_cs_nt(a, b, name, tn=512):
    m = a.shape[0]
    _, n, cs = b.shape
    tm = min(m, _tm(4 * cs))

    def body(a_ref, b_ref, o_ref):
        acc = _dg(a_ref[:, 0:cs], b_ref[0], 1, 1)
        for s in range(1, 4):
            acc = acc + _dg(a_ref[:, s * cs:(s + 1) * cs], b_ref[s], 1, 1)
        o_ref[...] = acc

    return pl.pallas_call(
        body, name=name, grid=(m // tm, n // tn),
        in_specs=[pl.BlockSpec((tm, 4 * cs), lambda i, j: (i, 0)), pl.BlockSpec((4, tn, cs), lambda i, j: (0, j, 0))],
        out_specs=pl.BlockSpec((tm, tn), lambda i, j: (i, j)),
        out_shape=jax.ShapeDtypeStruct((m, n), F32), compiler_params=_cp(("parallel", "parallel")))(a, b)


def _mm_dw(a, b, name, col_shards=False, tn=512):
    k, m = a.shape
    n = b.shape[1]
    tm = min(m, D)
    a_spec = pl.BlockSpec((k, tm), lambda i, j: (0, i))
    b_spec = pl.BlockSpec((k, tn), lambda i, j: (0, j))
    if col_shards:
        per = (n // 4) // tn
        o_spec = pl.BlockSpec((1, tm, tn), lambda i, j: (j // per, i, j % per))
        shape = (4, m, n // 4)
    else:
        o_spec = pl.BlockSpec((tm, tn), lambda i, j: (i, j))
        shape = (m, n)

    def body(a_ref, b_ref, o_ref, ob_ref):
        acc = _dg(a_ref[...], b_ref[...], 0, 0).reshape(o_ref.shape)
        o_ref[...] = acc
        ob_ref[...] = acc.astype(BF16)

    return pl.pallas_call(
        body, name=name, grid=(m // tm, n // tn), in_specs=[a_spec, b_spec], out_specs=[o_spec, o_spec],
        out_shape=[jax.ShapeDtypeStruct(shape, F32), jax.ShapeDtypeStruct(shape, BF16)],
        compiler_params=_cp(("parallel", "parallel")))(a, b)


TR = 256


def _rms_fwd(x, w, name):
    def body(x_ref, w_ref, o_ref):
        xv = x_ref[...]
        r = lax.rsqrt(jnp.mean(xv * xv, axis=-1, keepdims=True) + RMS_EPS)
        o_ref[...] = (xv * r * w_ref[...]).astype(BF16)

    return pl.pallas_call(
        body, name=name, grid=(L // TR,),
        in_specs=[pl.BlockSpec((TR, D), lambda i: (i, 0)), pl.BlockSpec((1, D), lambda i: (0, 0))],
        out_specs=pl.BlockSpec((TR, D), lambda i: (i, 0)),
        out_shape=jax.ShapeDtypeStruct((L, D), BF16), compiler_params=_cp(("parallel",)))(x, w)


def _rms_bwd(x, w, dh, dres, name):
    def body(x_ref, w_ref, dh_ref, dr_ref, dx_ref, dxb_ref, dw_ref):
        xv = x_ref[...]
        r = lax.rsqrt(jnp.mean(xv * xv, axis=-1, keepdims=True) + RMS_EPS)
        n = xv * r
        dhv = dh_ref[...]
        g = dhv * w_ref[...]
        dx = dr_ref[...] + r * (g - n * jnp.mean(g * n, axis=-1, keepdims=True))
        dx_ref[...] = dx
        dxb_ref[...] = dx.astype(BF16)

        @pl.when(pl.program_id(0) == 0)
        def _():
            dw_ref[...] = jnp.zeros_like(dw_ref)
        dw_ref[...] += jnp.sum(dhv * n, axis=0, keepdims=True)

    row = pl.BlockSpec((TR, D), lambda i: (i, 0))
    vec = pl.BlockSpec((1, D), lambda i: (0, 0))
    return pl.pallas_call(
        body, name=name, grid=(L // TR,), in_specs=[row, vec, row, row], out_specs=[row, row, vec],
        out_shape=[jax.ShapeDtypeStruct((L, D), F32), jax.ShapeDtypeStruct((L, D), BF16),
                   jax.ShapeDtypeStruct((1, D), F32)],
        compiler_params=_cp(("arbitrary",)))(x, w, dh, dres)


def _adamw_layer(w, g, m, v, layer, prev, name):
    _, rows, cols = w.shape
    tr = 256 if cols * 256 * 4 <= 2 * 1024 * 1024 else 128
    c1 = 1.0 / (1.0 - B1 ** STEP)
    c2 = 1.0 / (1.0 - B2 ** STEP)
    n_prev = 0 if prev is None else 4

    def body(*refs):
        w_ref, g_ref, m_ref, v_ref = refs[:4]
        go_ref, d_ref, mo_ref, vo_ref = refs[4 + n_prev:]
        gv = g_ref[...]
        mn = B1 * m_ref[0] + (1.0 - B1) * gv
        vn = B2 * v_ref[0] + (1.0 - B2) * (gv * gv)
        go_ref[0] = gv
        mo_ref[0] = mn
        vo_ref[0] = vn
        d_ref[0] = -LR * ((mn * c1) / (jnp.sqrt(vn * c2) + EPS_A) + WD * w_ref[0])

    lay = pl.BlockSpec((1, tr, cols), lambda i: (layer, i, 0))
    shp = jax.ShapeDtypeStruct(w.shape, F32)
    return pl.pallas_call(
        body, name=name, grid=(rows // tr,),
        in_specs=[lay, pl.BlockSpec((tr, cols), lambda i: (i, 0)), lay, lay] + [ANY] * n_prev,
        out_specs=[lay] * 4, out_shape=[shp] * 4,
        input_output_aliases={4 + i: i for i in range(n_prev)},
        compiler_params=_cp(("parallel",)))(w, g, m, v, *(prev or ()))


def _loss_bwd(y, t, name):
    def body(y_ref, t_ref, l_ref, d_ref, db_ref):
        e = y_ref[...] - t_ref[...]
        d = e * (1.0 / D)
        d_ref[...] = d
        db_ref[...] = d.astype(BF16)

        @pl.when(pl.program_id(0) == 0)
        def _():
            l_ref[...] = jnp.zeros_like(l_ref)
        s = jnp.sum(jnp.sum(e * e, axis=-1, keepdims=True), axis=0, keepdims=True)
        l_ref[...] += jnp.broadcast_to(s, l_ref.shape)

    row = pl.BlockSpec((TR, D), lambda i: (i, 0))
    tile = pl.BlockSpec((8, 128), lambda i: (0, 0))
    return pl.pallas_call(
        body, name=name, grid=(L // TR,), in_specs=[row, row], out_specs=[tile, row, row],
        out_shape=[jax.ShapeDtypeStruct((8, 128), F32), jax.ShapeDtypeStruct((L, D), F32),
                   jax.ShapeDtypeStruct((L, D), BF16)],
        compiler_params=_cp(("arbitrary",)))(y, t)


def _adamw(w, g, m, v, name):
    rows, cols = w.shape
    tr = rows
    for cand in (512, 256, 128, 64, 32, 16, 8):
        if rows % cand == 0 and cand * cols * 4 <= 2 * 1024 * 1024:
            tr = cand
            break
    c1 = 1.0 / (1.0 - B1 ** STEP)
    c2 = 1.0 / (1.0 - B2 ** STEP)

    def body(w_ref, g_ref, m_ref, v_ref, d_ref, mo_ref, vo_ref):
        gv = g_ref[...]
        mn = B1 * m_ref[...] + (1.0 - B1) * gv
        vn = B2 * v_ref[...] + (1.0 - B2) * (gv * gv)
        mo_ref[...] = mn
        vo_ref[...] = vn
        d_ref[...] = -LR * ((mn * c1) / (jnp.sqrt(vn * c2) + EPS_A) + WD * w_ref[...])

    blk = pl.BlockSpec((tr, cols), lambda i: (i, 0))
    shp = jax.ShapeDtypeStruct((rows, cols), F32)
    return pl.pallas_call(body, name=name, grid=(rows // tr,), in_specs=[blk] * 4, out_specs=[blk] * 3,
                          out_shape=[shp] * 3, compiler_params=_cp(("parallel",)))(w, g, m, v)


CT = 256
CPAD = 32


U_X, U_Z, U_A, U_G, U_Q, U_K, U_V, U_DT = 0, 1536, 2560, 3072, 3584, 4096, 4224, 4352


CEXT = 8
CWIN = CT + CPAD
CROWS = L + CPAD + CEXT


def _fill_shifted(src_ref, base, win_ref, sh_ref):
    win_ref[...] = src_ref[pl.ds(base, CWIN + CEXT), :]
    for p in range(8):
        sh_ref[p] = win_ref[pl.ds(p, CWIN), :]


def _tap(sh_ref, o):
    return sh_ref[o % 8, 8 * (o // 8):8 * (o // 8) + CT, :]


CB = 128


def _conv_specs():
    return [pl.BlockSpec((L, CB), lambda j: (0, U_A // CB + j)), pl.BlockSpec((L, CB), lambda j: (0, U_G // CB + j))]


def _conv_scratch(n_padded):
    return ([pltpu.VMEM((CROWS, CB), F32)] * n_padded
            + [pltpu.VMEM((CWIN + CEXT, CB), F32), pltpu.VMEM((8, CWIN, CB), F32)])


def _fill_gated(a_ref, g_ref, hp_ref):
    hp_ref[0:CPAD, :] = jnp.zeros((CPAD, CB), F32)
    hp_ref[CPAD:CPAD + L, :] = a_ref[...] * _sigmoid(g_ref[...])
    hp_ref[CPAD + L:, :] = jnp.zeros((CEXT, CB), F32)


def _conf_conv_fwd(u, w, b, name):
    def body(a_ref, g_ref, w_ref, b_ref, c_ref, hp_ref, win_ref, sh_ref):
        _fill_gated(a_ref, g_ref, hp_ref)

        def tile(i, carry):
            base = pl.multiple_of(i * CT, CT)
            _fill_shifted(hp_ref, base, win_ref, sh_ref)
            c = jnp.broadcast_to(b_ref[...], (CT, CB))
            for k in range(CMK):
                c = c + w_ref[k:k + 1, :] * _tap(sh_ref, 2 + k)
            c_ref[pl.ds(base, CT), :] = c
            return carry

        lax.fori_loop(0, L // CT, tile, 0)

    return pl.pallas_call(
        body, name=name, grid=(CMC // CB,),
        in_specs=_conv_specs() + [pl.BlockSpec((CMK, CB), lambda j: (0, j)), pl.BlockSpec((1, CB), lambda j: (0, j))],
        out_specs=pl.BlockSpec((L, CB), lambda j: (0, j)),
        out_shape=jax.ShapeDtypeStruct((L, CMC), F32), scratch_shapes=_conv_scratch(1),
        compiler_params=_cp(("parallel",)))(u, u, w, b)


def _conf_ln_fwd(c, lw, lb, name):
    def body(c_ref, lw_ref, lb_ref, o_ref):
        cv = c_ref[...]
        cc = cv - jnp.mean(cv, axis=-1, keepdims=True)
        var = jnp.mean(cc * cc, axis=-1, keepdims=True)
        l = cc * lax.rsqrt(var + LN_EPS) * lw_ref[...] + lb_ref[...]
        o_ref[...] = (l * _sigmoid(l)).astype(BF16)

    row = pl.BlockSpec((TR, CMC), lambda i: (i, 0))
    vec = pl.BlockSpec((1, CMC), lambda i: (0, 0))
    return pl.pallas_call(body, name=name, grid=(L // TR,), in_specs=[row, vec, vec], out_specs=row,
                          out_shape=jax.ShapeDtypeStruct((L, CMC), BF16),
                          compiler_params=_cp(("parallel",)))(c, lw, lb)


def _conf_ln_bwd(c, lw, lb, dy, name):
    def body(c_ref, lw_ref, lb_ref, dy_ref, dc_ref, db_ref, dlw_ref, dlb_ref):
        cv = c_ref[...]
        cc = cv - jnp.mean(cv, axis=-1, keepdims=True)
        var = jnp.mean(cc * cc, axis=-1, keepdims=True)
        rstd = lax.rsqrt(var + LN_EPS)
        n = cc * rstd
        l = n * lw_ref[...] + lb_ref[...]
        sl = _sigmoid(l)
        dl = dy_ref[...] * (sl * (1.0 + l * (1.0 - sl)))
        dn = dl * lw_ref[...]
        dc = rstd * (dn - jnp.mean(dn, axis=-1, keepdims=True) - n * jnp.mean(dn * n, axis=-1, keepdims=True))
        dc_ref[...] = dc

        @pl.when(pl.program_id(0) == 0)
        def _():
            db_ref[...] = jnp.zeros_like(db_ref)
            dlw_ref[...] = jnp.zeros_like(dlw_ref)
            dlb_ref[...] = jnp.zeros_like(dlb_ref)
        db_ref[...] += jnp.sum(dc, axis=0, keepdims=True)
        dlw_ref[...] += jnp.sum(dl * n, axis=0, keepdims=True)
        dlb_ref[...] += jnp.sum(dl, axis=0, keepdims=True)

    row = pl.BlockSpec((TR, CMC), lambda i: (i, 0))
    vec = pl.BlockSpec((1, CMC), lambda i: (0, 0))
    vshape = jax.ShapeDtypeStruct((1, CMC), F32)
    return pl.pallas_call(
        body, name=name, grid=(L // TR,),
        in_specs=[row, vec, vec, pl.BlockSpec((TR, CMC), lambda i: (i, (SSD_W + ATT_W) // CMC))],
        out_specs=[row, vec, vec, vec], out_shape=[jax.ShapeDtypeStruct((L, CMC), F32), vshape, vshape, vshape],
        compiler_params=_cp(("arbitrary",)))(c, lw, lb, dy)


def _conf_conv_bwd(u, dc, w, name):
    def body(a_ref, g_ref, dc_ref, w_ref, da_ref, dg_ref, dw_ref, hp_ref, dcp_ref, win_ref, sh_ref, dwacc_ref):
        _fill_gated(a_ref, g_ref, hp_ref)
        dcp_ref[0:L, :] = dc_ref[...]
        dcp_ref[L:, :] = jnp.zeros((CPAD + CEXT, CB), F32)
        dwacc_ref[...] = jnp.zeros_like(dwacc_ref)

        def tile(i, carry):
            base = pl.multiple_of(i * CT, CT)
            _fill_shifted(hp_ref, base, win_ref, sh_ref)
            dcv = dcp_ref[pl.ds(base, CT), :]
            for k in range(CMK):
                dwacc_ref[k] += (dcv * _tap(sh_ref, 2 + k)).reshape(CT // 8, 8, CB).sum(axis=0)
            _fill_shifted(dcp_ref, base, win_ref, sh_ref)
            dh = jnp.zeros((CT, CB), F32)
            for k in range(CMK):
                dh = dh + w_ref[k:k + 1, :] * _tap(sh_ref, CMK - 1 - k)
            av = a_ref[pl.ds(base, CT), :]
            sg = _sigmoid(g_ref[pl.ds(base, CT), :])
            da_ref[pl.ds(base, CT), :] = (dh * sg).astype(BF16)
            dg_ref[pl.ds(base, CT), :] = (dh * av * sg * (1.0 - sg)).astype(BF16)
            return carry

        lax.fori_loop(0, L // CT, tile, 0)
        for k in range(CMK):
            dw_ref[k:k + 1, :] = jnp.sum(dwacc_ref[k], axis=0, keepdims=True)
        dw_ref[CMK:, :] = jnp.zeros((32 - CMK, CB), F32)

    col = pl.BlockSpec((L, CB), lambda j: (0, j))
    return pl.pallas_call(
        body, name=name, grid=(CMC // CB,),
        in_specs=_conv_specs() + [col, pl.BlockSpec((CMK, CB), lambda j: (0, j))],
        out_specs=[col, col, pl.BlockSpec((32, CB), lambda j: (0, j))],
        out_shape=[jax.ShapeDtypeStruct((L, CMC), BF16), jax.ShapeDtypeStruct((L, CMC), BF16),
                   jax.ShapeDtypeStruct((32, CMC), F32)],
        scratch_shapes=_conv_scratch(2) + [pltpu.VMEM((32, 8, CB), F32)],
        compiler_params=_cp(("parallel",)))(u, u, dc, w)


NPAIR = NQH // 2


def _partner(x, lo32):
    return jnp.where(lo32, pltpu.roll(x, 96, 1), pltpu.roll(x, 32, 1))


def _swa_prep(x, w2, w2p, c4, s4, bd, lo32):
    r = lax.rsqrt(_xdot_l(x * x, bd) * (1.0 / HD) + RMS_EPS)
    xh = x * r
    return r, xh, xh * w2 * c4 + _partner(xh, lo32) * w2p * s4


def _swa_unprep(dr, r, xh, w2, w2p, c4, s4, bd, lo32):
    dn = dr * c4
    dnp = dr * s4
    gx = dn * w2 + _partner(dnp * w2p, lo32)
    dw = jnp.sum((dn + _partner(dnp, lo32)) * xh, axis=0, keepdims=True)
    mu = _xdot_l(gx * xh, bd) * (1.0 / HD)
    return r * (gx - xh * mu), dw


def _swa_softmax(s, sink):
    row = lax.broadcasted_iota(jnp.int32, (L, 2 * Q), 0)
    col = lax.broadcasted_iota(jnp.int32, (L, 2 * Q), 1)
    rm = row & (Q - 1)
    valid = (col > rm) & (col <= rm + Q) & ((row >= Q) | (col >= Q))
    s = jnp.where(valid, s * (1.0 / math.sqrt(HD)), NEG)
    m = jnp.maximum(jnp.max(s, axis=-1, keepdims=True), sink)
    p = jnp.exp(s - m)
    ps = jnp.exp(sink - m)
    inv = 1.0 / (jnp.sum(p, axis=-1, keepdims=True) + ps)
    return p * inv, ps * inv


def _swa_in_specs():
    tab = pl.BlockSpec((L, 128), lambda p: (0, 0))
    wv = pl.BlockSpec((1, 128), lambda p: (0, 0))
    sk = pl.BlockSpec((1, 1, 128), lambda p: (p, 0, 0))
    return [pl.BlockSpec((L, 128), lambda p: (0, U_Q // 128 + p)), pl.BlockSpec((L, 128), lambda p: (0, U_K // 128)),
            pl.BlockSpec((L, 128), lambda p: (0, U_V // 128)), tab, tab, wv, wv, wv, wv, sk, sk,
            pl.BlockSpec((128, 128), lambda p: (0, 0))]


def _swa_setup(q_ref, k_ref, v_ref, c_ref, s_ref, qw_ref, qwp_ref, kw_ref, kwp_ref, bd_ref, kpad, vpad):
    g = pl.program_id(0) // 2
    lane = lax.broadcasted_iota(jnp.int32, (L, 128), 1)
    lo32 = (lane & 32) == 0
    own = (lane >> 6) == g
    c4, s4, bd = c_ref[...], s_ref[...], bd_ref[...]
    qn = _swa_prep(q_ref[...], qw_ref[...], qwp_ref[...], c4, s4, bd, lo32)
    kn = _swa_prep(k_ref[...], kw_ref[...], kwp_ref[...], c4, s4, bd, lo32)
    vv = v_ref[...]
    kpad[0:Q, :] = jnp.zeros((Q, 128), BF16)
    vpad[0:Q, :] = jnp.zeros((Q, 128), BF16)
    kpad[Q:, :] = jnp.where(own, kn[2], pltpu.roll(kn[2], HD, 1)).astype(BF16)
    vpad[Q:, :] = jnp.where(own, vv, pltpu.roll(vv, HD, 1)).astype(BF16)
    return qn, kn, lo32, own, c4, s4, bd


def _swa_fwd(u, cos4, sin4, qw2, qw2p, kw2, kw2p, sink_e, sink_o, bd, name):
    def body(q_ref, k_ref, v_ref, c_ref, s_ref, qw_ref, qwp_ref, kw_ref, kwp_ref, ske_ref, sko_ref, bd_ref,
             o_ref, kpad, vpad, s_scr, p_scr):
        qn, _, _, _, _, _, _ = _swa_setup(q_ref, k_ref, v_ref, c_ref, s_ref, qw_ref, qwp_ref, kw_ref, kwp_ref,
                                          bd_ref, kpad, vpad)
        qr = qn[2]
        first = lax.broadcasted_iota(jnp.int32, (Q, 128), 1) < HD
        for n in range(NC):
            rows = slice(n * Q, (n + 1) * Q)
            kc = kpad[n * Q:(n + 2) * Q, :]
            s_scr[0, rows, :] = _dg(jnp.where(first, qr[rows], 0.0).astype(BF16), kc, 1, 1)
            s_scr[1, rows, :] = _dg(jnp.where(first, 0.0, qr[rows]).astype(BF16), kc, 1, 1)
        for h, sk_ref in ((0, ske_ref), (1, sko_ref)):
            p, _ = _swa_softmax(s_scr[h], sk_ref[0][:, 0:1])
            p_scr[h] = p.astype(BF16)
        for n in range(NC):
            rows = slice(n * Q, (n + 1) * Q)
            vc = vpad[n * Q:(n + 2) * Q, :]
            o_ref[rows, :] = jnp.where(first, _dg(p_scr[0, rows, :], vc, 1, 0),
                                       _dg(p_scr[1, rows, :], vc, 1, 0)).astype(BF16)

    return pl.pallas_call(
        body, name=name, grid=(NPAIR,), in_specs=_swa_in_specs(),
        out_specs=pl.BlockSpec((L, 128), lambda p: (0, p)),
        out_shape=jax.ShapeDtypeStruct((L, ATT_W), BF16),
        scratch_shapes=[pltpu.VMEM((L + Q, 128), BF16), pltpu.VMEM((L + Q, 128), BF16),
                        pltpu.VMEM((2, L, 2 * Q), F32), pltpu.VMEM((2, L, 2 * Q), BF16)],
        compiler_params=_cp(("arbitrary",)))(u, u, u, cos4, sin4, qw2, qw2p, kw2, kw2p, sink_e, sink_o, bd)


def _swa_bwd(u, dy, cos4, sin4, qw2, qw2p, kw2, kw2p, sink_e, sink_o, bd, name):
    def body(q_ref, k_ref, v_ref, c_ref, s_ref, qw_ref, qwp_ref, kw_ref, kwp_ref, ske_ref, sko_ref, bd_ref, do_ref,
             dq_ref, dk_ref, dv_ref, dqw_ref, dkw_ref, dse_ref, dso_ref,
             kpad, vpad, s_scr, dp_scr, ds_scr, pb_scr, dkr_acc, dv_acc, dqr_scr):
        pidx = pl.program_id(0)

        @pl.when(pidx == 0)
        def _():
            dkr_acc[...] = jnp.zeros_like(dkr_acc)
            dv_acc[...] = jnp.zeros_like(dv_acc)
            dqw_ref[...] = jnp.zeros_like(dqw_ref)

        qn, kn, lo32, own, c4, s4, bd = _swa_setup(q_ref, k_ref, v_ref, c_ref, s_ref, qw_ref, qwp_ref, kw_ref,
                                                   kwp_ref, bd_ref, kpad, vpad)
        qr = qn[2]
        lane_q = lax.broadcasted_iota(jnp.int32, (Q, 128), 1)
        first = lane_q < HD
        own_q = (lane_q >> 6) == pidx // 2

        def halves(t):
            return jnp.where(first, t, 0.0).astype(BF16), jnp.where(first, 0.0, t).astype(BF16)

        for n in range(NC):
            rows = slice(n * Q, (n + 1) * Q)
            kc = kpad[n * Q:(n + 2) * Q, :]
            vc = vpad[n * Q:(n + 2) * Q, :]
            qm = halves(qr[rows])
            dom = halves(do_ref[rows, :])
            for h in range(2):
                s_scr[h, rows, :] = _dg(qm[h], kc, 1, 1)
                dp_scr[h, rows, :] = _dg(dom[h], vc, 1, 1)
        for h, sk_ref, dsk_ref in ((0, ske_ref, dse_ref), (1, sko_ref, dso_ref)):
            p, ps = _swa_softmax(s_scr[h], sk_ref[0][:, 0:1])
            dp = dp_scr[h]
            delta = jnp.sum(p * dp, axis=-1, keepdims=True)
            dsk_ref[0] = jnp.broadcast_to(-jnp.sum(ps * delta, axis=0, keepdims=True), (1, 128))
            ds_scr[h] = (p * (dp - delta) * (1.0 / math.sqrt(HD))).astype(BF16)
            pb_scr[h] = p.astype(BF16)
        for n in range(NC):
            rows = slice(n * Q, (n + 1) * Q)
            kc = kpad[n * Q:(n + 2) * Q, :]
            dqr_scr[rows, :] = jnp.where(first, _dg(ds_scr[0, rows, :], kc, 1, 0), _dg(ds_scr[1, rows, :], kc, 1, 0))
        for m in range(NC):
            acc_k = jnp.zeros((Q, 128), F32)
            acc_v = jnp.zeros((Q, 128), F32)
            for n, cols in ((m, slice(Q, 2 * Q)), (m + 1, slice(0, Q))):
                if n >= NC:
                    continue
                rows = slice(n * Q, (n + 1) * Q)
                qm = halves(qr[rows])
                dom = halves(do_ref[rows, :])
                for h in range(2):
                    acc_k = acc_k + _dg(ds_scr[h, rows, cols], qm[h], 0, 0)
                    acc_v = acc_v + _dg(pb_scr[h, rows, cols], dom[h], 0, 0)
            rows = slice(m * Q, (m + 1) * Q)
            dkr_acc[rows, :] += jnp.where(own_q, acc_k + pltpu.roll(acc_k, HD, 1), 0.0)
            dv_acc[rows, :] += jnp.where(own_q, acc_v + pltpu.roll(acc_v, HD, 1), 0.0)
        dq, dqw = _swa_unprep(dqr_scr[...], qn[0], qn[1], qw_ref[...], qwp_ref[...], c4, s4, bd, lo32)
        dq_ref[...] = dq.astype(BF16)
        dqw_ref[...] += dqw

        @pl.when(pidx == NPAIR - 1)
        def _():
            dk, dkw = _swa_unprep(dkr_acc[...], kn[0], kn[1], kw_ref[...], kwp_ref[...], c4, s4, bd, lo32)
            dk_ref[...] = dk.astype(BF16)
            dkw_ref[...] = dkw
            dv_ref[...] = dv_acc[...].astype(BF16)

    full = pl.BlockSpec((L, 128), lambda p: (0, 0))
    wv = pl.BlockSpec((1, 128), lambda p: (0, 0))
    sk = pl.BlockSpec((1, 1, 128), lambda p: (p, 0, 0))
    vec = jax.ShapeDtypeStruct((1, 128), F32)
    skv = jax.ShapeDtypeStruct((NPAIR, 1, 128), F32)
    return pl.pallas_call(
        body, name=name, grid=(NPAIR,),
        in_specs=_swa_in_specs() + [pl.BlockSpec((L, 128), lambda p: (0, SSD_W // 128 + p))],
        out_specs=[pl.BlockSpec((L, 128), lambda p: (0, p)), full, full, wv, wv, sk, sk],
        out_shape=[jax.ShapeDtypeStruct((L, ATT_W), BF16), jax.ShapeDtypeStruct((L, 128), BF16),
                   jax.ShapeDtypeStruct((L, 128), BF16), vec, vec, skv, skv],
        scratch_shapes=[pltpu.VMEM((L + Q, 128), BF16), pltpu.VMEM((L + Q, 128), BF16),
                        pltpu.VMEM((2, L, 2 * Q), F32), pltpu.VMEM((2, L, 2 * Q), F32),
                        pltpu.VMEM((2, L, 2 * Q), BF16), pltpu.VMEM((2, L, 2 * Q), BF16),
                        pltpu.VMEM((L, 128), F32), pltpu.VMEM((L, 128), F32), pltpu.VMEM((L, 128), F32)],
        compiler_params=_cp(("arbitrary",)))(u, u, u, cos4, sin4, qw2, qw2p, kw2, kw2p, sink_e, sink_o, bd, dy)


def _ssd_consts():
    hh = jnp.arange(128)[:, None]
    e = (hh == (jnp.arange(SSD_W)[None, :] // HP)).astype(BF16)
    e2 = (hh == (jnp.arange(NH * 128)[None, :] // 128)).astype(BF16)
    et = e.T
    tril = (jnp.arange(Q)[:, None] >= jnp.arange(Q)[None, :]).astype(BF16)
    triu = tril.T
    eye = jnp.eye(128, dtype=BF16)
    return e, e2, et, tril, triu, eye


def _ssd_common(x_ref, ext_scr, cw_ref, cb_ref, dt_ref, dtb_ref, alog_ref, e_ref, e2_ref, tril_ref, triu_ref,
                arow_scr, acol_scr, eax_scr):
    conv = jnp.broadcast_to(cb_ref[...], (Q, XBC))
    for k in range(4):
        conv = conv + cw_ref[k:k + 1, :] * ext_scr[pl.ds(5 + k, Q), :]
    sg = _sigmoid(conv)
    xbc = conv * sg
    dtpre = dt_ref[...] + dtb_ref[...]
    dt = _softplus(dtpre)
    a = -jnp.exp(alog_ref[...])
    adt = dt * a
    acol = _xdot_r(tril_ref[...], adt)
    acol_scr[...] = acol
    arow_scr[...] = _xdot_l(adt, triu_ref[...], 0, 0)
    alast = acol_scr[Q - 1:Q, :]
    ea = jnp.exp(acol)
    decs = jnp.exp(alast - acol)
    e = e_ref[...]
    dt_x = _xdot_l(dt, e)
    eax_scr[...] = _xdot_l(ea, e)
    decs_x = _xdot_l(decs, e)
    acx2 = _xdot_l(acol, e2_ref[...])
    return conv, sg, xbc, dtpre, dt, a, adt, acol, alast, ea, decs, dt_x, decs_x, acx2


def _ssd_fwd(u, cw, cb, dtb, alog, dxp, nw, consts, name):
    e, e2, et, tril, triu, eye = consts

    def body(z0_ref, z1_ref, x_ref, dt_ref, cw_ref, cb_ref, dtb_ref, alog_ref, dx_ref, nw_ref, e_ref, e2_ref,
             tril_ref, triu_ref, ya_ref, ypre_ref, st_ref, s_scr, ext_scr, arow_scr, acol_scr, eax_scr):
        c = pl.program_id(0)

        @pl.when(c == 0)
        def _():
            s_scr[...] = jnp.zeros_like(s_scr)
            ext_scr[0:8, :] = jnp.zeros((8, XBC), F32)
        ext_scr[8:8 + Q, :] = x_ref[...]
        (conv, sg, xbc, dtpre, dt, a, adt, acol, alast, ea, decs, dt_x, decs_x, acx2) = _ssd_common(
            x_ref, ext_scr, cw_ref, cb_ref, dt_ref, dtb_ref, alog_ref, e_ref, e2_ref, tril_ref, triu_ref,
            arow_scr, acol_scr, eax_scr)
        ext_scr[0:8, :] = ext_scr[Q:Q + 8, :]
        xs = xbc[:, :SSD_W]
        xdt = xs * dt_x
        lane = lax.broadcasted_iota(jnp.int32, (Q, 128), 1)
        causal = lax.broadcasted_iota(jnp.int32, (Q, Q), 0) >= lax.broadcasted_iota(jnp.int32, (Q, Q), 1)
        for g in range(2):
            bg = xbc[:, SSD_W + g * NS:SSD_W + (g + 1) * NS].astype(BF16)
            cg = xbc[:, SSD_W + 2 * NS + g * NS:SSD_W + 2 * NS + (g + 1) * NS].astype(BF16)
            cbm = _dg(cg, bg, 1, 1)
            sgv = s_scr[g]
            st_ref[0, g] = sgv
            gc = slice(g * 512, (g + 1) * 512)
            yoff = _dg(cg, sgv.astype(BF16), 1, 0) * eax_scr[:, gc]
            for pr in range(4):
                h0 = g * 8 + 2 * pr
                h1 = h0 + 1
                c0 = g * 512 + pr * 128
                xp = xdt[:, c0:c0 + 128].astype(BF16)
                w0 = (cbm * jnp.exp(jnp.where(causal, acx2[:, h0 * 128:(h0 + 1) * 128] - arow_scr[h0:h0 + 1, :],
                                              NEG))).astype(BF16)
                w1 = (cbm * jnp.exp(jnp.where(causal, acx2[:, h1 * 128:(h1 + 1) * 128] - arow_scr[h1:h1 + 1, :],
                                              NEG))).astype(BF16)
                yd = jnp.where(lane < HP, _dg(w0, xp, 1, 0), _dg(w1, xp, 1, 0))
                ypre_ref[:, c0:c0 + 128] = (yd + yoff[:, pr * 128:(pr + 1) * 128]
                                            + xs[:, c0:c0 + 128] * dx_ref[:, c0:c0 + 128])
            contrib = _dg(bg, (xdt[:, gc] * decs_x[:, gc]).astype(BF16), 0, 0)
            s_scr[g] = sgv * eax_scr[Q - 1:Q, gc] + contrib
        for g, zr in enumerate((z0_ref, z1_ref)):
            gc = slice(g * 512, (g + 1) * 512)
            zz = zr[...]
            ggg = ypre_ref[:, gc] * (zz * _sigmoid(zz))
            rstd = lax.rsqrt(jnp.mean(ggg * ggg, axis=-1, keepdims=True) + RMS_EPS)
            ya_ref[:, gc] = (ggg * rstd * nw_ref[:, gc]).astype(BF16)

    def row(w, blk=0):
        return pl.BlockSpec((Q, w), lambda c: (c, blk))

    def full(shape):
        return pl.BlockSpec(shape, lambda c: (0,) * len(shape))

    return pl.pallas_call(
        body, name=name, grid=(NC,),
        in_specs=[row(512, U_Z // 512), row(512, U_Z // 512 + 1), row(XBC, U_X // XBC), row(128, U_DT // 128),
                  full((4, XBC)), full((1, XBC)), full((1, 128)), full((1, 128)),
                  full((1, SSD_W)), full((1, SSD_W)), full((128, SSD_W)), full((128, NH * 128)), full((Q, Q)),
                  full((Q, Q))],
        out_specs=[row(SSD_W), row(SSD_W), pl.BlockSpec((1, 2, NS, 512), lambda c: (c, 0, 0, 0))],
        out_shape=[jax.ShapeDtypeStruct((L, SSD_W), BF16), jax.ShapeDtypeStruct((L, SSD_W), F32),
                   jax.ShapeDtypeStruct((NC, 2, NS, 512), F32)],
        scratch_shapes=[pltpu.VMEM((2, NS, 512), F32), pltpu.VMEM((Q + 8, XBC), F32), pltpu.VMEM((128, Q), F32),
                        pltpu.VMEM((Q, 128), F32), pltpu.VMEM((Q, SSD_W), F32)],
        compiler_params=_cp(("arbitrary",)))(u, u, u, u, cw, cb, dtb, alog, dxp, nw, e, e2, tril, triu)


def _ssd_bwd(u, ypre, st, dy, cw, cb, dtb, alog, dxp, nw, consts, name):
    e, e2, et, tril, triu, eye = consts

    def body(z0_ref, z1_ref, x_ref, xp_ref, dt_ref, ypre_ref, st_ref, dya_ref, cw_ref, cb_ref, dtb_ref, alog_ref, dx_ref,
             nw_ref, e_ref, e2_ref, et_ref, tril_ref, triu_ref, eye_ref,
             dz_ref, dxr_ref, ddtr_ref, dcw_ref, dcb_ref, ddtb_ref, dalog_ref, dd_ref, dnw_ref,
             g_scr, ext_scr, ext2_scr, arow_scr, acol_scr, eax_scr, darow_scr, dxdt_scr, t1_scr, t2_scr, dgg_scr):
        i = pl.program_id(0)

        @pl.when(i == 0)
        def _():
            g_scr[...] = jnp.zeros_like(g_scr)
            ext2_scr[Q:Q + 8, :] = jnp.zeros((8, XBC), F32)
            for r in (dcw_ref, dcb_ref, ddtb_ref, dalog_ref, dd_ref, dnw_ref):
                r[...] = jnp.zeros_like(r)
        not_first = jnp.where(i < NC - 1, 1.0, 0.0)
        ext_scr[0:8, :] = xp_ref[Q - 8:Q, :] * not_first
        ext_scr[8:8 + Q, :] = x_ref[...]
        (conv, sg, xbc, dtpre, dt, a, adt, acol, alast, ea, decs, dt_x, decs_x, acx2) = _ssd_common(
            x_ref, ext_scr, cw_ref, cb_ref, dt_ref, dtb_ref, alog_ref, e_ref, e2_ref, tril_ref, triu_ref,
            arow_scr, acol_scr, eax_scr)
        et_m = et_ref[...]
        xs = xbc[:, :SSD_W]
        xdt = xs * dt_x
        y = ypre_ref[...]
        zz = jnp.concatenate([z0_ref[...], z1_ref[...]], axis=1)
        sz = _sigmoid(zz)
        silu_z = zz * sz
        gg = y * silu_z
        dya = dya_ref[...]
        for g in range(2):
            gc = slice(g * 512, (g + 1) * 512)
            ggg = gg[:, gc]
            rstd = lax.rsqrt(jnp.mean(ggg * ggg, axis=-1, keepdims=True) + RMS_EPS)
            n = ggg * rstd
            dyag = dya[:, gc]
            dnw_ref[:, gc] += jnp.sum(dyag * n, axis=0, keepdims=True)
            dn = dyag * nw_ref[:, gc]
            dgg_scr[:, gc] = rstd * (dn - n * jnp.mean(dn * n, axis=-1, keepdims=True))
        dgg = dgg_scr[...]
        dy = dgg * silu_z
        dz_ref[...] = (dgg * y * (sz * (1.0 + zz * (1.0 - sz)))).astype(BF16)
        dd_ref[...] += _rowdot(jnp.sum(dy * xs, axis=0, keepdims=True), et_m)
        dxs = dy * dx_ref[...]
        dys = dy * eax_scr[...]
        lane = lax.broadcasted_iota(jnp.int32, (Q, 128), 1)
        causal = lax.broadcasted_iota(jnp.int32, (Q, Q), 0) >= lax.broadcasted_iota(jnp.int32, (Q, Q), 1)
        darow_scr[...] = jnp.zeros_like(darow_scr)
        dacol = jnp.zeros((Q, 128), F32)
        dcdx = []
        dbs = []
        dcs = []
        for g in range(2):
            gc = slice(g * 512, (g + 1) * 512)
            bg = xbc[:, SSD_W + g * NS:SSD_W + (g + 1) * NS].astype(BF16)
            cg = xbc[:, SSD_W + 2 * NS + g * NS:SSD_W + 2 * NS + (g + 1) * NS].astype(BF16)
            cbm = _dg(cg, bg, 1, 1)
            sgv = st_ref[0, g]
            sgb = sgv.astype(BF16)
            gv = g_scr[g]
            gvb = gv.astype(BF16)
            yoff = _dg(cg, sgb, 1, 0) * eax_scr[:, gc]
            dysg = dys[:, gc].astype(BF16)
            dcg = _dg(dysg, sgb, 1, 1)
            ds_off = _dg(cg, dysg, 0, 0)
            t1_scr[:, gc] = dy[:, gc] * yoff
            xdec = xdt[:, gc] * decs_x[:, gc]
            dxd = _dg(bg, gvb, 1, 0)
            dbg = _dg(xdec.astype(BF16), gvb, 1, 1)
            dxdt_g = dxd * decs_x[:, gc]
            t2_scr[:, gc] = dxd * xdt[:, gc]
            cdx = eax_scr[Q - 1:Q, gc]
            dcdx.append(jnp.sum(gv * sgv, axis=0, keepdims=True))
            g_scr[g] = gv * cdx + ds_off
            dcb_acc = jnp.zeros((Q, Q), F32)
            for pr in range(4):
                c0 = g * 512 + pr * 128
                xp = xdt[:, c0:c0 + 128].astype(BF16)
                dyp = dy[:, c0:c0 + 128]
                dypb = dyp.astype(BF16)
                halves = []
                for hh, keep in ((g * 8 + 2 * pr, lane < HP), (g * 8 + 2 * pr + 1, lane >= HP)):
                    lam = jnp.exp(jnp.where(causal, acx2[:, hh * 128:(hh + 1) * 128] - arow_scr[hh:hh + 1, :], NEG))
                    w = cbm * lam
                    dw = _dg(jnp.where(keep, dyp, 0.0).astype(BF16), xp, 1, 1)
                    dcb_acc = dcb_acc + dw * lam
                    t = dw * w
                    dacol = dacol + jnp.sum(t, axis=-1, keepdims=True) * (lane == hh).astype(F32)
                    darow_scr[hh:hh + 1, :] -= jnp.sum(t, axis=0, keepdims=True)
                    halves.append(_dg(w.astype(BF16), dypb, 0, 0))
                dxdt_scr[:, c0:c0 + 128] = (jnp.where(lane < HP, halves[0], halves[1])
                                            + dxdt_g[:, pr * 128:(pr + 1) * 128])
            dcbb = dcb_acc.astype(BF16)
            dcs.append(dcg + _dg(dcbb, bg, 1, 0))
            dbs.append(dbg + _dg(dcbb, cg, 0, 0))
        dacol = dacol + _xdot_l(t1_scr[...], et_m)
        ddecs = _xdot_l(t2_scr[...], et_m) * decs
        dacol = dacol - ddecs
        dalast = jnp.sum(ddecs, axis=0, keepdims=True)
        dcd = _rowdot(jnp.concatenate(dcdx, axis=1), et_m)
        dalast = dalast + dcd * jnp.exp(alast)
        dacol = dacol + _xdot_l(darow_scr[...], eye_ref[...], 0, 0)
        rowi = lax.broadcasted_iota(jnp.int32, (Q, 128), 0)
        dacol = dacol + jnp.where(rowi == Q - 1, dalast, 0.0)
        dadt = _xdot_r(triu_ref[...], dacol)
        dxdt = dxdt_scr[...]
        ddt = dadt * a + _xdot_l(dxdt * xs, et_m)
        dalog_ref[...] += jnp.sum(dadt * dt, axis=0, keepdims=True) * a
        dxs = dxs + dxdt * dt_x
        ddtr = ddt * _sigmoid(dtpre)
        ddtb_ref[...] += jnp.sum(ddtr, axis=0, keepdims=True)
        ddtr_ref[...] = ddtr.astype(BF16)
        dsilu = sg * (1.0 + conv * (1.0 - sg))
        ext2_scr[0:Q, 0:SSD_W] = dxs * dsilu[:, :SSD_W]
        for g in range(2):
            o1 = SSD_W + g * NS
            o2 = SSD_W + 2 * NS + g * NS
            ext2_scr[0:Q, o1:o1 + NS] = dbs[g] * dsilu[:, o1:o1 + NS]
            ext2_scr[0:Q, o2:o2 + NS] = dcs[g] * dsilu[:, o2:o2 + NS]
        dconv = ext2_scr[0:Q, :]
        dcb_ref[...] += jnp.sum(dconv, axis=0, keepdims=True)
        dxr = jnp.zeros((Q, XBC), F32)
        for k in range(4):
            dcw_ref[k:k + 1, :] += jnp.sum(dconv * ext_scr[pl.ds(5 + k, Q), :], axis=0, keepdims=True)
            dxr = dxr + cw_ref[k:k + 1, :] * ext2_scr[pl.ds(3 - k, Q), :]
        dxr_ref[...] = dxr.astype(BF16)
        ext2_scr[Q:Q + 8, :] = ext2_scr[0:8, :]

    def row(w, blk=0):
        return pl.BlockSpec((Q, w), lambda i: (NC - 1 - i, blk))

    def full(shape):
        return pl.BlockSpec(shape, lambda i: (0,) * len(shape))

    prev = pl.BlockSpec((Q, XBC), lambda i: (jnp.maximum(NC - 2 - i, 0), U_X // XBC))
    return pl.pallas_call(
        body, name=name, grid=(NC,),
        in_specs=[row(512, U_Z // 512), row(512, U_Z // 512 + 1), row(XBC, U_X // XBC), prev, row(128, U_DT // 128),
                  row(SSD_W),
                  pl.BlockSpec((1, 2, NS, 512), lambda i: (NC - 1 - i, 0, 0, 0)), row(SSD_W),
                  full((4, XBC)), full((1, XBC)), full((1, 128)), full((1, 128)), full((1, SSD_W)),
                  full((1, SSD_W)), full((128, SSD_W)), full((128, NH * 128)), full((SSD_W, 128)), full((Q, Q)),
                  full((Q, Q)), full((128, 128))],
        out_specs=[row(SSD_W), row(XBC), row(128), full((8, XBC)), full((1, XBC)), full((1, 128)), full((1, 128)),
                   full((1, 128)), full((1, SSD_W))],
        out_shape=[jax.ShapeDtypeStruct((L, SSD_W), BF16), jax.ShapeDtypeStruct((L, XBC), BF16),
                   jax.ShapeDtypeStruct((L, 128), BF16), jax.ShapeDtypeStruct((8, XBC), F32),
                   jax.ShapeDtypeStruct((1, XBC), F32), jax.ShapeDtypeStruct((1, 128), F32),
                   jax.ShapeDtypeStruct((1, 128), F32), jax.ShapeDtypeStruct((1, 128), F32),
                   jax.ShapeDtypeStruct((1, SSD_W), F32)],
        scratch_shapes=[pltpu.VMEM((2, NS, 512), F32), pltpu.VMEM((Q + 8, XBC), F32), pltpu.VMEM((Q + 8, XBC), F32),
                        pltpu.VMEM((128, Q), F32), pltpu.VMEM((Q, 128), F32), pltpu.VMEM((Q, SSD_W), F32),
                        pltpu.VMEM((128, Q), F32), pltpu.VMEM((Q, SSD_W), F32), pltpu.VMEM((Q, SSD_W), F32),
                        pltpu.VMEM((Q, SSD_W), F32), pltpu.VMEM((Q, SSD_W), F32)],
        compiler_params=_cp(("arbitrary",)))(u, u, u, u, u, ypre, st, dy, cw, cb, dtb, alog, dxp, nw,
                                             e, e2, et, tril, triu, eye)


def _my_pos():
    return lax.axis_index("x"), lax.axis_index("y"), lax.axis_index("c")


CHIP_REL = ((1, 0), (0, 1), (1, 1))
CHIP_XOR = (2, 1, 3)
BIG = ("w_in", "w_out", "w_mlp_up", "w_mlp_down")
NW = len(BIG)
AT = 256


def _chips(x, y):
    return [(1 - x if dx else x, 1 - y if dy else y) for dx, dy in CHIP_REL]


HBM_SPEC = pl.BlockSpec(memory_space=pltpu.HBM)
SEM_SPEC = pl.BlockSpec(memory_space=pltpu.SEMAPHORE)
EFFECT = pltpu.SideEffectType.DATAFLOW_SIDE_EFFECTING


def _hbm(t):
    return pltpu.with_memory_space_constraint(t, pltpu.HBM)


def _split_start(srcs, lands, after, copies, name):
    n = len(srcs)

    def body(*refs):
        src_refs, land_refs = refs[:n], refs[n:2 * n]
        send_sems, recv_sems = refs[2 * n + 1], refs[2 * n + 2]
        token = refs[-1]
        for w, k, src, dst, dev in copies(src_refs, land_refs):
            pltpu.make_async_remote_copy(src_ref=src, dst_ref=dst, send_sem=send_sems.at[3 * w + k],
                                         recv_sem=recv_sems.at[3 * w + k], device_id=dev, device_id_type=MESH).start()
        token[...] = jnp.zeros_like(token)

    outs = pl.pallas_call(
        body, name=name,
        out_shape=(pltpu.SemaphoreType.DMA((3 * n,)), pltpu.SemaphoreType.DMA((3 * n,)),
                   *[pltpu.HBM(t.shape, t.dtype) for t in srcs], *[pltpu.HBM(t.shape, t.dtype) for t in lands],
                   jax.ShapeDtypeStruct((8, 128), F32)),
        in_specs=[HBM_SPEC] * (2 * n) + [ANY],
        out_specs=(SEM_SPEC, SEM_SPEC, *([HBM_SPEC] * (2 * n)), VMEM_SPEC),
        input_output_aliases={i: 2 + i for i in range(2 * n)},
        compiler_params=pltpu.CompilerParams(has_side_effects=EFFECT))(
            *[_hbm(t) for t in srcs], *[_hbm(t) for t in lands], after)
    return outs[0], outs[1], list(outs[2:2 + n]), list(outs[2 + n:2 + 2 * n]), outs[-1]


def _split_wait(send_sems, recv_sems, srcs, lands, after, copies, name):
    n = len(srcs)

    def body(*refs):
        src_refs, land_refs = refs[:n], refs[n:2 * n]
        ssem, rsem = refs[2 * n], refs[2 * n + 1]
        for w, k, src, dst, dev in copies(src_refs, land_refs):
            cp = pltpu.make_async_remote_copy(src_ref=src, dst_ref=dst, send_sem=ssem.at[3 * w + k],
                                              recv_sem=rsem.at[3 * w + k], device_id=dev, device_id_type=MESH)
            cp.wait_send()
            cp.wait_recv()

    outs = pl.pallas_call(
        body, name=name,
        out_shape=tuple([pltpu.HBM(t.shape, t.dtype) for t in srcs] + [pltpu.HBM(t.shape, t.dtype) for t in lands]),
        in_specs=[HBM_SPEC] * (2 * n) + [SEM_SPEC, SEM_SPEC, ANY],
        out_specs=tuple([HBM_SPEC] * (2 * n)),
        input_output_aliases={i: i for i in range(2 * n)},
        compiler_params=pltpu.CompilerParams(has_side_effects=EFFECT))(*srcs, *lands, send_sems, recv_sems, after)
    return list(outs[:n]), list(outs[n:])


def _ag_copies(arrival):
    def copies(src_refs, land_refs):
        x, y, c = _my_pos()
        s = 2 * x + y
        chips = _chips(x, y)
        for w in range(NW):
            hr = src_refs[w].shape[0] // 2
            mine = pl.ds(c * hr, hr)
            for k in range(3):
                slot = s ^ CHIP_XOR[k] if arrival else s
                yield w, k, src_refs[w].at[mine], land_refs[w].at[slot, mine], (*chips[k], c)
    return copies


def _ag_forward(lands, name):
    def body(*refs):
        outs = refs[NW:2 * NW]
        send_sems, recv_sems = refs[2 * NW:]
        x, y, c = _my_pos()
        s = 2 * x + y
        sib = (x, y, 1 - c)
        sends = []
        for w in range(NW):
            hr = outs[w].shape[1] // 2
            for k in range(3):
                blk = outs[w].at[s ^ CHIP_XOR[k], pl.ds(c * hr, hr)]
                fw = pltpu.make_async_remote_copy(
                    src_ref=blk, dst_ref=blk, send_sem=send_sems.at[w, k], recv_sem=recv_sems.at[w, k],
                    device_id=sib, device_id_type=MESH)
                fw.start()
                sends.append(fw)
        for w in range(NW):
            hr = outs[w].shape[1] // 2
            for k in range(3):
                blk = outs[w].at[s ^ CHIP_XOR[k], pl.ds((1 - c) * hr, hr)]
                pltpu.make_async_remote_copy(
                    src_ref=blk, dst_ref=blk, send_sem=send_sems.at[w, k], recv_sem=recv_sems.at[w, k],
                    device_id=sib, device_id_type=MESH).wait_recv()
        for cp in sends:
            cp.wait_send()

    return pl.pallas_call(
        body, name=name, in_specs=[ANY] * NW, out_specs=[ANY] * NW,
        out_shape=[jax.ShapeDtypeStruct(t.shape, t.dtype) for t in lands],
        input_output_aliases={w: w for w in range(NW)},
        scratch_shapes=[pltpu.SemaphoreType.DMA((NW, 3)), pltpu.SemaphoreType.DMA((NW, 3))])(*lands)


def _rs_copies(src_refs, land_refs):
    x, y, c = _my_pos()
    s = 2 * x + y
    chips = _chips(x, y)
    for w in range(NW):
        for k in range(3):
            yield w, k, src_refs[w].at[s ^ CHIP_XOR[k]], land_refs[w].at[k], (*chips[k], c)


def _place_own(shard, gathered, sidx, name):
    r, cc = shard.shape

    def body(s_ref, a_ref, g_ref, o_ref):
        o_ref[0] = a_ref[...]

    return pl.pallas_call(
        body, name=name,
        grid_spec=pltpu.PrefetchScalarGridSpec(
            num_scalar_prefetch=1, grid=(r // AT,),
            in_specs=[pl.BlockSpec((AT, cc), lambda i, s_ref: (i, 0)), ANY],
            out_specs=pl.BlockSpec((1, AT, cc), lambda i, s_ref: (s_ref[0], i, 0))),
        out_shape=jax.ShapeDtypeStruct(gathered.shape, gathered.dtype),
        input_output_aliases={2: 0}, compiler_params=_cp(("parallel",)))(sidx, shard, gathered)


def _rs_pair(dwb, name):
    def body(*refs):
        ins, outs = refs[:NW], refs[NW:2 * NW]
        send_sems, recv_sems = refs[2 * NW:]
        x, y, c = _my_pos()
        cps = []
        for w in range(NW):
            hr = ins[w].shape[1] // 2
            cp = pltpu.make_async_remote_copy(
                src_ref=ins[w].at[:, pl.ds((1 - c) * hr, hr)], dst_ref=outs[w], send_sem=send_sems.at[w],
                recv_sem=recv_sems.at[w], device_id=(x, y, 1 - c), device_id_type=MESH)
            cp.start()
            cps.append(cp)
        for cp in cps:
            cp.wait()

    return pl.pallas_call(
        body, name=name, in_specs=[ANY] * NW, out_specs=[ANY] * NW,
        out_shape=[jax.ShapeDtypeStruct((4, t.shape[1] // 2, t.shape[2]), t.dtype) for t in dwb],
        scratch_shapes=[pltpu.SemaphoreType.DMA((NW,)), pltpu.SemaphoreType.DMA((NW,))])(*dwb)


def _rs_sib(q, name):
    def body(*refs):
        ins, outs = refs[:NW], refs[NW:2 * NW]
        send_sems, recv_sems = refs[2 * NW:]
        x, y, c = _my_pos()
        cps = []
        for w in range(NW):
            hr = outs[w].shape[0] // 2
            mine = pl.ds(c * hr, hr)
            cp = pltpu.make_async_remote_copy(
                src_ref=outs[w].at[mine], dst_ref=outs[w].at[mine], send_sem=send_sems.at[w],
                recv_sem=recv_sems.at[w], device_id=(x, y, 1 - c), device_id_type=MESH)
            cp.start()
            cps.append(cp)
        for w in range(NW):
            hr = outs[w].shape[0] // 2
            other = outs[w].at[pl.ds((1 - c) * hr, hr)]
            pltpu.make_async_remote_copy(
                src_ref=other, dst_ref=other, send_sem=send_sems.at[w], recv_sem=recv_sems.at[w],
                device_id=(x, y, 1 - c), device_id_type=MESH).wait_recv()
        for cp in cps:
            cp.wait_send()

    return pl.pallas_call(
        body, name=name, in_specs=[ANY] * NW, out_specs=[ANY] * NW,
        out_shape=[jax.ShapeDtypeStruct(t.shape, t.dtype) for t in q],
        input_output_aliases={w: w for w in range(NW)},
        scratch_shapes=[pltpu.SemaphoreType.DMA((NW,)), pltpu.SemaphoreType.DMA((NW,))])(*q)


def _rs_add2(dw, got, cidx, name):
    _, r, cc = dw.shape
    hr = r // 2
    nb = hr // AT

    def body(c_ref, a_ref, b_ref, o_ref, ob_ref):
        acc = a_ref[...] + b_ref[...].astype(F32)
        o_ref[...] = acc
        ob_ref[...] = acc.astype(BF16)

    blk = pl.BlockSpec((1, AT, cc), lambda sh, i, c_ref: (sh, i, 0))
    return pl.pallas_call(
        body, name=name,
        grid_spec=pltpu.PrefetchScalarGridSpec(
            num_scalar_prefetch=1, grid=(4, nb),
            in_specs=[pl.BlockSpec((1, AT, cc), lambda sh, i, c_ref: (sh, c_ref[0] * nb + i, 0)), blk],
            out_specs=[blk, blk]),
        out_shape=[jax.ShapeDtypeStruct((4, hr, cc), F32), jax.ShapeDtypeStruct((4, hr, cc), BF16)],
        compiler_params=_cp(("parallel", "parallel")))(cidx, dw, got)


def _rs_add4(p, got, scidx, name):
    _, hr, cc = p.shape
    nb = hr // AT

    def body(s_ref, p_ref, g0_ref, g1_ref, g2_ref, o_ref):
        acc = p_ref[0] + g0_ref[0].astype(F32)
        acc = acc + g1_ref[0].astype(F32)
        o_ref[...] = acc + g2_ref[0].astype(F32)

    def gk(k):
        return pl.BlockSpec((1, AT, cc), lambda i, s_ref: (k, i, 0))

    return pl.pallas_call(
        body, name=name,
        grid_spec=pltpu.PrefetchScalarGridSpec(
            num_scalar_prefetch=1, grid=(nb,),
            in_specs=[pl.BlockSpec((1, AT, cc), lambda i, s_ref: (s_ref[0], i, 0)), gk(0), gk(1), gk(2)],
            out_specs=pl.BlockSpec((AT, cc), lambda i, s_ref: (s_ref[1] * nb + i, 0))),
        out_shape=jax.ShapeDtypeStruct((2 * hr, cc), F32),
        compiler_params=_cp(("parallel",)))(scidx, p, got, got, got)


def _rs_begin(dws, dwbs, after):
    _, _, c = _my_pos()
    cidx = jnp.reshape(c, (1,)).astype(jnp.int32)
    got = _rs_pair(dwbs, "rs_pair")
    pairs = [_rs_add2(dws[w], got[w], cidx, "rs_add2_%d" % w) for w in range(NW)]
    pb = [p[1] for p in pairs]
    lands = [lax.empty((3,) + t.shape[1:], BF16) for t in pb]
    ssem, rsem, pb, lands, token = _split_start(pb, lands, after, _rs_copies, "rs_chip_start")
    return ([p[0] for p in pairs], ssem, rsem, pb, lands), token


def _rs_end(state, after):
    x, y, c = _my_pos()
    scidx = jnp.stack([2 * x + y, c]).astype(jnp.int32)
    p, ssem, rsem, pb, lands = state
    _, recv = _split_wait(ssem, rsem, pb, lands, after, _rs_copies, "rs_chip_wait")
    q = [_rs_add4(p[w], recv[w], scidx, "rs_add4_%d" % w) for w in range(NW)]
    return _rs_sib(q, "rs_sib")


def _allreduce_small(buf, name):
    rows = buf.shape[0]

    def body(src_ref, out_ref, gat_ref, send_sems, recv_sems):
        x, y, c = _my_pos()
        me = 4 * x + 2 * y + c
        gat_ref[me] = src_ref[...]
        cps = []
        for r in range(1, N_DEV):
            tx = 1 - x if (r >> 2) & 1 else x
            ty = 1 - y if (r >> 1) & 1 else y
            tc = 1 - c if r & 1 else c
            cps.append(pltpu.make_async_remote_copy(
                src_ref=src_ref, dst_ref=gat_ref.at[me], send_sem=send_sems.at[r - 1], recv_sem=recv_sems.at[r - 1],
                device_id=(tx, ty, tc), device_id_type=MESH))
        for cp in cps:
            cp.start()
        for cp in cps:
            cp.wait()
        acc = gat_ref[0]
        for k in range(1, N_DEV):
            acc = acc + gat_ref[k]
        out_ref[...] = acc

    return pl.pallas_call(
        body, name=name, in_specs=[VMEM_SPEC], out_specs=VMEM_SPEC, out_shape=jax.ShapeDtypeStruct((rows, 128), F32),
        scratch_shapes=[pltpu.VMEM((N_DEV, rows, 128), F32), pltpu.SemaphoreType.DMA((N_DEV - 1,)),
                        pltpu.SemaphoreType.DMA((N_DEV - 1,))],
        compiler_params=_cp())(buf)


SMALL = (("norm_mix_w", (D,)), ("ssd_conv_w", (4, XBC)), ("ssd_conv_b", (XBC,)), ("ssd_dt_bias", (NH,)),
         ("ssd_a_log", (NH,)), ("ssd_d", (NH,)), ("ssd_norm_w", (SSD_W,)), ("q_norm_w", (HD,)),
         ("k_norm_w", (HD,)), ("attn_sinks", (NQH,)), ("cm_dw_w", (CMK, CMC)), ("cm_dw_b", (CMC,)),
         ("cm_ln_w", (CMC,)), ("cm_ln_b", (CMC,)), ("norm_mlp_w", (D,)))
SHARDED_SMALL = ("ssd_conv_w", "cm_dw_w")


def _seg_len(shape):
    n = 1
    for d in shape:
        n *= d
    return -(-n // 128) * 128


def _pack_small(vals, names):
    parts = []
    for name, shape in SMALL:
        if name not in names:
            continue
        v = vals[name].reshape(DEPTH, -1)
        pad = _seg_len(shape) - v.shape[1]
        parts.append(jnp.pad(v, ((0, 0), (0, pad))))
    flat = jnp.concatenate(parts, axis=1)
    return flat.reshape(-1, 128)


def _unpack_small(buf, names):
    flat = buf.reshape(DEPTH, -1)
    out = {}
    off = 0
    for name, shape in SMALL:
        if name not in names:
            continue
        n = 1
        for d in shape:
            n *= d
        out[name] = flat[:, off:off + n].reshape((DEPTH,) + shape)
        off += _seg_len(shape)
    return out


SW = N_IN // 4
SWP = 1152
ORIG = (("z", 0, 1024), ("x", 1024, 2560), ("dt", 2560, 2576), ("q", 2576, 3088), ("k", 3088, 3216),
        ("v", 3216, 3344), ("a", 3344, 3856), ("g", 3856, 4368))


def _orig_cols(g_in, lo, hi):
    out = []
    for s in range(4):
        a, b = max(lo, s * SW), min(hi, (s + 1) * SW)
        if a < b:
            out.append(g_in[s][:, a - s * SW:b - s * SW])
    return out


def _shard_major(parts):
    cols = []
    for s in range(4):
        for name, g0, g1 in ORIG:
            a, b = max(g0, s * SW), min(g1, (s + 1) * SW)
            if a < b:
                cols.append(parts[name][:, a - g0:b - g0])
        cols.append(jnp.zeros((L, SWP - SW), BF16))
    return jnp.concatenate(cols, axis=1)


def _rope_tables():
    inv = 10000.0 ** (-jnp.arange(0, HD, 2, dtype=F32) / HD)
    ang = jnp.arange(L, dtype=F32)[:, None] * inv[None, :]
    return jnp.cos(ang), jnp.sin(ang)


def _swa_tables():
    cos, sin = _rope_tables()
    return jnp.tile(cos, (1, 4)), jnp.tile(jnp.concatenate([-sin, sin], axis=1), (1, 2))


def _swa_weights(w):
    return jnp.tile(w, 2)[None], jnp.tile(jnp.concatenate([w[HH:], w[:HH]]), 2)[None]


def _swa_sinks(s):
    s2 = s.reshape(NPAIR, 2)
    return (jnp.broadcast_to(s2[:, 0][:, None, None], (NPAIR, 1, 128)),
            jnp.broadcast_to(s2[:, 1][:, None, None], (NPAIR, 1, 128)))


def _swa_blockdiag():
    i = jnp.arange(128)
    return (i[:, None] // HD == i[None, :] // HD).astype(BF16)


def _pad128(v):
    return jnp.pad(v, (0, 128 - v.shape[0]))[None, :]


def kernel(x, norm_mix_w, w_in, ssd_conv_w, ssd_conv_b, ssd_dt_bias, ssd_a_log, ssd_d, ssd_norm_w, q_norm_w, k_norm_w, attn_sinks, cm_dw_w, cm_dw_b, cm_ln_w, cm_ln_b, w_out, norm_mlp_w, w_mlp_up, w_mlp_down, loss_target, m_norm_mix_w, m_w_in, m_ssd_conv_w, m_ssd_conv_b, m_ssd_dt_bias, m_ssd_a_log, m_ssd_d, m_ssd_norm_w, m_q_norm_w, m_k_norm_w, m_attn_sinks, m_cm_dw_w, m_cm_dw_b, m_cm_ln_w, m_cm_ln_b, m_w_out, m_norm_mlp_w, m_w_mlp_up, m_w_mlp_down, v_norm_mix_w, v_w_in, v_ssd_conv_w, v_ssd_conv_b, v_ssd_dt_bias, v_ssd_a_log, v_ssd_d, v_ssd_norm_w, v_q_norm_w, v_k_norm_w, v_attn_sinks, v_cm_dw_w, v_cm_dw_b, v_cm_ln_w, v_cm_ln_b, v_w_out, v_norm_mlp_w, v_w_mlp_up, v_w_mlp_down):
    px, py, pc = _my_pos()
    shard = 2 * px + py
    sidx = jnp.reshape(shard, (1,)).astype(jnp.int32)
    consts = _ssd_consts()
    cos4, sin4 = _swa_tables()
    bd = _swa_blockdiag()

    wb = [w.astype(BF16) for w in (jnp.pad(w_in, ((0, 0), (0, 0), (0, SWP - SW))), w_out, w_mlp_up, w_mlp_down)]

    def gather_start(l, after):
        own = [t[l] for t in wb]
        lands = [lax.empty((4,) + t.shape, BF16) for t in own]
        return _split_start(own, lands, after, _ag_copies(False), "ag_start")

    zero_tile = jnp.zeros((8, 128), F32)
    in_flight = gather_start(0, zero_tile)
    zc = jnp.zeros((DEPTH, 4, XBC), F32)
    zc = lax.dynamic_update_slice_in_dim(zc, ssd_conv_w, shard * (XBC // 4), axis=2)
    zd = jnp.zeros((DEPTH, CMK, CMC), F32)
    zd = lax.dynamic_update_slice_in_dim(zd, cm_dw_w, shard * (CMC // 4), axis=2)
    half = jnp.where(pc == 0, 1.0, 0.0).astype(F32)
    gw = _allreduce_small(_pack_small({"ssd_conv_w": zc * half, "cm_dw_w": zd * half}, SHARDED_SMALL)
                          + in_flight[4][0:1, 0:1], "ag_small")
    after_first = gw
    gw = _unpack_small(gw, SHARDED_SMALL)
    conv_w_full, dw_w_full = gw["ssd_conv_w"], gw["cm_dw_w"]

    xcur = x[0]
    saved = []
    for l in range(DEPTH):
        ssem, rsem, own, lands, _ = in_flight
        own, lands = _split_wait(ssem, rsem, own, lands, after_first if l == 0 else xcur, _ag_copies(True), "ag_wait")
        lands = _ag_forward(lands, "ag_forward")
        g_in, g_out, g_up, g_dn = [_place_own(own[w], g, sidx, "ag_place_%d" % w) for w, g in enumerate(lands)]
        nmw = norm_mix_w[l][None]
        if l + 1 < DEPTH:
            in_flight = gather_start(l + 1, g_dn)
            nmw = nmw + in_flight[4][0:1, 0:1]
        grp = dict((n, (a, b)) for n, a, b in ORIG)
        w_perm = jnp.concatenate(
            _orig_cols(g_in, *grp["x"]) + _orig_cols(g_in, *grp["z"]) + _orig_cols(g_in, grp["a"][0], grp["g"][1])
            + _orig_cols(g_in, grp["q"][0], grp["v"][1]) + _orig_cols(g_in, *grp["dt"])
            + [jnp.zeros((D, 128 - NH), BF16)], axis=1)
        g_out = g_out.reshape(2 * D, D)
        g_dn = g_dn.reshape(DFF, D)
        h = _rms_fwd(xcur, nmw, "rms_mix_fwd")
        u = _mm(h, w_perm, "nn", "in_proj", tn=640)
        alog = _pad128(ssd_a_log[l])
        dtb = _pad128(ssd_dt_bias[l])
        dxp = jnp.repeat(ssd_d[l], HP)[None, :]
        ssd_p = (conv_w_full[l], ssd_conv_b[l][None], dtb, alog, dxp, ssd_norm_w[l][None])
        ya, ypre, st = _ssd_fwd(u, *ssd_p, consts, "ssd_fwd")
        swa_p = (cos4, sin4, *_swa_weights(q_norm_w[l]), *_swa_weights(k_norm_w[l]), *_swa_sinks(attn_sinks[l]), bd)
        yb = _swa_fwd(u, *swa_p, "attn_fwd")
        cconv = _conf_conv_fwd(u, dw_w_full[l], cm_dw_b[l][None], "conf_conv_fwd")
        conf_p = (cconv, cm_ln_w[l][None], cm_ln_b[l][None])
        yc = _conf_ln_fwd(*conf_p, "conf_ln_fwd")
        ycat = jnp.concatenate([ya, yb, yc], axis=1)
        x1 = _mm(ycat, g_out, "nn", "out_proj", add=xcur)
        hm = _rms_fwd(x1, norm_mlp_w[l][None], "rms_mlp_fwd")
        r_up = _mm_up(hm, g_up, "mlp_up")
        x2 = _mm(r_up, g_dn, "nn", "mlp_down", add=x1)
        saved.append(dict(x=xcur, h=h, u=u, ypre=ypre, st=st, swa_p=swa_p, conf_p=conf_p, ycat=ycat, x1=x1,
                          hm=hm, r_up=r_up, ssd_p=ssd_p, g_in=g_in, g_out=g_out, g_up=g_up, g_dn=g_dn))
        xcur = x2

    lsum, dx, dxb = _loss_bwd(xcur, loss_target[0], "loss")

    loc = locals()
    names = ["norm_mix_w", "w_in", "ssd_conv_w", "ssd_conv_b", "ssd_dt_bias", "ssd_a_log", "ssd_d", "ssd_norm_w",
             "q_norm_w", "k_norm_w", "attn_sinks", "cm_dw_w", "cm_dw_b", "cm_ln_w", "cm_ln_b", "w_out", "norm_mlp_w",
             "w_mlp_up", "w_mlp_down"]
    weights = {n: loc[n] for n in names}
    moms = {n: loc["m_" + n] for n in names}
    vars_ = {n: loc["v_" + n] for n in names}
    big_out = {n: None for n in BIG}

    def finish_layer(layer, shard_grads):
        for n, g in zip(BIG, shard_grads):
            g = g[:, :SW] if n == "w_in" else g
            big_out[n] = _adamw_layer(weights[n], g, moms[n], vars_[n], layer, big_out[n], "adamw_" + n)

    pending = None
    gsm = {name: [] for name, _ in SMALL}
    for l in reversed(range(DEPTH)):
        sv = saved[l]
        da = _mm(dxb, sv["g_dn"], "nt", "mlp_down_dx", relu2_of=sv["r_up"])
        dwdn, dwdn_b = _mm_dw(sv["r_up"], dxb, "mlp_down_dw")
        dwup, dwup_b = _mm_dw(sv["hm"], da, "mlp_up_dw", col_shards=True)
        dhm = _mm_cs_nt(da, sv["g_up"], "mlp_up_dx")
        dx1, dx1b, dnw = _rms_bwd(sv["x1"], norm_mlp_w[l][None], dhm, dx, "rms_mlp_bwd")
        gsm["norm_mlp_w"].append(dnw[0])
        dy = _mm(dx1b, sv["g_out"], "nt", "out_proj_dx")
        dwout, dwout_b = _mm_dw(sv["ycat"], dx1b, "out_proj_dw")
        dcc, dwb, dlw, dlb = _conf_ln_bwd(*sv["conf_p"], dy, "conf_ln_bwd")
        da_c, dg_c, dww = _conf_conv_bwd(sv["u"], dcc, dw_w_full[l], "conf_conv_bwd")
        gsm["cm_dw_w"].append(dww[:CMK])
        gsm["cm_dw_b"].append(dwb[0])
        gsm["cm_ln_w"].append(dlw[0])
        gsm["cm_ln_b"].append(dlb[0])
        dq, dk, dv, dqw, dkw, dse, dso = _swa_bwd(sv["u"], dy, *sv["swa_p"], "attn_bwd")
        gsm["q_norm_w"].append(dqw[0, :HD] + dqw[0, HD:])
        gsm["k_norm_w"].append(dkw[0, :HD] + dkw[0, HD:])
        gsm["attn_sinks"].append(jnp.stack([dse[:, 0, 0], dso[:, 0, 0]], axis=1).reshape(NQH))
        (dz, dxr, ddtr, dcw, dcb, ddtb, dalog, ddd, dnsw) = _ssd_bwd(
            sv["u"], sv["ypre"], sv["st"], dy, *sv["ssd_p"], consts, "ssd_bwd")
        gsm["ssd_conv_w"].append(dcw[:4])
        gsm["ssd_conv_b"].append(dcb[0])
        gsm["ssd_dt_bias"].append(ddtb[0, :NH])
        gsm["ssd_a_log"].append(dalog[0, :NH])
        gsm["ssd_d"].append(ddd[0, :NH])
        gsm["ssd_norm_w"].append(dnsw[0])
        du = _shard_major(dict(z=dz, x=dxr, dt=ddtr[:, :NH], q=dq, k=dk, v=dv, a=da_c, g=dg_c))
        dwin, dwin_b = _mm_dw(sv["h"], du, "in_dw", col_shards=True, tn=SWP // 3)
        state, token = _rs_begin(
            [dwin, dwout.reshape(4, D // 2, D), dwup, dwdn.reshape(4, D, D)],
            [dwin_b, dwout_b.reshape(4, D // 2, D), dwup_b, dwdn_b.reshape(4, D, D)], zero_tile)
        dh = _mm_cs_nt(du, sv["g_in"], "in_dx")
        dx, dxb, dnm = _rms_bwd(sv["x"], norm_mix_w[l][None] + token[0:1, 0:1], dh, dx1, "rms_mix_bwd")
        gsm["norm_mix_w"].append(dnm[0])
        if pending is not None:
            finish_layer(l + 1, _rs_end(pending, dx))
        pending = state

    gsm = {k: jnp.stack(v[::-1]) for k, v in gsm.items()}
    packed = _pack_small(gsm, [n for n, _ in SMALL])
    packed = jnp.concatenate([packed, lsum], axis=0)
    red = _allreduce_small(packed, "ar_small")
    finish_layer(0, _rs_end(pending, red))
    loss = 0.5 * red[-8, 0] / D
    gsm = _unpack_small(red[:-8], [n for n, _ in SMALL])
    gsm["ssd_conv_w"] = lax.dynamic_slice_in_dim(gsm["ssd_conv_w"], shard * (XBC // 4), XBC // 4, axis=2)
    gsm["cm_dw_w"] = lax.dynamic_slice_in_dim(gsm["cm_dw_w"], shard * (CMC // 4), CMC // 4, axis=2)
    grads = dict(gsm)
    delta, new_m, new_v = {}, {}, {}
    for n in BIG:
        grads[n], delta[n], new_m[n], new_v[n] = big_out[n]

    packed_names = [n for n, _ in SMALL if n not in SHARDED_SMALL]
    pw = _pack_small(weights, packed_names)
    pg = _pack_small(grads, packed_names)
    pm = _pack_small(moms, packed_names)
    pv = _pack_small(vars_, packed_names)
    pd, pmn, pvn = _adamw(pw, pg, pm, pv, "adamw_small")
    for dst, buf in ((delta, pd), (new_m, pmn), (new_v, pvn)):
        dst.update(_unpack_small(buf, packed_names))
    for n in SHARDED_SMALL:
        shp = weights[n].shape
        flat = lambda t: t.reshape(-1, shp[-1])
        d_, m_, v_ = _adamw(flat(weights[n]), flat(grads[n]), flat(moms[n]), flat(vars_[n]), "adamw_" + n)
        delta[n], new_m[n], new_v[n] = d_.reshape(shp), m_.reshape(shp), v_.reshape(shp)

    return (loss, dx[None], *[grads[n] for n in names], *[delta[n] for n in names],
            *[new_m[n] for n in names], *[new_v[n] for n in names])
```

```python
import functools
import math

import jax
import jax.numpy as jnp
from jax import lax
from jax.experimental import pallas as pl
from jax.experimental.pallas import tpu as pltpu

F32 = jnp.float32
BF16 = jnp.bfloat16
MESH = pl.DeviceIdType.MESH
ANY = pl.BlockSpec(memory_space=pl.ANY)
VMEM_SPEC = pl.BlockSpec(memory_space=pltpu.VMEM)

D = 1024
L = 2048
DEPTH = 4
SSD_W = 1024
XBC = 1536
NH = 16
HP = 64
NS = 128
Q = 128
NC = L // Q
ATT_W = 512
NQH = 8
NKV = 2
HD = 64
HH = HD // 2
CMC = 512
CMK = 31
DFF = 4096
N_IN = 4368
N_PAD = 4480
RMS_EPS = 1e-6
LN_EPS = 1e-5
NEG = -1e30
LR, B1, B2, EPS_A, WD, STEP = 0.001, 0.9, 0.999, 1e-8, 0.01, 10
VMEM_LIMIT = 56 * 1024 * 1024
N_DEV = 8


def _cp(sem=None):
    kw = dict(vmem_limit_bytes=VMEM_LIMIT)
    if sem is not None:
        kw["dimension_semantics"] = sem
    return pltpu.CompilerParams(**kw)


def _dg(a, b, ca, cb):
    return lax.dot_general(a, b, (((ca,), (cb,)), ((), ())), preferred_element_type=F32)


def _split3(x):
    hi = x.astype(BF16)
    r = x - hi.astype(F32)
    mid = r.astype(BF16)
    lo = (r - mid.astype(F32)).astype(BF16)
    return hi, mid, lo


def _xdot_l(x, m, ca=1, cb=0):
    hi, mid, lo = _split3(x)
    return _dg(hi, m, ca, cb) + _dg(mid, m, ca, cb) + _dg(lo, m, ca, cb)


def _xdot_r(m, x, ca=1, cb=0):
    hi, mid, lo = _split3(x)
    return _dg(m, hi, ca, cb) + _dg(m, mid, ca, cb) + _dg(m, lo, ca, cb)


def _rowdot(v, m):
    return _xdot_l(jnp.broadcast_to(v, (8, v.shape[1])), m)[0:1]


def _sigmoid(x):
    return 1.0 / (1.0 + jnp.exp(-x))


def _softplus(x):
    e = jnp.exp(-jnp.abs(x))
    u = 1.0 + e
    l1p = jnp.where(u == 1.0, e, jnp.log(u) * (e / jnp.where(u == 1.0, 1.0, u - 1.0)))
    return jnp.maximum(x, 0.0) + l1p


def _tm(k):
    return L if k <= D else L // 2


def _mm(a, b, mode, name, add=None, relu2_of=None, tn=512):
    m, k = a.shape
    tm = min(m, _tm(k))
    a_spec = pl.BlockSpec((tm, k), lambda i, j: (i, 0))
    if mode == "nn":
        n = b.shape[1]
        b_spec = pl.BlockSpec((k, tn), lambda i, j: (0, j))
        cb = 0
    else:
        n = b.shape[0]
        b_spec = pl.BlockSpec((tn, k), lambda i, j: (j, 0))
        cb = 1
    assert m % tm == 0 and n % tn == 0, (m, n, tm, tn)
    o_spec = pl.BlockSpec((tm, tn), lambda i, j: (i, j))
    out_dtype = F32
    if relu2_of is not None:
        def body(a_ref, b_ref, c_ref, o_ref):
            o_ref[...] = (_dg(a_ref[...], b_ref[...], 1, cb) * (2.0 * jnp.sqrt(c_ref[...].astype(F32)))).astype(BF16)
        ins, specs, out_dtype = (a, b, relu2_of), [a_spec, b_spec, o_spec], BF16
    elif add is None:
        def body(a_ref, b_ref, o_ref):
            o_ref[...] = _dg(a_ref[...], b_ref[...], 1, cb)
        ins, specs = (a, b), [a_spec, b_spec]
    else:
        def body(a_ref, b_ref, c_ref, o_ref):
            o_ref[...] = _dg(a_ref[...], b_ref[...], 1, cb) + c_ref[...]
        ins, specs = (a, b, add), [a_spec, b_spec, o_spec]
    return pl.pallas_call(
        body, name=name, grid=(m // tm, n // tn), in_specs=specs, out_specs=o_spec,
        out_shape=jax.ShapeDtypeStruct((m, n), out_dtype), compiler_params=_cp(("parallel", "parallel")))(*ins)


def _mm_up(a, b, name, tn=512):
    m = a.shape[0]
    cs = DFF // 4
    per = cs // tn
    tm = min(m, _tm(D))

    def body(a_ref, b_ref, r_ref):
        r = jnp.maximum(_dg(a_ref[...], b_ref[0], 1, 0), 0.0)
        r_ref[...] = (r * r).astype(BF16)

    return pl.pallas_call(
        body, name=name, grid=(m // tm, DFF // tn),
        in_specs=[pl.BlockSpec((tm, D), lambda i, j: (i, 0)),
                  pl.BlockSpec((1, D, tn), lambda i, j: (j // per, 0, j % per))],
        out_specs=pl.BlockSpec((tm, tn), lambda i, j: (i, j)),
        out_shape=jax.ShapeDtypeStruct((m, DFF), BF16), compiler_params=_cp(("parallel", "parallel")))(a, b)


def _mm_cs_nt(a, b, name, tn=512):
    m = a.shape[0]
    _, n, cs = b.shape
    tm = min(m, _tm(4 * cs))

    def body(a_ref, b_ref, o_ref):
        acc = _dg(a_ref[:, 0:cs], b_ref[0], 1, 1)
        for s in range(1, 4):
            acc = acc + _dg(a_ref[:, s * cs:(s + 1) * cs], b_ref[s], 1, 1)
        o_ref[...] = acc

    return pl.pallas_call(
        body, name=name, grid=(m // tm, n // tn),
        in_specs=[pl.BlockSpec((tm, 4 * cs), lambda i, j: (i, 0)), pl.BlockSpec((4, tn, cs), lambda i, j: (0, j, 0))],
        out_specs=pl.BlockSpec((tm, tn), lambda i, j: (i, j)),
        out_shape=jax.ShapeDtypeStruct((m, n), F32), compiler_params=_cp(("parallel", "parallel")))(a, b)


def _mm_dw(a, b, name, col_shards=False, tn=512):
    k, m = a.shape
    n = b.shape[1]
    tm = min(m, D)
    a_spec = pl.BlockSpec((k, tm), lambda i, j: (0, i))
    b_spec = pl.BlockSpec((k, tn), lambda i, j: (0, j))
    if col_shards:
        per = (n // 4) // tn
        o_spec = pl.BlockSpec((1, tm, tn), lambda i, j: (j // per, i, j % per))
        shape = (4, m, n // 4)
    else:
        o_spec = pl.BlockSpec((tm, tn), lambda i, j: (i, j))
        shape = (m, n)

    def body(a_ref, b_ref, o_ref, ob_ref):
        acc = _dg(a_ref[...], b_ref[...], 0, 0).reshape(o_ref.shape)
        o_ref[...] = acc
        ob_ref[...] = acc.astype(BF16)

    return pl.pallas_call(
        body, name=name, grid=(m // tm, n // tn), in_specs=[a_spec, b_spec], out_specs=[o_spec, o_spec],
        out_shape=[jax.ShapeDtypeStruct(shape, F32), jax.ShapeDtypeStruct(shape, BF16)],
        compiler_params=_cp(("parallel", "parallel")))(a, b)


TR = 256


def _rms_fwd(x, w, name):
    def body(x_ref, w_ref, o_ref):
        xv = x_ref[...]
        r = lax.rsqrt(jnp.mean(xv * xv, axis=-1, keepdims=True) + RMS_EPS)
        o_ref[...] = (xv * r * w_ref[...]).astype(BF16)

    return pl.pallas_call(
        body, name=name, grid=(L // TR,),
        in_specs=[pl.BlockSpec((TR, D), lambda i: (i, 0)), pl.BlockSpec((1, D), lambda i: (0, 0))],
        out_specs=pl.BlockSpec((TR, D), lambda i: (i, 0)),
        out_shape=jax.ShapeDtypeStruct((L, D), BF16), compiler_params=_cp(("parallel",)))(x, w)


def _rms_bwd(x, w, dh, dres, name):
    def body(x_ref, w_ref, dh_ref, dr_ref, dx_ref, dxb_ref, dw_ref):
        xv = x_ref[...]
        r = lax.rsqrt(jnp.mean(xv * xv, axis=-1, keepdims=True) + RMS_EPS)
        n = xv * r
        dhv = dh_ref[...]
        g = dhv * w_ref[...]
        dx = dr_ref[...] + r * (g - n * jnp.mean(g * n, axis=-1, keepdims=True))
        dx_ref[...] = dx
        dxb_ref[...] = dx.astype(BF16)

        @pl.when(pl.program_id(0) == 0)
        def _():
            dw_ref[...] = jnp.zeros_like(dw_ref)
        dw_ref[...] += jnp.sum(dhv * n, axis=0, keepdims=True)

    row = pl.BlockSpec((TR, D), lambda i: (i, 0))
    vec = pl.BlockSpec((1, D), lambda i: (0, 0))
    return pl.pallas_call(
        body, name=name, grid=(L // TR,), in_specs=[row, vec, row, row], out_specs=[row, row, vec],
        out_shape=[jax.ShapeDtypeStruct((L, D), F32), jax.ShapeDtypeStruct((L, D), BF16),
                   jax.ShapeDtypeStruct((1, D), F32)],
        compiler_params=_cp(("arbitrary",)))(x, w, dh, dres)


def _adamw_layer(w, g, m, v, layer, prev, name):
    _, rows, cols = w.shape
    tr = 256 if cols * 256 * 4 <= 2 * 1024 * 1024 else 128
    c1 = 1.0 / (1.0 - B1 ** STEP)
    c2 = 1.0 / (1.0 - B2 ** STEP)
    n_prev = 0 if prev is None else 4

    def body(*refs):
        w_ref, g_ref, m_ref, v_ref = refs[:4]
        go_ref, d_ref, mo_ref, vo_ref = refs[4 + n_prev:]
        gv = g_ref[...]
        mn = B1 * m_ref[0] + (1.0 - B1) * gv
        vn = B2 * v_ref[0] + (1.0 - B2) * (gv * gv)
        go_ref[0] = gv
        mo_ref[0] = mn
        vo_ref[0] = vn
        d_ref[0] = -LR * ((mn * c1) / (jnp.sqrt(vn * c2) + EPS_A) + WD * w_ref[0])

    lay = pl.BlockSpec((1, tr, cols), lambda i: (layer, i, 0))
    shp = jax.ShapeDtypeStruct(w.shape, F32)
    return pl.pallas_call(
        body, name=name, grid=(rows // tr,),
        in_specs=[lay, pl.BlockSpec((tr, cols), lambda i: (i, 0)), lay, lay] + [ANY] * n_prev,
        out_specs=[lay] * 4, out_shape=[shp] * 4,
        input_output_aliases={4 + i: i for i in range(n_prev)},
        compiler_params=_cp(("parallel",)))(w, g, m, v, *(prev or ()))


def _loss_bwd(y, t, name):
    def body(y_ref, t_ref, l_ref, d_ref, db_ref):
        e = y_ref[...] - t_ref[...]
        d = e * (1.0 / D)
        d_ref[...] = d
        db_ref[...] = d.astype(BF16)

        @pl.when(pl.program_id(0) == 0)
        def _():
            l_ref[...] = jnp.zeros_like(l_ref)
        s = jnp.sum(jnp.sum(e * e, axis=-1, keepdims=True), axis=0, keepdims=True)
        l_ref[...] += jnp.broadcast_to(s, l_ref.shape)

    row = pl.BlockSpec((TR, D), lambda i: (i, 0))
    tile = pl.BlockSpec((8, 128), lambda i: (0, 0))
    return pl.pallas_call(
        body, name=name, grid=(L // TR,), in_specs=[row, row], out_specs=[tile, row, row],
        out_shape=[jax.ShapeDtypeStruct((8, 128), F32), jax.ShapeDtypeStruct((L, D), F32),
                   jax.ShapeDtypeStruct((L, D), BF16)],
        compiler_params=_cp(("arbitrary",)))(y, t)


def _adamw(w, g, m, v, name):
    rows, cols = w.shape
    tr = rows
    for cand in (512, 256, 128, 64, 32, 16, 8):
        if rows % cand == 0 and cand * cols * 4 <= 2 * 1024 * 1024:
            tr = cand
            break
    c1 = 1.0 / (1.0 - B1 ** STEP)
    c2 = 1.0 / (1.0 - B2 ** STEP)

    def body(w_ref, g_ref, m_ref, v_ref, d_ref, mo_ref, vo_ref):
        gv = g_ref[...]
        mn = B1 * m_ref[...] + (1.0 - B1) * gv
        vn = B2 * v_ref[...] + (1.0 - B2) * (gv * gv)
        mo_ref[...] = mn
        vo_ref[...] = vn
        d_ref[...] = -LR * ((mn * c1) / (jnp.sqrt(vn * c2) + EPS_A) + WD * w_ref[...])

    blk = pl.BlockSpec((tr, cols), lambda i: (i, 0))
    shp = jax.ShapeDtypeStruct((rows, cols), F32)
    return pl.pallas_call(body, name=name, grid=(rows // tr,), in_specs=[blk] * 4, out_specs=[blk] * 3,
                          out_shape=[shp] * 3, compiler_params=_cp(("parallel",)))(w, g, m, v)


CT = 256
CPAD = 32


U_X, U_Z, U_A, U_G, U_Q, U_K, U_V, U_DT = 0, 1536, 2560, 3072, 3584, 4096, 4224, 4352


CEXT = 8
CWIN = CT + CPAD
CROWS = L + CPAD + CEXT


def _fill_shifted(src_ref, base, win_ref, sh_ref):
    win_ref[...] = src_ref[pl.ds(base, CWIN + CEXT), :]
    for p in range(8):
        sh_ref[p] = win_ref[pl.ds(p, CWIN), :]


def _tap(sh_ref, o):
    return sh_ref[o % 8, 8 * (o // 8):8 * (o // 8) + CT, :]


CB = 128


def _conv_specs():
    return [pl.BlockSpec((L, CB), lambda j: (0, U_A // CB + j)), pl.BlockSpec((L, CB), lambda j: (0, U_G // CB + j))]


def _conv_scratch(n_padded):
    return ([pltpu.VMEM((CROWS, CB), F32)] * n_padded
            + [pltpu.VMEM((CWIN + CEXT, CB), F32), pltpu.VMEM((8, CWIN, CB), F32)])


def _fill_gated(a_ref, g_ref, hp_ref):
    hp_ref[0:CPAD, :] = jnp.zeros((CPAD, CB), F32)
    hp_ref[CPAD:CPAD + L, :] = a_ref[...] * _sigmoid(g_ref[...])
    hp_ref[CPAD + L:, :] = jnp.zeros((CEXT, CB), F32)


def _conf_conv_fwd(u, w, b, name):
    def body(a_ref, g_ref, w_ref, b_ref, c_ref, hp_ref, win_ref, sh_ref):
        _fill_gated(a_ref, g_ref, hp_ref)

        def tile(i, carry):
            base = pl.multiple_of(i * CT, CT)
            _fill_shifted(hp_ref, base, win_ref, sh_ref)
            c = jnp.broadcast_to(b_ref[...], (CT, CB))
            for k in range(CMK):
                c = c + w_ref[k:k + 1, :] * _tap(sh_ref, 2 + k)
            c_ref[pl.ds(base, CT), :] = c
            return carry

        lax.fori_loop(0, L // CT, tile, 0)

    return pl.pallas_call(
        body, name=name, grid=(CMC // CB,),
        in_specs=_conv_specs() + [pl.BlockSpec((CMK, CB), lambda j: (0, j)), pl.BlockSpec((1, CB), lambda j: (0, j))],
        out_specs=pl.BlockSpec((L, CB), lambda j: (0, j)),
        out_shape=jax.ShapeDtypeStruct((L, CMC), F32), scratch_shapes=_conv_scratch(1),
        compiler_params=_cp(("parallel",)))(u, u, w, b)


def _conf_ln_fwd(c, lw, lb, name):
    def body(c_ref, lw_ref, lb_ref, o_ref):
        cv = c_ref[...]
        cc = cv - jnp.mean(cv, axis=-1, keepdims=True)
        var = jnp.mean(cc * cc, axis=-1, keepdims=True)
        l = cc * lax.rsqrt(var + LN_EPS) * lw_ref[...] + lb_ref[...]
        o_ref[...] = (l * _sigmoid(l)).astype(BF16)

    row = pl.BlockSpec((TR, CMC), lambda i: (i, 0))
    vec = pl.BlockSpec((1, CMC), lambda i: (0, 0))
    return pl.pallas_call(body, name=name, grid=(L // TR,), in_specs=[row, vec, vec], out_specs=row,
                          out_shape=jax.ShapeDtypeStruct((L, CMC), BF16),
                          compiler_params=_cp(("parallel",)))(c, lw, lb)


def _conf_ln_bwd(c, lw, lb, dy, name):
    def body(c_ref, lw_ref, lb_ref, dy_ref, dc_ref, db_ref, dlw_ref, dlb_ref):
        cv = c_ref[...]
        cc = cv - jnp.mean(cv, axis=-1, keepdims=True)
        var = jnp.mean(cc * cc, axis=-1, keepdims=True)
        rstd = lax.rsqrt(var + LN_EPS)
        n = cc * rstd
        l = n * lw_ref[...] + lb_ref[...]
        sl = _sigmoid(l)
        dl = dy_ref[...] * (sl * (1.0 + l * (1.0 - sl)))
        dn = dl * lw_ref[...]
        dc = rstd * (dn - jnp.mean(dn, axis=-1, keepdims=True) - n * jnp.mean(dn * n, axis=-1, keepdims=True))
        dc_ref[...] = dc

        @pl.when(pl.program_id(0) == 0)
        def _():
            db_ref[...] = jnp.zeros_like(db_ref)
            dlw_ref[...] = jnp.zeros_like(dlw_ref)
            dlb_ref[...] = jnp.zeros_like(dlb_ref)
        db_ref[...] += jnp.sum(dc, axis=0, keepdims=True)
        dlw_ref[...] += jnp.sum(dl * n, axis=0, keepdims=True)
        dlb_ref[...] += jnp.sum(dl, axis=0, keepdims=True)

    row = pl.BlockSpec((TR, CMC), lambda i: (i, 0))
    vec = pl.BlockSpec((1, CMC), lambda i: (0, 0))
    vshape = jax.ShapeDtypeStruct((1, CMC), F32)
    return pl.pallas_call(
        body, name=name, grid=(L // TR,),
        in_specs=[row, vec, vec, pl.BlockSpec((TR, CMC), lambda i: (i, (SSD_W + ATT_W) // CMC))],
        out_specs=[row, vec, vec, vec], out_shape=[jax.ShapeDtypeStruct((L, CMC), F32), vshape, vshape, vshape],
        compiler_params=_cp(("arbitrary",)))(c, lw, lb, dy)


def _conf_conv_bwd(u, dc, w, name):
    def body(a_ref, g_ref, dc_ref, w_ref, da_ref, dg_ref, dw_ref, hp_ref, dcp_ref, win_ref, sh_ref, dwacc_ref):
        _fill_gated(a_ref, g_ref, hp_ref)
        dcp_ref[0:L, :] = dc_ref[...]
        dcp_ref[L:, :] = jnp.zeros((CPAD + CEXT, CB), F32)
        dwacc_ref[...] = jnp.zeros_like(dwacc_ref)

        def tile(i, carry):
            base = pl.multiple_of(i * CT, CT)
            _fill_shifted(hp_ref, base, win_ref, sh_ref)
            dcv = dcp_ref[pl.ds(base, CT), :]
            for k in range(CMK):
                dwacc_ref[k] += (dcv * _tap(sh_ref, 2 + k)).reshape(CT // 8, 8, CB).sum(axis=0)
            _fill_shifted(dcp_ref, base, win_ref, sh_ref)
            dh = jnp.zeros((CT, CB), F32)
            for k in range(CMK):
                dh = dh + w_ref[k:k + 1, :] * _tap(sh_ref, CMK - 1 - k)
            av = a_ref[pl.ds(base, CT), :]
            sg = _sigmoid(g_ref[pl.ds(base, CT), :])
            da_ref[pl.ds(base, CT), :] = (dh * sg).astype(BF16)
            dg_ref[pl.ds(base, CT), :] = (dh * av * sg * (1.0 - sg)).astype(BF16)
            return carry

        lax.fori_loop(0, L // CT, tile, 0)
        for k in range(CMK):
            dw_ref[k:k + 1, :] = jnp.sum(dwacc_ref[k], axis=0, keepdims=True)
        dw_ref[CMK:, :] = jnp.zeros((32 - CMK, CB), F32)

    col = pl.BlockSpec((L, CB), lambda j: (0, j))
    return pl.pallas_call(
        body, name=name, grid=(CMC // CB,),
        in_specs=_conv_specs() + [col, pl.BlockSpec((CMK, CB), lambda j: (0, j))],
        out_specs=[col, col, pl.BlockSpec((32, CB), lambda j: (0, j))],
        out_shape=[jax.ShapeDtypeStruct((L, CMC), BF16), jax.ShapeDtypeStruct((L, CMC), BF16),
                   jax.ShapeDtypeStruct((32, CMC), F32)],
        scratch_shapes=_conv_scratch(2) + [pltpu.VMEM((32, 8, CB), F32)],
        compiler_params=_cp(("parallel",)))(u, u, dc, w)


NPAIR = NQH // 2


def _partner(x, lo32):
    return jnp.where(lo32, pltpu.roll(x, 96, 1), pltpu.roll(x, 32, 1))


def _swa_prep(x, w2, w2p, c4, s4, bd, lo32):
    r = lax.rsqrt(_xdot_l(x * x, bd) * (1.0 / HD) + RMS_EPS)
    xh = x * r
    return r, xh, xh * w2 * c4 + _partner(xh, lo32) * w2p * s4


def _swa_unprep(dr, r, xh, w2, w2p, c4, s4, bd, lo32):
    dn = dr * c4
    dnp = dr * s4
    gx = dn * w2 + _partner(dnp * w2p, lo32)
    dw = jnp.sum((dn + _partner(dnp, lo32)) * xh, axis=0, keepdims=True)
    mu = _xdot_l(gx * xh, bd) * (1.0 / HD)
    return r * (gx - xh * mu), dw


def _swa_softmax(s, sink):
    row = lax.broadcasted_iota(jnp.int32, (L, 2 * Q), 0)
    col = lax.broadcasted_iota(jnp.int32, (L, 2 * Q), 1)
    rm = row & (Q - 1)
    valid = (col > rm) & (col <= rm + Q) & ((row >= Q) | (col >= Q))
    s = jnp.where(valid, s * (1.0 / math.sqrt(HD)), NEG)
    m = jnp.maximum(jnp.max(s, axis=-1, keepdims=True), sink)
    p = jnp.exp(s - m)
    ps = jnp.exp(sink - m)
    inv = 1.0 / (jnp.sum(p, axis=-1, keepdims=True) + ps)
    return p * inv, ps * inv


def _swa_in_specs():
    tab = pl.BlockSpec((L, 128), lambda p: (0, 0))
    wv = pl.BlockSpec((1, 128), lambda p: (0, 0))
    sk = pl.BlockSpec((1, 1, 128), lambda p: (p, 0, 0))
    return [pl.BlockSpec((L, 128), lambda p: (0, U_Q // 128 + p)), pl.BlockSpec((L, 128), lambda p: (0, U_K // 128)),
            pl.BlockSpec((L, 128), lambda p: (0, U_V // 128)), tab, tab, wv, wv, wv, wv, sk, sk,
            pl.BlockSpec((128, 128), lambda p: (0, 0))]


def _swa_setup(q_ref, k_ref, v_ref, c_ref, s_ref, qw_ref, qwp_ref, kw_ref, kwp_ref, bd_ref, kpad, vpad):
    g = pl.program_id(0) // 2
    lane = lax.broadcasted_iota(jnp.int32, (L, 128), 1)
    lo32 = (lane & 32) == 0
    own = (lane >> 6) == g
    c4, s4, bd = c_ref[...], s_ref[...], bd_ref[...]
    qn = _swa_prep(q_ref[...], qw_ref[...], qwp_ref[...], c4, s4, bd, lo32)
    kn = _swa_prep(k_ref[...], kw_ref[...], kwp_ref[...], c4, s4, bd, lo32)
    vv = v_ref[...]
    kpad[0:Q, :] = jnp.zeros((Q, 128), BF16)
    vpad[0:Q, :] = jnp.zeros((Q, 128), BF16)
    kpad[Q:, :] = jnp.where(own, kn[2], pltpu.roll(kn[2], HD, 1)).astype(BF16)
    vpad[Q:, :] = jnp.where(own, vv, pltpu.roll(vv, HD, 1)).astype(BF16)
    return qn, kn, lo32, own, c4, s4, bd


def _swa_fwd(u, cos4, sin4, qw2, qw2p, kw2, kw2p, sink_e, sink_o, bd, name):
    def body(q_ref, k_ref, v_ref, c_ref, s_ref, qw_ref, qwp_ref, kw_ref, kwp_ref, ske_ref, sko_ref, bd_ref,
             o_ref, kpad, vpad, s_scr, p_scr):
        qn, _, _, _, _, _, _ = _swa_setup(q_ref, k_ref, v_ref, c_ref, s_ref, qw_ref, qwp_ref, kw_ref, kwp_ref,
                                          bd_ref, kpad, vpad)
        qr = qn[2]
        first = lax.broadcasted_iota(jnp.int32, (Q, 128), 1) < HD
        for n in range(NC):
            rows = slice(n * Q, (n + 1) * Q)
            kc = kpad[n * Q:(n + 2) * Q, :]
            s_scr[0, rows, :] = _dg(jnp.where(first, qr[rows], 0.0).astype(BF16), kc, 1, 1)
            s_scr[1, rows, :] = _dg(jnp.where(first, 0.0, qr[rows]).astype(BF16), kc, 1, 1)
        for h, sk_ref in ((0, ske_ref), (1, sko_ref)):
            p, _ = _swa_softmax(s_scr[h], sk_ref[0][:, 0:1])
            p_scr[h] = p.astype(BF16)
        for n in range(NC):
            rows = slice(n * Q, (n + 1) * Q)
            vc = vpad[n * Q:(n + 2) * Q, :]
            o_ref[rows, :] = jnp.where(first, _dg(p_scr[0, rows, :], vc, 1, 0),
                                       _dg(p_scr[1, rows, :], vc, 1, 0)).astype(BF16)

    return pl.pallas_call(
        body, name=name, grid=(NPAIR,), in_specs=_swa_in_specs(),
        out_specs=pl.BlockSpec((L, 128), lambda p: (0, p)),
        out_shape=jax.ShapeDtypeStruct((L, ATT_W), BF16),
        scratch_shapes=[pltpu.VMEM((L + Q, 128), BF16), pltpu.VMEM((L + Q, 128), BF16),
                        pltpu.VMEM((2, L, 2 * Q), F32), pltpu.VMEM((2, L, 2 * Q), BF16)],
        compiler_params=_cp(("arbitrary",)))(u, u, u, cos4, sin4, qw2, qw2p, kw2, kw2p, sink_e, sink_o, bd)


def _swa_bwd(u, dy, cos4, sin4, qw2, qw2p, kw2, kw2p, sink_e, sink_o, bd, name):
    def body(q_ref, k_ref, v_ref, c_ref, s_ref, qw_ref, qwp_ref, kw_ref, kwp_ref, ske_ref, sko_ref, bd_ref, do_ref,
             dq_ref, dk_ref, dv_ref, dqw_ref, dkw_ref, dse_ref, dso_ref,
             kpad, vpad, s_scr, dp_scr, ds_scr, pb_scr, dkr_acc, dv_acc, dqr_scr):
        pidx = pl.program_id(0)

        @pl.when(pidx == 0)
        def _():
            dkr_acc[...] = jnp.zeros_like(dkr_acc)
            dv_acc[...] = jnp.zeros_like(dv_acc)
            dqw_ref[...] = jnp.zeros_like(dqw_ref)

        qn, kn, lo32, own, c4, s4, bd = _swa_setup(q_ref, k_ref, v_ref, c_ref, s_ref, qw_ref, qwp_ref, kw_ref,
                                                   kwp_ref, bd_ref, kpad, vpad)
        qr = qn[2]
        lane_q = lax.broadcasted_iota(jnp.int32, (Q, 128), 1)
        first = lane_q < HD
        own_q = (lane_q >> 6) == pidx // 2

        def halves(t):
            return jnp.where(first, t, 0.0).astype(BF16), jnp.where(first, 0.0, t).astype(BF16)

        for n in range(NC):
            rows = slice(n * Q, (n + 1) * Q)
            kc = kpad[n * Q:(n + 2) * Q, :]
            vc = vpad[n * Q:(n + 2) * Q, :]
            qm = halves(qr[rows])
            dom = halves(do_ref[rows, :])
            for h in range(2):
                s_scr[h, rows, :] = _dg(qm[h], kc, 1, 1)
                dp_scr[h, rows, :] = _dg(dom[h], vc, 1, 1)
        for h, sk_ref, dsk_ref in ((0, ske_ref, dse_ref), (1, sko_ref, dso_ref)):
            p, ps = _swa_softmax(s_scr[h], sk_ref[0][:, 0:1])
            dp = dp_scr[h]
            delta = jnp.sum(p * dp, axis=-1, keepdims=True)
            dsk_ref[0] = jnp.broadcast_to(-jnp.sum(ps * delta, axis=0, keepdims=True), (1, 128))
            ds_scr[h] = (p * (dp - delta) * (1.0 / math.sqrt(HD))).astype(BF16)
            pb_scr[h] = p.astype(BF16)
        for n in range(NC):
            rows = slice(n * Q, (n + 1) * Q)
            kc = kpad[n * Q:(n + 2) * Q, :]
            dqr_scr[rows, :] = jnp.where(first, _dg(ds_scr[0, rows, :], kc, 1, 0), _dg(ds_scr[1, rows, :], kc, 1, 0))
        for m in range(NC):
            acc_k = jnp.zeros((Q, 128), F32)
            acc_v = jnp.zeros((Q, 128), F32)
            for n, cols in ((m, slice(Q, 2 * Q)), (m + 1, slice(0, Q))):
                if n >= NC:
                    continue
                rows = slice(n * Q, (n + 1) * Q)
                qm = halves(qr[rows])
                dom = halves(do_ref[rows, :])
                for h in range(2):
                    acc_k = acc_k + _dg(ds_scr[h, rows, cols], qm[h], 0, 0)
                    acc_v = acc_v + _dg(pb_scr[h, rows, cols], dom[h], 0, 0)
            rows = slice(m * Q, (m + 1) * Q)
            dkr_acc[rows, :] += jnp.where(own_q, acc_k + pltpu.roll(acc_k, HD, 1), 0.0)
            dv_acc[rows, :] += jnp.where(own_q, acc_v + pltpu.roll(acc_v, HD, 1), 0.0)
        dq, dqw = _swa_unprep(dqr_scr[...], qn[0], qn[1], qw_ref[...], qwp_ref[...], c4, s4, bd, lo32)
        dq_ref[...] = dq.astype(BF16)
        dqw_ref[...] += dqw

        @pl.when(pidx == NPAIR - 1)
        def _():
            dk, dkw = _swa_unprep(dkr_acc[...], kn[0], kn[1], kw_ref[...], kwp_ref[...], c4, s4, bd, lo32)
            dk_ref[...] = dk.astype(BF16)
            dkw_ref[...] = dkw
            dv_ref[...] = dv_acc[...].astype(BF16)

    full = pl.BlockSpec((L, 128), lambda p: (0, 0))
    wv = pl.BlockSpec((1, 128), lambda p: (0, 0))
    sk = pl.BlockSpec((1, 1, 128), lambda p: (p, 0, 0))
    vec = jax.ShapeDtypeStruct((1, 128), F32)
    skv = jax.ShapeDtypeStruct((NPAIR, 1, 128), F32)
    return pl.pallas_call(
        body, name=name, grid=(NPAIR,),
        in_specs=_swa_in_specs() + [pl.BlockSpec((L, 128), lambda p: (0, SSD_W // 128 + p))],
        out_specs=[pl.BlockSpec((L, 128), lambda p: (0, p)), full, full, wv, wv, sk, sk],
        out_shape=[jax.ShapeDtypeStruct((L, ATT_W), BF16), jax.ShapeDtypeStruct((L, 128), BF16),
                   jax.ShapeDtypeStruct((L, 128), BF16), vec, vec, skv, skv],
        scratch_shapes=[pltpu.VMEM((L + Q, 128), BF16), pltpu.VMEM((L + Q, 128), BF16),
                        pltpu.VMEM((2, L, 2 * Q), F32), pltpu.VMEM((2, L, 2 * Q), F32),
                        pltpu.VMEM((2, L, 2 * Q), BF16), pltpu.VMEM((2, L, 2 * Q), BF16),
                        pltpu.VMEM((L, 128), F32), pltpu.VMEM((L, 128), F32), pltpu.VMEM((L, 128), F32)],
        compiler_params=_cp(("arbitrary",)))(u, u, u, cos4, sin4, qw2, qw2p, kw2, kw2p, sink_e, sink_o, bd, dy)


def _ssd_consts():
    hh = jnp.arange(128)[:, None]
    e = (hh == (jnp.arange(SSD_W)[None, :] // HP)).astype(BF16)
    e2 = (hh == (jnp.arange(NH * 128)[None, :] // 128)).astype(BF16)
    et = e.T
    tril = (jnp.arange(Q)[:, None] >= jnp.arange(Q)[None, :]).astype(BF16)
    triu = tril.T
    eye = jnp.eye(128, dtype=BF16)
    return e, e2, et, tril, triu, eye


def _ssd_common(x_ref, ext_scr, cw_ref, cb_ref, dt_ref, dtb_ref, alog_ref, e_ref, e2_ref, tril_ref, triu_ref,
                arow_scr, acol_scr, eax_scr):
    conv = jnp.broadcast_to(cb_ref[...], (Q, XBC))
    for k in range(4):
        conv = conv + cw_ref[k:k + 1, :] * ext_scr[pl.ds(5 + k, Q), :]
    sg = _sigmoid(conv)
    xbc = conv * sg
    dtpre = dt_ref[...] + dtb_ref[...]
    dt = _softplus(dtpre)
    a = -jnp.exp(alog_ref[...])
    adt = dt * a
    acol = _xdot_r(tril_ref[...], adt)
    acol_scr[...] = acol
    arow_scr[...] = _xdot_l(adt, triu_ref[...], 0, 0)
    alast = acol_scr[Q - 1:Q, :]
    ea = jnp.exp(acol)
    decs = jnp.exp(alast - acol)
    e = e_ref[...]
    dt_x = _xdot_l(dt, e)
    eax_scr[...] = _xdot_l(ea, e)
    decs_x = _xdot_l(decs, e)
    acx2 = _xdot_l(acol, e2_ref[...])
    return conv, sg, xbc, dtpre, dt, a, adt, acol, alast, ea, decs, dt_x, decs_x, acx2


def _ssd_fwd(u, cw, cb, dtb, alog, dxp, nw, consts, name):
    e, e2, et, tril, triu, eye = consts

    def body(z0_ref, z1_ref, x_ref, dt_ref, cw_ref, cb_ref, dtb_ref, alog_ref, dx_ref, nw_ref, e_ref, e2_ref,
             tril_ref, triu_ref, ya_ref, ypre_ref, st_ref, s_scr, ext_scr, arow_scr, acol_scr, eax_scr):
        c = pl.program_id(0)

        @pl.when(c == 0)
        def _():
            s_scr[...] = jnp.zeros_like(s_scr)
            ext_scr[0:8, :] = jnp.zeros((8, XBC), F32)
        ext_scr[8:8 + Q, :] = x_ref[...]
        (conv, sg, xbc, dtpre, dt, a, adt, acol, alast, ea, decs, dt_x, decs_x, acx2) = _ssd_common(
            x_ref, ext_scr, cw_ref, cb_ref, dt_ref, dtb_ref, alog_ref, e_ref, e2_ref, tril_ref, triu_ref,
            arow_scr, acol_scr, eax_scr)
        ext_scr[0:8, :] = ext_scr[Q:Q + 8, :]
        xs = xbc[:, :SSD_W]
        xdt = xs * dt_x
        lane = lax.broadcasted_iota(jnp.int32, (Q, 128), 1)
        causal = lax.broadcasted_iota(jnp.int32, (Q, Q), 0) >= lax.broadcasted_iota(jnp.int32, (Q, Q), 1)
        for g in range(2):
            bg = xbc[:, SSD_W + g * NS:SSD_W + (g + 1) * NS].astype(BF16)
            cg = xbc[:, SSD_W + 2 * NS + g * NS:SSD_W + 2 * NS + (g + 1) * NS].astype(BF16)
            cbm = _dg(cg, bg, 1, 1)
            sgv = s_scr[g]
            st_ref[0, g] = sgv
            gc = slice(g * 512, (g + 1) * 512)
            yoff = _dg(cg, sgv.astype(BF16), 1, 0) * eax_scr[:, gc]
            for pr in range(4):
                h0 = g * 8 + 2 * pr
                h1 = h0 + 1
                c0 = g * 512 + pr * 128
                xp = xdt[:, c0:c0 + 128].astype(BF16)
                w0 = (cbm * jnp.exp(jnp.where(causal, acx2[:, h0 * 128:(h0 + 1) * 128] - arow_scr[h0:h0 + 1, :],
                                              NEG))).astype(BF16)
                w1 = (cbm * jnp.exp(jnp.where(causal, acx2[:, h1 * 128:(h1 + 1) * 128] - arow_scr[h1:h1 + 1, :],
                                              NEG))).astype(BF16)
                yd = jnp.where(lane < HP, _dg(w0, xp, 1, 0), _dg(w1, xp, 1, 0))
                ypre_ref[:, c0:c0 + 128] = (yd + yoff[:, pr * 128:(pr + 1) * 128]
                                            + xs[:, c0:c0 + 128] * dx_ref[:, c0:c0 + 128])
            contrib = _dg(bg, (xdt[:, gc] * decs_x[:, gc]).astype(BF16), 0, 0)
            s_scr[g] = sgv * eax_scr[Q - 1:Q, gc] + contrib
        for g, zr in enumerate((z0_ref, z1_ref)):
            gc = slice(g * 512, (g + 1) * 512)
            zz = zr[...]
            ggg = ypre_ref[:, gc] * (zz * _sigmoid(zz))
            rstd = lax.rsqrt(jnp.mean(ggg * ggg, axis=-1, keepdims=True) + RMS_EPS)
            ya_ref[:, gc] = (ggg * rstd * nw_ref[:, gc]).astype(BF16)

    def row(w, blk=0):
        return pl.BlockSpec((Q, w), lambda c: (c, blk))

    def full(shape):
        return pl.BlockSpec(shape, lambda c: (0,) * len(shape))

    return pl.pallas_call(
        body, name=name, grid=(NC,),
        in_specs=[row(512, U_Z // 512), row(512, U_Z // 512 + 1), row(XBC, U_X // XBC), row(128, U_DT // 128),
                  full((4, XBC)), full((1, XBC)), full((1, 128)), full((1, 128)),
                  full((1, SSD_W)), full((1, SSD_W)), full((128, SSD_W)), full((128, NH * 128)), full((Q, Q)),
                  full((Q, Q))],
        out_specs=[row(SSD_W), row(SSD_W), pl.BlockSpec((1, 2, NS, 512), lambda c: (c, 0, 0, 0))],
        out_shape=[jax.ShapeDtypeStruct((L, SSD_W), BF16), jax.ShapeDtypeStruct((L, SSD_W), F32),
                   jax.ShapeDtypeStruct((NC, 2, NS, 512), F32)],
        scratch_shapes=[pltpu.VMEM((2, NS, 512), F32), pltpu.VMEM((Q + 8, XBC), F32), pltpu.VMEM((128, Q), F32),
                        pltpu.VMEM((Q, 128), F32), pltpu.VMEM((Q, SSD_W), F32)],
        compiler_params=_cp(("arbitrary",)))(u, u, u, u, cw, cb, dtb, alog, dxp, nw, e, e2, tril, triu)


def _ssd_bwd(u, ypre, st, dy, cw, cb, dtb, alog, dxp, nw, consts, name):
    e, e2, et, tril, triu, eye = consts

    def body(z0_ref, z1_ref, x_ref, xp_ref, dt_ref, ypre_ref, st_ref, dya_ref, cw_ref, cb_ref, dtb_ref, alog_ref, dx_ref,
             nw_ref, e_ref, e2_ref, et_ref, tril_ref, triu_ref, eye_ref,
             dz_ref, dxr_ref, ddtr_ref, dcw_ref, dcb_ref, ddtb_ref, dalog_ref, dd_ref, dnw_ref,
             g_scr, ext_scr, ext2_scr, arow_scr, acol_scr, eax_scr, darow_scr, dxdt_scr, t1_scr, t2_scr, dgg_scr):
        i = pl.program_id(0)

        @pl.when(i == 0)
        def _():
            g_scr[...] = jnp.zeros_like(g_scr)
            ext2_scr[Q:Q + 8, :] = jnp.zeros((8, XBC), F32)
            for r in (dcw_ref, dcb_ref, ddtb_ref, dalog_ref, dd_ref, dnw_ref):
                r[...] = jnp.zeros_like(r)
        not_first = jnp.where(i < NC - 1, 1.0, 0.0)
        ext_scr[0:8, :] = xp_ref[Q - 8:Q, :] * not_first
        ext_scr[8:8 + Q, :] = x_ref[...]
        (conv, sg, xbc, dtpre, dt, a, adt, acol, alast, ea, decs, dt_x, decs_x, acx2) = _ssd_common(
            x_ref, ext_scr, cw_ref, cb_ref, dt_ref, dtb_ref, alog_ref, e_ref, e2_ref, tril_ref, triu_ref,
            arow_scr, acol_scr, eax_scr)
        et_m = et_ref[...]
        xs = xbc[:, :SSD_W]
        xdt = xs * dt_x
        y = ypre_ref[...]
        zz = jnp.concatenate([z0_ref[...], z1_ref[...]], axis=1)
        sz = _sigmoid(zz)
        silu_z = zz * sz
        gg = y * silu_z
        dya = dya_ref[...]
        for g in range(2):
            gc = slice(g * 512, (g + 1) * 512)
            ggg = gg[:, gc]
            rstd = lax.rsqrt(jnp.mean(ggg * ggg, axis=-1, keepdims=True) + RMS_EPS)
            n = ggg * rstd
            dyag = dya[:, gc]
            dnw_ref[:, gc] += jnp.sum(dyag * n, axis=0, keepdims=True)
            dn = dyag * nw_ref[:, gc]
            dgg_scr[:, gc] = rstd * (dn - n * jnp.mean(dn * n, axis=-1, keepdims=True))
        dgg = dgg_scr[...]
        dy = dgg * silu_z
        dz_ref[...] = (dgg * y * (sz * (1.0 + zz * (1.0 - sz)))).astype(BF16)
        dd_ref[...] += _rowdot(jnp.sum(dy * xs, axis=0, keepdims=True), et_m)
        dxs = dy * dx_ref[...]
        dys = dy * eax_scr[...]
        lane = lax.broadcasted_iota(jnp.int32, (Q, 128), 1)
        causal = lax.broadcasted_iota(jnp.int32, (Q, Q), 0) >= lax.broadcasted_iota(jnp.int32, (Q, Q), 1)
        darow_scr[...] = jnp.zeros_like(darow_scr)
        dacol = jnp.zeros((Q, 128), F32)
        dcdx = []
        dbs = []
        dcs = []
        for g in range(2):
            gc = slice(g * 512, (g + 1) * 512)
            bg = xbc[:, SSD_W + g * NS:SSD_W + (g + 1) * NS].astype(BF16)
            cg = xbc[:, SSD_W + 2 * NS + g * NS:SSD_W + 2 * NS + (g + 1) * NS].astype(BF16)
            cbm = _dg(cg, bg, 1, 1)
            sgv = st_ref[0, g]
            sgb = sgv.astype(BF16)
            gv = g_scr[g]
            gvb = gv.astype(BF16)
            yoff = _dg(cg, sgb, 1, 0) * eax_scr[:, gc]
            dysg = dys[:, gc].astype(BF16)
            dcg = _dg(dysg, sgb, 1, 1)
            ds_off = _dg(cg, dysg, 0, 0)
            t1_scr[:, gc] = dy[:, gc] * yoff
            xdec = xdt[:, gc] * decs_x[:, gc]
            dxd = _dg(bg, gvb, 1, 0)
            dbg = _dg(xdec.astype(BF16), gvb, 1, 1)
            dxdt_g = dxd * decs_x[:, gc]
            t2_scr[:, gc] = dxd * xdt[:, gc]
            cdx = eax_scr[Q - 1:Q, gc]
            dcdx.append(jnp.sum(gv * sgv, axis=0, keepdims=True))
            g_scr[g] = gv * cdx + ds_off
            dcb_acc = jnp.zeros((Q, Q), F32)
            for pr in range(4):
                c0 = g * 512 + pr * 128
                xp = xdt[:, c0:c0 + 128].astype(BF16)
                dyp = dy[:, c0:c0 + 128]
                dypb = dyp.astype(BF16)
                halves = []
                for hh, keep in ((g * 8 + 2 * pr, lane < HP), (g * 8 + 2 * pr + 1, lane >= HP)):
                    lam = jnp.exp(jnp.where(causal, acx2[:, hh * 128:(hh + 1) * 128] - arow_scr[hh:hh + 1, :], NEG))
                    w = cbm * lam
                    dw = _dg(jnp.where(keep, dyp, 0.0).astype(BF16), xp, 1, 1)
                    dcb_acc = dcb_acc + dw * lam
                    t = dw * w
                    dacol = dacol + jnp.sum(t, axis=-1, keepdims=True) * (lane == hh).astype(F32)
                    darow_scr[hh:hh + 1, :] -= jnp.sum(t, axis=0, keepdims=True)
                    halves.append(_dg(w.astype(BF16), dypb, 0, 0))
                dxdt_scr[:, c0:c0 + 128] = (jnp.where(lane < HP, halves[0], halves[1])
                                            + dxdt_g[:, pr * 128:(pr + 1) * 128])
            dcbb = dcb_acc.astype(BF16)
            dcs.append(dcg + _dg(dcbb, bg, 1, 0))
            dbs.append(dbg + _dg(dcbb, cg, 0, 0))
        dacol = dacol + _xdot_l(t1_scr[...], et_m)
        ddecs = _xdot_l(t2_scr[...], et_m) * decs
        dacol = dacol - ddecs
        dalast = jnp.sum(ddecs, axis=0, keepdims=True)
        dcd = _rowdot(jnp.concatenate(dcdx, axis=1), et_m)
        dalast = dalast + dcd * jnp.exp(alast)
        dacol = dacol + _xdot_l(darow_scr[...], eye_ref[...], 0, 0)
        rowi = lax.broadcasted_iota(jnp.int32, (Q, 128), 0)
        dacol = dacol + jnp.where(rowi == Q - 1, dalast, 0.0)
        dadt = _xdot_r(triu_ref[...], dacol)
        dxdt = dxdt_scr[...]
        ddt = dadt * a + _xdot_l(dxdt * xs, et_m)
        dalog_ref[...] += jnp.sum(dadt * dt, axis=0, keepdims=True) * a
        dxs = dxs + dxdt * dt_x
        ddtr = ddt * _sigmoid(dtpre)
        ddtb_ref[...] += jnp.sum(ddtr, axis=0, keepdims=True)
        ddtr_ref[...] = ddtr.astype(BF16)
        dsilu = sg * (1.0 + conv * (1.0 - sg))
        ext2_scr[0:Q, 0:SSD_W] = dxs * dsilu[:, :SSD_W]
        for g in range(2):
            o1 = SSD_W + g * NS
            o2 = SSD_W + 2 * NS + g * NS
            ext2_scr[0:Q, o1:o1 + NS] = dbs[g] * dsilu[:, o1:o1 + NS]
            ext2_scr[0:Q, o2:o2 + NS] = dcs[g] * dsilu[:, o2:o2 + NS]
        dconv = ext2_scr[0:Q, :]
        dcb_ref[...] += jnp.sum(dconv, axis=0, keepdims=True)
        dxr = jnp.zeros((Q, XBC), F32)
        for k in range(4):
            dcw_ref[k:k + 1, :] += jnp.sum(dconv * ext_scr[pl.ds(5 + k, Q), :], axis=0, keepdims=True)
            dxr = dxr + cw_ref[k:k + 1, :] * ext2_scr[pl.ds(3 - k, Q), :]
        dxr_ref[...] = dxr.astype(BF16)
        ext2_scr[Q:Q + 8, :] = ext2_scr[0:8, :]

    def row(w, blk=0):
        return pl.BlockSpec((Q, w), lambda i: (NC - 1 - i, blk))

    def full(shape):
        return pl.BlockSpec(shape, lambda i: (0,) * len(shape))

    prev = pl.BlockSpec((Q, XBC), lambda i: (jnp.maximum(NC - 2 - i, 0), U_X // XBC))
    return pl.pallas_call(
        body, name=name, grid=(NC,),
        in_specs=[row(512, U_Z // 512), row(512, U_Z // 512 + 1), row(XBC, U_X // XBC), prev, row(128, U_DT // 128),
                  row(SSD_W),
                  pl.BlockSpec((1, 2, NS, 512), lambda i: (NC - 1 - i, 0, 0, 0)), row(SSD_W),
                  full((4, XBC)), full((1, XBC)), full((1, 128)), full((1, 128)), full((1, SSD_W)),
                  full((1, SSD_W)), full((128, SSD_W)), full((128, NH * 128)), full((SSD_W, 128)), full((Q, Q)),
                  full((Q, Q)), full((128, 128))],
        out_specs=[row(SSD_W), row(XBC), row(128), full((8, XBC)), full((1, XBC)), full((1, 128)), full((1, 128)),
                   full((1, 128)), full((1, SSD_W))],
        out_shape=[jax.ShapeDtypeStruct((L, SSD_W), BF16), jax.ShapeDtypeStruct((L, XBC), BF16),
                   jax.ShapeDtypeStruct((L, 128), BF16), jax.ShapeDtypeStruct((8, XBC), F32),
                   jax.ShapeDtypeStruct((1, XBC), F32), jax.ShapeDtypeStruct((1, 128), F32),
                   jax.ShapeDtypeStruct((1, 128), F32), jax.ShapeDtypeStruct((1, 128), F32),
                   jax.ShapeDtypeStruct((1, SSD_W), F32)],
        scratch_shapes=[pltpu.VMEM((2, NS, 512), F32), pltpu.VMEM((Q + 8, XBC), F32), pltpu.VMEM((Q + 8, XBC), F32),
                        pltpu.VMEM((128, Q), F32), pltpu.VMEM((Q, 128), F32), pltpu.VMEM((Q, SSD_W), F32),
                        pltpu.VMEM((128, Q), F32), pltpu.VMEM((Q, SSD_W), F32), pltpu.VMEM((Q, SSD_W), F32),
                        pltpu.VMEM((Q, SSD_W), F32), pltpu.VMEM((Q, SSD_W), F32)],
        compiler_params=_cp(("arbitrary",)))(u, u, u, u, u, ypre, st, dy, cw, cb, dtb, alog, dxp, nw,
                                             e, e2, et, tril, triu, eye)


def _my_pos():
    return lax.axis_index("x"), lax.axis_index("y"), lax.axis_index("c")


CHIP_REL = ((1, 0), (0, 1), (1, 1))
CHIP_XOR = (2, 1, 3)
BIG = ("w_in", "w_out", "w_mlp_up", "w_mlp_down")
NW = len(BIG)
AT = 256


def _chips(x, y):
    return [(1 - x if dx else x, 1 - y if dy else y) for dx, dy in CHIP_REL]


HBM_SPEC = pl.BlockSpec(memory_space=pltpu.HBM)
SEM_SPEC = pl.BlockSpec(memory_space=pltpu.SEMAPHORE)
EFFECT = pltpu.SideEffectType.DATAFLOW_SIDE_EFFECTING


def _hbm(t):
    return pltpu.with_memory_space_constraint(t, pltpu.HBM)


def _split_start(srcs, lands, after, copies, name):
    n = len(srcs)

    def body(*refs):
        src_refs, land_refs = refs[:n], refs[n:2 * n]
        send_sems, recv_sems = refs[2 * n + 1], refs[2 * n + 2]
        token = refs[-1]
        for w, k, src, dst, dev in copies(src_refs, land_refs):
            pltpu.make_async_remote_copy(src_ref=src, dst_ref=dst, send_sem=send_sems.at[3 * w + k],
                                         recv_sem=recv_sems.at[3 * w + k], device_id=dev, device_id_type=MESH).start()
        token[...] = jnp.zeros_like(token)

    outs = pl.pallas_call(
        body, name=name,
        out_shape=(pltpu.SemaphoreType.DMA((3 * n,)), pltpu.SemaphoreType.DMA((3 * n,)),
                   *[pltpu.HBM(t.shape, t.dtype) for t in srcs], *[pltpu.HBM(t.shape, t.dtype) for t in lands],
                   jax.ShapeDtypeStruct((8, 128), F32)),
        in_specs=[HBM_SPEC] * (2 * n) + [ANY],
        out_specs=(SEM_SPEC, SEM_SPEC, *([HBM_SPEC] * (2 * n)), VMEM_SPEC),
        input_output_aliases={i: 2 + i for i in range(2 * n)},
        compiler_params=pltpu.CompilerParams(has_side_effects=EFFECT))(
            *[_hbm(t) for t in srcs], *[_hbm(t) for t in lands], after)
    return outs[0], outs[1], list(outs[2:2 + n]), list(outs[2 + n:2 + 2 * n]), outs[-1]


def _split_wait(send_sems, recv_sems, srcs, lands, after, copies, name):
    n = len(srcs)

    def body(*refs):
        src_refs, land_refs = refs[:n], refs[n:2 * n]
        ssem, rsem = refs[2 * n], refs[2 * n + 1]
        for w, k, src, dst, dev in copies(src_refs, land_refs):
            cp = pltpu.make_async_remote_copy(src_ref=src, dst_ref=dst, send_sem=ssem.at[3 * w + k],
                                              recv_sem=rsem.at[3 * w + k], device_id=dev, device_id_type=MESH)
            cp.wait_send()
            cp.wait_recv()

    outs = pl.pallas_call(
        body, name=name,
        out_shape=tuple([pltpu.HBM(t.shape, t.dtype) for t in srcs] + [pltpu.HBM(t.shape, t.dtype) for t in lands]),
        in_specs=[HBM_SPEC] * (2 * n) + [SEM_SPEC, SEM_SPEC, ANY],
        out_specs=tuple([HBM_SPEC] * (2 * n)),
        input_output_aliases={i: i for i in range(2 * n)},
        compiler_params=pltpu.CompilerParams(has_side_effects=EFFECT))(*srcs, *lands, send_sems, recv_sems, after)
    return list(outs[:n]), list(outs[n:])


def _ag_copies(arrival):
    def copies(src_refs, land_refs):
        x, y, c = _my_pos()
        s = 2 * x + y
        chips = _chips(x, y)
        for w in range(len(src_refs)):
            hr = src_refs[w].shape[0] // 2
            mine = pl.ds(c * hr, hr)
            for k in range(3):
                slot = s ^ CHIP_XOR[k] if arrival else s
                yield w, k, src_refs[w].at[mine], land_refs[w].at[slot, mine], (*chips[k], c)
    return copies


def _ag_forward(lands, name):
    n = len(lands)

    def body(*refs):
        outs = refs[n:2 * n]
        send_sems, recv_sems = refs[2 * n:]
        x, y, c = _my_pos()
        s = 2 * x + y
        sib = (x, y, 1 - c)
        sends = []
        for w in range(n):
            hr = outs[w].shape[1] // 2
            for k in range(3):
                blk = outs[w].at[s ^ CHIP_XOR[k], pl.ds(c * hr, hr)]
                fw = pltpu.make_async_remote_copy(
                    src_ref=blk, dst_ref=blk, send_sem=send_sems.at[w, k], recv_sem=recv_sems.at[w, k],
                    device_id=sib, device_id_type=MESH)
                fw.start()
                sends.append(fw)
        for w in range(n):
            hr = outs[w].shape[1] // 2
            for k in range(3):
                blk = outs[w].at[s ^ CHIP_XOR[k], pl.ds((1 - c) * hr, hr)]
                pltpu.make_async_remote_copy(
                    src_ref=blk, dst_ref=blk, send_sem=send_sems.at[w, k], recv_sem=recv_sems.at[w, k],
                    device_id=sib, device_id_type=MESH).wait_recv()
        for cp in sends:
            cp.wait_send()

    return pl.pallas_call(
        body, name=name, in_specs=[ANY] * n, out_specs=[ANY] * n,
        out_shape=[jax.ShapeDtypeStruct(t.shape, t.dtype) for t in lands],
        input_output_aliases={w: w for w in range(n)},
        scratch_shapes=[pltpu.SemaphoreType.DMA((n, 3)), pltpu.SemaphoreType.DMA((n, 3))])(*lands)


def _rs_copies(src_refs, land_refs):
    x, y, c = _my_pos()
    s = 2 * x + y
    chips = _chips(x, y)
    for w in range(len(src_refs)):
        for k in range(3):
            yield w, k, src_refs[w].at[s ^ CHIP_XOR[k]], land_refs[w].at[k], (*chips[k], c)


def _place_own(shard, gathered, sidx, name):
    r, cc = shard.shape

    def body(s_ref, a_ref, g_ref, o_ref):
        o_ref[0] = a_ref[...]

    return pl.pallas_call(
        body, name=name,
        grid_spec=pltpu.PrefetchScalarGridSpec(
            num_scalar_prefetch=1, grid=(r // AT,),
            in_specs=[pl.BlockSpec((AT, cc), lambda i, s_ref: (i, 0)), ANY],
            out_specs=pl.BlockSpec((1, AT, cc), lambda i, s_ref: (s_ref[0], i, 0))),
        out_shape=jax.ShapeDtypeStruct(gathered.shape, gathered.dtype),
        input_output_aliases={2: 0}, compiler_params=_cp(("parallel",)))(sidx, shard, gathered)


def _rs_pair(dwb, name):
    n = len(dwb)

    def body(*refs):
        ins, outs = refs[:n], refs[n:2 * n]
        send_sems, recv_sems = refs[2 * n:]
        x, y, c = _my_pos()
        cps = []
        for w in range(n):
            hr = ins[w].shape[1] // 2
            cp = pltpu.make_async_remote_copy(
                src_ref=ins[w].at[:, pl.ds((1 - c) * hr, hr)], dst_ref=outs[w], send_sem=send_sems.at[w],
                recv_sem=recv_sems.at[w], device_id=(x, y, 1 - c), device_id_type=MESH)
            cp.start()
            cps.append(cp)
        for cp in cps:
            cp.wait()

    return pl.pallas_call(
        body, name=name, in_specs=[ANY] * n, out_specs=[ANY] * n,
        out_shape=[jax.ShapeDtypeStruct((4, t.shape[1] // 2, t.shape[2]), t.dtype) for t in dwb],
        scratch_shapes=[pltpu.SemaphoreType.DMA((n,)), pltpu.SemaphoreType.DMA((n,))])(*dwb)


def _rs_sib(q, name):
    n = len(q)

    def body(*refs):
        outs = refs[n:2 * n]
        send_sems, recv_sems = refs[2 * n:]
        x, y, c = _my_pos()
        cps = []
        for w in range(n):
            hr = outs[w].shape[0] // 2
            mine = pl.ds(c * hr, hr)
            cp = pltpu.make_async_remote_copy(
                src_ref=outs[w].at[mine], dst_ref=outs[w].at[mine], send_sem=send_sems.at[w],
                recv_sem=recv_sems.at[w], device_id=(x, y, 1 - c), device_id_type=MESH)
            cp.start()
            cps.append(cp)
        for w in range(n):
            hr = outs[w].shape[0] // 2
            other = outs[w].at[pl.ds((1 - c) * hr, hr)]
            pltpu.make_async_remote_copy(
                src_ref=other, dst_ref=other, send_sem=send_sems.at[w], recv_sem=recv_sems.at[w],
                device_id=(x, y, 1 - c), device_id_type=MESH).wait_recv()
        for cp in cps:
            cp.wait_send()

    return pl.pallas_call(
        body, name=name, in_specs=[ANY] * n, out_specs=[ANY] * n,
        out_shape=[jax.ShapeDtypeStruct(t.shape, t.dtype) for t in q],
        input_output_aliases={w: w for w in range(n)},
        scratch_shapes=[pltpu.SemaphoreType.DMA((n,)), pltpu.SemaphoreType.DMA((n,))])(*q)


def _rs_add2(dw, got, cidx, name):
    _, r, cc = dw.shape
    hr = r // 2
    nb = hr // AT

    def body(c_ref, a_ref, b_ref, o_ref, ob_ref):
        acc = a_ref[...] + b_ref[...].astype(F32)
        o_ref[...] = acc
        ob_ref[...] = acc.astype(BF16)

    blk = pl.BlockSpec((1, AT, cc), lambda sh, i, c_ref: (sh, i, 0))
    return pl.pallas_call(
        body, name=name,
        grid_spec=pltpu.PrefetchScalarGridSpec(
            num_scalar_prefetch=1, grid=(4, nb),
            in_specs=[pl.BlockSpec((1, AT, cc), lambda sh, i, c_ref: (sh, c_ref[0] * nb + i, 0)), blk],
            out_specs=[blk, blk]),
        out_shape=[jax.ShapeDtypeStruct((4, hr, cc), F32), jax.ShapeDtypeStruct((4, hr, cc), BF16)],
        compiler_params=_cp(("parallel", "parallel")))(cidx, dw, got)


def _rs_add4(p, got, scidx, name):
    _, hr, cc = p.shape
    nb = hr // AT

    def body(s_ref, p_ref, g0_ref, g1_ref, g2_ref, o_ref):
        acc = p_ref[0] + g0_ref[0].astype(F32)
        acc = acc + g1_ref[0].astype(F32)
        o_ref[...] = acc + g2_ref[0].astype(F32)

    def gk(k):
        return pl.BlockSpec((1, AT, cc), lambda i, s_ref: (k, i, 0))

    return pl.pallas_call(
        body, name=name,
        grid_spec=pltpu.PrefetchScalarGridSpec(
            num_scalar_prefetch=1, grid=(nb,),
            in_specs=[pl.BlockSpec((1, AT, cc), lambda i, s_ref: (s_ref[0], i, 0)), gk(0), gk(1), gk(2)],
            out_specs=pl.BlockSpec((AT, cc), lambda i, s_ref: (s_ref[1] * nb + i, 0))),
        out_shape=jax.ShapeDtypeStruct((2 * hr, cc), F32),
        compiler_params=_cp(("parallel",)))(scidx, p, got, got, got)


def _rs_begin(dws, dwbs, after, tag=""):
    _, _, c = _my_pos()
    cidx = jnp.reshape(c, (1,)).astype(jnp.int32)
    got = _rs_pair(dwbs, "rs_pair" + tag)
    pairs = [_rs_add2(dws[w], got[w], cidx, "rs_add2") for w in range(len(dws))]
    pb = [p[1] for p in pairs]
    lands = [lax.empty((3,) + t.shape[1:], BF16) for t in pb]
    ssem, rsem, pb, lands, token = _split_start(pb, lands, after, _rs_copies, "rs_chip_start" + tag)
    return ([p[0] for p in pairs], ssem, rsem, pb, lands), token


def _rs_end(state, after, tag=""):
    x, y, c = _my_pos()
    scidx = jnp.stack([2 * x + y, c]).astype(jnp.int32)
    p, ssem, rsem, pb, lands = state
    _, recv = _split_wait(ssem, rsem, pb, lands, after, _rs_copies, "rs_chip_wait" + tag)
    q = [_rs_add4(p[w], recv[w], scidx, "rs_add4") for w in range(len(p))]
    return _rs_sib(q, "rs_sib" + tag)


def _allreduce_small(buf, name):
    rows = buf.shape[0]

    def body(src_ref, out_ref, gat_ref, send_sems, recv_sems):
        x, y, c = _my_pos()
        me = 4 * x + 2 * y + c
        gat_ref[me] = src_ref[...]
        cps = []
        for r in range(1, N_DEV):
            tx = 1 - x if (r >> 2) & 1 else x
            ty = 1 - y if (r >> 1) & 1 else y
            tc = 1 - c if r & 1 else c
            cps.append(pltpu.make_async_remote_copy(
                src_ref=src_ref, dst_ref=gat_ref.at[me], send_sem=send_sems.at[r - 1], recv_sem=recv_sems.at[r - 1],
                device_id=(tx, ty, tc), device_id_type=MESH))
        for cp in cps:
            cp.start()
        for cp in cps:
            cp.wait()
        acc = gat_ref[0]
        for k in range(1, N_DEV):
            acc = acc + gat_ref[k]
        out_ref[...] = acc

    return pl.pallas_call(
        body, name=name, in_specs=[VMEM_SPEC], out_specs=VMEM_SPEC, out_shape=jax.ShapeDtypeStruct((rows, 128), F32),
        scratch_shapes=[pltpu.VMEM((N_DEV, rows, 128), F32), pltpu.SemaphoreType.DMA((N_DEV - 1,)),
                        pltpu.SemaphoreType.DMA((N_DEV - 1,))],
        compiler_params=_cp())(buf)


SMALL = (("norm_mix_w", (D,)), ("ssd_conv_w", (4, XBC)), ("ssd_conv_b", (XBC,)), ("ssd_dt_bias", (NH,)),
         ("ssd_a_log", (NH,)), ("ssd_d", (NH,)), ("ssd_norm_w", (SSD_W,)), ("q_norm_w", (HD,)),
         ("k_norm_w", (HD,)), ("attn_sinks", (NQH,)), ("cm_dw_w", (CMK, CMC)), ("cm_dw_b", (CMC,)),
         ("cm_ln_w", (CMC,)), ("cm_ln_b", (CMC,)), ("norm_mlp_w", (D,)))
SHARDED_SMALL = ("ssd_conv_w", "cm_dw_w")


def _seg_len(shape):
    n = 1
    for d in shape:
        n *= d
    return -(-n // 128) * 128


def _pack_small(vals, names):
    parts = []
    for name, shape in SMALL:
        if name not in names:
            continue
        v = vals[name].reshape(DEPTH, -1)
        pad = _seg_len(shape) - v.shape[1]
        parts.append(jnp.pad(v, ((0, 0), (0, pad))))
    flat = jnp.concatenate(parts, axis=1)
    return flat.reshape(-1, 128)


def _unpack_small(buf, names):
    flat = buf.reshape(DEPTH, -1)
    out = {}
    off = 0
    for name, shape in SMALL:
        if name not in names:
            continue
        n = 1
        for d in shape:
            n *= d
        out[name] = flat[:, off:off + n].reshape((DEPTH,) + shape)
        off += _seg_len(shape)
    return out


SW = N_IN // 4
SWP = 1152
ORIG = (("z", 0, 1024), ("x", 1024, 2560), ("dt", 2560, 2576), ("q", 2576, 3088), ("k", 3088, 3216),
        ("v", 3216, 3344), ("a", 3344, 3856), ("g", 3856, 4368))


def _orig_cols(g_in, lo, hi):
    out = []
    for s in range(4):
        a, b = max(lo, s * SW), min(hi, (s + 1) * SW)
        if a < b:
            out.append(g_in[s][:, a - s * SW:b - s * SW])
    return out


def _shard_major(parts):
    cols = []
    for s in range(4):
        for name, g0, g1 in ORIG:
            a, b = max(g0, s * SW), min(g1, (s + 1) * SW)
            if a < b:
                cols.append(parts[name][:, a - g0:b - g0])
        cols.append(jnp.zeros((L, SWP - SW), BF16))
    return jnp.concatenate(cols, axis=1)


def _rope_tables():
    inv = 10000.0 ** (-jnp.arange(0, HD, 2, dtype=F32) / HD)
    ang = jnp.arange(L, dtype=F32)[:, None] * inv[None, :]
    return jnp.cos(ang), jnp.sin(ang)


def _swa_tables():
    cos, sin = _rope_tables()
    return jnp.tile(cos, (1, 4)), jnp.tile(jnp.concatenate([-sin, sin], axis=1), (1, 2))


def _swa_weights(w):
    return jnp.tile(w, 2)[None], jnp.tile(jnp.concatenate([w[HH:], w[:HH]]), 2)[None]


def _swa_sinks(s):
    s2 = s.reshape(NPAIR, 2)
    return (jnp.broadcast_to(s2[:, 0][:, None, None], (NPAIR, 1, 128)),
            jnp.broadcast_to(s2[:, 1][:, None, None], (NPAIR, 1, 128)))


def _swa_blockdiag():
    i = jnp.arange(128)
    return (i[:, None] // HD == i[None, :] // HD).astype(BF16)


def _pad128(v):
    return jnp.pad(v, (0, 128 - v.shape[0]))[None, :]


def kernel(x, norm_mix_w, w_in, ssd_conv_w, ssd_conv_b, ssd_dt_bias, ssd_a_log, ssd_d, ssd_norm_w, q_norm_w, k_norm_w, attn_sinks, cm_dw_w, cm_dw_b, cm_ln_w, cm_ln_b, w_out, norm_mlp_w, w_mlp_up, w_mlp_down, loss_target, m_norm_mix_w, m_w_in, m_ssd_conv_w, m_ssd_conv_b, m_ssd_dt_bias, m_ssd_a_log, m_ssd_d, m_ssd_norm_w, m_q_norm_w, m_k_norm_w, m_attn_sinks, m_cm_dw_w, m_cm_dw_b, m_cm_ln_w, m_cm_ln_b, m_w_out, m_norm_mlp_w, m_w_mlp_up, m_w_mlp_down, v_norm_mix_w, v_w_in, v_ssd_conv_w, v_ssd_conv_b, v_ssd_dt_bias, v_ssd_a_log, v_ssd_d, v_ssd_norm_w, v_q_norm_w, v_k_norm_w, v_attn_sinks, v_cm_dw_w, v_cm_dw_b, v_cm_ln_w, v_cm_ln_b, v_w_out, v_norm_mlp_w, v_w_mlp_up, v_w_mlp_down):
    px, py, pc = _my_pos()
    shard = 2 * px + py
    sidx = jnp.reshape(shard, (1,)).astype(jnp.int32)
    consts = _ssd_consts()
    cos4, sin4 = _swa_tables()
    bd = _swa_blockdiag()

    wb = [w.astype(BF16) for w in (jnp.pad(w_in, ((0, 0), (0, 0), (0, SWP - SW))), w_out, w_mlp_up, w_mlp_down)]

    def gather_start(l, sel, after, tag=""):
        own = [wb[w][l] for w in sel]
        lands = [lax.empty((4,) + t.shape, BF16) for t in own]
        return _split_start(own, lands, after, _ag_copies(False), "ag_start" + tag)

    def gather_finish(in_flight, after, tag=""):
        ssem, rsem, own, lands, _ = in_flight
        own, lands = _split_wait(ssem, rsem, own, lands, after, _ag_copies(True), "ag_wait" + tag)
        lands = _ag_forward(lands, "ag_forward" + tag)
        return [_place_own(o, g, sidx, "ag_place") for o, g in zip(own, lands)]

    zero_tile = jnp.zeros((8, 128), F32)
    first_in = gather_start(0, [0], zero_tile, "_in0")
    zc = jnp.zeros((DEPTH, 4, XBC), F32)
    zc = lax.dynamic_update_slice_in_dim(zc, ssd_conv_w, shard * (XBC // 4), axis=2)
    zd = jnp.zeros((DEPTH, CMK, CMC), F32)
    zd = lax.dynamic_update_slice_in_dim(zd, cm_dw_w, shard * (CMC // 4), axis=2)
    half = jnp.where(pc == 0, 1.0, 0.0).astype(F32)
    gw_packed = _allreduce_small(_pack_small({"ssd_conv_w": zc * half, "cm_dw_w": zd * half}, SHARDED_SMALL)
                                 + first_in[4][0:1, 0:1], "ag_small")
    first_rest = gather_start(0, [1, 2, 3], gw_packed, "_rest0")
    gw = _unpack_small(gw_packed, SHARDED_SMALL)
    conv_w_full, dw_w_full = gw["ssd_conv_w"], gw["cm_dw_w"]

    xcur = x[0]
    saved = []
    in_flight = None
    for l in range(DEPTH):
        if l == 0:
            (g_in,) = gather_finish(first_in, first_rest[4], "_in0")
            nxt_after = g_in
        else:
            g_in, g_out, g_up, g_dn = gather_finish(in_flight, xcur)
            nxt_after = g_dn
        nmw = norm_mix_w[l][None]
        if l + 1 < DEPTH:
            in_flight = gather_start(l + 1, [0, 1, 2, 3], nxt_after)
            nmw = nmw + in_flight[4][0:1, 0:1]
        grp = dict((n, (a, b)) for n, a, b in ORIG)
        w_perm = jnp.concatenate(
            _orig_cols(g_in, *grp["x"]) + _orig_cols(g_in, *grp["z"]) + _orig_cols(g_in, grp["a"][0], grp["g"][1])
            + _orig_cols(g_in, grp["q"][0], grp["v"][1]) + _orig_cols(g_in, *grp["dt"])
            + [jnp.zeros((D, 128 - NH), BF16)], axis=1)
        h = _rms_fwd(xcur, nmw, "rms_mix_fwd")
        u = _mm(h, w_perm, "nn", "in_proj", tn=640)
        alog = _pad128(ssd_a_log[l])
        dtb = _pad128(ssd_dt_bias[l])
        dxp = jnp.repeat(ssd_d[l], HP)[None, :]
        ssd_p = (conv_w_full[l], ssd_conv_b[l][None], dtb, alog, dxp, ssd_norm_w[l][None])
        ya, ypre, st = _ssd_fwd(u, *ssd_p, consts, "ssd_fwd")
        swa_p = (cos4, sin4, *_swa_weights(q_norm_w[l]), *_swa_weights(k_norm_w[l]), *_swa_sinks(attn_sinks[l]), bd)
        yb = _swa_fwd(u, *swa_p, "attn_fwd")
        cconv = _conf_conv_fwd(u, dw_w_full[l], cm_dw_b[l][None], "conf_conv_fwd")
        conf_p = (cconv, cm_ln_w[l][None], cm_ln_b[l][None])
        yc = _conf_ln_fwd(*conf_p, "conf_ln_fwd")
        if l == 0:
            g_out, g_up, g_dn = gather_finish(first_rest, yc, "_rest0")
        g_out = g_out.reshape(2 * D, D)
        g_dn = g_dn.reshape(DFF, D)
        ycat = jnp.concatenate([ya, yb, yc], axis=1)
        x1 = _mm(ycat, g_out, "nn", "out_proj", add=xcur)
        hm = _rms_fwd(x1, norm_mlp_w[l][None], "rms_mlp_fwd")
        r_up = _mm_up(hm, g_up, "mlp_up")
        x2 = _mm(r_up, g_dn, "nn", "mlp_down", add=x1)
        saved.append(dict(x=xcur, h=h, u=u, ypre=ypre, st=st, swa_p=swa_p, conf_p=conf_p, ycat=ycat, x1=x1,
                          hm=hm, r_up=r_up, ssd_p=ssd_p, g_in=g_in, g_out=g_out, g_up=g_up, g_dn=g_dn))
        xcur = x2

    lsum, dx, dxb = _loss_bwd(xcur, loss_target[0], "loss")

    loc = locals()
    names = ["norm_mix_w", "w_in", "ssd_conv_w", "ssd_conv_b", "ssd_dt_bias", "ssd_a_log", "ssd_d", "ssd_norm_w",
             "q_norm_w", "k_norm_w", "attn_sinks", "cm_dw_w", "cm_dw_b", "cm_ln_w", "cm_ln_b", "w_out", "norm_mlp_w",
             "w_mlp_up", "w_mlp_down"]
    weights = {n: loc[n] for n in names}
    moms = {n: loc["m_" + n] for n in names}
    vars_ = {n: loc["v_" + n] for n in names}
    big_out = {n: None for n in BIG}

    def finish_layer(layer, which, shard_grads):
        for n, g in zip(which, shard_grads):
            g = g[:, :SW] if n == "w_in" else g
            big_out[n] = _adamw_layer(weights[n], g, moms[n], vars_[n], layer, big_out[n], "adamw_" + n)

    pending = None
    last_mlp = None
    gsm = {name: [] for name, _ in SMALL}
    for l in reversed(range(DEPTH)):
        sv = saved[l]
        da = _mm(dxb, sv["g_dn"], "nt", "mlp_down_dx", relu2_of=sv["r_up"])
        dwdn, dwdn_b = _mm_dw(sv["r_up"], dxb, "mlp_down_dw")
        dwup, dwup_b = _mm_dw(sv["hm"], da, "mlp_up_dw", col_shards=True)
        dhm = _mm_cs_nt(da, sv["g_up"], "mlp_up_dx")
        nlw = norm_mlp_w[l][None]
        if l == 0:
            last_mlp, token = _rs_begin([dwup, dwdn.reshape(4, D, D)], [dwup_b, dwdn_b.reshape(4, D, D)], zero_tile,
                                        "_mlp0")
            nlw = nlw + token[0:1, 0:1]
        dx1, dx1b, dnw = _rms_bwd(sv["x1"], nlw, dhm, dx, "rms_mlp_bwd")
        gsm["norm_mlp_w"].append(dnw[0])
        dy = _mm(dx1b, sv["g_out"], "nt", "out_proj_dx")
        dwout, dwout_b = _mm_dw(sv["ycat"], dx1b, "out_proj_dw")
        dcc, dwb, dlw, dlb = _conf_ln_bwd(*sv["conf_p"], dy, "conf_ln_bwd")
        da_c, dg_c, dww = _conf_conv_bwd(sv["u"], dcc, dw_w_full[l], "conf_conv_bwd")
        gsm["cm_dw_w"].append(dww[:CMK])
        gsm["cm_dw_b"].append(dwb[0])
        gsm["cm_ln_w"].append(dlw[0])
        gsm["cm_ln_b"].append(dlb[0])
        dq, dk, dv, dqw, dkw, dse, dso = _swa_bwd(sv["u"], dy, *sv["swa_p"], "attn_bwd")
        gsm["q_norm_w"].append(dqw[0, :HD] + dqw[0, HD:])
        gsm["k_norm_w"].append(dkw[0, :HD] + dkw[0, HD:])
        gsm["attn_sinks"].append(jnp.stack([dse[:, 0, 0], dso[:, 0, 0]], axis=1).reshape(NQH))
        (dz, dxr, ddtr, dcw, dcb, ddtb, dalog, ddd, dnsw) = _ssd_bwd(
            sv["u"], sv["ypre"], sv["st"], dy, *sv["ssd_p"], consts, "ssd_bwd")
        gsm["ssd_conv_w"].append(dcw[:4])
        gsm["ssd_conv_b"].append(dcb[0])
        gsm["ssd_dt_bias"].append(ddtb[0, :NH])
        gsm["ssd_a_log"].append(dalog[0, :NH])
        gsm["ssd_d"].append(ddd[0, :NH])
        gsm["ssd_norm_w"].append(dnsw[0])
        du = _shard_major(dict(z=dz, x=dxr, dt=ddtr[:, :NH], q=dq, k=dk, v=dv, a=da_c, g=dg_c))
        dwin, dwin_b = _mm_dw(sv["h"], du, "in_dw", col_shards=True, tn=SWP // 3)
        if l == 0:
            state, token = _rs_begin([dwin, dwout.reshape(4, D // 2, D)], [dwin_b, dwout_b.reshape(4, D // 2, D)],
                                     zero_tile, "_io0")
        else:
            state, token = _rs_begin(
                [dwin, dwout.reshape(4, D // 2, D), dwup, dwdn.reshape(4, D, D)],
                [dwin_b, dwout_b.reshape(4, D // 2, D), dwup_b, dwdn_b.reshape(4, D, D)], zero_tile)
        dh = _mm_cs_nt(du, sv["g_in"], "in_dx")
        dx, dxb, dnm = _rms_bwd(sv["x"], norm_mix_w[l][None] + token[0:1, 0:1], dh, dx1, "rms_mix_bwd")
        gsm["norm_mix_w"].append(dnm[0])
        if pending is not None:
            finish_layer(l + 1, BIG, _rs_end(pending, dx))
        pending = state

    finish_layer(0, BIG[2:], _rs_end(last_mlp, dx, "_mlp0"))
    gsm = {k: jnp.stack(v[::-1]) for k, v in gsm.items()}
    packed = _pack_small(gsm, [n for n, _ in SMALL])
    packed = jnp.concatenate([packed, lsum], axis=0)
    red = _allreduce_small(packed, "ar_small")
    finish_layer(0, BIG[:2], _rs_end(pending, red, "_io0"))
    loss = 0.5 * red[-8, 0] / D
    gsm = _unpack_small(red[:-8], [n for n, _ in SMALL])
    gsm["ssd_conv_w"] = lax.dynamic_slice_in_dim(gsm["ssd_conv_w"], shard * (XBC // 4), XBC // 4, axis=2)
    gsm["cm_dw_w"] = lax.dynamic_slice_in_dim(gsm["cm_dw_w"], shard * (CMC // 4), CMC // 4, axis=2)
    grads = dict(gsm)
    delta, new_m, new_v = {}, {}, {}
    for n in BIG:
        grads[n], delta[n], new_m[n], new_v[n] = big_out[n]

    packed_names = [n for n, _ in SMALL if n not in SHARDED_SMALL]
    pw = _pack_small(weights, packed_names)
    pg = _pack_small(grads, packed_names)
    pm = _pack_small(moms, packed_names)
    pv = _pack_small(vars_, packed_names)
    pd, pmn, pvn = _adamw(pw, pg, pm, pv, "adamw_small")
    for dst, buf in ((delta, pd), (new_m, pmn), (new_v, pvn)):
        dst.update(_unpack_small(buf, packed_names))
    for n in SHARDED_SMALL:
        shp = weights[n].shape
        flat = lambda t: t.reshape(-1, shp[-1])
        d_, m_, v_ = _adamw(flat(weights[n]), flat(grads[n]), flat(moms[n]), flat(vars_[n]), "adamw_" + n)
        delta[n], new_m[n], new_v[n] = d_.reshape(shp), m_.reshape(shp), v_.reshape(shp)

    return (loss, dx[None], *[grads[n] for n in names], *[delta[n] for n in names],
            *[new_m[n] for n in names], *[new_v[n] for n in names])
```

```python
import functools
import math

import jax
import jax.numpy as jnp
from jax import lax
from jax.experimental import pallas as pl
from jax.experimental.pallas import tpu as pltpu

F32 = jnp.float32
BF16 = jnp.bfloat16
MESH = pl.DeviceIdType.MESH
ANY = pl.BlockSpec(memory_space=pl.ANY)
VMEM_SPEC = pl.BlockSpec(memory_space=pltpu.VMEM)

D = 1024
L = 2048
DEPTH = 4
SSD_W = 1024
XBC = 1536
NH = 16
HP = 64
NS = 128
Q = 128
NC = L // Q
ATT_W = 512
NQH = 8
NKV = 2
HD = 64
HH = HD // 2
CMC = 512
CMK = 31
DFF = 4096
N_IN = 4368
N_PAD = 4480
RMS_EPS = 1e-6
LN_EPS = 1e-5
NEG = -1e30
LR, B1, B2, EPS_A, WD, STEP = 0.001, 0.9, 0.999, 1e-8, 0.01, 10
VMEM_LIMIT = 56 * 1024 * 1024
N_DEV = 8


def _cp(sem=None):
    kw = dict(vmem_limit_bytes=VMEM_LIMIT)
    if sem is not None:
        kw["dimension_semantics"] = sem
    return pltpu.CompilerParams(**kw)


def _dg(a, b, ca, cb):
    return lax.dot_general(a, b, (((ca,), (cb,)), ((), ())), preferred_element_type=F32)


def _split3(x):
    hi = x.astype(BF16)
    r = x - hi.astype(F32)
    mid = r.astype(BF16)
    lo = (r - mid.astype(F32)).astype(BF16)
    return hi, mid, lo


def _xdot_l(x, m, ca=1, cb=0):
    hi, mid, lo = _split3(x)
    return _dg(hi, m, ca, cb) + _dg(mid, m, ca, cb) + _dg(lo, m, ca, cb)


def _xdot_r(m, x, ca=1, cb=0):
    hi, mid, lo = _split3(x)
    return _dg(m, hi, ca, cb) + _dg(m, mid, ca, cb) + _dg(m, lo, ca, cb)


def _rowdot(v, m):
    return _xdot_l(jnp.broadcast_to(v, (8, v.shape[1])), m)[0:1]


def _sigmoid(x):
    return 1.0 / (1.0 + jnp.exp(-x))


def _softplus(x):
    e = jnp.exp(-jnp.abs(x))
    u = 1.0 + e
    l1p = jnp.where(u == 1.0, e, jnp.log(u) * (e / jnp.where(u == 1.0, 1.0, u - 1.0)))
    return jnp.maximum(x, 0.0) + l1p


def _tm(k):
    return L if k <= D else L // 2


def _mm(a, b, mode, name, add=None, relu2_of=None, tn=512):
    m, k = a.shape
    tm = min(m, _tm(k))
    a_spec = pl.BlockSpec((tm, k), lambda i, j: (i, 0))
    if mode == "nn":
        n = b.shape[1]
        b_spec = pl.BlockSpec((k, tn), lambda i, j: (0, j))
        cb = 0
    else:
        n = b.shape[0]
        b_spec = pl.BlockSpec((tn, k), lambda i, j: (j, 0))
        cb = 1
    assert m % tm == 0 and n % tn == 0, (m, n, tm, tn)
    o_spec = pl.BlockSpec((tm, tn), lambda i, j: (i, j))
    out_dtype = F32
    if relu2_of is not None:
        def body(a_ref, b_ref, c_ref, o_ref):
            o_ref[...] = (_dg(a_ref[...], b_ref[...], 1, cb) * (2.0 * jnp.sqrt(c_ref[...].astype(F32)))).astype(BF16)
        ins, specs, out_dtype = (a, b, relu2_of), [a_spec, b_spec, o_spec], BF16
    elif add is None:
        def body(a_ref, b_ref, o_ref):
            o_ref[...] = _dg(a_ref[...], b_ref[...], 1, cb)
        ins, specs = (a, b), [a_spec, b_spec]
    else:
        def body(a_ref, b_ref, c_ref, o_ref):
            o_ref[...] = _dg(a_ref[...], b_ref[...], 1, cb) + c_ref[...]
        ins, specs = (a, b, add), [a_spec, b_spec, o_spec]
    return pl.pallas_call(
        body, name=name, grid=(m // tm, n // tn), in_specs=specs, out_specs=o_spec,
        out_shape=jax.ShapeDtypeStruct((m, n), out_dtype), compiler_params=_cp(("parallel", "parallel")))(*ins)


def _mm_up(a, b, name, tn=512):
    m = a.shape[0]
    cs = DFF // 4
    per = cs // tn
    tm = min(m, _tm(D))

    def body(a_ref, b_ref, r_ref):
        r = jnp.maximum(_dg(a_ref[...], b_ref[0], 1, 0), 0.0)
        r_ref[...] = (r * r).astype(BF16)

    return pl.pallas_call(
        body, name=name, grid=(m // tm, DFF // tn),
        in_specs=[pl.BlockSpec((tm, D), lambda i, j: (i, 0)),
                  pl.BlockSpec((1, D, tn), lambda i, j: (j // per, 0, j % per))],
        out_specs=pl.BlockSpec((tm, tn), lambda i, j: (i, j)),
        out_shape=jax.ShapeDtypeStruct((m, DFF), BF16), compiler_params=_cp(("parallel", "parallel")))(a, b)


def _mm_cs_nt(a, b, name, tn=512):
    m = a.shape[0]
    _, n, cs = b.shape
    tm = min(m, _tm(4 * cs))

    def body(a_ref, b_ref, o_ref):
        acc = _dg(a_ref[:, 0:cs], b_ref[0], 1, 1)
        for s in range(1, 4):
            acc = acc + _dg(a_ref[:, s * cs:(s + 1) * cs], b_ref[s], 1, 1)
        o_ref[...] = acc

    return pl.pallas_call(
        body, name=name, grid=(m // tm, n // tn),
        in_specs=[pl.BlockSpec((tm, 4 * cs), lambda i, j: (i, 0)), pl.BlockSpec((4, tn, cs), lambda i, j: (0, j, 0))],
        out_specs=pl.BlockSpec((tm, tn), lambda i, j: (i, j)),
        out_shape=jax.ShapeDtypeStruct((m, n), F32), compiler_params=_cp(("parallel", "parallel")))(a, b)


def _mm_dw(a, b, name, col_shards=False, tn=512):
    k, m = a.shape
    n = b.shape[1]
    tm = min(m, D)
    a_spec = pl.BlockSpec((k, tm), lambda i, j: (0, i))
    b_spec = pl.BlockSpec((k, tn), lambda i, j: (0, j))
    if col_shards:
        per = (n // 4) // tn
        o_spec = pl.BlockSpec((1, tm, tn), lambda i, j: (j // per, i, j % per))
        shape = (4, m, n // 4)
    else:
        o_spec = pl.BlockSpec((tm, tn), lambda i, j: (i, j))
        shape = (m, n)

    def body(a_ref, b_ref, o_ref, ob_ref):
        acc = _dg(a_ref[...], b_ref[...], 0, 0).reshape(o_ref.shape)
        o_ref[...] = acc
        ob_ref[...] = acc.astype(BF16)

    return pl.pallas_call(
        body, name=name, grid=(m // tm, n // tn), in_specs=[a_spec, b_spec], out_specs=[o_spec, o_spec],
        out_shape=[jax.ShapeDtypeStruct(shape, F32), jax.ShapeDtypeStruct(shape, BF16)],
        compiler_params=_cp(("parallel", "parallel")))(a, b)


TR = 256


def _rms_fwd(x, w, name):
    def body(x_ref, w_ref, o_ref):
        xv = x_ref[...]
        r = lax.rsqrt(jnp.mean(xv * xv, axis=-1, keepdims=True) + RMS_EPS)
        o_ref[...] = (xv * r * w_ref[...]).astype(BF16)

    return pl.pallas_call(
        body, name=name, grid=(L // TR,),
        in_specs=[pl.BlockSpec((TR, D), lambda i: (i, 0)), pl.BlockSpec((1, D), lambda i: (0, 0))],
        out_specs=pl.BlockSpec((TR, D), lambda i: (i, 0)),
        out_shape=jax.ShapeDtypeStruct((L, D), BF16), compiler_params=_cp(("parallel",)))(x, w)


def _rms_bwd(x, w, dh, dres, name):
    def body(x_ref, w_ref, dh_ref, dr_ref, dx_ref, dxb_ref, dw_ref):
        xv = x_ref[...]
        r = lax.rsqrt(jnp.mean(xv * xv, axis=-1, keepdims=True) + RMS_EPS)
        n = xv * r
        dhv = dh_ref[...]
        g = dhv * w_ref[...]
        dx = dr_ref[...] + r * (g - n * jnp.mean(g * n, axis=-1, keepdims=True))
        dx_ref[...] = dx
        dxb_ref[...] = dx.astype(BF16)

        @pl.when(pl.program_id(0) == 0)
        def _():
            dw_ref[...] = jnp.zeros_like(dw_ref)
        dw_ref[...] += jnp.sum(dhv * n, axis=0, keepdims=True)

    row = pl.BlockSpec((TR, D), lambda i: (i, 0))
    vec = pl.BlockSpec((1, D), lambda i: (0, 0))
    return pl.pallas_call(
        body, name=name, grid=(L // TR,), in_specs=[row, vec, row, row], out_specs=[row, row, vec],
        out_shape=[jax.ShapeDtypeStruct((L, D), F32), jax.ShapeDtypeStruct((L, D), BF16),
                   jax.ShapeDtypeStruct((1, D), F32)],
        compiler_params=_cp(("arbitrary",)))(x, w, dh, dres)


def _adamw_lead(w, g, m, v, name):
    lead, a, b = w.shape
    tr = max(t for t in range(1, lead + 1) if lead % t == 0 and t * a * b * 4 <= 2 * 1024 * 1024)
    c1 = 1.0 / (1.0 - B1 ** STEP)
    c2 = 1.0 / (1.0 - B2 ** STEP)

    def body(w_ref, g_ref, m_ref, v_ref, d_ref, mo_ref, vo_ref):
        gv = g_ref[...]
        mn = B1 * m_ref[...] + (1.0 - B1) * gv
        vn = B2 * v_ref[...] + (1.0 - B2) * (gv * gv)
        mo_ref[...] = mn
        vo_ref[...] = vn
        d_ref[...] = -LR * ((mn * c1) / (jnp.sqrt(vn * c2) + EPS_A) + WD * w_ref[...])

    blk = pl.BlockSpec((tr, a, b), lambda i: (i, 0, 0))
    shp = jax.ShapeDtypeStruct(w.shape, F32)
    return pl.pallas_call(body, name=name, grid=(lead // tr,), in_specs=[blk] * 4, out_specs=[blk] * 3,
                          out_shape=[shp] * 3, compiler_params=_cp(("parallel",)))(w, g, m, v)


def _adamw_layer(w, g, m, v, layer, prev, name):
    _, rows, cols = w.shape
    tr = 256 if cols * 256 * 4 <= 2 * 1024 * 1024 else 128
    c1 = 1.0 / (1.0 - B1 ** STEP)
    c2 = 1.0 / (1.0 - B2 ** STEP)
    n_prev = 0 if prev is None else 4

    def body(*refs):
        w_ref, g_ref, m_ref, v_ref = refs[:4]
        go_ref, d_ref, mo_ref, vo_ref = refs[4 + n_prev:]
        gv = g_ref[...]
        mn = B1 * m_ref[0] + (1.0 - B1) * gv
        vn = B2 * v_ref[0] + (1.0 - B2) * (gv * gv)
        go_ref[0] = gv
        mo_ref[0] = mn
        vo_ref[0] = vn
        d_ref[0] = -LR * ((mn * c1) / (jnp.sqrt(vn * c2) + EPS_A) + WD * w_ref[0])

    lay = pl.BlockSpec((1, tr, cols), lambda i: (layer, i, 0))
    shp = jax.ShapeDtypeStruct(w.shape, F32)
    return pl.pallas_call(
        body, name=name, grid=(rows // tr,),
        in_specs=[lay, pl.BlockSpec((tr, cols), lambda i: (i, 0)), lay, lay] + [ANY] * n_prev,
        out_specs=[lay] * 4, out_shape=[shp] * 4,
        input_output_aliases={4 + i: i for i in range(n_prev)},
        compiler_params=_cp(("parallel",)))(w, g, m, v, *(prev or ()))


def _loss_bwd(y, t, name):
    def body(y_ref, t_ref, l_ref, d_ref, db_ref):
        e = y_ref[...] - t_ref[...]
        d = e * (1.0 / D)
        d_ref[...] = d
        db_ref[...] = d.astype(BF16)

        @pl.when(pl.program_id(0) == 0)
        def _():
            l_ref[...] = jnp.zeros_like(l_ref)
        s = jnp.sum(jnp.sum(e * e, axis=-1, keepdims=True), axis=0, keepdims=True)
        l_ref[...] += jnp.broadcast_to(s, l_ref.shape)

    row = pl.BlockSpec((TR, D), lambda i: (i, 0))
    tile = pl.BlockSpec((8, 128), lambda i: (0, 0))
    return pl.pallas_call(
        body, name=name, grid=(L // TR,), in_specs=[row, row], out_specs=[tile, row, row],
        out_shape=[jax.ShapeDtypeStruct((8, 128), F32), jax.ShapeDtypeStruct((L, D), F32),
                   jax.ShapeDtypeStruct((L, D), BF16)],
        compiler_params=_cp(("arbitrary",)))(y, t)


def _adamw(w, g, m, v, name):
    rows, cols = w.shape
    tr = rows
    for cand in (512, 256, 128, 64, 32, 16, 8):
        if rows % cand == 0 and cand * cols * 4 <= 2 * 1024 * 1024:
            tr = cand
            break
    c1 = 1.0 / (1.0 - B1 ** STEP)
    c2 = 1.0 / (1.0 - B2 ** STEP)

    def body(w_ref, g_ref, m_ref, v_ref, d_ref, mo_ref, vo_ref):
        gv = g_ref[...]
        mn = B1 * m_ref[...] + (1.0 - B1) * gv
        vn = B2 * v_ref[...] + (1.0 - B2) * (gv * gv)
        mo_ref[...] = mn
        vo_ref[...] = vn
        d_ref[...] = -LR * ((mn * c1) / (jnp.sqrt(vn * c2) + EPS_A) + WD * w_ref[...])

    blk = pl.BlockSpec((tr, cols), lambda i: (i, 0))
    shp = jax.ShapeDtypeStruct((rows, cols), F32)
    return pl.pallas_call(body, name=name, grid=(rows // tr,), in_specs=[blk] * 4, out_specs=[blk] * 3,
                          out_shape=[shp] * 3, compiler_params=_cp(("parallel",)))(w, g, m, v)


CT = 256
CPAD = 32


U_X, U_Z, U_A, U_G, U_Q, U_K, U_V, U_DT = 0, 1536, 2560, 3072, 3584, 4096, 4224, 4352


CEXT = 8
CWIN = CT + CPAD
CROWS = L + CPAD + CEXT


def _fill_shifted(src_ref, base, win_ref, sh_ref):
    win_ref[...] = src_ref[pl.ds(base, CWIN + CEXT), :]
    for p in range(8):
        sh_ref[p] = win_ref[pl.ds(p, CWIN), :]


def _tap(sh_ref, o):
    return sh_ref[o % 8, 8 * (o // 8):8 * (o // 8) + CT, :]


CB = 128


def _conv_specs():
    return [pl.BlockSpec((L, CB), lambda j: (0, U_A // CB + j)), pl.BlockSpec((L, CB), lambda j: (0, U_G // CB + j))]


def _conv_scratch(n_padded):
    return ([pltpu.VMEM((CROWS, CB), F32)] * n_padded
            + [pltpu.VMEM((CWIN + CEXT, CB), F32), pltpu.VMEM((8, CWIN, CB), F32)])


def _fill_gated(a_ref, g_ref, hp_ref):
    hp_ref[0:CPAD, :] = jnp.zeros((CPAD, CB), F32)
    hp_ref[CPAD:CPAD + L, :] = a_ref[...] * _sigmoid(g_ref[...])
    hp_ref[CPAD + L:, :] = jnp.zeros((CEXT, CB), F32)


def _conf_conv_fwd(u, w, b, name):
    def body(a_ref, g_ref, w_ref, b_ref, c_ref, hp_ref, win_ref, sh_ref):
        _fill_gated(a_ref, g_ref, hp_ref)

        def tile(i, carry):
            base = pl.multiple_of(i * CT, CT)
            _fill_shifted(hp_ref, base, win_ref, sh_ref)
            c = jnp.broadcast_to(b_ref[...], (CT, CB))
            for k in range(CMK):
                c = c + w_ref[k:k + 1, :] * _tap(sh_ref, 2 + k)
            c_ref[pl.ds(base, CT), :] = c
            return carry

        lax.fori_loop(0, L // CT, tile, 0)

    return pl.pallas_call(
        body, name=name, grid=(CMC // CB,),
        in_specs=_conv_specs() + [pl.BlockSpec((CMK, CB), lambda j: (0, j)), pl.BlockSpec((1, CB), lambda j: (0, j))],
        out_specs=pl.BlockSpec((L, CB), lambda j: (0, j)),
        out_shape=jax.ShapeDtypeStruct((L, CMC), F32), scratch_shapes=_conv_scratch(1),
        compiler_params=_cp(("parallel",)))(u, u, w, b)


def _conf_ln_fwd(c, lw, lb, name):
    def body(c_ref, lw_ref, lb_ref, o_ref):
        cv = c_ref[...]
        cc = cv - jnp.mean(cv, axis=-1, keepdims=True)
        var = jnp.mean(cc * cc, axis=-1, keepdims=True)
        l = cc * lax.rsqrt(var + LN_EPS) * lw_ref[...] + lb_ref[...]
        o_ref[...] = (l * _sigmoid(l)).astype(BF16)

    row = pl.BlockSpec((TR, CMC), lambda i: (i, 0))
    vec = pl.BlockSpec((1, CMC), lambda i: (0, 0))
    return pl.pallas_call(body, name=name, grid=(L // TR,), in_specs=[row, vec, vec], out_specs=row,
                          out_shape=jax.ShapeDtypeStruct((L, CMC), BF16),
                          compiler_params=_cp(("parallel",)))(c, lw, lb)


def _conf_ln_bwd(c, lw, lb, dy, name):
    def body(c_ref, lw_ref, lb_ref, dy_ref, dc_ref, db_ref, dlw_ref, dlb_ref):
        cv = c_ref[...]
        cc = cv - jnp.mean(cv, axis=-1, keepdims=True)
        var = jnp.mean(cc * cc, axis=-1, keepdims=True)
        rstd = lax.rsqrt(var + LN_EPS)
        n = cc * rstd
        l = n * lw_ref[...] + lb_ref[...]
        sl = _sigmoid(l)
        dl = dy_ref[...] * (sl * (1.0 + l * (1.0 - sl)))
        dn = dl * lw_ref[...]
        dc = rstd * (dn - jnp.mean(dn, axis=-1, keepdims=True) - n * jnp.mean(dn * n, axis=-1, keepdims=True))
        dc_ref[...] = dc

        @pl.when(pl.program_id(0) == 0)
        def _():
            db_ref[...] = jnp.zeros_like(db_ref)
            dlw_ref[...] = jnp.zeros_like(dlw_ref)
            dlb_ref[...] = jnp.zeros_like(dlb_ref)
        db_ref[...] += jnp.sum(dc, axis=0, keepdims=True)
        dlw_ref[...] += jnp.sum(dl * n, axis=0, keepdims=True)
        dlb_ref[...] += jnp.sum(dl, axis=0, keepdims=True)

    row = pl.BlockSpec((TR, CMC), lambda i: (i, 0))
    vec = pl.BlockSpec((1, CMC), lambda i: (0, 0))
    vshape = jax.ShapeDtypeStruct((1, CMC), F32)
    return pl.pallas_call(
        body, name=name, grid=(L // TR,),
        in_specs=[row, vec, vec, pl.BlockSpec((TR, CMC), lambda i: (i, (SSD_W + ATT_W) // CMC))],
        out_specs=[row, vec, vec, vec], out_shape=[jax.ShapeDtypeStruct((L, CMC), F32), vshape, vshape, vshape],
        compiler_params=_cp(("arbitrary",)))(c, lw, lb, dy)


def _conf_conv_bwd(u, dc, w, name):
    def body(a_ref, g_ref, dc_ref, w_ref, da_ref, dg_ref, dw_ref, hp_ref, dcp_ref, win_ref, sh_ref, dwacc_ref):
        _fill_gated(a_ref, g_ref, hp_ref)
        dcp_ref[0:L, :] = dc_ref[...]
        dcp_ref[L:, :] = jnp.zeros((CPAD + CEXT, CB), F32)
        dwacc_ref[...] = jnp.zeros_like(dwacc_ref)

        def tile(i, carry):
            base = pl.multiple_of(i * CT, CT)
            _fill_shifted(hp_ref, base, win_ref, sh_ref)
            dcv = dcp_ref[pl.ds(base, CT), :]
            for k in range(CMK):
                dwacc_ref[k] += (dcv * _tap(sh_ref, 2 + k)).reshape(CT // 8, 8, CB).sum(axis=0)
            _fill_shifted(dcp_ref, base, win_ref, sh_ref)
            dh = jnp.zeros((CT, CB), F32)
            for k in range(CMK):
                dh = dh + w_ref[k:k + 1, :] * _tap(sh_ref, CMK - 1 - k)
            av = a_ref[pl.ds(base, CT), :]
            sg = _sigmoid(g_ref[pl.ds(base, CT), :])
            da_ref[pl.ds(base, CT), :] = (dh * sg).astype(BF16)
            dg_ref[pl.ds(base, CT), :] = (dh * av * sg * (1.0 - sg)).astype(BF16)
            return carry

        lax.fori_loop(0, L // CT, tile, 0)
        for k in range(CMK):
            dw_ref[k:k + 1, :] = jnp.sum(dwacc_ref[k], axis=0, keepdims=True)
        dw_ref[CMK:, :] = jnp.zeros((32 - CMK, CB), F32)

    col = pl.BlockSpec((L, CB), lambda j: (0, j))
    return pl.pallas_call(
        body, name=name, grid=(CMC // CB,),
        in_specs=_conv_specs() + [col, pl.BlockSpec((CMK, CB), lambda j: (0, j))],
        out_specs=[col, col, pl.BlockSpec((32, CB), lambda j: (0, j))],
        out_shape=[jax.ShapeDtypeStruct((L, CMC), BF16), jax.ShapeDtypeStruct((L, CMC), BF16),
                   jax.ShapeDtypeStruct((32, CMC), F32)],
        scratch_shapes=_conv_scratch(2) + [pltpu.VMEM((32, 8, CB), F32)],
        compiler_params=_cp(("parallel",)))(u, u, dc, w)


NPAIR = NQH // 2


def _partner(x, lo32):
    return jnp.where(lo32, pltpu.roll(x, 96, 1), pltpu.roll(x, 32, 1))


def _swa_prep(x, w2, w2p, c4, s4, bd, lo32):
    r = lax.rsqrt(_xdot_l(x * x, bd) * (1.0 / HD) + RMS_EPS)
    xh = x * r
    return r, xh, xh * w2 * c4 + _partner(xh, lo32) * w2p * s4


def _swa_unprep(dr, r, xh, w2, w2p, c4, s4, bd, lo32):
    dn = dr * c4
    dnp = dr * s4
    gx = dn * w2 + _partner(dnp * w2p, lo32)
    dw = jnp.sum((dn + _partner(dnp, lo32)) * xh, axis=0, keepdims=True)
    mu = _xdot_l(gx * xh, bd) * (1.0 / HD)
    return r * (gx - xh * mu), dw


def _swa_softmax(s, sink):
    row = lax.broadcasted_iota(jnp.int32, (L, 2 * Q), 0)
    col = lax.broadcasted_iota(jnp.int32, (L, 2 * Q), 1)
    rm = row & (Q - 1)
    valid = (col > rm) & (col <= rm + Q) & ((row >= Q) | (col >= Q))
    s = jnp.where(valid, s * (1.0 / math.sqrt(HD)), NEG)
    m = jnp.maximum(jnp.max(s, axis=-1, keepdims=True), sink)
    p = jnp.exp(s - m)
    ps = jnp.exp(sink - m)
    inv = 1.0 / (jnp.sum(p, axis=-1, keepdims=True) + ps)
    return p * inv, ps * inv


def _swa_in_specs():
    tab = pl.BlockSpec((L, 128), lambda p: (0, 0))
    wv = pl.BlockSpec((1, 128), lambda p: (0, 0))
    sk = pl.BlockSpec((1, 1, 128), lambda p: (p, 0, 0))
    return [pl.BlockSpec((L, 128), lambda p: (0, U_Q // 128 + p)), pl.BlockSpec((L, 128), lambda p: (0, U_K // 128)),
            pl.BlockSpec((L, 128), lambda p: (0, U_V // 128)), tab, tab, wv, wv, wv, wv, sk, sk,
            pl.BlockSpec((128, 128), lambda p: (0, 0))]


def _swa_setup(q_ref, k_ref, v_ref, c_ref, s_ref, qw_ref, qwp_ref, kw_ref, kwp_ref, bd_ref, kpad, vpad):
    g = pl.program_id(0) // 2
    lane = lax.broadcasted_iota(jnp.int32, (L, 128), 1)
    lo32 = (lane & 32) == 0
    own = (lane >> 6) == g
    c4, s4, bd = c_ref[...], s_ref[...], bd_ref[...]
    qn = _swa_prep(q_ref[...], qw_ref[...], qwp_ref[...], c4, s4, bd, lo32)
    kn = _swa_prep(k_ref[...], kw_ref[...], kwp_ref[...], c4, s4, bd, lo32)
    vv = v_ref[...]
    kpad[0:Q, :] = jnp.zeros((Q, 128), BF16)
    vpad[0:Q, :] = jnp.zeros((Q, 128), BF16)
    kpad[Q:, :] = jnp.where(own, kn[2], pltpu.roll(kn[2], HD, 1)).astype(BF16)
    vpad[Q:, :] = jnp.where(own, vv, pltpu.roll(vv, HD, 1)).astype(BF16)
    return qn, kn, lo32, own, c4, s4, bd


def _swa_fwd(u, cos4, sin4, qw2, qw2p, kw2, kw2p, sink_e, sink_o, bd, name):
    def body(q_ref, k_ref, v_ref, c_ref, s_ref, qw_ref, qwp_ref, kw_ref, kwp_ref, ske_ref, sko_ref, bd_ref,
             o_ref, kpad, vpad, s_scr, p_scr):
        qn, _, _, _, _, _, _ = _swa_setup(q_ref, k_ref, v_ref, c_ref, s_ref, qw_ref, qwp_ref, kw_ref, kwp_ref,
                                          bd_ref, kpad, vpad)
        qr = qn[2]
        first = lax.broadcasted_iota(jnp.int32, (Q, 128), 1) < HD
        for n in range(NC):
            rows = slice(n * Q, (n + 1) * Q)
            kc = kpad[n * Q:(n + 2) * Q, :]
            s_scr[0, rows, :] = _dg(jnp.where(first, qr[rows], 0.0).astype(BF16), kc, 1, 1)
            s_scr[1, rows, :] = _dg(jnp.where(first, 0.0, qr[rows]).astype(BF16), kc, 1, 1)
        for h, sk_ref in ((0, ske_ref), (1, sko_ref)):
            p, _ = _swa_softmax(s_scr[h], sk_ref[0][:, 0:1])
            p_scr[h] = p.astype(BF16)
        for n in range(NC):
            rows = slice(n * Q, (n + 1) * Q)
            vc = vpad[n * Q:(n + 2) * Q, :]
            o_ref[rows, :] = jnp.where(first, _dg(p_scr[0, rows, :], vc, 1, 0),
                                       _dg(p_scr[1, rows, :], vc, 1, 0)).astype(BF16)

    return pl.pallas_call(
        body, name=name, grid=(NPAIR,), in_specs=_swa_in_specs(),
        out_specs=pl.BlockSpec((L, 128), lambda p: (0, p)),
        out_shape=jax.ShapeDtypeStruct((L, ATT_W), BF16),
        scratch_shapes=[pltpu.VMEM((L + Q, 128), BF16), pltpu.VMEM((L + Q, 128), BF16),
                        pltpu.VMEM((2, L, 2 * Q), F32), pltpu.VMEM((2, L, 2 * Q), BF16)],
        compiler_params=_cp(("arbitrary",)))(u, u, u, cos4, sin4, qw2, qw2p, kw2, kw2p, sink_e, sink_o, bd)


def _swa_bwd(u, dy, cos4, sin4, qw2, qw2p, kw2, kw2p, sink_e, sink_o, bd, name):
    def body(q_ref, k_ref, v_ref, c_ref, s_ref, qw_ref, qwp_ref, kw_ref, kwp_ref, ske_ref, sko_ref, bd_ref, do_ref,
             dq_ref, dk_ref, dv_ref, dqw_ref, dkw_ref, dse_ref, dso_ref,
             kpad, vpad, s_scr, dp_scr, ds_scr, pb_scr, dkr_acc, dv_acc, dqr_scr):
        pidx = pl.program_id(0)

        @pl.when(pidx == 0)
        def _():
            dkr_acc[...] = jnp.zeros_like(dkr_acc)
            dv_acc[...] = jnp.zeros_like(dv_acc)
            dqw_ref[...] = jnp.zeros_like(dqw_ref)

        qn, kn, lo32, own, c4, s4, bd = _swa_setup(q_ref, k_ref, v_ref, c_ref, s_ref, qw_ref, qwp_ref, kw_ref,
                                                   kwp_ref, bd_ref, kpad, vpad)
        qr = qn[2]
        lane_q = lax.broadcasted_iota(jnp.int32, (Q, 128), 1)
        first = lane_q < HD
        own_q = (lane_q >> 6) == pidx // 2

        def halves(t):
            return jnp.where(first, t, 0.0).astype(BF16), jnp.where(first, 0.0, t).astype(BF16)

        for n in range(NC):
            rows = slice(n * Q, (n + 1) * Q)
            kc = kpad[n * Q:(n + 2) * Q, :]
            vc = vpad[n * Q:(n + 2) * Q, :]
            qm = halves(qr[rows])
            dom = halves(do_ref[rows, :])
            for h in range(2):
                s_scr[h, rows, :] = _dg(qm[h], kc, 1, 1)
                dp_scr[h, rows, :] = _dg(dom[h], vc, 1, 1)
        for h, sk_ref, dsk_ref in ((0, ske_ref, dse_ref), (1, sko_ref, dso_ref)):
            p, ps = _swa_softmax(s_scr[h], sk_ref[0][:, 0:1])
            dp = dp_scr[h]
            delta = jnp.sum(p * dp, axis=-1, keepdims=True)
            dsk_ref[0] = jnp.broadcast_to(-jnp.sum(ps * delta, axis=0, keepdims=True), (1, 128))
            ds_scr[h] = (p * (dp - delta) * (1.0 / math.sqrt(HD))).astype(BF16)
            pb_scr[h] = p.astype(BF16)
        for n in range(NC):
            rows = slice(n * Q, (n + 1) * Q)
            kc = kpad[n * Q:(n + 2) * Q, :]
            dqr_scr[rows, :] = jnp.where(first, _dg(ds_scr[0, rows, :], kc, 1, 0), _dg(ds_scr[1, rows, :], kc, 1, 0))
        for m in range(NC):
            acc_k = jnp.zeros((Q, 128), F32)
            acc_v = jnp.zeros((Q, 128), F32)
            for n, cols in ((m, slice(Q, 2 * Q)), (m + 1, slice(0, Q))):
                if n >= NC:
                    continue
                rows = slice(n * Q, (n + 1) * Q)
                qm = halves(qr[rows])
                dom = halves(do_ref[rows, :])
                for h in range(2):
                    acc_k = acc_k + _dg(ds_scr[h, rows, cols], qm[h], 0, 0)
                    acc_v = acc_v + _dg(pb_scr[h, rows, cols], dom[h], 0, 0)
            rows = slice(m * Q, (m + 1) * Q)
            dkr_acc[rows, :] += jnp.where(own_q, acc_k + pltpu.roll(acc_k, HD, 1), 0.0)
            dv_acc[rows, :] += jnp.where(own_q, acc_v + pltpu.roll(acc_v, HD, 1), 0.0)
        dq, dqw = _swa_unprep(dqr_scr[...], qn[0], qn[1], qw_ref[...], qwp_ref[...], c4, s4, bd, lo32)
        dq_ref[...] = dq.astype(BF16)
        dqw_ref[...] += dqw

        @pl.when(pidx == NPAIR - 1)
        def _():
            dk, dkw = _swa_unprep(dkr_acc[...], kn[0], kn[1], kw_ref[...], kwp_ref[...], c4, s4, bd, lo32)
            dk_ref[...] = dk.astype(BF16)
            dkw_ref[...] = dkw
            dv_ref[...] = dv_acc[...].astype(BF16)

    full = pl.BlockSpec((L, 128), lambda p: (0, 0))
    wv = pl.BlockSpec((1, 128), lambda p: (0, 0))
    sk = pl.BlockSpec((1, 1, 128), lambda p: (p, 0, 0))
    vec = jax.ShapeDtypeStruct((1, 128), F32)
    skv = jax.ShapeDtypeStruct((NPAIR, 1, 128), F32)
    return pl.pallas_call(
        body, name=name, grid=(NPAIR,),
        in_specs=_swa_in_specs() + [pl.BlockSpec((L, 128), lambda p: (0, SSD_W // 128 + p))],
        out_specs=[pl.BlockSpec((L, 128), lambda p: (0, p)), full, full, wv, wv, sk, sk],
        out_shape=[jax.ShapeDtypeStruct((L, ATT_W), BF16), jax.ShapeDtypeStruct((L, 128), BF16),
                   jax.ShapeDtypeStruct((L, 128), BF16), vec, vec, skv, skv],
        scratch_shapes=[pltpu.VMEM((L + Q, 128), BF16), pltpu.VMEM((L + Q, 128), BF16),
                        pltpu.VMEM((2, L, 2 * Q), F32), pltpu.VMEM((2, L, 2 * Q), F32),
                        pltpu.VMEM((2, L, 2 * Q), BF16), pltpu.VMEM((2, L, 2 * Q), BF16),
                        pltpu.VMEM((L, 128), F32), pltpu.VMEM((L, 128), F32), pltpu.VMEM((L, 128), F32)],
        compiler_params=_cp(("arbitrary",)))(u, u, u, cos4, sin4, qw2, qw2p, kw2, kw2p, sink_e, sink_o, bd, dy)


def _ssd_consts():
    hh = jnp.arange(128)[:, None]
    e = (hh == (jnp.arange(SSD_W)[None, :] // HP)).astype(BF16)
    e2 = (hh == (jnp.arange(NH * 128)[None, :] // 128)).astype(BF16)
    et = e.T
    tril = (jnp.arange(Q)[:, None] >= jnp.arange(Q)[None, :]).astype(BF16)
    triu = tril.T
    eye = jnp.eye(128, dtype=BF16)
    return e, e2, et, tril, triu, eye


def _ssd_common(x_ref, ext_scr, cw_ref, cb_ref, dt_ref, dtb_ref, alog_ref, e_ref, e2_ref, tril_ref, triu_ref,
                arow_scr, acol_scr, eax_scr):
    conv = jnp.broadcast_to(cb_ref[...], (Q, XBC))
    for k in range(4):
        conv = conv + cw_ref[k:k + 1, :] * ext_scr[pl.ds(5 + k, Q), :]
    sg = _sigmoid(conv)
    xbc = conv * sg
    dtpre = dt_ref[...] + dtb_ref[...]
    dt = _softplus(dtpre)
    a = -jnp.exp(alog_ref[...])
    adt = dt * a
    acol = _xdot_r(tril_ref[...], adt)
    acol_scr[...] = acol
    arow_scr[...] = _xdot_l(adt, triu_ref[...], 0, 0)
    alast = acol_scr[Q - 1:Q, :]
    ea = jnp.exp(acol)
    decs = jnp.exp(alast - acol)
    e = e_ref[...]
    dt_x = _xdot_l(dt, e)
    eax_scr[...] = _xdot_l(ea, e)
    decs_x = _xdot_l(decs, e)
    acx2 = _xdot_l(acol, e2_ref[...])
    return conv, sg, xbc, dtpre, dt, a, adt, acol, alast, ea, decs, dt_x, decs_x, acx2


def _ssd_fwd(u, cw, cb, dtb, alog, dxp, nw, consts, name):
    e, e2, et, tril, triu, eye = consts

    def body(z0_ref, z1_ref, x_ref, dt_ref, cw_ref, cb_ref, dtb_ref, alog_ref, dx_ref, nw_ref, e_ref, e2_ref,
             tril_ref, triu_ref, ya_ref, ypre_ref, st_ref, s_scr, ext_scr, arow_scr, acol_scr, eax_scr):
        c = pl.program_id(0)

        @pl.when(c == 0)
        def _():
            s_scr[...] = jnp.zeros_like(s_scr)
            ext_scr[0:8, :] = jnp.zeros((8, XBC), F32)
        ext_scr[8:8 + Q, :] = x_ref[...]
        (conv, sg, xbc, dtpre, dt, a, adt, acol, alast, ea, decs, dt_x, decs_x, acx2) = _ssd_common(
            x_ref, ext_scr, cw_ref, cb_ref, dt_ref, dtb_ref, alog_ref, e_ref, e2_ref, tril_ref, triu_ref,
            arow_scr, acol_scr, eax_scr)
        ext_scr[0:8, :] = ext_scr[Q:Q + 8, :]
        xs = xbc[:, :SSD_W]
        xdt = xs * dt_x
        lane = lax.broadcasted_iota(jnp.int32, (Q, 128), 1)
        causal = lax.broadcasted_iota(jnp.int32, (Q, Q), 0) >= lax.broadcasted_iota(jnp.int32, (Q, Q), 1)
        for g in range(2):
            bg = xbc[:, SSD_W + g * NS:SSD_W + (g + 1) * NS].astype(BF16)
            cg = xbc[:, SSD_W + 2 * NS + g * NS:SSD_W + 2 * NS + (g + 1) * NS].astype(BF16)
            cbm = _dg(cg, bg, 1, 1)
            sgv = s_scr[g]
            st_ref[0, g] = sgv
            gc = slice(g * 512, (g + 1) * 512)
            yoff = _dg(cg, sgv.astype(BF16), 1, 0) * eax_scr[:, gc]
            for pr in range(4):
                h0 = g * 8 + 2 * pr
                h1 = h0 + 1
                c0 = g * 512 + pr * 128
                xp = xdt[:, c0:c0 + 128].astype(BF16)
                w0 = (cbm * jnp.exp(jnp.where(causal, acx2[:, h0 * 128:(h0 + 1) * 128] - arow_scr[h0:h0 + 1, :],
                                              NEG))).astype(BF16)
                w1 = (cbm * jnp.exp(jnp.where(causal, acx2[:, h1 * 128:(h1 + 1) * 128] - arow_scr[h1:h1 + 1, :],
                                              NEG))).astype(BF16)
                yd = jnp.where(lane < HP, _dg(w0, xp, 1, 0), _dg(w1, xp, 1, 0))
                ypre_ref[:, c0:c0 + 128] = (yd + yoff[:, pr * 128:(pr + 1) * 128]
                                            + xs[:, c0:c0 + 128] * dx_ref[:, c0:c0 + 128])
            contrib = _dg(bg, (xdt[:, gc] * decs_x[:, gc]).astype(BF16), 0, 0)
            s_scr[g] = sgv * eax_scr[Q - 1:Q, gc] + contrib
        for g, zr in enumerate((z0_ref, z1_ref)):
            gc = slice(g * 512, (g + 1) * 512)
            zz = zr[...]
            ggg = ypre_ref[:, gc] * (zz * _sigmoid(zz))
            rstd = lax.rsqrt(jnp.mean(ggg * ggg, axis=-1, keepdims=True) + RMS_EPS)
            ya_ref[:, gc] = (ggg * rstd * nw_ref[:, gc]).astype(BF16)

    def row(w, blk=0):
        return pl.BlockSpec((Q, w), lambda c: (c, blk))

    def full(shape):
        return pl.BlockSpec(shape, lambda c: (0,) * len(shape))

    return pl.pallas_call(
        body, name=name, grid=(NC,),
        in_specs=[row(512, U_Z // 512), row(512, U_Z // 512 + 1), row(XBC, U_X // XBC), row(128, U_DT // 128),
                  full((4, XBC)), full((1, XBC)), full((1, 128)), full((1, 128)),
                  full((1, SSD_W)), full((1, SSD_W)), full((128, SSD_W)), full((128, NH * 128)), full((Q, Q)),
                  full((Q, Q))],
        out_specs=[row(SSD_W), row(SSD_W), pl.BlockSpec((1, 2, NS, 512), lambda c: (c, 0, 0, 0))],
        out_shape=[jax.ShapeDtypeStruct((L, SSD_W), BF16), jax.ShapeDtypeStruct((L, SSD_W), F32),
                   jax.ShapeDtypeStruct((NC, 2, NS, 512), F32)],
        scratch_shapes=[pltpu.VMEM((2, NS, 512), F32), pltpu.VMEM((Q + 8, XBC), F32), pltpu.VMEM((128, Q), F32),
                        pltpu.VMEM((Q, 128), F32), pltpu.VMEM((Q, SSD_W), F32)],
        compiler_params=_cp(("arbitrary",)))(u, u, u, u, cw, cb, dtb, alog, dxp, nw, e, e2, tril, triu)


def _ssd_bwd(u, ypre, st, dy, cw, cb, dtb, alog, dxp, nw, consts, name):
    e, e2, et, tril, triu, eye = consts

    def body(z0_ref, z1_ref, x_ref, xp_ref, dt_ref, ypre_ref, st_ref, dya_ref, cw_ref, cb_ref, dtb_ref, alog_ref, dx_ref,
             nw_ref, e_ref, e2_ref, et_ref, tril_ref, triu_ref, eye_ref,
             dz_ref, dxr_ref, ddtr_ref, dcw_ref, dcb_ref, ddtb_ref, dalog_ref, dd_ref, dnw_ref,
             g_scr, ext_scr, ext2_scr, arow_scr, acol_scr, eax_scr, darow_scr, dxdt_scr, t1_scr, t2_scr, dgg_scr):
        i = pl.program_id(0)

        @pl.when(i == 0)
        def _():
            g_scr[...] = jnp.zeros_like(g_scr)
            ext2_scr[Q:Q + 8, :] = jnp.zeros((8, XBC), F32)
            for r in (dcw_ref, dcb_ref, ddtb_ref, dalog_ref, dd_ref, dnw_ref):
                r[...] = jnp.zeros_like(r)
        not_first = jnp.where(i < NC - 1, 1.0, 0.0)
        ext_scr[0:8, :] = xp_ref[Q - 8:Q, :] * not_first
        ext_scr[8:8 + Q, :] = x_ref[...]
        (conv, sg, xbc, dtpre, dt, a, adt, acol, alast, ea, decs, dt_x, decs_x, acx2) = _ssd_common(
            x_ref, ext_scr, cw_ref, cb_ref, dt_ref, dtb_ref, alog_ref, e_ref, e2_ref, tril_ref, triu_ref,
            arow_scr, acol_scr, eax_scr)
        et_m = et_ref[...]
        xs = xbc[:, :SSD_W]
        xdt = xs * dt_x
        y = ypre_ref[...]
        zz = jnp.concatenate([z0_ref[...], z1_ref[...]], axis=1)
        sz = _sigmoid(zz)
        silu_z = zz * sz
        gg = y * silu_z
        dya = dya_ref[...]
        for g in range(2):
            gc = slice(g * 512, (g + 1) * 512)
            ggg = gg[:, gc]
            rstd = lax.rsqrt(jnp.mean(ggg * ggg, axis=-1, keepdims=True) + RMS_EPS)
            n = ggg * rstd
            dyag = dya[:, gc]
            dnw_ref[:, gc] += jnp.sum(dyag * n, axis=0, keepdims=True)
            dn = dyag * nw_ref[:, gc]
            dgg_scr[:, gc] = rstd * (dn - n * jnp.mean(dn * n, axis=-1, keepdims=True))
        dgg = dgg_scr[...]
        dy = dgg * silu_z
        dz_ref[...] = (dgg * y * (sz * (1.0 + zz * (1.0 - sz)))).astype(BF16)
        dd_ref[...] += _rowdot(jnp.sum(dy * xs, axis=0, keepdims=True), et_m)
        dxs = dy * dx_ref[...]
        dys = dy * eax_scr[...]
        lane = lax.broadcasted_iota(jnp.int32, (Q, 128), 1)
        causal = lax.broadcasted_iota(jnp.int32, (Q, Q), 0) >= lax.broadcasted_iota(jnp.int32, (Q, Q), 1)
        darow_scr[...] = jnp.zeros_like(darow_scr)
        dacol = jnp.zeros((Q, 128), F32)
        dcdx = []
        dbs = []
        dcs = []
        for g in range(2):
            gc = slice(g * 512, (g + 1) * 512)
            bg = xbc[:, SSD_W + g * NS:SSD_W + (g + 1) * NS].astype(BF16)
            cg = xbc[:, SSD_W + 2 * NS + g * NS:SSD_W + 2 * NS + (g + 1) * NS].astype(BF16)
            cbm = _dg(cg, bg, 1, 1)
            sgv = st_ref[0, g]
            sgb = sgv.astype(BF16)
            gv = g_scr[g]
            gvb = gv.astype(BF16)
            yoff = _dg(cg, sgb, 1, 0) * eax_scr[:, gc]
            dysg = dys[:, gc].astype(BF16)
            dcg = _dg(dysg, sgb, 1, 1)
            ds_off = _dg(cg, dysg, 0, 0)
            t1_scr[:, gc] = dy[:, gc] * yoff
            xdec = xdt[:, gc] * decs_x[:, gc]
            dxd = _dg(bg, gvb, 1, 0)
            dbg = _dg(xdec.astype(BF16), gvb, 1, 1)
            dxdt_g = dxd * decs_x[:, gc]
            t2_scr[:, gc] = dxd * xdt[:, gc]
            cdx = eax_scr[Q - 1:Q, gc]
            dcdx.append(jnp.sum(gv * sgv, axis=0, keepdims=True))
            g_scr[g] = gv * cdx + ds_off
            dcb_acc = jnp.zeros((Q, Q), F32)
            for pr in range(4):
                c0 = g * 512 + pr * 128
                xp = xdt[:, c0:c0 + 128].astype(BF16)
                dyp = dy[:, c0:c0 + 128]
                dypb = dyp.astype(BF16)
                halves = []
                for hh, keep in ((g * 8 + 2 * pr, lane < HP), (g * 8 + 2 * pr + 1, lane >= HP)):
                    lam = jnp.exp(jnp.where(causal, acx2[:, hh * 128:(hh + 1) * 128] - arow_scr[hh:hh + 1, :], NEG))
                    w = cbm * lam
                    dw = _dg(jnp.where(keep, dyp, 0.0).astype(BF16), xp, 1, 1)
                    dcb_acc = dcb_acc + dw * lam
                    t = dw * w
                    dacol = dacol + jnp.sum(t, axis=-1, keepdims=True) * (lane == hh).astype(F32)
                    darow_scr[hh:hh + 1, :] -= jnp.sum(t, axis=0, keepdims=True)
                    halves.append(_dg(w.astype(BF16), dypb, 0, 0))
                dxdt_scr[:, c0:c0 + 128] = (jnp.where(lane < HP, halves[0], halves[1])
                                            + dxdt_g[:, pr * 128:(pr + 1) * 128])
            dcbb = dcb_acc.astype(BF16)
            dcs.append(dcg + _dg(dcbb, bg, 1, 0))
            dbs.append(dbg + _dg(dcbb, cg, 0, 0))
        dacol = dacol + _xdot_l(t1_scr[...], et_m)
        ddecs = _xdot_l(t2_scr[...], et_m) * decs
        dacol = dacol - ddecs
        dalast = jnp.sum(ddecs, axis=0, keepdims=True)
        dcd = _rowdot(jnp.concatenate(dcdx, axis=1), et_m)
        dalast = dalast + dcd * jnp.exp(alast)
        dacol = dacol + _xdot_l(darow_scr[...], eye_ref[...], 0, 0)
        rowi = lax.broadcasted_iota(jnp.int32, (Q, 128), 0)
        dacol = dacol + jnp.where(rowi == Q - 1, dalast, 0.0)
        dadt = _xdot_r(triu_ref[...], dacol)
        dxdt = dxdt_scr[...]
        ddt = dadt * a + _xdot_l(dxdt * xs, et_m)
        dalog_ref[...] += jnp.sum(dadt * dt, axis=0, keepdims=True) * a
        dxs = dxs + dxdt * dt_x
        ddtr = ddt * _sigmoid(dtpre)
        ddtb_ref[...] += jnp.sum(ddtr, axis=0, keepdims=True)
        ddtr_ref[...] = ddtr.astype(BF16)
        dsilu = sg * (1.0 + conv * (1.0 - sg))
        ext2_scr[0:Q, 0:SSD_W] = dxs * dsilu[:, :SSD_W]
        for g in range(2):
            o1 = SSD_W + g * NS
            o2 = SSD_W + 2 * NS + g * NS
            ext2_scr[0:Q, o1:o1 + NS] = dbs[g] * dsilu[:, o1:o1 + NS]
            ext2_scr[0:Q, o2:o2 + NS] = dcs[g] * dsilu[:, o2:o2 + NS]
        dconv = ext2_scr[0:Q, :]
        dcb_ref[...] += jnp.sum(dconv, axis=0, keepdims=True)
        dxr = jnp.zeros((Q, XBC), F32)
        for k in range(4):
            dcw_ref[k:k + 1, :] += jnp.sum(dconv * ext_scr[pl.ds(5 + k, Q), :], axis=0, keepdims=True)
            dxr = dxr + cw_ref[k:k + 1, :] * ext2_scr[pl.ds(3 - k, Q), :]
        dxr_ref[...] = dxr.astype(BF16)
        ext2_scr[Q:Q + 8, :] = ext2_scr[0:8, :]

    def row(w, blk=0):
        return pl.BlockSpec((Q, w), lambda i: (NC - 1 - i, blk))

    def full(shape):
        return pl.BlockSpec(shape, lambda i: (0,) * len(shape))

    prev = pl.BlockSpec((Q, XBC), lambda i: (jnp.maximum(NC - 2 - i, 0), U_X // XBC))
    return pl.pallas_call(
        body, name=name, grid=(NC,),
        in_specs=[row(512, U_Z // 512), row(512, U_Z // 512 + 1), row(XBC, U_X // XBC), prev, row(128, U_DT // 128),
                  row(SSD_W),
                  pl.BlockSpec((1, 2, NS, 512), lambda i: (NC - 1 - i, 0, 0, 0)), row(SSD_W),
                  full((4, XBC)), full((1, XBC)), full((1, 128)), full((1, 128)), full((1, SSD_W)),
                  full((1, SSD_W)), full((128, SSD_W)), full((128, NH * 128)), full((SSD_W, 128)), full((Q, Q)),
                  full((Q, Q)), full((128, 128))],
        out_specs=[row(SSD_W), row(XBC), row(128), full((8, XBC)), full((1, XBC)), full((1, 128)), full((1, 128)),
                   full((1, 128)), full((1, SSD_W))],
        out_shape=[jax.ShapeDtypeStruct((L, SSD_W), BF16), jax.ShapeDtypeStruct((L, XBC), BF16),
                   jax.ShapeDtypeStruct((L, 128), BF16), jax.ShapeDtypeStruct((8, XBC), F32),
                   jax.ShapeDtypeStruct((1, XBC), F32), jax.ShapeDtypeStruct((1, 128), F32),
                   jax.ShapeDtypeStruct((1, 128), F32), jax.ShapeDtypeStruct((1, 128), F32),
                   jax.ShapeDtypeStruct((1, SSD_W), F32)],
        scratch_shapes=[pltpu.VMEM((2, NS, 512), F32), pltpu.VMEM((Q + 8, XBC), F32), pltpu.VMEM((Q + 8, XBC), F32),
                        pltpu.VMEM((128, Q), F32), pltpu.VMEM((Q, 128), F32), pltpu.VMEM((Q, SSD_W), F32),
                        pltpu.VMEM((128, Q), F32), pltpu.VMEM((Q, SSD_W), F32), pltpu.VMEM((Q, SSD_W), F32),
                        pltpu.VMEM((Q, SSD_W), F32), pltpu.VMEM((Q, SSD_W), F32)],
        compiler_params=_cp(("arbitrary",)))(u, u, u, u, u, ypre, st, dy, cw, cb, dtb, alog, dxp, nw,
                                             e, e2, et, tril, triu, eye)


def _my_pos():
    return lax.axis_index("x"), lax.axis_index("y"), lax.axis_index("c")


CHIP_REL = ((1, 0), (0, 1), (1, 1))
CHIP_XOR = (2, 1, 3)
BIG = ("w_in", "w_out", "w_mlp_up", "w_mlp_down")
NW = len(BIG)
AT = 256


def _chips(x, y):
    return [(1 - x if dx else x, 1 - y if dy else y) for dx, dy in CHIP_REL]


HBM_SPEC = pl.BlockSpec(memory_space=pltpu.HBM)
SEM_SPEC = pl.BlockSpec(memory_space=pltpu.SEMAPHORE)
EFFECT = pltpu.SideEffectType.DATAFLOW_SIDE_EFFECTING


def _hbm(t):
    return pltpu.with_memory_space_constraint(t, pltpu.HBM)


def _split_start(srcs, lands, after, copies, name):
    n = len(srcs)

    def body(*refs):
        src_refs, land_refs = refs[:n], refs[n:2 * n]
        send_sems, recv_sems = refs[2 * n + 1], refs[2 * n + 2]
        token = refs[-1]
        for w, k, src, dst, dev in copies(src_refs, land_refs):
            pltpu.make_async_remote_copy(src_ref=src, dst_ref=dst, send_sem=send_sems.at[3 * w + k],
                                         recv_sem=recv_sems.at[3 * w + k], device_id=dev, device_id_type=MESH).start()
        token[...] = jnp.zeros_like(token)

    outs = pl.pallas_call(
        body, name=name,
        out_shape=(pltpu.SemaphoreType.DMA((3 * n,)), pltpu.SemaphoreType.DMA((3 * n,)),
                   *[pltpu.HBM(t.shape, t.dtype) for t in srcs], *[pltpu.HBM(t.shape, t.dtype) for t in lands],
                   jax.ShapeDtypeStruct((8, 128), F32)),
        in_specs=[HBM_SPEC] * (2 * n) + [ANY],
        out_specs=(SEM_SPEC, SEM_SPEC, *([HBM_SPEC] * (2 * n)), VMEM_SPEC),
        input_output_aliases={i: 2 + i for i in range(2 * n)},
        compiler_params=pltpu.CompilerParams(has_side_effects=EFFECT))(
            *[_hbm(t) for t in srcs], *[_hbm(t) for t in lands], after)
    return outs[0], outs[1], list(outs[2:2 + n]), list(outs[2 + n:2 + 2 * n]), outs[-1]


def _split_wait(send_sems, recv_sems, srcs, lands, after, copies, name):
    n = len(srcs)

    def body(*refs):
        src_refs, land_refs = refs[:n], refs[n:2 * n]
        ssem, rsem = refs[2 * n], refs[2 * n + 1]
        for w, k, src, dst, dev in copies(src_refs, land_refs):
            cp = pltpu.make_async_remote_copy(src_ref=src, dst_ref=dst, send_sem=ssem.at[3 * w + k],
                                              recv_sem=rsem.at[3 * w + k], device_id=dev, device_id_type=MESH)
            cp.wait_send()
            cp.wait_recv()

    outs = pl.pallas_call(
        body, name=name,
        out_shape=tuple([pltpu.HBM(t.shape, t.dtype) for t in srcs] + [pltpu.HBM(t.shape, t.dtype) for t in lands]),
        in_specs=[HBM_SPEC] * (2 * n) + [SEM_SPEC, SEM_SPEC, ANY],
        out_specs=tuple([HBM_SPEC] * (2 * n)),
        input_output_aliases={i: i for i in range(2 * n)},
        compiler_params=pltpu.CompilerParams(has_side_effects=EFFECT))(*srcs, *lands, send_sems, recv_sems, after)
    return list(outs[:n]), list(outs[n:])


def _ag_copies(arrival):
    def copies(src_refs, land_refs):
        x, y, c = _my_pos()
        s = 2 * x + y
        chips = _chips(x, y)
        for w in range(len(src_refs)):
            hr = src_refs[w].shape[0] // 2
            mine = pl.ds(c * hr, hr)
            for k in range(3):
                slot = s ^ CHIP_XOR[k] if arrival else s
                yield w, k, src_refs[w].at[mine], land_refs[w].at[slot, mine], (*chips[k], c)
    return copies


def _ag_forward(lands, name):
    n = len(lands)

    def body(*refs):
        outs = refs[n:2 * n]
        send_sems, recv_sems = refs[2 * n:]
        x, y, c = _my_pos()
        s = 2 * x + y
        sib = (x, y, 1 - c)
        sends = []
        for w in range(n):
            hr = outs[w].shape[1] // 2
            for k in range(3):
                blk = outs[w].at[s ^ CHIP_XOR[k], pl.ds(c * hr, hr)]
                fw = pltpu.make_async_remote_copy(
                    src_ref=blk, dst_ref=blk, send_sem=send_sems.at[w, k], recv_sem=recv_sems.at[w, k],
                    device_id=sib, device_id_type=MESH)
                fw.start()
                sends.append(fw)
        for w in range(n):
            hr = outs[w].shape[1] // 2
            for k in range(3):
                blk = outs[w].at[s ^ CHIP_XOR[k], pl.ds((1 - c) * hr, hr)]
                pltpu.make_async_remote_copy(
                    src_ref=blk, dst_ref=blk, send_sem=send_sems.at[w, k], recv_sem=recv_sems.at[w, k],
                    device_id=sib, device_id_type=MESH).wait_recv()
        for cp in sends:
            cp.wait_send()

    return pl.pallas_call(
        body, name=name, in_specs=[ANY] * n, out_specs=[ANY] * n,
        out_shape=[jax.ShapeDtypeStruct(t.shape, t.dtype) for t in lands],
        input_output_aliases={w: w for w in range(n)},
        scratch_shapes=[pltpu.SemaphoreType.DMA((n, 3)), pltpu.SemaphoreType.DMA((n, 3))])(*lands)


def _rs_copies(src_refs, land_refs):
    x, y, c = _my_pos()
    s = 2 * x + y
    chips = _chips(x, y)
    for w in range(len(src_refs)):
        for k in range(3):
            yield w, k, src_refs[w].at[s ^ CHIP_XOR[k]], land_refs[w].at[k], (*chips[k], c)


def _place_own(shard, gathered, sidx, name):
    r, cc = shard.shape

    def body(s_ref, a_ref, g_ref, o_ref):
        o_ref[0] = a_ref[...]

    return pl.pallas_call(
        body, name=name,
        grid_spec=pltpu.PrefetchScalarGridSpec(
            num_scalar_prefetch=1, grid=(r // AT,),
            in_specs=[pl.BlockSpec((AT, cc), lambda i, s_ref: (i, 0)), ANY],
            out_specs=pl.BlockSpec((1, AT, cc), lambda i, s_ref: (s_ref[0], i, 0))),
        out_shape=jax.ShapeDtypeStruct(gathered.shape, gathered.dtype),
        input_output_aliases={2: 0}, compiler_params=_cp(("parallel",)))(sidx, shard, gathered)


def _rs_pair(dwb, name):
    n = len(dwb)

    def body(*refs):
        ins, outs = refs[:n], refs[n:2 * n]
        send_sems, recv_sems = refs[2 * n:]
        x, y, c = _my_pos()
        cps = []
        for w in range(n):
            hr = ins[w].shape[1] // 2
            cp = pltpu.make_async_remote_copy(
                src_ref=ins[w].at[:, pl.ds((1 - c) * hr, hr)], dst_ref=outs[w], send_sem=send_sems.at[w],
                recv_sem=recv_sems.at[w], device_id=(x, y, 1 - c), device_id_type=MESH)
            cp.start()
            cps.append(cp)
        for cp in cps:
            cp.wait()

    return pl.pallas_call(
        body, name=name, in_specs=[ANY] * n, out_specs=[ANY] * n,
        out_shape=[jax.ShapeDtypeStruct((4, t.shape[1] // 2, t.shape[2]), t.dtype) for t in dwb],
        scratch_shapes=[pltpu.SemaphoreType.DMA((n,)), pltpu.SemaphoreType.DMA((n,))])(*dwb)


def _rs_sib(q, name):
    n = len(q)

    def body(*refs):
        outs = refs[n:2 * n]
        send_sems, recv_sems = refs[2 * n:]
        x, y, c = _my_pos()
        cps = []
        for w in range(n):
            hr = outs[w].shape[0] // 2
            mine = pl.ds(c * hr, hr)
            cp = pltpu.make_async_remote_copy(
                src_ref=outs[w].at[mine], dst_ref=outs[w].at[mine], send_sem=send_sems.at[w],
                recv_sem=recv_sems.at[w], device_id=(x, y, 1 - c), device_id_type=MESH)
            cp.start()
            cps.append(cp)
        for w in range(n):
            hr = outs[w].shape[0] // 2
            other = outs[w].at[pl.ds((1 - c) * hr, hr)]
            pltpu.make_async_remote_copy(
                src_ref=other, dst_ref=other, send_sem=send_sems.at[w], recv_sem=recv_sems.at[w],
                device_id=(x, y, 1 - c), device_id_type=MESH).wait_recv()
        for cp in cps:
            cp.wait_send()

    return pl.pallas_call(
        body, name=name, in_specs=[ANY] * n, out_specs=[ANY] * n,
        out_shape=[jax.ShapeDtypeStruct(t.shape, t.dtype) for t in q],
        input_output_aliases={w: w for w in range(n)},
        scratch_shapes=[pltpu.SemaphoreType.DMA((n,)), pltpu.SemaphoreType.DMA((n,))])(*q)


def _rs_add2(dw, got, cidx, name):
    _, r, cc = dw.shape
    hr = r // 2
    nb = hr // AT

    def body(c_ref, a_ref, b_ref, o_ref, ob_ref):
        acc = a_ref[...] + b_ref[...].astype(F32)
        o_ref[...] = acc
        ob_ref[...] = acc.astype(BF16)

    blk = pl.BlockSpec((1, AT, cc), lambda sh, i, c_ref: (sh, i, 0))
    return pl.pallas_call(
        body, name=name,
        grid_spec=pltpu.PrefetchScalarGridSpec(
            num_scalar_prefetch=1, grid=(4, nb),
            in_specs=[pl.BlockSpec((1, AT, cc), lambda sh, i, c_ref: (sh, c_ref[0] * nb + i, 0)), blk],
            out_specs=[blk, blk]),
        out_shape=[jax.ShapeDtypeStruct((4, hr, cc), F32), jax.ShapeDtypeStruct((4, hr, cc), BF16)],
        compiler_params=_cp(("parallel", "parallel")))(cidx, dw, got)


def _rs_add4(p, got, scidx, name):
    _, hr, cc = p.shape
    nb = hr // AT

    def body(s_ref, p_ref, g0_ref, g1_ref, g2_ref, o_ref):
        acc = p_ref[0] + g0_ref[0].astype(F32)
        acc = acc + g1_ref[0].astype(F32)
        o_ref[...] = acc + g2_ref[0].astype(F32)

    def gk(k):
        return pl.BlockSpec((1, AT, cc), lambda i, s_ref: (k, i, 0))

    return pl.pallas_call(
        body, name=name,
        grid_spec=pltpu.PrefetchScalarGridSpec(
            num_scalar_prefetch=1, grid=(nb,),
            in_specs=[pl.BlockSpec((1, AT, cc), lambda i, s_ref: (s_ref[0], i, 0)), gk(0), gk(1), gk(2)],
            out_specs=pl.BlockSpec((AT, cc), lambda i, s_ref: (s_ref[1] * nb + i, 0))),
        out_shape=jax.ShapeDtypeStruct((2 * hr, cc), F32),
        compiler_params=_cp(("parallel",)))(scidx, p, got, got, got)


def _rs_begin(dws, dwbs, after, tag=""):
    _, _, c = _my_pos()
    cidx = jnp.reshape(c, (1,)).astype(jnp.int32)
    got = _rs_pair(dwbs, "rs_pair" + tag)
    pairs = [_rs_add2(dws[w], got[w], cidx, "rs_add2") for w in range(len(dws))]
    pb = [p[1] for p in pairs]
    lands = [lax.empty((3,) + t.shape[1:], BF16) for t in pb]
    ssem, rsem, pb, lands, token = _split_start(pb, lands, after, _rs_copies, "rs_chip_start" + tag)
    return ([p[0] for p in pairs], ssem, rsem, pb, lands), token


def _rs_end(state, after, tag=""):
    x, y, c = _my_pos()
    scidx = jnp.stack([2 * x + y, c]).astype(jnp.int32)
    p, ssem, rsem, pb, lands = state
    _, recv = _split_wait(ssem, rsem, pb, lands, after, _rs_copies, "rs_chip_wait" + tag)
    q = [_rs_add4(p[w], recv[w], scidx, "rs_add4") for w in range(len(p))]
    return _rs_sib(q, "rs_sib" + tag)


def _allreduce_small(buf, name):
    rows = buf.shape[0]

    def body(src_ref, out_ref, gat_ref, send_sems, recv_sems):
        x, y, c = _my_pos()
        me = 4 * x + 2 * y + c
        gat_ref[me] = src_ref[...]
        cps = []
        for r in range(1, N_DEV):
            tx = 1 - x if (r >> 2) & 1 else x
            ty = 1 - y if (r >> 1) & 1 else y
            tc = 1 - c if r & 1 else c
            cps.append(pltpu.make_async_remote_copy(
                src_ref=src_ref, dst_ref=gat_ref.at[me], send_sem=send_sems.at[r - 1], recv_sem=recv_sems.at[r - 1],
                device_id=(tx, ty, tc), device_id_type=MESH))
        for cp in cps:
            cp.start()
        for cp in cps:
            cp.wait()
        acc = gat_ref[0]
        for k in range(1, N_DEV):
            acc = acc + gat_ref[k]
        out_ref[...] = acc

    return pl.pallas_call(
        body, name=name, in_specs=[VMEM_SPEC], out_specs=VMEM_SPEC, out_shape=jax.ShapeDtypeStruct((rows, 128), F32),
        scratch_shapes=[pltpu.VMEM((N_DEV, rows, 128), F32), pltpu.SemaphoreType.DMA((N_DEV - 1,)),
                        pltpu.SemaphoreType.DMA((N_DEV - 1,))],
        compiler_params=_cp())(buf)


SMALL = (("norm_mix_w", (D,)), ("ssd_conv_w", (4, XBC)), ("ssd_conv_b", (XBC,)), ("ssd_dt_bias", (NH,)),
         ("ssd_a_log", (NH,)), ("ssd_d", (NH,)), ("ssd_norm_w", (SSD_W,)), ("q_norm_w", (HD,)),
         ("k_norm_w", (HD,)), ("attn_sinks", (NQH,)), ("cm_dw_w", (CMK, CMC)), ("cm_dw_b", (CMC,)),
         ("cm_ln_w", (CMC,)), ("cm_ln_b", (CMC,)), ("norm_mlp_w", (D,)))
SHARDED_SMALL = ("ssd_conv_w", "cm_dw_w")


def _seg_len(shape):
    n = 1
    for d in shape:
        n *= d
    return -(-n // 128) * 128


def _pack_small(vals, names):
    parts = []
    for name, shape in SMALL:
        if name not in names:
            continue
        v = vals[name].reshape(DEPTH, -1)
        pad = _seg_len(shape) - v.shape[1]
        parts.append(jnp.pad(v, ((0, 0), (0, pad))))
    flat = jnp.concatenate(parts, axis=1)
    return flat.reshape(-1, 128)


def _unpack_small(buf, names):
    flat = buf.reshape(DEPTH, -1)
    out = {}
    off = 0
    for name, shape in SMALL:
        if name not in names:
            continue
        n = 1
        for d in shape:
            n *= d
        out[name] = flat[:, off:off + n].reshape((DEPTH,) + shape)
        off += _seg_len(shape)
    return out


SW = N_IN // 4
SWP = 1152
ORIG = (("z", 0, 1024), ("x", 1024, 2560), ("dt", 2560, 2576), ("q", 2576, 3088), ("k", 3088, 3216),
        ("v", 3216, 3344), ("a", 3344, 3856), ("g", 3856, 4368))


def _orig_cols(g_in, lo, hi):
    out = []
    for s in range(4):
        a, b = max(lo, s * SW), min(hi, (s + 1) * SW)
        if a < b:
            out.append(g_in[s][:, a - s * SW:b - s * SW])
    return out


def _shard_major(parts):
    cols = []
    for s in range(4):
        for name, g0, g1 in ORIG:
            a, b = max(g0, s * SW), min(g1, (s + 1) * SW)
            if a < b:
                cols.append(parts[name][:, a - g0:b - g0])
        cols.append(jnp.zeros((L, SWP - SW), BF16))
    return jnp.concatenate(cols, axis=1)


def _rope_tables():
    inv = 10000.0 ** (-jnp.arange(0, HD, 2, dtype=F32) / HD)
    ang = jnp.arange(L, dtype=F32)[:, None] * inv[None, :]
    return jnp.cos(ang), jnp.sin(ang)


def _swa_tables():
    cos, sin = _rope_tables()
    return jnp.tile(cos, (1, 4)), jnp.tile(jnp.concatenate([-sin, sin], axis=1), (1, 2))


def _swa_weights(w):
    return jnp.tile(w, 2)[None], jnp.tile(jnp.concatenate([w[HH:], w[:HH]]), 2)[None]


def _swa_sinks(s):
    s2 = s.reshape(NPAIR, 2)
    return (jnp.broadcast_to(s2[:, 0][:, None, None], (NPAIR, 1, 128)),
            jnp.broadcast_to(s2[:, 1][:, None, None], (NPAIR, 1, 128)))


def _swa_blockdiag():
    i = jnp.arange(128)
    return (i[:, None] // HD == i[None, :] // HD).astype(BF16)


def _pad128(v):
    return jnp.pad(v, (0, 128 - v.shape[0]))[None, :]


def kernel(x, norm_mix_w, w_in, ssd_conv_w, ssd_conv_b, ssd_dt_bias, ssd_a_log, ssd_d, ssd_norm_w, q_norm_w, k_norm_w, attn_sinks, cm_dw_w, cm_dw_b, cm_ln_w, cm_ln_b, w_out, norm_mlp_w, w_mlp_up, w_mlp_down, loss_target, m_norm_mix_w, m_w_in, m_ssd_conv_w, m_ssd_conv_b, m_ssd_dt_bias, m_ssd_a_log, m_ssd_d, m_ssd_norm_w, m_q_norm_w, m_k_norm_w, m_attn_sinks, m_cm_dw_w, m_cm_dw_b, m_cm_ln_w, m_cm_ln_b, m_w_out, m_norm_mlp_w, m_w_mlp_up, m_w_mlp_down, v_norm_mix_w, v_w_in, v_ssd_conv_w, v_ssd_conv_b, v_ssd_dt_bias, v_ssd_a_log, v_ssd_d, v_ssd_norm_w, v_q_norm_w, v_k_norm_w, v_attn_sinks, v_cm_dw_w, v_cm_dw_b, v_cm_ln_w, v_cm_ln_b, v_w_out, v_norm_mlp_w, v_w_mlp_up, v_w_mlp_down):
    px, py, pc = _my_pos()
    shard = 2 * px + py
    sidx = jnp.reshape(shard, (1,)).astype(jnp.int32)
    consts = _ssd_consts()
    cos4, sin4 = _swa_tables()
    bd = _swa_blockdiag()

    wb = [w.astype(BF16) for w in (jnp.pad(w_in, ((0, 0), (0, 0), (0, SWP - SW))), w_out, w_mlp_up, w_mlp_down)]

    def gather_start(l, sel, after, tag=""):
        own = [wb[w][l] for w in sel]
        lands = [lax.empty((4,) + t.shape, BF16) for t in own]
        return _split_start(own, lands, after, _ag_copies(False), "ag_start" + tag)

    def gather_finish(in_flight, after, tag=""):
        ssem, rsem, own, lands, _ = in_flight
        own, lands = _split_wait(ssem, rsem, own, lands, after, _ag_copies(True), "ag_wait" + tag)
        lands = _ag_forward(lands, "ag_forward" + tag)
        return [_place_own(o, g, sidx, "ag_place") for o, g in zip(own, lands)]

    zero_tile = jnp.zeros((8, 128), F32)
    first_in = gather_start(0, [0], zero_tile, "_in0")
    zc = jnp.zeros((DEPTH, 4, XBC), F32)
    zc = lax.dynamic_update_slice_in_dim(zc, ssd_conv_w, shard * (XBC // 4), axis=2)
    zd = jnp.zeros((DEPTH, CMK, CMC), F32)
    zd = lax.dynamic_update_slice_in_dim(zd, cm_dw_w, shard * (CMC // 4), axis=2)
    half = jnp.where(pc == 0, 1.0, 0.0).astype(F32)
    gw_packed = _allreduce_small(_pack_small({"ssd_conv_w": zc * half, "cm_dw_w": zd * half}, SHARDED_SMALL)
                                 + first_in[4][0:1, 0:1], "ag_small")
    first_rest = gather_start(0, [1, 2, 3], gw_packed, "_rest0")
    gw = _unpack_small(gw_packed, SHARDED_SMALL)
    conv_w_full, dw_w_full = gw["ssd_conv_w"], gw["cm_dw_w"]

    xcur = x[0]
    saved = []
    in_flight = None
    for l in range(DEPTH):
        if l == 0:
            (g_in,) = gather_finish(first_in, first_rest[4], "_in0")
            nxt_after = g_in
        else:
            g_in, g_out, g_up, g_dn = gather_finish(in_flight, xcur)
            nxt_after = g_dn
        nmw = norm_mix_w[l][None]
        if l + 1 < DEPTH:
            in_flight = gather_start(l + 1, [0, 1, 2, 3], nxt_after)
            nmw = nmw + in_flight[4][0:1, 0:1]
        grp = dict((n, (a, b)) for n, a, b in ORIG)
        w_perm = jnp.concatenate(
            _orig_cols(g_in, *grp["x"]) + _orig_cols(g_in, *grp["z"]) + _orig_cols(g_in, grp["a"][0], grp["g"][1])
            + _orig_cols(g_in, grp["q"][0], grp["v"][1]) + _orig_cols(g_in, *grp["dt"])
            + [jnp.zeros((D, 128 - NH), BF16)], axis=1)
        h = _rms_fwd(xcur, nmw, "rms_mix_fwd")
        u = _mm(h, w_perm, "nn", "in_proj", tn=640)
        alog = _pad128(ssd_a_log[l])
        dtb = _pad128(ssd_dt_bias[l])
        dxp = jnp.repeat(ssd_d[l], HP)[None, :]
        ssd_p = (conv_w_full[l], ssd_conv_b[l][None], dtb, alog, dxp, ssd_norm_w[l][None])
        ya, ypre, st = _ssd_fwd(u, *ssd_p, consts, "ssd_fwd")
        swa_p = (cos4, sin4, *_swa_weights(q_norm_w[l]), *_swa_weights(k_norm_w[l]), *_swa_sinks(attn_sinks[l]), bd)
        yb = _swa_fwd(u, *swa_p, "attn_fwd")
        cconv = _conf_conv_fwd(u, dw_w_full[l], cm_dw_b[l][None], "conf_conv_fwd")
        conf_p = (cconv, cm_ln_w[l][None], cm_ln_b[l][None])
        yc = _conf_ln_fwd(*conf_p, "conf_ln_fwd")
        if l == 0:
            g_out, g_up, g_dn = gather_finish(first_rest, yc, "_rest0")
        g_out = g_out.reshape(2 * D, D)
        g_dn = g_dn.reshape(DFF, D)
        ycat = jnp.concatenate([ya, yb, yc], axis=1)
        x1 = _mm(ycat, g_out, "nn", "out_proj", add=xcur)
        hm = _rms_fwd(x1, norm_mlp_w[l][None], "rms_mlp_fwd")
        r_up = _mm_up(hm, g_up, "mlp_up")
        x2 = _mm(r_up, g_dn, "nn", "mlp_down", add=x1)
        saved.append(dict(x=xcur, h=h, u=u, ypre=ypre, st=st, swa_p=swa_p, conf_p=conf_p, ycat=ycat, x1=x1,
                          hm=hm, r_up=r_up, ssd_p=ssd_p, g_in=g_in, g_out=g_out, g_up=g_up, g_dn=g_dn))
        xcur = x2

    lsum, dx, dxb = _loss_bwd(xcur, loss_target[0], "loss")

    loc = locals()
    names = ["norm_mix_w", "w_in", "ssd_conv_w", "ssd_conv_b", "ssd_dt_bias", "ssd_a_log", "ssd_d", "ssd_norm_w",
             "q_norm_w", "k_norm_w", "attn_sinks", "cm_dw_w", "cm_dw_b", "cm_ln_w", "cm_ln_b", "w_out", "norm_mlp_w",
             "w_mlp_up", "w_mlp_down"]
    weights = {n: loc[n] for n in names}
    moms = {n: loc["m_" + n] for n in names}
    vars_ = {n: loc["v_" + n] for n in names}
    big_out = {n: None for n in BIG}
    win_grads = [None] * DEPTH

    def finish_layer(layer, which, shard_grads):
        for n, g in zip(which, shard_grads):
            if n == "w_in":
                win_grads[layer] = g
            else:
                big_out[n] = _adamw_layer(weights[n], g, moms[n], vars_[n], layer, big_out[n], "adamw_" + n)

    pending = None
    last_mlp = None
    gsm = {name: [] for name, _ in SMALL}
    for l in reversed(range(DEPTH)):
        sv = saved[l]
        da = _mm(dxb, sv["g_dn"], "nt", "mlp_down_dx", relu2_of=sv["r_up"])
        dwdn, dwdn_b = _mm_dw(sv["r_up"], dxb, "mlp_down_dw")
        dwup, dwup_b = _mm_dw(sv["hm"], da, "mlp_up_dw", col_shards=True)
        dhm = _mm_cs_nt(da, sv["g_up"], "mlp_up_dx")
        nlw = norm_mlp_w[l][None]
        if l == 0:
            last_mlp, token = _rs_begin([dwup, dwdn.reshape(4, D, D)], [dwup_b, dwdn_b.reshape(4, D, D)], zero_tile,
                                        "_mlp0")
            nlw = nlw + token[0:1, 0:1]
        dx1, dx1b, dnw = _rms_bwd(sv["x1"], nlw, dhm, dx, "rms_mlp_bwd")
        gsm["norm_mlp_w"].append(dnw[0])
        dy = _mm(dx1b, sv["g_out"], "nt", "out_proj_dx")
        dwout, dwout_b = _mm_dw(sv["ycat"], dx1b, "out_proj_dw")
        dcc, dwb, dlw, dlb = _conf_ln_bwd(*sv["conf_p"], dy, "conf_ln_bwd")
        da_c, dg_c, dww = _conf_conv_bwd(sv["u"], dcc, dw_w_full[l], "conf_conv_bwd")
        gsm["cm_dw_w"].append(dww[:CMK])
        gsm["cm_dw_b"].append(dwb[0])
        gsm["cm_ln_w"].append(dlw[0])
        gsm["cm_ln_b"].append(dlb[0])
        dq, dk, dv, dqw, dkw, dse, dso = _swa_bwd(sv["u"], dy, *sv["swa_p"], "attn_bwd")
        gsm["q_norm_w"].append(dqw[0, :HD] + dqw[0, HD:])
        gsm["k_norm_w"].append(dkw[0, :HD] + dkw[0, HD:])
        gsm["attn_sinks"].append(jnp.stack([dse[:, 0, 0], dso[:, 0, 0]], axis=1).reshape(NQH))
        (dz, dxr, ddtr, dcw, dcb, ddtb, dalog, ddd, dnsw) = _ssd_bwd(
            sv["u"], sv["ypre"], sv["st"], dy, *sv["ssd_p"], consts, "ssd_bwd")
        gsm["ssd_conv_w"].append(dcw[:4])
        gsm["ssd_conv_b"].append(dcb[0])
        gsm["ssd_dt_bias"].append(ddtb[0, :NH])
        gsm["ssd_a_log"].append(dalog[0, :NH])
        gsm["ssd_d"].append(ddd[0, :NH])
        gsm["ssd_norm_w"].append(dnsw[0])
        du = _shard_major(dict(z=dz, x=dxr, dt=ddtr[:, :NH], q=dq, k=dk, v=dv, a=da_c, g=dg_c))
        dwin, dwin_b = _mm_dw(sv["h"], du, "in_dw", col_shards=True, tn=SWP // 3)
        if l == 0:
            state, token = _rs_begin([dwin, dwout.reshape(4, D // 2, D)], [dwin_b, dwout_b.reshape(4, D // 2, D)],
                                     zero_tile, "_io0")
        else:
            state, token = _rs_begin(
                [dwin, dwout.reshape(4, D // 2, D), dwup, dwdn.reshape(4, D, D)],
                [dwin_b, dwout_b.reshape(4, D // 2, D), dwup_b, dwdn_b.reshape(4, D, D)], zero_tile)
        dh = _mm_cs_nt(du, sv["g_in"], "in_dx")
        dx, dxb, dnm = _rms_bwd(sv["x"], norm_mix_w[l][None] + token[0:1, 0:1], dh, dx1, "rms_mix_bwd")
        gsm["norm_mix_w"].append(dnm[0])
        if pending is not None:
            finish_layer(l + 1, BIG, _rs_end(pending, dx))
        pending = state

    finish_layer(0, BIG[2:], _rs_end(last_mlp, dx, "_mlp0"))
    gsm = {k: jnp.stack(v[::-1]) for k, v in gsm.items()}
    packed = _pack_small(gsm, [n for n, _ in SMALL])
    packed = jnp.concatenate([packed, lsum], axis=0)
    red = _allreduce_small(packed, "ar_small")
    finish_layer(0, BIG[:2], _rs_end(pending, red, "_io0"))
    loss = 0.5 * red[-8, 0] / D
    gsm = _unpack_small(red[:-8], [n for n, _ in SMALL])
    gsm["ssd_conv_w"] = lax.dynamic_slice_in_dim(gsm["ssd_conv_w"], shard * (XBC // 4), XBC // 4, axis=2)
    gsm["cm_dw_w"] = lax.dynamic_slice_in_dim(gsm["cm_dw_w"], shard * (CMC // 4), CMC // 4, axis=2)
    grads = dict(gsm)
    delta, new_m, new_v = {}, {}, {}
    for n in BIG[1:]:
        grads[n], delta[n], new_m[n], new_v[n] = big_out[n]
    to_lead = lambda t: jnp.transpose(t, (2, 0, 1))
    g_lead = jnp.stack([jnp.transpose(g[:, :SW]) for g in win_grads], axis=1)
    d_lead, m_lead, v_lead = _adamw_lead(to_lead(w_in), g_lead, to_lead(m_w_in), to_lead(v_w_in), "adamw_w_in")
    from_lead = lambda t: jnp.transpose(t, (1, 2, 0))
    grads["w_in"], delta["w_in"] = from_lead(g_lead), from_lead(d_lead)
    new_m["w_in"], new_v["w_in"] = from_lead(m_lead), from_lead(v_lead)

    packed_names = [n for n, _ in SMALL if n not in SHARDED_SMALL]
    pw = _pack_small(weights, packed_names)
    pg = _pack_small(grads, packed_names)
    pm = _pack_small(moms, packed_names)
    pv = _pack_small(vars_, packed_names)
    pd, pmn, pvn = _adamw(pw, pg, pm, pv, "adamw_small")
    for dst, buf in ((delta, pd), (new_m, pmn), (new_v, pvn)):
        dst.update(_unpack_small(buf, packed_names))
    for n in SHARDED_SMALL:
        shp = weights[n].shape
        flat = lambda t: t.reshape(-1, shp[-1])
        d_, m_, v_ = _adamw(flat(weights[n]), flat(grads[n]), flat(moms[n]), flat(vars_[n]), "adamw_" + n)
        delta[n], new_m[n], new_v[n] = d_.reshape(shp), m_.reshape(shp), v_.reshape(shp)

    return (loss, dx[None], *[grads[n] for n in names], *[delta[n] for n in names],
            *[new_m[n] for n in names], *[new_v[n] for n in names])
```

```python
import functools
import math

import jax
import jax.numpy as jnp
from jax import lax
from jax.experimental import pallas as pl
from jax.experimental.pallas import tpu as pltpu

F32 = jnp.float32
BF16 = jnp.bfloat16
MESH = pl.DeviceIdType.MESH
ANY = pl.BlockSpec(memory_space=pl.ANY)
VMEM_SPEC = pl.BlockSpec(memory_space=pltpu.VMEM)

D = 1024
L = 2048
DEPTH = 4
SSD_W = 1024
XBC = 1536
NH = 16
HP = 64
NS = 128
Q = 128
NC = L // Q
ATT_W = 512
NQH = 8
NKV = 2
HD = 64
HH = HD // 2
CMC = 512
CMK = 31
DFF = 4096
N_IN = 4368
N_PAD = 4480
RMS_EPS = 1e-6
LN_EPS = 1e-5
NEG = -1e30
LR, B1, B2, EPS_A, WD, STEP = 0.001, 0.9, 0.999, 1e-8, 0.01, 10
VMEM_LIMIT = 56 * 1024 * 1024
N_DEV = 8


def _cp(sem=None):
    kw = dict(vmem_limit_bytes=VMEM_LIMIT)
    if sem is not None:
        kw["dimension_semantics"] = sem
    return pltpu.CompilerParams(**kw)


def _dg(a, b, ca, cb):
    return lax.dot_general(a, b, (((ca,), (cb,)), ((), ())), preferred_element_type=F32)


def _split3(x):
    hi = x.astype(BF16)
    r = x - hi.astype(F32)
    mid = r.astype(BF16)
    lo = (r - mid.astype(F32)).astype(BF16)
    return hi, mid, lo


def _xdot_l(x, m, ca=1, cb=0):
    hi, mid, lo = _split3(x)
    return _dg(hi, m, ca, cb) + _dg(mid, m, ca, cb) + _dg(lo, m, ca, cb)


def _xdot_r(m, x, ca=1, cb=0):
    hi, mid, lo = _split3(x)
    return _dg(m, hi, ca, cb) + _dg(m, mid, ca, cb) + _dg(m, lo, ca, cb)


def _rowdot(v, m):
    return _xdot_l(jnp.broadcast_to(v, (8, v.shape[1])), m)[0:1]


def _sigmoid(x):
    return 1.0 / (1.0 + jnp.exp(-x))


def _softplus(x):
    e = jnp.exp(-jnp.abs(x))
    u = 1.0 + e
    l1p = jnp.where(u == 1.0, e, jnp.log(u) * (e / jnp.where(u == 1.0, 1.0, u - 1.0)))
    return jnp.maximum(x, 0.0) + l1p


def _tm(k):
    return L if k <= D else L // 2


def _mm(a, b, mode, name, add=None, relu2_of=None, tn=512):
    m, k = a.shape
    tm = min(m, _tm(k))
    a_spec = pl.BlockSpec((tm, k), lambda i, j: (i, 0))
    if mode == "nn":
        n = b.shape[1]
        b_spec = pl.BlockSpec((k, tn), lambda i, j: (0, j))
        cb = 0
    else:
        n = b.shape[0]
        b_spec = pl.BlockSpec((tn, k), lambda i, j: (j, 0))
        cb = 1
    assert m % tm == 0 and n % tn == 0, (m, n, tm, tn)
    o_spec = pl.BlockSpec((tm, tn), lambda i, j: (i, j))
    out_dtype = F32
    if relu2_of is not None:
        def body(a_ref, b_ref, c_ref, o_ref):
            o_ref[...] = (_dg(a_ref[...], b_ref[...], 1, cb) * (2.0 * jnp.sqrt(c_ref[...].astype(F32)))).astype(BF16)
        ins, specs, out_dtype = (a, b, relu2_of), [a_spec, b_spec, o_spec], BF16
    elif add is None:
        def body(a_ref, b_ref, o_ref):
            o_ref[...] = _dg(a_ref[...], b_ref[...], 1, cb)
        ins, specs = (a, b), [a_spec, b_spec]
    else:
        def body(a_ref, b_ref, c_ref, o_ref):
            o_ref[...] = _dg(a_ref[...], b_ref[...], 1, cb) + c_ref[...]
        ins, specs = (a, b, add), [a_spec, b_spec, o_spec]
    return pl.pallas_call(
        body, name=name, grid=(m // tm, n // tn), in_specs=specs, out_specs=o_spec,
        out_shape=jax.ShapeDtypeStruct((m, n), out_dtype), compiler_params=_cp(("parallel", "parallel")))(*ins)


def _mm_up(a, b, name, tn=512):
    m = a.shape[0]
    cs = DFF // 4
    per = cs // tn
    tm = min(m, _tm(D))

    def body(a_ref, b_ref, r_ref):
        r = jnp.maximum(_dg(a_ref[...], b_ref[0], 1, 0), 0.0)
        r_ref[...] = (r * r).astype(BF16)

    return pl.pallas_call(
        body, name=name, grid=(m // tm, DFF // tn),
        in_specs=[pl.BlockSpec((tm, D), lambda i, j: (i, 0)),
                  pl.BlockSpec((1, D, tn), lambda i, j: (j // per, 0, j % per))],
        out_specs=pl.BlockSpec((tm, tn), lambda i, j: (i, j)),
        out_shape=jax.ShapeDtypeStruct((m, DFF), BF16), compiler_params=_cp(("parallel", "parallel")))(a, b)


def _mm_cs_nt(a, b, name, tn=512):
    m = a.shape[0]
    _, n, cs = b.shape
    tm = min(m, _tm(4 * cs))

    def body(a_ref, b_ref, o_ref):
        acc = _dg(a_ref[:, 0:cs], b_ref[0], 1, 1)
        for s in range(1, 4):
            acc = acc + _dg(a_ref[:, s * cs:(s + 1) * cs], b_ref[s], 1, 1)
        o_ref[...] = acc

    return pl.pallas_call(
        body, name=name, grid=(m // tm, n // tn),
        in_specs=[pl.BlockSpec((tm, 4 * cs), lambda i, j: (i, 0)), pl.BlockSpec((4, tn, cs), lambda i, j: (0, j, 0))],
        out_specs=pl.BlockSpec((tm, tn), lambda i, j: (i, j)),
        out_shape=jax.ShapeDtypeStruct((m, n), F32), compiler_params=_cp(("parallel", "parallel")))(a, b)


def _mm_dw(a, b, name, col_shards=False, tn=512):
    k, m = a.shape
    n = b.shape[1]
    tm = min(m, D)
    a_spec = pl.BlockSpec((k, tm), lambda i, j: (0, i))
    b_spec = pl.BlockSpec((k, tn), lambda i, j: (0, j))
    if col_shards:
        per = (n // 4) // tn
        o_spec = pl.BlockSpec((1, tm, tn), lambda i, j: (j // per, i, j % per))
        shape = (4, m, n // 4)
    else:
        o_spec = pl.BlockSpec((tm, tn), lambda i, j: (i, j))
        shape = (m, n)

    def body(a_ref, b_ref, o_ref, ob_ref):
        acc = _dg(a_ref[...], b_ref[...], 0, 0).reshape(o_ref.shape)
        o_ref[...] = acc
        ob_ref[...] = acc.astype(BF16)

    return pl.pallas_call(
        body, name=name, grid=(m // tm, n // tn), in_specs=[a_spec, b_spec], out_specs=[o_spec, o_spec],
        out_shape=[jax.ShapeDtypeStruct(shape, F32), jax.ShapeDtypeStruct(shape, BF16)],
        compiler_params=_cp(("parallel", "parallel")))(a, b)


TR = 256


def _rms_fwd(x, w, name):
    def body(x_ref, w_ref, o_ref):
        xv = x_ref[...]
        r = lax.rsqrt(jnp.mean(xv * xv, axis=-1, keepdims=True) + RMS_EPS)
        o_ref[...] = (xv * r * w_ref[...]).astype(BF16)

    return pl.pallas_call(
        body, name=name, grid=(L // TR,),
        in_specs=[pl.BlockSpec((TR, D), lambda i: (i, 0)), pl.BlockSpec((1, D), lambda i: (0, 0))],
        out_specs=pl.BlockSpec((TR, D), lambda i: (i, 0)),
        out_shape=jax.ShapeDtypeStruct((L, D), BF16), compiler_params=_cp(("parallel",)))(x, w)


def _rms_bwd(x, w, dh, dres, name):
    def body(x_ref, w_ref, dh_ref, dr_ref, dx_ref, dxb_ref, dw_ref):
        xv = x_ref[...]
        r = lax.rsqrt(jnp.mean(xv * xv, axis=-1, keepdims=True) + RMS_EPS)
        n = xv * r
        dhv = dh_ref[...]
        g = dhv * w_ref[...]
        dx = dr_ref[...] + r * (g - n * jnp.mean(g * n, axis=-1, keepdims=True))
        dx_ref[...] = dx
        dxb_ref[...] = dx.astype(BF16)

        @pl.when(pl.program_id(0) == 0)
        def _():
            dw_ref[...] = jnp.zeros_like(dw_ref)
        dw_ref[...] += jnp.sum(dhv * n, axis=0, keepdims=True)

    row = pl.BlockSpec((TR, D), lambda i: (i, 0))
    vec = pl.BlockSpec((1, D), lambda i: (0, 0))
    return pl.pallas_call(
        body, name=name, grid=(L // TR,), in_specs=[row, vec, row, row], out_specs=[row, row, vec],
        out_shape=[jax.ShapeDtypeStruct((L, D), F32), jax.ShapeDtypeStruct((L, D), BF16),
                   jax.ShapeDtypeStruct((1, D), F32)],
        compiler_params=_cp(("arbitrary",)))(x, w, dh, dres)


def _adamw_lead(w, g, m, v, name):
    lead, a, b = w.shape
    tr = max(t for t in range(1, lead + 1) if lead % t == 0 and t * a * b * 4 <= 2 * 1024 * 1024)
    c1 = 1.0 / (1.0 - B1 ** STEP)
    c2 = 1.0 / (1.0 - B2 ** STEP)

    def body(w_ref, g_ref, m_ref, v_ref, d_ref, mo_ref, vo_ref):
        gv = g_ref[...]
        mn = B1 * m_ref[...] + (1.0 - B1) * gv
        vn = B2 * v_ref[...] + (1.0 - B2) * (gv * gv)
        mo_ref[...] = mn
        vo_ref[...] = vn
        d_ref[...] = -LR * ((mn * c1) / (jnp.sqrt(vn * c2) + EPS_A) + WD * w_ref[...])

    blk = pl.BlockSpec((tr, a, b), lambda i: (i, 0, 0))
    shp = jax.ShapeDtypeStruct(w.shape, F32)
    return pl.pallas_call(body, name=name, grid=(lead // tr,), in_specs=[blk] * 4, out_specs=[blk] * 3,
                          out_shape=[shp] * 3, compiler_params=_cp(("parallel",)))(w, g, m, v)


def _adamw_layer(w, g, m, v, layer, prev, name):
    _, rows, cols = w.shape
    tr = 256 if cols * 256 * 4 <= 2 * 1024 * 1024 else 128
    c1 = 1.0 / (1.0 - B1 ** STEP)
    c2 = 1.0 / (1.0 - B2 ** STEP)
    n_prev = 0 if prev is None else 4

    def body(*refs):
        w_ref, g_ref, m_ref, v_ref = refs[:4]
        go_ref, d_ref, mo_ref, vo_ref = refs[4 + n_prev:]
        gv = g_ref[...]
        mn = B1 * m_ref[0] + (1.0 - B1) * gv
        vn = B2 * v_ref[0] + (1.0 - B2) * (gv * gv)
        go_ref[0] = gv
        mo_ref[0] = mn
        vo_ref[0] = vn
        d_ref[0] = -LR * ((mn * c1) / (jnp.sqrt(vn * c2) + EPS_A) + WD * w_ref[0])

    lay = pl.BlockSpec((1, tr, cols), lambda i: (layer, i, 0))
    shp = jax.ShapeDtypeStruct(w.shape, F32)
    return pl.pallas_call(
        body, name=name, grid=(rows // tr,),
        in_specs=[lay, pl.BlockSpec((tr, cols), lambda i: (i, 0)), lay, lay] + [ANY] * n_prev,
        out_specs=[lay] * 4, out_shape=[shp] * 4,
        input_output_aliases={4 + i: i for i in range(n_prev)},
        compiler_params=_cp(("parallel",)))(w, g, m, v, *(prev or ()))


def _loss_bwd(y, t, name):
    def body(y_ref, t_ref, l_ref, d_ref, db_ref):
        e = y_ref[...] - t_ref[...]
        d = e * (1.0 / D)
        d_ref[...] = d
        db_ref[...] = d.astype(BF16)

        @pl.when(pl.program_id(0) == 0)
        def _():
            l_ref[...] = jnp.zeros_like(l_ref)
        s = jnp.sum(jnp.sum(e * e, axis=-1, keepdims=True), axis=0, keepdims=True)
        l_ref[...] += jnp.broadcast_to(s, l_ref.shape)

    row = pl.BlockSpec((TR, D), lambda i: (i, 0))
    tile = pl.BlockSpec((8, 128), lambda i: (0, 0))
    return pl.pallas_call(
        body, name=name, grid=(L // TR,), in_specs=[row, row], out_specs=[tile, row, row],
        out_shape=[jax.ShapeDtypeStruct((8, 128), F32), jax.ShapeDtypeStruct((L, D), F32),
                   jax.ShapeDtypeStruct((L, D), BF16)],
        compiler_params=_cp(("arbitrary",)))(y, t)


def _adamw(w, g, m, v, name):
    rows, cols = w.shape
    tr = rows
    for cand in (512, 256, 128, 64, 32, 16, 8):
        if rows % cand == 0 and cand * cols * 4 <= 2 * 1024 * 1024:
            tr = cand
            break
    c1 = 1.0 / (1.0 - B1 ** STEP)
    c2 = 1.0 / (1.0 - B2 ** STEP)

    def body(w_ref, g_ref, m_ref, v_ref, d_ref, mo_ref, vo_ref):
        gv = g_ref[...]
        mn = B1 * m_ref[...] + (1.0 - B1) * gv
        vn = B2 * v_ref[...] + (1.0 - B2) * (gv * gv)
        mo_ref[...] = mn
        vo_ref[...] = vn
        d_ref[...] = -LR * ((mn * c1) / (jnp.sqrt(vn * c2) + EPS_A) + WD * w_ref[...])

    blk = pl.BlockSpec((tr, cols), lambda i: (i, 0))
    shp = jax.ShapeDtypeStruct((rows, cols), F32)
    return pl.pallas_call(body, name=name, grid=(rows // tr,), in_specs=[blk] * 4, out_specs=[blk] * 3,
                          out_shape=[shp] * 3, compiler_params=_cp(("parallel",)))(w, g, m, v)


CT = 256
CPAD = 32


U_X, U_Z, U_A, U_G, U_Q, U_K, U_V, U_DT = 0, 1536, 2560, 3072, 3584, 4096, 4224, 4352


CEXT = 8
CWIN = CT + CPAD
CROWS = L + CPAD + CEXT


def _fill_shifted(src_ref, base, win_ref, sh_ref):
    win_ref[...] = src_ref[pl.ds(base, CWIN + CEXT), :]
    for p in range(8):
        sh_ref[p] = win_ref[pl.ds(p, CWIN), :]


def _tap(sh_ref, o):
    return sh_ref[o % 8, 8 * (o // 8):8 * (o // 8) + CT, :]


CB = 128


def _conv_specs():
    return [pl.BlockSpec((L, CB), lambda j: (0, U_A // CB + j)), pl.BlockSpec((L, CB), lambda j: (0, U_G // CB + j))]


def _conv_scratch(n_padded):
    return ([pltpu.VMEM((CROWS, CB), F32)] * n_padded
            + [pltpu.VMEM((CWIN + CEXT, CB), F32), pltpu.VMEM((8, CWIN, CB), F32)])


def _fill_gated(a_ref, g_ref, hp_ref):
    hp_ref[0:CPAD, :] = jnp.zeros((CPAD, CB), F32)
    hp_ref[CPAD:CPAD + L, :] = a_ref[...] * _sigmoid(g_ref[...])
    hp_ref[CPAD + L:, :] = jnp.zeros((CEXT, CB), F32)


def _conf_conv_fwd(u, w, b, name):
    def body(a_ref, g_ref, w_ref, b_ref, c_ref, hp_ref, win_ref, sh_ref):
        _fill_gated(a_ref, g_ref, hp_ref)

        def tile(i, carry):
            base = pl.multiple_of(i * CT, CT)
            _fill_shifted(hp_ref, base, win_ref, sh_ref)
            c = jnp.broadcast_to(b_ref[...], (CT, CB))
            for k in range(CMK):
                c = c + w_ref[k:k + 1, :] * _tap(sh_ref, 2 + k)
            c_ref[pl.ds(base, CT), :] = c
            return carry

        lax.fori_loop(0, L // CT, tile, 0)

    return pl.pallas_call(
        body, name=name, grid=(CMC // CB,),
        in_specs=_conv_specs() + [pl.BlockSpec((CMK, CB), lambda j: (0, j)), pl.BlockSpec((1, CB), lambda j: (0, j))],
        out_specs=pl.BlockSpec((L, CB), lambda j: (0, j)),
        out_shape=jax.ShapeDtypeStruct((L, CMC), F32), scratch_shapes=_conv_scratch(1),
        compiler_params=_cp(("parallel",)))(u, u, w, b)


def _conf_ln_fwd(c, lw, lb, name):
    def body(c_ref, lw_ref, lb_ref, o_ref):
        cv = c_ref[...]
        cc = cv - jnp.mean(cv, axis=-1, keepdims=True)
        var = jnp.mean(cc * cc, axis=-1, keepdims=True)
        l = cc * lax.rsqrt(var + LN_EPS) * lw_ref[...] + lb_ref[...]
        o_ref[...] = (l * _sigmoid(l)).astype(BF16)

    row = pl.BlockSpec((TR, CMC), lambda i: (i, 0))
    vec = pl.BlockSpec((1, CMC), lambda i: (0, 0))
    return pl.pallas_call(body, name=name, grid=(L // TR,), in_specs=[row, vec, vec], out_specs=row,
                          out_shape=jax.ShapeDtypeStruct((L, CMC), BF16),
                          compiler_params=_cp(("parallel",)))(c, lw, lb)


def _conf_ln_bwd(c, lw, lb, dy, name):
    def body(c_ref, lw_ref, lb_ref, dy_ref, dc_ref, db_ref, dlw_ref, dlb_ref):
        cv = c_ref[...]
        cc = cv - jnp.mean(cv, axis=-1, keepdims=True)
        var = jnp.mean(cc * cc, axis=-1, keepdims=True)
        rstd = lax.rsqrt(var + LN_EPS)
        n = cc * rstd
        l = n * lw_ref[...] + lb_ref[...]
        sl = _sigmoid(l)
        dl = dy_ref[...] * (sl * (1.0 + l * (1.0 - sl)))
        dn = dl * lw_ref[...]
        dc = rstd * (dn - jnp.mean(dn, axis=-1, keepdims=True) - n * jnp.mean(dn * n, axis=-1, keepdims=True))
        dc_ref[...] = dc

        @pl.when(pl.program_id(0) == 0)
        def _():
            db_ref[...] = jnp.zeros_like(db_ref)
            dlw_ref[...] = jnp.zeros_like(dlw_ref)
            dlb_ref[...] = jnp.zeros_like(dlb_ref)
        db_ref[...] += jnp.sum(dc, axis=0, keepdims=True)
        dlw_ref[...] += jnp.sum(dl * n, axis=0, keepdims=True)
        dlb_ref[...] += jnp.sum(dl, axis=0, keepdims=True)

    row = pl.BlockSpec((TR, CMC), lambda i: (i, 0))
    vec = pl.BlockSpec((1, CMC), lambda i: (0, 0))
    vshape = jax.ShapeDtypeStruct((1, CMC), F32)
    return pl.pallas_call(
        body, name=name, grid=(L // TR,),
        in_specs=[row, vec, vec, pl.BlockSpec((TR, CMC), lambda i: (i, (SSD_W + ATT_W) // CMC))],
        out_specs=[row, vec, vec, vec], out_shape=[jax.ShapeDtypeStruct((L, CMC), F32), vshape, vshape, vshape],
        compiler_params=_cp(("arbitrary",)))(c, lw, lb, dy)


def _conf_conv_bwd(u, dc, w, name):
    def body(a_ref, g_ref, dc_ref, w_ref, da_ref, dg_ref, dw_ref, hp_ref, dcp_ref, win_ref, sh_ref, dwacc_ref):
        _fill_gated(a_ref, g_ref, hp_ref)
        dcp_ref[0:L, :] = dc_ref[...]
        dcp_ref[L:, :] = jnp.zeros((CPAD + CEXT, CB), F32)
        dwacc_ref[...] = jnp.zeros_like(dwacc_ref)

        def tile(i, carry):
            base = pl.multiple_of(i * CT, CT)
            _fill_shifted(hp_ref, base, win_ref, sh_ref)
            dcv = dcp_ref[pl.ds(base, CT), :]
            for k in range(CMK):
                dwacc_ref[k] += (dcv * _tap(sh_ref, 2 + k)).reshape(CT // 8, 8, CB).sum(axis=0)
            _fill_shifted(dcp_ref, base, win_ref, sh_ref)
            dh = jnp.zeros((CT, CB), F32)
            for k in range(CMK):
                dh = dh + w_ref[k:k + 1, :] * _tap(sh_ref, CMK - 1 - k)
            av = a_ref[pl.ds(base, CT), :]
            sg = _sigmoid(g_ref[pl.ds(base, CT), :])
            da_ref[pl.ds(base, CT), :] = (dh * sg).astype(BF16)
            dg_ref[pl.ds(base, CT), :] = (dh * av * sg * (1.0 - sg)).astype(BF16)
            return carry

        lax.fori_loop(0, L // CT, tile, 0)
        for k in range(CMK):
            dw_ref[k:k + 1, :] = jnp.sum(dwacc_ref[k], axis=0, keepdims=True)
        dw_ref[CMK:, :] = jnp.zeros((32 - CMK, CB), F32)

    col = pl.BlockSpec((L, CB), lambda j: (0, j))
    return pl.pallas_call(
        body, name=name, grid=(CMC // CB,),
        in_specs=_conv_specs() + [col, pl.BlockSpec((CMK, CB), lambda j: (0, j))],
        out_specs=[col, col, pl.BlockSpec((32, CB), lambda j: (0, j))],
        out_shape=[jax.ShapeDtypeStruct((L, CMC), BF16), jax.ShapeDtypeStruct((L, CMC), BF16),
                   jax.ShapeDtypeStruct((32, CMC), F32)],
        scratch_shapes=_conv_scratch(2) + [pltpu.VMEM((32, 8, CB), F32)],
        compiler_params=_cp(("parallel",)))(u, u, dc, w)


NPAIR = NQH // 2


def _partner(x, lo32):
    return jnp.where(lo32, pltpu.roll(x, 96, 1), pltpu.roll(x, 32, 1))


def _swa_prep(x, w2, w2p, c4, s4, bd, lo32):
    r = lax.rsqrt(_xdot_l(x * x, bd) * (1.0 / HD) + RMS_EPS)
    xh = x * r
    return r, xh, xh * w2 * c4 + _partner(xh, lo32) * w2p * s4


def _swa_unprep(dr, r, xh, w2, w2p, c4, s4, bd, lo32):
    dn = dr * c4
    dnp = dr * s4
    gx = dn * w2 + _partner(dnp * w2p, lo32)
    dw = jnp.sum((dn + _partner(dnp, lo32)) * xh, axis=0, keepdims=True)
    mu = _xdot_l(gx * xh, bd) * (1.0 / HD)
    return r * (gx - xh * mu), dw


def _swa_softmax(s, sink):
    row = lax.broadcasted_iota(jnp.int32, (L, 2 * Q), 0)
    col = lax.broadcasted_iota(jnp.int32, (L, 2 * Q), 1)
    rm = row & (Q - 1)
    valid = (col > rm) & (col <= rm + Q) & ((row >= Q) | (col >= Q))
    s = jnp.where(valid, s * (1.0 / math.sqrt(HD)), NEG)
    m = jnp.maximum(jnp.max(s, axis=-1, keepdims=True), sink)
    p = jnp.exp(s - m)
    ps = jnp.exp(sink - m)
    inv = 1.0 / (jnp.sum(p, axis=-1, keepdims=True) + ps)
    return p * inv, ps * inv


def _swa_in_specs():
    tab = pl.BlockSpec((L, 128), lambda p: (0, 0))
    wv = pl.BlockSpec((1, 128), lambda p: (0, 0))
    sk = pl.BlockSpec((1, 1, 128), lambda p: (p, 0, 0))
    return [pl.BlockSpec((L, 128), lambda p: (0, U_Q // 128 + p)), pl.BlockSpec((L, 128), lambda p: (0, U_K // 128)),
            pl.BlockSpec((L, 128), lambda p: (0, U_V // 128)), tab, tab, wv, wv, wv, wv, sk, sk,
            pl.BlockSpec((128, 128), lambda p: (0, 0))]


def _swa_setup(q_ref, k_ref, v_ref, c_ref, s_ref, qw_ref, qwp_ref, kw_ref, kwp_ref, bd_ref, kpad, vpad):
    g = pl.program_id(0) // 2
    lane = lax.broadcasted_iota(jnp.int32, (L, 128), 1)
    lo32 = (lane & 32) == 0
    own = (lane >> 6) == g
    c4, s4, bd = c_ref[...], s_ref[...], bd_ref[...]
    qn = _swa_prep(q_ref[...], qw_ref[...], qwp_ref[...], c4, s4, bd, lo32)
    kn = _swa_prep(k_ref[...], kw_ref[...], kwp_ref[...], c4, s4, bd, lo32)
    vv = v_ref[...]
    kpad[0:Q, :] = jnp.zeros((Q, 128), BF16)
    vpad[0:Q, :] = jnp.zeros((Q, 128), BF16)
    kpad[Q:, :] = jnp.where(own, kn[2], pltpu.roll(kn[2], HD, 1)).astype(BF16)
    vpad[Q:, :] = jnp.where(own, vv, pltpu.roll(vv, HD, 1)).astype(BF16)
    return qn, kn, lo32, own, c4, s4, bd


def _swa_fwd(u, cos4, sin4, qw2, qw2p, kw2, kw2p, sink_e, sink_o, bd, name):
    def body(q_ref, k_ref, v_ref, c_ref, s_ref, qw_ref, qwp_ref, kw_ref, kwp_ref, ske_ref, sko_ref, bd_ref,
             o_ref, kpad, vpad, s_scr, p_scr):
        qn, _, _, _, _, _, _ = _swa_setup(q_ref, k_ref, v_ref, c_ref, s_ref, qw_ref, qwp_ref, kw_ref, kwp_ref,
                                          bd_ref, kpad, vpad)
        qr = qn[2]
        first = lax.broadcasted_iota(jnp.int32, (Q, 128), 1) < HD
        for n in range(NC):
            rows = slice(n * Q, (n + 1) * Q)
            kc = kpad[n * Q:(n + 2) * Q, :]
            s_scr[0, rows, :] = _dg(jnp.where(first, qr[rows], 0.0).astype(BF16), kc, 1, 1)
            s_scr[1, rows, :] = _dg(jnp.where(first, 0.0, qr[rows]).astype(BF16), kc, 1, 1)
        for h, sk_ref in ((0, ske_ref), (1, sko_ref)):
            p, _ = _swa_softmax(s_scr[h], sk_ref[0][:, 0:1])
            p_scr[h] = p.astype(BF16)
        for n in range(NC):
            rows = slice(n * Q, (n + 1) * Q)
            vc = vpad[n * Q:(n + 2) * Q, :]
            o_ref[rows, :] = jnp.where(first, _dg(p_scr[0, rows, :], vc, 1, 0),
                                       _dg(p_scr[1, rows, :], vc, 1, 0)).astype(BF16)

    return pl.pallas_call(
        body, name=name, grid=(NPAIR,), in_specs=_swa_in_specs(),
        out_specs=pl.BlockSpec((L, 128), lambda p: (0, p)),
        out_shape=jax.ShapeDtypeStruct((L, ATT_W), BF16),
        scratch_shapes=[pltpu.VMEM((L + Q, 128), BF16), pltpu.VMEM((L + Q, 128), BF16),
                        pltpu.VMEM((2, L, 2 * Q), F32), pltpu.VMEM((2, L, 2 * Q), BF16)],
        compiler_params=_cp(("arbitrary",)))(u, u, u, cos4, sin4, qw2, qw2p, kw2, kw2p, sink_e, sink_o, bd)


def _swa_bwd(u, dy, cos4, sin4, qw2, qw2p, kw2, kw2p, sink_e, sink_o, bd, name):
    def body(q_ref, k_ref, v_ref, c_ref, s_ref, qw_ref, qwp_ref, kw_ref, kwp_ref, ske_ref, sko_ref, bd_ref, do_ref,
             dq_ref, dk_ref, dv_ref, dqw_ref, dkw_ref, dse_ref, dso_ref,
             kpad, vpad, s_scr, dp_scr, ds_scr, pb_scr, dkr_acc, dv_acc, dqr_scr):
        pidx = pl.program_id(0)

        @pl.when(pidx == 0)
        def _():
            dkr_acc[...] = jnp.zeros_like(dkr_acc)
            dv_acc[...] = jnp.zeros_like(dv_acc)
            dqw_ref[...] = jnp.zeros_like(dqw_ref)

        qn, kn, lo32, own, c4, s4, bd = _swa_setup(q_ref, k_ref, v_ref, c_ref, s_ref, qw_ref, qwp_ref, kw_ref,
                                                   kwp_ref, bd_ref, kpad, vpad)
        qr = qn[2]
        lane_q = lax.broadcasted_iota(jnp.int32, (Q, 128), 1)
        first = lane_q < HD
        own_q = (lane_q >> 6) == pidx // 2

        def halves(t):
            return jnp.where(first, t, 0.0).astype(BF16), jnp.where(first, 0.0, t).astype(BF16)

        for n in range(NC):
            rows = slice(n * Q, (n + 1) * Q)
            kc = kpad[n * Q:(n + 2) * Q, :]
            vc = vpad[n * Q:(n + 2) * Q, :]
            qm = halves(qr[rows])
            dom = halves(do_ref[rows, :])
            for h in range(2):
                s_scr[h, rows, :] = _dg(qm[h], kc, 1, 1)
                dp_scr[h, rows, :] = _dg(dom[h], vc, 1, 1)
        for h, sk_ref, dsk_ref in ((0, ske_ref, dse_ref), (1, sko_ref, dso_ref)):
            p, ps = _swa_softmax(s_scr[h], sk_ref[0][:, 0:1])
            dp = dp_scr[h]
            delta = jnp.sum(p * dp, axis=-1, keepdims=True)
            dsk_ref[0] = jnp.broadcast_to(-jnp.sum(ps * delta, axis=0, keepdims=True), (1, 128))
            ds_scr[h] = (p * (dp - delta) * (1.0 / math.sqrt(HD))).astype(BF16)
            pb_scr[h] = p.astype(BF16)
        for n in range(NC):
            rows = slice(n * Q, (n + 1) * Q)
            kc = kpad[n * Q:(n + 2) * Q, :]
            dqr_scr[rows, :] = jnp.where(first, _dg(ds_scr[0, rows, :], kc, 1, 0), _dg(ds_scr[1, rows, :], kc, 1, 0))
        for m in range(NC):
            acc_k = jnp.zeros((Q, 128), F32)
            acc_v = jnp.zeros((Q, 128), F32)
            for n, cols in ((m, slice(Q, 2 * Q)), (m + 1, slice(0, Q))):
                if n >= NC:
                    continue
                rows = slice(n * Q, (n + 1) * Q)
                qm = halves(qr[rows])
                dom = halves(do_ref[rows, :])
                for h in range(2):
                    acc_k = acc_k + _dg(ds_scr[h, rows, cols], qm[h], 0, 0)
                    acc_v = acc_v + _dg(pb_scr[h, rows, cols], dom[h], 0, 0)
            rows = slice(m * Q, (m + 1) * Q)
            dkr_acc[rows, :] += jnp.where(own_q, acc_k + pltpu.roll(acc_k, HD, 1), 0.0)
            dv_acc[rows, :] += jnp.where(own_q, acc_v + pltpu.roll(acc_v, HD, 1), 0.0)
        dq, dqw = _swa_unprep(dqr_scr[...], qn[0], qn[1], qw_ref[...], qwp_ref[...], c4, s4, bd, lo32)
        dq_ref[...] = dq.astype(BF16)
        dqw_ref[...] += dqw

        @pl.when(pidx == NPAIR - 1)
        def _():
            dk, dkw = _swa_unprep(dkr_acc[...], kn[0], kn[1], kw_ref[...], kwp_ref[...], c4, s4, bd, lo32)
            dk_ref[...] = dk.astype(BF16)
            dkw_ref[...] = dkw
            dv_ref[...] = dv_acc[...].astype(BF16)

    full = pl.BlockSpec((L, 128), lambda p: (0, 0))
    wv = pl.BlockSpec((1, 128), lambda p: (0, 0))
    sk = pl.BlockSpec((1, 1, 128), lambda p: (p, 0, 0))
    vec = jax.ShapeDtypeStruct((1, 128), F32)
    skv = jax.ShapeDtypeStruct((NPAIR, 1, 128), F32)
    return pl.pallas_call(
        body, name=name, grid=(NPAIR,),
        in_specs=_swa_in_specs() + [pl.BlockSpec((L, 128), lambda p: (0, SSD_W // 128 + p))],
        out_specs=[pl.BlockSpec((L, 128), lambda p: (0, p)), full, full, wv, wv, sk, sk],
        out_shape=[jax.ShapeDtypeStruct((L, ATT_W), BF16), jax.ShapeDtypeStruct((L, 128), BF16),
                   jax.ShapeDtypeStruct((L, 128), BF16), vec, vec, skv, skv],
        scratch_shapes=[pltpu.VMEM((L + Q, 128), BF16), pltpu.VMEM((L + Q, 128), BF16),
                        pltpu.VMEM((2, L, 2 * Q), F32), pltpu.VMEM((2, L, 2 * Q), F32),
                        pltpu.VMEM((2, L, 2 * Q), BF16), pltpu.VMEM((2, L, 2 * Q), BF16),
                        pltpu.VMEM((L, 128), F32), pltpu.VMEM((L, 128), F32), pltpu.VMEM((L, 128), F32)],
        compiler_params=_cp(("arbitrary",)))(u, u, u, cos4, sin4, qw2, qw2p, kw2, kw2p, sink_e, sink_o, bd, dy)


def _ssd_consts():
    hh = jnp.arange(128)[:, None]
    e = (hh == (jnp.arange(SSD_W)[None, :] // HP)).astype(BF16)
    e2 = (hh == (jnp.arange(NH * 128)[None, :] // 128)).astype(BF16)
    et = e.T
    tril = (jnp.arange(Q)[:, None] >= jnp.arange(Q)[None, :]).astype(BF16)
    triu = tril.T
    eye = jnp.eye(128, dtype=BF16)
    return e, e2, et, tril, triu, eye


def _ssd_common(x_ref, ext_scr, cw_ref, cb_ref, dt_ref, dtb_ref, alog_ref, e_ref, e2_ref, tril_ref, triu_ref,
                arow_scr, acol_scr, eax_scr):
    conv = jnp.broadcast_to(cb_ref[...], (Q, XBC))
    for k in range(4):
        conv = conv + cw_ref[k:k + 1, :] * ext_scr[pl.ds(5 + k, Q), :]
    sg = _sigmoid(conv)
    xbc = conv * sg
    dtpre = dt_ref[...] + dtb_ref[...]
    dt = _softplus(dtpre)
    a = -jnp.exp(alog_ref[...])
    adt = dt * a
    acol = _xdot_r(tril_ref[...], adt)
    acol_scr[...] = acol
    arow_scr[...] = _xdot_l(adt, triu_ref[...], 0, 0)
    alast = acol_scr[Q - 1:Q, :]
    ea = jnp.exp(acol)
    decs = jnp.exp(alast - acol)
    e = e_ref[...]
    dt_x = _xdot_l(dt, e)
    eax_scr[...] = _xdot_l(ea, e)
    decs_x = _xdot_l(decs, e)
    acx2 = _xdot_l(acol, e2_ref[...])
    return conv, sg, xbc, dtpre, dt, a, adt, acol, alast, ea, decs, dt_x, decs_x, acx2


def _ssd_fwd(u, cw, cb, dtb, alog, dxp, nw, consts, name):
    e, e2, et, tril, triu, eye = consts

    def body(z0_ref, z1_ref, x_ref, dt_ref, cw_ref, cb_ref, dtb_ref, alog_ref, dx_ref, nw_ref, e_ref, e2_ref,
             tril_ref, triu_ref, ya_ref, ypre_ref, st_ref, s_scr, ext_scr, arow_scr, acol_scr, eax_scr):
        c = pl.program_id(0)

        @pl.when(c == 0)
        def _():
            s_scr[...] = jnp.zeros_like(s_scr)
            ext_scr[0:8, :] = jnp.zeros((8, XBC), F32)
        ext_scr[8:8 + Q, :] = x_ref[...]
        (conv, sg, xbc, dtpre, dt, a, adt, acol, alast, ea, decs, dt_x, decs_x, acx2) = _ssd_common(
            x_ref, ext_scr, cw_ref, cb_ref, dt_ref, dtb_ref, alog_ref, e_ref, e2_ref, tril_ref, triu_ref,
            arow_scr, acol_scr, eax_scr)
        ext_scr[0:8, :] = ext_scr[Q:Q + 8, :]
        xs = xbc[:, :SSD_W]
        xdt = xs * dt_x
        lane = lax.broadcasted_iota(jnp.int32, (Q, 128), 1)
        causal = lax.broadcasted_iota(jnp.int32, (Q, Q), 0) >= lax.broadcasted_iota(jnp.int32, (Q, Q), 1)
        for g in range(2):
            bg = xbc[:, SSD_W + g * NS:SSD_W + (g + 1) * NS].astype(BF16)
            cg = xbc[:, SSD_W + 2 * NS + g * NS:SSD_W + 2 * NS + (g + 1) * NS].astype(BF16)
            cbm = _dg(cg, bg, 1, 1)
            sgv = s_scr[g]
            st_ref[0, g] = sgv
            gc = slice(g * 512, (g + 1) * 512)
            yoff = _dg(cg, sgv.astype(BF16), 1, 0) * eax_scr[:, gc]
            for pr in range(4):
                h0 = g * 8 + 2 * pr
                h1 = h0 + 1
                c0 = g * 512 + pr * 128
                xp = xdt[:, c0:c0 + 128].astype(BF16)
                w0 = (cbm * jnp.exp(jnp.where(causal, acx2[:, h0 * 128:(h0 + 1) * 128] - arow_scr[h0:h0 + 1, :],
                                              NEG))).astype(BF16)
                w1 = (cbm * jnp.exp(jnp.where(causal, acx2[:, h1 * 128:(h1 + 1) * 128] - arow_scr[h1:h1 + 1, :],
                                              NEG))).astype(BF16)
                yd = jnp.where(lane < HP, _dg(w0, xp, 1, 0), _dg(w1, xp, 1, 0))
                ypre_ref[:, c0:c0 + 128] = (yd + yoff[:, pr * 128:(pr + 1) * 128]
                                            + xs[:, c0:c0 + 128] * dx_ref[:, c0:c0 + 128])
            contrib = _dg(bg, (xdt[:, gc] * decs_x[:, gc]).astype(BF16), 0, 0)
            s_scr[g] = sgv * eax_scr[Q - 1:Q, gc] + contrib
        for g, zr in enumerate((z0_ref, z1_ref)):
            gc = slice(g * 512, (g + 1) * 512)
            zz = zr[...]
            ggg = ypre_ref[:, gc] * (zz * _sigmoid(zz))
            rstd = lax.rsqrt(jnp.mean(ggg * ggg, axis=-1, keepdims=True) + RMS_EPS)
            ya_ref[:, gc] = (ggg * rstd * nw_ref[:, gc]).astype(BF16)

    def row(w, blk=0):
        return pl.BlockSpec((Q, w), lambda c: (c, blk))

    def full(shape):
        return pl.BlockSpec(shape, lambda c: (0,) * len(shape))

    return pl.pallas_call(
        body, name=name, grid=(NC,),
        in_specs=[row(512, U_Z // 512), row(512, U_Z // 512 + 1), row(XBC, U_X // XBC), row(128, U_DT // 128),
                  full((4, XBC)), full((1, XBC)), full((1, 128)), full((1, 128)),
                  full((1, SSD_W)), full((1, SSD_W)), full((128, SSD_W)), full((128, NH * 128)), full((Q, Q)),
                  full((Q, Q))],
        out_specs=[row(SSD_W), row(SSD_W), pl.BlockSpec((1, 2, NS, 512), lambda c: (c, 0, 0, 0))],
        out_shape=[jax.ShapeDtypeStruct((L, SSD_W), BF16), jax.ShapeDtypeStruct((L, SSD_W), F32),
                   jax.ShapeDtypeStruct((NC, 2, NS, 512), F32)],
        scratch_shapes=[pltpu.VMEM((2, NS, 512), F32), pltpu.VMEM((Q + 8, XBC), F32), pltpu.VMEM((128, Q), F32),
                        pltpu.VMEM((Q, 128), F32), pltpu.VMEM((Q, SSD_W), F32)],
        compiler_params=_cp(("arbitrary",)))(u, u, u, u, cw, cb, dtb, alog, dxp, nw, e, e2, tril, triu)


def _ssd_bwd(u, ypre, st, dy, cw, cb, dtb, alog, dxp, nw, consts, name):
    e, e2, et, tril, triu, eye = consts

    def body(z0_ref, z1_ref, x_ref, xp_ref, dt_ref, ypre_ref, st_ref, dya_ref, cw_ref, cb_ref, dtb_ref, alog_ref, dx_ref,
             nw_ref, e_ref, e2_ref, et_ref, tril_ref, triu_ref, eye_ref,
             dz_ref, dxr_ref, ddtr_ref, dcw_ref, dcb_ref, ddtb_ref, dalog_ref, dd_ref, dnw_ref,
             g_scr, ext_scr, ext2_scr, arow_scr, acol_scr, eax_scr, darow_scr, dxdt_scr, t1_scr, t2_scr, dgg_scr):
        i = pl.program_id(0)

        @pl.when(i == 0)
        def _():
            g_scr[...] = jnp.zeros_like(g_scr)
            ext2_scr[Q:Q + 8, :] = jnp.zeros((8, XBC), F32)
            for r in (dcw_ref, dcb_ref, ddtb_ref, dalog_ref, dd_ref, dnw_ref):
                r[...] = jnp.zeros_like(r)
        not_first = jnp.where(i < NC - 1, 1.0, 0.0)
        ext_scr[0:8, :] = xp_ref[Q - 8:Q, :] * not_first
        ext_scr[8:8 + Q, :] = x_ref[...]
        (conv, sg, xbc, dtpre, dt, a, adt, acol, alast, ea, decs, dt_x, decs_x, acx2) = _ssd_common(
            x_ref, ext_scr, cw_ref, cb_ref, dt_ref, dtb_ref, alog_ref, e_ref, e2_ref, tril_ref, triu_ref,
            arow_scr, acol_scr, eax_scr)
        et_m = et_ref[...]
        xs = xbc[:, :SSD_W]
        xdt = xs * dt_x
        y = ypre_ref[...]
        zz = jnp.concatenate([z0_ref[...], z1_ref[...]], axis=1)
        sz = _sigmoid(zz)
        silu_z = zz * sz
        gg = y * silu_z
        dya = dya_ref[...]
        for g in range(2):
            gc = slice(g * 512, (g + 1) * 512)
            ggg = gg[:, gc]
            rstd = lax.rsqrt(jnp.mean(ggg * ggg, axis=-1, keepdims=True) + RMS_EPS)
            n = ggg * rstd
            dyag = dya[:, gc]
            dnw_ref[:, gc] += jnp.sum(dyag * n, axis=0, keepdims=True)
            dn = dyag * nw_ref[:, gc]
            dgg_scr[:, gc] = rstd * (dn - n * jnp.mean(dn * n, axis=-1, keepdims=True))
        dgg = dgg_scr[...]
        dy = dgg * silu_z
        dz_ref[...] = (dgg * y * (sz * (1.0 + zz * (1.0 - sz)))).astype(BF16)
        dd_ref[...] += _rowdot(jnp.sum(dy * xs, axis=0, keepdims=True), et_m)
        dxs = dy * dx_ref[...]
        dys = dy * eax_scr[...]
        lane = lax.broadcasted_iota(jnp.int32, (Q, 128), 1)
        causal = lax.broadcasted_iota(jnp.int32, (Q, Q), 0) >= lax.broadcasted_iota(jnp.int32, (Q, Q), 1)
        darow_scr[...] = jnp.zeros_like(darow_scr)
        dacol = jnp.zeros((Q, 128), F32)
        dcdx = []
        dbs = []
        dcs = []
        for g in range(2):
            gc = slice(g * 512, (g + 1) * 512)
            bg = xbc[:, SSD_W + g * NS:SSD_W + (g + 1) * NS].astype(BF16)
            cg = xbc[:, SSD_W + 2 * NS + g * NS:SSD_W + 2 * NS + (g + 1) * NS].astype(BF16)
            cbm = _dg(cg, bg, 1, 1)
            sgv = st_ref[0, g]
            sgb = sgv.astype(BF16)
            gv = g_scr[g]
            gvb = gv.astype(BF16)
            yoff = _dg(cg, sgb, 1, 0) * eax_scr[:, gc]
            dysg = dys[:, gc].astype(BF16)
            dcg = _dg(dysg, sgb, 1, 1)
            ds_off = _dg(cg, dysg, 0, 0)
            t1_scr[:, gc] = dy[:, gc] * yoff
            xdec = xdt[:, gc] * decs_x[:, gc]
            dxd = _dg(bg, gvb, 1, 0)
            dbg = _dg(xdec.astype(BF16), gvb, 1, 1)
            dxdt_g = dxd * decs_x[:, gc]
            t2_scr[:, gc] = dxd * xdt[:, gc]
            cdx = eax_scr[Q - 1:Q, gc]
            dcdx.append(jnp.sum(gv * sgv, axis=0, keepdims=True))
            g_scr[g] = gv * cdx + ds_off
            dcb_acc = jnp.zeros((Q, Q), F32)
            for pr in range(4):
                c0 = g * 512 + pr * 128
                xp = xdt[:, c0:c0 + 128].astype(BF16)
                dyp = dy[:, c0:c0 + 128]
                dypb = dyp.astype(BF16)
                halves = []
                for hh, keep in ((g * 8 + 2 * pr, lane < HP), (g * 8 + 2 * pr + 1, lane >= HP)):
                    lam = jnp.exp(jnp.where(causal, acx2[:, hh * 128:(hh + 1) * 128] - arow_scr[hh:hh + 1, :], NEG))
                    w = cbm * lam
                    dw = _dg(jnp.where(keep, dyp, 0.0).astype(BF16), xp, 1, 1)
                    dcb_acc = dcb_acc + dw * lam
                    t = dw * w
                    dacol = dacol + jnp.sum(t, axis=-1, keepdims=True) * (lane == hh).astype(F32)
                    darow_scr[hh:hh + 1, :] -= jnp.sum(t, axis=0, keepdims=True)
                    halves.append(_dg(w.astype(BF16), dypb, 0, 0))
                dxdt_scr[:, c0:c0 + 128] = (jnp.where(lane < HP, halves[0], halves[1])
                                            + dxdt_g[:, pr * 128:(pr + 1) * 128])
            dcbb = dcb_acc.astype(BF16)
            dcs.append(dcg + _dg(dcbb, bg, 1, 0))
            dbs.append(dbg + _dg(dcbb, cg, 0, 0))
        dacol = dacol + _xdot_l(t1_scr[...], et_m)
        ddecs = _xdot_l(t2_scr[...], et_m) * decs
        dacol = dacol - ddecs
        dalast = jnp.sum(ddecs, axis=0, keepdims=True)
        dcd = _rowdot(jnp.concatenate(dcdx, axis=1), et_m)
        dalast = dalast + dcd * jnp.exp(alast)
        dacol = dacol + _xdot_l(darow_scr[...], eye_ref[...], 0, 0)
        rowi = lax.broadcasted_iota(jnp.int32, (Q, 128), 0)
        dacol = dacol + jnp.where(rowi == Q - 1, dalast, 0.0)
        dadt = _xdot_r(triu_ref[...], dacol)
        dxdt = dxdt_scr[...]
        ddt = dadt * a + _xdot_l(dxdt * xs, et_m)
        dalog_ref[...] += jnp.sum(dadt * dt, axis=0, keepdims=True) * a
        dxs = dxs + dxdt * dt_x
        ddtr = ddt * _sigmoid(dtpre)
        ddtb_ref[...] += jnp.sum(ddtr, axis=0, keepdims=True)
        ddtr_ref[...] = ddtr.astype(BF16)
        dsilu = sg * (1.0 + conv * (1.0 - sg))
        ext2_scr[0:Q, 0:SSD_W] = dxs * dsilu[:, :SSD_W]
        for g in range(2):
            o1 = SSD_W + g * NS
            o2 = SSD_W + 2 * NS + g * NS
            ext2_scr[0:Q, o1:o1 + NS] = dbs[g] * dsilu[:, o1:o1 + NS]
            ext2_scr[0:Q, o2:o2 + NS] = dcs[g] * dsilu[:, o2:o2 + NS]
        dconv = ext2_scr[0:Q, :]
        dcb_ref[...] += jnp.sum(dconv, axis=0, keepdims=True)
        dxr = jnp.zeros((Q, XBC), F32)
        for k in range(4):
            dcw_ref[k:k + 1, :] += jnp.sum(dconv * ext_scr[pl.ds(5 + k, Q), :], axis=0, keepdims=True)
            dxr = dxr + cw_ref[k:k + 1, :] * ext2_scr[pl.ds(3 - k, Q), :]
        dxr_ref[...] = dxr.astype(BF16)
        ext2_scr[Q:Q + 8, :] = ext2_scr[0:8, :]

    def row(w, blk=0):
        return pl.BlockSpec((Q, w), lambda i: (NC - 1 - i, blk))

    def full(shape):
        return pl.BlockSpec(shape, lambda i: (0,) * len(shape))

    prev = pl.BlockSpec((Q, XBC), lambda i: (jnp.maximum(NC - 2 - i, 0), U_X // XBC))
    return pl.pallas_call(
        body, name=name, grid=(NC,),
        in_specs=[row(512, U_Z // 512), row(512, U_Z // 512 + 1), row(XBC, U_X // XBC), prev, row(128, U_DT // 128),
                  row(SSD_W),
                  pl.BlockSpec((1, 2, NS, 512), lambda i: (NC - 1 - i, 0, 0, 0)), row(SSD_W),
                  full((4, XBC)), full((1, XBC)), full((1, 128)), full((1, 128)), full((1, SSD_W)),
                  full((1, SSD_W)), full((128, SSD_W)), full((128, NH * 128)), full((SSD_W, 128)), full((Q, Q)),
                  full((Q, Q)), full((128, 128))],
        out_specs=[row(SSD_W), row(XBC), row(128), full((8, XBC)), full((1, XBC)), full((1, 128)), full((1, 128)),
                   full((1, 128)), full((1, SSD_W))],
        out_shape=[jax.ShapeDtypeStruct((L, SSD_W), BF16), jax.ShapeDtypeStruct((L, XBC), BF16),
                   jax.ShapeDtypeStruct((L, 128), BF16), jax.ShapeDtypeStruct((8, XBC), F32),
                   jax.ShapeDtypeStruct((1, XBC), F32), jax.ShapeDtypeStruct((1, 128), F32),
                   jax.ShapeDtypeStruct((1, 128), F32), jax.ShapeDtypeStruct((1, 128), F32),
                   jax.ShapeDtypeStruct((1, SSD_W), F32)],
        scratch_shapes=[pltpu.VMEM((2, NS, 512), F32), pltpu.VMEM((Q + 8, XBC), F32), pltpu.VMEM((Q + 8, XBC), F32),
                        pltpu.VMEM((128, Q), F32), pltpu.VMEM((Q, 128), F32), pltpu.VMEM((Q, SSD_W), F32),
                        pltpu.VMEM((128, Q), F32), pltpu.VMEM((Q, SSD_W), F32), pltpu.VMEM((Q, SSD_W), F32),
                        pltpu.VMEM((Q, SSD_W), F32), pltpu.VMEM((Q, SSD_W), F32)],
        compiler_params=_cp(("arbitrary",)))(u, u, u, u, u, ypre, st, dy, cw, cb, dtb, alog, dxp, nw,
                                             e, e2, et, tril, triu, eye)


def _my_pos():
    return lax.axis_index("x"), lax.axis_index("y"), lax.axis_index("c")


CHIP_REL = ((1, 0), (0, 1), (1, 1))
CHIP_XOR = (2, 1, 3)
BIG = ("w_in", "w_out", "w_mlp_up", "w_mlp_down")
NW = len(BIG)
AT = 256


def _chips(x, y):
    return [(1 - x if dx else x, 1 - y if dy else y) for dx, dy in CHIP_REL]


HBM_SPEC = pl.BlockSpec(memory_space=pltpu.HBM)
SEM_SPEC = pl.BlockSpec(memory_space=pltpu.SEMAPHORE)
EFFECT = pltpu.SideEffectType.DATAFLOW_SIDE_EFFECTING


def _hbm(t):
    return pltpu.with_memory_space_constraint(t, pltpu.HBM)


def _split_start(srcs, lands, after, copies, name):
    n = len(srcs)

    def body(*refs):
        src_refs, land_refs = refs[:n], refs[n:2 * n]
        send_sems, recv_sems = refs[2 * n + 1], refs[2 * n + 2]
        token = refs[-1]
        for w, k, src, dst, dev in copies(src_refs, land_refs):
            pltpu.make_async_remote_copy(src_ref=src, dst_ref=dst, send_sem=send_sems.at[3 * w + k],
                                         recv_sem=recv_sems.at[3 * w + k], device_id=dev, device_id_type=MESH).start()
        token[...] = jnp.zeros_like(token)

    outs = pl.pallas_call(
        body, name=name,
        out_shape=(pltpu.SemaphoreType.DMA((3 * n,)), pltpu.SemaphoreType.DMA((3 * n,)),
                   *[pltpu.HBM(t.shape, t.dtype) for t in srcs], *[pltpu.HBM(t.shape, t.dtype) for t in lands],
                   jax.ShapeDtypeStruct((8, 128), F32)),
        in_specs=[HBM_SPEC] * (2 * n) + [ANY],
        out_specs=(SEM_SPEC, SEM_SPEC, *([HBM_SPEC] * (2 * n)), VMEM_SPEC),
        input_output_aliases={i: 2 + i for i in range(2 * n)},
        compiler_params=pltpu.CompilerParams(has_side_effects=EFFECT))(
            *[_hbm(t) for t in srcs], *[_hbm(t) for t in lands], after)
    return outs[0], outs[1], list(outs[2:2 + n]), list(outs[2 + n:2 + 2 * n]), outs[-1]


def _split_wait(send_sems, recv_sems, srcs, lands, after, copies, name):
    n = len(srcs)

    def body(*refs):
        src_refs, land_refs = refs[:n], refs[n:2 * n]
        ssem, rsem = refs[2 * n], refs[2 * n + 1]
        for w, k, src, dst, dev in copies(src_refs, land_refs):
            cp = pltpu.make_async_remote_copy(src_ref=src, dst_ref=dst, send_sem=ssem.at[3 * w + k],
                                              recv_sem=rsem.at[3 * w + k], device_id=dev, device_id_type=MESH)
            cp.wait_send()
            cp.wait_recv()

    outs = pl.pallas_call(
        body, name=name,
        out_shape=tuple([pltpu.HBM(t.shape, t.dtype) for t in srcs] + [pltpu.HBM(t.shape, t.dtype) for t in lands]),
        in_specs=[HBM_SPEC] * (2 * n) + [SEM_SPEC, SEM_SPEC, ANY],
        out_specs=tuple([HBM_SPEC] * (2 * n)),
        input_output_aliases={i: i for i in range(2 * n)},
        compiler_params=pltpu.CompilerParams(has_side_effects=EFFECT))(*srcs, *lands, send_sems, recv_sems, after)
    return list(outs[:n]), list(outs[n:])


def _ag_copies(arrival):
    def copies(src_refs, land_refs):
        x, y, c = _my_pos()
        s = 2 * x + y
        chips = _chips(x, y)
        for w in range(len(src_refs)):
            hr = src_refs[w].shape[0] // 2
            mine = pl.ds(c * hr, hr)
            for k in range(3):
                slot = s ^ CHIP_XOR[k] if arrival else s
                yield w, k, src_refs[w].at[mine], land_refs[w].at[slot, mine], (*chips[k], c)
    return copies


def _ag_forward(lands, name):
    n = len(lands)

    def body(*refs):
        outs = refs[n:2 * n]
        send_sems, recv_sems = refs[2 * n:]
        x, y, c = _my_pos()
        s = 2 * x + y
        sib = (x, y, 1 - c)
        sends = []
        for w in range(n):
            hr = outs[w].shape[1] // 2
            for k in range(3):
                blk = outs[w].at[s ^ CHIP_XOR[k], pl.ds(c * hr, hr)]
                fw = pltpu.make_async_remote_copy(
                    src_ref=blk, dst_ref=blk, send_sem=send_sems.at[w, k], recv_sem=recv_sems.at[w, k],
                    device_id=sib, device_id_type=MESH)
                fw.start()
                sends.append(fw)
        for w in range(n):
            hr = outs[w].shape[1] // 2
            for k in range(3):
                blk = outs[w].at[s ^ CHIP_XOR[k], pl.ds((1 - c) * hr, hr)]
                pltpu.make_async_remote_copy(
                    src_ref=blk, dst_ref=blk, send_sem=send_sems.at[w, k], recv_sem=recv_sems.at[w, k],
                    device_id=sib, device_id_type=MESH).wait_recv()
        for cp in sends:
            cp.wait_send()

    return pl.pallas_call(
        body, name=name, in_specs=[ANY] * n, out_specs=[ANY] * n,
        out_shape=[jax.ShapeDtypeStruct(t.shape, t.dtype) for t in lands],
        input_output_aliases={w: w for w in range(n)},
        scratch_shapes=[pltpu.SemaphoreType.DMA((n, 3)), pltpu.SemaphoreType.DMA((n, 3))])(*lands)


def _rs_copies(src_refs, land_refs):
    x, y, c = _my_pos()
    s = 2 * x + y
    chips = _chips(x, y)
    for w in range(len(src_refs)):
        for k in range(3):
            yield w, k, src_refs[w].at[s ^ CHIP_XOR[k]], land_refs[w].at[k], (*chips[k], c)


def _place_own(shard, gathered, sidx, name):
    r, cc = shard.shape

    def body(s_ref, a_ref, g_ref, o_ref):
        o_ref[0] = a_ref[...]

    return pl.pallas_call(
        body, name=name,
        grid_spec=pltpu.PrefetchScalarGridSpec(
            num_scalar_prefetch=1, grid=(r // AT,),
            in_specs=[pl.BlockSpec((AT, cc), lambda i, s_ref: (i, 0)), ANY],
            out_specs=pl.BlockSpec((1, AT, cc), lambda i, s_ref: (s_ref[0], i, 0))),
        out_shape=jax.ShapeDtypeStruct(gathered.shape, gathered.dtype),
        input_output_aliases={2: 0}, compiler_params=_cp(("parallel",)))(sidx, shard, gathered)


def _rs_pair(dwb, name):
    n = len(dwb)

    def body(*refs):
        ins, outs = refs[:n], refs[n:2 * n]
        send_sems, recv_sems = refs[2 * n:]
        x, y, c = _my_pos()
        cps = []
        for w in range(n):
            hr = ins[w].shape[1] // 2
            cp = pltpu.make_async_remote_copy(
                src_ref=ins[w].at[:, pl.ds((1 - c) * hr, hr)], dst_ref=outs[w], send_sem=send_sems.at[w],
                recv_sem=recv_sems.at[w], device_id=(x, y, 1 - c), device_id_type=MESH)
            cp.start()
            cps.append(cp)
        for cp in cps:
            cp.wait()

    return pl.pallas_call(
        body, name=name, in_specs=[ANY] * n, out_specs=[ANY] * n,
        out_shape=[jax.ShapeDtypeStruct((4, t.shape[1] // 2, t.shape[2]), t.dtype) for t in dwb],
        scratch_shapes=[pltpu.SemaphoreType.DMA((n,)), pltpu.SemaphoreType.DMA((n,))])(*dwb)


def _rs_sib(q, name):
    n = len(q)

    def body(*refs):
        outs = refs[n:2 * n]
        send_sems, recv_sems = refs[2 * n:]
        x, y, c = _my_pos()
        cps = []
        for w in range(n):
            hr = outs[w].shape[0] // 2
            mine = pl.ds(c * hr, hr)
            cp = pltpu.make_async_remote_copy(
                src_ref=outs[w].at[mine], dst_ref=outs[w].at[mine], send_sem=send_sems.at[w],
                recv_sem=recv_sems.at[w], device_id=(x, y, 1 - c), device_id_type=MESH)
            cp.start()
            cps.append(cp)
        for w in range(n):
            hr = outs[w].shape[0] // 2
            other = outs[w].at[pl.ds((1 - c) * hr, hr)]
            pltpu.make_async_remote_copy(
                src_ref=other, dst_ref=other, send_sem=send_sems.at[w], recv_sem=recv_sems.at[w],
                device_id=(x, y, 1 - c), device_id_type=MESH).wait_recv()
        for cp in cps:
            cp.wait_send()

    return pl.pallas_call(
        body, name=name, in_specs=[ANY] * n, out_specs=[ANY] * n,
        out_shape=[jax.ShapeDtypeStruct(t.shape, t.dtype) for t in q],
        input_output_aliases={w: w for w in range(n)},
        scratch_shapes=[pltpu.SemaphoreType.DMA((n,)), pltpu.SemaphoreType.DMA((n,))])(*q)


def _rs_rows(hr):
    return 2 * AT if hr % (2 * AT) == 0 else AT


def _rs_add2(dw, got, scidx, name):
    _, r, cc = dw.shape
    hr = r // 2
    at = _rs_rows(hr)
    nb = hr // at

    def body(s_ref, a_ref, b_ref, o_ref, ob_ref):
        acc = a_ref[...] + b_ref[...].astype(F32)
        ob_ref[...] = acc.astype(BF16)

        @pl.when(pl.program_id(1) == s_ref[0])
        def _():
            o_ref[...] = acc[0]

    blk = pl.BlockSpec((1, at, cc), lambda i, sh, s_ref: (sh, i, 0))
    return pl.pallas_call(
        body, name=name,
        grid_spec=pltpu.PrefetchScalarGridSpec(
            num_scalar_prefetch=1, grid=(nb, 4),
            in_specs=[pl.BlockSpec((1, at, cc), lambda i, sh, s_ref: (sh, s_ref[1] * nb + i, 0)), blk],
            out_specs=[pl.BlockSpec((at, cc), lambda i, sh, s_ref: (i, 0)), blk]),
        out_shape=[jax.ShapeDtypeStruct((hr, cc), F32), jax.ShapeDtypeStruct((4, hr, cc), BF16)],
        compiler_params=_cp(("parallel", "arbitrary")))(scidx, dw, got)


def _rs_add4(p, got, scidx, name):
    hr, cc = p.shape
    at = _rs_rows(hr)
    nb = hr // at

    def body(s_ref, p_ref, g0_ref, g1_ref, g2_ref, o_ref):
        acc = p_ref[...] + g0_ref[0].astype(F32)
        acc = acc + g1_ref[0].astype(F32)
        o_ref[...] = acc + g2_ref[0].astype(F32)

    def gk(k):
        return pl.BlockSpec((1, at, cc), lambda i, s_ref: (k, i, 0))

    return pl.pallas_call(
        body, name=name,
        grid_spec=pltpu.PrefetchScalarGridSpec(
            num_scalar_prefetch=1, grid=(nb,),
            in_specs=[pl.BlockSpec((at, cc), lambda i, s_ref: (i, 0)), gk(0), gk(1), gk(2)],
            out_specs=pl.BlockSpec((at, cc), lambda i, s_ref: (s_ref[1] * nb + i, 0))),
        out_shape=jax.ShapeDtypeStruct((2 * hr, cc), F32),
        compiler_params=_cp(("parallel",)))(scidx, p, got, got, got)


def _rs_begin(dws, dwbs, after, tag=""):
    x, y, c = _my_pos()
    scidx = jnp.stack([2 * x + y, c]).astype(jnp.int32)
    got = _rs_pair(dwbs, "rs_pair" + tag)
    pairs = [_rs_add2(dws[w], got[w], scidx, "rs_add2") for w in range(len(dws))]
    pb = [p[1] for p in pairs]
    lands = [lax.empty((3,) + t.shape[1:], BF16) for t in pb]
    ssem, rsem, pb, lands, token = _split_start(pb, lands, after, _rs_copies, "rs_chip_start" + tag)
    return ([p[0] for p in pairs], ssem, rsem, pb, lands), token


def _rs_end(state, after, tag=""):
    x, y, c = _my_pos()
    scidx = jnp.stack([2 * x + y, c]).astype(jnp.int32)
    p, ssem, rsem, pb, lands = state
    _, recv = _split_wait(ssem, rsem, pb, lands, after, _rs_copies, "rs_chip_wait" + tag)
    q = [_rs_add4(p[w], recv[w], scidx, "rs_add4") for w in range(len(p))]
    return _rs_sib(q, "rs_sib" + tag)


def _allreduce_small(buf, name):
    rows = buf.shape[0]

    def body(src_ref, out_ref, gat_ref, send_sems, recv_sems):
        x, y, c = _my_pos()
        me = 4 * x + 2 * y + c
        gat_ref[me] = src_ref[...]
        cps = []
        for r in range(1, N_DEV):
            tx = 1 - x if (r >> 2) & 1 else x
            ty = 1 - y if (r >> 1) & 1 else y
            tc = 1 - c if r & 1 else c
            cps.append(pltpu.make_async_remote_copy(
                src_ref=src_ref, dst_ref=gat_ref.at[me], send_sem=send_sems.at[r - 1], recv_sem=recv_sems.at[r - 1],
                device_id=(tx, ty, tc), device_id_type=MESH))
        for cp in cps:
            cp.start()
        for cp in cps:
            cp.wait()
        acc = gat_ref[0]
        for k in range(1, N_DEV):
            acc = acc + gat_ref[k]
        out_ref[...] = acc

    return pl.pallas_call(
        body, name=name, in_specs=[VMEM_SPEC], out_specs=VMEM_SPEC, out_shape=jax.ShapeDtypeStruct((rows, 128), F32),
        scratch_shapes=[pltpu.VMEM((N_DEV, rows, 128), F32), pltpu.SemaphoreType.DMA((N_DEV - 1,)),
                        pltpu.SemaphoreType.DMA((N_DEV - 1,))],
        compiler_params=_cp())(buf)


SMALL = (("norm_mix_w", (D,)), ("ssd_conv_w", (4, XBC)), ("ssd_conv_b", (XBC,)), ("ssd_dt_bias", (NH,)),
         ("ssd_a_log", (NH,)), ("ssd_d", (NH,)), ("ssd_norm_w", (SSD_W,)), ("q_norm_w", (HD,)),
         ("k_norm_w", (HD,)), ("attn_sinks", (NQH,)), ("cm_dw_w", (CMK, CMC)), ("cm_dw_b", (CMC,)),
         ("cm_ln_w", (CMC,)), ("cm_ln_b", (CMC,)), ("norm_mlp_w", (D,)))
SHARDED_SMALL = ("ssd_conv_w", "cm_dw_w")


def _seg_len(shape):
    n = 1
    for d in shape:
        n *= d
    return -(-n // 128) * 128


def _pack_small(vals, names):
    parts = []
    for name, shape in SMALL:
        if name not in names:
            continue
        v = vals[name].reshape(DEPTH, -1)
        pad = _seg_len(shape) - v.shape[1]
        parts.append(jnp.pad(v, ((0, 0), (0, pad))))
    flat = jnp.concatenate(parts, axis=1)
    return flat.reshape(-1, 128)


def _unpack_small(buf, names):
    flat = buf.reshape(DEPTH, -1)
    out = {}
    off = 0
    for name, shape in SMALL:
        if name not in names:
            continue
        n = 1
        for d in shape:
            n *= d
        out[name] = flat[:, off:off + n].reshape((DEPTH,) + shape)
        off += _seg_len(shape)
    return out


SW = N_IN // 4
SWP = 1152
ORIG = (("z", 0, 1024), ("x", 1024, 2560), ("dt", 2560, 2576), ("q", 2576, 3088), ("k", 3088, 3216),
        ("v", 3216, 3344), ("a", 3344, 3856), ("g", 3856, 4368))


def _orig_cols(g_in, lo, hi):
    out = []
    for s in range(4):
        a, b = max(lo, s * SW), min(hi, (s + 1) * SW)
        if a < b:
            out.append(g_in[s][:, a - s * SW:b - s * SW])
    return out


def _shard_major(parts):
    cols = []
    for s in range(4):
        for name, g0, g1 in ORIG:
            a, b = max(g0, s * SW), min(g1, (s + 1) * SW)
            if a < b:
                cols.append(parts[name][:, a - g0:b - g0])
        cols.append(jnp.zeros((L, SWP - SW), BF16))
    return jnp.concatenate(cols, axis=1)


def _rope_tables():
    inv = 10000.0 ** (-jnp.arange(0, HD, 2, dtype=F32) / HD)
    ang = jnp.arange(L, dtype=F32)[:, None] * inv[None, :]
    return jnp.cos(ang), jnp.sin(ang)


def _swa_tables():
    cos, sin = _rope_tables()
    return jnp.tile(cos, (1, 4)), jnp.tile(jnp.concatenate([-sin, sin], axis=1), (1, 2))


def _swa_weights(w):
    return jnp.tile(w, 2)[None], jnp.tile(jnp.concatenate([w[HH:], w[:HH]]), 2)[None]


def _swa_sinks(s):
    s2 = s.reshape(NPAIR, 2)
    return (jnp.broadcast_to(s2[:, 0][:, None, None], (NPAIR, 1, 128)),
            jnp.broadcast_to(s2[:, 1][:, None, None], (NPAIR, 1, 128)))


def _swa_blockdiag():
    i = jnp.arange(128)
    return (i[:, None] // HD == i[None, :] // HD).astype(BF16)


def _pad128(v):
    return jnp.pad(v, (0, 128 - v.shape[0]))[None, :]


def kernel(x, norm_mix_w, w_in, ssd_conv_w, ssd_conv_b, ssd_dt_bias, ssd_a_log, ssd_d, ssd_norm_w, q_norm_w, k_norm_w, attn_sinks, cm_dw_w, cm_dw_b, cm_ln_w, cm_ln_b, w_out, norm_mlp_w, w_mlp_up, w_mlp_down, loss_target, m_norm_mix_w, m_w_in, m_ssd_conv_w, m_ssd_conv_b, m_ssd_dt_bias, m_ssd_a_log, m_ssd_d, m_ssd_norm_w, m_q_norm_w, m_k_norm_w, m_attn_sinks, m_cm_dw_w, m_cm_dw_b, m_cm_ln_w, m_cm_ln_b, m_w_out, m_norm_mlp_w, m_w_mlp_up, m_w_mlp_down, v_norm_mix_w, v_w_in, v_ssd_conv_w, v_ssd_conv_b, v_ssd_dt_bias, v_ssd_a_log, v_ssd_d, v_ssd_norm_w, v_q_norm_w, v_k_norm_w, v_attn_sinks, v_cm_dw_w, v_cm_dw_b, v_cm_ln_w, v_cm_ln_b, v_w_out, v_norm_mlp_w, v_w_mlp_up, v_w_mlp_down):
    px, py, pc = _my_pos()
    shard = 2 * px + py
    sidx = jnp.reshape(shard, (1,)).astype(jnp.int32)
    consts = _ssd_consts()
    cos4, sin4 = _swa_tables()
    bd = _swa_blockdiag()

    big_w = (w_in, w_out, w_mlp_up, w_mlp_down)

    def own_shard(w, l):
        t = big_w[w][l]
        if w == 0:
            t = jnp.pad(t, ((0, 0), (0, SWP - SW)))
        return t.astype(BF16)

    def gather_start(l, sel, after, tag=""):
        own = [own_shard(w, l) for w in sel]
        lands = [lax.empty((4,) + t.shape, BF16) for t in own]
        return _split_start(own, lands, after, _ag_copies(False), "ag_start" + tag)

    def gather_finish(in_flight, after, tag=""):
        ssem, rsem, own, lands, _ = in_flight
        own, lands = _split_wait(ssem, rsem, own, lands, after, _ag_copies(True), "ag_wait" + tag)
        lands = _ag_forward(lands, "ag_forward" + tag)
        return [_place_own(o, g, sidx, "ag_place") for o, g in zip(own, lands)]

    zero_tile = jnp.zeros((8, 128), F32)
    first_in = gather_start(0, [0], zero_tile, "_in0")
    zc = jnp.zeros((DEPTH, 4, XBC), F32)
    zc = lax.dynamic_update_slice_in_dim(zc, ssd_conv_w, shard * (XBC // 4), axis=2)
    zd = jnp.zeros((DEPTH, CMK, CMC), F32)
    zd = lax.dynamic_update_slice_in_dim(zd, cm_dw_w, shard * (CMC // 4), axis=2)
    half = jnp.where(pc == 0, 1.0, 0.0).astype(F32)
    gw_packed = _allreduce_small(_pack_small({"ssd_conv_w": zc * half, "cm_dw_w": zd * half}, SHARDED_SMALL)
                                 + first_in[4][0:1, 0:1], "ag_small")
    first_rest = gather_start(0, [1, 2, 3], gw_packed, "_rest0")
    gw = _unpack_small(gw_packed, SHARDED_SMALL)
    conv_w_full, dw_w_full = gw["ssd_conv_w"], gw["cm_dw_w"]

    xcur = x[0]
    saved = []
    in_flight = None
    for l in range(DEPTH):
        if l == 0:
            (g_in,) = gather_finish(first_in, first_rest[4], "_in0")
            nxt_after = g_in
        else:
            g_in, g_out, g_up, g_dn = gather_finish(in_flight, xcur)
            nxt_after = g_dn
        nmw = norm_mix_w[l][None]
        if l + 1 < DEPTH:
            in_flight = gather_start(l + 1, [0, 1, 2, 3], nxt_after)
            nmw = nmw + in_flight[4][0:1, 0:1]
        grp = dict((n, (a, b)) for n, a, b in ORIG)
        w_perm = jnp.concatenate(
            _orig_cols(g_in, *grp["x"]) + _orig_cols(g_in, *grp["z"]) + _orig_cols(g_in, grp["a"][0], grp["g"][1])
            + _orig_cols(g_in, grp["q"][0], grp["v"][1]) + _orig_cols(g_in, *grp["dt"])
            + [jnp.zeros((D, 128 - NH), BF16)], axis=1)
        h = _rms_fwd(xcur, nmw, "rms_mix_fwd")
        u = _mm(h, w_perm, "nn", "in_proj", tn=640)
        alog = _pad128(ssd_a_log[l])
        dtb = _pad128(ssd_dt_bias[l])
        dxp = jnp.repeat(ssd_d[l], HP)[None, :]
        ssd_p = (conv_w_full[l], ssd_conv_b[l][None], dtb, alog, dxp, ssd_norm_w[l][None])
        ya, ypre, st = _ssd_fwd(u, *ssd_p, consts, "ssd_fwd")
        swa_p = (cos4, sin4, *_swa_weights(q_norm_w[l]), *_swa_weights(k_norm_w[l]), *_swa_sinks(attn_sinks[l]), bd)
        yb = _swa_fwd(u, *swa_p, "attn_fwd")
        cconv = _conf_conv_fwd(u, dw_w_full[l], cm_dw_b[l][None], "conf_conv_fwd")
        conf_p = (cconv, cm_ln_w[l][None], cm_ln_b[l][None])
        yc = _conf_ln_fwd(*conf_p, "conf_ln_fwd")
        if l == 0:
            g_out, g_up, g_dn = gather_finish(first_rest, yc, "_rest0")
        g_out = g_out.reshape(2 * D, D)
        g_dn = g_dn.reshape(DFF, D)
        ycat = jnp.concatenate([ya, yb, yc], axis=1)
        x1 = _mm(ycat, g_out, "nn", "out_proj", add=xcur)
        hm = _rms_fwd(x1, norm_mlp_w[l][None], "rms_mlp_fwd")
        r_up = _mm_up(hm, g_up, "mlp_up")
        x2 = _mm(r_up, g_dn, "nn", "mlp_down", add=x1)
        saved.append(dict(x=xcur, h=h, u=u, ypre=ypre, st=st, swa_p=swa_p, conf_p=conf_p, ycat=ycat, x1=x1,
                          hm=hm, r_up=r_up, ssd_p=ssd_p, g_in=g_in, g_out=g_out, g_up=g_up, g_dn=g_dn))
        xcur = x2

    lsum, dx, dxb = _loss_bwd(xcur, loss_target[0], "loss")

    loc = locals()
    names = ["norm_mix_w", "w_in", "ssd_conv_w", "ssd_conv_b", "ssd_dt_bias", "ssd_a_log", "ssd_d", "ssd_norm_w",
             "q_norm_w", "k_norm_w", "attn_sinks", "cm_dw_w", "cm_dw_b", "cm_ln_w", "cm_ln_b", "w_out", "norm_mlp_w",
             "w_mlp_up", "w_mlp_down"]
    weights = {n: loc[n] for n in names}
    moms = {n: loc["m_" + n] for n in names}
    vars_ = {n: loc["v_" + n] for n in names}
    big_out = {n: None for n in BIG}
    win_grads = [None] * DEPTH

    def finish_layer(layer, which, shard_grads):
        for n, g in zip(which, shard_grads):
            if n == "w_in":
                win_grads[layer] = g
            else:
                big_out[n] = _adamw_layer(weights[n], g, moms[n], vars_[n], layer, big_out[n], "adamw_" + n)

    pending = None
    last_mlp = None
    gsm = {name: [] for name, _ in SMALL}
    for l in reversed(range(DEPTH)):
        sv = saved[l]
        da = _mm(dxb, sv["g_dn"], "nt", "mlp_down_dx", relu2_of=sv["r_up"])
        dwdn, dwdn_b = _mm_dw(sv["r_up"], dxb, "mlp_down_dw")
        dwup, dwup_b = _mm_dw(sv["hm"], da, "mlp_up_dw", col_shards=True)
        dhm = _mm_cs_nt(da, sv["g_up"], "mlp_up_dx")
        nlw = norm_mlp_w[l][None]
        if l == 0:
            last_mlp, token = _rs_begin([dwup, dwdn.reshape(4, D, D)], [dwup_b, dwdn_b.reshape(4, D, D)], zero_tile,
                                        "_mlp0")
            nlw = nlw + token[0:1, 0:1]
        dx1, dx1b, dnw = _rms_bwd(sv["x1"], nlw, dhm, dx, "rms_mlp_bwd")
        gsm["norm_mlp_w"].append(dnw[0])
        dy = _mm(dx1b, sv["g_out"], "nt", "out_proj_dx")
        dwout, dwout_b = _mm_dw(sv["ycat"], dx1b, "out_proj_dw")
        dcc, dwb, dlw, dlb = _conf_ln_bwd(*sv["conf_p"], dy, "conf_ln_bwd")
        da_c, dg_c, dww = _conf_conv_bwd(sv["u"], dcc, dw_w_full[l], "conf_conv_bwd")
        gsm["cm_dw_w"].append(dww[:CMK])
        gsm["cm_dw_b"].append(dwb[0])
        gsm["cm_ln_w"].append(dlw[0])
        gsm["cm_ln_b"].append(dlb[0])
        dq, dk, dv, dqw, dkw, dse, dso = _swa_bwd(sv["u"], dy, *sv["swa_p"], "attn_bwd")
        gsm["q_norm_w"].append(dqw[0, :HD] + dqw[0, HD:])
        gsm["k_norm_w"].append(dkw[0, :HD] + dkw[0, HD:])
        gsm["attn_sinks"].append(jnp.stack([dse[:, 0, 0], dso[:, 0, 0]], axis=1).reshape(NQH))
        (dz, dxr, ddtr, dcw, dcb, ddtb, dalog, ddd, dnsw) = _ssd_bwd(
            sv["u"], sv["ypre"], sv["st"], dy, *sv["ssd_p"], consts, "ssd_bwd")
        gsm["ssd_conv_w"].append(dcw[:4])
        gsm["ssd_conv_b"].append(dcb[0])
        gsm["ssd_dt_bias"].append(ddtb[0, :NH])
        gsm["ssd_a_log"].append(dalog[0, :NH])
        gsm["ssd_d"].append(ddd[0, :NH])
        gsm["ssd_norm_w"].append(dnsw[0])
        du = _shard_major(dict(z=dz, x=dxr, dt=ddtr[:, :NH], q=dq, k=dk, v=dv, a=da_c, g=dg_c))
        dwin, dwin_b = _mm_dw(sv["h"], du, "in_dw", col_shards=True, tn=SWP // 3)
        if l == 0:
            state, token = _rs_begin([dwin, dwout.reshape(4, D // 2, D)], [dwin_b, dwout_b.reshape(4, D // 2, D)],
                                     zero_tile, "_io0")
        else:
            state, token = _rs_begin(
                [dwin, dwout.reshape(4, D // 2, D), dwup, dwdn.reshape(4, D, D)],
                [dwin_b, dwout_b.reshape(4, D // 2, D), dwup_b, dwdn_b.reshape(4, D, D)], zero_tile)
        dh = _mm_cs_nt(du, sv["g_in"], "in_dx")
        dx, dxb, dnm = _rms_bwd(sv["x"], norm_mix_w[l][None] + token[0:1, 0:1], dh, dx1, "rms_mix_bwd")
        gsm["norm_mix_w"].append(dnm[0])
        if pending is not None:
            finish_layer(l + 1, BIG, _rs_end(pending, dx))
        pending = state

    finish_layer(0, BIG[2:], _rs_end(last_mlp, dx, "_mlp0"))
    gsm = {k: jnp.stack(v[::-1]) for k, v in gsm.items()}
    packed = _pack_small(gsm, [n for n, _ in SMALL])
    packed = jnp.concatenate([packed, lsum], axis=0)
    red = _allreduce_small(packed, "ar_small")
    finish_layer(0, BIG[:2], _rs_end(pending, red, "_io0"))
    loss = 0.5 * red[-8, 0] / D
    gsm = _unpack_small(red[:-8], [n for n, _ in SMALL])
    gsm["ssd_conv_w"] = lax.dynamic_slice_in_dim(gsm["ssd_conv_w"], shard * (XBC // 4), XBC // 4, axis=2)
    gsm["cm_dw_w"] = lax.dynamic_slice_in_dim(gsm["cm_dw_w"], shard * (CMC // 4), CMC // 4, axis=2)
    grads = dict(gsm)
    delta, new_m, new_v = {}, {}, {}
    for n in BIG[1:]:
        grads[n], delta[n], new_m[n], new_v[n] = big_out[n]
    to_lead = lambda t: jnp.transpose(t, (2, 0, 1))
    g_lead = jnp.stack([jnp.transpose(g[:, :SW]) for g in win_grads], axis=1)
    d_lead, m_lead, v_lead = _adamw_lead(to_lead(w_in), g_lead, to_lead(m_w_in), to_lead(v_w_in), "adamw_w_in")
    from_lead = lambda t: jnp.transpose(t, (1, 2, 0))
    grads["w_in"], delta["w_in"] = from_lead(g_lead), from_lead(d_lead)
    new_m["w_in"], new_v["w_in"] = from_lead(m_lead), from_lead(v_lead)

    packed_names = [n for n, _ in SMALL if n not in SHARDED_SMALL]
    pw = _pack_small(weights, packed_names)
    pg = _pack_small(grads, packed_names)
    pm = _pack_small(moms, packed_names)
    pv = _pack_small(vars_, packed_names)
    pd, pmn, pvn = _adamw(pw, pg, pm, pv, "adamw_small")
    for dst, buf in ((delta, pd), (new_m, pmn), (new_v, pvn)):
        dst.update(_unpack_small(buf, packed_names))
    for n in SHARDED_SMALL:
        shp = weights[n].shape
        flat = lambda t: t.reshape(-1, shp[-1])
        d_, m_, v_ = _adamw(flat(weights[n]), flat(grads[n]), flat(moms[n]), flat(vars_[n]), "adamw_" + n)
        delta[n], new_m[n], new_v[n] = d_.reshape(shp), m_.reshape(shp), v_.reshape(shp)

    return (loss, dx[None], *[grads[n] for n in names], *[delta[n] for n in names],
            *[new_m[n] for n in names], *[new_v[n] for n in names])
```

```python
import functools
import math

import jax
import jax.numpy as jnp
from jax import lax
from jax.experimental import pallas as pl
from jax.experimental.pallas import tpu as pltpu

F32 = jnp.float32
BF16 = jnp.bfloat16
MESH = pl.DeviceIdType.MESH
ANY = pl.BlockSpec(memory_space=pl.ANY)
VMEM_SPEC = pl.BlockSpec(memory_space=pltpu.VMEM)

D = 1024
L = 2048
DEPTH = 4
SSD_W = 1024
XBC = 1536
NH = 16
HP = 64
NS = 128
Q = 128
NC = L // Q
ATT_W = 512
NQH = 8
NKV = 2
HD = 64
HH = HD // 2
CMC = 512
CMK = 31
DFF = 4096
N_IN = 4368
N_PAD = 4480
RMS_EPS = 1e-6
LN_EPS = 1e-5
NEG = -1e30
LR, B1, B2, EPS_A, WD, STEP = 0.001, 0.9, 0.999, 1e-8, 0.01, 10
VMEM_LIMIT = 56 * 1024 * 1024
N_DEV = 8


def _cp(sem=None):
    kw = dict(vmem_limit_bytes=VMEM_LIMIT)
    if sem is not None:
        kw["dimension_semantics"] = sem
    return pltpu.CompilerParams(**kw)


def _dg(a, b, ca, cb):
    return lax.dot_general(a, b, (((ca,), (cb,)), ((), ())), preferred_element_type=F32)


def _split3(x):
    hi = x.astype(BF16)
    r = x - hi.astype(F32)
    mid = r.astype(BF16)
    lo = (r - mid.astype(F32)).astype(BF16)
    return hi, mid, lo


def _xdot_l(x, m, ca=1, cb=0):
    hi, mid, lo = _split3(x)
    return _dg(hi, m, ca, cb) + _dg(mid, m, ca, cb) + _dg(lo, m, ca, cb)


def _xdot_r(m, x, ca=1, cb=0):
    hi, mid, lo = _split3(x)
    return _dg(m, hi, ca, cb) + _dg(m, mid, ca, cb) + _dg(m, lo, ca, cb)


def _rowdot(v, m):
    return _xdot_l(jnp.broadcast_to(v, (8, v.shape[1])), m)[0:1]


def _sigmoid(x):
    return 1.0 / (1.0 + jnp.exp(-x))


def _softplus(x):
    e = jnp.exp(-jnp.abs(x))
    u = 1.0 + e
    l1p = jnp.where(u == 1.0, e, jnp.log(u) * (e / jnp.where(u == 1.0, 1.0, u - 1.0)))
    return jnp.maximum(x, 0.0) + l1p


def _tm(k):
    return L if k <= D else L // 2


def _mm(a, b, mode, name, add=None, relu2_of=None, tn=512):
    m, k = a.shape
    tm = min(m, _tm(k))
    a_spec = pl.BlockSpec((tm, k), lambda i, j: (i, 0))
    if mode == "nn":
        n = b.shape[1]
        b_spec = pl.BlockSpec((k, tn), lambda i, j: (0, j))
        cb = 0
    else:
        n = b.shape[0]
        b_spec = pl.BlockSpec((tn, k), lambda i, j: (j, 0))
        cb = 1
    assert m % tm == 0 and n % tn == 0, (m, n, tm, tn)
    o_spec = pl.BlockSpec((tm, tn), lambda i, j: (i, j))
    out_dtype = F32
    if relu2_of is not None:
        def body(a_ref, b_ref, c_ref, o_ref):
            o_ref[...] = (_dg(a_ref[...], b_ref[...], 1, cb) * (2.0 * jnp.sqrt(c_ref[...].astype(F32)))).astype(BF16)
        ins, specs, out_dtype = (a, b, relu2_of), [a_spec, b_spec, o_spec], BF16
    elif add is None:
        def body(a_ref, b_ref, o_ref):
            o_ref[...] = _dg(a_ref[...], b_ref[...], 1, cb)
        ins, specs = (a, b), [a_spec, b_spec]
    else:
        def body(a_ref, b_ref, c_ref, o_ref):
            o_ref[...] = _dg(a_ref[...], b_ref[...], 1, cb) + c_ref[...]
        ins, specs = (a, b, add), [a_spec, b_spec, o_spec]
    return pl.pallas_call(
        body, name=name, grid=(m // tm, n // tn), in_specs=specs, out_specs=o_spec,
        out_shape=jax.ShapeDtypeStruct((m, n), out_dtype), compiler_params=_cp(("parallel", "parallel")))(*ins)


def _mm_up(a, b, name, tn=512):
    m = a.shape[0]
    cs = DFF // 4
    per = cs // tn
    tm = min(m, _tm(D))

    def body(a_ref, b_ref, r_ref):
        r = jnp.maximum(_dg(a_ref[...], b_ref[0], 1, 0), 0.0)
        r_ref[...] = (r * r).astype(BF16)

    return pl.pallas_call(
        body, name=name, grid=(m // tm, DFF // tn),
        in_specs=[pl.BlockSpec((tm, D), lambda i, j: (i, 0)),
                  pl.BlockSpec((1, D, tn), lambda i, j: (j // per, 0, j % per))],
        out_specs=pl.BlockSpec((tm, tn), lambda i, j: (i, j)),
        out_shape=jax.ShapeDtypeStruct((m, DFF), BF16), compiler_params=_cp(("parallel", "parallel")))(a, b)


def _mm_cs_nt(a, b, name, tn=512):
    m = a.shape[0]
    _, n, cs = b.shape
    tm = min(m, _tm(4 * cs))

    def body(a_ref, b_ref, o_ref):
        acc = _dg(a_ref[:, 0:cs], b_ref[0], 1, 1)
        for s in range(1, 4):
            acc = acc + _dg(a_ref[:, s * cs:(s + 1) * cs], b_ref[s], 1, 1)
        o_ref[...] = acc

    return pl.pallas_call(
        body, name=name, grid=(m // tm, n // tn),
        in_specs=[pl.BlockSpec((tm, 4 * cs), lambda i, j: (i, 0)), pl.BlockSpec((4, tn, cs), lambda i, j: (0, j, 0))],
        out_specs=pl.BlockSpec((tm, tn), lambda i, j: (i, j)),
        out_shape=jax.ShapeDtypeStruct((m, n), F32), compiler_params=_cp(("parallel", "parallel")))(a, b)


def _mm_dw(a, b, name, col_shards=False, tn=512, a_is_t=False):
    if a_is_t:
        m, k = a.shape
    else:
        k, m = a.shape
    n = b.shape[1]
    tm = min(m, D)
    a_spec = pl.BlockSpec((tm, k), lambda i, j: (i, 0)) if a_is_t else pl.BlockSpec((k, tm), lambda i, j: (0, i))
    ca = 1 if a_is_t else 0
    b_spec = pl.BlockSpec((k, tn), lambda i, j: (0, j))
    if col_shards:
        per = (n // 4) // tn
        o_spec = pl.BlockSpec((1, tm, tn), lambda i, j: (j // per, i, j % per))
        shape = (4, m, n // 4)
    else:
        o_spec = pl.BlockSpec((tm, tn), lambda i, j: (i, j))
        shape = (m, n)

    def body(a_ref, b_ref, o_ref, ob_ref):
        acc = _dg(a_ref[...], b_ref[...], ca, 0).reshape(o_ref.shape)
        o_ref[...] = acc
        ob_ref[...] = acc.astype(BF16)

    return pl.pallas_call(
        body, name=name, grid=(m // tm, n // tn), in_specs=[a_spec, b_spec], out_specs=[o_spec, o_spec],
        out_shape=[jax.ShapeDtypeStruct(shape, F32), jax.ShapeDtypeStruct(shape, BF16)],
        compiler_params=_cp(("parallel", "parallel")))(a, b)


TR = 256


def _rms_fwd(x, w, name):
    def body(x_ref, w_ref, o_ref, ot_ref):
        xv = x_ref[...]
        r = lax.rsqrt(jnp.mean(xv * xv, axis=-1, keepdims=True) + RMS_EPS)
        y = xv * r * w_ref[...]
        o_ref[...] = y.astype(BF16)
        ot_ref[...] = y.T.astype(BF16)

    return pl.pallas_call(
        body, name=name, grid=(L // TR,),
        in_specs=[pl.BlockSpec((TR, D), lambda i: (i, 0)), pl.BlockSpec((1, D), lambda i: (0, 0))],
        out_specs=[pl.BlockSpec((TR, D), lambda i: (i, 0)), pl.BlockSpec((D, TR), lambda i: (0, i))],
        out_shape=[jax.ShapeDtypeStruct((L, D), BF16), jax.ShapeDtypeStruct((D, L), BF16)],
        compiler_params=_cp(("parallel",)))(x, w)


def _rms_bwd(x, w, dh, dres, name):
    def body(x_ref, w_ref, dh_ref, dr_ref, dx_ref, dxb_ref, dw_ref):
        xv = x_ref[...]
        r = lax.rsqrt(jnp.mean(xv * xv, axis=-1, keepdims=True) + RMS_EPS)
        n = xv * r
        dhv = dh_ref[...]
        g = dhv * w_ref[...]
        dx = dr_ref[...] + r * (g - n * jnp.mean(g * n, axis=-1, keepdims=True))
        dx_ref[...] = dx
        dxb_ref[...] = dx.astype(BF16)

        @pl.when(pl.program_id(0) == 0)
        def _():
            dw_ref[...] = jnp.zeros_like(dw_ref)
        dw_ref[...] += jnp.sum(dhv * n, axis=0, keepdims=True)

    row = pl.BlockSpec((TR, D), lambda i: (i, 0))
    vec = pl.BlockSpec((1, D), lambda i: (0, 0))
    return pl.pallas_call(
        body, name=name, grid=(L // TR,), in_specs=[row, vec, row, row], out_specs=[row, row, vec],
        out_shape=[jax.ShapeDtypeStruct((L, D), F32), jax.ShapeDtypeStruct((L, D), BF16),
                   jax.ShapeDtypeStruct((1, D), F32)],
        compiler_params=_cp(("arbitrary",)))(x, w, dh, dres)


def _adamw_lead(w, g, m, v, name):
    lead, a, b = w.shape
    tr = max(t for t in range(1, lead + 1) if lead % t == 0 and t * a * b * 4 <= 2 * 1024 * 1024)
    c1 = 1.0 / (1.0 - B1 ** STEP)
    c2 = 1.0 / (1.0 - B2 ** STEP)

    def body(w_ref, g_ref, m_ref, v_ref, d_ref, mo_ref, vo_ref):
        gv = g_ref[...]
        mn = B1 * m_ref[...] + (1.0 - B1) * gv
        vn = B2 * v_ref[...] + (1.0 - B2) * (gv * gv)
        mo_ref[...] = mn
        vo_ref[...] = vn
        d_ref[...] = -LR * ((mn * c1) / (jnp.sqrt(vn * c2) + EPS_A) + WD * w_ref[...])

    blk = pl.BlockSpec((tr, a, b), lambda i: (i, 0, 0))
    shp = jax.ShapeDtypeStruct(w.shape, F32)
    return pl.pallas_call(body, name=name, grid=(lead // tr,), in_specs=[blk] * 4, out_specs=[blk] * 3,
                          out_shape=[shp] * 3, compiler_params=_cp(("parallel",)))(w, g, m, v)


def _adamw_layer(w, g, m, v, layer, prev, name):
    _, rows, cols = w.shape
    tr = 256 if cols * 256 * 4 <= 2 * 1024 * 1024 else 128
    c1 = 1.0 / (1.0 - B1 ** STEP)
    c2 = 1.0 / (1.0 - B2 ** STEP)
    n_prev = 0 if prev is None else 4

    def body(*refs):
        w_ref, g_ref, m_ref, v_ref = refs[:4]
        go_ref, d_ref, mo_ref, vo_ref = refs[4 + n_prev:]
        gv = g_ref[...]
        mn = B1 * m_ref[0] + (1.0 - B1) * gv
        vn = B2 * v_ref[0] + (1.0 - B2) * (gv * gv)
        go_ref[0] = gv
        mo_ref[0] = mn
        vo_ref[0] = vn
        d_ref[0] = -LR * ((mn * c1) / (jnp.sqrt(vn * c2) + EPS_A) + WD * w_ref[0])

    lay = pl.BlockSpec((1, tr, cols), lambda i: (layer, i, 0))
    shp = jax.ShapeDtypeStruct(w.shape, F32)
    return pl.pallas_call(
        body, name=name, grid=(rows // tr,),
        in_specs=[lay, pl.BlockSpec((tr, cols), lambda i: (i, 0)), lay, lay] + [ANY] * n_prev,
        out_specs=[lay] * 4, out_shape=[shp] * 4,
        input_output_aliases={4 + i: i for i in range(n_prev)},
        compiler_params=_cp(("parallel",)))(w, g, m, v, *(prev or ()))


def _loss_bwd(y, t, name):
    def body(y_ref, t_ref, l_ref, d_ref, db_ref):
        e = y_ref[...] - t_ref[...]
        d = e * (1.0 / D)
        d_ref[...] = d
        db_ref[...] = d.astype(BF16)

        @pl.when(pl.program_id(0) == 0)
        def _():
            l_ref[...] = jnp.zeros_like(l_ref)
        s = jnp.sum(jnp.sum(e * e, axis=-1, keepdims=True), axis=0, keepdims=True)
        l_ref[...] += jnp.broadcast_to(s, l_ref.shape)

    row = pl.BlockSpec((TR, D), lambda i: (i, 0))
    tile = pl.BlockSpec((8, 128), lambda i: (0, 0))
    return pl.pallas_call(
        body, name=name, grid=(L // TR,), in_specs=[row, row], out_specs=[tile, row, row],
        out_shape=[jax.ShapeDtypeStruct((8, 128), F32), jax.ShapeDtypeStruct((L, D), F32),
                   jax.ShapeDtypeStruct((L, D), BF16)],
        compiler_params=_cp(("arbitrary",)))(y, t)


def _adamw(w, g, m, v, name):
    rows, cols = w.shape
    tr = rows
    for cand in (512, 256, 128, 64, 32, 16, 8):
        if rows % cand == 0 and cand * cols * 4 <= 2 * 1024 * 1024:
            tr = cand
            break
    c1 = 1.0 / (1.0 - B1 ** STEP)
    c2 = 1.0 / (1.0 - B2 ** STEP)

    def body(w_ref, g_ref, m_ref, v_ref, d_ref, mo_ref, vo_ref):
        gv = g_ref[...]
        mn = B1 * m_ref[...] + (1.0 - B1) * gv
        vn = B2 * v_ref[...] + (1.0 - B2) * (gv * gv)
        mo_ref[...] = mn
        vo_ref[...] = vn
        d_ref[...] = -LR * ((mn * c1) / (jnp.sqrt(vn * c2) + EPS_A) + WD * w_ref[...])

    blk = pl.BlockSpec((tr, cols), lambda i: (i, 0))
    shp = jax.ShapeDtypeStruct((rows, cols), F32)
    return pl.pallas_call(body, name=name, grid=(rows // tr,), in_specs=[blk] * 4, out_specs=[blk] * 3,
                          out_shape=[shp] * 3, compiler_params=_cp(("parallel",)))(w, g, m, v)


CT = 256
CPAD = 32


U_X, U_Z, U_A, U_G, U_Q, U_K, U_V, U_DT = 0, 1536, 2560, 3072, 3584, 4096, 4224, 4352


CEXT = 8
CWIN = CT + CPAD
CROWS = L + CPAD + CEXT


def _fill_shifted(src_ref, base, win_ref, sh_ref):
    win_ref[...] = src_ref[pl.ds(base, CWIN + CEXT), :]
    for p in range(8):
        sh_ref[p] = win_ref[pl.ds(p, CWIN), :]


def _tap(sh_ref, o):
    return sh_ref[o % 8, 8 * (o // 8):8 * (o // 8) + CT, :]


CB = 128


def _conv_specs():
    return [pl.BlockSpec((L, CB), lambda j: (0, U_A // CB + j)), pl.BlockSpec((L, CB), lambda j: (0, U_G // CB + j))]


def _conv_scratch(n_padded):
    return ([pltpu.VMEM((CROWS, CB), F32)] * n_padded
            + [pltpu.VMEM((CWIN + CEXT, CB), F32), pltpu.VMEM((8, CWIN, CB), F32)])


def _fill_gated(a_ref, g_ref, hp_ref):
    hp_ref[0:CPAD, :] = jnp.zeros((CPAD, CB), F32)
    hp_ref[CPAD:CPAD + L, :] = a_ref[...] * _sigmoid(g_ref[...])
    hp_ref[CPAD + L:, :] = jnp.zeros((CEXT, CB), F32)


def _conf_conv_fwd(u, w, b, name):
    def body(a_ref, g_ref, w_ref, b_ref, c_ref, hp_ref, win_ref, sh_ref):
        _fill_gated(a_ref, g_ref, hp_ref)

        def tile(i, carry):
            base = pl.multiple_of(i * CT, CT)
            _fill_shifted(hp_ref, base, win_ref, sh_ref)
            c = jnp.broadcast_to(b_ref[...], (CT, CB))
            for k in range(CMK):
                c = c + w_ref[k:k + 1, :] * _tap(sh_ref, 2 + k)
            c_ref[pl.ds(base, CT), :] = c
            return carry

        lax.fori_loop(0, L // CT, tile, 0)

    return pl.pallas_call(
        body, name=name, grid=(CMC // CB,),
        in_specs=_conv_specs() + [pl.BlockSpec((CMK, CB), lambda j: (0, j)), pl.BlockSpec((1, CB), lambda j: (0, j))],
        out_specs=pl.BlockSpec((L, CB), lambda j: (0, j)),
        out_shape=jax.ShapeDtypeStruct((L, CMC), F32), scratch_shapes=_conv_scratch(1),
        compiler_params=_cp(("parallel",)))(u, u, w, b)


def _conf_ln_fwd(c, lw, lb, name):
    def body(c_ref, lw_ref, lb_ref, o_ref):
        cv = c_ref[...]
        cc = cv - jnp.mean(cv, axis=-1, keepdims=True)
        var = jnp.mean(cc * cc, axis=-1, keepdims=True)
        l = cc * lax.rsqrt(var + LN_EPS) * lw_ref[...] + lb_ref[...]
        o_ref[...] = (l * _sigmoid(l)).astype(BF16)

    row = pl.BlockSpec((TR, CMC), lambda i: (i, 0))
    vec = pl.BlockSpec((1, CMC), lambda i: (0, 0))
    return pl.pallas_call(body, name=name, grid=(L // TR,), in_specs=[row, vec, vec], out_specs=row,
                          out_shape=jax.ShapeDtypeStruct((L, CMC), BF16),
                          compiler_params=_cp(("parallel",)))(c, lw, lb)


def _conf_ln_bwd(c, lw, lb, dy, name):
    def body(c_ref, lw_ref, lb_ref, dy_ref, dc_ref, db_ref, dlw_ref, dlb_ref):
        cv = c_ref[...]
        cc = cv - jnp.mean(cv, axis=-1, keepdims=True)
        var = jnp.mean(cc * cc, axis=-1, keepdims=True)
        rstd = lax.rsqrt(var + LN_EPS)
        n = cc * rstd
        l = n * lw_ref[...] + lb_ref[...]
        sl = _sigmoid(l)
        dl = dy_ref[...] * (sl * (1.0 + l * (1.0 - sl)))
        dn = dl * lw_ref[...]
        dc = rstd * (dn - jnp.mean(dn, axis=-1, keepdims=True) - n * jnp.mean(dn * n, axis=-1, keepdims=True))
        dc_ref[...] = dc

        @pl.when(pl.program_id(0) == 0)
        def _():
            db_ref[...] = jnp.zeros_like(db_ref)
            dlw_ref[...] = jnp.zeros_like(dlw_ref)
            dlb_ref[...] = jnp.zeros_like(dlb_ref)
        db_ref[...] += jnp.sum(dc, axis=0, keepdims=True)
        dlw_ref[...] += jnp.sum(dl * n, axis=0, keepdims=True)
        dlb_ref[...] += jnp.sum(dl, axis=0, keepdims=True)

    row = pl.BlockSpec((TR, CMC), lambda i: (i, 0))
    vec = pl.BlockSpec((1, CMC), lambda i: (0, 0))
    vshape = jax.ShapeDtypeStruct((1, CMC), F32)
    return pl.pallas_call(
        body, name=name, grid=(L // TR,),
        in_specs=[row, vec, vec, pl.BlockSpec((TR, CMC), lambda i: (i, (SSD_W + ATT_W) // CMC))],
        out_specs=[row, vec, vec, vec], out_shape=[jax.ShapeDtypeStruct((L, CMC), F32), vshape, vshape, vshape],
        compiler_params=_cp(("arbitrary",)))(c, lw, lb, dy)


def _conf_conv_bwd(u, dc, w, name):
    def body(a_ref, g_ref, dc_ref, w_ref, da_ref, dg_ref, dw_ref, hp_ref, dcp_ref, win_ref, sh_ref, dwacc_ref):
        _fill_gated(a_ref, g_ref, hp_ref)
        dcp_ref[0:L, :] = dc_ref[...]
        dcp_ref[L:, :] = jnp.zeros((CPAD + CEXT, CB), F32)
        dwacc_ref[...] = jnp.zeros_like(dwacc_ref)

        def tile(i, carry):
            base = pl.multiple_of(i * CT, CT)
            _fill_shifted(hp_ref, base, win_ref, sh_ref)
            dcv = dcp_ref[pl.ds(base, CT), :]
            for k in range(CMK):
                dwacc_ref[k] += (dcv * _tap(sh_ref, 2 + k)).reshape(CT // 8, 8, CB).sum(axis=0)
            _fill_shifted(dcp_ref, base, win_ref, sh_ref)
            dh = jnp.zeros((CT, CB), F32)
            for k in range(CMK):
                dh = dh + w_ref[k:k + 1, :] * _tap(sh_ref, CMK - 1 - k)
            av = a_ref[pl.ds(base, CT), :]
            sg = _sigmoid(g_ref[pl.ds(base, CT), :])
            da_ref[pl.ds(base, CT), :] = (dh * sg).astype(BF16)
            dg_ref[pl.ds(base, CT), :] = (dh * av * sg * (1.0 - sg)).astype(BF16)
            return carry

        lax.fori_loop(0, L // CT, tile, 0)
        for k in range(CMK):
            dw_ref[k:k + 1, :] = jnp.sum(dwacc_ref[k], axis=0, keepdims=True)
        dw_ref[CMK:, :] = jnp.zeros((32 - CMK, CB), F32)

    col = pl.BlockSpec((L, CB), lambda j: (0, j))
    return pl.pallas_call(
        body, name=name, grid=(CMC // CB,),
        in_specs=_conv_specs() + [col, pl.BlockSpec((CMK, CB), lambda j: (0, j))],
        out_specs=[col, col, pl.BlockSpec((32, CB), lambda j: (0, j))],
        out_shape=[jax.ShapeDtypeStruct((L, CMC), BF16), jax.ShapeDtypeStruct((L, CMC), BF16),
                   jax.ShapeDtypeStruct((32, CMC), F32)],
        scratch_shapes=_conv_scratch(2) + [pltpu.VMEM((32, 8, CB), F32)],
        compiler_params=_cp(("parallel",)))(u, u, dc, w)


NPAIR = NQH // 2


def _partner(x, lo32):
    return jnp.where(lo32, pltpu.roll(x, 96, 1), pltpu.roll(x, 32, 1))


def _swa_prep(x, w2, w2p, c4, s4, bd, lo32):
    r = lax.rsqrt(_xdot_l(x * x, bd) * (1.0 / HD) + RMS_EPS)
    xh = x * r
    return r, xh, xh * w2 * c4 + _partner(xh, lo32) * w2p * s4


def _swa_unprep(dr, r, xh, w2, w2p, c4, s4, bd, lo32):
    dn = dr * c4
    dnp = dr * s4
    gx = dn * w2 + _partner(dnp * w2p, lo32)
    dw = jnp.sum((dn + _partner(dnp, lo32)) * xh, axis=0, keepdims=True)
    mu = _xdot_l(gx * xh, bd) * (1.0 / HD)
    return r * (gx - xh * mu), dw


def _swa_softmax(s, sink):
    row = lax.broadcasted_iota(jnp.int32, (L, 2 * Q), 0)
    col = lax.broadcasted_iota(jnp.int32, (L, 2 * Q), 1)
    rm = row & (Q - 1)
    valid = (col > rm) & (col <= rm + Q) & ((row >= Q) | (col >= Q))
    s = jnp.where(valid, s * (1.0 / math.sqrt(HD)), NEG)
    m = jnp.maximum(jnp.max(s, axis=-1, keepdims=True), sink)
    p = jnp.exp(s - m)
    ps = jnp.exp(sink - m)
    inv = 1.0 / (jnp.sum(p, axis=-1, keepdims=True) + ps)
    return p * inv, ps * inv


def _swa_in_specs():
    tab = pl.BlockSpec((L, 128), lambda p: (0, 0))
    wv = pl.BlockSpec((1, 128), lambda p: (0, 0))
    sk = pl.BlockSpec((1, 1, 128), lambda p: (p, 0, 0))
    return [pl.BlockSpec((L, 128), lambda p: (0, U_Q // 128 + p)), pl.BlockSpec((L, 128), lambda p: (0, U_K // 128)),
            pl.BlockSpec((L, 128), lambda p: (0, U_V // 128)), tab, tab, wv, wv, wv, wv, sk, sk,
            pl.BlockSpec((128, 128), lambda p: (0, 0))]


def _swa_setup(q_ref, k_ref, v_ref, c_ref, s_ref, qw_ref, qwp_ref, kw_ref, kwp_ref, bd_ref, kpad, vpad):
    g = pl.program_id(0) // 2
    lane = lax.broadcasted_iota(jnp.int32, (L, 128), 1)
    lo32 = (lane & 32) == 0
    own = (lane >> 6) == g
    c4, s4, bd = c_ref[...], s_ref[...], bd_ref[...]
    qn = _swa_prep(q_ref[...], qw_ref[...], qwp_ref[...], c4, s4, bd, lo32)
    kn = _swa_prep(k_ref[...], kw_ref[...], kwp_ref[...], c4, s4, bd, lo32)
    vv = v_ref[...]
    kpad[0:Q, :] = jnp.zeros((Q, 128), BF16)
    vpad[0:Q, :] = jnp.zeros((Q, 128), BF16)
    kpad[Q:, :] = jnp.where(own, kn[2], pltpu.roll(kn[2], HD, 1)).astype(BF16)
    vpad[Q:, :] = jnp.where(own, vv, pltpu.roll(vv, HD, 1)).astype(BF16)
    return qn, kn, lo32, own, c4, s4, bd


def _swa_fwd(u, cos4, sin4, qw2, qw2p, kw2, kw2p, sink_e, sink_o, bd, name):
    def body(q_ref, k_ref, v_ref, c_ref, s_ref, qw_ref, qwp_ref, kw_ref, kwp_ref, ske_ref, sko_ref, bd_ref,
             o_ref, kpad, vpad, s_scr, p_scr):
        qn, _, _, _, _, _, _ = _swa_setup(q_ref, k_ref, v_ref, c_ref, s_ref, qw_ref, qwp_ref, kw_ref, kwp_ref,
                                          bd_ref, kpad, vpad)
        qr = qn[2]
        first = lax.broadcasted_iota(jnp.int32, (Q, 128), 1) < HD
        for n in range(NC):
            rows = slice(n * Q, (n + 1) * Q)
            kc = kpad[n * Q:(n + 2) * Q, :]
            s_scr[0, rows, :] = _dg(jnp.where(first, qr[rows], 0.0).astype(BF16), kc, 1, 1)
            s_scr[1, rows, :] = _dg(jnp.where(first, 0.0, qr[rows]).astype(BF16), kc, 1, 1)
        for h, sk_ref in ((0, ske_ref), (1, sko_ref)):
            p, _ = _swa_softmax(s_scr[h], sk_ref[0][:, 0:1])
            p_scr[h] = p.astype(BF16)
        for n in range(NC):
            rows = slice(n * Q, (n + 1) * Q)
            vc = vpad[n * Q:(n + 2) * Q, :]
            o_ref[rows, :] = jnp.where(first, _dg(p_scr[0, rows, :], vc, 1, 0),
                                       _dg(p_scr[1, rows, :], vc, 1, 0)).astype(BF16)

    return pl.pallas_call(
        body, name=name, grid=(NPAIR,), in_specs=_swa_in_specs(),
        out_specs=pl.BlockSpec((L, 128), lambda p: (0, p)),
        out_shape=jax.ShapeDtypeStruct((L, ATT_W), BF16),
        scratch_shapes=[pltpu.VMEM((L + Q, 128), BF16), pltpu.VMEM((L + Q, 128), BF16),
                        pltpu.VMEM((2, L, 2 * Q), F32), pltpu.VMEM((2, L, 2 * Q), BF16)],
        compiler_params=_cp(("arbitrary",)))(u, u, u, cos4, sin4, qw2, qw2p, kw2, kw2p, sink_e, sink_o, bd)


def _swa_bwd(u, dy, cos4, sin4, qw2, qw2p, kw2, kw2p, sink_e, sink_o, bd, name):
    def body(q_ref, k_ref, v_ref, c_ref, s_ref, qw_ref, qwp_ref, kw_ref, kwp_ref, ske_ref, sko_ref, bd_ref, do_ref,
             dq_ref, dk_ref, dv_ref, dqw_ref, dkw_ref, dse_ref, dso_ref,
             kpad, vpad, s_scr, dp_scr, ds_scr, pb_scr, dkr_acc, dv_acc, dqr_scr):
        pidx = pl.program_id(0)

        @pl.when(pidx == 0)
        def _():
            dkr_acc[...] = jnp.zeros_like(dkr_acc)
            dv_acc[...] = jnp.zeros_like(dv_acc)
            dqw_ref[...] = jnp.zeros_like(dqw_ref)

        qn, kn, lo32, own, c4, s4, bd = _swa_setup(q_ref, k_ref, v_ref, c_ref, s_ref, qw_ref, qwp_ref, kw_ref,
                                                   kwp_ref, bd_ref, kpad, vpad)
        qr = qn[2]
        lane_q = lax.broadcasted_iota(jnp.int32, (Q, 128), 1)
        first = lane_q < HD
        own_q = (lane_q >> 6) == pidx // 2

        def halves(t):
            return jnp.where(first, t, 0.0).astype(BF16), jnp.where(first, 0.0, t).astype(BF16)

        for n in range(NC):
            rows = slice(n * Q, (n + 1) * Q)
            kc = kpad[n * Q:(n + 2) * Q, :]
            vc = vpad[n * Q:(n + 2) * Q, :]
            qm = halves(qr[rows])
            dom = halves(do_ref[rows, :])
            for h in range(2):
                s_scr[h, rows, :] = _dg(qm[h], kc, 1, 1)
                dp_scr[h, rows, :] = _dg(dom[h], vc, 1, 1)
        for h, sk_ref, dsk_ref in ((0, ske_ref, dse_ref), (1, sko_ref, dso_ref)):
            p, ps = _swa_softmax(s_scr[h], sk_ref[0][:, 0:1])
            dp = dp_scr[h]
            delta = jnp.sum(p * dp, axis=-1, keepdims=True)
            dsk_ref[0] = jnp.broadcast_to(-jnp.sum(ps * delta, axis=0, keepdims=True), (1, 128))
            ds_scr[h] = (p * (dp - delta) * (1.0 / math.sqrt(HD))).astype(BF16)
            pb_scr[h] = p.astype(BF16)
        for n in range(NC):
            rows = slice(n * Q, (n + 1) * Q)
            kc = kpad[n * Q:(n + 2) * Q, :]
            dqr_scr[rows, :] = jnp.where(first, _dg(ds_scr[0, rows, :], kc, 1, 0), _dg(ds_scr[1, rows, :], kc, 1, 0))
        for m in range(NC):
            acc_k = jnp.zeros((Q, 128), F32)
            acc_v = jnp.zeros((Q, 128), F32)
            for n, cols in ((m, slice(Q, 2 * Q)), (m + 1, slice(0, Q))):
                if n >= NC:
                    continue
                rows = slice(n * Q, (n + 1) * Q)
                qm = halves(qr[rows])
                dom = halves(do_ref[rows, :])
                for h in range(2):
                    acc_k = acc_k + _dg(ds_scr[h, rows, cols], qm[h], 0, 0)
                    acc_v = acc_v + _dg(pb_scr[h, rows, cols], dom[h], 0, 0)
            rows = slice(m * Q, (m + 1) * Q)
            dkr_acc[rows, :] += jnp.where(own_q, acc_k + pltpu.roll(acc_k, HD, 1), 0.0)
            dv_acc[rows, :] += jnp.where(own_q, acc_v + pltpu.roll(acc_v, HD, 1), 0.0)
        dq, dqw = _swa_unprep(dqr_scr[...], qn[0], qn[1], qw_ref[...], qwp_ref[...], c4, s4, bd, lo32)
        dq_ref[...] = dq.astype(BF16)
        dqw_ref[...] += dqw

        @pl.when(pidx == NPAIR - 1)
        def _():
            dk, dkw = _swa_unprep(dkr_acc[...], kn[0], kn[1], kw_ref[...], kwp_ref[...], c4, s4, bd, lo32)
            dk_ref[...] = dk.astype(BF16)
            dkw_ref[...] = dkw
            dv_ref[...] = dv_acc[...].astype(BF16)

    full = pl.BlockSpec((L, 128), lambda p: (0, 0))
    wv = pl.BlockSpec((1, 128), lambda p: (0, 0))
    sk = pl.BlockSpec((1, 1, 128), lambda p: (p, 0, 0))
    vec = jax.ShapeDtypeStruct((1, 128), F32)
    skv = jax.ShapeDtypeStruct((NPAIR, 1, 128), F32)
    return pl.pallas_call(
        body, name=name, grid=(NPAIR,),
        in_specs=_swa_in_specs() + [pl.BlockSpec((L, 128), lambda p: (0, SSD_W // 128 + p))],
        out_specs=[pl.BlockSpec((L, 128), lambda p: (0, p)), full, full, wv, wv, sk, sk],
        out_shape=[jax.ShapeDtypeStruct((L, ATT_W), BF16), jax.ShapeDtypeStruct((L, 128), BF16),
                   jax.ShapeDtypeStruct((L, 128), BF16), vec, vec, skv, skv],
        scratch_shapes=[pltpu.VMEM((L + Q, 128), BF16), pltpu.VMEM((L + Q, 128), BF16),
                        pltpu.VMEM((2, L, 2 * Q), F32), pltpu.VMEM((2, L, 2 * Q), F32),
                        pltpu.VMEM((2, L, 2 * Q), BF16), pltpu.VMEM((2, L, 2 * Q), BF16),
                        pltpu.VMEM((L, 128), F32), pltpu.VMEM((L, 128), F32), pltpu.VMEM((L, 128), F32)],
        compiler_params=_cp(("arbitrary",)))(u, u, u, cos4, sin4, qw2, qw2p, kw2, kw2p, sink_e, sink_o, bd, dy)


def _ssd_consts():
    hh = jnp.arange(128)[:, None]
    e = (hh == (jnp.arange(SSD_W)[None, :] // HP)).astype(BF16)
    e2 = (hh == (jnp.arange(NH * 128)[None, :] // 128)).astype(BF16)
    et = e.T
    tril = (jnp.arange(Q)[:, None] >= jnp.arange(Q)[None, :]).astype(BF16)
    triu = tril.T
    eye = jnp.eye(128, dtype=BF16)
    return e, e2, et, tril, triu, eye


def _ssd_common(x_ref, ext_scr, cw_ref, cb_ref, dt_ref, dtb_ref, alog_ref, e_ref, e2_ref, tril_ref, triu_ref,
                arow_scr, acol_scr, eax_scr):
    conv = jnp.broadcast_to(cb_ref[...], (Q, XBC))
    for k in range(4):
        conv = conv + cw_ref[k:k + 1, :] * ext_scr[pl.ds(5 + k, Q), :]
    sg = _sigmoid(conv)
    xbc = conv * sg
    dtpre = dt_ref[...] + dtb_ref[...]
    dt = _softplus(dtpre)
    a = -jnp.exp(alog_ref[...])
    adt = dt * a
    acol = _xdot_r(tril_ref[...], adt)
    acol_scr[...] = acol
    arow_scr[...] = _xdot_l(adt, triu_ref[...], 0, 0)
    alast = acol_scr[Q - 1:Q, :]
    ea = jnp.exp(acol)
    decs = jnp.exp(alast - acol)
    e = e_ref[...]
    dt_x = _xdot_l(dt, e)
    eax_scr[...] = _xdot_l(ea, e)
    decs_x = _xdot_l(decs, e)
    acx2 = _xdot_l(acol, e2_ref[...])
    return conv, sg, xbc, dtpre, dt, a, adt, acol, alast, ea, decs, dt_x, decs_x, acx2


def _ssd_fwd(u, cw, cb, dtb, alog, dxp, nw, consts, name):
    e, e2, et, tril, triu, eye = consts

    def body(z0_ref, z1_ref, x_ref, dt_ref, cw_ref, cb_ref, dtb_ref, alog_ref, dx_ref, nw_ref, e_ref, e2_ref,
             tril_ref, triu_ref, ya_ref, ypre_ref, st_ref, s_scr, ext_scr, arow_scr, acol_scr, eax_scr):
        c = pl.program_id(0)

        @pl.when(c == 0)
        def _():
            s_scr[...] = jnp.zeros_like(s_scr)
            ext_scr[0:8, :] = jnp.zeros((8, XBC), F32)
        ext_scr[8:8 + Q, :] = x_ref[...]
        (conv, sg, xbc, dtpre, dt, a, adt, acol, alast, ea, decs, dt_x, decs_x, acx2) = _ssd_common(
            x_ref, ext_scr, cw_ref, cb_ref, dt_ref, dtb_ref, alog_ref, e_ref, e2_ref, tril_ref, triu_ref,
            arow_scr, acol_scr, eax_scr)
        ext_scr[0:8, :] = ext_scr[Q:Q + 8, :]
        xs = xbc[:, :SSD_W]
        xdt = xs * dt_x
        lane = lax.broadcasted_iota(jnp.int32, (Q, 128), 1)
        causal = lax.broadcasted_iota(jnp.int32, (Q, Q), 0) >= lax.broadcasted_iota(jnp.int32, (Q, Q), 1)
        for g in range(2):
            bg = xbc[:, SSD_W + g * NS:SSD_W + (g + 1) * NS].astype(BF16)
            cg = xbc[:, SSD_W + 2 * NS + g * NS:SSD_W + 2 * NS + (g + 1) * NS].astype(BF16)
            cbm = _dg(cg, bg, 1, 1)
            sgv = s_scr[g]
            st_ref[0, g] = sgv
            gc = slice(g * 512, (g + 1) * 512)
            yoff = _dg(cg, sgv.astype(BF16), 1, 0) * eax_scr[:, gc]
            for pr in range(4):
                h0 = g * 8 + 2 * pr
                h1 = h0 + 1
                c0 = g * 512 + pr * 128
                xp = xdt[:, c0:c0 + 128].astype(BF16)
                w0 = (cbm * jnp.exp(jnp.where(causal, acx2[:, h0 * 128:(h0 + 1) * 128] - arow_scr[h0:h0 + 1, :],
                                              NEG))).astype(BF16)
                w1 = (cbm * jnp.exp(jnp.where(causal, acx2[:, h1 * 128:(h1 + 1) * 128] - arow_scr[h1:h1 + 1, :],
                                              NEG))).astype(BF16)
                yd = jnp.where(lane < HP, _dg(w0, xp, 1, 0), _dg(w1, xp, 1, 0))
                ypre_ref[:, c0:c0 + 128] = (yd + yoff[:, pr * 128:(pr + 1) * 128]
                                            + xs[:, c0:c0 + 128] * dx_ref[:, c0:c0 + 128])
            contrib = _dg(bg, (xdt[:, gc] * decs_x[:, gc]).astype(BF16), 0, 0)
            s_scr[g] = sgv * eax_scr[Q - 1:Q, gc] + contrib
        for g, zr in enumerate((z0_ref, z1_ref)):
            gc = slice(g * 512, (g + 1) * 512)
            zz = zr[...]
            ggg = ypre_ref[:, gc] * (zz * _sigmoid(zz))
            rstd = lax.rsqrt(jnp.mean(ggg * ggg, axis=-1, keepdims=True) + RMS_EPS)
            ya_ref[:, gc] = (ggg * rstd * nw_ref[:, gc]).astype(BF16)

    def row(w, blk=0):
        return pl.BlockSpec((Q, w), lambda c: (c, blk))

    def full(shape):
        return pl.BlockSpec(shape, lambda c: (0,) * len(shape))

    return pl.pallas_call(
        body, name=name, grid=(NC,),
        in_specs=[row(512, U_Z // 512), row(512, U_Z // 512 + 1), row(XBC, U_X // XBC), row(128, U_DT // 128),
                  full((4, XBC)), full((1, XBC)), full((1, 128)), full((1, 128)),
                  full((1, SSD_W)), full((1, SSD_W)), full((128, SSD_W)), full((128, NH * 128)), full((Q, Q)),
                  full((Q, Q))],
        out_specs=[row(SSD_W), row(SSD_W), pl.BlockSpec((1, 2, NS, 512), lambda c: (c, 0, 0, 0))],
        out_shape=[jax.ShapeDtypeStruct((L, SSD_W), BF16), jax.ShapeDtypeStruct((L, SSD_W), F32),
                   jax.ShapeDtypeStruct((NC, 2, NS, 512), F32)],
        scratch_shapes=[pltpu.VMEM((2, NS, 512), F32), pltpu.VMEM((Q + 8, XBC), F32), pltpu.VMEM((128, Q), F32),
                        pltpu.VMEM((Q, 128), F32), pltpu.VMEM((Q, SSD_W), F32)],
        compiler_params=_cp(("arbitrary",)))(u, u, u, u, cw, cb, dtb, alog, dxp, nw, e, e2, tril, triu)


def _ssd_bwd(u, ypre, st, dy, cw, cb, dtb, alog, dxp, nw, consts, name):
    e, e2, et, tril, triu, eye = consts

    def body(z0_ref, z1_ref, x_ref, xp_ref, dt_ref, ypre_ref, st_ref, dya_ref, cw_ref, cb_ref, dtb_ref, alog_ref, dx_ref,
             nw_ref, e_ref, e2_ref, et_ref, tril_ref, triu_ref, eye_ref,
             dz_ref, dxr_ref, ddtr_ref, dcw_ref, dcb_ref, ddtb_ref, dalog_ref, dd_ref, dnw_ref,
             g_scr, ext_scr, ext2_scr, arow_scr, acol_scr, eax_scr, darow_scr, dxdt_scr, t1_scr, t2_scr, dgg_scr):
        i = pl.program_id(0)

        @pl.when(i == 0)
        def _():
            g_scr[...] = jnp.zeros_like(g_scr)
            ext2_scr[Q:Q + 8, :] = jnp.zeros((8, XBC), F32)
            for r in (dcw_ref, dcb_ref, ddtb_ref, dalog_ref, dd_ref, dnw_ref):
                r[...] = jnp.zeros_like(r)
        not_first = jnp.where(i < NC - 1, 1.0, 0.0)
        ext_scr[0:8, :] = xp_ref[Q - 8:Q, :] * not_first
        ext_scr[8:8 + Q, :] = x_ref[...]
        (conv, sg, xbc, dtpre, dt, a, adt, acol, alast, ea, decs, dt_x, decs_x, acx2) = _ssd_common(
            x_ref, ext_scr, cw_ref, cb_ref, dt_ref, dtb_ref, alog_ref, e_ref, e2_ref, tril_ref, triu_ref,
            arow_scr, acol_scr, eax_scr)
        et_m = et_ref[...]
        xs = xbc[:, :SSD_W]
        xdt = xs * dt_x
        y = ypre_ref[...]
        zz = jnp.concatenate([z0_ref[...], z1_ref[...]], axis=1)
        sz = _sigmoid(zz)
        silu_z = zz * sz
        gg = y * silu_z
        dya = dya_ref[...]
        for g in range(2):
            gc = slice(g * 512, (g + 1) * 512)
            ggg = gg[:, gc]
            rstd = lax.rsqrt(jnp.mean(ggg * ggg, axis=-1, keepdims=True) + RMS_EPS)
            n = ggg * rstd
            dyag = dya[:, gc]
            dnw_ref[:, gc] += jnp.sum(dyag * n, axis=0, keepdims=True)
            dn = dyag * nw_ref[:, gc]
            dgg_scr[:, gc] = rstd * (dn - n * jnp.mean(dn * n, axis=-1, keepdims=True))
        dgg = dgg_scr[...]
        dy = dgg * silu_z
        dz_ref[...] = (dgg * y * (sz * (1.0 + zz * (1.0 - sz)))).astype(BF16)
        dd_ref[...] += _rowdot(jnp.sum(dy * xs, axis=0, keepdims=True), et_m)
        dxs = dy * dx_ref[...]
        dys = dy * eax_scr[...]
        lane = lax.broadcasted_iota(jnp.int32, (Q, 128), 1)
        causal = lax.broadcasted_iota(jnp.int32, (Q, Q), 0) >= lax.broadcasted_iota(jnp.int32, (Q, Q), 1)
        darow_scr[...] = jnp.zeros_like(darow_scr)
        dacol = jnp.zeros((Q, 128), F32)
        dcdx = []
        dbs = []
        dcs = []
        for g in range(2):
            gc = slice(g * 512, (g + 1) * 512)
            bg = xbc[:, SSD_W + g * NS:SSD_W + (g + 1) * NS].astype(BF16)
            cg = xbc[:, SSD_W + 2 * NS + g * NS:SSD_W + 2 * NS + (g + 1) * NS].astype(BF16)
            cbm = _dg(cg, bg, 1, 1)
            sgv = st_ref[0, g]
            sgb = sgv.astype(BF16)
            gv = g_scr[g]
            gvb = gv.astype(BF16)
            yoff = _dg(cg, sgb, 1, 0) * eax_scr[:, gc]
            dysg = dys[:, gc].astype(BF16)
            dcg = _dg(dysg, sgb, 1, 1)
            ds_off = _dg(cg, dysg, 0, 0)
            t1_scr[:, gc] = dy[:, gc] * yoff
            xdec = xdt[:, gc] * decs_x[:, gc]
            dxd = _dg(bg, gvb, 1, 0)
            dbg = _dg(xdec.astype(BF16), gvb, 1, 1)
            dxdt_g = dxd * decs_x[:, gc]
            t2_scr[:, gc] = dxd * xdt[:, gc]
            cdx = eax_scr[Q - 1:Q, gc]
            dcdx.append(jnp.sum(gv * sgv, axis=0, keepdims=True))
            g_scr[g] = gv * cdx + ds_off
            dcb_acc = jnp.zeros((Q, Q), F32)
            for pr in range(4):
                c0 = g * 512 + pr * 128
                xp = xdt[:, c0:c0 + 128].astype(BF16)
                dyp = dy[:, c0:c0 + 128]
                dypb = dyp.astype(BF16)
                halves = []
                for hh, keep in ((g * 8 + 2 * pr, lane < HP), (g * 8 + 2 * pr + 1, lane >= HP)):
                    lam = jnp.exp(jnp.where(causal, acx2[:, hh * 128:(hh + 1) * 128] - arow_scr[hh:hh + 1, :], NEG))
                    w = cbm * lam
                    dw = _dg(jnp.where(keep, dyp, 0.0).astype(BF16), xp, 1, 1)
                    dcb_acc = dcb_acc + dw * lam
                    t = dw * w
                    dacol = dacol + jnp.sum(t, axis=-1, keepdims=True) * (lane == hh).astype(F32)
                    darow_scr[hh:hh + 1, :] -= jnp.sum(t, axis=0, keepdims=True)
                    halves.append(_dg(w.astype(BF16), dypb, 0, 0))
                dxdt_scr[:, c0:c0 + 128] = (jnp.where(lane < HP, halves[0], halves[1])
                                            + dxdt_g[:, pr * 128:(pr + 1) * 128])
            dcbb = dcb_acc.astype(BF16)
            dcs.append(dcg + _dg(dcbb, bg, 1, 0))
            dbs.append(dbg + _dg(dcbb, cg, 0, 0))
        dacol = dacol + _xdot_l(t1_scr[...], et_m)
        ddecs = _xdot_l(t2_scr[...], et_m) * decs
        dacol = dacol - ddecs
        dalast = jnp.sum(ddecs, axis=0, keepdims=True)
        dcd = _rowdot(jnp.concatenate(dcdx, axis=1), et_m)
        dalast = dalast + dcd * jnp.exp(alast)
        dacol = dacol + _xdot_l(darow_scr[...], eye_ref[...], 0, 0)
        rowi = lax.broadcasted_iota(jnp.int32, (Q, 128), 0)
        dacol = dacol + jnp.where(rowi == Q - 1, dalast, 0.0)
        dadt = _xdot_r(triu_ref[...], dacol)
        dxdt = dxdt_scr[...]
        ddt = dadt * a + _xdot_l(dxdt * xs, et_m)
        dalog_ref[...] += jnp.sum(dadt * dt, axis=0, keepdims=True) * a
        dxs = dxs + dxdt * dt_x
        ddtr = ddt * _sigmoid(dtpre)
        ddtb_ref[...] += jnp.sum(ddtr, axis=0, keepdims=True)
        ddtr_ref[...] = ddtr.astype(BF16)
        dsilu = sg * (1.0 + conv * (1.0 - sg))
        ext2_scr[0:Q, 0:SSD_W] = dxs * dsilu[:, :SSD_W]
        for g in range(2):
            o1 = SSD_W + g * NS
            o2 = SSD_W + 2 * NS + g * NS
            ext2_scr[0:Q, o1:o1 + NS] = dbs[g] * dsilu[:, o1:o1 + NS]
            ext2_scr[0:Q, o2:o2 + NS] = dcs[g] * dsilu[:, o2:o2 + NS]
        dconv = ext2_scr[0:Q, :]
        dcb_ref[...] += jnp.sum(dconv, axis=0, keepdims=True)
        dxr = jnp.zeros((Q, XBC), F32)
        for k in range(4):
            dcw_ref[k:k + 1, :] += jnp.sum(dconv * ext_scr[pl.ds(5 + k, Q), :], axis=0, keepdims=True)
            dxr = dxr + cw_ref[k:k + 1, :] * ext2_scr[pl.ds(3 - k, Q), :]
        dxr_ref[...] = dxr.astype(BF16)
        ext2_scr[Q:Q + 8, :] = ext2_scr[0:8, :]

    def row(w, blk=0):
        return pl.BlockSpec((Q, w), lambda i: (NC - 1 - i, blk))

    def full(shape):
        return pl.BlockSpec(shape, lambda i: (0,) * len(shape))

    prev = pl.BlockSpec((Q, XBC), lambda i: (jnp.maximum(NC - 2 - i, 0), U_X // XBC))
    return pl.pallas_call(
        body, name=name, grid=(NC,),
        in_specs=[row(512, U_Z // 512), row(512, U_Z // 512 + 1), row(XBC, U_X // XBC), prev, row(128, U_DT // 128),
                  row(SSD_W),
                  pl.BlockSpec((1, 2, NS, 512), lambda i: (NC - 1 - i, 0, 0, 0)), row(SSD_W),
                  full((4, XBC)), full((1, XBC)), full((1, 128)), full((1, 128)), full((1, SSD_W)),
                  full((1, SSD_W)), full((128, SSD_W)), full((128, NH * 128)), full((SSD_W, 128)), full((Q, Q)),
                  full((Q, Q)), full((128, 128))],
        out_specs=[row(SSD_W), row(XBC), row(128), full((8, XBC)), full((1, XBC)), full((1, 128)), full((1, 128)),
                   full((1, 128)), full((1, SSD_W))],
        out_shape=[jax.ShapeDtypeStruct((L, SSD_W), BF16), jax.ShapeDtypeStruct((L, XBC), BF16),
                   jax.ShapeDtypeStruct((L, 128), BF16), jax.ShapeDtypeStruct((8, XBC), F32),
                   jax.ShapeDtypeStruct((1, XBC), F32), jax.ShapeDtypeStruct((1, 128), F32),
                   jax.ShapeDtypeStruct((1, 128), F32), jax.ShapeDtypeStruct((1, 128), F32),
                   jax.ShapeDtypeStruct((1, SSD_W), F32)],
        scratch_shapes=[pltpu.VMEM((2, NS, 512), F32), pltpu.VMEM((Q + 8, XBC), F32), pltpu.VMEM((Q + 8, XBC), F32),
                        pltpu.VMEM((128, Q), F32), pltpu.VMEM((Q, 128), F32), pltpu.VMEM((Q, SSD_W), F32),
                        pltpu.VMEM((128, Q), F32), pltpu.VMEM((Q, SSD_W), F32), pltpu.VMEM((Q, SSD_W), F32),
                        pltpu.VMEM((Q, SSD_W), F32), pltpu.VMEM((Q, SSD_W), F32)],
        compiler_params=_cp(("arbitrary",)))(u, u, u, u, u, ypre, st, dy, cw, cb, dtb, alog, dxp, nw,
                                             e, e2, et, tril, triu, eye)


def _my_pos():
    return lax.axis_index("x"), lax.axis_index("y"), lax.axis_index("c")


CHIP_REL = ((1, 0), (0, 1), (1, 1))
CHIP_XOR = (2, 1, 3)
BIG = ("w_in", "w_out", "w_mlp_up", "w_mlp_down")
NW = len(BIG)
AT = 256


def _chips(x, y):
    return [(1 - x if dx else x, 1 - y if dy else y) for dx, dy in CHIP_REL]


HBM_SPEC = pl.BlockSpec(memory_space=pltpu.HBM)
SEM_SPEC = pl.BlockSpec(memory_space=pltpu.SEMAPHORE)
EFFECT = pltpu.SideEffectType.DATAFLOW_SIDE_EFFECTING


def _hbm(t):
    return pltpu.with_memory_space_constraint(t, pltpu.HBM)


def _split_start(srcs, lands, after, copies, name):
    n = len(srcs)

    def body(*refs):
        src_refs, land_refs = refs[:n], refs[n:2 * n]
        send_sems, recv_sems = refs[2 * n + 1], refs[2 * n + 2]
        token = refs[-1]
        for w, k, src, dst, dev in copies(src_refs, land_refs):
            pltpu.make_async_remote_copy(src_ref=src, dst_ref=dst, send_sem=send_sems.at[3 * w + k],
                                         recv_sem=recv_sems.at[3 * w + k], device_id=dev, device_id_type=MESH).start()
        token[...] = jnp.zeros_like(token)

    outs = pl.pallas_call(
        body, name=name,
        out_shape=(pltpu.SemaphoreType.DMA((3 * n,)), pltpu.SemaphoreType.DMA((3 * n,)),
                   *[pltpu.HBM(t.shape, t.dtype) for t in srcs], *[pltpu.HBM(t.shape, t.dtype) for t in lands],
                   jax.ShapeDtypeStruct((8, 128), F32)),
        in_specs=[HBM_SPEC] * (2 * n) + [ANY],
        out_specs=(SEM_SPEC, SEM_SPEC, *([HBM_SPEC] * (2 * n)), VMEM_SPEC),
        input_output_aliases={i: 2 + i for i in range(2 * n)},
        compiler_params=pltpu.CompilerParams(has_side_effects=EFFECT))(
            *[_hbm(t) for t in srcs], *[_hbm(t) for t in lands], after)
    return outs[0], outs[1], list(outs[2:2 + n]), list(outs[2 + n:2 + 2 * n]), outs[-1]


def _split_wait(send_sems, recv_sems, srcs, lands, after, copies, name):
    n = len(srcs)

    def body(*refs):
        src_refs, land_refs = refs[:n], refs[n:2 * n]
        ssem, rsem = refs[2 * n], refs[2 * n + 1]
        for w, k, src, dst, dev in copies(src_refs, land_refs):
            cp = pltpu.make_async_remote_copy(src_ref=src, dst_ref=dst, send_sem=ssem.at[3 * w + k],
                                              recv_sem=rsem.at[3 * w + k], device_id=dev, device_id_type=MESH)
            cp.wait_send()
            cp.wait_recv()

    outs = pl.pallas_call(
        body, name=name,
        out_shape=tuple([pltpu.HBM(t.shape, t.dtype) for t in srcs] + [pltpu.HBM(t.shape, t.dtype) for t in lands]),
        in_specs=[HBM_SPEC] * (2 * n) + [SEM_SPEC, SEM_SPEC, ANY],
        out_specs=tuple([HBM_SPEC] * (2 * n)),
        input_output_aliases={i: i for i in range(2 * n)},
        compiler_params=pltpu.CompilerParams(has_side_effects=EFFECT))(*srcs, *lands, send_sems, recv_sems, after)
    return list(outs[:n]), list(outs[n:])


def _ag_copies(arrival):
    def copies(src_refs, land_refs):
        x, y, c = _my_pos()
        s = 2 * x + y
        chips = _chips(x, y)
        for w in range(len(src_refs)):
            hr = src_refs[w].shape[0] // 2
            mine = pl.ds(c * hr, hr)
            for k in range(3):
                slot = s ^ CHIP_XOR[k] if arrival else s
                yield w, k, src_refs[w].at[mine], land_refs[w].at[slot, mine], (*chips[k], c)
    return copies


def _ag_forward(lands, name):
    n = len(lands)

    def body(*refs):
        outs = refs[n:2 * n]
        send_sems, recv_sems = refs[2 * n:]
        x, y, c = _my_pos()
        s = 2 * x + y
        sib = (x, y, 1 - c)
        sends = []
        for w in range(n):
            hr = outs[w].shape[1] // 2
            for k in range(3):
                blk = outs[w].at[s ^ CHIP_XOR[k], pl.ds(c * hr, hr)]
                fw = pltpu.make_async_remote_copy(
                    src_ref=blk, dst_ref=blk, send_sem=send_sems.at[w, k], recv_sem=recv_sems.at[w, k],
                    device_id=sib, device_id_type=MESH)
                fw.start()
                sends.append(fw)
        for w in range(n):
            hr = outs[w].shape[1] // 2
            for k in range(3):
                blk = outs[w].at[s ^ CHIP_XOR[k], pl.ds((1 - c) * hr, hr)]
                pltpu.make_async_remote_copy(
                    src_ref=blk, dst_ref=blk, send_sem=send_sems.at[w, k], recv_sem=recv_sems.at[w, k],
                    device_id=sib, device_id_type=MESH).wait_recv()
        for cp in sends:
            cp.wait_send()

    return pl.pallas_call(
        body, name=name, in_specs=[ANY] * n, out_specs=[ANY] * n,
        out_shape=[jax.ShapeDtypeStruct(t.shape, t.dtype) for t in lands],
        input_output_aliases={w: w for w in range(n)},
        scratch_shapes=[pltpu.SemaphoreType.DMA((n, 3)), pltpu.SemaphoreType.DMA((n, 3))])(*lands)


def _rs_copies(src_refs, land_refs):
    x, y, c = _my_pos()
    s = 2 * x + y
    chips = _chips(x, y)
    for w in range(len(src_refs)):
        for k in range(3):
            yield w, k, src_refs[w].at[s ^ CHIP_XOR[k]], land_refs[w].at[k], (*chips[k], c)


def _place_own(shard, gathered, sidx, name):
    r, cc = shard.shape

    def body(s_ref, a_ref, g_ref, o_ref):
        o_ref[0] = a_ref[...]

    return pl.pallas_call(
        body, name=name,
        grid_spec=pltpu.PrefetchScalarGridSpec(
            num_scalar_prefetch=1, grid=(r // AT,),
            in_specs=[pl.BlockSpec((AT, cc), lambda i, s_ref: (i, 0)), ANY],
            out_specs=pl.BlockSpec((1, AT, cc), lambda i, s_ref: (s_ref[0], i, 0))),
        out_shape=jax.ShapeDtypeStruct(gathered.shape, gathered.dtype),
        input_output_aliases={2: 0}, compiler_params=_cp(("parallel",)))(sidx, shard, gathered)


def _rs_pair(dwb, name):
    n = len(dwb)

    def body(*refs):
        ins, outs = refs[:n], refs[n:2 * n]
        send_sems, recv_sems = refs[2 * n:]
        x, y, c = _my_pos()
        cps = []
        for w in range(n):
            hr = ins[w].shape[1] // 2
            cp = pltpu.make_async_remote_copy(
                src_ref=ins[w].at[:, pl.ds((1 - c) * hr, hr)], dst_ref=outs[w], send_sem=send_sems.at[w],
                recv_sem=recv_sems.at[w], device_id=(x, y, 1 - c), device_id_type=MESH)
            cp.start()
            cps.append(cp)
        for cp in cps:
            cp.wait()

    return pl.pallas_call(
        body, name=name, in_specs=[ANY] * n, out_specs=[ANY] * n,
        out_shape=[jax.ShapeDtypeStruct((4, t.shape[1] // 2, t.shape[2]), t.dtype) for t in dwb],
        scratch_shapes=[pltpu.SemaphoreType.DMA((n,)), pltpu.SemaphoreType.DMA((n,))])(*dwb)


def _rs_sib(q, name):
    n = len(q)

    def body(*refs):
        outs = refs[n:2 * n]
        send_sems, recv_sems = refs[2 * n:]
        x, y, c = _my_pos()
        cps = []
        for w in range(n):
            hr = outs[w].shape[0] // 2
            mine = pl.ds(c * hr, hr)
            cp = pltpu.make_async_remote_copy(
                src_ref=outs[w].at[mine], dst_ref=outs[w].at[mine], send_sem=send_sems.at[w],
                recv_sem=recv_sems.at[w], device_id=(x, y, 1 - c), device_id_type=MESH)
            cp.start()
            cps.append(cp)
        for w in range(n):
            hr = outs[w].shape[0] // 2
            other = outs[w].at[pl.ds((1 - c) * hr, hr)]
            pltpu.make_async_remote_copy(
                src_ref=other, dst_ref=other, send_sem=send_sems.at[w], recv_sem=recv_sems.at[w],
                device_id=(x, y, 1 - c), device_id_type=MESH).wait_recv()
        for cp in cps:
            cp.wait_send()

    return pl.pallas_call(
        body, name=name, in_specs=[ANY] * n, out_specs=[ANY] * n,
        out_shape=[jax.ShapeDtypeStruct(t.shape, t.dtype) for t in q],
        input_output_aliases={w: w for w in range(n)},
        scratch_shapes=[pltpu.SemaphoreType.DMA((n,)), pltpu.SemaphoreType.DMA((n,))])(*q)


def _rs_rows(hr):
    return 2 * AT if hr % (2 * AT) == 0 else AT


def _rs_add2(dw, got, scidx, name):
    _, r, cc = dw.shape
    hr = r // 2
    at = _rs_rows(hr)
    nb = hr // at

    def body(s_ref, a_ref, b_ref, o_ref, ob_ref):
        acc = a_ref[...] + b_ref[...].astype(F32)
        ob_ref[...] = acc.astype(BF16)

        @pl.when(pl.program_id(1) == s_ref[0])
        def _():
            o_ref[...] = acc[0]

    blk = pl.BlockSpec((1, at, cc), lambda i, sh, s_ref: (sh, i, 0))
    return pl.pallas_call(
        body, name=name,
        grid_spec=pltpu.PrefetchScalarGridSpec(
            num_scalar_prefetch=1, grid=(nb, 4),
            in_specs=[pl.BlockSpec((1, at, cc), lambda i, sh, s_ref: (sh, s_ref[1] * nb + i, 0)), blk],
            out_specs=[pl.BlockSpec((at, cc), lambda i, sh, s_ref: (i, 0)), blk]),
        out_shape=[jax.ShapeDtypeStruct((hr, cc), F32), jax.ShapeDtypeStruct((4, hr, cc), BF16)],
        compiler_params=_cp(("parallel", "arbitrary")))(scidx, dw, got)


def _rs_add4(p, got, scidx, name):
    hr, cc = p.shape
    at = _rs_rows(hr)
    nb = hr // at

    def body(s_ref, p_ref, g0_ref, g1_ref, g2_ref, o_ref):
        acc = p_ref[...] + g0_ref[0].astype(F32)
        acc = acc + g1_ref[0].astype(F32)
        o_ref[...] = acc + g2_ref[0].astype(F32)

    def gk(k):
        return pl.BlockSpec((1, at, cc), lambda i, s_ref: (k, i, 0))

    return pl.pallas_call(
        body, name=name,
        grid_spec=pltpu.PrefetchScalarGridSpec(
            num_scalar_prefetch=1, grid=(nb,),
            in_specs=[pl.BlockSpec((at, cc), lambda i, s_ref: (i, 0)), gk(0), gk(1), gk(2)],
            out_specs=pl.BlockSpec((at, cc), lambda i, s_ref: (s_ref[1] * nb + i, 0))),
        out_shape=jax.ShapeDtypeStruct((2 * hr, cc), F32),
        compiler_params=_cp(("parallel",)))(scidx, p, got, got, got)


def _rs_begin(dws, dwbs, after, tag=""):
    x, y, c = _my_pos()
    scidx = jnp.stack([2 * x + y, c]).astype(jnp.int32)
    got = _rs_pair(dwbs, "rs_pair" + tag)
    pairs = [_rs_add2(dws[w], got[w], scidx, "rs_add2") for w in range(len(dws))]
    pb = [p[1] for p in pairs]
    lands = [lax.empty((3,) + t.shape[1:], BF16) for t in pb]
    ssem, rsem, pb, lands, token = _split_start(pb, lands, after, _rs_copies, "rs_chip_start" + tag)
    return ([p[0] for p in pairs], ssem, rsem, pb, lands), token


def _rs_end(state, after, tag=""):
    x, y, c = _my_pos()
    scidx = jnp.stack([2 * x + y, c]).astype(jnp.int32)
    p, ssem, rsem, pb, lands = state
    _, recv = _split_wait(ssem, rsem, pb, lands, after, _rs_copies, "rs_chip_wait" + tag)
    q = [_rs_add4(p[w], recv[w], scidx, "rs_add4") for w in range(len(p))]
    return _rs_sib(q, "rs_sib" + tag)


def _allreduce_small(buf, name):
    rows = buf.shape[0]

    def body(src_ref, out_ref, gat_ref, send_sems, recv_sems):
        x, y, c = _my_pos()
        me = 4 * x + 2 * y + c
        gat_ref[me] = src_ref[...]
        cps = []
        for r in range(1, N_DEV):
            tx = 1 - x if (r >> 2) & 1 else x
            ty = 1 - y if (r >> 1) & 1 else y
            tc = 1 - c if r & 1 else c
            cps.append(pltpu.make_async_remote_copy(
                src_ref=src_ref, dst_ref=gat_ref.at[me], send_sem=send_sems.at[r - 1], recv_sem=recv_sems.at[r - 1],
                device_id=(tx, ty, tc), device_id_type=MESH))
        for cp in cps:
            cp.start()
        for cp in cps:
            cp.wait()
        acc = gat_ref[0]
        for k in range(1, N_DEV):
            acc = acc + gat_ref[k]
        out_ref[...] = acc

    return pl.pallas_call(
        body, name=name, in_specs=[VMEM_SPEC], out_specs=VMEM_SPEC, out_shape=jax.ShapeDtypeStruct((rows, 128), F32),
        scratch_shapes=[pltpu.VMEM((N_DEV, rows, 128), F32), pltpu.SemaphoreType.DMA((N_DEV - 1,)),
                        pltpu.SemaphoreType.DMA((N_DEV - 1,))],
        compiler_params=_cp())(buf)


SMALL = (("norm_mix_w", (D,)), ("ssd_conv_w", (4, XBC)), ("ssd_conv_b", (XBC,)), ("ssd_dt_bias", (NH,)),
         ("ssd_a_log", (NH,)), ("ssd_d", (NH,)), ("ssd_norm_w", (SSD_W,)), ("q_norm_w", (HD,)),
         ("k_norm_w", (HD,)), ("attn_sinks", (NQH,)), ("cm_dw_w", (CMK, CMC)), ("cm_dw_b", (CMC,)),
         ("cm_ln_w", (CMC,)), ("cm_ln_b", (CMC,)), ("norm_mlp_w", (D,)))
SHARDED_SMALL = ("ssd_conv_w", "cm_dw_w")


def _seg_len(shape):
    n = 1
    for d in shape:
        n *= d
    return -(-n // 128) * 128


def _pack_small(vals, names):
    parts = []
    for name, shape in SMALL:
        if name not in names:
            continue
        v = vals[name].reshape(DEPTH, -1)
        pad = _seg_len(shape) - v.shape[1]
        parts.append(jnp.pad(v, ((0, 0), (0, pad))))
    flat = jnp.concatenate(parts, axis=1)
    return flat.reshape(-1, 128)


def _unpack_small(buf, names):
    flat = buf.reshape(DEPTH, -1)
    out = {}
    off = 0
    for name, shape in SMALL:
        if name not in names:
            continue
        n = 1
        for d in shape:
            n *= d
        out[name] = flat[:, off:off + n].reshape((DEPTH,) + shape)
        off += _seg_len(shape)
    return out


SW = N_IN // 4
SWP = 1152
ORIG = (("z", 0, 1024), ("x", 1024, 2560), ("dt", 2560, 2576), ("q", 2576, 3088), ("k", 3088, 3216),
        ("v", 3216, 3344), ("a", 3344, 3856), ("g", 3856, 4368))


def _orig_cols(g_in, lo, hi):
    out = []
    for s in range(4):
        a, b = max(lo, s * SW), min(hi, (s + 1) * SW)
        if a < b:
            out.append(g_in[s][:, a - s * SW:b - s * SW])
    return out


def _shard_major(parts):
    cols = []
    for s in range(4):
        for name, g0, g1 in ORIG:
            a, b = max(g0, s * SW), min(g1, (s + 1) * SW)
            if a < b:
                cols.append(parts[name][:, a - g0:b - g0])
        cols.append(jnp.zeros((L, SWP - SW), BF16))
    return jnp.concatenate(cols, axis=1)


def _rope_tables():
    inv = 10000.0 ** (-jnp.arange(0, HD, 2, dtype=F32) / HD)
    ang = jnp.arange(L, dtype=F32)[:, None] * inv[None, :]
    return jnp.cos(ang), jnp.sin(ang)


def _swa_tables():
    cos, sin = _rope_tables()
    return jnp.tile(cos, (1, 4)), jnp.tile(jnp.concatenate([-sin, sin], axis=1), (1, 2))


def _swa_weights(w):
    return jnp.tile(w, 2)[None], jnp.tile(jnp.concatenate([w[HH:], w[:HH]]), 2)[None]


def _swa_sinks(s):
    s2 = s.reshape(NPAIR, 2)
    return (jnp.broadcast_to(s2[:, 0][:, None, None], (NPAIR, 1, 128)),
            jnp.broadcast_to(s2[:, 1][:, None, None], (NPAIR, 1, 128)))


def _swa_blockdiag():
    i = jnp.arange(128)
    return (i[:, None] // HD == i[None, :] // HD).astype(BF16)


def _pad128(v):
    return jnp.pad(v, (0, 128 - v.shape[0]))[None, :]


def kernel(x, norm_mix_w, w_in, ssd_conv_w, ssd_conv_b, ssd_dt_bias, ssd_a_log, ssd_d, ssd_norm_w, q_norm_w, k_norm_w, attn_sinks, cm_dw_w, cm_dw_b, cm_ln_w, cm_ln_b, w_out, norm_mlp_w, w_mlp_up, w_mlp_down, loss_target, m_norm_mix_w, m_w_in, m_ssd_conv_w, m_ssd_conv_b, m_ssd_dt_bias, m_ssd_a_log, m_ssd_d, m_ssd_norm_w, m_q_norm_w, m_k_norm_w, m_attn_sinks, m_cm_dw_w, m_cm_dw_b, m_cm_ln_w, m_cm_ln_b, m_w_out, m_norm_mlp_w, m_w_mlp_up, m_w_mlp_down, v_norm_mix_w, v_w_in, v_ssd_conv_w, v_ssd_conv_b, v_ssd_dt_bias, v_ssd_a_log, v_ssd_d, v_ssd_norm_w, v_q_norm_w, v_k_norm_w, v_attn_sinks, v_cm_dw_w, v_cm_dw_b, v_cm_ln_w, v_cm_ln_b, v_w_out, v_norm_mlp_w, v_w_mlp_up, v_w_mlp_down):
    px, py, pc = _my_pos()
    shard = 2 * px + py
    sidx = jnp.reshape(shard, (1,)).astype(jnp.int32)
    consts = _ssd_consts()
    cos4, sin4 = _swa_tables()
    bd = _swa_blockdiag()

    big_w = (w_in, w_out, w_mlp_up, w_mlp_down)

    def own_shard(w, l):
        t = big_w[w][l]
        if w == 0:
            t = jnp.pad(t, ((0, 0), (0, SWP - SW)))
        return t.astype(BF16)

    def gather_start(l, sel, after, tag=""):
        own = [own_shard(w, l) for w in sel]
        lands = [lax.empty((4,) + t.shape, BF16) for t in own]
        return _split_start(own, lands, after, _ag_copies(False), "ag_start" + tag)

    def gather_finish(in_flight, after, tag=""):
        ssem, rsem, own, lands, _ = in_flight
        own, lands = _split_wait(ssem, rsem, own, lands, after, _ag_copies(True), "ag_wait" + tag)
        lands = _ag_forward(lands, "ag_forward" + tag)
        return [_place_own(o, g, sidx, "ag_place") for o, g in zip(own, lands)]

    zero_tile = jnp.zeros((8, 128), F32)
    first_in = gather_start(0, [0], zero_tile, "_in0")
    zc = jnp.zeros((DEPTH, 4, XBC), F32)
    zc = lax.dynamic_update_slice_in_dim(zc, ssd_conv_w, shard * (XBC // 4), axis=2)
    zd = jnp.zeros((DEPTH, CMK, CMC), F32)
    zd = lax.dynamic_update_slice_in_dim(zd, cm_dw_w, shard * (CMC // 4), axis=2)
    half = jnp.where(pc == 0, 1.0, 0.0).astype(F32)
    gw_packed = _allreduce_small(_pack_small({"ssd_conv_w": zc * half, "cm_dw_w": zd * half}, SHARDED_SMALL)
                                 + first_in[4][0:1, 0:1], "ag_small")
    first_rest = gather_start(0, [1, 2, 3], gw_packed, "_rest0")
    gw = _unpack_small(gw_packed, SHARDED_SMALL)
    conv_w_full, dw_w_full = gw["ssd_conv_w"], gw["cm_dw_w"]

    xcur = x[0]
    saved = []
    in_flight = None
    for l in range(DEPTH):
        if l == 0:
            (g_in,) = gather_finish(first_in, first_rest[4], "_in0")
            nxt_after = g_in
        else:
            g_in, g_out, g_up, g_dn = gather_finish(in_flight, xcur)
            nxt_after = g_dn
        nmw = norm_mix_w[l][None]
        if l + 1 < DEPTH:
            in_flight = gather_start(l + 1, [0, 1, 2, 3], nxt_after)
            nmw = nmw + in_flight[4][0:1, 0:1]
        grp = dict((n, (a, b)) for n, a, b in ORIG)
        w_perm = jnp.concatenate(
            _orig_cols(g_in, *grp["x"]) + _orig_cols(g_in, *grp["z"]) + _orig_cols(g_in, grp["a"][0], grp["g"][1])
            + _orig_cols(g_in, grp["q"][0], grp["v"][1]) + _orig_cols(g_in, *grp["dt"])
            + [jnp.zeros((D, 128 - NH), BF16)], axis=1)
        h, h_t = _rms_fwd(xcur, nmw, "rms_mix_fwd")
        u = _mm(h, w_perm, "nn", "in_proj", tn=640)
        alog = _pad128(ssd_a_log[l])
        dtb = _pad128(ssd_dt_bias[l])
        dxp = jnp.repeat(ssd_d[l], HP)[None, :]
        ssd_p = (conv_w_full[l], ssd_conv_b[l][None], dtb, alog, dxp, ssd_norm_w[l][None])
        ya, ypre, st = _ssd_fwd(u, *ssd_p, consts, "ssd_fwd")
        swa_p = (cos4, sin4, *_swa_weights(q_norm_w[l]), *_swa_weights(k_norm_w[l]), *_swa_sinks(attn_sinks[l]), bd)
        yb = _swa_fwd(u, *swa_p, "attn_fwd")
        cconv = _conf_conv_fwd(u, dw_w_full[l], cm_dw_b[l][None], "conf_conv_fwd")
        conf_p = (cconv, cm_ln_w[l][None], cm_ln_b[l][None])
        yc = _conf_ln_fwd(*conf_p, "conf_ln_fwd")
        if l == 0:
            g_out, g_up, g_dn = gather_finish(first_rest, yc, "_rest0")
        g_out = g_out.reshape(2 * D, D)
        g_dn = g_dn.reshape(DFF, D)
        ycat = jnp.concatenate([ya, yb, yc], axis=1)
        x1 = _mm(ycat, g_out, "nn", "out_proj", add=xcur)
        hm, hm_t = _rms_fwd(x1, norm_mlp_w[l][None], "rms_mlp_fwd")
        r_up = _mm_up(hm, g_up, "mlp_up")
        x2 = _mm(r_up, g_dn, "nn", "mlp_down", add=x1)
        saved.append(dict(x=xcur, h_t=h_t, u=u, ypre=ypre, st=st, swa_p=swa_p, conf_p=conf_p, ycat=ycat, x1=x1,
                          hm_t=hm_t, r_up=r_up, ssd_p=ssd_p, g_in=g_in, g_out=g_out, g_up=g_up, g_dn=g_dn))
        xcur = x2

    lsum, dx, dxb = _loss_bwd(xcur, loss_target[0], "loss")

    loc = locals()
    names = ["norm_mix_w", "w_in", "ssd_conv_w", "ssd_conv_b", "ssd_dt_bias", "ssd_a_log", "ssd_d", "ssd_norm_w",
             "q_norm_w", "k_norm_w", "attn_sinks", "cm_dw_w", "cm_dw_b", "cm_ln_w", "cm_ln_b", "w_out", "norm_mlp_w",
             "w_mlp_up", "w_mlp_down"]
    weights = {n: loc[n] for n in names}
    moms = {n: loc["m_" + n] for n in names}
    vars_ = {n: loc["v_" + n] for n in names}
    big_out = {n: None for n in BIG}
    win_grads = [None] * DEPTH

    def finish_layer(layer, which, shard_grads):
        for n, g in zip(which, shard_grads):
            if n == "w_in":
                win_grads[layer] = g
            else:
                big_out[n] = _adamw_layer(weights[n], g, moms[n], vars_[n], layer, big_out[n], "adamw_" + n)

    pending = None
    last_mlp = None
    gsm = {name: [] for name, _ in SMALL}
    for l in reversed(range(DEPTH)):
        sv = saved[l]
        da = _mm(dxb, sv["g_dn"], "nt", "mlp_down_dx", relu2_of=sv["r_up"])
        dwdn, dwdn_b = _mm_dw(sv["r_up"], dxb, "mlp_down_dw")
        dwup, dwup_b = _mm_dw(sv["hm_t"], da, "mlp_up_dw", col_shards=True, a_is_t=True)
        dhm = _mm_cs_nt(da, sv["g_up"], "mlp_up_dx")
        nlw = norm_mlp_w[l][None]
        if l == 0:
            last_mlp, token = _rs_begin([dwup, dwdn.reshape(4, D, D)], [dwup_b, dwdn_b.reshape(4, D, D)], zero_tile,
                                        "_mlp0")
            nlw = nlw + token[0:1, 0:1]
        dx1, dx1b, dnw = _rms_bwd(sv["x1"], nlw, dhm, dx, "rms_mlp_bwd")
        gsm["norm_mlp_w"].append(dnw[0])
        dy = _mm(dx1b, sv["g_out"], "nt", "out_proj_dx")
        dwout, dwout_b = _mm_dw(sv["ycat"], dx1b, "out_proj_dw")
        dcc, dwb, dlw, dlb = _conf_ln_bwd(*sv["conf_p"], dy, "conf_ln_bwd")
        da_c, dg_c, dww = _conf_conv_bwd(sv["u"], dcc, dw_w_full[l], "conf_conv_bwd")
        gsm["cm_dw_w"].append(dww[:CMK])
        gsm["cm_dw_b"].append(dwb[0])
        gsm["cm_ln_w"].append(dlw[0])
        gsm["cm_ln_b"].append(dlb[0])
        dq, dk, dv, dqw, dkw, dse, dso = _swa_bwd(sv["u"], dy, *sv["swa_p"], "attn_bwd")
        gsm["q_norm_w"].append(dqw[0, :HD] + dqw[0, HD:])
        gsm["k_norm_w"].append(dkw[0, :HD] + dkw[0, HD:])
        gsm["attn_sinks"].append(jnp.stack([dse[:, 0, 0], dso[:, 0, 0]], axis=1).reshape(NQH))
        (dz, dxr, ddtr, dcw, dcb, ddtb, dalog, ddd, dnsw) = _ssd_bwd(
            sv["u"], sv["ypre"], sv["st"], dy, *sv["ssd_p"], consts, "ssd_bwd")
        gsm["ssd_conv_w"].append(dcw[:4])
        gsm["ssd_conv_b"].append(dcb[0])
        gsm["ssd_dt_bias"].append(ddtb[0, :NH])
        gsm["ssd_a_log"].append(dalog[0, :NH])
        gsm["ssd_d"].append(ddd[0, :NH])
        gsm["ssd_norm_w"].append(dnsw[0])
        du = _shard_major(dict(z=dz, x=dxr, dt=ddtr[:, :NH], q=dq, k=dk, v=dv, a=da_c, g=dg_c))
        dwin, dwin_b = _mm_dw(sv["h_t"], du, "in_dw", col_shards=True, tn=SWP // 3, a_is_t=True)
        if l == 0:
            state, token = _rs_begin([dwin, dwout.reshape(4, D // 2, D)], [dwin_b, dwout_b.reshape(4, D // 2, D)],
                                     zero_tile, "_io0")
        else:
            state, token = _rs_begin(
                [dwin, dwout.reshape(4, D // 2, D), dwup, dwdn.reshape(4, D, D)],
                [dwin_b, dwout_b.reshape(4, D // 2, D), dwup_b, dwdn_b.reshape(4, D, D)], zero_tile)
        dh = _mm_cs_nt(du, sv["g_in"], "in_dx")
        dx, dxb, dnm = _rms_bwd(sv["x"], norm_mix_w[l][None] + token[0:1, 0:1], dh, dx1, "rms_mix_bwd")
        gsm["norm_mix_w"].append(dnm[0])
        if pending is not None:
            finish_layer(l + 1, BIG, _rs_end(pending, dx))
        pending = state

    finish_layer(0, BIG[2:], _rs_end(last_mlp, dx, "_mlp0"))
    gsm = {k: jnp.stack(v[::-1]) for k, v in gsm.items()}
    packed = _pack_small(gsm, [n for n, _ in SMALL])
    packed = jnp.concatenate([packed, lsum], axis=0)
    red = _allreduce_small(packed, "ar_small")
    finish_layer(0, BIG[:2], _rs_end(pending, red, "_io0"))
    loss = 0.5 * red[-8, 0] / D
    gsm = _unpack_small(red[:-8], [n for n, _ in SMALL])
    gsm["ssd_conv_w"] = lax.dynamic_slice_in_dim(gsm["ssd_conv_w"], shard * (XBC // 4), XBC // 4, axis=2)
    gsm["cm_dw_w"] = lax.dynamic_slice_in_dim(gsm["cm_dw_w"], shard * (CMC // 4), CMC // 4, axis=2)
    grads = dict(gsm)
    delta, new_m, new_v = {}, {}, {}
    for n in BIG[1:]:
        grads[n], delta[n], new_m[n], new_v[n] = big_out[n]
    to_lead = lambda t: jnp.transpose(t, (2, 0, 1))
    g_lead = jnp.stack([jnp.transpose(g[:, :SW]) for g in win_grads], axis=1)
    d_lead, m_lead, v_lead = _adamw_lead(to_lead(w_in), g_lead, to_lead(m_w_in), to_lead(v_w_in), "adamw_w_in")
    from_lead = lambda t: jnp.transpose(t, (1, 2, 0))
    grads["w_in"], delta["w_in"] = from_lead(g_lead), from_lead(d_lead)
    new_m["w_in"], new_v["w_in"] = from_lead(m_lead), from_lead(v_lead)

    packed_names = [n for n, _ in SMALL if n not in SHARDED_SMALL]
    pw = _pack_small(weights, packed_names)
    pg = _pack_small(grads, packed_names)
    pm = _pack_small(moms, packed_names)
    pv = _pack_small(vars_, packed_names)
    pd, pmn, pvn = _adamw(pw, pg, pm, pv, "adamw_small")
    for dst, buf in ((delta, pd), (new_m, pmn), (new_v, pvn)):
        dst.update(_unpack_small(buf, packed_names))
    for n in SHARDED_SMALL:
        shp = weights[n].shape
        flat = lambda t: t.reshape(-1, shp[-1])
        d_, m_, v_ = _adamw(flat(weights[n]), flat(grads[n]), flat(moms[n]), flat(vars_[n]), "adamw_" + n)
        delta[n], new_m[n], new_v[n] = d_.reshape(shp), m_.reshape(shp), v_.reshape(shp)

    return (loss, dx[None], *[grads[n] for n in names], *[delta[n] for n in names],
            *[new_m[n] for n in names], *[new_v[n] for n in names])
```

```python
import functools
import math

import jax
import jax.numpy as jnp
from jax import lax
from jax.experimental import pallas as pl
from jax.experimental.pallas import tpu as pltpu

F32 = jnp.float32
BF16 = jnp.bfloat16
MESH = pl.DeviceIdType.MESH
ANY = pl.BlockSpec(memory_space=pl.ANY)
VMEM_SPEC = pl.BlockSpec(memory_space=pltpu.VMEM)

D = 1024
L = 2048
DEPTH = 4
SSD_W = 1024
XBC = 1536
NH = 16
HP = 64
NS = 128
Q = 128
NC = L // Q
ATT_W = 512
NQH = 8
NKV = 2
HD = 64
HH = HD // 2
CMC = 512
CMK = 31
DFF = 4096
N_IN = 4368
N_PAD = 4480
RMS_EPS = 1e-6
LN_EPS = 1e-5
NEG = -1e30
LR, B1, B2, EPS_A, WD, STEP = 0.001, 0.9, 0.999, 1e-8, 0.01, 10
VMEM_LIMIT = 56 * 1024 * 1024
N_DEV = 8


def _cp(sem=None):
    kw = dict(vmem_limit_bytes=VMEM_LIMIT)
    if sem is not None:
        kw["dimension_semantics"] = sem
    return pltpu.CompilerParams(**kw)


def _dg(a, b, ca, cb):
    return lax.dot_general(a, b, (((ca,), (cb,)), ((), ())), preferred_element_type=F32)


def _split3(x):
    hi = x.astype(BF16)
    r = x - hi.astype(F32)
    mid = r.astype(BF16)
    lo = (r - mid.astype(F32)).astype(BF16)
    return hi, mid, lo


def _xdot_l(x, m, ca=1, cb=0):
    hi, mid, lo = _split3(x)
    return _dg(hi, m, ca, cb) + _dg(mid, m, ca, cb) + _dg(lo, m, ca, cb)


def _xdot_r(m, x, ca=1, cb=0):
    hi, mid, lo = _split3(x)
    return _dg(m, hi, ca, cb) + _dg(m, mid, ca, cb) + _dg(m, lo, ca, cb)


def _rowdot(v, m):
    return _xdot_l(jnp.broadcast_to(v, (8, v.shape[1])), m)[0:1]


def _sigmoid(x):
    return 1.0 / (1.0 + jnp.exp(-x))


def _softplus(x):
    e = jnp.exp(-jnp.abs(x))
    u = 1.0 + e
    l1p = jnp.where(u == 1.0, e, jnp.log(u) * (e / jnp.where(u == 1.0, 1.0, u - 1.0)))
    return jnp.maximum(x, 0.0) + l1p


def _tm(k):
    return L if k <= D else L // 2


def _mm(a, b, mode, name, add=None, relu2_of=None, tn=512):
    m, k = a.shape
    tm = min(m, _tm(k))
    a_spec = pl.BlockSpec((tm, k), lambda i, j: (i, 0))
    if mode == "nn":
        n = b.shape[1]
        b_spec = pl.BlockSpec((k, tn), lambda i, j: (0, j))
        cb = 0
    else:
        n = b.shape[0]
        b_spec = pl.BlockSpec((tn, k), lambda i, j: (j, 0))
        cb = 1
    assert m % tm == 0 and n % tn == 0, (m, n, tm, tn)
    o_spec = pl.BlockSpec((tm, tn), lambda i, j: (i, j))
    out_dtype = F32
    if relu2_of is not None:
        def body(a_ref, b_ref, c_ref, o_ref):
            o_ref[...] = (_dg(a_ref[...], b_ref[...], 1, cb) * (2.0 * jnp.sqrt(c_ref[...].astype(F32)))).astype(BF16)
        ins, specs, out_dtype = (a, b, relu2_of), [a_spec, b_spec, o_spec], BF16
    elif add is None:
        def body(a_ref, b_ref, o_ref):
            o_ref[...] = _dg(a_ref[...], b_ref[...], 1, cb)
        ins, specs = (a, b), [a_spec, b_spec]
    else:
        def body(a_ref, b_ref, c_ref, o_ref):
            o_ref[...] = _dg(a_ref[...], b_ref[...], 1, cb) + c_ref[...]
        ins, specs = (a, b, add), [a_spec, b_spec, o_spec]
    return pl.pallas_call(
        body, name=name, grid=(m // tm, n // tn), in_specs=specs, out_specs=o_spec,
        out_shape=jax.ShapeDtypeStruct((m, n), out_dtype), compiler_params=_cp(("parallel", "parallel")))(*ins)


def _mm_up(a, b, name, tn=512):
    m = a.shape[0]
    cs = DFF // 4
    per = cs // tn
    tm = min(m, _tm(D))

    def body(a_ref, b_ref, r_ref):
        r = jnp.maximum(_dg(a_ref[...], b_ref[0], 1, 0), 0.0)
        r_ref[...] = (r * r).astype(BF16)

    return pl.pallas_call(
        body, name=name, grid=(m // tm, DFF // tn),
        in_specs=[pl.BlockSpec((tm, D), lambda i, j: (i, 0)),
                  pl.BlockSpec((1, D, tn), lambda i, j: (j // per, 0, j % per))],
        out_specs=pl.BlockSpec((tm, tn), lambda i, j: (i, j)),
        out_shape=jax.ShapeDtypeStruct((m, DFF), BF16), compiler_params=_cp(("parallel", "parallel")))(a, b)


def _mm_cs_nt(a, b, name, tn=512):
    m = a.shape[0]
    _, n, cs = b.shape
    tm = min(m, _tm(4 * cs))

    def body(a_ref, b_ref, o_ref):
        acc = _dg(a_ref[:, 0:cs], b_ref[0], 1, 1)
        for s in range(1, 4):
            acc = acc + _dg(a_ref[:, s * cs:(s + 1) * cs], b_ref[s], 1, 1)
        o_ref[...] = acc

    return pl.pallas_call(
        body, name=name, grid=(m // tm, n // tn),
        in_specs=[pl.BlockSpec((tm, 4 * cs), lambda i, j: (i, 0)), pl.BlockSpec((4, tn, cs), lambda i, j: (0, j, 0))],
        out_specs=pl.BlockSpec((tm, tn), lambda i, j: (i, j)),
        out_shape=jax.ShapeDtypeStruct((m, n), F32), compiler_params=_cp(("parallel", "parallel")))(a, b)


def _mm_dw(a, b, name, col_shards=False, tn=512):
    k, m = a.shape
    n = b.shape[1]
    tm = min(m, D)
    a_spec = pl.BlockSpec((k, tm), lambda i, j: (0, i))
    b_spec = pl.BlockSpec((k, tn), lambda i, j: (0, j))
    if col_shards:
        per = (n // 4) // tn
        o_spec = pl.BlockSpec((1, tm, tn), lambda i, j: (j // per, i, j % per))
        shape = (4, m, n // 4)
    else:
        o_spec = pl.BlockSpec((tm, tn), lambda i, j: (i, j))
        shape = (m, n)

    def body(a_ref, b_ref, o_ref, ob_ref):
        acc = _dg(a_ref[...], b_ref[...], 0, 0).reshape(o_ref.shape)
        o_ref[...] = acc
        ob_ref[...] = acc.astype(BF16)

    return pl.pallas_call(
        body, name=name, grid=(m // tm, n // tn), in_specs=[a_spec, b_spec], out_specs=[o_spec, o_spec],
        out_shape=[jax.ShapeDtypeStruct(shape, F32), jax.ShapeDtypeStruct(shape, BF16)],
        compiler_params=_cp(("parallel", "parallel")))(a, b)


TR = 256


def _rms_fwd(x, w, name):
    def body(x_ref, w_ref, o_ref):
        xv = x_ref[...]
        r = lax.rsqrt(jnp.mean(xv * xv, axis=-1, keepdims=True) + RMS_EPS)
        o_ref[...] = (xv * r * w_ref[...]).astype(BF16)

    return pl.pallas_call(
        body, name=name, grid=(L // TR,),
        in_specs=[pl.BlockSpec((TR, D), lambda i: (i, 0)), pl.BlockSpec((1, D), lambda i: (0, 0))],
        out_specs=pl.BlockSpec((TR, D), lambda i: (i, 0)),
        out_shape=jax.ShapeDtypeStruct((L, D), BF16), compiler_params=_cp(("parallel",)))(x, w)


def _rms_bwd(x, w, dh, dres, name):
    def body(x_ref, w_ref, dh_ref, dr_ref, dx_ref, dxb_ref, dw_ref):
        xv = x_ref[...]
        r = lax.rsqrt(jnp.mean(xv * xv, axis=-1, keepdims=True) + RMS_EPS)
        n = xv * r
        dhv = dh_ref[...]
        g = dhv * w_ref[...]
        dx = dr_ref[...] + r * (g - n * jnp.mean(g * n, axis=-1, keepdims=True))
        dx_ref[...] = dx
        dxb_ref[...] = dx.astype(BF16)

        @pl.when(pl.program_id(0) == 0)
        def _():
            dw_ref[...] = jnp.zeros_like(dw_ref)
        dw_ref[...] += jnp.sum(dhv * n, axis=0, keepdims=True)

    row = pl.BlockSpec((TR, D), lambda i: (i, 0))
    vec = pl.BlockSpec((1, D), lambda i: (0, 0))
    return pl.pallas_call(
        body, name=name, grid=(L // TR,), in_specs=[row, vec, row, row], out_specs=[row, row, vec],
        out_shape=[jax.ShapeDtypeStruct((L, D), F32), jax.ShapeDtypeStruct((L, D), BF16),
                   jax.ShapeDtypeStruct((1, D), F32)],
        compiler_params=_cp(("arbitrary",)))(x, w, dh, dres)


def _adamw_lead(w, g, m, v, name):
    lead, a, b = w.shape
    tr = max(t for t in range(1, lead + 1) if lead % t == 0 and t * a * b * 4 <= 2 * 1024 * 1024)
    c1 = 1.0 / (1.0 - B1 ** STEP)
    c2 = 1.0 / (1.0 - B2 ** STEP)

    def body(w_ref, g_ref, m_ref, v_ref, d_ref, mo_ref, vo_ref):
        gv = g_ref[...]
        mn = B1 * m_ref[...] + (1.0 - B1) * gv
        vn = B2 * v_ref[...] + (1.0 - B2) * (gv * gv)
        mo_ref[...] = mn
        vo_ref[...] = vn
        d_ref[...] = -LR * ((mn * c1) / (jnp.sqrt(vn * c2) + EPS_A) + WD * w_ref[...])

    blk = pl.BlockSpec((tr, a, b), lambda i: (i, 0, 0))
    shp = jax.ShapeDtypeStruct(w.shape, F32)
    return pl.pallas_call(body, name=name, grid=(lead // tr,), in_specs=[blk] * 4, out_specs=[blk] * 3,
                          out_shape=[shp] * 3, compiler_params=_cp(("parallel",)))(w, g, m, v)


def _adamw_layer(w, g, m, v, layer, prev, name):
    _, rows, cols = w.shape
    tr = 256 if cols * 256 * 4 <= 2 * 1024 * 1024 else 128
    c1 = 1.0 / (1.0 - B1 ** STEP)
    c2 = 1.0 / (1.0 - B2 ** STEP)
    n_prev = 0 if prev is None else 4

    def body(*refs):
        w_ref, g_ref, m_ref, v_ref = refs[:4]
        go_ref, d_ref, mo_ref, vo_ref = refs[4 + n_prev:]
        gv = g_ref[...]
        mn = B1 * m_ref[0] + (1.0 - B1) * gv
        vn = B2 * v_ref[0] + (1.0 - B2) * (gv * gv)
        go_ref[0] = gv
        mo_ref[0] = mn
        vo_ref[0] = vn
        d_ref[0] = -LR * ((mn * c1) / (jnp.sqrt(vn * c2) + EPS_A) + WD * w_ref[0])

    lay = pl.BlockSpec((1, tr, cols), lambda i: (layer, i, 0))
    shp = jax.ShapeDtypeStruct(w.shape, F32)
    return pl.pallas_call(
        body, name=name, grid=(rows // tr,),
        in_specs=[lay, pl.BlockSpec((tr, cols), lambda i: (i, 0)), lay, lay] + [ANY] * n_prev,
        out_specs=[lay] * 4, out_shape=[shp] * 4,
        input_output_aliases={4 + i: i for i in range(n_prev)},
        compiler_params=_cp(("parallel",)))(w, g, m, v, *(prev or ()))


def _loss_bwd(y, t, name):
    def body(y_ref, t_ref, l_ref, d_ref, db_ref):
        e = y_ref[...] - t_ref[...]
        d = e * (1.0 / D)
        d_ref[...] = d
        db_ref[...] = d.astype(BF16)

        @pl.when(pl.program_id(0) == 0)
        def _():
            l_ref[...] = jnp.zeros_like(l_ref)
        s = jnp.sum(jnp.sum(e * e, axis=-1, keepdims=True), axis=0, keepdims=True)
        l_ref[...] += jnp.broadcast_to(s, l_ref.shape)

    row = pl.BlockSpec((TR, D), lambda i: (i, 0))
    tile = pl.BlockSpec((8, 128), lambda i: (0, 0))
    return pl.pallas_call(
        body, name=name, grid=(L // TR,), in_specs=[row, row], out_specs=[tile, row, row],
        out_shape=[jax.ShapeDtypeStruct((8, 128), F32), jax.ShapeDtypeStruct((L, D), F32),
                   jax.ShapeDtypeStruct((L, D), BF16)],
        compiler_params=_cp(("arbitrary",)))(y, t)


def _adamw(w, g, m, v, name):
    rows, cols = w.shape
    tr = rows
    for cand in (512, 256, 128, 64, 32, 16, 8):
        if rows % cand == 0 and cand * cols * 4 <= 2 * 1024 * 1024:
            tr = cand
            break
    c1 = 1.0 / (1.0 - B1 ** STEP)
    c2 = 1.0 / (1.0 - B2 ** STEP)

    def body(w_ref, g_ref, m_ref, v_ref, d_ref, mo_ref, vo_ref):
        gv = g_ref[...]
        mn = B1 * m_ref[...] + (1.0 - B1) * gv
        vn = B2 * v_ref[...] + (1.0 - B2) * (gv * gv)
        mo_ref[...] = mn
        vo_ref[...] = vn
        d_ref[...] = -LR * ((mn * c1) / (jnp.sqrt(vn * c2) + EPS_A) + WD * w_ref[...])

    blk = pl.BlockSpec((tr, cols), lambda i: (i, 0))
    shp = jax.ShapeDtypeStruct((rows, cols), F32)
    return pl.pallas_call(body, name=name, grid=(rows // tr,), in_specs=[blk] * 4, out_specs=[blk] * 3,
                          out_shape=[shp] * 3, compiler_params=_cp(("parallel",)))(w, g, m, v)


CT = 256
CPAD = 32


U_X, U_Z, U_A, U_G, U_Q, U_K, U_V, U_DT = 0, 1536, 2560, 3072, 3584, 4096, 4224, 4352


CEXT = 8
CWIN = CT + CPAD
CROWS = L + CPAD + CEXT


def _fill_shifted(src_ref, base, win_ref, sh_ref):
    win_ref[...] = src_ref[pl.ds(base, CWIN + CEXT), :]
    for p in range(8):
        sh_ref[p] = win_ref[pl.ds(p, CWIN), :]


def _tap(sh_ref, o):
    return sh_ref[o % 8, 8 * (o // 8):8 * (o // 8) + CT, :]


CB = 128


def _conv_specs():
    return [pl.BlockSpec((L, CB), lambda j: (0, U_A // CB + j)), pl.BlockSpec((L, CB), lambda j: (0, U_G // CB + j))]


def _conv_scratch(n_padded):
    return ([pltpu.VMEM((CROWS, CB), F32)] * n_padded
            + [pltpu.VMEM((CWIN + CEXT, CB), F32), pltpu.VMEM((8, CWIN, CB), F32)])


def _fill_gated(a_ref, g_ref, hp_ref):
    hp_ref[0:CPAD, :] = jnp.zeros((CPAD, CB), F32)
    hp_ref[CPAD:CPAD + L, :] = a_ref[...] * _sigmoid(g_ref[...])
    hp_ref[CPAD + L:, :] = jnp.zeros((CEXT, CB), F32)


def _conf_conv_fwd(u, w, b, name):
    def body(a_ref, g_ref, w_ref, b_ref, c_ref, hp_ref, win_ref, sh_ref):
        _fill_gated(a_ref, g_ref, hp_ref)

        def tile(i, carry):
            base = pl.multiple_of(i * CT, CT)
            _fill_shifted(hp_ref, base, win_ref, sh_ref)
            c = jnp.broadcast_to(b_ref[...], (CT, CB))
            for k in range(CMK):
                c = c + w_ref[k:k + 1, :] * _tap(sh_ref, 2 + k)
            c_ref[pl.ds(base, CT), :] = c
            return carry

        lax.fori_loop(0, L // CT, tile, 0)

    return pl.pallas_call(
        body, name=name, grid=(CMC // CB,),
        in_specs=_conv_specs() + [pl.BlockSpec((CMK, CB), lambda j: (0, j)), pl.BlockSpec((1, CB), lambda j: (0, j))],
        out_specs=pl.BlockSpec((L, CB), lambda j: (0, j)),
        out_shape=jax.ShapeDtypeStruct((L, CMC), F32), scratch_shapes=_conv_scratch(1),
        compiler_params=_cp(("parallel",)))(u, u, w, b)


def _conf_ln_fwd(c, lw, lb, name):
    def body(c_ref, lw_ref, lb_ref, o_ref):
        cv = c_ref[...]
        cc = cv - jnp.mean(cv, axis=-1, keepdims=True)
        var = jnp.mean(cc * cc, axis=-1, keepdims=True)
        l = cc * lax.rsqrt(var + LN_EPS) * lw_ref[...] + lb_ref[...]
        o_ref[...] = (l * _sigmoid(l)).astype(BF16)

    row = pl.BlockSpec((TR, CMC), lambda i: (i, 0))
    vec = pl.BlockSpec((1, CMC), lambda i: (0, 0))
    return pl.pallas_call(body, name=name, grid=(L // TR,), in_specs=[row, vec, vec], out_specs=row,
                          out_shape=jax.ShapeDtypeStruct((L, CMC), BF16),
                          compiler_params=_cp(("parallel",)))(c, lw, lb)


def _conf_ln_bwd(c, lw, lb, dy, name):
    def body(c_ref, lw_ref, lb_ref, dy_ref, dc_ref, db_ref, dlw_ref, dlb_ref):
        cv = c_ref[...]
        cc = cv - jnp.mean(cv, axis=-1, keepdims=True)
        var = jnp.mean(cc * cc, axis=-1, keepdims=True)
        rstd = lax.rsqrt(var + LN_EPS)
        n = cc * rstd
        l = n * lw_ref[...] + lb_ref[...]
        sl = _sigmoid(l)
        dl = dy_ref[...] * (sl * (1.0 + l * (1.0 - sl)))
        dn = dl * lw_ref[...]
        dc = rstd * (dn - jnp.mean(dn, axis=-1, keepdims=True) - n * jnp.mean(dn * n, axis=-1, keepdims=True))
        dc_ref[...] = dc

        @pl.when(pl.program_id(0) == 0)
        def _():
            db_ref[...] = jnp.zeros_like(db_ref)
            dlw_ref[...] = jnp.zeros_like(dlw_ref)
            dlb_ref[...] = jnp.zeros_like(dlb_ref)
        db_ref[...] += jnp.sum(dc, axis=0, keepdims=True)
        dlw_ref[...] += jnp.sum(dl * n, axis=0, keepdims=True)
        dlb_ref[...] += jnp.sum(dl, axis=0, keepdims=True)

    row = pl.BlockSpec((TR, CMC), lambda i: (i, 0))
    vec = pl.BlockSpec((1, CMC), lambda i: (0, 0))
    vshape = jax.ShapeDtypeStruct((1, CMC), F32)
    return pl.pallas_call(
        body, name=name, grid=(L // TR,),
        in_specs=[row, vec, vec, pl.BlockSpec((TR, CMC), lambda i: (i, (SSD_W + ATT_W) // CMC))],
        out_specs=[row, vec, vec, vec], out_shape=[jax.ShapeDtypeStruct((L, CMC), F32), vshape, vshape, vshape],
        compiler_params=_cp(("arbitrary",)))(c, lw, lb, dy)


def _conf_conv_bwd(u, dc, w, name):
    def body(a_ref, g_ref, dc_ref, w_ref, da_ref, dg_ref, dw_ref, hp_ref, dcp_ref, win_ref, sh_ref, dwacc_ref):
        _fill_gated(a_ref, g_ref, hp_ref)
        dcp_ref[0:L, :] = dc_ref[...]
        dcp_ref[L:, :] = jnp.zeros((CPAD + CEXT, CB), F32)
        dwacc_ref[...] = jnp.zeros_like(dwacc_ref)

        def tile(i, carry):
            base = pl.multiple_of(i * CT, CT)
            _fill_shifted(hp_ref, base, win_ref, sh_ref)
            dcv = dcp_ref[pl.ds(base, CT), :]
            for k in range(CMK):
                dwacc_ref[k] += (dcv * _tap(sh_ref, 2 + k)).reshape(CT // 8, 8, CB).sum(axis=0)
            _fill_shifted(dcp_ref, base, win_ref, sh_ref)
            dh = jnp.zeros((CT, CB), F32)
            for k in range(CMK):
                dh = dh + w_ref[k:k + 1, :] * _tap(sh_ref, CMK - 1 - k)
            av = a_ref[pl.ds(base, CT), :]
            sg = _sigmoid(g_ref[pl.ds(base, CT), :])
            da_ref[pl.ds(base, CT), :] = (dh * sg).astype(BF16)
            dg_ref[pl.ds(base, CT), :] = (dh * av * sg * (1.0 - sg)).astype(BF16)
            return carry

        lax.fori_loop(0, L // CT, tile, 0)
        for k in range(CMK):
            dw_ref[k:k + 1, :] = jnp.sum(dwacc_ref[k], axis=0, keepdims=True)
        dw_ref[CMK:, :] = jnp.zeros((32 - CMK, CB), F32)

    col = pl.BlockSpec((L, CB), lambda j: (0, j))
    return pl.pallas_call(
        body, name=name, grid=(CMC // CB,),
        in_specs=_conv_specs() + [col, pl.BlockSpec((CMK, CB), lambda j: (0, j))],
        out_specs=[col, col, pl.BlockSpec((32, CB), lambda j: (0, j))],
        out_shape=[jax.ShapeDtypeStruct((L, CMC), BF16), jax.ShapeDtypeStruct((L, CMC), BF16),
                   jax.ShapeDtypeStruct((32, CMC), F32)],
        scratch_shapes=_conv_scratch(2) + [pltpu.VMEM((32, 8, CB), F32)],
        compiler_params=_cp(("parallel",)))(u, u, dc, w)


NPAIR = NQH // 2


def _partner(x, lo32):
    return jnp.where(lo32, pltpu.roll(x, 96, 1), pltpu.roll(x, 32, 1))


def _swa_prep(x, w2, w2p, c4, s4, bd, lo32):
    r = lax.rsqrt(_xdot_l(x * x, bd) * (1.0 / HD) + RMS_EPS)
    xh = x * r
    return r, xh, xh * w2 * c4 + _partner(xh, lo32) * w2p * s4


def _swa_unprep(dr, r, xh, w2, w2p, c4, s4, bd, lo32):
    dn = dr * c4
    dnp = dr * s4
    gx = dn * w2 + _partner(dnp * w2p, lo32)
    dw = jnp.sum((dn + _partner(dnp, lo32)) * xh, axis=0, keepdims=True)
    mu = _xdot_l(gx * xh, bd) * (1.0 / HD)
    return r * (gx - xh * mu), dw


def _swa_softmax(s, sink):
    row = lax.broadcasted_iota(jnp.int32, (L, 2 * Q), 0)
    col = lax.broadcasted_iota(jnp.int32, (L, 2 * Q), 1)
    rm = row & (Q - 1)
    valid = (col > rm) & (col <= rm + Q) & ((row >= Q) | (col >= Q))
    s = jnp.where(valid, s * (1.0 / math.sqrt(HD)), NEG)
    m = jnp.maximum(jnp.max(s, axis=-1, keepdims=True), sink)
    p = jnp.exp(s - m)
    ps = jnp.exp(sink - m)
    inv = 1.0 / (jnp.sum(p, axis=-1, keepdims=True) + ps)
    return p * inv, ps * inv


def _swa_in_specs():
    tab = pl.BlockSpec((L, 128), lambda p: (0, 0))
    wv = pl.BlockSpec((1, 128), lambda p: (0, 0))
    sk = pl.BlockSpec((1, 1, 128), lambda p: (p, 0, 0))
    return [pl.BlockSpec((L, 128), lambda p: (0, U_Q // 128 + p)), pl.BlockSpec((L, 128), lambda p: (0, U_K // 128)),
            pl.BlockSpec((L, 128), lambda p: (0, U_V // 128)), tab, tab, wv, wv, wv, wv, sk, sk,
            pl.BlockSpec((128, 128), lambda p: (0, 0))]


def _swa_setup(q_ref, k_ref, v_ref, c_ref, s_ref, qw_ref, qwp_ref, kw_ref, kwp_ref, bd_ref, kpad, vpad):
    g = pl.program_id(0) // 2
    lane = lax.broadcasted_iota(jnp.int32, (L, 128), 1)
    lo32 = (lane & 32) == 0
    own = (lane >> 6) == g
    c4, s4, bd = c_ref[...], s_ref[...], bd_ref[...]
    qn = _swa_prep(q_ref[...], qw_ref[...], qwp_ref[...], c4, s4, bd, lo32)
    kn = _swa_prep(k_ref[...], kw_ref[...], kwp_ref[...], c4, s4, bd, lo32)
    vv = v_ref[...]
    kpad[0:Q, :] = jnp.zeros((Q, 128), BF16)
    vpad[0:Q, :] = jnp.zeros((Q, 128), BF16)
    kpad[Q:, :] = jnp.where(own, kn[2], pltpu.roll(kn[2], HD, 1)).astype(BF16)
    vpad[Q:, :] = jnp.where(own, vv, pltpu.roll(vv, HD, 1)).astype(BF16)
    return qn, kn, lo32, own, c4, s4, bd


def _swa_fwd(u, cos4, sin4, qw2, qw2p, kw2, kw2p, sink_e, sink_o, bd, name):
    def body(q_ref, k_ref, v_ref, c_ref, s_ref, qw_ref, qwp_ref, kw_ref, kwp_ref, ske_ref, sko_ref, bd_ref,
             o_ref, kpad, vpad, s_scr, p_scr):
        qn, _, _, _, _, _, _ = _swa_setup(q_ref, k_ref, v_ref, c_ref, s_ref, qw_ref, qwp_ref, kw_ref, kwp_ref,
                                          bd_ref, kpad, vpad)
        qr = qn[2]
        first = lax.broadcasted_iota(jnp.int32, (Q, 128), 1) < HD
        for n in range(NC):
            rows = slice(n * Q, (n + 1) * Q)
            kc = kpad[n * Q:(n + 2) * Q, :]
            s_scr[0, rows, :] = _dg(jnp.where(first, qr[rows], 0.0).astype(BF16), kc, 1, 1)
            s_scr[1, rows, :] = _dg(jnp.where(first, 0.0, qr[rows]).astype(BF16), kc, 1, 1)
        for h, sk_ref in ((0, ske_ref), (1, sko_ref)):
            p, _ = _swa_softmax(s_scr[h], sk_ref[0][:, 0:1])
            p_scr[h] = p.astype(BF16)
        for n in range(NC):
            rows = slice(n * Q, (n + 1) * Q)
            vc = vpad[n * Q:(n + 2) * Q, :]
            o_ref[rows, :] = jnp.where(first, _dg(p_scr[0, rows, :], vc, 1, 0),
                                       _dg(p_scr[1, rows, :], vc, 1, 0)).astype(BF16)

    return pl.pallas_call(
        body, name=name, grid=(NPAIR,), in_specs=_swa_in_specs(),
        out_specs=pl.BlockSpec((L, 128), lambda p: (0, p)),
        out_shape=jax.ShapeDtypeStruct((L, ATT_W), BF16),
        scratch_shapes=[pltpu.VMEM((L + Q, 128), BF16), pltpu.VMEM((L + Q, 128), BF16),
                        pltpu.VMEM((2, L, 2 * Q), F32), pltpu.VMEM((2, L, 2 * Q), BF16)],
        compiler_params=_cp(("arbitrary",)))(u, u, u, cos4, sin4, qw2, qw2p, kw2, kw2p, sink_e, sink_o, bd)


def _swa_bwd(u, dy, cos4, sin4, qw2, qw2p, kw2, kw2p, sink_e, sink_o, bd, name):
    def body(q_ref, k_ref, v_ref, c_ref, s_ref, qw_ref, qwp_ref, kw_ref, kwp_ref, ske_ref, sko_ref, bd_ref, do_ref,
             dq_ref, dk_ref, dv_ref, dqw_ref, dkw_ref, dse_ref, dso_ref,
             kpad, vpad, s_scr, dp_scr, ds_scr, pb_scr, dkr_acc, dv_acc, dqr_scr):
        pidx = pl.program_id(0)

        @pl.when(pidx == 0)
        def _():
            dkr_acc[...] = jnp.zeros_like(dkr_acc)
            dv_acc[...] = jnp.zeros_like(dv_acc)
            dqw_ref[...] = jnp.zeros_like(dqw_ref)

        qn, kn, lo32, own, c4, s4, bd = _swa_setup(q_ref, k_ref, v_ref, c_ref, s_ref, qw_ref, qwp_ref, kw_ref,
                                                   kwp_ref, bd_ref, kpad, vpad)
        qr = qn[2]
        lane_q = lax.broadcasted_iota(jnp.int32, (Q, 128), 1)
        first = lane_q < HD
        own_q = (lane_q >> 6) == pidx // 2

        def halves(t):
            return jnp.where(first, t, 0.0).astype(BF16), jnp.where(first, 0.0, t).astype(BF16)

        for n in range(NC):
            rows = slice(n * Q, (n + 1) * Q)
            kc = kpad[n * Q:(n + 2) * Q, :]
            vc = vpad[n * Q:(n + 2) * Q, :]
            qm = halves(qr[rows])
            dom = halves(do_ref[rows, :])
            for h in range(2):
                s_scr[h, rows, :] = _dg(qm[h], kc, 1, 1)
                dp_scr[h, rows, :] = _dg(dom[h], vc, 1, 1)
        for h, sk_ref, dsk_ref in ((0, ske_ref, dse_ref), (1, sko_ref, dso_ref)):
            p, ps = _swa_softmax(s_scr[h], sk_ref[0][:, 0:1])
            dp = dp_scr[h]
            delta = jnp.sum(p * dp, axis=-1, keepdims=True)
            dsk_ref[0] = jnp.broadcast_to(-jnp.sum(ps * delta, axis=0, keepdims=True), (1, 128))
            ds_scr[h] = (p * (dp - delta) * (1.0 / math.sqrt(HD))).astype(BF16)
            pb_scr[h] = p.astype(BF16)
        for n in range(NC):
            rows = slice(n * Q, (n + 1) * Q)
            kc = kpad[n * Q:(n + 2) * Q, :]
            dqr_scr[rows, :] = jnp.where(first, _dg(ds_scr[0, rows, :], kc, 1, 0), _dg(ds_scr[1, rows, :], kc, 1, 0))
        for m in range(NC):
            acc_k = jnp.zeros((Q, 128), F32)
            acc_v = jnp.zeros((Q, 128), F32)
            for n, cols in ((m, slice(Q, 2 * Q)), (m + 1, slice(0, Q))):
                if n >= NC:
                    continue
                rows = slice(n * Q, (n + 1) * Q)
                qm = halves(qr[rows])
                dom = halves(do_ref[rows, :])
                for h in range(2):
                    acc_k = acc_k + _dg(ds_scr[h, rows, cols], qm[h], 0, 0)
                    acc_v = acc_v + _dg(pb_scr[h, rows, cols], dom[h], 0, 0)
            rows = slice(m * Q, (m + 1) * Q)
            dkr_acc[rows, :] += jnp.where(own_q, acc_k + pltpu.roll(acc_k, HD, 1), 0.0)
            dv_acc[rows, :] += jnp.where(own_q, acc_v + pltpu.roll(acc_v, HD, 1), 0.0)
        dq, dqw = _swa_unprep(dqr_scr[...], qn[0], qn[1], qw_ref[...], qwp_ref[...], c4, s4, bd, lo32)
        dq_ref[...] = dq.astype(BF16)
        dqw_ref[...] += dqw

        @pl.when(pidx == NPAIR - 1)
        def _():
            dk, dkw = _swa_unprep(dkr_acc[...], kn[0], kn[1], kw_ref[...], kwp_ref[...], c4, s4, bd, lo32)
            dk_ref[...] = dk.astype(BF16)
            dkw_ref[...] = dkw
            dv_ref[...] = dv_acc[...].astype(BF16)

    full = pl.BlockSpec((L, 128), lambda p: (0, 0))
    wv = pl.BlockSpec((1, 128), lambda p: (0, 0))
    sk = pl.BlockSpec((1, 1, 128), lambda p: (p, 0, 0))
    vec = jax.ShapeDtypeStruct((1, 128), F32)
    skv = jax.ShapeDtypeStruct((NPAIR, 1, 128), F32)
    return pl.pallas_call(
        body, name=name, grid=(NPAIR,),
        in_specs=_swa_in_specs() + [pl.BlockSpec((L, 128), lambda p: (0, SSD_W // 128 + p))],
        out_specs=[pl.BlockSpec((L, 128), lambda p: (0, p)), full, full, wv, wv, sk, sk],
        out_shape=[jax.ShapeDtypeStruct((L, ATT_W), BF16), jax.ShapeDtypeStruct((L, 128), BF16),
                   jax.ShapeDtypeStruct((L, 128), BF16), vec, vec, skv, skv],
        scratch_shapes=[pltpu.VMEM((L + Q, 128), BF16), pltpu.VMEM((L + Q, 128), BF16),
                        pltpu.VMEM((2, L, 2 * Q), F32), pltpu.VMEM((2, L, 2 * Q), F32),
                        pltpu.VMEM((2, L, 2 * Q), BF16), pltpu.VMEM((2, L, 2 * Q), BF16),
                        pltpu.VMEM((L, 128), F32), pltpu.VMEM((L, 128), F32), pltpu.VMEM((L, 128), F32)],
        compiler_params=_cp(("arbitrary",)))(u, u, u, cos4, sin4, qw2, qw2p, kw2, kw2p, sink_e, sink_o, bd, dy)


def _ssd_consts():
    hh = jnp.arange(128)[:, None]
    e = (hh == (jnp.arange(SSD_W)[None, :] // HP)).astype(BF16)
    e2 = (hh == (jnp.arange(NH * 128)[None, :] // 128)).astype(BF16)
    et = e.T
    tril = (jnp.arange(Q)[:, None] >= jnp.arange(Q)[None, :]).astype(BF16)
    triu = tril.T
    eye = jnp.eye(128, dtype=BF16)
    return e, e2, et, tril, triu, eye


def _ssd_common(x_ref, ext_scr, cw_ref, cb_ref, dt_ref, dtb_ref, alog_ref, e_ref, e2_ref, tril_ref, triu_ref,
                arow_scr, acol_scr, eax_scr):
    conv = jnp.broadcast_to(cb_ref[...], (Q, XBC))
    for k in range(4):
        conv = conv + cw_ref[k:k + 1, :] * ext_scr[pl.ds(5 + k, Q), :]
    sg = _sigmoid(conv)
    xbc = conv * sg
    dtpre = dt_ref[...] + dtb_ref[...]
    dt = _softplus(dtpre)
    a = -jnp.exp(alog_ref[...])
    adt = dt * a
    acol = _xdot_r(tril_ref[...], adt)
    acol_scr[...] = acol
    arow_scr[...] = _xdot_l(adt, triu_ref[...], 0, 0)
    alast = acol_scr[Q - 1:Q, :]
    ea = jnp.exp(acol)
    decs = jnp.exp(alast - acol)
    e = e_ref[...]
    dt_x = _xdot_l(dt, e)
    eax_scr[...] = _xdot_l(ea, e)
    decs_x = _xdot_l(decs, e)
    acx2 = _xdot_l(acol, e2_ref[...])
    return conv, sg, xbc, dtpre, dt, a, adt, acol, alast, ea, decs, dt_x, decs_x, acx2


def _ssd_fwd(u, cw, cb, dtb, alog, dxp, nw, consts, name):
    e, e2, et, tril, triu, eye = consts

    def body(z0_ref, z1_ref, x_ref, dt_ref, cw_ref, cb_ref, dtb_ref, alog_ref, dx_ref, nw_ref, e_ref, e2_ref,
             tril_ref, triu_ref, ya_ref, ypre_ref, st_ref, s_scr, ext_scr, arow_scr, acol_scr, eax_scr):
        c = pl.program_id(0)

        @pl.when(c == 0)
        def _():
            s_scr[...] = jnp.zeros_like(s_scr)
            ext_scr[0:8, :] = jnp.zeros((8, XBC), F32)
        ext_scr[8:8 + Q, :] = x_ref[...]
        (conv, sg, xbc, dtpre, dt, a, adt, acol, alast, ea, decs, dt_x, decs_x, acx2) = _ssd_common(
            x_ref, ext_scr, cw_ref, cb_ref, dt_ref, dtb_ref, alog_ref, e_ref, e2_ref, tril_ref, triu_ref,
            arow_scr, acol_scr, eax_scr)
        ext_scr[0:8, :] = ext_scr[Q:Q + 8, :]
        xs = xbc[:, :SSD_W]
        xdt = xs * dt_x
        lane = lax.broadcasted_iota(jnp.int32, (Q, 128), 1)
        causal = lax.broadcasted_iota(jnp.int32, (Q, Q), 0) >= lax.broadcasted_iota(jnp.int32, (Q, Q), 1)
        for g in range(2):
            bg = xbc[:, SSD_W + g * NS:SSD_W + (g + 1) * NS].astype(BF16)
            cg = xbc[:, SSD_W + 2 * NS + g * NS:SSD_W + 2 * NS + (g + 1) * NS].astype(BF16)
            cbm = _dg(cg, bg, 1, 1)
            sgv = s_scr[g]
            st_ref[0, g] = sgv
            gc = slice(g * 512, (g + 1) * 512)
            yoff = _dg(cg, sgv.astype(BF16), 1, 0) * eax_scr[:, gc]
            for pr in range(4):
                h0 = g * 8 + 2 * pr
                h1 = h0 + 1
                c0 = g * 512 + pr * 128
                xp = xdt[:, c0:c0 + 128].astype(BF16)
                w0 = (cbm * jnp.exp(jnp.where(causal, acx2[:, h0 * 128:(h0 + 1) * 128] - arow_scr[h0:h0 + 1, :],
                                              NEG))).astype(BF16)
                w1 = (cbm * jnp.exp(jnp.where(causal, acx2[:, h1 * 128:(h1 + 1) * 128] - arow_scr[h1:h1 + 1, :],
                                              NEG))).astype(BF16)
                yd = jnp.where(lane < HP, _dg(w0, xp, 1, 0), _dg(w1, xp, 1, 0))
                ypre_ref[:, c0:c0 + 128] = (yd + yoff[:, pr * 128:(pr + 1) * 128]
                                            + xs[:, c0:c0 + 128] * dx_ref[:, c0:c0 + 128])
            contrib = _dg(bg, (xdt[:, gc] * decs_x[:, gc]).astype(BF16), 0, 0)
            s_scr[g] = sgv * eax_scr[Q - 1:Q, gc] + contrib
        for g, zr in enumerate((z0_ref, z1_ref)):
            gc = slice(g * 512, (g + 1) * 512)
            zz = zr[...]
            ggg = ypre_ref[:, gc] * (zz * _sigmoid(zz))
            rstd = lax.rsqrt(jnp.mean(ggg * ggg, axis=-1, keepdims=True) + RMS_EPS)
            ya_ref[:, gc] = (ggg * rstd * nw_ref[:, gc]).astype(BF16)

    def row(w, blk=0):
        return pl.BlockSpec((Q, w), lambda c: (c, blk))

    def full(shape):
        return pl.BlockSpec(shape, lambda c: (0,) * len(shape))

    return pl.pallas_call(
        body, name=name, grid=(NC,),
        in_specs=[row(512, U_Z // 512), row(512, U_Z // 512 + 1), row(XBC, U_X // XBC), row(128, U_DT // 128),
                  full((4, XBC)), full((1, XBC)), full((1, 128)), full((1, 128)),
                  full((1, SSD_W)), full((1, SSD_W)), full((128, SSD_W)), full((128, NH * 128)), full((Q, Q)),
                  full((Q, Q))],
        out_specs=[row(SSD_W), row(SSD_W), pl.BlockSpec((1, 2, NS, 512), lambda c: (c, 0, 0, 0))],
        out_shape=[jax.ShapeDtypeStruct((L, SSD_W), BF16), jax.ShapeDtypeStruct((L, SSD_W), F32),
                   jax.ShapeDtypeStruct((NC, 2, NS, 512), F32)],
        scratch_shapes=[pltpu.VMEM((2, NS, 512), F32), pltpu.VMEM((Q + 8, XBC), F32), pltpu.VMEM((128, Q), F32),
                        pltpu.VMEM((Q, 128), F32), pltpu.VMEM((Q, SSD_W), F32)],
        compiler_params=_cp(("arbitrary",)))(u, u, u, u, cw, cb, dtb, alog, dxp, nw, e, e2, tril, triu)


def _ssd_bwd(u, ypre, st, dy, cw, cb, dtb, alog, dxp, nw, consts, name):
    e, e2, et, tril, triu, eye = consts

    def body(z0_ref, z1_ref, x_ref, xp_ref, dt_ref, ypre_ref, st_ref, dya_ref, cw_ref, cb_ref, dtb_ref, alog_ref, dx_ref,
             nw_ref, e_ref, e2_ref, et_ref, tril_ref, triu_ref, eye_ref,
             dz_ref, dxr_ref, ddtr_ref, dcw_ref, dcb_ref, ddtb_ref, dalog_ref, dd_ref, dnw_ref,
             g_scr, ext_scr, ext2_scr, arow_scr, acol_scr, eax_scr, darow_scr, dxdt_scr, t1_scr, t2_scr, dgg_scr):
        i = pl.program_id(0)

        @pl.when(i == 0)
        def _():
            g_scr[...] = jnp.zeros_like(g_scr)
            ext2_scr[Q:Q + 8, :] = jnp.zeros((8, XBC), F32)
            for r in (dcw_ref, dcb_ref, ddtb_ref, dalog_ref, dd_ref, dnw_ref):
                r[...] = jnp.zeros_like(r)
        not_first = jnp.where(i < NC - 1, 1.0, 0.0)
        ext_scr[0:8, :] = xp_ref[Q - 8:Q, :] * not_first
        ext_scr[8:8 + Q, :] = x_ref[...]
        (conv, sg, xbc, dtpre, dt, a, adt, acol, alast, ea, decs, dt_x, decs_x, acx2) = _ssd_common(
            x_ref, ext_scr, cw_ref, cb_ref, dt_ref, dtb_ref, alog_ref, e_ref, e2_ref, tril_ref, triu_ref,
            arow_scr, acol_scr, eax_scr)
        et_m = et_ref[...]
        xs = xbc[:, :SSD_W]
        xdt = xs * dt_x
        y = ypre_ref[...]
        zz = jnp.concatenate([z0_ref[...], z1_ref[...]], axis=1)
        sz = _sigmoid(zz)
        silu_z = zz * sz
        gg = y * silu_z
        dya = dya_ref[...]
        for g in range(2):
            gc = slice(g * 512, (g + 1) * 512)
            ggg = gg[:, gc]
            rstd = lax.rsqrt(jnp.mean(ggg * ggg, axis=-1, keepdims=True) + RMS_EPS)
            n = ggg * rstd
            dyag = dya[:, gc]
            dnw_ref[:, gc] += jnp.sum(dyag * n, axis=0, keepdims=True)
            dn = dyag * nw_ref[:, gc]
            dgg_scr[:, gc] = rstd * (dn - n * jnp.mean(dn * n, axis=-1, keepdims=True))
        dgg = dgg_scr[...]
        dy = dgg * silu_z
        dz_ref[...] = (dgg * y * (sz * (1.0 + zz * (1.0 - sz)))).astype(BF16)
        dd_ref[...] += _rowdot(jnp.sum(dy * xs, axis=0, keepdims=True), et_m)
        dxs = dy * dx_ref[...]
        dys = dy * eax_scr[...]
        lane = lax.broadcasted_iota(jnp.int32, (Q, 128), 1)
        causal = lax.broadcasted_iota(jnp.int32, (Q, Q), 0) >= lax.broadcasted_iota(jnp.int32, (Q, Q), 1)
        darow_scr[...] = jnp.zeros_like(darow_scr)
        dacol = jnp.zeros((Q, 128), F32)
        dcdx = []
        dbs = []
        dcs = []
        for g in range(2):
            gc = slice(g * 512, (g + 1) * 512)
            bg = xbc[:, SSD_W + g * NS:SSD_W + (g + 1) * NS].astype(BF16)
            cg = xbc[:, SSD_W + 2 * NS + g * NS:SSD_W + 2 * NS + (g + 1) * NS].astype(BF16)
            cbm = _dg(cg, bg, 1, 1)
            sgv = st_ref[0, g]
            sgb = sgv.astype(BF16)
            gv = g_scr[g]
            gvb = gv.astype(BF16)
            yoff = _dg(cg, sgb, 1, 0) * eax_scr[:, gc]
            dysg = dys[:, gc].astype(BF16)
            dcg = _dg(dysg, sgb, 1, 1)
            ds_off = _dg(cg, dysg, 0, 0)
            t1_scr[:, gc] = dy[:, gc] * yoff
            xdec = xdt[:, gc] * decs_x[:, gc]
            dxd = _dg(bg, gvb, 1, 0)
            dbg = _dg(xdec.astype(BF16), gvb, 1, 1)
            dxdt_g = dxd * decs_x[:, gc]
            t2_scr[:, gc] = dxd * xdt[:, gc]
            cdx = eax_scr[Q - 1:Q, gc]
            dcdx.append(jnp.sum(gv * sgv, axis=0, keepdims=True))
            g_scr[g] = gv * cdx + ds_off
            dcb_acc = jnp.zeros((Q, Q), F32)
            for pr in range(4):
                c0 = g * 512 + pr * 128
                xp = xdt[:, c0:c0 + 128].astype(BF16)
                dyp = dy[:, c0:c0 + 128]
                dypb = dyp.astype(BF16)
                halves = []
                for hh, keep in ((g * 8 + 2 * pr, lane < HP), (g * 8 + 2 * pr + 1, lane >= HP)):
                    lam = jnp.exp(jnp.where(causal, acx2[:, hh * 128:(hh + 1) * 128] - arow_scr[hh:hh + 1, :], NEG))
                    w = cbm * lam
                    dw = _dg(jnp.where(keep, dyp, 0.0).astype(BF16), xp, 1, 1)
                    dcb_acc = dcb_acc + dw * lam
                    t = dw * w
                    dacol = dacol + jnp.sum(t, axis=-1, keepdims=True) * (lane == hh).astype(F32)
                    darow_scr[hh:hh + 1, :] -= jnp.sum(t, axis=0, keepdims=True)
                    halves.append(_dg(w.astype(BF16), dypb, 0, 0))
                dxdt_scr[:, c0:c0 + 128] = (jnp.where(lane < HP, halves[0], halves[1])
                                            + dxdt_g[:, pr * 128:(pr + 1) * 128])
            dcbb = dcb_acc.astype(BF16)
            dcs.append(dcg + _dg(dcbb, bg, 1, 0))
            dbs.append(dbg + _dg(dcbb, cg, 0, 0))
        dacol = dacol + _xdot_l(t1_scr[...], et_m)
        ddecs = _xdot_l(t2_scr[...], et_m) * decs
        dacol = dacol - ddecs
        dalast = jnp.sum(ddecs, axis=0, keepdims=True)
        dcd = _rowdot(jnp.concatenate(dcdx, axis=1), et_m)
        dalast = dalast + dcd * jnp.exp(alast)
        dacol = dacol + _xdot_l(darow_scr[...], eye_ref[...], 0, 0)
        rowi = lax.broadcasted_iota(jnp.int32, (Q, 128), 0)
        dacol = dacol + jnp.where(rowi == Q - 1, dalast, 0.0)
        dadt = _xdot_r(triu_ref[...], dacol)
        dxdt = dxdt_scr[...]
        ddt = dadt * a + _xdot_l(dxdt * xs, et_m)
        dalog_ref[...] += jnp.sum(dadt * dt, axis=0, keepdims=True) * a
        dxs = dxs + dxdt * dt_x
        ddtr = ddt * _sigmoid(dtpre)
        ddtb_ref[...] += jnp.sum(ddtr, axis=0, keepdims=True)
        ddtr_ref[...] = ddtr.astype(BF16)
        dsilu = sg * (1.0 + conv * (1.0 - sg))
        ext2_scr[0:Q, 0:SSD_W] = dxs * dsilu[:, :SSD_W]
        for g in range(2):
            o1 = SSD_W + g * NS
            o2 = SSD_W + 2 * NS + g * NS
            ext2_scr[0:Q, o1:o1 + NS] = dbs[g] * dsilu[:, o1:o1 + NS]
            ext2_scr[0:Q, o2:o2 + NS] = dcs[g] * dsilu[:, o2:o2 + NS]
        dconv = ext2_scr[0:Q, :]
        dcb_ref[...] += jnp.sum(dconv, axis=0, keepdims=True)
        dxr = jnp.zeros((Q, XBC), F32)
        for k in range(4):
            dcw_ref[k:k + 1, :] += jnp.sum(dconv * ext_scr[pl.ds(5 + k, Q), :], axis=0, keepdims=True)
            dxr = dxr + cw_ref[k:k + 1, :] * ext2_scr[pl.ds(3 - k, Q), :]
        dxr_ref[...] = dxr.astype(BF16)
        ext2_scr[Q:Q + 8, :] = ext2_scr[0:8, :]

    def row(w, blk=0):
        return pl.BlockSpec((Q, w), lambda i: (NC - 1 - i, blk))

    def full(shape):
        return pl.BlockSpec(shape, lambda i: (0,) * len(shape))

    prev = pl.BlockSpec((Q, XBC), lambda i: (jnp.maximum(NC - 2 - i, 0), U_X // XBC))
    return pl.pallas_call(
        body, name=name, grid=(NC,),
        in_specs=[row(512, U_Z // 512), row(512, U_Z // 512 + 1), row(XBC, U_X // XBC), prev, row(128, U_DT // 128),
                  row(SSD_W),
                  pl.BlockSpec((1, 2, NS, 512), lambda i: (NC - 1 - i, 0, 0, 0)), row(SSD_W),
                  full((4, XBC)), full((1, XBC)), full((1, 128)), full((1, 128)), full((1, SSD_W)),
                  full((1, SSD_W)), full((128, SSD_W)), full((128, NH * 128)), full((SSD_W, 128)), full((Q, Q)),
                  full((Q, Q)), full((128, 128))],
        out_specs=[row(SSD_W), row(XBC), row(128), full((8, XBC)), full((1, XBC)), full((1, 128)), full((1, 128)),
                   full((1, 128)), full((1, SSD_W))],
        out_shape=[jax.ShapeDtypeStruct((L, SSD_W), BF16), jax.ShapeDtypeStruct((L, XBC), BF16),
                   jax.ShapeDtypeStruct((L, 128), BF16), jax.ShapeDtypeStruct((8, XBC), F32),
                   jax.ShapeDtypeStruct((1, XBC), F32), jax.ShapeDtypeStruct((1, 128), F32),
                   jax.ShapeDtypeStruct((1, 128), F32), jax.ShapeDtypeStruct((1, 128), F32),
                   jax.ShapeDtypeStruct((1, SSD_W), F32)],
        scratch_shapes=[pltpu.VMEM((2, NS, 512), F32), pltpu.VMEM((Q + 8, XBC), F32), pltpu.VMEM((Q + 8, XBC), F32),
                        pltpu.VMEM((128, Q), F32), pltpu.VMEM((Q, 128), F32), pltpu.VMEM((Q, SSD_W), F32),
                        pltpu.VMEM((128, Q), F32), pltpu.VMEM((Q, SSD_W), F32), pltpu.VMEM((Q, SSD_W), F32),
                        pltpu.VMEM((Q, SSD_W), F32), pltpu.VMEM((Q, SSD_W), F32)],
        compiler_params=_cp(("arbitrary",)))(u, u, u, u, u, ypre, st, dy, cw, cb, dtb, alog, dxp, nw,
                                             e, e2, et, tril, triu, eye)


def _my_pos():
    return lax.axis_index("x"), lax.axis_index("y"), lax.axis_index("c")


CHIP_REL = ((1, 0), (0, 1), (1, 1))
CHIP_XOR = (2, 1, 3)
BIG = ("w_in", "w_out", "w_mlp_up", "w_mlp_down")
NW = len(BIG)
AT = 256


def _chips(x, y):
    return [(1 - x if dx else x, 1 - y if dy else y) for dx, dy in CHIP_REL]


HBM_SPEC = pl.BlockSpec(memory_space=pltpu.HBM)
SEM_SPEC = pl.BlockSpec(memory_space=pltpu.SEMAPHORE)
EFFECT = pltpu.SideEffectType.DATAFLOW_SIDE_EFFECTING


def _hbm(t):
    return pltpu.with_memory_space_constraint(t, pltpu.HBM)


def _split_start(srcs, lands, after, copies, name):
    n = len(srcs)

    def body(*refs):
        src_refs, land_refs = refs[:n], refs[n:2 * n]
        send_sems, recv_sems = refs[2 * n + 1], refs[2 * n + 2]
        token = refs[-1]
        for w, k, src, dst, dev in copies(src_refs, land_refs):
            pltpu.make_async_remote_copy(src_ref=src, dst_ref=dst, send_sem=send_sems.at[3 * w + k],
                                         recv_sem=recv_sems.at[3 * w + k], device_id=dev, device_id_type=MESH).start()
        token[...] = jnp.zeros_like(token)

    outs = pl.pallas_call(
        body, name=name,
        out_shape=(pltpu.SemaphoreType.DMA((3 * n,)), pltpu.SemaphoreType.DMA((3 * n,)),
                   *[pltpu.HBM(t.shape, t.dtype) for t in srcs], *[pltpu.HBM(t.shape, t.dtype) for t in lands],
                   jax.ShapeDtypeStruct((8, 128), F32)),
        in_specs=[HBM_SPEC] * (2 * n) + [ANY],
        out_specs=(SEM_SPEC, SEM_SPEC, *([HBM_SPEC] * (2 * n)), VMEM_SPEC),
        input_output_aliases={i: 2 + i for i in range(2 * n)},
        compiler_params=pltpu.CompilerParams(has_side_effects=EFFECT))(
            *[_hbm(t) for t in srcs], *[_hbm(t) for t in lands], after)
    return outs[0], outs[1], list(outs[2:2 + n]), list(outs[2 + n:2 + 2 * n]), outs[-1]


def _split_wait(send_sems, recv_sems, srcs, lands, after, copies, name):
    n = len(srcs)

    def body(*refs):
        src_refs, land_refs = refs[:n], refs[n:2 * n]
        ssem, rsem = refs[2 * n], refs[2 * n + 1]
        for w, k, src, dst, dev in copies(src_refs, land_refs):
            cp = pltpu.make_async_remote_copy(src_ref=src, dst_ref=dst, send_sem=ssem.at[3 * w + k],
                                              recv_sem=rsem.at[3 * w + k], device_id=dev, device_id_type=MESH)
            cp.wait_send()
            cp.wait_recv()

    outs = pl.pallas_call(
        body, name=name,
        out_shape=tuple([pltpu.HBM(t.shape, t.dtype) for t in srcs] + [pltpu.HBM(t.shape, t.dtype) for t in lands]),
        in_specs=[HBM_SPEC] * (2 * n) + [SEM_SPEC, SEM_SPEC, ANY],
        out_specs=tuple([HBM_SPEC] * (2 * n)),
        input_output_aliases={i: i for i in range(2 * n)},
        compiler_params=pltpu.CompilerParams(has_side_effects=EFFECT))(*srcs, *lands, send_sems, recv_sems, after)
    return list(outs[:n]), list(outs[n:])


def _ag_copies(arrival):
    def copies(src_refs, land_refs):
        x, y, c = _my_pos()
        s = 2 * x + y
        chips = _chips(x, y)
        for w in range(len(src_refs)):
            hr = src_refs[w].shape[0] // 2
            mine = pl.ds(c * hr, hr)
            for k in range(3):
                slot = s ^ CHIP_XOR[k] if arrival else s
                yield w, k, src_refs[w].at[mine], land_refs[w].at[slot, mine], (*chips[k], c)
    return copies


def _ag_forward(lands, name):
    n = len(lands)

    def body(*refs):
        outs = refs[n:2 * n]
        send_sems, recv_sems = refs[2 * n:]
        x, y, c = _my_pos()
        s = 2 * x + y
        sib = (x, y, 1 - c)
        sends = []
        for w in range(n):
            hr = outs[w].shape[1] // 2
            for k in range(3):
                blk = outs[w].at[s ^ CHIP_XOR[k], pl.ds(c * hr, hr)]
                fw = pltpu.make_async_remote_copy(
                    src_ref=blk, dst_ref=blk, send_sem=send_sems.at[w, k], recv_sem=recv_sems.at[w, k],
                    device_id=sib, device_id_type=MESH)
                fw.start()
                sends.append(fw)
        for w in range(n):
            hr = outs[w].shape[1] // 2
            for k in range(3):
                blk = outs[w].at[s ^ CHIP_XOR[k], pl.ds((1 - c) * hr, hr)]
                pltpu.make_async_remote_copy(
                    src_ref=blk, dst_ref=blk, send_sem=send_sems.at[w, k], recv_sem=recv_sems.at[w, k],
                    device_id=sib, device_id_type=MESH).wait_recv()
        for cp in sends:
            cp.wait_send()

    return pl.pallas_call(
        body, name=name, in_specs=[ANY] * n, out_specs=[ANY] * n,
        out_shape=[jax.ShapeDtypeStruct(t.shape, t.dtype) for t in lands],
        input_output_aliases={w: w for w in range(n)},
        scratch_shapes=[pltpu.SemaphoreType.DMA((n, 3)), pltpu.SemaphoreType.DMA((n, 3))])(*lands)


def _rs_copies(src_refs, land_refs):
    x, y, c = _my_pos()
    s = 2 * x + y
    chips = _chips(x, y)
    for w in range(len(src_refs)):
        for k in range(3):
            yield w, k, src_refs[w].at[s ^ CHIP_XOR[k]], land_refs[w].at[k], (*chips[k], c)


def _place_own(shard, gathered, sidx, name):
    r, cc = shard.shape

    def body(s_ref, a_ref, g_ref, o_ref):
        o_ref[0] = a_ref[...]

    return pl.pallas_call(
        body, name=name,
        grid_spec=pltpu.PrefetchScalarGridSpec(
            num_scalar_prefetch=1, grid=(r // AT,),
            in_specs=[pl.BlockSpec((AT, cc), lambda i, s_ref: (i, 0)), ANY],
            out_specs=pl.BlockSpec((1, AT, cc), lambda i, s_ref: (s_ref[0], i, 0))),
        out_shape=jax.ShapeDtypeStruct(gathered.shape, gathered.dtype),
        input_output_aliases={2: 0}, compiler_params=_cp(("parallel",)))(sidx, shard, gathered)


def _rs_pair(dwb, name):
    n = len(dwb)

    def body(*refs):
        ins, outs = refs[:n], refs[n:2 * n]
        send_sems, recv_sems = refs[2 * n:]
        x, y, c = _my_pos()
        cps = []
        for w in range(n):
            hr = ins[w].shape[1] // 2
            cp = pltpu.make_async_remote_copy(
                src_ref=ins[w].at[:, pl.ds((1 - c) * hr, hr)], dst_ref=outs[w], send_sem=send_sems.at[w],
                recv_sem=recv_sems.at[w], device_id=(x, y, 1 - c), device_id_type=MESH)
            cp.start()
            cps.append(cp)
        for cp in cps:
            cp.wait()

    return pl.pallas_call(
        body, name=name, in_specs=[ANY] * n, out_specs=[ANY] * n,
        out_shape=[jax.ShapeDtypeStruct((4, t.shape[1] // 2, t.shape[2]), t.dtype) for t in dwb],
        scratch_shapes=[pltpu.SemaphoreType.DMA((n,)), pltpu.SemaphoreType.DMA((n,))])(*dwb)


def _rs_sib(q, name):
    n = len(q)

    def body(*refs):
        outs = refs[n:2 * n]
        send_sems, recv_sems = refs[2 * n:]
        x, y, c = _my_pos()
        cps = []
        for w in range(n):
            hr = outs[w].shape[0] // 2
            mine = pl.ds(c * hr, hr)
            cp = pltpu.make_async_remote_copy(
                src_ref=outs[w].at[mine], dst_ref=outs[w].at[mine], send_sem=send_sems.at[w],
                recv_sem=recv_sems.at[w], device_id=(x, y, 1 - c), device_id_type=MESH)
            cp.start()
            cps.append(cp)
        for w in range(n):
            hr = outs[w].shape[0] // 2
            other = outs[w].at[pl.ds((1 - c) * hr, hr)]
            pltpu.make_async_remote_copy(
                src_ref=other, dst_ref=other, send_sem=send_sems.at[w], recv_sem=recv_sems.at[w],
                device_id=(x, y, 1 - c), device_id_type=MESH).wait_recv()
        for cp in cps:
            cp.wait_send()

    return pl.pallas_call(
        body, name=name, in_specs=[ANY] * n, out_specs=[ANY] * n,
        out_shape=[jax.ShapeDtypeStruct(t.shape, t.dtype) for t in q],
        input_output_aliases={w: w for w in range(n)},
        scratch_shapes=[pltpu.SemaphoreType.DMA((n,)), pltpu.SemaphoreType.DMA((n,))])(*q)


def _rs_rows(hr):
    return 2 * AT if hr % (2 * AT) == 0 else AT


def _rs_add2(dw, got, scidx, name):
    _, r, cc = dw.shape
    hr = r // 2
    at = _rs_rows(hr)
    nb = hr // at

    def body(s_ref, a_ref, b_ref, o_ref, ob_ref):
        acc = a_ref[...] + b_ref[...].astype(F32)
        ob_ref[...] = acc.astype(BF16)

        @pl.when(pl.program_id(1) == s_ref[0])
        def _():
            o_ref[...] = acc[0]

    blk = pl.BlockSpec((1, at, cc), lambda i, sh, s_ref: (sh, i, 0))
    return pl.pallas_call(
        body, name=name,
        grid_spec=pltpu.PrefetchScalarGridSpec(
            num_scalar_prefetch=1, grid=(nb, 4),
            in_specs=[pl.BlockSpec((1, at, cc), lambda i, sh, s_ref: (sh, s_ref[1] * nb + i, 0)), blk],
            out_specs=[pl.BlockSpec((at, cc), lambda i, sh, s_ref: (i, 0)), blk]),
        out_shape=[jax.ShapeDtypeStruct((hr, cc), F32), jax.ShapeDtypeStruct((4, hr, cc), BF16)],
        compiler_params=_cp(("parallel", "arbitrary")))(scidx, dw, got)


def _rs_add4(p, got, scidx, name):
    hr, cc = p.shape
    at = _rs_rows(hr)
    nb = hr // at

    def body(s_ref, p_ref, g0_ref, g1_ref, g2_ref, o_ref):
        acc = p_ref[...] + g0_ref[0].astype(F32)
        acc = acc + g1_ref[0].astype(F32)
        o_ref[...] = acc + g2_ref[0].astype(F32)

    def gk(k):
        return pl.BlockSpec((1, at, cc), lambda i, s_ref: (k, i, 0))

    return pl.pallas_call(
        body, name=name,
        grid_spec=pltpu.PrefetchScalarGridSpec(
            num_scalar_prefetch=1, grid=(nb,),
            in_specs=[pl.BlockSpec((at, cc), lambda i, s_ref: (i, 0)), gk(0), gk(1), gk(2)],
            out_specs=pl.BlockSpec((at, cc), lambda i, s_ref: (s_ref[1] * nb + i, 0))),
        out_shape=jax.ShapeDtypeStruct((2 * hr, cc), F32),
        compiler_params=_cp(("parallel",)))(scidx, p, got, got, got)


def _rs_begin(dws, dwbs, after, tag=""):
    x, y, c = _my_pos()
    scidx = jnp.stack([2 * x + y, c]).astype(jnp.int32)
    got = _rs_pair(dwbs, "rs_pair" + tag)
    pairs = [_rs_add2(dws[w], got[w], scidx, "rs_add2") for w in range(len(dws))]
    pb = [p[1] for p in pairs]
    lands = [lax.empty((3,) + t.shape[1:], BF16) for t in pb]
    ssem, rsem, pb, lands, token = _split_start(pb, lands, after, _rs_copies, "rs_chip_start" + tag)
    return ([p[0] for p in pairs], ssem, rsem, pb, lands), token


def _rs_end(state, after, tag=""):
    x, y, c = _my_pos()
    scidx = jnp.stack([2 * x + y, c]).astype(jnp.int32)
    p, ssem, rsem, pb, lands = state
    _, recv = _split_wait(ssem, rsem, pb, lands, after, _rs_copies, "rs_chip_wait" + tag)
    q = [_rs_add4(p[w], recv[w], scidx, "rs_add4") for w in range(len(p))]
    return _rs_sib(q, "rs_sib" + tag)


def _allreduce_small(buf, name):
    rows = buf.shape[0]

    def body(src_ref, out_ref, gat_ref, send_sems, recv_sems):
        x, y, c = _my_pos()
        me = 4 * x + 2 * y + c
        gat_ref[me] = src_ref[...]
        cps = []
        for r in range(1, N_DEV):
            tx = 1 - x if (r >> 2) & 1 else x
            ty = 1 - y if (r >> 1) & 1 else y
            tc = 1 - c if r & 1 else c
            cps.append(pltpu.make_async_remote_copy(
                src_ref=src_ref, dst_ref=gat_ref.at[me], send_sem=send_sems.at[r - 1], recv_sem=recv_sems.at[r - 1],
                device_id=(tx, ty, tc), device_id_type=MESH))
        for cp in cps:
            cp.start()
        for cp in cps:
            cp.wait()
        acc = gat_ref[0]
        for k in range(1, N_DEV):
            acc = acc + gat_ref[k]
        out_ref[...] = acc

    return pl.pallas_call(
        body, name=name, in_specs=[VMEM_SPEC], out_specs=VMEM_SPEC, out_shape=jax.ShapeDtypeStruct((rows, 128), F32),
        scratch_shapes=[pltpu.VMEM((N_DEV, rows, 128), F32), pltpu.SemaphoreType.DMA((N_DEV - 1,)),
                        pltpu.SemaphoreType.DMA((N_DEV - 1,))],
        compiler_params=_cp())(buf)


SMALL = (("norm_mix_w", (D,)), ("ssd_conv_w", (4, XBC)), ("ssd_conv_b", (XBC,)), ("ssd_dt_bias", (NH,)),
         ("ssd_a_log", (NH,)), ("ssd_d", (NH,)), ("ssd_norm_w", (SSD_W,)), ("q_norm_w", (HD,)),
         ("k_norm_w", (HD,)), ("attn_sinks", (NQH,)), ("cm_dw_w", (CMK, CMC)), ("cm_dw_b", (CMC,)),
         ("cm_ln_w", (CMC,)), ("cm_ln_b", (CMC,)), ("norm_mlp_w", (D,)))
SHARDED_SMALL = ("ssd_conv_w", "cm_dw_w")


def _seg_len(shape):
    n = 1
    for d in shape:
        n *= d
    return -(-n // 128) * 128


def _pack_small(vals, names):
    parts = []
    for name, shape in SMALL:
        if name not in names:
            continue
        v = vals[name].reshape(DEPTH, -1)
        pad = _seg_len(shape) - v.shape[1]
        parts.append(jnp.pad(v, ((0, 0), (0, pad))))
    flat = jnp.concatenate(parts, axis=1)
    return flat.reshape(-1, 128)


def _unpack_small(buf, names):
    flat = buf.reshape(DEPTH, -1)
    out = {}
    off = 0
    for name, shape in SMALL:
        if name not in names:
            continue
        n = 1
        for d in shape:
            n *= d
        out[name] = flat[:, off:off + n].reshape((DEPTH,) + shape)
        off += _seg_len(shape)
    return out


SW = N_IN // 4
SWP = 1152
ORIG = (("z", 0, 1024), ("x", 1024, 2560), ("dt", 2560, 2576), ("q", 2576, 3088), ("k", 3088, 3216),
        ("v", 3216, 3344), ("a", 3344, 3856), ("g", 3856, 4368))


def _orig_cols(g_in, lo, hi):
    out = []
    for s in range(4):
        a, b = max(lo, s * SW), min(hi, (s + 1) * SW)
        if a < b:
            out.append(g_in[s][:, a - s * SW:b - s * SW])
    return out


def _shard_major(parts):
    cols = []
    for s in range(4):
        for name, g0, g1 in ORIG:
            a, b = max(g0, s * SW), min(g1, (s + 1) * SW)
            if a < b:
                cols.append(parts[name][:, a - g0:b - g0])
        cols.append(jnp.zeros((L, SWP - SW), BF16))
    return jnp.concatenate(cols, axis=1)


def _rope_tables():
    inv = 10000.0 ** (-jnp.arange(0, HD, 2, dtype=F32) / HD)
    ang = jnp.arange(L, dtype=F32)[:, None] * inv[None, :]
    return jnp.cos(ang), jnp.sin(ang)


def _swa_tables():
    cos, sin = _rope_tables()
    return jnp.tile(cos, (1, 4)), jnp.tile(jnp.concatenate([-sin, sin], axis=1), (1, 2))


def _swa_weights(w):
    return jnp.tile(w, 2)[None], jnp.tile(jnp.concatenate([w[HH:], w[:HH]]), 2)[None]


def _swa_sinks(s):
    s2 = s.reshape(NPAIR, 2)
    return (jnp.broadcast_to(s2[:, 0][:, None, None], (NPAIR, 1, 128)),
            jnp.broadcast_to(s2[:, 1][:, None, None], (NPAIR, 1, 128)))


def _swa_blockdiag():
    i = jnp.arange(128)
    return (i[:, None] // HD == i[None, :] // HD).astype(BF16)


def _pad128(v):
    return jnp.pad(v, (0, 128 - v.shape[0]))[None, :]


def kernel(x, norm_mix_w, w_in, ssd_conv_w, ssd_conv_b, ssd_dt_bias, ssd_a_log, ssd_d, ssd_norm_w, q_norm_w, k_norm_w, attn_sinks, cm_dw_w, cm_dw_b, cm_ln_w, cm_ln_b, w_out, norm_mlp_w, w_mlp_up, w_mlp_down, loss_target, m_norm_mix_w, m_w_in, m_ssd_conv_w, m_ssd_conv_b, m_ssd_dt_bias, m_ssd_a_log, m_ssd_d, m_ssd_norm_w, m_q_norm_w, m_k_norm_w, m_attn_sinks, m_cm_dw_w, m_cm_dw_b, m_cm_ln_w, m_cm_ln_b, m_w_out, m_norm_mlp_w, m_w_mlp_up, m_w_mlp_down, v_norm_mix_w, v_w_in, v_ssd_conv_w, v_ssd_conv_b, v_ssd_dt_bias, v_ssd_a_log, v_ssd_d, v_ssd_norm_w, v_q_norm_w, v_k_norm_w, v_attn_sinks, v_cm_dw_w, v_cm_dw_b, v_cm_ln_w, v_cm_ln_b, v_w_out, v_norm_mlp_w, v_w_mlp_up, v_w_mlp_down):
    px, py, pc = _my_pos()
    shard = 2 * px + py
    sidx = jnp.reshape(shard, (1,)).astype(jnp.int32)
    consts = _ssd_consts()
    cos4, sin4 = _swa_tables()
    bd = _swa_blockdiag()

    big_w = (w_in, w_out, w_mlp_up, w_mlp_down)

    def own_shard(w, l):
        t = big_w[w][l]
        if w == 0:
            t = jnp.pad(t, ((0, 0), (0, SWP - SW)))
        return t.astype(BF16)

    def gather_start(l, sel, after, tag=""):
        own = [own_shard(w, l) for w in sel]
        lands = [lax.empty((4,) + t.shape, BF16) for t in own]
        return _split_start(own, lands, after, _ag_copies(False), "ag_start" + tag)

    def gather_finish(in_flight, after, tag=""):
        ssem, rsem, own, lands, _ = in_flight
        own, lands = _split_wait(ssem, rsem, own, lands, after, _ag_copies(True), "ag_wait" + tag)
        lands = _ag_forward(lands, "ag_forward" + tag)
        return [_place_own(o, g, sidx, "ag_place") for o, g in zip(own, lands)]

    zero_tile = jnp.zeros((8, 128), F32)
    first_in = gather_start(0, [0], zero_tile, "_in0")
    zc = jnp.zeros((DEPTH, 4, XBC), F32)
    zc = lax.dynamic_update_slice_in_dim(zc, ssd_conv_w, shard * (XBC // 4), axis=2)
    zd = jnp.zeros((DEPTH, CMK, CMC), F32)
    zd = lax.dynamic_update_slice_in_dim(zd, cm_dw_w, shard * (CMC // 4), axis=2)
    half = jnp.where(pc == 0, 1.0, 0.0).astype(F32)
    gw_packed = _allreduce_small(_pack_small({"ssd_conv_w": zc * half, "cm_dw_w": zd * half}, SHARDED_SMALL)
                                 + first_in[4][0:1, 0:1], "ag_small")
    first_rest = gather_start(0, [1, 2, 3], gw_packed, "_rest0")
    gw = _unpack_small(gw_packed, SHARDED_SMALL)
    conv_w_full, dw_w_full = gw["ssd_conv_w"], gw["cm_dw_w"]

    xcur = x[0]
    saved = []
    in_flight = None
    for l in range(DEPTH):
        if l == 0:
            (g_in,) = gather_finish(first_in, first_rest[4], "_in0")
            nxt_after = g_in
        else:
            g_in, g_out, g_up, g_dn = gather_finish(in_flight, xcur)
            nxt_after = g_dn
        nmw = norm_mix_w[l][None]
        if l + 1 < DEPTH:
            in_flight = gather_start(l + 1, [0, 1, 2, 3], nxt_after)
            nmw = nmw + in_flight[4][0:1, 0:1]
        grp = dict((n, (a, b)) for n, a, b in ORIG)
        w_perm = jnp.concatenate(
            _orig_cols(g_in, *grp["x"]) + _orig_cols(g_in, *grp["z"]) + _orig_cols(g_in, grp["a"][0], grp["g"][1])
            + _orig_cols(g_in, grp["q"][0], grp["v"][1]) + _orig_cols(g_in, *grp["dt"])
            + [jnp.zeros((D, 128 - NH), BF16)], axis=1)
        h = _rms_fwd(xcur, nmw, "rms_mix_fwd")
        u = _mm(h, w_perm, "nn", "in_proj", tn=640)
        alog = _pad128(ssd_a_log[l])
        dtb = _pad128(ssd_dt_bias[l])
        dxp = jnp.repeat(ssd_d[l], HP)[None, :]
        ssd_p = (conv_w_full[l], ssd_conv_b[l][None], dtb, alog, dxp, ssd_norm_w[l][None])
        ya, ypre, st = _ssd_fwd(u, *ssd_p, consts, "ssd_fwd")
        swa_p = (cos4, sin4, *_swa_weights(q_norm_w[l]), *_swa_weights(k_norm_w[l]), *_swa_sinks(attn_sinks[l]), bd)
        yb = _swa_fwd(u, *swa_p, "attn_fwd")
        cconv = _conf_conv_fwd(u, dw_w_full[l], cm_dw_b[l][None], "conf_conv_fwd")
        conf_p = (cconv, cm_ln_w[l][None], cm_ln_b[l][None])
        yc = _conf_ln_fwd(*conf_p, "conf_ln_fwd")
        if l == 0:
            g_out, g_up, g_dn = gather_finish(first_rest, yc, "_rest0")
        g_out = g_out.reshape(2 * D, D)
        g_dn = g_dn.reshape(DFF, D)
        ycat = jnp.concatenate([ya, yb, yc], axis=1)
        x1 = _mm(ycat, g_out, "nn", "out_proj", add=xcur)
        hm = _rms_fwd(x1, norm_mlp_w[l][None], "rms_mlp_fwd")
        r_up = _mm_up(hm, g_up, "mlp_up", tn=DFF // 4)
        x2 = _mm(r_up, g_dn, "nn", "mlp_down", add=x1)
        saved.append(dict(x=xcur, h=h, u=u, ypre=ypre, st=st, swa_p=swa_p, conf_p=conf_p, ycat=ycat, x1=x1,
                          hm=hm, r_up=r_up, ssd_p=ssd_p, g_in=g_in, g_out=g_out, g_up=g_up, g_dn=g_dn))
        xcur = x2

    lsum, dx, dxb = _loss_bwd(xcur, loss_target[0], "loss")

    loc = locals()
    names = ["norm_mix_w", "w_in", "ssd_conv_w", "ssd_conv_b", "ssd_dt_bias", "ssd_a_log", "ssd_d", "ssd_norm_w",
             "q_norm_w", "k_norm_w", "attn_sinks", "cm_dw_w", "cm_dw_b", "cm_ln_w", "cm_ln_b", "w_out", "norm_mlp_w",
             "w_mlp_up", "w_mlp_down"]
    weights = {n: loc[n] for n in names}
    moms = {n: loc["m_" + n] for n in names}
    vars_ = {n: loc["v_" + n] for n in names}
    big_out = {n: None for n in BIG}
    win_grads = [None] * DEPTH

    def finish_layer(layer, which, shard_grads):
        for n, g in zip(which, shard_grads):
            if n == "w_in":
                win_grads[layer] = g
            else:
                big_out[n] = _adamw_layer(weights[n], g, moms[n], vars_[n], layer, big_out[n], "adamw_" + n)

    pending = None
    last_mlp = None
    gsm = {name: [] for name, _ in SMALL}
    for l in reversed(range(DEPTH)):
        sv = saved[l]
        da = _mm(dxb, sv["g_dn"], "nt", "mlp_down_dx", relu2_of=sv["r_up"], tn=D)
        dwdn, dwdn_b = _mm_dw(sv["r_up"], dxb, "mlp_down_dw")
        dwup, dwup_b = _mm_dw(sv["hm"], da, "mlp_up_dw", col_shards=True)
        dhm = _mm_cs_nt(da, sv["g_up"], "mlp_up_dx")
        nlw = norm_mlp_w[l][None]
        if l == 0:
            last_mlp, token = _rs_begin([dwup, dwdn.reshape(4, D, D)], [dwup_b, dwdn_b.reshape(4, D, D)], zero_tile,
                                        "_mlp0")
            nlw = nlw + token[0:1, 0:1]
        dx1, dx1b, dnw = _rms_bwd(sv["x1"], nlw, dhm, dx, "rms_mlp_bwd")
        gsm["norm_mlp_w"].append(dnw[0])
        dy = _mm(dx1b, sv["g_out"], "nt", "out_proj_dx", tn=D)
        dwout, dwout_b = _mm_dw(sv["ycat"], dx1b, "out_proj_dw")
        dcc, dwb, dlw, dlb = _conf_ln_bwd(*sv["conf_p"], dy, "conf_ln_bwd")
        da_c, dg_c, dww = _conf_conv_bwd(sv["u"], dcc, dw_w_full[l], "conf_conv_bwd")
        gsm["cm_dw_w"].append(dww[:CMK])
        gsm["cm_dw_b"].append(dwb[0])
        gsm["cm_ln_w"].append(dlw[0])
        gsm["cm_ln_b"].append(dlb[0])
        dq, dk, dv, dqw, dkw, dse, dso = _swa_bwd(sv["u"], dy, *sv["swa_p"], "attn_bwd")
        gsm["q_norm_w"].append(dqw[0, :HD] + dqw[0, HD:])
        gsm["k_norm_w"].append(dkw[0, :HD] + dkw[0, HD:])
        gsm["attn_sinks"].append(jnp.stack([dse[:, 0, 0], dso[:, 0, 0]], axis=1).reshape(NQH))
        (dz, dxr, ddtr, dcw, dcb, ddtb, dalog, ddd, dnsw) = _ssd_bwd(
            sv["u"], sv["ypre"], sv["st"], dy, *sv["ssd_p"], consts, "ssd_bwd")
        gsm["ssd_conv_w"].append(dcw[:4])
        gsm["ssd_conv_b"].append(dcb[0])
        gsm["ssd_dt_bias"].append(ddtb[0, :NH])
        gsm["ssd_a_log"].append(dalog[0, :NH])
        gsm["ssd_d"].append(ddd[0, :NH])
        gsm["ssd_norm_w"].append(dnsw[0])
        du = _shard_major(dict(z=dz, x=dxr, dt=ddtr[:, :NH], q=dq, k=dk, v=dv, a=da_c, g=dg_c))
        dwin, dwin_b = _mm_dw(sv["h"], du, "in_dw", col_shards=True, tn=SWP // 3)
        if l == 0:
            state, token = _rs_begin([dwin, dwout.reshape(4, D // 2, D)], [dwin_b, dwout_b.reshape(4, D // 2, D)],
                                     zero_tile, "_io0")
        else:
            state, token = _rs_begin(
                [dwin, dwout.reshape(4, D // 2, D), dwup, dwdn.reshape(4, D, D)],
                [dwin_b, dwout_b.reshape(4, D // 2, D), dwup_b, dwdn_b.reshape(4, D, D)], zero_tile)
        dh = _mm_cs_nt(du, sv["g_in"], "in_dx")
        dx, dxb, dnm = _rms_bwd(sv["x"], norm_mix_w[l][None] + token[0:1, 0:1], dh, dx1, "rms_mix_bwd")
        gsm["norm_mix_w"].append(dnm[0])
        if pending is not None:
            finish_layer(l + 1, BIG, _rs_end(pending, dx))
        pending = state

    finish_layer(0, BIG[2:], _rs_end(last_mlp, dx, "_mlp0"))
    gsm = {k: jnp.stack(v[::-1]) for k, v in gsm.items()}
    packed = _pack_small(gsm, [n for n, _ in SMALL])
    packed = jnp.concatenate([packed, lsum], axis=0)
    red = _allreduce_small(packed, "ar_small")
    finish_layer(0, BIG[:2], _rs_end(pending, red, "_io0"))
    loss = 0.5 * red[-8, 0] / D
    gsm = _unpack_small(red[:-8], [n for n, _ in SMALL])
    gsm["ssd_conv_w"] = lax.dynamic_slice_in_dim(gsm["ssd_conv_w"], shard * (XBC // 4), XBC // 4, axis=2)
    gsm["cm_dw_w"] = lax.dynamic_slice_in_dim(gsm["cm_dw_w"], shard * (CMC // 4), CMC // 4, axis=2)
    grads = dict(gsm)
    delta, new_m, new_v = {}, {}, {}
    for n in BIG[1:]:
        grads[n], delta[n], new_m[n], new_v[n] = big_out[n]
    to_lead = lambda t: jnp.transpose(t, (2, 0, 1))
    g_lead = jnp.stack([jnp.transpose(g[:, :SW]) for g in win_grads], axis=1)
    d_lead, m_lead, v_lead = _adamw_lead(to_lead(w_in), g_lead, to_lead(m_w_in), to_lead(v_w_in), "adamw_w_in")
    from_lead = lambda t: jnp.transpose(t, (1, 2, 0))
    grads["w_in"], delta["w_in"] = from_lead(g_lead), from_lead(d_lead)
    new_m["w_in"], new_v["w_in"] = from_lead(m_lead), from_lead(v_lead)

    packed_names = [n for n, _ in SMALL if n not in SHARDED_SMALL]
    pw = _pack_small(weights, packed_names)
    pg = _pack_small(grads, packed_names)
    pm = _pack_small(moms, packed_names)
    pv = _pack_small(vars_, packed_names)
    pd, pmn, pvn = _adamw(pw, pg, pm, pv, "adamw_small")
    for dst, buf in ((delta, pd), (new_m, pmn), (new_v, pvn)):
        dst.update(_unpack_small(buf, packed_names))
    for n in SHARDED_SMALL:
        shp = weights[n].shape
        flat = lambda t: t.reshape(-1, shp[-1])
        d_, m_, v_ = _adamw(flat(weights[n]), flat(grads[n]), flat(moms[n]), flat(vars_[n]), "adamw_" + n)
        delta[n], new_m[n], new_v[n] = d_.reshape(shp), m_.reshape(shp), v_.reshape(shp)

    return (loss, dx[None], *[grads[n] for n in names], *[delta[n] for n in names],
            *[new_m[n] for n in names], *[new_v[n] for n in names])
```

```python
import functools
import math

import jax
import jax.numpy as jnp
from jax import lax
from jax.experimental import pallas as pl
from jax.experimental.pallas import tpu as pltpu

F32 = jnp.float32
BF16 = jnp.bfloat16
MESH = pl.DeviceIdType.MESH
ANY = pl.BlockSpec(memory_space=pl.ANY)
VMEM_SPEC = pl.BlockSpec(memory_space=pltpu.VMEM)

D = 1024
L = 2048
DEPTH = 4
SSD_W = 1024
XBC = 1536
NH = 16
HP = 64
NS = 128
Q = 128
NC = L // Q
ATT_W = 512
NQH = 8
NKV = 2
HD = 64
HH = HD // 2
CMC = 512
CMK = 31
DFF = 4096
N_IN = 4368
N_PAD = 4480
RMS_EPS = 1e-6
LN_EPS = 1e-5
NEG = -1e30
LR, B1, B2, EPS_A, WD, STEP = 0.001, 0.9, 0.999, 1e-8, 0.01, 10
VMEM_LIMIT = 56 * 1024 * 1024
N_DEV = 8


def _cp(sem=None):
    kw = dict(vmem_limit_bytes=VMEM_LIMIT)
    if sem is not None:
        kw["dimension_semantics"] = sem
    return pltpu.CompilerParams(**kw)


def _dg(a, b, ca, cb):
    return lax.dot_general(a, b, (((ca,), (cb,)), ((), ())), preferred_element_type=F32)


def _split3(x):
    hi = x.astype(BF16)
    r = x - hi.astype(F32)
    mid = r.astype(BF16)
    lo = (r - mid.astype(F32)).astype(BF16)
    return hi, mid, lo


def _xdot_l(x, m, ca=1, cb=0):
    hi, mid, lo = _split3(x)
    return _dg(hi, m, ca, cb) + _dg(mid, m, ca, cb) + _dg(lo, m, ca, cb)


def _xdot_r(m, x, ca=1, cb=0):
    hi, mid, lo = _split3(x)
    return _dg(m, hi, ca, cb) + _dg(m, mid, ca, cb) + _dg(m, lo, ca, cb)


def _rowdot(v, m):
    return _xdot_l(jnp.broadcast_to(v, (8, v.shape[1])), m)[0:1]


def _sigmoid(x):
    return 1.0 / (1.0 + jnp.exp(-x))


def _softplus(x):
    e = jnp.exp(-jnp.abs(x))
    u = 1.0 + e
    l1p = jnp.where(u == 1.0, e, jnp.log(u) * (e / jnp.where(u == 1.0, 1.0, u - 1.0)))
    return jnp.maximum(x, 0.0) + l1p


def _tm(k):
    return L if k <= D else L // 2


def _mm(a, b, mode, name, add=None, relu2_of=None, tn=512):
    m, k = a.shape
    tm = min(m, _tm(k))
    a_spec = pl.BlockSpec((tm, k), lambda i, j: (i, 0))
    if mode == "nn":
        n = b.shape[1]
        b_spec = pl.BlockSpec((k, tn), lambda i, j: (0, j))
        cb = 0
    else:
        n = b.shape[0]
        b_spec = pl.BlockSpec((tn, k), lambda i, j: (j, 0))
        cb = 1
    assert m % tm == 0 and n % tn == 0, (m, n, tm, tn)
    o_spec = pl.BlockSpec((tm, tn), lambda i, j: (i, j))
    out_dtype = F32
    if relu2_of is not None:
        def body(a_ref, b_ref, c_ref, o_ref):
            o_ref[...] = (_dg(a_ref[...], b_ref[...], 1, cb) * (2.0 * jnp.sqrt(c_ref[...].astype(F32)))).astype(BF16)
        ins, specs, out_dtype = (a, b, relu2_of), [a_spec, b_spec, o_spec], BF16
    elif add is None:
        def body(a_ref, b_ref, o_ref):
            o_ref[...] = _dg(a_ref[...], b_ref[...], 1, cb)
        ins, specs = (a, b), [a_spec, b_spec]
    else:
        def body(a_ref, b_ref, c_ref, o_ref):
            o_ref[...] = _dg(a_ref[...], b_ref[...], 1, cb) + c_ref[...]
        ins, specs = (a, b, add), [a_spec, b_spec, o_spec]
    return pl.pallas_call(
        body, name=name, grid=(m // tm, n // tn), in_specs=specs, out_specs=o_spec,
        out_shape=jax.ShapeDtypeStruct((m, n), out_dtype), compiler_params=_cp(("parallel", "parallel")))(*ins)


def _mm_up(a, b, name, tn=512):
    m = a.shape[0]
    cs = DFF // 4
    per = cs // tn
    tm = min(m, _tm(D))

    def body(a_ref, b_ref, r_ref):
        r = jnp.maximum(_dg(a_ref[...], b_ref[0], 1, 0), 0.0)
        r_ref[...] = (r * r).astype(BF16)

    return pl.pallas_call(
        body, name=name, grid=(m // tm, DFF // tn),
        in_specs=[pl.BlockSpec((tm, D), lambda i, j: (i, 0)),
                  pl.BlockSpec((1, D, tn), lambda i, j: (j // per, 0, j % per))],
        out_specs=pl.BlockSpec((tm, tn), lambda i, j: (i, j)),
        out_shape=jax.ShapeDtypeStruct((m, DFF), BF16), compiler_params=_cp(("parallel", "parallel")))(a, b)


def _mm_cs_nt(a, b, name, tn=512):
    m = a.shape[0]
    _, n, cs = b.shape
    tm = min(m, _tm(4 * cs))

    def body(a_ref, b_ref, o_ref):
        acc = _dg(a_ref[:, 0:cs], b_ref[0], 1, 1)
        for s in range(1, 4):
            acc = acc + _dg(a_ref[:, s * cs:(s + 1) * cs], b_ref[s], 1, 1)
        o_ref[...] = acc

    return pl.pallas_call(
        body, name=name, grid=(m // tm, n // tn),
        in_specs=[pl.BlockSpec((tm, 4 * cs), lambda i, j: (i, 0)), pl.BlockSpec((4, tn, cs), lambda i, j: (0, j, 0))],
        out_specs=pl.BlockSpec((tm, tn), lambda i, j: (i, j)),
        out_shape=jax.ShapeDtypeStruct((m, n), F32), compiler_params=_cp(("parallel", "parallel")))(a, b)


def _mm_dw(a, b, name, col_shards=False, tn=512):
    k, m = a.shape
    n = b.shape[1]
    tm = min(m, D)
    a_spec = pl.BlockSpec((k, tm), lambda i, j: (0, i))
    b_spec = pl.BlockSpec((k, tn), lambda i, j: (0, j))
    if col_shards:
        per = (n // 4) // tn
        o_spec = pl.BlockSpec((1, tm, tn), lambda i, j: (j // per, i, j % per))
        shape = (4, m, n // 4)
    else:
        o_spec = pl.BlockSpec((tm, tn), lambda i, j: (i, j))
        shape = (m, n)

    def body(a_ref, b_ref, o_ref, ob_ref):
        acc = _dg(a_ref[...], b_ref[...], 0, 0).reshape(o_ref.shape)
        o_ref[...] = acc
        ob_ref[...] = acc.astype(BF16)

    return pl.pallas_call(
        body, name=name, grid=(m // tm, n // tn), in_specs=[a_spec, b_spec], out_specs=[o_spec, o_spec],
        out_shape=[jax.ShapeDtypeStruct(shape, F32), jax.ShapeDtypeStruct(shape, BF16)],
        compiler_params=_cp(("parallel", "parallel")))(a, b)


TR = 256


def _rms_fwd(x, w, name):
    def body(x_ref, w_ref, o_ref):
        xv = x_ref[...]
        r = lax.rsqrt(jnp.mean(xv * xv, axis=-1, keepdims=True) + RMS_EPS)
        o_ref[...] = (xv * r * w_ref[...]).astype(BF16)

    return pl.pallas_call(
        body, name=name, grid=(L // TR,),
        in_specs=[pl.BlockSpec((TR, D), lambda i: (i, 0)), pl.BlockSpec((1, D), lambda i: (0, 0))],
        out_specs=pl.BlockSpec((TR, D), lambda i: (i, 0)),
        out_shape=jax.ShapeDtypeStruct((L, D), BF16), compiler_params=_cp(("parallel",)))(x, w)


def _rms_bwd(x, w, dh, dres, name):
    def body(x_ref, w_ref, dh_ref, dr_ref, dx_ref, dxb_ref, dw_ref):
        xv = x_ref[...]
        r = lax.rsqrt(jnp.mean(xv * xv, axis=-1, keepdims=True) + RMS_EPS)
        n = xv * r
        dhv = dh_ref[...]
        g = dhv * w_ref[...]
        dx = dr_ref[...] + r * (g - n * jnp.mean(g * n, axis=-1, keepdims=True))
        dx_ref[...] = dx
        dxb_ref[...] = dx.astype(BF16)

        @pl.when(pl.program_id(0) == 0)
        def _():
            dw_ref[...] = jnp.zeros_like(dw_ref)
        dw_ref[...] += jnp.sum(dhv * n, axis=0, keepdims=True)

    row = pl.BlockSpec((TR, D), lambda i: (i, 0))
    vec = pl.BlockSpec((1, D), lambda i: (0, 0))
    return pl.pallas_call(
        body, name=name, grid=(L // TR,), in_specs=[row, vec, row, row], out_specs=[row, row, vec],
        out_shape=[jax.ShapeDtypeStruct((L, D), F32), jax.ShapeDtypeStruct((L, D), BF16),
                   jax.ShapeDtypeStruct((1, D), F32)],
        compiler_params=_cp(("arbitrary",)))(x, w, dh, dres)


def _adamw_lead(w, g, m, v, name):
    lead, a, b = w.shape
    tr = max(t for t in range(1, lead + 1) if lead % t == 0 and t * a * b * 4 <= 2 * 1024 * 1024)
    c1 = 1.0 / (1.0 - B1 ** STEP)
    c2 = 1.0 / (1.0 - B2 ** STEP)

    def body(w_ref, g_ref, m_ref, v_ref, d_ref, mo_ref, vo_ref):
        gv = g_ref[...]
        mn = B1 * m_ref[...] + (1.0 - B1) * gv
        vn = B2 * v_ref[...] + (1.0 - B2) * (gv * gv)
        mo_ref[...] = mn
        vo_ref[...] = vn
        d_ref[...] = -LR * ((mn * c1) / (jnp.sqrt(vn * c2) + EPS_A) + WD * w_ref[...])

    blk = pl.BlockSpec((tr, a, b), lambda i: (i, 0, 0))
    shp = jax.ShapeDtypeStruct(w.shape, F32)
    return pl.pallas_call(body, name=name, grid=(lead // tr,), in_specs=[blk] * 4, out_specs=[blk] * 3,
                          out_shape=[shp] * 3, compiler_params=_cp(("parallel",)))(w, g, m, v)


def _adamw_layer(w, g, m, v, layer, prev, name):
    _, rows, cols = w.shape
    tr = 512 if rows % 512 == 0 else 256
    c1 = 1.0 / (1.0 - B1 ** STEP)
    c2 = 1.0 / (1.0 - B2 ** STEP)
    n_prev = 0 if prev is None else 4

    def body(*refs):
        w_ref, g_ref, m_ref, v_ref = refs[:4]
        go_ref, d_ref, mo_ref, vo_ref = refs[4 + n_prev:]
        gv = g_ref[...]
        mn = B1 * m_ref[0] + (1.0 - B1) * gv
        vn = B2 * v_ref[0] + (1.0 - B2) * (gv * gv)
        go_ref[0] = gv
        mo_ref[0] = mn
        vo_ref[0] = vn
        d_ref[0] = -LR * ((mn * c1) / (jnp.sqrt(vn * c2) + EPS_A) + WD * w_ref[0])

    lay = pl.BlockSpec((1, tr, cols), lambda i: (layer, i, 0))
    shp = jax.ShapeDtypeStruct(w.shape, F32)
    return pl.pallas_call(
        body, name=name, grid=(rows // tr,),
        in_specs=[lay, pl.BlockSpec((tr, cols), lambda i: (i, 0)), lay, lay] + [ANY] * n_prev,
        out_specs=[lay] * 4, out_shape=[shp] * 4,
        input_output_aliases={4 + i: i for i in range(n_prev)},
        compiler_params=_cp(("parallel",)))(w, g, m, v, *(prev or ()))


def _loss_bwd(y, t, name):
    def body(y_ref, t_ref, l_ref, d_ref, db_ref):
        e = y_ref[...] - t_ref[...]
        d = e * (1.0 / D)
        d_ref[...] = d
        db_ref[...] = d.astype(BF16)

        @pl.when(pl.program_id(0) == 0)
        def _():
            l_ref[...] = jnp.zeros_like(l_ref)
        s = jnp.sum(jnp.sum(e * e, axis=-1, keepdims=True), axis=0, keepdims=True)
        l_ref[...] += jnp.broadcast_to(s, l_ref.shape)

    row = pl.BlockSpec((TR, D), lambda i: (i, 0))
    tile = pl.BlockSpec((8, 128), lambda i: (0, 0))
    return pl.pallas_call(
        body, name=name, grid=(L // TR,), in_specs=[row, row], out_specs=[tile, row, row],
        out_shape=[jax.ShapeDtypeStruct((8, 128), F32), jax.ShapeDtypeStruct((L, D), F32),
                   jax.ShapeDtypeStruct((L, D), BF16)],
        compiler_params=_cp(("arbitrary",)))(y, t)


def _adamw(w, g, m, v, name):
    rows, cols = w.shape
    tr = rows
    for cand in (512, 256, 128, 64, 32, 16, 8):
        if rows % cand == 0 and cand * cols * 4 <= 2 * 1024 * 1024:
            tr = cand
            break
    c1 = 1.0 / (1.0 - B1 ** STEP)
    c2 = 1.0 / (1.0 - B2 ** STEP)

    def body(w_ref, g_ref, m_ref, v_ref, d_ref, mo_ref, vo_ref):
        gv = g_ref[...]
        mn = B1 * m_ref[...] + (1.0 - B1) * gv
        vn = B2 * v_ref[...] + (1.0 - B2) * (gv * gv)
        mo_ref[...] = mn
        vo_ref[...] = vn
        d_ref[...] = -LR * ((mn * c1) / (jnp.sqrt(vn * c2) + EPS_A) + WD * w_ref[...])

    blk = pl.BlockSpec((tr, cols), lambda i: (i, 0))
    shp = jax.ShapeDtypeStruct((rows, cols), F32)
    return pl.pallas_call(body, name=name, grid=(rows // tr,), in_specs=[blk] * 4, out_specs=[blk] * 3,
                          out_shape=[shp] * 3, compiler_params=_cp(("parallel",)))(w, g, m, v)


CT = 256
CPAD = 32


U_X, U_Z, U_A, U_G, U_Q, U_K, U_V, U_DT = 0, 1536, 2560, 3072, 3584, 4096, 4224, 4352


CEXT = 8
CWIN = CT + CPAD
CROWS = L + CPAD + CEXT


def _fill_shifted(src_ref, base, win_ref, sh_ref):
    win_ref[...] = src_ref[pl.ds(base, CWIN + CEXT), :]
    for p in range(8):
        sh_ref[p] = win_ref[pl.ds(p, CWIN), :]


def _tap(sh_ref, o):
    return sh_ref[o % 8, 8 * (o // 8):8 * (o // 8) + CT, :]


CB = 128


def _conv_specs():
    return [pl.BlockSpec((L, CB), lambda j: (0, U_A // CB + j)), pl.BlockSpec((L, CB), lambda j: (0, U_G // CB + j))]


def _conv_scratch(n_padded):
    return ([pltpu.VMEM((CROWS, CB), F32)] * n_padded
            + [pltpu.VMEM((CWIN + CEXT, CB), F32), pltpu.VMEM((8, CWIN, CB), F32)])


def _fill_gated(a_ref, g_ref, hp_ref):
    hp_ref[0:CPAD, :] = jnp.zeros((CPAD, CB), F32)
    hp_ref[CPAD:CPAD + L, :] = a_ref[...] * _sigmoid(g_ref[...])
    hp_ref[CPAD + L:, :] = jnp.zeros((CEXT, CB), F32)


def _conf_conv_fwd(u, w, b, name):
    def body(a_ref, g_ref, w_ref, b_ref, c_ref, hp_ref, win_ref, sh_ref):
        _fill_gated(a_ref, g_ref, hp_ref)

        def tile(i, carry):
            base = pl.multiple_of(i * CT, CT)
            _fill_shifted(hp_ref, base, win_ref, sh_ref)
            c = jnp.broadcast_to(b_ref[...], (CT, CB))
            for k in range(CMK):
                c = c + w_ref[k:k + 1, :] * _tap(sh_ref, 2 + k)
            c_ref[pl.ds(base, CT), :] = c
            return carry

        lax.fori_loop(0, L // CT, tile, 0)

    return pl.pallas_call(
        body, name=name, grid=(CMC // CB,),
        in_specs=_conv_specs() + [pl.BlockSpec((CMK, CB), lambda j: (0, j)), pl.BlockSpec((1, CB), lambda j: (0, j))],
        out_specs=pl.BlockSpec((L, CB), lambda j: (0, j)),
        out_shape=jax.ShapeDtypeStruct((L, CMC), F32), scratch_shapes=_conv_scratch(1),
        compiler_params=_cp(("parallel",)))(u, u, w, b)


def _conf_ln_fwd(c, lw, lb, name):
    def body(c_ref, lw_ref, lb_ref, o_ref):
        cv = c_ref[...]
        cc = cv - jnp.mean(cv, axis=-1, keepdims=True)
        var = jnp.mean(cc * cc, axis=-1, keepdims=True)
        l = cc * lax.rsqrt(var + LN_EPS) * lw_ref[...] + lb_ref[...]
        o_ref[...] = (l * _sigmoid(l)).astype(BF16)

    row = pl.BlockSpec((TR, CMC), lambda i: (i, 0))
    vec = pl.BlockSpec((1, CMC), lambda i: (0, 0))
    return pl.pallas_call(body, name=name, grid=(L // TR,), in_specs=[row, vec, vec], out_specs=row,
                          out_shape=jax.ShapeDtypeStruct((L, CMC), BF16),
                          compiler_params=_cp(("parallel",)))(c, lw, lb)


def _conf_ln_bwd(c, lw, lb, dy, name):
    def body(c_ref, lw_ref, lb_ref, dy_ref, dc_ref, db_ref, dlw_ref, dlb_ref):
        cv = c_ref[...]
        cc = cv - jnp.mean(cv, axis=-1, keepdims=True)
        var = jnp.mean(cc * cc, axis=-1, keepdims=True)
        rstd = lax.rsqrt(var + LN_EPS)
        n = cc * rstd
        l = n * lw_ref[...] + lb_ref[...]
        sl = _sigmoid(l)
        dl = dy_ref[...] * (sl * (1.0 + l * (1.0 - sl)))
        dn = dl * lw_ref[...]
        dc = rstd * (dn - jnp.mean(dn, axis=-1, keepdims=True) - n * jnp.mean(dn * n, axis=-1, keepdims=True))
        dc_ref[...] = dc

        @pl.when(pl.program_id(0) == 0)
        def _():
            db_ref[...] = jnp.zeros_like(db_ref)
            dlw_ref[...] = jnp.zeros_like(dlw_ref)
            dlb_ref[...] = jnp.zeros_like(dlb_ref)
        db_ref[...] += jnp.sum(dc, axis=0, keepdims=True)
        dlw_ref[...] += jnp.sum(dl * n, axis=0, keepdims=True)
        dlb_ref[...] += jnp.sum(dl, axis=0, keepdims=True)

    row = pl.BlockSpec((TR, CMC), lambda i: (i, 0))
    vec = pl.BlockSpec((1, CMC), lambda i: (0, 0))
    vshape = jax.ShapeDtypeStruct((1, CMC), F32)
    return pl.pallas_call(
        body, name=name, grid=(L // TR,),
        in_specs=[row, vec, vec, pl.BlockSpec((TR, CMC), lambda i: (i, (SSD_W + ATT_W) // CMC))],
        out_specs=[row, vec, vec, vec], out_shape=[jax.ShapeDtypeStruct((L, CMC), F32), vshape, vshape, vshape],
        compiler_params=_cp(("arbitrary",)))(c, lw, lb, dy)


def _conf_conv_bwd(u, dc, w, name):
    def body(a_ref, g_ref, dc_ref, w_ref, da_ref, dg_ref, dw_ref, hp_ref, dcp_ref, win_ref, sh_ref, dwacc_ref):
        _fill_gated(a_ref, g_ref, hp_ref)
        dcp_ref[0:L, :] = dc_ref[...]
        dcp_ref[L:, :] = jnp.zeros((CPAD + CEXT, CB), F32)
        dwacc_ref[...] = jnp.zeros_like(dwacc_ref)

        def tile(i, carry):
            base = pl.multiple_of(i * CT, CT)
            _fill_shifted(hp_ref, base, win_ref, sh_ref)
            dcv = dcp_ref[pl.ds(base, CT), :]
            for k in range(CMK):
                dwacc_ref[k] += (dcv * _tap(sh_ref, 2 + k)).reshape(CT // 8, 8, CB).sum(axis=0)
            _fill_shifted(dcp_ref, base, win_ref, sh_ref)
            dh = jnp.zeros((CT, CB), F32)
            for k in range(CMK):
                dh = dh + w_ref[k:k + 1, :] * _tap(sh_ref, CMK - 1 - k)
            av = a_ref[pl.ds(base, CT), :]
            sg = _sigmoid(g_ref[pl.ds(base, CT), :])
            da_ref[pl.ds(base, CT), :] = (dh * sg).astype(BF16)
            dg_ref[pl.ds(base, CT), :] = (dh * av * sg * (1.0 - sg)).astype(BF16)
            return carry

        lax.fori_loop(0, L // CT, tile, 0)
        for k in range(CMK):
            dw_ref[k:k + 1, :] = jnp.sum(dwacc_ref[k], axis=0, keepdims=True)
        dw_ref[CMK:, :] = jnp.zeros((32 - CMK, CB), F32)

    col = pl.BlockSpec((L, CB), lambda j: (0, j))
    return pl.pallas_call(
        body, name=name, grid=(CMC // CB,),
        in_specs=_conv_specs() + [col, pl.BlockSpec((CMK, CB), lambda j: (0, j))],
        out_specs=[col, col, pl.BlockSpec((32, CB), lambda j: (0, j))],
        out_shape=[jax.ShapeDtypeStruct((L, CMC), BF16), jax.ShapeDtypeStruct((L, CMC), BF16),
                   jax.ShapeDtypeStruct((32, CMC), F32)],
        scratch_shapes=_conv_scratch(2) + [pltpu.VMEM((32, 8, CB), F32)],
        compiler_params=_cp(("parallel",)))(u, u, dc, w)


NPAIR = NQH // 2


def _partner(x, lo32):
    return jnp.where(lo32, pltpu.roll(x, 96, 1), pltpu.roll(x, 32, 1))


def _swa_prep(x, w2, w2p, c4, s4, bd, lo32):
    r = lax.rsqrt(_xdot_l(x * x, bd) * (1.0 / HD) + RMS_EPS)
    xh = x * r
    return r, xh, xh * w2 * c4 + _partner(xh, lo32) * w2p * s4


def _swa_unprep(dr, r, xh, w2, w2p, c4, s4, bd, lo32):
    dn = dr * c4
    dnp = dr * s4
    gx = dn * w2 + _partner(dnp * w2p, lo32)
    dw = jnp.sum((dn + _partner(dnp, lo32)) * xh, axis=0, keepdims=True)
    mu = _xdot_l(gx * xh, bd) * (1.0 / HD)
    return r * (gx - xh * mu), dw


def _swa_softmax(s, sink):
    row = lax.broadcasted_iota(jnp.int32, (L, 2 * Q), 0)
    col = lax.broadcasted_iota(jnp.int32, (L, 2 * Q), 1)
    rm = row & (Q - 1)
    valid = (col > rm) & (col <= rm + Q) & ((row >= Q) | (col >= Q))
    s = jnp.where(valid, s * (1.0 / math.sqrt(HD)), NEG)
    m = jnp.maximum(jnp.max(s, axis=-1, keepdims=True), sink)
    p = jnp.exp(s - m)
    ps = jnp.exp(sink - m)
    inv = 1.0 / (jnp.sum(p, axis=-1, keepdims=True) + ps)
    return p * inv, ps * inv


def _swa_in_specs():
    tab = pl.BlockSpec((L, 128), lambda p: (0, 0))
    wv = pl.BlockSpec((1, 128), lambda p: (0, 0))
    sk = pl.BlockSpec((1, 1, 128), lambda p: (p, 0, 0))
    return [pl.BlockSpec((L, 128), lambda p: (0, U_Q // 128 + p)), pl.BlockSpec((L, 128), lambda p: (0, U_K // 128)),
            pl.BlockSpec((L, 128), lambda p: (0, U_V // 128)), tab, tab, wv, wv, wv, wv, sk, sk,
            pl.BlockSpec((128, 128), lambda p: (0, 0))]


def _swa_setup(q_ref, k_ref, v_ref, c_ref, s_ref, qw_ref, qwp_ref, kw_ref, kwp_ref, bd_ref, kpad, vpad):
    g = pl.program_id(0) // 2
    lane = lax.broadcasted_iota(jnp.int32, (L, 128), 1)
    lo32 = (lane & 32) == 0
    own = (lane >> 6) == g
    c4, s4, bd = c_ref[...], s_ref[...], bd_ref[...]
    qn = _swa_prep(q_ref[...], qw_ref[...], qwp_ref[...], c4, s4, bd, lo32)
    kn = _swa_prep(k_ref[...], kw_ref[...], kwp_ref[...], c4, s4, bd, lo32)
    vv = v_ref[...]
    kpad[0:Q, :] = jnp.zeros((Q, 128), BF16)
    vpad[0:Q, :] = jnp.zeros((Q, 128), BF16)
    kpad[Q:, :] = jnp.where(own, kn[2], pltpu.roll(kn[2], HD, 1)).astype(BF16)
    vpad[Q:, :] = jnp.where(own, vv, pltpu.roll(vv, HD, 1)).astype(BF16)
    return qn, kn, lo32, own, c4, s4, bd


def _swa_fwd(u, cos4, sin4, qw2, qw2p, kw2, kw2p, sink_e, sink_o, bd, name):
    def body(q_ref, k_ref, v_ref, c_ref, s_ref, qw_ref, qwp_ref, kw_ref, kwp_ref, ske_ref, sko_ref, bd_ref,
             o_ref, kpad, vpad, s_scr, p_scr):
        qn, _, _, _, _, _, _ = _swa_setup(q_ref, k_ref, v_ref, c_ref, s_ref, qw_ref, qwp_ref, kw_ref, kwp_ref,
                                          bd_ref, kpad, vpad)
        qr = qn[2]
        first = lax.broadcasted_iota(jnp.int32, (Q, 128), 1) < HD
        for n in range(NC):
            rows = slice(n * Q, (n + 1) * Q)
            kc = kpad[n * Q:(n + 2) * Q, :]
            s_scr[0, rows, :] = _dg(jnp.where(first, qr[rows], 0.0).astype(BF16), kc, 1, 1)
            s_scr[1, rows, :] = _dg(jnp.where(first, 0.0, qr[rows]).astype(BF16), kc, 1, 1)
        for h, sk_ref in ((0, ske_ref), (1, sko_ref)):
            p, _ = _swa_softmax(s_scr[h], sk_ref[0][:, 0:1])
            p_scr[h] = p.astype(BF16)
        for n in range(NC):
            rows = slice(n * Q, (n + 1) * Q)
            vc = vpad[n * Q:(n + 2) * Q, :]
            o_ref[rows, :] = jnp.where(first, _dg(p_scr[0, rows, :], vc, 1, 0),
                                       _dg(p_scr[1, rows, :], vc, 1, 0)).astype(BF16)

    return pl.pallas_call(
        body, name=name, grid=(NPAIR,), in_specs=_swa_in_specs(),
        out_specs=pl.BlockSpec((L, 128), lambda p: (0, p)),
        out_shape=jax.ShapeDtypeStruct((L, ATT_W), BF16),
        scratch_shapes=[pltpu.VMEM((L + Q, 128), BF16), pltpu.VMEM((L + Q, 128), BF16),
                        pltpu.VMEM((2, L, 2 * Q), F32), pltpu.VMEM((2, L, 2 * Q), BF16)],
        compiler_params=_cp(("arbitrary",)))(u, u, u, cos4, sin4, qw2, qw2p, kw2, kw2p, sink_e, sink_o, bd)


def _swa_bwd(u, dy, cos4, sin4, qw2, qw2p, kw2, kw2p, sink_e, sink_o, bd, name):
    def body(q_ref, k_ref, v_ref, c_ref, s_ref, qw_ref, qwp_ref, kw_ref, kwp_ref, ske_ref, sko_ref, bd_ref, do_ref,
             dq_ref, dk_ref, dv_ref, dqw_ref, dkw_ref, dse_ref, dso_ref,
             kpad, vpad, s_scr, dp_scr, ds_scr, pb_scr, dkr_acc, dv_acc, dqr_scr):
        pidx = pl.program_id(0)

        @pl.when(pidx == 0)
        def _():
            dkr_acc[...] = jnp.zeros_like(dkr_acc)
            dv_acc[...] = jnp.zeros_like(dv_acc)
            dqw_ref[...] = jnp.zeros_like(dqw_ref)

        qn, kn, lo32, own, c4, s4, bd = _swa_setup(q_ref, k_ref, v_ref, c_ref, s_ref, qw_ref, qwp_ref, kw_ref,
                                                   kwp_ref, bd_ref, kpad, vpad)
        qr = qn[2]
        lane_q = lax.broadcasted_iota(jnp.int32, (Q, 128), 1)
        first = lane_q < HD
        own_q = (lane_q >> 6) == pidx // 2

        def halves(t):
            return jnp.where(first, t, 0.0).astype(BF16), jnp.where(first, 0.0, t).astype(BF16)

        for n in range(NC):
            rows = slice(n * Q, (n + 1) * Q)
            kc = kpad[n * Q:(n + 2) * Q, :]
            vc = vpad[n * Q:(n + 2) * Q, :]
            qm = halves(qr[rows])
            dom = halves(do_ref[rows, :])
            for h in range(2):
                s_scr[h, rows, :] = _dg(qm[h], kc, 1, 1)
                dp_scr[h, rows, :] = _dg(dom[h], vc, 1, 1)
        for h, sk_ref, dsk_ref in ((0, ske_ref, dse_ref), (1, sko_ref, dso_ref)):
            p, ps = _swa_softmax(s_scr[h], sk_ref[0][:, 0:1])
            dp = dp_scr[h]
            delta = jnp.sum(p * dp, axis=-1, keepdims=True)
            dsk_ref[0] = jnp.broadcast_to(-jnp.sum(ps * delta, axis=0, keepdims=True), (1, 128))
            ds_scr[h] = (p * (dp - delta) * (1.0 / math.sqrt(HD))).astype(BF16)
            pb_scr[h] = p.astype(BF16)
        for n in range(NC):
            rows = slice(n * Q, (n + 1) * Q)
            kc = kpad[n * Q:(n + 2) * Q, :]
            dqr_scr[rows, :] = jnp.where(first, _dg(ds_scr[0, rows, :], kc, 1, 0), _dg(ds_scr[1, rows, :], kc, 1, 0))
        for m in range(NC):
            acc_k = jnp.zeros((Q, 128), F32)
            acc_v = jnp.zeros((Q, 128), F32)
            for n, cols in ((m, slice(Q, 2 * Q)), (m + 1, slice(0, Q))):
                if n >= NC:
                    continue
                rows = slice(n * Q, (n + 1) * Q)
                qm = halves(qr[rows])
                dom = halves(do_ref[rows, :])
                for h in range(2):
                    acc_k = acc_k + _dg(ds_scr[h, rows, cols], qm[h], 0, 0)
                    acc_v = acc_v + _dg(pb_scr[h, rows, cols], dom[h], 0, 0)
            rows = slice(m * Q, (m + 1) * Q)
            dkr_acc[rows, :] += jnp.where(own_q, acc_k + pltpu.roll(acc_k, HD, 1), 0.0)
            dv_acc[rows, :] += jnp.where(own_q, acc_v + pltpu.roll(acc_v, HD, 1), 0.0)
        dq, dqw = _swa_unprep(dqr_scr[...], qn[0], qn[1], qw_ref[...], qwp_ref[...], c4, s4, bd, lo32)
        dq_ref[...] = dq.astype(BF16)
        dqw_ref[...] += dqw

        @pl.when(pidx == NPAIR - 1)
        def _():
            dk, dkw = _swa_unprep(dkr_acc[...], kn[0], kn[1], kw_ref[...], kwp_ref[...], c4, s4, bd, lo32)
            dk_ref[...] = dk.astype(BF16)
            dkw_ref[...] = dkw
            dv_ref[...] = dv_acc[...].astype(BF16)

    full = pl.BlockSpec((L, 128), lambda p: (0, 0))
    wv = pl.BlockSpec((1, 128), lambda p: (0, 0))
    sk = pl.BlockSpec((1, 1, 128), lambda p: (p, 0, 0))
    vec = jax.ShapeDtypeStruct((1, 128), F32)
    skv = jax.ShapeDtypeStruct((NPAIR, 1, 128), F32)
    return pl.pallas_call(
        body, name=name, grid=(NPAIR,),
        in_specs=_swa_in_specs() + [pl.BlockSpec((L, 128), lambda p: (0, SSD_W // 128 + p))],
        out_specs=[pl.BlockSpec((L, 128), lambda p: (0, p)), full, full, wv, wv, sk, sk],
        out_shape=[jax.ShapeDtypeStruct((L, ATT_W), BF16), jax.ShapeDtypeStruct((L, 128), BF16),
                   jax.ShapeDtypeStruct((L, 128), BF16), vec, vec, skv, skv],
        scratch_shapes=[pltpu.VMEM((L + Q, 128), BF16), pltpu.VMEM((L + Q, 128), BF16),
                        pltpu.VMEM((2, L, 2 * Q), F32), pltpu.VMEM((2, L, 2 * Q), F32),
                        pltpu.VMEM((2, L, 2 * Q), BF16), pltpu.VMEM((2, L, 2 * Q), BF16),
                        pltpu.VMEM((L, 128), F32), pltpu.VMEM((L, 128), F32), pltpu.VMEM((L, 128), F32)],
        compiler_params=_cp(("arbitrary",)))(u, u, u, cos4, sin4, qw2, qw2p, kw2, kw2p, sink_e, sink_o, bd, dy)


def _ssd_consts():
    hh = jnp.arange(128)[:, None]
    e = (hh == (jnp.arange(SSD_W)[None, :] // HP)).astype(BF16)
    e2 = (hh == (jnp.arange(NH * 128)[None, :] // 128)).astype(BF16)
    et = e.T
    tril = (jnp.arange(Q)[:, None] >= jnp.arange(Q)[None, :]).astype(BF16)
    triu = tril.T
    eye = jnp.eye(128, dtype=BF16)
    return e, e2, et, tril, triu, eye


def _ssd_common(x_ref, ext_scr, cw_ref, cb_ref, dt_ref, dtb_ref, alog_ref, e_ref, e2_ref, tril_ref, triu_ref,
                arow_scr, acol_scr, eax_scr):
    conv = jnp.broadcast_to(cb_ref[...], (Q, XBC))
    for k in range(4):
        conv = conv + cw_ref[k:k + 1, :] * ext_scr[pl.ds(5 + k, Q), :]
    sg = _sigmoid(conv)
    xbc = conv * sg
    dtpre = dt_ref[...] + dtb_ref[...]
    dt = _softplus(dtpre)
    a = -jnp.exp(alog_ref[...])
    adt = dt * a
    acol = _xdot_r(tril_ref[...], adt)
    acol_scr[...] = acol
    arow_scr[...] = _xdot_l(adt, triu_ref[...], 0, 0)
    alast = acol_scr[Q - 1:Q, :]
    ea = jnp.exp(acol)
    decs = jnp.exp(alast - acol)
    e = e_ref[...]
    dt_x = _xdot_l(dt, e)
    eax_scr[...] = _xdot_l(ea, e)
    decs_x = _xdot_l(decs, e)
    acx2 = _xdot_l(acol, e2_ref[...])
    return conv, sg, xbc, dtpre, dt, a, adt, acol, alast, ea, decs, dt_x, decs_x, acx2


def _ssd_fwd(u, cw, cb, dtb, alog, dxp, nw, consts, name):
    e, e2, et, tril, triu, eye = consts

    def body(z0_ref, z1_ref, x_ref, dt_ref, cw_ref, cb_ref, dtb_ref, alog_ref, dx_ref, nw_ref, e_ref, e2_ref,
             tril_ref, triu_ref, ya_ref, ypre_ref, st_ref, s_scr, ext_scr, arow_scr, acol_scr, eax_scr):
        c = pl.program_id(0)

        @pl.when(c == 0)
        def _():
            s_scr[...] = jnp.zeros_like(s_scr)
            ext_scr[0:8, :] = jnp.zeros((8, XBC), F32)
        ext_scr[8:8 + Q, :] = x_ref[...]
        (conv, sg, xbc, dtpre, dt, a, adt, acol, alast, ea, decs, dt_x, decs_x, acx2) = _ssd_common(
            x_ref, ext_scr, cw_ref, cb_ref, dt_ref, dtb_ref, alog_ref, e_ref, e2_ref, tril_ref, triu_ref,
            arow_scr, acol_scr, eax_scr)
        ext_scr[0:8, :] = ext_scr[Q:Q + 8, :]
        xs = xbc[:, :SSD_W]
        xdt = xs * dt_x
        lane = lax.broadcasted_iota(jnp.int32, (Q, 128), 1)
        causal = lax.broadcasted_iota(jnp.int32, (Q, Q), 0) >= lax.broadcasted_iota(jnp.int32, (Q, Q), 1)
        for g in range(2):
            bg = xbc[:, SSD_W + g * NS:SSD_W + (g + 1) * NS].astype(BF16)
            cg = xbc[:, SSD_W + 2 * NS + g * NS:SSD_W + 2 * NS + (g + 1) * NS].astype(BF16)
            cbm = _dg(cg, bg, 1, 1)
            sgv = s_scr[g]
            st_ref[0, g] = sgv
            gc = slice(g * 512, (g + 1) * 512)
            yoff = _dg(cg, sgv.astype(BF16), 1, 0) * eax_scr[:, gc]
            for pr in range(4):
                h0 = g * 8 + 2 * pr
                h1 = h0 + 1
                c0 = g * 512 + pr * 128
                xp = xdt[:, c0:c0 + 128].astype(BF16)
                w0 = (cbm * jnp.exp(jnp.where(causal, acx2[:, h0 * 128:(h0 + 1) * 128] - arow_scr[h0:h0 + 1, :],
                                              NEG))).astype(BF16)
                w1 = (cbm * jnp.exp(jnp.where(causal, acx2[:, h1 * 128:(h1 + 1) * 128] - arow_scr[h1:h1 + 1, :],
                                              NEG))).astype(BF16)
                yd = jnp.where(lane < HP, _dg(w0, xp, 1, 0), _dg(w1, xp, 1, 0))
                ypre_ref[:, c0:c0 + 128] = (yd + yoff[:, pr * 128:(pr + 1) * 128]
                                            + xs[:, c0:c0 + 128] * dx_ref[:, c0:c0 + 128])
            contrib = _dg(bg, (xdt[:, gc] * decs_x[:, gc]).astype(BF16), 0, 0)
            s_scr[g] = sgv * eax_scr[Q - 1:Q, gc] + contrib
        for g, zr in enumerate((z0_ref, z1_ref)):
            gc = slice(g * 512, (g + 1) * 512)
            zz = zr[...]
            ggg = ypre_ref[:, gc] * (zz * _sigmoid(zz))
            rstd = lax.rsqrt(jnp.mean(ggg * ggg, axis=-1, keepdims=True) + RMS_EPS)
            ya_ref[:, gc] = (ggg * rstd * nw_ref[:, gc]).astype(BF16)

    def row(w, blk=0):
        return pl.BlockSpec((Q, w), lambda c: (c, blk))

    def full(shape):
        return pl.BlockSpec(shape, lambda c: (0,) * len(shape))

    return pl.pallas_call(
        body, name=name, grid=(NC,),
        in_specs=[row(512, U_Z // 512), row(512, U_Z // 512 + 1), row(XBC, U_X // XBC), row(128, U_DT // 128),
                  full((4, XBC)), full((1, XBC)), full((1, 128)), full((1, 128)),
                  full((1, SSD_W)), full((1, SSD_W)), full((128, SSD_W)), full((128, NH * 128)), full((Q, Q)),
                  full((Q, Q))],
        out_specs=[row(SSD_W), row(SSD_W), pl.BlockSpec((1, 2, NS, 512), lambda c: (c, 0, 0, 0))],
        out_shape=[jax.ShapeDtypeStruct((L, SSD_W), BF16), jax.ShapeDtypeStruct((L, SSD_W), F32),
                   jax.ShapeDtypeStruct((NC, 2, NS, 512), F32)],
        scratch_shapes=[pltpu.VMEM((2, NS, 512), F32), pltpu.VMEM((Q + 8, XBC), F32), pltpu.VMEM((128, Q), F32),
                        pltpu.VMEM((Q, 128), F32), pltpu.VMEM((Q, SSD_W), F32)],
        compiler_params=_cp(("arbitrary",)))(u, u, u, u, cw, cb, dtb, alog, dxp, nw, e, e2, tril, triu)


def _ssd_bwd(u, ypre, st, dy, cw, cb, dtb, alog, dxp, nw, consts, name):
    e, e2, et, tril, triu, eye = consts

    def body(z0_ref, z1_ref, x_ref, xp_ref, dt_ref, ypre_ref, st_ref, dya_ref, cw_ref, cb_ref, dtb_ref, alog_ref, dx_ref,
             nw_ref, e_ref, e2_ref, et_ref, tril_ref, triu_ref, eye_ref,
             dz_ref, dxr_ref, ddtr_ref, dcw_ref, dcb_ref, ddtb_ref, dalog_ref, dd_ref, dnw_ref,
             g_scr, ext_scr, ext2_scr, arow_scr, acol_scr, eax_scr, darow_scr, dxdt_scr, t1_scr, t2_scr, dgg_scr):
        i = pl.program_id(0)

        @pl.when(i == 0)
        def _():
            g_scr[...] = jnp.zeros_like(g_scr)
            ext2_scr[Q:Q + 8, :] = jnp.zeros((8, XBC), F32)
            for r in (dcw_ref, dcb_ref, ddtb_ref, dalog_ref, dd_ref, dnw_ref):
                r[...] = jnp.zeros_like(r)
        not_first = jnp.where(i < NC - 1, 1.0, 0.0)
        ext_scr[0:8, :] = xp_ref[Q - 8:Q, :] * not_first
        ext_scr[8:8 + Q, :] = x_ref[...]
        (conv, sg, xbc, dtpre, dt, a, adt, acol, alast, ea, decs, dt_x, decs_x, acx2) = _ssd_common(
            x_ref, ext_scr, cw_ref, cb_ref, dt_ref, dtb_ref, alog_ref, e_ref, e2_ref, tril_ref, triu_ref,
            arow_scr, acol_scr, eax_scr)
        et_m = et_ref[...]
        xs = xbc[:, :SSD_W]
        xdt = xs * dt_x
        y = ypre_ref[...]
        zz = jnp.concatenate([z0_ref[...], z1_ref[...]], axis=1)
        sz = _sigmoid(zz)
        silu_z = zz * sz
        gg = y * silu_z
        dya = dya_ref[...]
        for g in range(2):
            gc = slice(g * 512, (g + 1) * 512)
            ggg = gg[:, gc]
            rstd = lax.rsqrt(jnp.mean(ggg * ggg, axis=-1, keepdims=True) + RMS_EPS)
            n = ggg * rstd
            dyag = dya[:, gc]
            dnw_ref[:, gc] += jnp.sum(dyag * n, axis=0, keepdims=True)
            dn = dyag * nw_ref[:, gc]
            dgg_scr[:, gc] = rstd * (dn - n * jnp.mean(dn * n, axis=-1, keepdims=True))
        dgg = dgg_scr[...]
        dy = dgg * silu_z
        dz_ref[...] = (dgg * y * (sz * (1.0 + zz * (1.0 - sz)))).astype(BF16)
        dd_ref[...] += _rowdot(jnp.sum(dy * xs, axis=0, keepdims=True), et_m)
        dxs = dy * dx_ref[...]
        dys = dy * eax_scr[...]
        lane = lax.broadcasted_iota(jnp.int32, (Q, 128), 1)
        causal = lax.broadcasted_iota(jnp.int32, (Q, Q), 0) >= lax.broadcasted_iota(jnp.int32, (Q, Q), 1)
        darow_scr[...] = jnp.zeros_like(darow_scr)
        dacol = jnp.zeros((Q, 128), F32)
        dcdx = []
        dbs = []
        dcs = []
        for g in range(2):
            gc = slice(g * 512, (g + 1) * 512)
            bg = xbc[:, SSD_W + g * NS:SSD_W + (g + 1) * NS].astype(BF16)
            cg = xbc[:, SSD_W + 2 * NS + g * NS:SSD_W + 2 * NS + (g + 1) * NS].astype(BF16)
            cbm = _dg(cg, bg, 1, 1)
            sgv = st_ref[0, g]
            sgb = sgv.astype(BF16)
            gv = g_scr[g]
            gvb = gv.astype(BF16)
            yoff = _dg(cg, sgb, 1, 0) * eax_scr[:, gc]
            dysg = dys[:, gc].astype(BF16)
            dcg = _dg(dysg, sgb, 1, 1)
            ds_off = _dg(cg, dysg, 0, 0)
            t1_scr[:, gc] = dy[:, gc] * yoff
            xdec = xdt[:, gc] * decs_x[:, gc]
            dxd = _dg(bg, gvb, 1, 0)
            dbg = _dg(xdec.astype(BF16), gvb, 1, 1)
            dxdt_g = dxd * decs_x[:, gc]
            t2_scr[:, gc] = dxd * xdt[:, gc]
            cdx = eax_scr[Q - 1:Q, gc]
            dcdx.append(jnp.sum(gv * sgv, axis=0, keepdims=True))
            g_scr[g] = gv * cdx + ds_off
            dcb_acc = jnp.zeros((Q, Q), F32)
            for pr in range(4):
                c0 = g * 512 + pr * 128
                xp = xdt[:, c0:c0 + 128].astype(BF16)
                dyp = dy[:, c0:c0 + 128]
                dypb = dyp.astype(BF16)
                halves = []
                for hh, keep in ((g * 8 + 2 * pr, lane < HP), (g * 8 + 2 * pr + 1, lane >= HP)):
                    lam = jnp.exp(jnp.where(causal, acx2[:, hh * 128:(hh + 1) * 128] - arow_scr[hh:hh + 1, :], NEG))
                    w = cbm * lam
                    dw = _dg(jnp.where(keep, dyp, 0.0).astype(BF16), xp, 1, 1)
                    dcb_acc = dcb_acc + dw * lam
                    t = dw * w
                    dacol = dacol + jnp.sum(t, axis=-1, keepdims=True) * (lane == hh).astype(F32)
                    darow_scr[hh:hh + 1, :] -= jnp.sum(t, axis=0, keepdims=True)
                    halves.append(_dg(w.astype(BF16), dypb, 0, 0))
                dxdt_scr[:, c0:c0 + 128] = (jnp.where(lane < HP, halves[0], halves[1])
                                            + dxdt_g[:, pr * 128:(pr + 1) * 128])
            dcbb = dcb_acc.astype(BF16)
            dcs.append(dcg + _dg(dcbb, bg, 1, 0))
            dbs.append(dbg + _dg(dcbb, cg, 0, 0))
        dacol = dacol + _xdot_l(t1_scr[...], et_m)
        ddecs = _xdot_l(t2_scr[...], et_m) * decs
        dacol = dacol - ddecs
        dalast = jnp.sum(ddecs, axis=0, keepdims=True)
        dcd = _rowdot(jnp.concatenate(dcdx, axis=1), et_m)
        dalast = dalast + dcd * jnp.exp(alast)
        dacol = dacol + _xdot_l(darow_scr[...], eye_ref[...], 0, 0)
        rowi = lax.broadcasted_iota(jnp.int32, (Q, 128), 0)
        dacol = dacol + jnp.where(rowi == Q - 1, dalast, 0.0)
        dadt = _xdot_r(triu_ref[...], dacol)
        dxdt = dxdt_scr[...]
        ddt = dadt * a + _xdot_l(dxdt * xs, et_m)
        dalog_ref[...] += jnp.sum(dadt * dt, axis=0, keepdims=True) * a
        dxs = dxs + dxdt * dt_x
        ddtr = ddt * _sigmoid(dtpre)
        ddtb_ref[...] += jnp.sum(ddtr, axis=0, keepdims=True)
        ddtr_ref[...] = ddtr.astype(BF16)
        dsilu = sg * (1.0 + conv * (1.0 - sg))
        ext2_scr[0:Q, 0:SSD_W] = dxs * dsilu[:, :SSD_W]
        for g in range(2):
            o1 = SSD_W + g * NS
            o2 = SSD_W + 2 * NS + g * NS
            ext2_scr[0:Q, o1:o1 + NS] = dbs[g] * dsilu[:, o1:o1 + NS]
            ext2_scr[0:Q, o2:o2 + NS] = dcs[g] * dsilu[:, o2:o2 + NS]
        dconv = ext2_scr[0:Q, :]
        dcb_ref[...] += jnp.sum(dconv, axis=0, keepdims=True)
        dxr = jnp.zeros((Q, XBC), F32)
        for k in range(4):
            dcw_ref[k:k + 1, :] += jnp.sum(dconv * ext_scr[pl.ds(5 + k, Q), :], axis=0, keepdims=True)
            dxr = dxr + cw_ref[k:k + 1, :] * ext2_scr[pl.ds(3 - k, Q), :]
        dxr_ref[...] = dxr.astype(BF16)
        ext2_scr[Q:Q + 8, :] = ext2_scr[0:8, :]

    def row(w, blk=0):
        return pl.BlockSpec((Q, w), lambda i: (NC - 1 - i, blk))

    def full(shape):
        return pl.BlockSpec(shape, lambda i: (0,) * len(shape))

    prev = pl.BlockSpec((Q, XBC), lambda i: (jnp.maximum(NC - 2 - i, 0), U_X // XBC))
    return pl.pallas_call(
        body, name=name, grid=(NC,),
        in_specs=[row(512, U_Z // 512), row(512, U_Z // 512 + 1), row(XBC, U_X // XBC), prev, row(128, U_DT // 128),
                  row(SSD_W),
                  pl.BlockSpec((1, 2, NS, 512), lambda i: (NC - 1 - i, 0, 0, 0)), row(SSD_W),
                  full((4, XBC)), full((1, XBC)), full((1, 128)), full((1, 128)), full((1, SSD_W)),
                  full((1, SSD_W)), full((128, SSD_W)), full((128, NH * 128)), full((SSD_W, 128)), full((Q, Q)),
                  full((Q, Q)), full((128, 128))],
        out_specs=[row(SSD_W), row(XBC), row(128), full((8, XBC)), full((1, XBC)), full((1, 128)), full((1, 128)),
                   full((1, 128)), full((1, SSD_W))],
        out_shape=[jax.ShapeDtypeStruct((L, SSD_W), BF16), jax.ShapeDtypeStruct((L, XBC), BF16),
                   jax.ShapeDtypeStruct((L, 128), BF16), jax.ShapeDtypeStruct((8, XBC), F32),
                   jax.ShapeDtypeStruct((1, XBC), F32), jax.ShapeDtypeStruct((1, 128), F32),
                   jax.ShapeDtypeStruct((1, 128), F32), jax.ShapeDtypeStruct((1, 128), F32),
                   jax.ShapeDtypeStruct((1, SSD_W), F32)],
        scratch_shapes=[pltpu.VMEM((2, NS, 512), F32), pltpu.VMEM((Q + 8, XBC), F32), pltpu.VMEM((Q + 8, XBC), F32),
                        pltpu.VMEM((128, Q), F32), pltpu.VMEM((Q, 128), F32), pltpu.VMEM((Q, SSD_W), F32),
                        pltpu.VMEM((128, Q), F32), pltpu.VMEM((Q, SSD_W), F32), pltpu.VMEM((Q, SSD_W), F32),
                        pltpu.VMEM((Q, SSD_W), F32), pltpu.VMEM((Q, SSD_W), F32)],
        compiler_params=_cp(("arbitrary",)))(u, u, u, u, u, ypre, st, dy, cw, cb, dtb, alog, dxp, nw,
                                             e, e2, et, tril, triu, eye)


def _my_pos():
    return lax.axis_index("x"), lax.axis_index("y"), lax.axis_index("c")


CHIP_REL = ((1, 0), (0, 1), (1, 1))
CHIP_XOR = (2, 1, 3)
BIG = ("w_in", "w_out", "w_mlp_up", "w_mlp_down")
NW = len(BIG)
AT = 256


def _chips(x, y):
    return [(1 - x if dx else x, 1 - y if dy else y) for dx, dy in CHIP_REL]


HBM_SPEC = pl.BlockSpec(memory_space=pltpu.HBM)
SEM_SPEC = pl.BlockSpec(memory_space=pltpu.SEMAPHORE)
EFFECT = pltpu.SideEffectType.DATAFLOW_SIDE_EFFECTING


def _hbm(t):
    return pltpu.with_memory_space_constraint(t, pltpu.HBM)


def _split_start(srcs, lands, after, copies, name):
    n = len(srcs)

    def body(*refs):
        src_refs, land_refs = refs[:n], refs[n:2 * n]
        send_sems, recv_sems = refs[2 * n + 1], refs[2 * n + 2]
        token = refs[-1]
        for w, k, src, dst, dev in copies(src_refs, land_refs):
            pltpu.make_async_remote_copy(src_ref=src, dst_ref=dst, send_sem=send_sems.at[3 * w + k],
                                         recv_sem=recv_sems.at[3 * w + k], device_id=dev, device_id_type=MESH).start()
        token[...] = jnp.zeros_like(token)

    outs = pl.pallas_call(
        body, name=name,
        out_shape=(pltpu.SemaphoreType.DMA((3 * n,)), pltpu.SemaphoreType.DMA((3 * n,)),
                   *[pltpu.HBM(t.shape, t.dtype) for t in srcs], *[pltpu.HBM(t.shape, t.dtype) for t in lands],
                   jax.ShapeDtypeStruct((8, 128), F32)),
        in_specs=[HBM_SPEC] * (2 * n) + [ANY],
        out_specs=(SEM_SPEC, SEM_SPEC, *([HBM_SPEC] * (2 * n)), VMEM_SPEC),
        input_output_aliases={i: 2 + i for i in range(2 * n)},
        compiler_params=pltpu.CompilerParams(has_side_effects=EFFECT))(
            *[_hbm(t) for t in srcs], *[_hbm(t) for t in lands], after)
    return outs[0], outs[1], list(outs[2:2 + n]), list(outs[2 + n:2 + 2 * n]), outs[-1]


def _split_wait(send_sems, recv_sems, srcs, lands, after, copies, name):
    n = len(srcs)

    def body(*refs):
        src_refs, land_refs = refs[:n], refs[n:2 * n]
        ssem, rsem = refs[2 * n], refs[2 * n + 1]
        for w, k, src, dst, dev in copies(src_refs, land_refs):
            cp = pltpu.make_async_remote_copy(src_ref=src, dst_ref=dst, send_sem=ssem.at[3 * w + k],
                                              recv_sem=rsem.at[3 * w + k], device_id=dev, device_id_type=MESH)
            cp.wait_send()
            cp.wait_recv()

    outs = pl.pallas_call(
        body, name=name,
        out_shape=tuple([pltpu.HBM(t.shape, t.dtype) for t in srcs] + [pltpu.HBM(t.shape, t.dtype) for t in lands]),
        in_specs=[HBM_SPEC] * (2 * n) + [SEM_SPEC, SEM_SPEC, ANY],
        out_specs=tuple([HBM_SPEC] * (2 * n)),
        input_output_aliases={i: i for i in range(2 * n)},
        compiler_params=pltpu.CompilerParams(has_side_effects=EFFECT))(*srcs, *lands, send_sems, recv_sems, after)
    return list(outs[:n]), list(outs[n:])


def _ag_copies(arrival):
    def copies(src_refs, land_refs):
        x, y, c = _my_pos()
        s = 2 * x + y
        chips = _chips(x, y)
        for w in range(len(src_refs)):
            hr = src_refs[w].shape[0] // 2
            mine = pl.ds(c * hr, hr)
            for k in range(3):
                slot = s ^ CHIP_XOR[k] if arrival else s
                yield w, k, src_refs[w].at[mine], land_refs[w].at[slot, mine], (*chips[k], c)
    return copies


def _ag_forward(lands, name):
    n = len(lands)

    def body(*refs):
        outs = refs[n:2 * n]
        send_sems, recv_sems = refs[2 * n:]
        x, y, c = _my_pos()
        s = 2 * x + y
        sib = (x, y, 1 - c)
        sends = []
        for w in range(n):
            hr = outs[w].shape[1] // 2
            for k in range(3):
                blk = outs[w].at[s ^ CHIP_XOR[k], pl.ds(c * hr, hr)]
                fw = pltpu.make_async_remote_copy(
                    src_ref=blk, dst_ref=blk, send_sem=send_sems.at[w, k], recv_sem=recv_sems.at[w, k],
                    device_id=sib, device_id_type=MESH)
                fw.start()
                sends.append(fw)
        for w in range(n):
            hr = outs[w].shape[1] // 2
            for k in range(3):
                blk = outs[w].at[s ^ CHIP_XOR[k], pl.ds((1 - c) * hr, hr)]
                pltpu.make_async_remote_copy(
                    src_ref=blk, dst_ref=blk, send_sem=send_sems.at[w, k], recv_sem=recv_sems.at[w, k],
                    device_id=sib, device_id_type=MESH).wait_recv()
        for cp in sends:
            cp.wait_send()

    return pl.pallas_call(
        body, name=name, in_specs=[ANY] * n, out_specs=[ANY] * n,
        out_shape=[jax.ShapeDtypeStruct(t.shape, t.dtype) for t in lands],
        input_output_aliases={w: w for w in range(n)},
        scratch_shapes=[pltpu.SemaphoreType.DMA((n, 3)), pltpu.SemaphoreType.DMA((n, 3))])(*lands)


def _rs_copies(src_refs, land_refs):
    x, y, c = _my_pos()
    s = 2 * x + y
    chips = _chips(x, y)
    for w in range(len(src_refs)):
        for k in range(3):
            yield w, k, src_refs[w].at[s ^ CHIP_XOR[k]], land_refs[w].at[k], (*chips[k], c)


def _place_own(shard, gathered, sidx, name):
    r, cc = shard.shape
    at = _rs_rows(r)

    def body(s_ref, a_ref, g_ref, o_ref):
        o_ref[0] = a_ref[...]

    return pl.pallas_call(
        body, name=name,
        grid_spec=pltpu.PrefetchScalarGridSpec(
            num_scalar_prefetch=1, grid=(r // at,),
            in_specs=[pl.BlockSpec((at, cc), lambda i, s_ref: (i, 0)), ANY],
            out_specs=pl.BlockSpec((1, at, cc), lambda i, s_ref: (s_ref[0], i, 0))),
        out_shape=jax.ShapeDtypeStruct(gathered.shape, gathered.dtype),
        input_output_aliases={2: 0}, compiler_params=_cp(("parallel",)))(sidx, shard, gathered)


def _rs_pair(dwb, name):
    n = len(dwb)

    def body(*refs):
        ins, outs = refs[:n], refs[n:2 * n]
        send_sems, recv_sems = refs[2 * n:]
        x, y, c = _my_pos()
        cps = []
        for w in range(n):
            hr = ins[w].shape[1] // 2
            cp = pltpu.make_async_remote_copy(
                src_ref=ins[w].at[:, pl.ds((1 - c) * hr, hr)], dst_ref=outs[w], send_sem=send_sems.at[w],
                recv_sem=recv_sems.at[w], device_id=(x, y, 1 - c), device_id_type=MESH)
            cp.start()
            cps.append(cp)
        for cp in cps:
            cp.wait()

    return pl.pallas_call(
        body, name=name, in_specs=[ANY] * n, out_specs=[ANY] * n,
        out_shape=[jax.ShapeDtypeStruct((4, t.shape[1] // 2, t.shape[2]), t.dtype) for t in dwb],
        scratch_shapes=[pltpu.SemaphoreType.DMA((n,)), pltpu.SemaphoreType.DMA((n,))])(*dwb)


def _rs_sib(q, name):
    n = len(q)

    def body(*refs):
        outs = refs[n:2 * n]
        send_sems, recv_sems = refs[2 * n:]
        x, y, c = _my_pos()
        cps = []
        for w in range(n):
            hr = outs[w].shape[0] // 2
            mine = pl.ds(c * hr, hr)
            cp = pltpu.make_async_remote_copy(
                src_ref=outs[w].at[mine], dst_ref=outs[w].at[mine], send_sem=send_sems.at[w],
                recv_sem=recv_sems.at[w], device_id=(x, y, 1 - c), device_id_type=MESH)
            cp.start()
            cps.append(cp)
        for w in range(n):
            hr = outs[w].shape[0] // 2
            other = outs[w].at[pl.ds((1 - c) * hr, hr)]
            pltpu.make_async_remote_copy(
                src_ref=other, dst_ref=other, send_sem=send_sems.at[w], recv_sem=recv_sems.at[w],
                device_id=(x, y, 1 - c), device_id_type=MESH).wait_recv()
        for cp in cps:
            cp.wait_send()

    return pl.pallas_call(
        body, name=name, in_specs=[ANY] * n, out_specs=[ANY] * n,
        out_shape=[jax.ShapeDtypeStruct(t.shape, t.dtype) for t in q],
        input_output_aliases={w: w for w in range(n)},
        scratch_shapes=[pltpu.SemaphoreType.DMA((n,)), pltpu.SemaphoreType.DMA((n,))])(*q)


def _rs_rows(hr):
    return 2 * AT if hr % (2 * AT) == 0 else AT


def _rs_add2(dw, got, scidx, name):
    _, r, cc = dw.shape
    hr = r // 2
    at = _rs_rows(hr)
    nb = hr // at

    def body(s_ref, a_ref, b_ref, o_ref, ob_ref):
        acc = a_ref[...] + b_ref[...].astype(F32)
        ob_ref[...] = acc.astype(BF16)

        @pl.when(pl.program_id(1) == s_ref[0])
        def _():
            o_ref[...] = acc[0]

    blk = pl.BlockSpec((1, at, cc), lambda i, sh, s_ref: (sh, i, 0))
    return pl.pallas_call(
        body, name=name,
        grid_spec=pltpu.PrefetchScalarGridSpec(
            num_scalar_prefetch=1, grid=(nb, 4),
            in_specs=[pl.BlockSpec((1, at, cc), lambda i, sh, s_ref: (sh, s_ref[1] * nb + i, 0)), blk],
            out_specs=[pl.BlockSpec((at, cc), lambda i, sh, s_ref: (i, 0)), blk]),
        out_shape=[jax.ShapeDtypeStruct((hr, cc), F32), jax.ShapeDtypeStruct((4, hr, cc), BF16)],
        compiler_params=_cp(("parallel", "arbitrary")))(scidx, dw, got)


def _rs_add4(p, got, scidx, name):
    hr, cc = p.shape
    at = _rs_rows(hr)
    nb = hr // at

    def body(s_ref, p_ref, g0_ref, g1_ref, g2_ref, o_ref):
        acc = p_ref[...] + g0_ref[0].astype(F32)
        acc = acc + g1_ref[0].astype(F32)
        o_ref[...] = acc + g2_ref[0].astype(F32)

    def gk(k):
        return pl.BlockSpec((1, at, cc), lambda i, s_ref: (k, i, 0))

    return pl.pallas_call(
        body, name=name,
        grid_spec=pltpu.PrefetchScalarGridSpec(
            num_scalar_prefetch=1, grid=(nb,),
            in_specs=[pl.BlockSpec((at, cc), lambda i, s_ref: (i, 0)), gk(0), gk(1), gk(2)],
            out_specs=pl.BlockSpec((at, cc), lambda i, s_ref: (s_ref[1] * nb + i, 0))),
        out_shape=jax.ShapeDtypeStruct((2 * hr, cc), F32),
        compiler_params=_cp(("parallel",)))(scidx, p, got, got, got)


def _rs_begin(dws, dwbs, after, tag=""):
    x, y, c = _my_pos()
    scidx = jnp.stack([2 * x + y, c]).astype(jnp.int32)
    got = _rs_pair(dwbs, "rs_pair" + tag)
    pairs = [_rs_add2(dws[w], got[w], scidx, "rs_add2") for w in range(len(dws))]
    pb = [p[1] for p in pairs]
    lands = [lax.empty((3,) + t.shape[1:], BF16) for t in pb]
    ssem, rsem, pb, lands, token = _split_start(pb, lands, after, _rs_copies, "rs_chip_start" + tag)
    return ([p[0] for p in pairs], ssem, rsem, pb, lands), token


def _rs_end(state, after, tag=""):
    x, y, c = _my_pos()
    scidx = jnp.stack([2 * x + y, c]).astype(jnp.int32)
    p, ssem, rsem, pb, lands = state
    _, recv = _split_wait(ssem, rsem, pb, lands, after, _rs_copies, "rs_chip_wait" + tag)
    q = [_rs_add4(p[w], recv[w], scidx, "rs_add4") for w in range(len(p))]
    return _rs_sib(q, "rs_sib" + tag)


def _allreduce_small(buf, name):
    rows = buf.shape[0]

    def body(src_ref, out_ref, gat_ref, send_sems, recv_sems):
        x, y, c = _my_pos()
        me = 4 * x + 2 * y + c
        gat_ref[me] = src_ref[...]
        cps = []
        for r in range(1, N_DEV):
            tx = 1 - x if (r >> 2) & 1 else x
            ty = 1 - y if (r >> 1) & 1 else y
            tc = 1 - c if r & 1 else c
            cps.append(pltpu.make_async_remote_copy(
                src_ref=src_ref, dst_ref=gat_ref.at[me], send_sem=send_sems.at[r - 1], recv_sem=recv_sems.at[r - 1],
                device_id=(tx, ty, tc), device_id_type=MESH))
        for cp in cps:
            cp.start()
        for cp in cps:
            cp.wait()
        acc = gat_ref[0]
        for k in range(1, N_DEV):
            acc = acc + gat_ref[k]
        out_ref[...] = acc

    return pl.pallas_call(
        body, name=name, in_specs=[VMEM_SPEC], out_specs=VMEM_SPEC, out_shape=jax.ShapeDtypeStruct((rows, 128), F32),
        scratch_shapes=[pltpu.VMEM((N_DEV, rows, 128), F32), pltpu.SemaphoreType.DMA((N_DEV - 1,)),
                        pltpu.SemaphoreType.DMA((N_DEV - 1,))],
        compiler_params=_cp())(buf)


SMALL = (("norm_mix_w", (D,)), ("ssd_conv_w", (4, XBC)), ("ssd_conv_b", (XBC,)), ("ssd_dt_bias", (NH,)),
         ("ssd_a_log", (NH,)), ("ssd_d", (NH,)), ("ssd_norm_w", (SSD_W,)), ("q_norm_w", (HD,)),
         ("k_norm_w", (HD,)), ("attn_sinks", (NQH,)), ("cm_dw_w", (CMK, CMC)), ("cm_dw_b", (CMC,)),
         ("cm_ln_w", (CMC,)), ("cm_ln_b", (CMC,)), ("norm_mlp_w", (D,)))
SHARDED_SMALL = ("ssd_conv_w", "cm_dw_w")


def _seg_len(shape):
    n = 1
    for d in shape:
        n *= d
    return -(-n // 128) * 128


def _pack_small(vals, names):
    parts = []
    for name, shape in SMALL:
        if name not in names:
            continue
        v = vals[name].reshape(DEPTH, -1)
        pad = _seg_len(shape) - v.shape[1]
        parts.append(jnp.pad(v, ((0, 0), (0, pad))))
    flat = jnp.concatenate(parts, axis=1)
    return flat.reshape(-1, 128)


def _unpack_small(buf, names):
    flat = buf.reshape(DEPTH, -1)
    out = {}
    off = 0
    for name, shape in SMALL:
        if name not in names:
            continue
        n = 1
        for d in shape:
            n *= d
        out[name] = flat[:, off:off + n].reshape((DEPTH,) + shape)
        off += _seg_len(shape)
    return out


SW = N_IN // 4
SWP = 1152
ORIG = (("z", 0, 1024), ("x", 1024, 2560), ("dt", 2560, 2576), ("q", 2576, 3088), ("k", 3088, 3216),
        ("v", 3216, 3344), ("a", 3344, 3856), ("g", 3856, 4368))


def _orig_cols(g_in, lo, hi):
    out = []
    for s in range(4):
        a, b = max(lo, s * SW), min(hi, (s + 1) * SW)
        if a < b:
            out.append(g_in[s][:, a - s * SW:b - s * SW])
    return out


def _shard_major(parts):
    cols = []
    for s in range(4):
        for name, g0, g1 in ORIG:
            a, b = max(g0, s * SW), min(g1, (s + 1) * SW)
            if a < b:
                cols.append(parts[name][:, a - g0:b - g0])
        cols.append(jnp.zeros((L, SWP - SW), BF16))
    return jnp.concatenate(cols, axis=1)


def _rope_tables():
    inv = 10000.0 ** (-jnp.arange(0, HD, 2, dtype=F32) / HD)
    ang = jnp.arange(L, dtype=F32)[:, None] * inv[None, :]
    return jnp.cos(ang), jnp.sin(ang)


def _swa_tables():
    cos, sin = _rope_tables()
    return jnp.tile(cos, (1, 4)), jnp.tile(jnp.concatenate([-sin, sin], axis=1), (1, 2))


def _swa_weights(w):
    return jnp.tile(w, 2)[None], jnp.tile(jnp.concatenate([w[HH:], w[:HH]]), 2)[None]


def _swa_sinks(s):
    s2 = s.reshape(NPAIR, 2)
    return (jnp.broadcast_to(s2[:, 0][:, None, None], (NPAIR, 1, 128)),
            jnp.broadcast_to(s2[:, 1][:, None, None], (NPAIR, 1, 128)))


def _swa_blockdiag():
    i = jnp.arange(128)
    return (i[:, None] // HD == i[None, :] // HD).astype(BF16)


def _pad128(v):
    return jnp.pad(v, (0, 128 - v.shape[0]))[None, :]


def kernel(x, norm_mix_w, w_in, ssd_conv_w, ssd_conv_b, ssd_dt_bias, ssd_a_log, ssd_d, ssd_norm_w, q_norm_w, k_norm_w, attn_sinks, cm_dw_w, cm_dw_b, cm_ln_w, cm_ln_b, w_out, norm_mlp_w, w_mlp_up, w_mlp_down, loss_target, m_norm_mix_w, m_w_in, m_ssd_conv_w, m_ssd_conv_b, m_ssd_dt_bias, m_ssd_a_log, m_ssd_d, m_ssd_norm_w, m_q_norm_w, m_k_norm_w, m_attn_sinks, m_cm_dw_w, m_cm_dw_b, m_cm_ln_w, m_cm_ln_b, m_w_out, m_norm_mlp_w, m_w_mlp_up, m_w_mlp_down, v_norm_mix_w, v_w_in, v_ssd_conv_w, v_ssd_conv_b, v_ssd_dt_bias, v_ssd_a_log, v_ssd_d, v_ssd_norm_w, v_q_norm_w, v_k_norm_w, v_attn_sinks, v_cm_dw_w, v_cm_dw_b, v_cm_ln_w, v_cm_ln_b, v_w_out, v_norm_mlp_w, v_w_mlp_up, v_w_mlp_down):
    px, py, pc = _my_pos()
    shard = 2 * px + py
    sidx = jnp.reshape(shard, (1,)).astype(jnp.int32)
    consts = _ssd_consts()
    cos4, sin4 = _swa_tables()
    bd = _swa_blockdiag()

    big_w = (w_in, w_out, w_mlp_up, w_mlp_down)

    def own_shard(w, l):
        t = big_w[w][l]
        if w == 0:
            t = jnp.pad(t, ((0, 0), (0, SWP - SW)))
        return t.astype(BF16)

    def gather_start(l, sel, after, tag=""):
        own = [own_shard(w, l) for w in sel]
        lands = [lax.empty((4,) + t.shape, BF16) for t in own]
        return _split_start(own, lands, after, _ag_copies(False), "ag_start" + tag)

    def gather_finish(in_flight, after, tag=""):
        ssem, rsem, own, lands, _ = in_flight
        own, lands = _split_wait(ssem, rsem, own, lands, after, _ag_copies(True), "ag_wait" + tag)
        lands = _ag_forward(lands, "ag_forward" + tag)
        return [_place_own(o, g, sidx, "ag_place") for o, g in zip(own, lands)]

    zero_tile = jnp.zeros((8, 128), F32)
    first_in = gather_start(0, [0], zero_tile, "_in0")
    zc = jnp.zeros((DEPTH, 4, XBC), F32)
    zc = lax.dynamic_update_slice_in_dim(zc, ssd_conv_w, shard * (XBC // 4), axis=2)
    zd = jnp.zeros((DEPTH, CMK, CMC), F32)
    zd = lax.dynamic_update_slice_in_dim(zd, cm_dw_w, shard * (CMC // 4), axis=2)
    half = jnp.where(pc == 0, 1.0, 0.0).astype(F32)
    gw_packed = _allreduce_small(_pack_small({"ssd_conv_w": zc * half, "cm_dw_w": zd * half}, SHARDED_SMALL)
                                 + first_in[4][0:1, 0:1], "ag_small")
    first_rest = gather_start(0, [1, 2, 3], gw_packed, "_rest0")
    gw = _unpack_small(gw_packed, SHARDED_SMALL)
    conv_w_full, dw_w_full = gw["ssd_conv_w"], gw["cm_dw_w"]

    xcur = x[0]
    saved = []
    in_flight = None
    for l in range(DEPTH):
        if l == 0:
            (g_in,) = gather_finish(first_in, first_rest[4], "_in0")
            nxt_after = g_in
        else:
            g_in, g_out, g_up, g_dn = gather_finish(in_flight, xcur)
            nxt_after = g_dn
        nmw = norm_mix_w[l][None]
        if l + 1 < DEPTH:
            in_flight = gather_start(l + 1, [0, 1, 2, 3], nxt_after)
            nmw = nmw + in_flight[4][0:1, 0:1]
        grp = dict((n, (a, b)) for n, a, b in ORIG)
        w_perm = jnp.concatenate(
            _orig_cols(g_in, *grp["x"]) + _orig_cols(g_in, *grp["z"]) + _orig_cols(g_in, grp["a"][0], grp["g"][1])
            + _orig_cols(g_in, grp["q"][0], grp["v"][1]) + _orig_cols(g_in, *grp["dt"])
            + [jnp.zeros((D, 128 - NH), BF16)], axis=1)
        h = _rms_fwd(xcur, nmw, "rms_mix_fwd")
        u = _mm(h, w_perm, "nn", "in_proj", tn=640)
        alog = _pad128(ssd_a_log[l])
        dtb = _pad128(ssd_dt_bias[l])
        dxp = jnp.repeat(ssd_d[l], HP)[None, :]
        ssd_p = (conv_w_full[l], ssd_conv_b[l][None], dtb, alog, dxp, ssd_norm_w[l][None])
        ya, ypre, st = _ssd_fwd(u, *ssd_p, consts, "ssd_fwd")
        swa_p = (cos4, sin4, *_swa_weights(q_norm_w[l]), *_swa_weights(k_norm_w[l]), *_swa_sinks(attn_sinks[l]), bd)
        yb = _swa_fwd(u, *swa_p, "attn_fwd")
        cconv = _conf_conv_fwd(u, dw_w_full[l], cm_dw_b[l][None], "conf_conv_fwd")
        conf_p = (cconv, cm_ln_w[l][None], cm_ln_b[l][None])
        yc = _conf_ln_fwd(*conf_p, "conf_ln_fwd")
        if l == 0:
            g_out, g_up, g_dn = gather_finish(first_rest, yc, "_rest0")
        g_out = g_out.reshape(2 * D, D)
        g_dn = g_dn.reshape(DFF, D)
        ycat = jnp.concatenate([ya, yb, yc], axis=1)
        x1 = _mm(ycat, g_out, "nn", "out_proj", add=xcur)
        hm = _rms_fwd(x1, norm_mlp_w[l][None], "rms_mlp_fwd")
        r_up = _mm_up(hm, g_up, "mlp_up")
        x2 = _mm(r_up, g_dn, "nn", "mlp_down", add=x1)
        saved.append(dict(x=xcur, h=h, u=u, ypre=ypre, st=st, swa_p=swa_p, conf_p=conf_p, ycat=ycat, x1=x1,
                          hm=hm, r_up=r_up, ssd_p=ssd_p, g_in=g_in, g_out=g_out, g_up=g_up, g_dn=g_dn))
        xcur = x2

    lsum, dx, dxb = _loss_bwd(xcur, loss_target[0], "loss")

    loc = locals()
    names = ["norm_mix_w", "w_in", "ssd_conv_w", "ssd_conv_b", "ssd_dt_bias", "ssd_a_log", "ssd_d", "ssd_norm_w",
             "q_norm_w", "k_norm_w", "attn_sinks", "cm_dw_w", "cm_dw_b", "cm_ln_w", "cm_ln_b", "w_out", "norm_mlp_w",
             "w_mlp_up", "w_mlp_down"]
    weights = {n: loc[n] for n in names}
    moms = {n: loc["m_" + n] for n in names}
    vars_ = {n: loc["v_" + n] for n in names}
    big_out = {n: None for n in BIG}
    win_grads = [None] * DEPTH

    def finish_layer(layer, which, shard_grads):
        for n, g in zip(which, shard_grads):
            if n == "w_in":
                win_grads[layer] = g
            else:
                big_out[n] = _adamw_layer(weights[n], g, moms[n], vars_[n], layer, big_out[n], "adamw_" + n)

    pending = None
    last_mlp = None
    gsm = {name: [] for name, _ in SMALL}
    for l in reversed(range(DEPTH)):
        sv = saved[l]
        da = _mm(dxb, sv["g_dn"], "nt", "mlp_down_dx", relu2_of=sv["r_up"])
        dwdn, dwdn_b = _mm_dw(sv["r_up"], dxb, "mlp_down_dw")
        dwup, dwup_b = _mm_dw(sv["hm"], da, "mlp_up_dw", col_shards=True)
        dhm = _mm_cs_nt(da, sv["g_up"], "mlp_up_dx")
        nlw = norm_mlp_w[l][None]
        if l == 0:
            last_mlp, token = _rs_begin([dwup, dwdn.reshape(4, D, D)], [dwup_b, dwdn_b.reshape(4, D, D)], zero_tile,
                                        "_mlp0")
            nlw = nlw + token[0:1, 0:1]
        dx1, dx1b, dnw = _rms_bwd(sv["x1"], nlw, dhm, dx, "rms_mlp_bwd")
        gsm["norm_mlp_w"].append(dnw[0])
        dy = _mm(dx1b, sv["g_out"], "nt", "out_proj_dx")
        dwout, dwout_b = _mm_dw(sv["ycat"], dx1b, "out_proj_dw")
        dcc, dwb, dlw, dlb = _conf_ln_bwd(*sv["conf_p"], dy, "conf_ln_bwd")
        da_c, dg_c, dww = _conf_conv_bwd(sv["u"], dcc, dw_w_full[l], "conf_conv_bwd")
        gsm["cm_dw_w"].append(dww[:CMK])
        gsm["cm_dw_b"].append(dwb[0])
        gsm["cm_ln_w"].append(dlw[0])
        gsm["cm_ln_b"].append(dlb[0])
        dq, dk, dv, dqw, dkw, dse, dso = _swa_bwd(sv["u"], dy, *sv["swa_p"], "attn_bwd")
        gsm["q_norm_w"].append(dqw[0, :HD] + dqw[0, HD:])
        gsm["k_norm_w"].append(dkw[0, :HD] + dkw[0, HD:])
        gsm["attn_sinks"].append(jnp.stack([dse[:, 0, 0], dso[:, 0, 0]], axis=1).reshape(NQH))
        (dz, dxr, ddtr, dcw, dcb, ddtb, dalog, ddd, dnsw) = _ssd_bwd(
            sv["u"], sv["ypre"], sv["st"], dy, *sv["ssd_p"], consts, "ssd_bwd")
        gsm["ssd_conv_w"].append(dcw[:4])
        gsm["ssd_conv_b"].append(dcb[0])
        gsm["ssd_dt_bias"].append(ddtb[0, :NH])
        gsm["ssd_a_log"].append(dalog[0, :NH])
        gsm["ssd_d"].append(ddd[0, :NH])
        gsm["ssd_norm_w"].append(dnsw[0])
        du = _shard_major(dict(z=dz, x=dxr, dt=ddtr[:, :NH], q=dq, k=dk, v=dv, a=da_c, g=dg_c))
        dwin, dwin_b = _mm_dw(sv["h"], du, "in_dw", col_shards=True, tn=SWP // 3)
        if l == 0:
            state, token = _rs_begin([dwin, dwout.reshape(4, D // 2, D)], [dwin_b, dwout_b.reshape(4, D // 2, D)],
                                     zero_tile, "_io0")
        else:
            state, token = _rs_begin(
                [dwin, dwout.reshape(4, D // 2, D), dwup, dwdn.reshape(4, D, D)],
                [dwin_b, dwout_b.reshape(4, D // 2, D), dwup_b, dwdn_b.reshape(4, D, D)], zero_tile)
        dh = _mm_cs_nt(du, sv["g_in"], "in_dx")
        dx, dxb, dnm = _rms_bwd(sv["x"], norm_mix_w[l][None] + token[0:1, 0:1], dh, dx1, "rms_mix_bwd")
        gsm["norm_mix_w"].append(dnm[0])
        if pending is not None:
            finish_layer(l + 1, BIG, _rs_end(pending, dx))
        pending = state

    finish_layer(0, BIG[2:], _rs_end(last_mlp, dx, "_mlp0"))
    gsm = {k: jnp.stack(v[::-1]) for k, v in gsm.items()}
    packed = _pack_small(gsm, [n for n, _ in SMALL])
    packed = jnp.concatenate([packed, lsum], axis=0)
    red = _allreduce_small(packed, "ar_small")
    finish_layer(0, BIG[:2], _rs_end(pending, red, "_io0"))
    loss = 0.5 * red[-8, 0] / D
    gsm = _unpack_small(red[:-8], [n for n, _ in SMALL])
    gsm["ssd_conv_w"] = lax.dynamic_slice_in_dim(gsm["ssd_conv_w"], shard * (XBC // 4), XBC // 4, axis=2)
    gsm["cm_dw_w"] = lax.dynamic_slice_in_dim(gsm["cm_dw_w"], shard * (CMC // 4), CMC // 4, axis=2)
    grads = dict(gsm)
    delta, new_m, new_v = {}, {}, {}
    for n in BIG[1:]:
        grads[n], delta[n], new_m[n], new_v[n] = big_out[n]
    to_lead = lambda t: jnp.transpose(t, (2, 0, 1))
    g_lead = jnp.stack([jnp.transpose(g[:, :SW]) for g in win_grads], axis=1)
    d_lead, m_lead, v_lead = _adamw_lead(to_lead(w_in), g_lead, to_lead(m_w_in), to_lead(v_w_in), "adamw_w_in")
    from_lead = lambda t: jnp.transpose(t, (1, 2, 0))
    grads["w_in"], delta["w_in"] = from_lead(g_lead), from_lead(d_lead)
    new_m["w_in"], new_v["w_in"] = from_lead(m_lead), from_lead(v_lead)

    packed_names = [n for n, _ in SMALL if n not in SHARDED_SMALL]
    pw = _pack_small(weights, packed_names)
    pg = _pack_small(grads, packed_names)
    pm = _pack_small(moms, packed_names)
    pv = _pack_small(vars_, packed_names)
    pd, pmn, pvn = _adamw(pw, pg, pm, pv, "adamw_small")
    for dst, buf in ((delta, pd), (new_m, pmn), (new_v, pvn)):
        dst.update(_unpack_small(buf, packed_names))
    for n in SHARDED_SMALL:
        shp = weights[n].shape
        flat = lambda t: t.reshape(-1, shp[-1])
        d_, m_, v_ = _adamw(flat(weights[n]), flat(grads[n]), flat(moms[n]), flat(vars_[n]), "adamw_" + n)
        delta[n], new_m[n], new_v[n] = d_.reshape(shp), m_.reshape(shp), v_.reshape(shp)

    return (loss, dx[None], *[grads[n] for n in names], *[delta[n] for n in names],
            *[new_m[n] for n in names], *[new_v[n] for n in names])
```

```python
import functools
import math

import jax
import jax.numpy as jnp
from jax import lax
from jax.experimental import pallas as pl
from jax.experimental.pallas import tpu as pltpu

F32 = jnp.float32
BF16 = jnp.bfloat16
MESH = pl.DeviceIdType.MESH
ANY = pl.BlockSpec(memory_space=pl.ANY)
VMEM_SPEC = pl.BlockSpec(memory_space=pltpu.VMEM)

D = 1024
L = 2048
DEPTH = 4
SSD_W = 1024
XBC = 1536
NH = 16
HP = 64
NS = 128
Q = 128
NC = L // Q
ATT_W = 512
NQH = 8
NKV = 2
HD = 64
HH = HD // 2
CMC = 512
CMK = 31
DFF = 4096
N_IN = 4368
N_PAD = 4480
RMS_EPS = 1e-6
LN_EPS = 1e-5
NEG = -1e30
LR, B1, B2, EPS_A, WD, STEP = 0.001, 0.9, 0.999, 1e-8, 0.01, 10
VMEM_LIMIT = 56 * 1024 * 1024
N_DEV = 8


def _cp(sem=None):
    kw = dict(vmem_limit_bytes=VMEM_LIMIT)
    if sem is not None:
        kw["dimension_semantics"] = sem
    return pltpu.CompilerParams(**kw)


def _dg(a, b, ca, cb):
    return lax.dot_general(a, b, (((ca,), (cb,)), ((), ())), preferred_element_type=F32)


def _split3(x):
    hi = x.astype(BF16)
    r = x - hi.astype(F32)
    mid = r.astype(BF16)
    lo = (r - mid.astype(F32)).astype(BF16)
    return hi, mid, lo


def _xdot_l(x, m, ca=1, cb=0):
    hi, mid, lo = _split3(x)
    return _dg(hi, m, ca, cb) + _dg(mid, m, ca, cb) + _dg(lo, m, ca, cb)


def _xdot_r(m, x, ca=1, cb=0):
    hi, mid, lo = _split3(x)
    return _dg(m, hi, ca, cb) + _dg(m, mid, ca, cb) + _dg(m, lo, ca, cb)


def _rowdot(v, m):
    return _xdot_l(jnp.broadcast_to(v, (8, v.shape[1])), m)[0:1]


def _sigmoid(x):
    return 1.0 / (1.0 + jnp.exp(-x))


def _softplus(x):
    e = jnp.exp(-jnp.abs(x))
    u = 1.0 + e
    l1p = jnp.where(u == 1.0, e, jnp.log(u) * (e / jnp.where(u == 1.0, 1.0, u - 1.0)))
    return jnp.maximum(x, 0.0) + l1p


def _tm(k):
    return L if k <= D else L // 2


def _mm(a, b, mode, name, add=None, relu2_of=None, tn=512):
    m, k = a.shape
    tm = min(m, _tm(k))
    a_spec = pl.BlockSpec((tm, k), lambda i, j: (i, 0))
    if mode == "nn":
        n = b.shape[1]
        b_spec = pl.BlockSpec((k, tn), lambda i, j: (0, j))
        cb = 0
    else:
        n = b.shape[0]
        b_spec = pl.BlockSpec((tn, k), lambda i, j: (j, 0))
        cb = 1
    assert m % tm == 0 and n % tn == 0, (m, n, tm, tn)
    o_spec = pl.BlockSpec((tm, tn), lambda i, j: (i, j))
    out_dtype = F32
    if relu2_of is not None:
        def body(a_ref, b_ref, c_ref, o_ref):
            o_ref[...] = (_dg(a_ref[...], b_ref[...], 1, cb) * (2.0 * jnp.sqrt(c_ref[...].astype(F32)))).astype(BF16)
        ins, specs, out_dtype = (a, b, relu2_of), [a_spec, b_spec, o_spec], BF16
    elif add is None:
        def body(a_ref, b_ref, o_ref):
            o_ref[...] = _dg(a_ref[...], b_ref[...], 1, cb)
        ins, specs = (a, b), [a_spec, b_spec]
    else:
        def body(a_ref, b_ref, c_ref, o_ref):
            o_ref[...] = _dg(a_ref[...], b_ref[...], 1, cb) + c_ref[...]
        ins, specs = (a, b, add), [a_spec, b_spec, o_spec]
    return pl.pallas_call(
        body, name=name, grid=(m // tm, n // tn), in_specs=specs, out_specs=o_spec,
        out_shape=jax.ShapeDtypeStruct((m, n), out_dtype), compiler_params=_cp(("parallel", "parallel")))(*ins)


def _mm_up(a, b, name, tn=512):
    m = a.shape[0]
    cs = DFF // 4
    per = cs // tn
    tm = min(m, _tm(D))

    def body(a_ref, b_ref, r_ref):
        r = jnp.maximum(_dg(a_ref[...], b_ref[0], 1, 0), 0.0)
        r_ref[...] = (r * r).astype(BF16)

    return pl.pallas_call(
        body, name=name, grid=(m // tm, DFF // tn),
        in_specs=[pl.BlockSpec((tm, D), lambda i, j: (i, 0)),
                  pl.BlockSpec((1, D, tn), lambda i, j: (j // per, 0, j % per))],
        out_specs=pl.BlockSpec((tm, tn), lambda i, j: (i, j)),
        out_shape=jax.ShapeDtypeStruct((m, DFF), BF16), compiler_params=_cp(("parallel", "parallel")))(a, b)


def _mm_cs_nt(a, b, name, tn=512):
    m = a.shape[0]
    _, n, cs = b.shape
    tm = min(m, _tm(4 * cs))

    def body(a_ref, b_ref, o_ref):
        acc = _dg(a_ref[:, 0:cs], b_ref[0], 1, 1)
        for s in range(1, 4):
            acc = acc + _dg(a_ref[:, s * cs:(s + 1) * cs], b_ref[s], 1, 1)
        o_ref[...] = acc

    return pl.pallas_call(
        body, name=name, grid=(m // tm, n // tn),
        in_specs=[pl.BlockSpec((tm, 4 * cs), lambda i, j: (i, 0)), pl.BlockSpec((4, tn, cs), lambda i, j: (0, j, 0))],
        out_specs=pl.BlockSpec((tm, tn), lambda i, j: (i, j)),
        out_shape=jax.ShapeDtypeStruct((m, n), F32), compiler_params=_cp(("parallel", "parallel")))(a, b)


def _mm_dw(a, b, name, col_shards=False, tn=512):
    k, m = a.shape
    n = b.shape[1]
    tm = min(m, D)
    a_spec = pl.BlockSpec((k, tm), lambda i, j: (0, i))
    b_spec = pl.BlockSpec((k, tn), lambda i, j: (0, j))
    if col_shards:
        per = (n // 4) // tn
        o_spec = pl.BlockSpec((1, tm, tn), lambda i, j: (j // per, i, j % per))
        shape = (4, m, n // 4)
    else:
        o_spec = pl.BlockSpec((tm, tn), lambda i, j: (i, j))
        shape = (m, n)

    def body(a_ref, b_ref, o_ref, ob_ref):
        acc = _dg(a_ref[...], b_ref[...], 0, 0).reshape(o_ref.shape)
        o_ref[...] = acc
        ob_ref[...] = acc.astype(BF16)

    return pl.pallas_call(
        body, name=name, grid=(m // tm, n // tn), in_specs=[a_spec, b_spec], out_specs=[o_spec, o_spec],
        out_shape=[jax.ShapeDtypeStruct(shape, F32), jax.ShapeDtypeStruct(shape, BF16)],
        compiler_params=_cp(("parallel", "parallel")))(a, b)


TR = 256


def _rms_fwd(x, w, name):
    def body(x_ref, w_ref, o_ref):
        xv = x_ref[...]
        r = lax.rsqrt(jnp.mean(xv * xv, axis=-1, keepdims=True) + RMS_EPS)
        o_ref[...] = (xv * r * w_ref[...]).astype(BF16)

    return pl.pallas_call(
        body, name=name, grid=(L // TR,),
        in_specs=[pl.BlockSpec((TR, D), lambda i: (i, 0)), pl.BlockSpec((1, D), lambda i: (0, 0))],
        out_specs=pl.BlockSpec((TR, D), lambda i: (i, 0)),
        out_shape=jax.ShapeDtypeStruct((L, D), BF16), compiler_params=_cp(("parallel",)))(x, w)


def _rms_bwd(x, w, dh, dres, name):
    def body(x_ref, w_ref, dh_ref, dr_ref, dx_ref, dxb_ref, dw_ref):
        xv = x_ref[...]
        r = lax.rsqrt(jnp.mean(xv * xv, axis=-1, keepdims=True) + RMS_EPS)
        n = xv * r
        dhv = dh_ref[...]
        g = dhv * w_ref[...]
        dx = dr_ref[...] + r * (g - n * jnp.mean(g * n, axis=-1, keepdims=True))
        dx_ref[...] = dx
        dxb_ref[...] = dx.astype(BF16)

        @pl.when(pl.program_id(0) == 0)
        def _():
            dw_ref[...] = jnp.zeros_like(dw_ref)
        dw_ref[...] += jnp.sum(dhv * n, axis=0, keepdims=True)

    row = pl.BlockSpec((TR, D), lambda i: (i, 0))
    vec = pl.BlockSpec((1, D), lambda i: (0, 0))
    return pl.pallas_call(
        body, name=name, grid=(L // TR,), in_specs=[row, vec, row, row], out_specs=[row, row, vec],
        out_shape=[jax.ShapeDtypeStruct((L, D), F32), jax.ShapeDtypeStruct((L, D), BF16),
                   jax.ShapeDtypeStruct((1, D), F32)],
        compiler_params=_cp(("arbitrary",)))(x, w, dh, dres)


def _adamw_lead(w, g, m, v, name):
    lead, a, b = w.shape
    tr = max(t for t in range(1, lead + 1) if lead % t == 0 and t * a * b * 4 <= 2 * 1024 * 1024)
    c1 = 1.0 / (1.0 - B1 ** STEP)
    c2 = 1.0 / (1.0 - B2 ** STEP)

    def body(w_ref, g_ref, m_ref, v_ref, d_ref, mo_ref, vo_ref):
        gv = g_ref[...]
        mn = B1 * m_ref[...] + (1.0 - B1) * gv
        vn = B2 * v_ref[...] + (1.0 - B2) * (gv * gv)
        mo_ref[...] = mn
        vo_ref[...] = vn
        d_ref[...] = -LR * ((mn * c1) / (jnp.sqrt(vn * c2) + EPS_A) + WD * w_ref[...])

    blk = pl.BlockSpec((tr, a, b), lambda i: (i, 0, 0))
    shp = jax.ShapeDtypeStruct(w.shape, F32)
    return pl.pallas_call(body, name=name, grid=(lead // tr,), in_specs=[blk] * 4, out_specs=[blk] * 3,
                          out_shape=[shp] * 3, compiler_params=_cp(("parallel",)))(w, g, m, v)


def _adamw_layer(w, g, m, v, layer, prev, name):
    _, rows, cols = w.shape
    tr = 512 if rows % 512 == 0 else 256
    c1 = 1.0 / (1.0 - B1 ** STEP)
    c2 = 1.0 / (1.0 - B2 ** STEP)
    n_prev = 0 if prev is None else 4

    def body(*refs):
        w_ref, g_ref, m_ref, v_ref = refs[:4]
        go_ref, d_ref, mo_ref, vo_ref = refs[4 + n_prev:]
        gv = g_ref[...]
        mn = B1 * m_ref[0] + (1.0 - B1) * gv
        vn = B2 * v_ref[0] + (1.0 - B2) * (gv * gv)
        go_ref[0] = gv
        mo_ref[0] = mn
        vo_ref[0] = vn
        d_ref[0] = -LR * ((mn * c1) / (jnp.sqrt(vn * c2) + EPS_A) + WD * w_ref[0])

    lay = pl.BlockSpec((1, tr, cols), lambda i: (layer, i, 0))
    shp = jax.ShapeDtypeStruct(w.shape, F32)
    return pl.pallas_call(
        body, name=name, grid=(rows // tr,),
        in_specs=[lay, pl.BlockSpec((tr, cols), lambda i: (i, 0)), lay, lay] + [ANY] * n_prev,
        out_specs=[lay] * 4, out_shape=[shp] * 4,
        input_output_aliases={4 + i: i for i in range(n_prev)},
        compiler_params=_cp(("parallel",)))(w, g, m, v, *(prev or ()))


def _loss_bwd(y, t, name):
    def body(y_ref, t_ref, l_ref, d_ref, db_ref):
        e = y_ref[...] - t_ref[...]
        d = e * (1.0 / D)
        d_ref[...] = d
        db_ref[...] = d.astype(BF16)

        @pl.when(pl.program_id(0) == 0)
        def _():
            l_ref[...] = jnp.zeros_like(l_ref)
        s = jnp.sum(jnp.sum(e * e, axis=-1, keepdims=True), axis=0, keepdims=True)
        l_ref[...] += jnp.broadcast_to(s, l_ref.shape)

    row = pl.BlockSpec((TR, D), lambda i: (i, 0))
    tile = pl.BlockSpec((8, 128), lambda i: (0, 0))
    return pl.pallas_call(
        body, name=name, grid=(L // TR,), in_specs=[row, row], out_specs=[tile, row, row],
        out_shape=[jax.ShapeDtypeStruct((8, 128), F32), jax.ShapeDtypeStruct((L, D), F32),
                   jax.ShapeDtypeStruct((L, D), BF16)],
        compiler_params=_cp(("arbitrary",)))(y, t)


def _adamw(w, g, m, v, name):
    rows, cols = w.shape
    tr = rows
    for cand in (512, 256, 128, 64, 32, 16, 8):
        if rows % cand == 0 and cand * cols * 4 <= 2 * 1024 * 1024:
            tr = cand
            break
    c1 = 1.0 / (1.0 - B1 ** STEP)
    c2 = 1.0 / (1.0 - B2 ** STEP)

    def body(w_ref, g_ref, m_ref, v_ref, d_ref, mo_ref, vo_ref):
        gv = g_ref[...]
        mn = B1 * m_ref[...] + (1.0 - B1) * gv
        vn = B2 * v_ref[...] + (1.0 - B2) * (gv * gv)
        mo_ref[...] = mn
        vo_ref[...] = vn
        d_ref[...] = -LR * ((mn * c1) / (jnp.sqrt(vn * c2) + EPS_A) + WD * w_ref[...])

    blk = pl.BlockSpec((tr, cols), lambda i: (i, 0))
    shp = jax.ShapeDtypeStruct((rows, cols), F32)
    return pl.pallas_call(body, name=name, grid=(rows // tr,), in_specs=[blk] * 4, out_specs=[blk] * 3,
                          out_shape=[shp] * 3, compiler_params=_cp(("parallel",)))(w, g, m, v)


CT = 256
CPAD = 32


U_X, U_Z, U_A, U_G, U_Q, U_K, U_V, U_DT = 0, 1536, 2560, 3072, 3584, 4096, 4224, 4352


CEXT = 8
CWIN = CT + CPAD
CROWS = L + CPAD + CEXT


def _fill_shifted(src_ref, base, win_ref, sh_ref):
    win_ref[...] = src_ref[pl.ds(base, CWIN + CEXT), :]
    for p in range(8):
        sh_ref[p] = win_ref[pl.ds(p, CWIN), :]


def _tap(sh_ref, o):
    return sh_ref[o % 8, 8 * (o // 8):8 * (o // 8) + CT, :]


CB = 128


def _conv_specs():
    return [pl.BlockSpec((L, CB), lambda j: (0, U_A // CB + j)), pl.BlockSpec((L, CB), lambda j: (0, U_G // CB + j))]


def _conv_scratch(n_padded):
    return ([pltpu.VMEM((CROWS, CB), F32)] * n_padded
            + [pltpu.VMEM((CWIN + CEXT, CB), F32), pltpu.VMEM((8, CWIN, CB), F32)])


def _fill_gated(a_ref, g_ref, hp_ref):
    hp_ref[0:CPAD, :] = jnp.zeros((CPAD, CB), F32)
    hp_ref[CPAD:CPAD + L, :] = a_ref[...] * _sigmoid(g_ref[...])
    hp_ref[CPAD + L:, :] = jnp.zeros((CEXT, CB), F32)


def _conf_conv_fwd(u, w, b, name):
    def body(a_ref, g_ref, w_ref, b_ref, c_ref, hp_ref, win_ref, sh_ref):
        _fill_gated(a_ref, g_ref, hp_ref)

        def tile(i, carry):
            base = pl.multiple_of(i * CT, CT)
            _fill_shifted(hp_ref, base, win_ref, sh_ref)
            c = jnp.broadcast_to(b_ref[...], (CT, CB))
            for k in range(CMK):
                c = c + w_ref[k:k + 1, :] * _tap(sh_ref, 2 + k)
            c_ref[pl.ds(base, CT), :] = c
            return carry

        lax.fori_loop(0, L // CT, tile, 0)

    return pl.pallas_call(
        body, name=name, grid=(CMC // CB,),
        in_specs=_conv_specs() + [pl.BlockSpec((CMK, CB), lambda j: (0, j)), pl.BlockSpec((1, CB), lambda j: (0, j))],
        out_specs=pl.BlockSpec((L, CB), lambda j: (0, j)),
        out_shape=jax.ShapeDtypeStruct((L, CMC), F32), scratch_shapes=_conv_scratch(1),
        compiler_params=_cp(("parallel",)))(u, u, w, b)


def _conf_ln_fwd(c, lw, lb, ycat, name):
    def body(c_ref, lw_ref, lb_ref, y_ref, o_ref):
        cv = c_ref[...]
        cc = cv - jnp.mean(cv, axis=-1, keepdims=True)
        var = jnp.mean(cc * cc, axis=-1, keepdims=True)
        l = cc * lax.rsqrt(var + LN_EPS) * lw_ref[...] + lb_ref[...]
        o_ref[...] = (l * _sigmoid(l)).astype(BF16)

    row = pl.BlockSpec((TR, CMC), lambda i: (i, 0))
    vec = pl.BlockSpec((1, CMC), lambda i: (0, 0))
    return pl.pallas_call(body, name=name, grid=(L // TR,), in_specs=[row, vec, vec, ANY],
                          out_specs=pl.BlockSpec((TR, CMC), lambda i: (i, (SSD_W + ATT_W) // CMC)),
                          out_shape=jax.ShapeDtypeStruct((L, 2 * D), BF16), input_output_aliases={3: 0},
                          compiler_params=_cp(("parallel",)))(c, lw, lb, ycat)


def _conf_ln_bwd(c, lw, lb, dy, name):
    def body(c_ref, lw_ref, lb_ref, dy_ref, dc_ref, db_ref, dlw_ref, dlb_ref):
        cv = c_ref[...]
        cc = cv - jnp.mean(cv, axis=-1, keepdims=True)
        var = jnp.mean(cc * cc, axis=-1, keepdims=True)
        rstd = lax.rsqrt(var + LN_EPS)
        n = cc * rstd
        l = n * lw_ref[...] + lb_ref[...]
        sl = _sigmoid(l)
        dl = dy_ref[...] * (sl * (1.0 + l * (1.0 - sl)))
        dn = dl * lw_ref[...]
        dc = rstd * (dn - jnp.mean(dn, axis=-1, keepdims=True) - n * jnp.mean(dn * n, axis=-1, keepdims=True))
        dc_ref[...] = dc

        @pl.when(pl.program_id(0) == 0)
        def _():
            db_ref[...] = jnp.zeros_like(db_ref)
            dlw_ref[...] = jnp.zeros_like(dlw_ref)
            dlb_ref[...] = jnp.zeros_like(dlb_ref)
        db_ref[...] += jnp.sum(dc, axis=0, keepdims=True)
        dlw_ref[...] += jnp.sum(dl * n, axis=0, keepdims=True)
        dlb_ref[...] += jnp.sum(dl, axis=0, keepdims=True)

    row = pl.BlockSpec((TR, CMC), lambda i: (i, 0))
    vec = pl.BlockSpec((1, CMC), lambda i: (0, 0))
    vshape = jax.ShapeDtypeStruct((1, CMC), F32)
    return pl.pallas_call(
        body, name=name, grid=(L // TR,),
        in_specs=[row, vec, vec, pl.BlockSpec((TR, CMC), lambda i: (i, (SSD_W + ATT_W) // CMC))],
        out_specs=[row, vec, vec, vec], out_shape=[jax.ShapeDtypeStruct((L, CMC), F32), vshape, vshape, vshape],
        compiler_params=_cp(("arbitrary",)))(c, lw, lb, dy)


def _conf_conv_bwd(u, dc, w, name):
    def body(a_ref, g_ref, dc_ref, w_ref, da_ref, dg_ref, dw_ref, hp_ref, dcp_ref, win_ref, sh_ref, dwacc_ref):
        _fill_gated(a_ref, g_ref, hp_ref)
        dcp_ref[0:L, :] = dc_ref[...]
        dcp_ref[L:, :] = jnp.zeros((CPAD + CEXT, CB), F32)
        dwacc_ref[...] = jnp.zeros_like(dwacc_ref)

        def tile(i, carry):
            base = pl.multiple_of(i * CT, CT)
            _fill_shifted(hp_ref, base, win_ref, sh_ref)
            dcv = dcp_ref[pl.ds(base, CT), :]
            for k in range(CMK):
                dwacc_ref[k] += (dcv * _tap(sh_ref, 2 + k)).reshape(CT // 8, 8, CB).sum(axis=0)
            _fill_shifted(dcp_ref, base, win_ref, sh_ref)
            dh = jnp.zeros((CT, CB), F32)
            for k in range(CMK):
                dh = dh + w_ref[k:k + 1, :] * _tap(sh_ref, CMK - 1 - k)
            av = a_ref[pl.ds(base, CT), :]
            sg = _sigmoid(g_ref[pl.ds(base, CT), :])
            da_ref[pl.ds(base, CT), :] = (dh * sg).astype(BF16)
            dg_ref[pl.ds(base, CT), :] = (dh * av * sg * (1.0 - sg)).astype(BF16)
            return carry

        lax.fori_loop(0, L // CT, tile, 0)
        for k in range(CMK):
            dw_ref[k:k + 1, :] = jnp.sum(dwacc_ref[k], axis=0, keepdims=True)
        dw_ref[CMK:, :] = jnp.zeros((32 - CMK, CB), F32)

    col = pl.BlockSpec((L, CB), lambda j: (0, j))
    return pl.pallas_call(
        body, name=name, grid=(CMC // CB,),
        in_specs=_conv_specs() + [col, pl.BlockSpec((CMK, CB), lambda j: (0, j))],
        out_specs=[col, col, pl.BlockSpec((32, CB), lambda j: (0, j))],
        out_shape=[jax.ShapeDtypeStruct((L, CMC), BF16), jax.ShapeDtypeStruct((L, CMC), BF16),
                   jax.ShapeDtypeStruct((32, CMC), F32)],
        scratch_shapes=_conv_scratch(2) + [pltpu.VMEM((32, 8, CB), F32)],
        compiler_params=_cp(("parallel",)))(u, u, dc, w)


NPAIR = NQH // 2


def _partner(x, lo32):
    return jnp.where(lo32, pltpu.roll(x, 96, 1), pltpu.roll(x, 32, 1))


def _swa_prep(x, w2, w2p, c4, s4, bd, lo32):
    r = lax.rsqrt(_xdot_l(x * x, bd) * (1.0 / HD) + RMS_EPS)
    xh = x * r
    return r, xh, xh * w2 * c4 + _partner(xh, lo32) * w2p * s4


def _swa_unprep(dr, r, xh, w2, w2p, c4, s4, bd, lo32):
    dn = dr * c4
    dnp = dr * s4
    gx = dn * w2 + _partner(dnp * w2p, lo32)
    dw = jnp.sum((dn + _partner(dnp, lo32)) * xh, axis=0, keepdims=True)
    mu = _xdot_l(gx * xh, bd) * (1.0 / HD)
    return r * (gx - xh * mu), dw


def _swa_softmax(s, sink):
    row = lax.broadcasted_iota(jnp.int32, (L, 2 * Q), 0)
    col = lax.broadcasted_iota(jnp.int32, (L, 2 * Q), 1)
    rm = row & (Q - 1)
    valid = (col > rm) & (col <= rm + Q) & ((row >= Q) | (col >= Q))
    s = jnp.where(valid, s * (1.0 / math.sqrt(HD)), NEG)
    m = jnp.maximum(jnp.max(s, axis=-1, keepdims=True), sink)
    p = jnp.exp(s - m)
    ps = jnp.exp(sink - m)
    inv = 1.0 / (jnp.sum(p, axis=-1, keepdims=True) + ps)
    return p * inv, ps * inv


def _swa_in_specs():
    tab = pl.BlockSpec((L, 128), lambda p: (0, 0))
    wv = pl.BlockSpec((1, 128), lambda p: (0, 0))
    sk = pl.BlockSpec((1, 1, 128), lambda p: (p, 0, 0))
    return [pl.BlockSpec((L, 128), lambda p: (0, U_Q // 128 + p)), pl.BlockSpec((L, 128), lambda p: (0, U_K // 128)),
            pl.BlockSpec((L, 128), lambda p: (0, U_V // 128)), tab, tab, wv, wv, wv, wv, sk, sk,
            pl.BlockSpec((128, 128), lambda p: (0, 0))]


def _swa_setup(q_ref, k_ref, v_ref, c_ref, s_ref, qw_ref, qwp_ref, kw_ref, kwp_ref, bd_ref, kpad, vpad):
    g = pl.program_id(0) // 2
    lane = lax.broadcasted_iota(jnp.int32, (L, 128), 1)
    lo32 = (lane & 32) == 0
    own = (lane >> 6) == g
    c4, s4, bd = c_ref[...], s_ref[...], bd_ref[...]
    qn = _swa_prep(q_ref[...], qw_ref[...], qwp_ref[...], c4, s4, bd, lo32)
    kn = _swa_prep(k_ref[...], kw_ref[...], kwp_ref[...], c4, s4, bd, lo32)
    vv = v_ref[...]
    kpad[0:Q, :] = jnp.zeros((Q, 128), BF16)
    vpad[0:Q, :] = jnp.zeros((Q, 128), BF16)
    kpad[Q:, :] = jnp.where(own, kn[2], pltpu.roll(kn[2], HD, 1)).astype(BF16)
    vpad[Q:, :] = jnp.where(own, vv, pltpu.roll(vv, HD, 1)).astype(BF16)
    return qn, kn, lo32, own, c4, s4, bd


def _swa_fwd(u, ycat, cos4, sin4, qw2, qw2p, kw2, kw2p, sink_e, sink_o, bd, name):
    def body(q_ref, k_ref, v_ref, c_ref, s_ref, qw_ref, qwp_ref, kw_ref, kwp_ref, ske_ref, sko_ref, bd_ref, y_ref,
             o_ref, kpad, vpad, s_scr, p_scr):
        qn, _, _, _, _, _, _ = _swa_setup(q_ref, k_ref, v_ref, c_ref, s_ref, qw_ref, qwp_ref, kw_ref, kwp_ref,
                                          bd_ref, kpad, vpad)
        qr = qn[2]
        first = lax.broadcasted_iota(jnp.int32, (Q, 128), 1) < HD
        for n in range(NC):
            rows = slice(n * Q, (n + 1) * Q)
            kc = kpad[n * Q:(n + 2) * Q, :]
            s_scr[0, rows, :] = _dg(jnp.where(first, qr[rows], 0.0).astype(BF16), kc, 1, 1)
            s_scr[1, rows, :] = _dg(jnp.where(first, 0.0, qr[rows]).astype(BF16), kc, 1, 1)
        for h, sk_ref in ((0, ske_ref), (1, sko_ref)):
            p, _ = _swa_softmax(s_scr[h], sk_ref[0][:, 0:1])
            p_scr[h] = p.astype(BF16)
        for n in range(NC):
            rows = slice(n * Q, (n + 1) * Q)
            vc = vpad[n * Q:(n + 2) * Q, :]
            o_ref[rows, :] = jnp.where(first, _dg(p_scr[0, rows, :], vc, 1, 0),
                                       _dg(p_scr[1, rows, :], vc, 1, 0)).astype(BF16)

    return pl.pallas_call(
        body, name=name, grid=(NPAIR,), in_specs=_swa_in_specs() + [ANY],
        out_specs=pl.BlockSpec((L, 128), lambda p: (0, SSD_W // 128 + p)),
        out_shape=jax.ShapeDtypeStruct((L, 2 * D), BF16), input_output_aliases={12: 0},
        scratch_shapes=[pltpu.VMEM((L + Q, 128), BF16), pltpu.VMEM((L + Q, 128), BF16),
                        pltpu.VMEM((2, L, 2 * Q), F32), pltpu.VMEM((2, L, 2 * Q), BF16)],
        compiler_params=_cp(("arbitrary",)))(u, u, u, cos4, sin4, qw2, qw2p, kw2, kw2p, sink_e, sink_o, bd, ycat)


def _swa_bwd(u, dy, cos4, sin4, qw2, qw2p, kw2, kw2p, sink_e, sink_o, bd, name):
    def body(q_ref, k_ref, v_ref, c_ref, s_ref, qw_ref, qwp_ref, kw_ref, kwp_ref, ske_ref, sko_ref, bd_ref, do_ref,
             dq_ref, dk_ref, dv_ref, dqw_ref, dkw_ref, dse_ref, dso_ref,
             kpad, vpad, s_scr, dp_scr, ds_scr, pb_scr, dkr_acc, dv_acc, dqr_scr):
        pidx = pl.program_id(0)

        @pl.when(pidx == 0)
        def _():
            dkr_acc[...] = jnp.zeros_like(dkr_acc)
            dv_acc[...] = jnp.zeros_like(dv_acc)
            dqw_ref[...] = jnp.zeros_like(dqw_ref)

        qn, kn, lo32, own, c4, s4, bd = _swa_setup(q_ref, k_ref, v_ref, c_ref, s_ref, qw_ref, qwp_ref, kw_ref,
                                                   kwp_ref, bd_ref, kpad, vpad)
        qr = qn[2]
        lane_q = lax.broadcasted_iota(jnp.int32, (Q, 128), 1)
        first = lane_q < HD
        own_q = (lane_q >> 6) == pidx // 2

        def halves(t):
            return jnp.where(first, t, 0.0).astype(BF16), jnp.where(first, 0.0, t).astype(BF16)

        for n in range(NC):
            rows = slice(n * Q, (n + 1) * Q)
            kc = kpad[n * Q:(n + 2) * Q, :]
            vc = vpad[n * Q:(n + 2) * Q, :]
            qm = halves(qr[rows])
            dom = halves(do_ref[rows, :])
            for h in range(2):
                s_scr[h, rows, :] = _dg(qm[h], kc, 1, 1)
                dp_scr[h, rows, :] = _dg(dom[h], vc, 1, 1)
        for h, sk_ref, dsk_ref in ((0, ske_ref, dse_ref), (1, sko_ref, dso_ref)):
            p, ps = _swa_softmax(s_scr[h], sk_ref[0][:, 0:1])
            dp = dp_scr[h]
            delta = jnp.sum(p * dp, axis=-1, keepdims=True)
            dsk_ref[0] = jnp.broadcast_to(-jnp.sum(ps * delta, axis=0, keepdims=True), (1, 128))
            ds_scr[h] = (p * (dp - delta) * (1.0 / math.sqrt(HD))).astype(BF16)
            pb_scr[h] = p.astype(BF16)
        for n in range(NC):
            rows = slice(n * Q, (n + 1) * Q)
            kc = kpad[n * Q:(n + 2) * Q, :]
            dqr_scr[rows, :] = jnp.where(first, _dg(ds_scr[0, rows, :], kc, 1, 0), _dg(ds_scr[1, rows, :], kc, 1, 0))
        for m in range(NC):
            acc_k = jnp.zeros((Q, 128), F32)
            acc_v = jnp.zeros((Q, 128), F32)
            for n, cols in ((m, slice(Q, 2 * Q)), (m + 1, slice(0, Q))):
                if n >= NC:
                    continue
                rows = slice(n * Q, (n + 1) * Q)
                qm = halves(qr[rows])
                dom = halves(do_ref[rows, :])
                for h in range(2):
                    acc_k = acc_k + _dg(ds_scr[h, rows, cols], qm[h], 0, 0)
                    acc_v = acc_v + _dg(pb_scr[h, rows, cols], dom[h], 0, 0)
            rows = slice(m * Q, (m + 1) * Q)
            dkr_acc[rows, :] += jnp.where(own_q, acc_k + pltpu.roll(acc_k, HD, 1), 0.0)
            dv_acc[rows, :] += jnp.where(own_q, acc_v + pltpu.roll(acc_v, HD, 1), 0.0)
        dq, dqw = _swa_unprep(dqr_scr[...], qn[0], qn[1], qw_ref[...], qwp_ref[...], c4, s4, bd, lo32)
        dq_ref[...] = dq.astype(BF16)
        dqw_ref[...] += dqw

        @pl.when(pidx == NPAIR - 1)
        def _():
            dk, dkw = _swa_unprep(dkr_acc[...], kn[0], kn[1], kw_ref[...], kwp_ref[...], c4, s4, bd, lo32)
            dk_ref[...] = dk.astype(BF16)
            dkw_ref[...] = dkw
            dv_ref[...] = dv_acc[...].astype(BF16)

    full = pl.BlockSpec((L, 128), lambda p: (0, 0))
    wv = pl.BlockSpec((1, 128), lambda p: (0, 0))
    sk = pl.BlockSpec((1, 1, 128), lambda p: (p, 0, 0))
    vec = jax.ShapeDtypeStruct((1, 128), F32)
    skv = jax.ShapeDtypeStruct((NPAIR, 1, 128), F32)
    return pl.pallas_call(
        body, name=name, grid=(NPAIR,),
        in_specs=_swa_in_specs() + [pl.BlockSpec((L, 128), lambda p: (0, SSD_W // 128 + p))],
        out_specs=[pl.BlockSpec((L, 128), lambda p: (0, p)), full, full, wv, wv, sk, sk],
        out_shape=[jax.ShapeDtypeStruct((L, ATT_W), BF16), jax.ShapeDtypeStruct((L, 128), BF16),
                   jax.ShapeDtypeStruct((L, 128), BF16), vec, vec, skv, skv],
        scratch_shapes=[pltpu.VMEM((L + Q, 128), BF16), pltpu.VMEM((L + Q, 128), BF16),
                        pltpu.VMEM((2, L, 2 * Q), F32), pltpu.VMEM((2, L, 2 * Q), F32),
                        pltpu.VMEM((2, L, 2 * Q), BF16), pltpu.VMEM((2, L, 2 * Q), BF16),
                        pltpu.VMEM((L, 128), F32), pltpu.VMEM((L, 128), F32), pltpu.VMEM((L, 128), F32)],
        compiler_params=_cp(("arbitrary",)))(u, u, u, cos4, sin4, qw2, qw2p, kw2, kw2p, sink_e, sink_o, bd, dy)


def _ssd_consts():
    hh = jnp.arange(128)[:, None]
    e = (hh == (jnp.arange(SSD_W)[None, :] // HP)).astype(BF16)
    e2 = (hh == (jnp.arange(NH * 128)[None, :] // 128)).astype(BF16)
    et = e.T
    tril = (jnp.arange(Q)[:, None] >= jnp.arange(Q)[None, :]).astype(BF16)
    triu = tril.T
    eye = jnp.eye(128, dtype=BF16)
    return e, e2, et, tril, triu, eye


def _ssd_common(x_ref, ext_scr, cw_ref, cb_ref, dt_ref, dtb_ref, alog_ref, e_ref, e2_ref, tril_ref, triu_ref,
                arow_scr, acol_scr, eax_scr):
    conv = jnp.broadcast_to(cb_ref[...], (Q, XBC))
    for k in range(4):
        conv = conv + cw_ref[k:k + 1, :] * ext_scr[pl.ds(5 + k, Q), :]
    sg = _sigmoid(conv)
    xbc = conv * sg
    dtpre = dt_ref[...] + dtb_ref[...]
    dt = _softplus(dtpre)
    a = -jnp.exp(alog_ref[...])
    adt = dt * a
    acol = _xdot_r(tril_ref[...], adt)
    acol_scr[...] = acol
    arow_scr[...] = _xdot_l(adt, triu_ref[...], 0, 0)
    alast = acol_scr[Q - 1:Q, :]
    ea = jnp.exp(acol)
    decs = jnp.exp(alast - acol)
    e = e_ref[...]
    dt_x = _xdot_l(dt, e)
    eax_scr[...] = _xdot_l(ea, e)
    decs_x = _xdot_l(decs, e)
    acx2 = _xdot_l(acol, e2_ref[...])
    return conv, sg, xbc, dtpre, dt, a, adt, acol, alast, ea, decs, dt_x, decs_x, acx2


def _ssd_fwd(u, cw, cb, dtb, alog, dxp, nw, consts, name):
    e, e2, et, tril, triu, eye = consts

    def body(z0_ref, z1_ref, x_ref, dt_ref, cw_ref, cb_ref, dtb_ref, alog_ref, dx_ref, nw_ref, e_ref, e2_ref,
             tril_ref, triu_ref, ya_ref, ypre_ref, st_ref, s_scr, ext_scr, arow_scr, acol_scr, eax_scr):
        c = pl.program_id(0)

        @pl.when(c == 0)
        def _():
            s_scr[...] = jnp.zeros_like(s_scr)
            ext_scr[0:8, :] = jnp.zeros((8, XBC), F32)
        ext_scr[8:8 + Q, :] = x_ref[...]
        (conv, sg, xbc, dtpre, dt, a, adt, acol, alast, ea, decs, dt_x, decs_x, acx2) = _ssd_common(
            x_ref, ext_scr, cw_ref, cb_ref, dt_ref, dtb_ref, alog_ref, e_ref, e2_ref, tril_ref, triu_ref,
            arow_scr, acol_scr, eax_scr)
        ext_scr[0:8, :] = ext_scr[Q:Q + 8, :]
        xs = xbc[:, :SSD_W]
        xdt = xs * dt_x
        lane = lax.broadcasted_iota(jnp.int32, (Q, 128), 1)
        causal = lax.broadcasted_iota(jnp.int32, (Q, Q), 0) >= lax.broadcasted_iota(jnp.int32, (Q, Q), 1)
        for g in range(2):
            bg = xbc[:, SSD_W + g * NS:SSD_W + (g + 1) * NS].astype(BF16)
            cg = xbc[:, SSD_W + 2 * NS + g * NS:SSD_W + 2 * NS + (g + 1) * NS].astype(BF16)
            cbm = _dg(cg, bg, 1, 1)
            sgv = s_scr[g]
            st_ref[0, g] = sgv
            gc = slice(g * 512, (g + 1) * 512)
            yoff = _dg(cg, sgv.astype(BF16), 1, 0) * eax_scr[:, gc]
            for pr in range(4):
                h0 = g * 8 + 2 * pr
                h1 = h0 + 1
                c0 = g * 512 + pr * 128
                xp = xdt[:, c0:c0 + 128].astype(BF16)
                w0 = (cbm * jnp.exp(jnp.where(causal, acx2[:, h0 * 128:(h0 + 1) * 128] - arow_scr[h0:h0 + 1, :],
                                              NEG))).astype(BF16)
                w1 = (cbm * jnp.exp(jnp.where(causal, acx2[:, h1 * 128:(h1 + 1) * 128] - arow_scr[h1:h1 + 1, :],
                                              NEG))).astype(BF16)
                yd = jnp.where(lane < HP, _dg(w0, xp, 1, 0), _dg(w1, xp, 1, 0))
                ypre_ref[:, c0:c0 + 128] = (yd + yoff[:, pr * 128:(pr + 1) * 128]
                                            + xs[:, c0:c0 + 128] * dx_ref[:, c0:c0 + 128])
            contrib = _dg(bg, (xdt[:, gc] * decs_x[:, gc]).astype(BF16), 0, 0)
            s_scr[g] = sgv * eax_scr[Q - 1:Q, gc] + contrib
        for g, zr in enumerate((z0_ref, z1_ref)):
            gc = slice(g * 512, (g + 1) * 512)
            zz = zr[...]
            ggg = ypre_ref[:, gc] * (zz * _sigmoid(zz))
            rstd = lax.rsqrt(jnp.mean(ggg * ggg, axis=-1, keepdims=True) + RMS_EPS)
            ya_ref[:, gc] = (ggg * rstd * nw_ref[:, gc]).astype(BF16)

    def row(w, blk=0):
        return pl.BlockSpec((Q, w), lambda c: (c, blk))

    def full(shape):
        return pl.BlockSpec(shape, lambda c: (0,) * len(shape))

    return pl.pallas_call(
        body, name=name, grid=(NC,),
        in_specs=[row(512, U_Z // 512), row(512, U_Z // 512 + 1), row(XBC, U_X // XBC), row(128, U_DT // 128),
                  full((4, XBC)), full((1, XBC)), full((1, 128)), full((1, 128)),
                  full((1, SSD_W)), full((1, SSD_W)), full((128, SSD_W)), full((128, NH * 128)), full((Q, Q)),
                  full((Q, Q))],
        out_specs=[row(SSD_W), row(SSD_W), pl.BlockSpec((1, 2, NS, 512), lambda c: (c, 0, 0, 0))],
        out_shape=[jax.ShapeDtypeStruct((L, 2 * D), BF16), jax.ShapeDtypeStruct((L, SSD_W), F32),
                   jax.ShapeDtypeStruct((NC, 2, NS, 512), F32)],
        scratch_shapes=[pltpu.VMEM((2, NS, 512), F32), pltpu.VMEM((Q + 8, XBC), F32), pltpu.VMEM((128, Q), F32),
                        pltpu.VMEM((Q, 128), F32), pltpu.VMEM((Q, SSD_W), F32)],
        compiler_params=_cp(("arbitrary",)))(u, u, u, u, cw, cb, dtb, alog, dxp, nw, e, e2, tril, triu)


def _ssd_bwd(u, ypre, st, dy, cw, cb, dtb, alog, dxp, nw, consts, name):
    e, e2, et, tril, triu, eye = consts

    def body(z0_ref, z1_ref, x_ref, xp_ref, dt_ref, ypre_ref, st_ref, dya_ref, cw_ref, cb_ref, dtb_ref, alog_ref, dx_ref,
             nw_ref, e_ref, e2_ref, et_ref, tril_ref, triu_ref, eye_ref,
             dz_ref, dxr_ref, ddtr_ref, dcw_ref, dcb_ref, ddtb_ref, dalog_ref, dd_ref, dnw_ref,
             g_scr, ext_scr, ext2_scr, arow_scr, acol_scr, eax_scr, darow_scr, dxdt_scr, t1_scr, t2_scr, dgg_scr):
        i = pl.program_id(0)

        @pl.when(i == 0)
        def _():
            g_scr[...] = jnp.zeros_like(g_scr)
            ext2_scr[Q:Q + 8, :] = jnp.zeros((8, XBC), F32)
            for r in (dcw_ref, dcb_ref, ddtb_ref, dalog_ref, dd_ref, dnw_ref):
                r[...] = jnp.zeros_like(r)
        not_first = jnp.where(i < NC - 1, 1.0, 0.0)
        ext_scr[0:8, :] = xp_ref[Q - 8:Q, :] * not_first
        ext_scr[8:8 + Q, :] = x_ref[...]
        (conv, sg, xbc, dtpre, dt, a, adt, acol, alast, ea, decs, dt_x, decs_x, acx2) = _ssd_common(
            x_ref, ext_scr, cw_ref, cb_ref, dt_ref, dtb_ref, alog_ref, e_ref, e2_ref, tril_ref, triu_ref,
            arow_scr, acol_scr, eax_scr)
        et_m = et_ref[...]
        xs = xbc[:, :SSD_W]
        xdt = xs * dt_x
        y = ypre_ref[...]
        zz = jnp.concatenate([z0_ref[...], z1_ref[...]], axis=1)
        sz = _sigmoid(zz)
        silu_z = zz * sz
        gg = y * silu_z
        dya = dya_ref[...]
        for g in range(2):
            gc = slice(g * 512, (g + 1) * 512)
            ggg = gg[:, gc]
            rstd = lax.rsqrt(jnp.mean(ggg * ggg, axis=-1, keepdims=True) + RMS_EPS)
            n = ggg * rstd
            dyag = dya[:, gc]
            dnw_ref[:, gc] += jnp.sum(dyag * n, axis=0, keepdims=True)
            dn = dyag * nw_ref[:, gc]
            dgg_scr[:, gc] = rstd * (dn - n * jnp.mean(dn * n, axis=-1, keepdims=True))
        dgg = dgg_scr[...]
        dy = dgg * silu_z
        dz_ref[...] = (dgg * y * (sz * (1.0 + zz * (1.0 - sz)))).astype(BF16)
        dd_ref[...] += _rowdot(jnp.sum(dy * xs, axis=0, keepdims=True), et_m)
        dxs = dy * dx_ref[...]
        dys = dy * eax_scr[...]
        lane = lax.broadcasted_iota(jnp.int32, (Q, 128), 1)
        causal = lax.broadcasted_iota(jnp.int32, (Q, Q), 0) >= lax.broadcasted_iota(jnp.int32, (Q, Q), 1)
        darow_scr[...] = jnp.zeros_like(darow_scr)
        dacol = jnp.zeros((Q, 128), F32)
        dcdx = []
        dbs = []
        dcs = []
        for g in range(2):
            gc = slice(g * 512, (g + 1) * 512)
            bg = xbc[:, SSD_W + g * NS:SSD_W + (g + 1) * NS].astype(BF16)
            cg = xbc[:, SSD_W + 2 * NS + g * NS:SSD_W + 2 * NS + (g + 1) * NS].astype(BF16)
            cbm = _dg(cg, bg, 1, 1)
            sgv = st_ref[0, g]
            sgb = sgv.astype(BF16)
            gv = g_scr[g]
            gvb = gv.astype(BF16)
            yoff = _dg(cg, sgb, 1, 0) * eax_scr[:, gc]
            dysg = dys[:, gc].astype(BF16)
            dcg = _dg(dysg, sgb, 1, 1)
            ds_off = _dg(cg, dysg, 0, 0)
            t1_scr[:, gc] = dy[:, gc] * yoff
            xdec = xdt[:, gc] * decs_x[:, gc]
            dxd = _dg(bg, gvb, 1, 0)
            dbg = _dg(xdec.astype(BF16), gvb, 1, 1)
            dxdt_g = dxd * decs_x[:, gc]
            t2_scr[:, gc] = dxd * xdt[:, gc]
            cdx = eax_scr[Q - 1:Q, gc]
            dcdx.append(jnp.sum(gv * sgv, axis=0, keepdims=True))
            g_scr[g] = gv * cdx + ds_off
            dcb_acc = jnp.zeros((Q, Q), F32)
            for pr in range(4):
                c0 = g * 512 + pr * 128
                xp = xdt[:, c0:c0 + 128].astype(BF16)
                dyp = dy[:, c0:c0 + 128]
                dypb = dyp.astype(BF16)
                halves = []
                for hh, keep in ((g * 8 + 2 * pr, lane < HP), (g * 8 + 2 * pr + 1, lane >= HP)):
                    lam = jnp.exp(jnp.where(causal, acx2[:, hh * 128:(hh + 1) * 128] - arow_scr[hh:hh + 1, :], NEG))
                    w = cbm * lam
                    dw = _dg(jnp.where(keep, dyp, 0.0).astype(BF16), xp, 1, 1)
                    dcb_acc = dcb_acc + dw * lam
                    t = dw * w
                    dacol = dacol + jnp.sum(t, axis=-1, keepdims=True) * (lane == hh).astype(F32)
                    darow_scr[hh:hh + 1, :] -= jnp.sum(t, axis=0, keepdims=True)
                    halves.append(_dg(w.astype(BF16), dypb, 0, 0))
                dxdt_scr[:, c0:c0 + 128] = (jnp.where(lane < HP, halves[0], halves[1])
                                            + dxdt_g[:, pr * 128:(pr + 1) * 128])
            dcbb = dcb_acc.astype(BF16)
            dcs.append(dcg + _dg(dcbb, bg, 1, 0))
            dbs.append(dbg + _dg(dcbb, cg, 0, 0))
        dacol = dacol + _xdot_l(t1_scr[...], et_m)
        ddecs = _xdot_l(t2_scr[...], et_m) * decs
        dacol = dacol - ddecs
        dalast = jnp.sum(ddecs, axis=0, keepdims=True)
        dcd = _rowdot(jnp.concatenate(dcdx, axis=1), et_m)
        dalast = dalast + dcd * jnp.exp(alast)
        dacol = dacol + _xdot_l(darow_scr[...], eye_ref[...], 0, 0)
        rowi = lax.broadcasted_iota(jnp.int32, (Q, 128), 0)
        dacol = dacol + jnp.where(rowi == Q - 1, dalast, 0.0)
        dadt = _xdot_r(triu_ref[...], dacol)
        dxdt = dxdt_scr[...]
        ddt = dadt * a + _xdot_l(dxdt * xs, et_m)
        dalog_ref[...] += jnp.sum(dadt * dt, axis=0, keepdims=True) * a
        dxs = dxs + dxdt * dt_x
        ddtr = ddt * _sigmoid(dtpre)
        ddtb_ref[...] += jnp.sum(ddtr, axis=0, keepdims=True)
        ddtr_ref[...] = ddtr.astype(BF16)
        dsilu = sg * (1.0 + conv * (1.0 - sg))
        ext2_scr[0:Q, 0:SSD_W] = dxs * dsilu[:, :SSD_W]
        for g in range(2):
            o1 = SSD_W + g * NS
            o2 = SSD_W + 2 * NS + g * NS
            ext2_scr[0:Q, o1:o1 + NS] = dbs[g] * dsilu[:, o1:o1 + NS]
            ext2_scr[0:Q, o2:o2 + NS] = dcs[g] * dsilu[:, o2:o2 + NS]
        dconv = ext2_scr[0:Q, :]
        dcb_ref[...] += jnp.sum(dconv, axis=0, keepdims=True)
        dxr = jnp.zeros((Q, XBC), F32)
        for k in range(4):
            dcw_ref[k:k + 1, :] += jnp.sum(dconv * ext_scr[pl.ds(5 + k, Q), :], axis=0, keepdims=True)
            dxr = dxr + cw_ref[k:k + 1, :] * ext2_scr[pl.ds(3 - k, Q), :]
        dxr_ref[...] = dxr.astype(BF16)
        ext2_scr[Q:Q + 8, :] = ext2_scr[0:8, :]

    def row(w, blk=0):
        return pl.BlockSpec((Q, w), lambda i: (NC - 1 - i, blk))

    def full(shape):
        return pl.BlockSpec(shape, lambda i: (0,) * len(shape))

    prev = pl.BlockSpec((Q, XBC), lambda i: (jnp.maximum(NC - 2 - i, 0), U_X // XBC))
    return pl.pallas_call(
        body, name=name, grid=(NC,),
        in_specs=[row(512, U_Z // 512), row(512, U_Z // 512 + 1), row(XBC, U_X // XBC), prev, row(128, U_DT // 128),
                  row(SSD_W),
                  pl.BlockSpec((1, 2, NS, 512), lambda i: (NC - 1 - i, 0, 0, 0)), row(SSD_W),
                  full((4, XBC)), full((1, XBC)), full((1, 128)), full((1, 128)), full((1, SSD_W)),
                  full((1, SSD_W)), full((128, SSD_W)), full((128, NH * 128)), full((SSD_W, 128)), full((Q, Q)),
                  full((Q, Q)), full((128, 128))],
        out_specs=[row(SSD_W), row(XBC), row(128), full((8, XBC)), full((1, XBC)), full((1, 128)), full((1, 128)),
                   full((1, 128)), full((1, SSD_W))],
        out_shape=[jax.ShapeDtypeStruct((L, SSD_W), BF16), jax.ShapeDtypeStruct((L, XBC), BF16),
                   jax.ShapeDtypeStruct((L, 128), BF16), jax.ShapeDtypeStruct((8, XBC), F32),
                   jax.ShapeDtypeStruct((1, XBC), F32), jax.ShapeDtypeStruct((1, 128), F32),
                   jax.ShapeDtypeStruct((1, 128), F32), jax.ShapeDtypeStruct((1, 128), F32),
                   jax.ShapeDtypeStruct((1, SSD_W), F32)],
        scratch_shapes=[pltpu.VMEM((2, NS, 512), F32), pltpu.VMEM((Q + 8, XBC), F32), pltpu.VMEM((Q + 8, XBC), F32),
                        pltpu.VMEM((128, Q), F32), pltpu.VMEM((Q, 128), F32), pltpu.VMEM((Q, SSD_W), F32),
                        pltpu.VMEM((128, Q), F32), pltpu.VMEM((Q, SSD_W), F32), pltpu.VMEM((Q, SSD_W), F32),
                        pltpu.VMEM((Q, SSD_W), F32), pltpu.VMEM((Q, SSD_W), F32)],
        compiler_params=_cp(("arbitrary",)))(u, u, u, u, u, ypre, st, dy, cw, cb, dtb, alog, dxp, nw,
                                             e, e2, et, tril, triu, eye)


def _my_pos():
    return lax.axis_index("x"), lax.axis_index("y"), lax.axis_index("c")


CHIP_REL = ((1, 0), (0, 1), (1, 1))
CHIP_XOR = (2, 1, 3)
BIG = ("w_in", "w_out", "w_mlp_up", "w_mlp_down")
NW = len(BIG)
AT = 256


def _chips(x, y):
    return [(1 - x if dx else x, 1 - y if dy else y) for dx, dy in CHIP_REL]


HBM_SPEC = pl.BlockSpec(memory_space=pltpu.HBM)
SEM_SPEC = pl.BlockSpec(memory_space=pltpu.SEMAPHORE)
EFFECT = pltpu.SideEffectType.DATAFLOW_SIDE_EFFECTING


def _hbm(t):
    return pltpu.with_memory_space_constraint(t, pltpu.HBM)


def _split_start(srcs, lands, after, copies, name):
    n = len(srcs)

    def body(*refs):
        src_refs, land_refs = refs[:n], refs[n:2 * n]
        send_sems, recv_sems = refs[2 * n + 1], refs[2 * n + 2]
        token = refs[-1]
        for w, k, src, dst, dev in copies(src_refs, land_refs):
            pltpu.make_async_remote_copy(src_ref=src, dst_ref=dst, send_sem=send_sems.at[3 * w + k],
                                         recv_sem=recv_sems.at[3 * w + k], device_id=dev, device_id_type=MESH).start()
        token[...] = jnp.zeros_like(token)

    outs = pl.pallas_call(
        body, name=name,
        out_shape=(pltpu.SemaphoreType.DMA((3 * n,)), pltpu.SemaphoreType.DMA((3 * n,)),
                   *[pltpu.HBM(t.shape, t.dtype) for t in srcs], *[pltpu.HBM(t.shape, t.dtype) for t in lands],
                   jax.ShapeDtypeStruct((8, 128), F32)),
        in_specs=[HBM_SPEC] * (2 * n) + [ANY],
        out_specs=(SEM_SPEC, SEM_SPEC, *([HBM_SPEC] * (2 * n)), VMEM_SPEC),
        input_output_aliases={i: 2 + i for i in range(2 * n)},
        compiler_params=pltpu.CompilerParams(has_side_effects=EFFECT))(
            *[_hbm(t) for t in srcs], *[_hbm(t) for t in lands], after)
    return outs[0], outs[1], list(outs[2:2 + n]), list(outs[2 + n:2 + 2 * n]), outs[-1]


def _split_wait(send_sems, recv_sems, srcs, lands, after, copies, name):
    n = len(srcs)

    def body(*refs):
        src_refs, land_refs = refs[:n], refs[n:2 * n]
        ssem, rsem = refs[2 * n], refs[2 * n + 1]
        for w, k, src, dst, dev in copies(src_refs, land_refs):
            cp = pltpu.make_async_remote_copy(src_ref=src, dst_ref=dst, send_sem=ssem.at[3 * w + k],
                                              recv_sem=rsem.at[3 * w + k], device_id=dev, device_id_type=MESH)
            cp.wait_send()
            cp.wait_recv()

    outs = pl.pallas_call(
        body, name=name,
        out_shape=tuple([pltpu.HBM(t.shape, t.dtype) for t in srcs] + [pltpu.HBM(t.shape, t.dtype) for t in lands]),
        in_specs=[HBM_SPEC] * (2 * n) + [SEM_SPEC, SEM_SPEC, ANY],
        out_specs=tuple([HBM_SPEC] * (2 * n)),
        input_output_aliases={i: i for i in range(2 * n)},
        compiler_params=pltpu.CompilerParams(has_side_effects=EFFECT))(*srcs, *lands, send_sems, recv_sems, after)
    return list(outs[:n]), list(outs[n:])


def _ag_copies(arrival):
    def copies(src_refs, land_refs):
        x, y, c = _my_pos()
        s = 2 * x + y
        chips = _chips(x, y)
        for w in range(len(src_refs)):
            hr = src_refs[w].shape[0] // 2
            mine = pl.ds(c * hr, hr)
            for k in range(3):
                slot = s ^ CHIP_XOR[k] if arrival else s
                yield w, k, src_refs[w].at[mine], land_refs[w].at[slot, mine], (*chips[k], c)
    return copies


def _ag_forward(lands, name):
    n = len(lands)

    def body(*refs):
        outs = refs[n:2 * n]
        send_sems, recv_sems = refs[2 * n:]
        x, y, c = _my_pos()
        s = 2 * x + y
        sib = (x, y, 1 - c)
        sends = []
        for w in range(n):
            hr = outs[w].shape[1] // 2
            for k in range(3):
                blk = outs[w].at[s ^ CHIP_XOR[k], pl.ds(c * hr, hr)]
                fw = pltpu.make_async_remote_copy(
                    src_ref=blk, dst_ref=blk, send_sem=send_sems.at[w, k], recv_sem=recv_sems.at[w, k],
                    device_id=sib, device_id_type=MESH)
                fw.start()
                sends.append(fw)
        for w in range(n):
            hr = outs[w].shape[1] // 2
            for k in range(3):
                blk = outs[w].at[s ^ CHIP_XOR[k], pl.ds((1 - c) * hr, hr)]
                pltpu.make_async_remote_copy(
                    src_ref=blk, dst_ref=blk, send_sem=send_sems.at[w, k], recv_sem=recv_sems.at[w, k],
                    device_id=sib, device_id_type=MESH).wait_recv()
        for cp in sends:
            cp.wait_send()

    return pl.pallas_call(
        body, name=name, in_specs=[ANY] * n, out_specs=[ANY] * n,
        out_shape=[jax.ShapeDtypeStruct(t.shape, t.dtype) for t in lands],
        input_output_aliases={w: w for w in range(n)},
        scratch_shapes=[pltpu.SemaphoreType.DMA((n, 3)), pltpu.SemaphoreType.DMA((n, 3))])(*lands)


def _rs_copies(src_refs, land_refs):
    x, y, c = _my_pos()
    s = 2 * x + y
    chips = _chips(x, y)
    for w in range(len(src_refs)):
        for k in range(3):
            yield w, k, src_refs[w].at[s ^ CHIP_XOR[k]], land_refs[w].at[k], (*chips[k], c)


def _place_own(shard, gathered, sidx, name):
    r, cc = shard.shape
    at = _rs_rows(r)

    def body(s_ref, a_ref, g_ref, o_ref):
        o_ref[0] = a_ref[...]

    return pl.pallas_call(
        body, name=name,
        grid_spec=pltpu.PrefetchScalarGridSpec(
            num_scalar_prefetch=1, grid=(r // at,),
            in_specs=[pl.BlockSpec((at, cc), lambda i, s_ref: (i, 0)), ANY],
            out_specs=pl.BlockSpec((1, at, cc), lambda i, s_ref: (s_ref[0], i, 0))),
        out_shape=jax.ShapeDtypeStruct(gathered.shape, gathered.dtype),
        input_output_aliases={2: 0}, compiler_params=_cp(("parallel",)))(sidx, shard, gathered)


def _rs_pair(dwb, name):
    n = len(dwb)

    def body(*refs):
        ins, outs = refs[:n], refs[n:2 * n]
        send_sems, recv_sems = refs[2 * n:]
        x, y, c = _my_pos()
        cps = []
        for w in range(n):
            hr = ins[w].shape[1] // 2
            cp = pltpu.make_async_remote_copy(
                src_ref=ins[w].at[:, pl.ds((1 - c) * hr, hr)], dst_ref=outs[w], send_sem=send_sems.at[w],
                recv_sem=recv_sems.at[w], device_id=(x, y, 1 - c), device_id_type=MESH)
            cp.start()
            cps.append(cp)
        for cp in cps:
            cp.wait()

    return pl.pallas_call(
        body, name=name, in_specs=[ANY] * n, out_specs=[ANY] * n,
        out_shape=[jax.ShapeDtypeStruct((4, t.shape[1] // 2, t.shape[2]), t.dtype) for t in dwb],
        scratch_shapes=[pltpu.SemaphoreType.DMA((n,)), pltpu.SemaphoreType.DMA((n,))])(*dwb)


def _rs_sib(q, name):
    n = len(q)

    def body(*refs):
        outs = refs[n:2 * n]
        send_sems, recv_sems = refs[2 * n:]
        x, y, c = _my_pos()
        cps = []
        for w in range(n):
            hr = outs[w].shape[0] // 2
            mine = pl.ds(c * hr, hr)
            cp = pltpu.make_async_remote_copy(
                src_ref=outs[w].at[mine], dst_ref=outs[w].at[mine], send_sem=send_sems.at[w],
                recv_sem=recv_sems.at[w], device_id=(x, y, 1 - c), device_id_type=MESH)
            cp.start()
            cps.append(cp)
        for w in range(n):
            hr = outs[w].shape[0] // 2
            other = outs[w].at[pl.ds((1 - c) * hr, hr)]
            pltpu.make_async_remote_copy(
                src_ref=other, dst_ref=other, send_sem=send_sems.at[w], recv_sem=recv_sems.at[w],
                device_id=(x, y, 1 - c), device_id_type=MESH).wait_recv()
        for cp in cps:
            cp.wait_send()

    return pl.pallas_call(
        body, name=name, in_specs=[ANY] * n, out_specs=[ANY] * n,
        out_shape=[jax.ShapeDtypeStruct(t.shape, t.dtype) for t in q],
        input_output_aliases={w: w for w in range(n)},
        scratch_shapes=[pltpu.SemaphoreType.DMA((n,)), pltpu.SemaphoreType.DMA((n,))])(*q)


def _rs_rows(hr):
    return 2 * AT if hr % (2 * AT) == 0 else AT


def _rs_add2(dw, got, scidx, name):
    _, r, cc = dw.shape
    hr = r // 2
    at = _rs_rows(hr)
    nb = hr // at

    def body(s_ref, a_ref, b_ref, o_ref, ob_ref):
        acc = a_ref[...] + b_ref[...].astype(F32)
        ob_ref[...] = acc.astype(BF16)

        @pl.when(pl.program_id(1) == s_ref[0])
        def _():
            o_ref[...] = acc[0]

    blk = pl.BlockSpec((1, at, cc), lambda i, sh, s_ref: (sh, i, 0))
    return pl.pallas_call(
        body, name=name,
        grid_spec=pltpu.PrefetchScalarGridSpec(
            num_scalar_prefetch=1, grid=(nb, 4),
            in_specs=[pl.BlockSpec((1, at, cc), lambda i, sh, s_ref: (sh, s_ref[1] * nb + i, 0)), blk],
            out_specs=[pl.BlockSpec((at, cc), lambda i, sh, s_ref: (i, 0)), blk]),
        out_shape=[jax.ShapeDtypeStruct((hr, cc), F32), jax.ShapeDtypeStruct((4, hr, cc), BF16)],
        compiler_params=_cp(("parallel", "arbitrary")))(scidx, dw, got)


def _rs_add4(p, got, scidx, name):
    hr, cc = p.shape
    at = _rs_rows(hr)
    nb = hr // at

    def body(s_ref, p_ref, g0_ref, g1_ref, g2_ref, o_ref):
        acc = p_ref[...] + g0_ref[0].astype(F32)
        acc = acc + g1_ref[0].astype(F32)
        o_ref[...] = acc + g2_ref[0].astype(F32)

    def gk(k):
        return pl.BlockSpec((1, at, cc), lambda i, s_ref: (k, i, 0))

    return pl.pallas_call(
        body, name=name,
        grid_spec=pltpu.PrefetchScalarGridSpec(
            num_scalar_prefetch=1, grid=(nb,),
            in_specs=[pl.BlockSpec((at, cc), lambda i, s_ref: (i, 0)), gk(0), gk(1), gk(2)],
            out_specs=pl.BlockSpec((at, cc), lambda i, s_ref: (s_ref[1] * nb + i, 0))),
        out_shape=jax.ShapeDtypeStruct((2 * hr, cc), F32),
        compiler_params=_cp(("parallel",)))(scidx, p, got, got, got)


def _rs_begin(dws, dwbs, after, tag=""):
    x, y, c = _my_pos()
    scidx = jnp.stack([2 * x + y, c]).astype(jnp.int32)
    got = _rs_pair(dwbs, "rs_pair" + tag)
    pairs = [_rs_add2(dws[w], got[w], scidx, "rs_add2") for w in range(len(dws))]
    pb = [p[1] for p in pairs]
    lands = [lax.empty((3,) + t.shape[1:], BF16) for t in pb]
    ssem, rsem, pb, lands, token = _split_start(pb, lands, after, _rs_copies, "rs_chip_start" + tag)
    return ([p[0] for p in pairs], ssem, rsem, pb, lands), token


def _rs_end(state, after, tag=""):
    x, y, c = _my_pos()
    scidx = jnp.stack([2 * x + y, c]).astype(jnp.int32)
    p, ssem, rsem, pb, lands = state
    _, recv = _split_wait(ssem, rsem, pb, lands, after, _rs_copies, "rs_chip_wait" + tag)
    q = [_rs_add4(p[w], recv[w], scidx, "rs_add4") for w in range(len(p))]
    return _rs_sib(q, "rs_sib" + tag)


def _allreduce_small(buf, name):
    rows = buf.shape[0]

    def body(src_ref, out_ref, gat_ref, send_sems, recv_sems):
        x, y, c = _my_pos()
        me = 4 * x + 2 * y + c
        gat_ref[me] = src_ref[...]
        cps = []
        for r in range(1, N_DEV):
            tx = 1 - x if (r >> 2) & 1 else x
            ty = 1 - y if (r >> 1) & 1 else y
            tc = 1 - c if r & 1 else c
            cps.append(pltpu.make_async_remote_copy(
                src_ref=src_ref, dst_ref=gat_ref.at[me], send_sem=send_sems.at[r - 1], recv_sem=recv_sems.at[r - 1],
                device_id=(tx, ty, tc), device_id_type=MESH))
        for cp in cps:
            cp.start()
        for cp in cps:
            cp.wait()
        acc = gat_ref[0]
        for k in range(1, N_DEV):
            acc = acc + gat_ref[k]
        out_ref[...] = acc

    return pl.pallas_call(
        body, name=name, in_specs=[VMEM_SPEC], out_specs=VMEM_SPEC, out_shape=jax.ShapeDtypeStruct((rows, 128), F32),
        scratch_shapes=[pltpu.VMEM((N_DEV, rows, 128), F32), pltpu.SemaphoreType.DMA((N_DEV - 1,)),
                        pltpu.SemaphoreType.DMA((N_DEV - 1,))],
        compiler_params=_cp())(buf)


SMALL = (("norm_mix_w", (D,)), ("ssd_conv_w", (4, XBC)), ("ssd_conv_b", (XBC,)), ("ssd_dt_bias", (NH,)),
         ("ssd_a_log", (NH,)), ("ssd_d", (NH,)), ("ssd_norm_w", (SSD_W,)), ("q_norm_w", (HD,)),
         ("k_norm_w", (HD,)), ("attn_sinks", (NQH,)), ("cm_dw_w", (CMK, CMC)), ("cm_dw_b", (CMC,)),
         ("cm_ln_w", (CMC,)), ("cm_ln_b", (CMC,)), ("norm_mlp_w", (D,)))
SHARDED_SMALL = ("ssd_conv_w", "cm_dw_w")


def _seg_len(shape):
    n = 1
    for d in shape:
        n *= d
    return -(-n // 128) * 128


def _pack_small(vals, names):
    parts = []
    for name, shape in SMALL:
        if name not in names:
            continue
        v = vals[name].reshape(DEPTH, -1)
        pad = _seg_len(shape) - v.shape[1]
        parts.append(jnp.pad(v, ((0, 0), (0, pad))))
    flat = jnp.concatenate(parts, axis=1)
    return flat.reshape(-1, 128)


def _unpack_small(buf, names):
    flat = buf.reshape(DEPTH, -1)
    out = {}
    off = 0
    for name, shape in SMALL:
        if name not in names:
            continue
        n = 1
        for d in shape:
            n *= d
        out[name] = flat[:, off:off + n].reshape((DEPTH,) + shape)
        off += _seg_len(shape)
    return out


SW = N_IN // 4
SWP = 1152
ORIG = (("z", 0, 1024), ("x", 1024, 2560), ("dt", 2560, 2576), ("q", 2576, 3088), ("k", 3088, 3216),
        ("v", 3216, 3344), ("a", 3344, 3856), ("g", 3856, 4368))


def _orig_cols(g_in, lo, hi):
    out = []
    for s in range(4):
        a, b = max(lo, s * SW), min(hi, (s + 1) * SW)
        if a < b:
            out.append(g_in[s][:, a - s * SW:b - s * SW])
    return out


def _shard_major(parts):
    cols = []
    for s in range(4):
        for name, g0, g1 in ORIG:
            a, b = max(g0, s * SW), min(g1, (s + 1) * SW)
            if a < b:
                cols.append(parts[name][:, a - g0:b - g0])
        cols.append(jnp.zeros((L, SWP - SW), BF16))
    return jnp.concatenate(cols, axis=1)


def _rope_tables():
    inv = 10000.0 ** (-jnp.arange(0, HD, 2, dtype=F32) / HD)
    ang = jnp.arange(L, dtype=F32)[:, None] * inv[None, :]
    return jnp.cos(ang), jnp.sin(ang)


def _swa_tables():
    cos, sin = _rope_tables()
    return jnp.tile(cos, (1, 4)), jnp.tile(jnp.concatenate([-sin, sin], axis=1), (1, 2))


def _swa_weights(w):
    return jnp.tile(w, 2)[None], jnp.tile(jnp.concatenate([w[HH:], w[:HH]]), 2)[None]


def _swa_sinks(s):
    s2 = s.reshape(NPAIR, 2)
    return (jnp.broadcast_to(s2[:, 0][:, None, None], (NPAIR, 1, 128)),
            jnp.broadcast_to(s2[:, 1][:, None, None], (NPAIR, 1, 128)))


def _swa_blockdiag():
    i = jnp.arange(128)
    return (i[:, None] // HD == i[None, :] // HD).astype(BF16)


def _pad128(v):
    return jnp.pad(v, (0, 128 - v.shape[0]))[None, :]


def kernel(x, norm_mix_w, w_in, ssd_conv_w, ssd_conv_b, ssd_dt_bias, ssd_a_log, ssd_d, ssd_norm_w, q_norm_w, k_norm_w, attn_sinks, cm_dw_w, cm_dw_b, cm_ln_w, cm_ln_b, w_out, norm_mlp_w, w_mlp_up, w_mlp_down, loss_target, m_norm_mix_w, m_w_in, m_ssd_conv_w, m_ssd_conv_b, m_ssd_dt_bias, m_ssd_a_log, m_ssd_d, m_ssd_norm_w, m_q_norm_w, m_k_norm_w, m_attn_sinks, m_cm_dw_w, m_cm_dw_b, m_cm_ln_w, m_cm_ln_b, m_w_out, m_norm_mlp_w, m_w_mlp_up, m_w_mlp_down, v_norm_mix_w, v_w_in, v_ssd_conv_w, v_ssd_conv_b, v_ssd_dt_bias, v_ssd_a_log, v_ssd_d, v_ssd_norm_w, v_q_norm_w, v_k_norm_w, v_attn_sinks, v_cm_dw_w, v_cm_dw_b, v_cm_ln_w, v_cm_ln_b, v_w_out, v_norm_mlp_w, v_w_mlp_up, v_w_mlp_down):
    px, py, pc = _my_pos()
    shard = 2 * px + py
    sidx = jnp.reshape(shard, (1,)).astype(jnp.int32)
    consts = _ssd_consts()
    cos4, sin4 = _swa_tables()
    bd = _swa_blockdiag()

    big_w = (w_in, w_out, w_mlp_up, w_mlp_down)

    def own_shard(w, l):
        t = big_w[w][l]
        if w == 0:
            t = jnp.pad(t, ((0, 0), (0, SWP - SW)))
        return t.astype(BF16)

    def gather_start(l, sel, after, tag=""):
        own = [own_shard(w, l) for w in sel]
        lands = [lax.empty((4,) + t.shape, BF16) for t in own]
        return _split_start(own, lands, after, _ag_copies(False), "ag_start" + tag)

    def gather_finish(in_flight, after, tag=""):
        ssem, rsem, own, lands, _ = in_flight
        own, lands = _split_wait(ssem, rsem, own, lands, after, _ag_copies(True), "ag_wait" + tag)
        lands = _ag_forward(lands, "ag_forward" + tag)
        return [_place_own(o, g, sidx, "ag_place") for o, g in zip(own, lands)]

    zero_tile = jnp.zeros((8, 128), F32)
    first_in = gather_start(0, [0], zero_tile, "_in0")
    zc = jnp.zeros((DEPTH, 4, XBC), F32)
    zc = lax.dynamic_update_slice_in_dim(zc, ssd_conv_w, shard * (XBC // 4), axis=2)
    zd = jnp.zeros((DEPTH, CMK, CMC), F32)
    zd = lax.dynamic_update_slice_in_dim(zd, cm_dw_w, shard * (CMC // 4), axis=2)
    half = jnp.where(pc == 0, 1.0, 0.0).astype(F32)
    gw_packed = _allreduce_small(_pack_small({"ssd_conv_w": zc * half, "cm_dw_w": zd * half}, SHARDED_SMALL)
                                 + first_in[4][0:1, 0:1], "ag_small")
    first_rest = gather_start(0, [1, 2, 3], gw_packed, "_rest0")
    gw = _unpack_small(gw_packed, SHARDED_SMALL)
    conv_w_full, dw_w_full = gw["ssd_conv_w"], gw["cm_dw_w"]

    xcur = x[0]
    saved = []
    in_flight = None
    for l in range(DEPTH):
        if l == 0:
            (g_in,) = gather_finish(first_in, first_rest[4], "_in0")
            nxt_after = g_in
        else:
            g_in, g_out, g_up, g_dn = gather_finish(in_flight, xcur)
            nxt_after = g_dn
        nmw = norm_mix_w[l][None]
        if l + 1 < DEPTH:
            in_flight = gather_start(l + 1, [0, 1, 2, 3], nxt_after)
            nmw = nmw + in_flight[4][0:1, 0:1]
        grp = dict((n, (a, b)) for n, a, b in ORIG)
        w_perm = jnp.concatenate(
            _orig_cols(g_in, *grp["x"]) + _orig_cols(g_in, *grp["z"]) + _orig_cols(g_in, grp["a"][0], grp["g"][1])
            + _orig_cols(g_in, grp["q"][0], grp["v"][1]) + _orig_cols(g_in, *grp["dt"])
            + [jnp.zeros((D, 128 - NH), BF16)], axis=1)
        h = _rms_fwd(xcur, nmw, "rms_mix_fwd")
        u = _mm(h, w_perm, "nn", "in_proj", tn=640)
        alog = _pad128(ssd_a_log[l])
        dtb = _pad128(ssd_dt_bias[l])
        dxp = jnp.repeat(ssd_d[l], HP)[None, :]
        ssd_p = (conv_w_full[l], ssd_conv_b[l][None], dtb, alog, dxp, ssd_norm_w[l][None])
        ycat, ypre, st = _ssd_fwd(u, *ssd_p, consts, "ssd_fwd")
        swa_p = (cos4, sin4, *_swa_weights(q_norm_w[l]), *_swa_weights(k_norm_w[l]), *_swa_sinks(attn_sinks[l]), bd)
        ycat = _swa_fwd(u, ycat, *swa_p, "attn_fwd")
        cconv = _conf_conv_fwd(u, dw_w_full[l], cm_dw_b[l][None], "conf_conv_fwd")
        conf_p = (cconv, cm_ln_w[l][None], cm_ln_b[l][None])
        ycat = _conf_ln_fwd(*conf_p, ycat, "conf_ln_fwd")
        if l == 0:
            g_out, g_up, g_dn = gather_finish(first_rest, ycat, "_rest0")
        g_out = g_out.reshape(2 * D, D)
        g_dn = g_dn.reshape(DFF, D)
        x1 = _mm(ycat, g_out, "nn", "out_proj", add=xcur)
        hm = _rms_fwd(x1, norm_mlp_w[l][None], "rms_mlp_fwd")
        r_up = _mm_up(hm, g_up, "mlp_up")
        x2 = _mm(r_up, g_dn, "nn", "mlp_down", add=x1)
        saved.append(dict(x=xcur, h=h, u=u, ypre=ypre, st=st, swa_p=swa_p, conf_p=conf_p, ycat=ycat, x1=x1,
                          hm=hm, r_up=r_up, ssd_p=ssd_p, g_in=g_in, g_out=g_out, g_up=g_up, g_dn=g_dn))
        xcur = x2

    lsum, dx, dxb = _loss_bwd(xcur, loss_target[0], "loss")

    loc = locals()
    names = ["norm_mix_w", "w_in", "ssd_conv_w", "ssd_conv_b", "ssd_dt_bias", "ssd_a_log", "ssd_d", "ssd_norm_w",
             "q_norm_w", "k_norm_w", "attn_sinks", "cm_dw_w", "cm_dw_b", "cm_ln_w", "cm_ln_b", "w_out", "norm_mlp_w",
             "w_mlp_up", "w_mlp_down"]
    weights = {n: loc[n] for n in names}
    moms = {n: loc["m_" + n] for n in names}
    vars_ = {n: loc["v_" + n] for n in names}
    big_out = {n: None for n in BIG}
    win_grads = [None] * DEPTH

    def finish_layer(layer, which, shard_grads):
        for n, g in zip(which, shard_grads):
            if n == "w_in":
                win_grads[layer] = g
            else:
                big_out[n] = _adamw_layer(weights[n], g, moms[n], vars_[n], layer, big_out[n], "adamw_" + n)

    pending = None
    last_mlp = None
    gsm = {name: [] for name, _ in SMALL}
    for l in reversed(range(DEPTH)):
        sv = saved[l]
        da = _mm(dxb, sv["g_dn"], "nt", "mlp_down_dx", relu2_of=sv["r_up"])
        dwdn, dwdn_b = _mm_dw(sv["r_up"], dxb, "mlp_down_dw")
        dwup, dwup_b = _mm_dw(sv["hm"], da, "mlp_up_dw", col_shards=True)
        dhm = _mm_cs_nt(da, sv["g_up"], "mlp_up_dx")
        nlw = norm_mlp_w[l][None]
        if l == 0:
            last_mlp, token = _rs_begin([dwup, dwdn.reshape(4, D, D)], [dwup_b, dwdn_b.reshape(4, D, D)], zero_tile,
                                        "_mlp0")
            nlw = nlw + token[0:1, 0:1]
        dx1, dx1b, dnw = _rms_bwd(sv["x1"], nlw, dhm, dx, "rms_mlp_bwd")
        gsm["norm_mlp_w"].append(dnw[0])
        dy = _mm(dx1b, sv["g_out"], "nt", "out_proj_dx")
        dwout, dwout_b = _mm_dw(sv["ycat"], dx1b, "out_proj_dw")
        dcc, dwb, dlw, dlb = _conf_ln_bwd(*sv["conf_p"], dy, "conf_ln_bwd")
        da_c, dg_c, dww = _conf_conv_bwd(sv["u"], dcc, dw_w_full[l], "conf_conv_bwd")
        gsm["cm_dw_w"].append(dww[:CMK])
        gsm["cm_dw_b"].append(dwb[0])
        gsm["cm_ln_w"].append(dlw[0])
        gsm["cm_ln_b"].append(dlb[0])
        dq, dk, dv, dqw, dkw, dse, dso = _swa_bwd(sv["u"], dy, *sv["swa_p"], "attn_bwd")
        gsm["q_norm_w"].append(dqw[0, :HD] + dqw[0, HD:])
        gsm["k_norm_w"].append(dkw[0, :HD] + dkw[0, HD:])
        gsm["attn_sinks"].append(jnp.stack([dse[:, 0, 0], dso[:, 0, 0]], axis=1).reshape(NQH))
        (dz, dxr, ddtr, dcw, dcb, ddtb, dalog, ddd, dnsw) = _ssd_bwd(
            sv["u"], sv["ypre"], sv["st"], dy, *sv["ssd_p"], consts, "ssd_bwd")
        gsm["ssd_conv_w"].append(dcw[:4])
        gsm["ssd_conv_b"].append(dcb[0])
        gsm["ssd_dt_bias"].append(ddtb[0, :NH])
        gsm["ssd_a_log"].append(dalog[0, :NH])
        gsm["ssd_d"].append(ddd[0, :NH])
        gsm["ssd_norm_w"].append(dnsw[0])
        du = _shard_major(dict(z=dz, x=dxr, dt=ddtr[:, :NH], q=dq, k=dk, v=dv, a=da_c, g=dg_c))
        dwin, dwin_b = _mm_dw(sv["h"], du, "in_dw", col_shards=True, tn=SWP // 3)
        if l == 0:
            state, token = _rs_begin([dwin, dwout.reshape(4, D // 2, D)], [dwin_b, dwout_b.reshape(4, D // 2, D)],
                                     zero_tile, "_io0")
        else:
            state, token = _rs_begin(
                [dwin, dwout.reshape(4, D // 2, D), dwup, dwdn.reshape(4, D, D)],
                [dwin_b, dwout_b.reshape(4, D // 2, D), dwup_b, dwdn_b.reshape(4, D, D)], zero_tile)
        dh = _mm_cs_nt(du, sv["g_in"], "in_dx")
        dx, dxb, dnm = _rms_bwd(sv["x"], norm_mix_w[l][None] + token[0:1, 0:1], dh, dx1, "rms_mix_bwd")
        gsm["norm_mix_w"].append(dnm[0])
        if pending is not None:
            finish_layer(l + 1, BIG, _rs_end(pending, dx))
        pending = state

    finish_layer(0, BIG[2:], _rs_end(last_mlp, dx, "_mlp0"))
    gsm = {k: jnp.stack(v[::-1]) for k, v in gsm.items()}
    packed = _pack_small(gsm, [n for n, _ in SMALL])
    packed = jnp.concatenate([packed, lsum], axis=0)
    red = _allreduce_small(packed, "ar_small")
    finish_layer(0, BIG[:2], _rs_end(pending, red, "_io0"))
    loss = 0.5 * red[-8, 0] / D
    gsm = _unpack_small(red[:-8], [n for n, _ in SMALL])
    gsm["ssd_conv_w"] = lax.dynamic_slice_in_dim(gsm["ssd_conv_w"], shard * (XBC // 4), XBC // 4, axis=2)
    gsm["cm_dw_w"] = lax.dynamic_slice_in_dim(gsm["cm_dw_w"], shard * (CMC // 4), CMC // 4, axis=2)
    grads = dict(gsm)
    delta, new_m, new_v = {}, {}, {}
    for n in BIG[1:]:
        grads[n], delta[n], new_m[n], new_v[n] = big_out[n]
    to_lead = lambda t: jnp.transpose(t, (2, 0, 1))
    g_lead = jnp.stack([jnp.transpose(g[:, :SW]) for g in win_grads], axis=1)
    d_lead, m_lead, v_lead = _adamw_lead(to_lead(w_in), g_lead, to_lead(m_w_in), to_lead(v_w_in), "adamw_w_in")
    from_lead = lambda t: jnp.transpose(t, (1, 2, 0))
    grads["w_in"], delta["w_in"] = from_lead(g_lead), from_lead(d_lead)
    new_m["w_in"], new_v["w_in"] = from_lead(m_lead), from_lead(v_lead)

    packed_names = [n for n, _ in SMALL if n not in SHARDED_SMALL]
    pw = _pack_small(weights, packed_names)
    pg = _pack_small(grads, packed_names)
    pm = _pack_small(moms, packed_names)
    pv = _pack_small(vars_, packed_names)
    pd, pmn, pvn = _adamw(pw, pg, pm, pv, "adamw_small")
    for dst, buf in ((delta, pd), (new_m, pmn), (new_v, pvn)):
        dst.update(_unpack_small(buf, packed_names))
    for n in SHARDED_SMALL:
        shp = weights[n].shape
        flat = lambda t: t.reshape(-1, shp[-1])
        d_, m_, v_ = _adamw(flat(weights[n]), flat(grads[n]), flat(moms[n]), flat(vars_[n]), "adamw_" + n)
        delta[n], new_m[n], new_v[n] = d_.reshape(shp), m_.reshape(shp), v_.reshape(shp)

    return (loss, dx[None], *[grads[n] for n in names], *[delta[n] for n in names],
            *[new_m[n] for n in names], *[new_v[n] for n in names])
```
